```python
import math
import jax, jax.numpy as jnp
from jax import lax
import numpy as np

D_MODEL = 1024
BATCH = 8
SEQ = 16384
DEPTH = 2

N_A = DEPTH // 2
N_B = DEPTH - N_A
CONV_WIDTH = 31
D_FF = ((8 * D_MODEL + 3 * 256 - 1) // (3 * 256)) * 256
HEAD_DIM = 64
N_Q_HEADS = D_MODEL // HEAD_DIM
N_KV_HEADS = 2
GROUP = N_Q_HEADS // N_KV_HEADS
WINDOW = 128
BLOCK = 128
ALIBI_MAX = 8.0
ALPHA = (2.0 * DEPTH) ** 0.25
BETA = (8.0 * DEPTH) ** -0.25
LN_EPS = 1e-5
NEG_INF = -1e30

kernel_name = "yoco_conformer_swa_sink_alibi_deepnorm"


def layer_norm(x, g, b):
    xf = x.astype(jnp.float32)
    mu = jnp.mean(xf, axis=-1, keepdims=True)
    var = jnp.mean(jnp.square(xf - mu), axis=-1, keepdims=True)
    y = (xf - mu) * lax.rsqrt(var + LN_EPS)
    return (y * g.astype(jnp.float32) + b.astype(jnp.float32)).astype(x.dtype)


def conformer_conv(x, w_pw1, b_pw1, w_dw, b_dw, ln_g, ln_b, w_pw2, b_pw2):
    h = x @ w_pw1 + b_pw1
    h = h[..., :D_MODEL] * jax.nn.sigmoid(h[..., D_MODEL:])
    h = lax.conv_general_dilated(
        h, w_dw[:, None, :].astype(h.dtype), window_strides=(1,),
        padding=[(CONV_WIDTH - 1, 0)],
        dimension_numbers=("NWC", "WIO", "NWC"),
        feature_group_count=D_MODEL) + b_dw
    h = jax.nn.silu(layer_norm(h, ln_g, ln_b))
    return h @ w_pw2 + b_pw2


def swiglu(x, w_gate, w_up, w_down):
    return (jax.nn.silu(x @ w_gate) * (x @ w_up)) @ w_down


def banded_blocks(t):
    b, s = t.shape[0], t.shape[1]
    nb = s // BLOCK
    pad = jnp.zeros((b, BLOCK) + t.shape[2:], t.dtype)
    prev = jnp.concatenate([pad, t[:, :s - BLOCK]], axis=1).reshape(b, nb, BLOCK, *t.shape[2:])
    cur = t.reshape(b, nb, BLOCK, *t.shape[2:])
    return jnp.concatenate([prev, cur], axis=2)


def shared_kv(h, w_k, b_k, w_v, b_v):
    b, s, _ = h.shape
    k = (h @ w_k + b_k).reshape(b, s, N_KV_HEADS, HEAD_DIM)
    v = (h @ w_v + b_v).reshape(b, s, N_KV_HEADS, HEAD_DIM)
    return banded_blocks(k), banded_blocks(v)


def window_attention(x, k_blk, v_blk, w_q, b_q, sinks, w_o, b_o):
    b, s, _ = x.shape
    nb = s // BLOCK
    q = (x @ w_q + b_q).reshape(b, nb, BLOCK, N_KV_HEADS, GROUP, HEAD_DIM)
    scores = jnp.einsum("bnikgd,bnjkd->bnkgij", q, k_blk).astype(jnp.float32)
    scores = scores * (1.0 / math.sqrt(HEAD_DIM))
    qi = jnp.arange(BLOCK)[:, None]
    kj = jnp.arange(2 * BLOCK)[None, :]
    delta = qi + BLOCK - kj
    key_pos = jnp.arange(nb)[:, None, None] * BLOCK - BLOCK + kj[None]
    valid = (delta >= 0) & (delta < WINDOW) & (key_pos >= 0)
    slopes = jnp.exp2(-ALIBI_MAX * jnp.arange(1, N_Q_HEADS + 1, dtype=jnp.float32) / N_Q_HEADS)
    slopes = slopes.reshape(N_KV_HEADS, GROUP)
    scores = scores - slopes[None, None, :, :, None, None] * delta.astype(jnp.float32)[None, None, None, None]
    scores = jnp.where(valid[None, :, None, None], scores, NEG_INF)
    sink = jnp.broadcast_to(
        sinks.astype(jnp.float32).reshape(N_KV_HEADS, GROUP)[None, None, :, :, None, None],
        scores.shape[:-1] + (1,))
    probs = jax.nn.softmax(jnp.concatenate([scores, sink], axis=-1), axis=-1)[..., :-1]
    o = jnp.einsum("bnkgij,bnjkd->bnikgd", probs.astype(v_blk.dtype), v_blk)
    o = o.reshape(b, s, N_Q_HEADS * HEAD_DIM)
    return o @ w_o + b_o


def _fwd_setup_inputs(seed: int = 0) -> dict:
    key = jax.random.key(seed)
    ks = jax.random.split(key, 40)
    f32 = jnp.float32
    D, F, HD = D_MODEL, D_FF, N_Q_HEADS * HEAD_DIM
    KVD = N_KV_HEADS * HEAD_DIM

    def nrm(k, shape, scale):
        return jax.random.normal(k, shape, f32) * scale

    def gain(k, shape):
        return 1.0 + 0.05 * jax.random.normal(k, shape, f32)

    return {
        "x": nrm(ks[0], (BATCH, SEQ, D), 1.0),
        "conv_w_pw1": nrm(ks[1], (N_A, D, 2 * D), D ** -0.5),
        "conv_b_pw1": nrm(ks[2], (N_A, 2 * D), 0.02),
        "conv_w_dw": nrm(ks[3], (N_A, CONV_WIDTH, D), CONV_WIDTH ** -0.5),
        "conv_b_dw": nrm(ks[4], (N_A, D), 0.02),
        "conv_ln_g": gain(ks[5], (N_A, D)),
        "conv_ln_b": nrm(ks[6], (N_A, D), 0.02),
        "conv_w_pw2": nrm(ks[7], (N_A, D, D), BETA * D ** -0.5),
        "conv_b_pw2": nrm(ks[8], (N_A, D), 0.02),
        "kv_w_k": nrm(ks[9], (D, KVD), D ** -0.5),
        "kv_b_k": nrm(ks[10], (KVD,), 0.02),
        "kv_w_v": nrm(ks[11], (D, KVD), BETA * D ** -0.5),
        "kv_b_v": nrm(ks[12], (KVD,), 0.02),
        "attn_w_q": nrm(ks[13], (N_B, D, HD), D ** -0.5),
        "attn_b_q": nrm(ks[14], (N_B, HD), 0.02),
        "attn_sinks": nrm(ks[15], (N_B, N_Q_HEADS), 1.0),
        "attn_w_o": nrm(ks[16], (N_B, HD, D), BETA * HD ** -0.5),
        "attn_b_o": nrm(ks[17], (N_B, D), 0.02),
        "ffn_w_gate": nrm(ks[18], (DEPTH, D, F), D ** -0.5),
        "ffn_w_up": nrm(ks[19], (DEPTH, D, F), D ** -0.5),
        "ffn_w_down": nrm(ks[20], (DEPTH, F, D), BETA * F ** -0.5),
        "ln_mix_g": gain(ks[21], (DEPTH, D)),
        "ln_mix_b": nrm(ks[22], (DEPTH, D), 0.02),
        "ln_ffn_g": gain(ks[23], (DEPTH, D)),
        "ln_ffn_b": nrm(ks[24], (DEPTH, D), 0.02),
    }


def _fwd_reference(x, conv_w_pw1, conv_b_pw1, conv_w_dw, conv_b_dw, conv_ln_g, conv_ln_b,
              conv_w_pw2, conv_b_pw2, kv_w_k, kv_b_k, kv_w_v, kv_b_v,
              attn_w_q, attn_b_q, attn_sinks, attn_w_o, attn_b_o,
              ffn_w_gate, ffn_w_up, ffn_w_down,
              ln_mix_g, ln_mix_b, ln_ffn_g, ln_ffn_b):
    k_blk, v_blk = None, None
    for layer in range(DEPTH):
        if layer < N_A:
            a = layer
            m = conformer_conv(x, conv_w_pw1[a], conv_b_pw1[a], conv_w_dw[a], conv_b_dw[a],
                               conv_ln_g[a], conv_ln_b[a], conv_w_pw2[a], conv_b_pw2[a])
        else:
            l = layer - N_A
            m = window_attention(x, k_blk, v_blk, attn_w_q[l], attn_b_q[l], attn_sinks[l],
                                 attn_w_o[l], attn_b_o[l])
        x = layer_norm(ALPHA * x + m, ln_mix_g[layer], ln_mix_b[layer])
        f = swiglu(x, ffn_w_gate[layer], ffn_w_up[layer], ffn_w_down[layer])
        x = layer_norm(ALPHA * x + f, ln_ffn_g[layer], ln_ffn_b[layer])
        if layer == N_A - 1:
            k_blk, v_blk = shared_kv(x, kv_w_k, kv_b_k, kv_w_v, kv_b_v)
    return x


import jax as _jax
import jax.numpy as _jnp

TWIN_FORMAT = 'train_step'
FWD_PARAMS = ['x', 'conv_w_pw1', 'conv_b_pw1', 'conv_w_dw', 'conv_b_dw', 'conv_ln_g', 'conv_ln_b', 'conv_w_pw2', 'conv_b_pw2', 'kv_w_k', 'kv_b_k', 'kv_w_v', 'kv_b_v', 'attn_w_q', 'attn_b_q', 'attn_sinks', 'attn_w_o', 'attn_b_o', 'ffn_w_gate', 'ffn_w_up', 'ffn_w_down', 'ln_mix_g', 'ln_mix_b', 'ln_ffn_g', 'ln_ffn_b']
TWIN_WEIGHTS = ['conv_w_pw1', 'conv_b_pw1', 'conv_w_dw', 'conv_b_dw', 'conv_ln_g', 'conv_ln_b', 'conv_w_pw2', 'conv_b_pw2', 'kv_w_k', 'kv_b_k', 'kv_w_v', 'kv_b_v', 'attn_w_q', 'attn_b_q', 'attn_sinks', 'attn_w_o', 'attn_b_o', 'ffn_w_gate', 'ffn_w_up', 'ffn_w_down', 'ln_mix_g', 'ln_mix_b', 'ln_ffn_g', 'ln_ffn_b']
TWIN_DIFF_INPUT = 'x'
TWIN_INPUTS = ['x', 'conv_w_pw1', 'conv_b_pw1', 'conv_w_dw', 'conv_b_dw', 'conv_ln_g', 'conv_ln_b', 'conv_w_pw2', 'conv_b_pw2', 'kv_w_k', 'kv_b_k', 'kv_w_v', 'kv_b_v', 'attn_w_q', 'attn_b_q', 'attn_sinks', 'attn_w_o', 'attn_b_o', 'ffn_w_gate', 'ffn_w_up', 'ffn_w_down', 'ln_mix_g', 'ln_mix_b', 'ln_ffn_g', 'ln_ffn_b', 'loss_target', 'm_conv_w_pw1', 'm_conv_b_pw1', 'm_conv_w_dw', 'm_conv_b_dw', 'm_conv_ln_g', 'm_conv_ln_b', 'm_conv_w_pw2', 'm_conv_b_pw2', 'm_kv_w_k', 'm_kv_b_k', 'm_kv_w_v', 'm_kv_b_v', 'm_attn_w_q', 'm_attn_b_q', 'm_attn_sinks', 'm_attn_w_o', 'm_attn_b_o', 'm_ffn_w_gate', 'm_ffn_w_up', 'm_ffn_w_down', 'm_ln_mix_g', 'm_ln_mix_b', 'm_ln_ffn_g', 'm_ln_ffn_b', 'v_conv_w_pw1', 'v_conv_b_pw1', 'v_conv_w_dw', 'v_conv_b_dw', 'v_conv_ln_g', 'v_conv_ln_b', 'v_conv_w_pw2', 'v_conv_b_pw2', 'v_kv_w_k', 'v_kv_b_k', 'v_kv_w_v', 'v_kv_b_v', 'v_attn_w_q', 'v_attn_b_q', 'v_attn_sinks', 'v_attn_w_o', 'v_attn_b_o', 'v_ffn_w_gate', 'v_ffn_w_up', 'v_ffn_w_down', 'v_ln_mix_g', 'v_ln_mix_b', 'v_ln_ffn_g', 'v_ln_ffn_b']
TWIN_OUTPUTS = ['loss', 'grad_x', 'grad_conv_w_pw1', 'grad_conv_b_pw1', 'grad_conv_w_dw', 'grad_conv_b_dw', 'grad_conv_ln_g', 'grad_conv_ln_b', 'grad_conv_w_pw2', 'grad_conv_b_pw2', 'grad_kv_w_k', 'grad_kv_b_k', 'grad_kv_w_v', 'grad_kv_b_v', 'grad_attn_w_q', 'grad_attn_b_q', 'grad_attn_sinks', 'grad_attn_w_o', 'grad_attn_b_o', 'grad_ffn_w_gate', 'grad_ffn_w_up', 'grad_ffn_w_down', 'grad_ln_mix_g', 'grad_ln_mix_b', 'grad_ln_ffn_g', 'grad_ln_ffn_b', 'delta_conv_w_pw1', 'delta_conv_b_pw1', 'delta_conv_w_dw', 'delta_conv_b_dw', 'delta_conv_ln_g', 'delta_conv_ln_b', 'delta_conv_w_pw2', 'delta_conv_b_pw2', 'delta_kv_w_k', 'delta_kv_b_k', 'delta_kv_w_v', 'delta_kv_b_v', 'delta_attn_w_q', 'delta_attn_b_q', 'delta_attn_sinks', 'delta_attn_w_o', 'delta_attn_b_o', 'delta_ffn_w_gate', 'delta_ffn_w_up', 'delta_ffn_w_down', 'delta_ln_mix_g', 'delta_ln_mix_b', 'delta_ln_ffn_g', 'delta_ln_ffn_b', 'new_m_conv_w_pw1', 'new_m_conv_b_pw1', 'new_m_conv_w_dw', 'new_m_conv_b_dw', 'new_m_conv_ln_g', 'new_m_conv_ln_b', 'new_m_conv_w_pw2', 'new_m_conv_b_pw2', 'new_m_kv_w_k', 'new_m_kv_b_k', 'new_m_kv_w_v', 'new_m_kv_b_v', 'new_m_attn_w_q', 'new_m_attn_b_q', 'new_m_attn_sinks', 'new_m_attn_w_o', 'new_m_attn_b_o', 'new_m_ffn_w_gate', 'new_m_ffn_w_up', 'new_m_ffn_w_down', 'new_m_ln_mix_g', 'new_m_ln_mix_b', 'new_m_ln_ffn_g', 'new_m_ln_ffn_b', 'new_v_conv_w_pw1', 'new_v_conv_b_pw1', 'new_v_conv_w_dw', 'new_v_conv_b_dw', 'new_v_conv_ln_g', 'new_v_conv_ln_b', 'new_v_conv_w_pw2', 'new_v_conv_b_pw2', 'new_v_kv_w_k', 'new_v_kv_b_k', 'new_v_kv_w_v', 'new_v_kv_b_v', 'new_v_attn_w_q', 'new_v_attn_b_q', 'new_v_attn_sinks', 'new_v_attn_w_o', 'new_v_attn_b_o', 'new_v_ffn_w_gate', 'new_v_ffn_w_up', 'new_v_ffn_w_down', 'new_v_ln_mix_g', 'new_v_ln_mix_b', 'new_v_ln_ffn_g', 'new_v_ln_ffn_b']
TWIN_LEAF_KINDS = {'loss': 'loss', 'grad_x': 'grad_x', 'grad_conv_w_pw1': 'grad_w', 'grad_conv_b_pw1': 'grad_w', 'grad_conv_w_dw': 'grad_w', 'grad_conv_b_dw': 'grad_w', 'grad_conv_ln_g': 'grad_w', 'grad_conv_ln_b': 'grad_w', 'grad_conv_w_pw2': 'grad_w', 'grad_conv_b_pw2': 'grad_w', 'grad_kv_w_k': 'grad_w', 'grad_kv_b_k': 'grad_w', 'grad_kv_w_v': 'grad_w', 'grad_kv_b_v': 'grad_w', 'grad_attn_w_q': 'grad_w', 'grad_attn_b_q': 'grad_w', 'grad_attn_sinks': 'grad_w', 'grad_attn_w_o': 'grad_w', 'grad_attn_b_o': 'grad_w', 'grad_ffn_w_gate': 'grad_w', 'grad_ffn_w_up': 'grad_w', 'grad_ffn_w_down': 'grad_w', 'grad_ln_mix_g': 'grad_w', 'grad_ln_mix_b': 'grad_w', 'grad_ln_ffn_g': 'grad_w', 'grad_ln_ffn_b': 'grad_w', 'delta_conv_w_pw1': 'delta_w', 'delta_conv_b_pw1': 'delta_w', 'delta_conv_w_dw': 'delta_w', 'delta_conv_b_dw': 'delta_w', 'delta_conv_ln_g': 'delta_w', 'delta_conv_ln_b': 'delta_w', 'delta_conv_w_pw2': 'delta_w', 'delta_conv_b_pw2': 'delta_w', 'delta_kv_w_k': 'delta_w', 'delta_kv_b_k': 'delta_w', 'delta_kv_w_v': 'delta_w', 'delta_kv_b_v': 'delta_w', 'delta_attn_w_q': 'delta_w', 'delta_attn_b_q': 'delta_w', 'delta_attn_sinks': 'delta_w', 'delta_attn_w_o': 'delta_w', 'delta_attn_b_o': 'delta_w', 'delta_ffn_w_gate': 'delta_w', 'delta_ffn_w_up': 'delta_w', 'delta_ffn_w_down': 'delta_w', 'delta_ln_mix_g': 'delta_w', 'delta_ln_mix_b': 'delta_w', 'delta_ln_ffn_g': 'delta_w', 'delta_ln_ffn_b': 'delta_w', 'new_m_conv_w_pw1': 'new_m', 'new_m_conv_b_pw1': 'new_m', 'new_m_conv_w_dw': 'new_m', 'new_m_conv_b_dw': 'new_m', 'new_m_conv_ln_g': 'new_m', 'new_m_conv_ln_b': 'new_m', 'new_m_conv_w_pw2': 'new_m', 'new_m_conv_b_pw2': 'new_m', 'new_m_kv_w_k': 'new_m', 'new_m_kv_b_k': 'new_m', 'new_m_kv_w_v': 'new_m', 'new_m_kv_b_v': 'new_m', 'new_m_attn_w_q': 'new_m', 'new_m_attn_b_q': 'new_m', 'new_m_attn_sinks': 'new_m', 'new_m_attn_w_o': 'new_m', 'new_m_attn_b_o': 'new_m', 'new_m_ffn_w_gate': 'new_m', 'new_m_ffn_w_up': 'new_m', 'new_m_ffn_w_down': 'new_m', 'new_m_ln_mix_g': 'new_m', 'new_m_ln_mix_b': 'new_m', 'new_m_ln_ffn_g': 'new_m', 'new_m_ln_ffn_b': 'new_m', 'new_v_conv_w_pw1': 'new_v', 'new_v_conv_b_pw1': 'new_v', 'new_v_conv_w_dw': 'new_v', 'new_v_conv_b_dw': 'new_v', 'new_v_conv_ln_g': 'new_v', 'new_v_conv_ln_b': 'new_v', 'new_v_conv_w_pw2': 'new_v', 'new_v_conv_b_pw2': 'new_v', 'new_v_kv_w_k': 'new_v', 'new_v_kv_b_k': 'new_v', 'new_v_kv_w_v': 'new_v', 'new_v_kv_b_v': 'new_v', 'new_v_attn_w_q': 'new_v', 'new_v_attn_b_q': 'new_v', 'new_v_attn_sinks': 'new_v', 'new_v_attn_w_o': 'new_v', 'new_v_attn_b_o': 'new_v', 'new_v_ffn_w_gate': 'new_v', 'new_v_ffn_w_up': 'new_v', 'new_v_ffn_w_down': 'new_v', 'new_v_ln_mix_g': 'new_v', 'new_v_ln_mix_b': 'new_v', 'new_v_ln_ffn_g': 'new_v', 'new_v_ln_ffn_b': 'new_v'}


def _forward(args):
    return _fwd_reference(*[args[k] for k in FWD_PARAMS])


def _output_shape():
    def fwd():
        inp = _fwd_setup_inputs(0)
        return _fwd_reference(*[inp[k] for k in FWD_PARAMS])
    out = _jax.eval_shape(fwd)
    return out.shape, out.dtype

N_MICROBATCH = 1
ADAM_LR = 0.001
ADAM_B1 = 0.9
ADAM_B2 = 0.999
ADAM_EPS = 1e-08
ADAM_WD = 0.01
ADAM_STEP = 10
PER_EXAMPLE_BATCH_AXIS = {'x': 0, 'loss_target': 0}
SHARED_INPUTS = []
_WEIGHT_DTYPES = {'conv_w_pw1': _jnp.float32, 'conv_b_pw1': _jnp.float32, 'conv_w_dw': _jnp.float32, 'conv_b_dw': _jnp.float32, 'conv_ln_g': _jnp.float32, 'conv_ln_b': _jnp.float32, 'conv_w_pw2': _jnp.float32, 'conv_b_pw2': _jnp.float32, 'kv_w_k': _jnp.float32, 'kv_b_k': _jnp.float32, 'kv_w_v': _jnp.float32, 'kv_b_v': _jnp.float32, 'attn_w_q': _jnp.float32, 'attn_b_q': _jnp.float32, 'attn_sinks': _jnp.float32, 'attn_w_o': _jnp.float32, 'attn_b_o': _jnp.float32, 'ffn_w_gate': _jnp.float32, 'ffn_w_up': _jnp.float32, 'ffn_w_down': _jnp.float32, 'ln_mix_g': _jnp.float32, 'ln_mix_b': _jnp.float32, 'ln_ffn_g': _jnp.float32, 'ln_ffn_b': _jnp.float32}
MOMENT_SCALE = {'conv_w_pw1': 6.077350e-02, 'conv_b_pw1': 2.081988e-01, 'conv_w_dw': 8.468473e-02, 'conv_b_dw': 4.576157e-01, 'conv_ln_g': 1.901602e-01, 'conv_ln_b': 3.030074e-01, 'conv_w_pw2': 2.364325e-01, 'conv_b_pw2': 1.368504e+00, 'kv_w_k': 4.277328e-02, 'kv_b_k': 1.031418e-02, 'kv_w_v': 1.250802e-01, 'kv_b_v': 1.847675e+00, 'attn_w_q': 1.509182e-02, 'attn_b_q': 1.529872e-02, 'attn_sinks': 6.679729e-02, 'attn_w_o': 4.532810e-02, 'attn_b_o': 1.362218e+00, 'ffn_w_gate': 4.643693e-02, 'ffn_w_up': 4.565004e-02, 'ffn_w_down': 1.512989e-01, 'ln_mix_g': 1.141258e+01, 'ln_mix_b': 1.937497e+00, 'ln_ffn_g': 9.203324e+01, 'ln_ffn_b': 7.388582e+00}


def _to_microbatches(a, axis):
    t = _jnp.moveaxis(a, axis, 0)
    t = t.reshape((N_MICROBATCH, t.shape[0] // N_MICROBATCH) + t.shape[1:])
    return _jnp.moveaxis(t, 1, axis + 1)


def setup_inputs(seed: int = 0) -> dict:
    inp = _fwd_setup_inputs(seed)
    key = _jax.random.fold_in(_jax.random.key(seed), 7919)
    shape, _ = _output_shape()
    out = dict(inp)
    out["loss_target"] = _jax.random.normal(_jax.random.fold_in(key, 0), shape, _jnp.float32)
    for i, name in enumerate(TWIN_WEIGHTS):
        w = inp[name].astype(_jnp.float32)
        if MOMENT_SCALE is None:
            s = _jnp.sqrt(_jnp.mean(_jnp.square(w)) + 1e-30)
        else:
            s = MOMENT_SCALE[name]
        km, kv = _jax.random.split(_jax.random.fold_in(key, i + 1))
        out[name] = w
        out["m_" + name] = s * _jax.random.normal(km, w.shape, _jnp.float32)
        out["v_" + name] = (s * s) * _jax.random.uniform(kv, w.shape, _jnp.float32, 0.5, 1.5)
    if N_MICROBATCH > 1:
        for name, axis in PER_EXAMPLE_BATCH_AXIS.items():
            out[name] = _to_microbatches(out[name], axis)
    return {'x': out['x'], 'conv_w_pw1': out['conv_w_pw1'], 'conv_b_pw1': out['conv_b_pw1'], 'conv_w_dw': out['conv_w_dw'], 'conv_b_dw': out['conv_b_dw'], 'conv_ln_g': out['conv_ln_g'], 'conv_ln_b': out['conv_ln_b'], 'conv_w_pw2': out['conv_w_pw2'], 'conv_b_pw2': out['conv_b_pw2'], 'kv_w_k': out['kv_w_k'], 'kv_b_k': out['kv_b_k'], 'kv_w_v': out['kv_w_v'], 'kv_b_v': out['kv_b_v'], 'attn_w_q': out['attn_w_q'], 'attn_b_q': out['attn_b_q'], 'attn_sinks': out['attn_sinks'], 'attn_w_o': out['attn_w_o'], 'attn_b_o': out['attn_b_o'], 'ffn_w_gate': out['ffn_w_gate'], 'ffn_w_up': out['ffn_w_up'], 'ffn_w_down': out['ffn_w_down'], 'ln_mix_g': out['ln_mix_g'], 'ln_mix_b': out['ln_mix_b'], 'ln_ffn_g': out['ln_ffn_g'], 'ln_ffn_b': out['ln_ffn_b'], 'loss_target': out['loss_target'], 'm_conv_w_pw1': out['m_conv_w_pw1'], 'm_conv_b_pw1': out['m_conv_b_pw1'], 'm_conv_w_dw': out['m_conv_w_dw'], 'm_conv_b_dw': out['m_conv_b_dw'], 'm_conv_ln_g': out['m_conv_ln_g'], 'm_conv_ln_b': out['m_conv_ln_b'], 'm_conv_w_pw2': out['m_conv_w_pw2'], 'm_conv_b_pw2': out['m_conv_b_pw2'], 'm_kv_w_k': out['m_kv_w_k'], 'm_kv_b_k': out['m_kv_b_k'], 'm_kv_w_v': out['m_kv_w_v'], 'm_kv_b_v': out['m_kv_b_v'], 'm_attn_w_q': out['m_attn_w_q'], 'm_attn_b_q': out['m_attn_b_q'], 'm_attn_sinks': out['m_attn_sinks'], 'm_attn_w_o': out['m_attn_w_o'], 'm_attn_b_o': out['m_attn_b_o'], 'm_ffn_w_gate': out['m_ffn_w_gate'], 'm_ffn_w_up': out['m_ffn_w_up'], 'm_ffn_w_down': out['m_ffn_w_down'], 'm_ln_mix_g': out['m_ln_mix_g'], 'm_ln_mix_b': out['m_ln_mix_b'], 'm_ln_ffn_g': out['m_ln_ffn_g'], 'm_ln_ffn_b': out['m_ln_ffn_b'], 'v_conv_w_pw1': out['v_conv_w_pw1'], 'v_conv_b_pw1': out['v_conv_b_pw1'], 'v_conv_w_dw': out['v_conv_w_dw'], 'v_conv_b_dw': out['v_conv_b_dw'], 'v_conv_ln_g': out['v_conv_ln_g'], 'v_conv_ln_b': out['v_conv_ln_b'], 'v_conv_w_pw2': out['v_conv_w_pw2'], 'v_conv_b_pw2': out['v_conv_b_pw2'], 'v_kv_w_k': out['v_kv_w_k'], 'v_kv_b_k': out['v_kv_b_k'], 'v_kv_w_v': out['v_kv_w_v'], 'v_kv_b_v': out['v_kv_b_v'], 'v_attn_w_q': out['v_attn_w_q'], 'v_attn_b_q': out['v_attn_b_q'], 'v_attn_sinks': out['v_attn_sinks'], 'v_attn_w_o': out['v_attn_w_o'], 'v_attn_b_o': out['v_attn_b_o'], 'v_ffn_w_gate': out['v_ffn_w_gate'], 'v_ffn_w_up': out['v_ffn_w_up'], 'v_ffn_w_down': out['v_ffn_w_down'], 'v_ln_mix_g': out['v_ln_mix_g'], 'v_ln_mix_b': out['v_ln_mix_b'], 'v_ln_ffn_g': out['v_ln_ffn_g'], 'v_ln_ffn_b': out['v_ln_ffn_b']}


def _loss(weights, diff, rest, loss_target):
    with _jax.named_scope("forward"):
        args = {**rest, TWIN_DIFF_INPUT: diff, **{k: w.astype(_WEIGHT_DTYPES[k]) for k, w in weights.items()}}
        y = _forward(args)
    with _jax.named_scope("loss_head"):
        err = _jnp.square(y.astype(_jnp.float32) - loss_target)
        return 0.5 * _jnp.sum(_jnp.mean(err, axis=-1)) if err.ndim else 0.5 * err


def _adamw(w, g, m, v):
    m = ADAM_B1 * m + (1.0 - ADAM_B1) * g
    v = ADAM_B2 * v + (1.0 - ADAM_B2) * _jnp.square(g)
    m_hat = m / (1.0 - ADAM_B1 ** ADAM_STEP)
    v_hat = v / (1.0 - ADAM_B2 ** ADAM_STEP)
    delta = -ADAM_LR * (m_hat / (_jnp.sqrt(v_hat) + ADAM_EPS) + ADAM_WD * w)
    return delta, m, v


def reference(x, conv_w_pw1, conv_b_pw1, conv_w_dw, conv_b_dw, conv_ln_g, conv_ln_b, conv_w_pw2, conv_b_pw2, kv_w_k, kv_b_k, kv_w_v, kv_b_v, attn_w_q, attn_b_q, attn_sinks, attn_w_o, attn_b_o, ffn_w_gate, ffn_w_up, ffn_w_down, ln_mix_g, ln_mix_b, ln_ffn_g, ln_ffn_b, loss_target, m_conv_w_pw1, m_conv_b_pw1, m_conv_w_dw, m_conv_b_dw, m_conv_ln_g, m_conv_ln_b, m_conv_w_pw2, m_conv_b_pw2, m_kv_w_k, m_kv_b_k, m_kv_w_v, m_kv_b_v, m_attn_w_q, m_attn_b_q, m_attn_sinks, m_attn_w_o, m_attn_b_o, m_ffn_w_gate, m_ffn_w_up, m_ffn_w_down, m_ln_mix_g, m_ln_mix_b, m_ln_ffn_g, m_ln_ffn_b, v_conv_w_pw1, v_conv_b_pw1, v_conv_w_dw, v_conv_b_dw, v_conv_ln_g, v_conv_ln_b, v_conv_w_pw2, v_conv_b_pw2, v_kv_w_k, v_kv_b_k, v_kv_w_v, v_kv_b_v, v_attn_w_q, v_attn_b_q, v_attn_sinks, v_attn_w_o, v_attn_b_o, v_ffn_w_gate, v_ffn_w_up, v_ffn_w_down, v_ln_mix_g, v_ln_mix_b, v_ln_ffn_g, v_ln_ffn_b):
    given = dict(x=x, conv_w_pw1=conv_w_pw1, conv_b_pw1=conv_b_pw1, conv_w_dw=conv_w_dw, conv_b_dw=conv_b_dw, conv_ln_g=conv_ln_g, conv_ln_b=conv_ln_b, conv_w_pw2=conv_w_pw2, conv_b_pw2=conv_b_pw2, kv_w_k=kv_w_k, kv_b_k=kv_b_k, kv_w_v=kv_w_v, kv_b_v=kv_b_v, attn_w_q=attn_w_q, attn_b_q=attn_b_q, attn_sinks=attn_sinks, attn_w_o=attn_w_o, attn_b_o=attn_b_o, ffn_w_gate=ffn_w_gate, ffn_w_up=ffn_w_up, ffn_w_down=ffn_w_down, ln_mix_g=ln_mix_g, ln_mix_b=ln_mix_b, ln_ffn_g=ln_ffn_g, ln_ffn_b=ln_ffn_b, loss_target=loss_target, m_conv_w_pw1=m_conv_w_pw1, m_conv_b_pw1=m_conv_b_pw1, m_conv_w_dw=m_conv_w_dw, m_conv_b_dw=m_conv_b_dw, m_conv_ln_g=m_conv_ln_g, m_conv_ln_b=m_conv_ln_b, m_conv_w_pw2=m_conv_w_pw2, m_conv_b_pw2=m_conv_b_pw2, m_kv_w_k=m_kv_w_k, m_kv_b_k=m_kv_b_k, m_kv_w_v=m_kv_w_v, m_kv_b_v=m_kv_b_v, m_attn_w_q=m_attn_w_q, m_attn_b_q=m_attn_b_q, m_attn_sinks=m_attn_sinks, m_attn_w_o=m_attn_w_o, m_attn_b_o=m_attn_b_o, m_ffn_w_gate=m_ffn_w_gate, m_ffn_w_up=m_ffn_w_up, m_ffn_w_down=m_ffn_w_down, m_ln_mix_g=m_ln_mix_g, m_ln_mix_b=m_ln_mix_b, m_ln_ffn_g=m_ln_ffn_g, m_ln_ffn_b=m_ln_ffn_b, v_conv_w_pw1=v_conv_w_pw1, v_conv_b_pw1=v_conv_b_pw1, v_conv_w_dw=v_conv_w_dw, v_conv_b_dw=v_conv_b_dw, v_conv_ln_g=v_conv_ln_g, v_conv_ln_b=v_conv_ln_b, v_conv_w_pw2=v_conv_w_pw2, v_conv_b_pw2=v_conv_b_pw2, v_kv_w_k=v_kv_w_k, v_kv_b_k=v_kv_b_k, v_kv_w_v=v_kv_w_v, v_kv_b_v=v_kv_b_v, v_attn_w_q=v_attn_w_q, v_attn_b_q=v_attn_b_q, v_attn_sinks=v_attn_sinks, v_attn_w_o=v_attn_w_o, v_attn_b_o=v_attn_b_o, v_ffn_w_gate=v_ffn_w_gate, v_ffn_w_up=v_ffn_w_up, v_ffn_w_down=v_ffn_w_down, v_ln_mix_g=v_ln_mix_g, v_ln_mix_b=v_ln_mix_b, v_ln_ffn_g=v_ln_ffn_g, v_ln_ffn_b=v_ln_ffn_b)
    weights = {n: given[n] for n in TWIN_WEIGHTS}
    shared = {n: given[n] for n in SHARED_INPUTS}
    per_example = {n: given[n] for n in ['x']}
    grad_fn = _jax.value_and_grad(_loss, argnums=(0, 1))

    def one_microbatch(ex, loss_target):
        ex = dict(ex)
        diff = ex.pop(TWIN_DIFF_INPUT)
        return grad_fn(weights, diff, {**shared, **ex}, loss_target)

    if N_MICROBATCH == 1:
        loss, (grad_w, grad_x) = one_microbatch(per_example, given["loss_target"])
    else:
        def body(carry, xs):
            loss_sum, grad_sum = carry
            l_k, (gw_k, gx_k) = one_microbatch(xs[0], xs[1])
            with _jax.named_scope("update"):
                return (loss_sum + l_k, _jax.tree.map(_jnp.add, grad_sum, gw_k)), gx_k

        init = (_jnp.zeros((), _jnp.float32), _jax.tree.map(_jnp.zeros_like, weights))
        (loss, grad_w), grad_x = _jax.lax.scan(body, init, (per_example, given["loss_target"]))
    with _jax.named_scope("update"):
        delta_w, new_m, new_v = {}, {}, {}
        for n in TWIN_WEIGHTS:
            delta_w[n], new_m[n], new_v[n] = _adamw(weights[n], grad_w[n], given["m_" + n], given["v_" + n])
    return (loss, grad_x, *[grad_w[n] for n in TWIN_WEIGHTS], *[delta_w[n] for n in TWIN_WEIGHTS],
            *[new_m[n] for n in TWIN_WEIGHTS], *[new_v[n] for n in TWIN_WEIGHTS])
```

```python
import functools
import math

import jax
import jax.numpy as jnp
from jax import lax
from jax.experimental import pallas as pl
from jax.experimental.pallas import tpu as pltpu

F32 = jnp.float32
BF16 = jnp.bfloat16

DEPTH = 2
ALPHA = (2.0 * DEPTH) ** 0.25
LN_EPS = 1e-5
NEG_INF = -1e30
HEAD_DIM = 64
N_KV_HEADS = 2
KVD = N_KV_HEADS * HEAD_DIM
BLOCK = 128
CONV_WIDTH = 31
CONV_HALO = 32
ALIBI_MAX = 8.0
ADAM_LR, ADAM_B1, ADAM_B2, ADAM_EPS, ADAM_WD, ADAM_STEP = 0.001, 0.9, 0.999, 1e-08, 0.01, 10

N_CHIPS = 4
PACK_LANES = 1024
PACK_ROWS = 256
VMEM_LIMIT = 56 * 1024 * 1024
MESH = pl.DeviceIdType.MESH

NT_DIMS = (((1,), (1,)), ((), ()))
TN_DIMS = (((0,), (0,)), ((), ()))

WEIGHTS = ['conv_w_pw1', 'conv_b_pw1', 'conv_w_dw', 'conv_b_dw', 'conv_ln_g', 'conv_ln_b', 'conv_w_pw2', 'conv_b_pw2',
           'kv_w_k', 'kv_b_k', 'kv_w_v', 'kv_b_v', 'attn_w_q', 'attn_b_q', 'attn_sinks', 'attn_w_o', 'attn_b_o',
           'ffn_w_gate', 'ffn_w_up', 'ffn_w_down', 'ln_mix_g', 'ln_mix_b', 'ln_ffn_g', 'ln_ffn_b']
BIG = ['conv_w_pw1', 'conv_w_pw2', 'kv_w_k', 'kv_w_v', 'attn_w_q', 'attn_w_o', 'ffn_w_gate', 'ffn_w_up', 'ffn_w_down']
SMALL_SHARDED = ['conv_b_pw1', 'conv_w_dw', 'conv_b_dw', 'conv_ln_g', 'conv_ln_b', 'conv_b_pw2']
REPLICATED = ['kv_b_k', 'kv_b_v', 'attn_b_q', 'attn_sinks', 'attn_b_o', 'ln_mix_g', 'ln_mix_b', 'ln_ffn_g', 'ln_ffn_b']
SMALL = SMALL_SHARDED + REPLICATED


def _cparams(n_grid=1):
    return pltpu.CompilerParams(dimension_semantics=("arbitrary",) * n_grid, vmem_limit_bytes=VMEM_LIMIT)


def _rows(tm, width):
    return pl.BlockSpec((tm, width), lambda i: (i, 0))


def _const(shape):
    return pl.BlockSpec(shape, lambda *_: (0,) * len(shape), pipeline_mode=pl.Buffered(1))


def _acc_out(shape):
    return pl.BlockSpec(shape, lambda *_: (0,) * len(shape))


def _dot(a, b):
    return jnp.dot(a, b, preferred_element_type=F32)


def _dot_nt(a, b):
    return lax.dot_general(a, b, NT_DIMS, preferred_element_type=F32)


def _dot_tn(a, b):
    return lax.dot_general(a, b, TN_DIMS, preferred_element_type=F32)


def _colsum8(v):
    m, n = v.shape
    return jnp.sum(v.reshape(m // 8, 8, n), axis=0)


def _ln_stats(z):
    mu = jnp.mean(z, axis=-1, keepdims=True)
    zc = z - mu
    var = jnp.mean(zc * zc, axis=-1, keepdims=True)
    rstd = lax.rsqrt(var + LN_EPS)
    return zc * rstd, rstd


def _ln_fwd(z, g, b):
    zhat, _ = _ln_stats(z)
    return zhat * g + b


def _ln_bwd(dy, z, g):
    zhat, rstd = _ln_stats(z)
    dzh = dy * g
    m1 = jnp.mean(dzh, axis=-1, keepdims=True)
    m2 = jnp.mean(dzh * zhat, axis=-1, keepdims=True)
    return rstd * (dzh - m1 - zhat * m2), zhat


def _silu_and_grad(n):
    sg = jax.nn.sigmoid(n)
    return n * sg, sg * (1.0 + n * (1.0 - sg))


def _acc_init(i, *refs):
    @pl.when(i == 0)
    def _():
        for r in refs:
            r[...] = jnp.zeros_like(r)


def _mesh_pos():
    x, y, c = lax.axis_index("x"), lax.axis_index("y"), lax.axis_index("c")
    chips = [(1 - x, y), (x, 1 - y), (1 - x, 1 - y)]
    return x, y, c, chips


HBM_SPEC = pl.BlockSpec(memory_space=pltpu.HBM)


def _all_gather_weights(wpack, spack):
    def body(wp_ref, sp_ref, gw_ref, gs_ref, send_sems, recv_sems, local_sems):
        x, y, c, chips = _mesh_pos()
        me = 2 * x + y
        sibling = (x, y, 1 - c)

        def big(k, chip_row, half, to, src=None):
            dst = gw_ref.at[chip_row, half]
            return pltpu.make_async_remote_copy(
                src_ref=dst if src is None else src, dst_ref=dst, send_sem=send_sems.at[k], recv_sem=recv_sems.at[k],
                device_id=to, device_id_type=MESH)

        def small(k, chip_row, to):
            return pltpu.make_async_remote_copy(
                src_ref=sp_ref, dst_ref=gs_ref.at[chip_row], send_sem=send_sems.at[k], recv_sem=recv_sems.at[k],
                device_id=to, device_id_type=MESH)

        own_w = pltpu.make_async_copy(wp_ref, gw_ref.at[me], local_sems.at[0])
        own_s = pltpu.make_async_copy(sp_ref, gs_ref.at[me], local_sems.at[1])
        own_w.start()
        own_s.start()
        first = [big(j, me, c, (cx, cy, c), src=wp_ref.at[c]) for j, (cx, cy) in enumerate(chips)]
        first += [small(3 + j, me, (cx, cy, c)) for j, (cx, cy) in enumerate(chips)]
        for cp in first:
            cp.start()
        passed = [big(6 + j, 2 * cx + cy, c, sibling) for j, (cx, cy) in enumerate(chips)]
        for j, (cx, cy) in enumerate(chips):
            big(j, 2 * cx + cy, c, (x, y, c)).wait_recv()
            passed[j].start()
        for j, (cx, cy) in enumerate(chips):
            small(3 + j, 2 * cx + cy, (x, y, c)).wait_recv()
        for j, (cx, cy) in enumerate(chips):
            big(6 + j, 2 * cx + cy, 1 - c, (x, y, c)).wait_recv()
        for cp in first + passed:
            cp.wait_send()
        own_w.wait()
        own_s.wait()

    return pl.pallas_call(
        body, name="all_gather_weights",
        out_shape=(jax.ShapeDtypeStruct((N_CHIPS,) + wpack.shape, wpack.dtype),
                   jax.ShapeDtypeStruct((N_CHIPS,) + spack.shape, spack.dtype)),
        in_specs=[HBM_SPEC, HBM_SPEC], out_specs=(HBM_SPEC, HBM_SPEC),
        scratch_shapes=[pltpu.SemaphoreType.DMA((9,)), pltpu.SemaphoreType.DMA((9,)), pltpu.SemaphoreType.DMA((2,))],
    )(wpack, spack)


def _pair_exchange_halves(p):
    n, _, r, l = p.shape

    def body(p_ref, got_ref, send_sem, recv_sem):
        x, y, c, _ = _mesh_pos()
        cp = pltpu.make_async_remote_copy(
            src_ref=p_ref.at[:, 1 - c], dst_ref=got_ref, send_sem=send_sem, recv_sem=recv_sem,
            device_id=(x, y, 1 - c), device_id_type=MESH)
        cp.start()
        cp.wait()

    return pl.pallas_call(
        body, name="grad_pair_exchange", out_shape=jax.ShapeDtypeStruct((n, r, l), p.dtype),
        in_specs=[HBM_SPEC], out_specs=HBM_SPEC,
        scratch_shapes=[pltpu.SemaphoreType.DMA(()), pltpu.SemaphoreType.DMA(())],
    )(p)


def _pair_sum(p, got, c):
    n, _, r, l = p.shape

    def body(c_ref, p_ref, got_ref, out_ref):
        out_ref[...] = p_ref[...] + got_ref[...]

    return pl.pallas_call(
        body, name="grad_pair_sum", out_shape=jax.ShapeDtypeStruct((n, r, l), F32),
        grid_spec=pltpu.PrefetchScalarGridSpec(
            num_scalar_prefetch=1, grid=(n, r // PACK_ROWS),
            in_specs=[pl.BlockSpec((None, None, PACK_ROWS, l), lambda j, i, c_ref: (j, c_ref[0], i, 0)),
                      pl.BlockSpec((None, PACK_ROWS, l), lambda j, i, c_ref: (j, i, 0))],
            out_specs=pl.BlockSpec((None, PACK_ROWS, l), lambda j, i, c_ref: (j, i, 0))),
        compiler_params=_cparams(2),
    )(c, p, got)


def _chip_scatter(s):
    n, r, l = s.shape

    def body(s_ref, got_ref, send_sems, recv_sems, local_sem):
        x, y, c, chips = _mesh_pos()
        me = 2 * x + y
        own = pltpu.make_async_copy(s_ref.at[me], got_ref.at[me], local_sem)
        own.start()
        sends = [pltpu.make_async_remote_copy(
            src_ref=s_ref.at[2 * cx + cy], dst_ref=got_ref.at[me], send_sem=send_sems.at[j], recv_sem=recv_sems.at[j],
            device_id=(cx, cy, c), device_id_type=MESH) for j, (cx, cy) in enumerate(chips)]
        for cp in sends:
            cp.start()
        for j, (cx, cy) in enumerate(chips):
            pltpu.make_async_remote_copy(
                src_ref=s_ref.at[me], dst_ref=got_ref.at[2 * cx + cy], send_sem=send_sems.at[j],
                recv_sem=recv_sems.at[j], device_id=(x, y, c), device_id_type=MESH).wait_recv()
        for cp in sends:
            cp.wait_send()
        own.wait()

    return pl.pallas_call(
        body, name="grad_chip_scatter", out_shape=jax.ShapeDtypeStruct((n, r, l), s.dtype),
        in_specs=[HBM_SPEC], out_specs=HBM_SPEC,
        scratch_shapes=[pltpu.SemaphoreType.DMA((3,)), pltpu.SemaphoreType.DMA((3,)), pltpu.SemaphoreType.DMA(())],
    )(s)


def _chip_sum(got):
    n, r, l = got.shape

    def body(got_ref, out_ref):
        out_ref[...] = ((got_ref[0] + got_ref[1]) + got_ref[2]) + got_ref[3]

    return pl.pallas_call(
        body, name="grad_chip_sum", out_shape=jax.ShapeDtypeStruct((r, l), F32), grid=(r // PACK_ROWS,),
        in_specs=[pl.BlockSpec((n, PACK_ROWS, l), lambda i: (0, i, 0))],
        out_specs=pl.BlockSpec((PACK_ROWS, l), lambda i: (i, 0)), compiler_params=_cparams(1),
    )(got)


def _pair_share(f):
    r, l = f.shape

    def body(f_ref, out_ref, send_sem, recv_sem, local_sem):
        x, y, c, _ = _mesh_pos()
        own = pltpu.make_async_copy(f_ref, out_ref.at[c], local_sem)
        own.start()
        cp = pltpu.make_async_remote_copy(
            src_ref=f_ref, dst_ref=out_ref.at[c], send_sem=send_sem, recv_sem=recv_sem,
            device_id=(x, y, 1 - c), device_id_type=MESH)
        cp.start()
        pltpu.make_async_remote_copy(
            src_ref=f_ref, dst_ref=out_ref.at[1 - c], send_sem=send_sem, recv_sem=recv_sem,
            device_id=(x, y, 1 - c), device_id_type=MESH).wait_recv()
        cp.wait_send()
        own.wait()

    return pl.pallas_call(
        body, name="grad_pair_share", out_shape=jax.ShapeDtypeStruct((2, r, l), f.dtype),
        in_specs=[HBM_SPEC], out_specs=HBM_SPEC,
        scratch_shapes=[pltpu.SemaphoreType.DMA(()), pltpu.SemaphoreType.DMA(()), pltpu.SemaphoreType.DMA(())],
    )(f)


def _fwd_pw1_glu(x, w1s, b1, tm):
    t, d = x.shape
    dh = d // 2

    def body(x_ref, w_ref, b_ref, a_ref, g_ref, u_ref):
        xb = x_ref[...].astype(BF16)
        for hh in range(2):
            cs = slice(hh * dh, (hh + 1) * dh)
            a = _dot(xb, w_ref[hh]) + b_ref[:, hh * dh:(hh + 1) * dh]
            g = _dot(xb, w_ref[2 + hh]) + b_ref[:, d + hh * dh:d + (hh + 1) * dh]
            a_ref[:, cs] = a.astype(BF16)
            g_ref[:, cs] = g.astype(BF16)
            u_ref[:, cs] = a * jax.nn.sigmoid(g)

    return pl.pallas_call(
        body, name="fwd_pw1_glu", grid=(t // tm,),
        in_specs=[_rows(tm, d), _const((4, d, dh)), _const((1, 2 * d))],
        out_specs=[_rows(tm, d)] * 3,
        out_shape=[jax.ShapeDtypeStruct((t, d), BF16), jax.ShapeDtypeStruct((t, d), BF16),
                   jax.ShapeDtypeStruct((t, d), F32)],
        compiler_params=_cparams(),
    )(x, w1s, b1)


def _fill_shifted(sh_ref, ext_ref):
    n = sh_ref.shape[1]
    for s in range(8):
        sh_ref[s] = ext_ref[pl.ds(s, n), :]


CONV_CHUNK = 16


def _fwd_conv_tail(u, x0, wdw, bdw, lng, lnb, w2, b2, mixg, mixb, tm):
    t, d = u.shape
    hb = tm // CONV_HALO

    def body(u_ref, uh_ref, x_ref, w_ref, bdw_ref, lng_ref, lnb_ref, w2_ref, b2_ref, mg_ref, mb_ref,
             c_ref, z_ref, y_ref, ext, sh):
        i = pl.program_id(0)
        ext[0:CONV_HALO] = jnp.where(i == 0, 0.0, uh_ref[...])
        ext[CONV_HALO:CONV_HALO + tm] = u_ref[...]
        ext[CONV_HALO + tm:CONV_HALO + tm + 8] = jnp.zeros((8, d), F32)
        _fill_shifted(sh, ext)

        def chunk(r, carry):
            base = pl.multiple_of(r * CONV_CHUNK, CONV_CHUNK)
            acc = jnp.zeros((CONV_CHUNK, d), F32)
            for k in range(CONV_WIDTH):
                e = k + CONV_HALO - (CONV_WIDTH - 1)
                acc = acc + w_ref[k:k + 1, :] * sh[e % 8, pl.ds(base + (e // 8) * 8, CONV_CHUNK), :]
            c_ref[pl.ds(base, CONV_CHUNK), :] = acc + bdw_ref[...]
            return carry

        lax.fori_loop(0, tm // CONV_CHUNK, chunk, 0)
        n = _ln_fwd(c_ref[...], lng_ref[...], lnb_ref[...])
        s = n * jax.nn.sigmoid(n)
        m = _dot(s.astype(BF16), w2_ref[...]) + b2_ref[...]
        z = ALPHA * x_ref[...] + m
        z_ref[...] = z
        y_ref[...] = _ln_fwd(z, mg_ref[...], mb_ref[...])

    vec = _const((1, d))
    return pl.pallas_call(
        body, name="fwd_conv_tail", grid=(t // tm,),
        in_specs=[_rows(tm, d), pl.BlockSpec((CONV_HALO, d), lambda i: (jnp.maximum(i * hb - 1, 0), 0)), _rows(tm, d),
                  _const((CONV_HALO, d)), vec, vec, vec, _const((d, d)), vec, vec, vec],
        out_specs=[_rows(tm, d)] * 3,
        out_shape=[jax.ShapeDtypeStruct((t, d), F32)] * 3,
        scratch_shapes=[pltpu.VMEM((tm + CONV_HALO + 8, d), F32), pltpu.VMEM((8, tm + CONV_HALO, d), F32)],
        compiler_params=_cparams(),
    )(u, u, x0, wdw, bdw, lng, lnb, w2, b2, mixg, mixb)


def _fwd_ffn(x, wg, wu, wd, layer, lng, lnb, tm):
    t, d = x.shape
    fs = wg.shape[-1]

    def body(x_ref, wg_ref, wu_ref, wd_ref, g_ref, b_ref, gg_ref, uu_ref, z_ref, y_ref):
        xv = x_ref[...]
        xb = xv.astype(BF16)
        f = jnp.zeros((tm, d), F32)
        for j in range(N_CHIPS):
            gj = _dot(xb, wg_ref[j])
            uj = _dot(xb, wu_ref[j])
            gg_ref[j] = gj.astype(BF16)
            uu_ref[j] = uj.astype(BF16)
            hm = gj * jax.nn.sigmoid(gj) * uj
            f = f + _dot(hm.astype(BF16), wd_ref[j])
        z = ALPHA * xv + f
        z_ref[...] = z
        y_ref[...] = _ln_fwd(z, g_ref[...], b_ref[...])

    wcol = pl.BlockSpec((N_CHIPS, None, d, fs), lambda i: (0, layer, 0, 0), pipeline_mode=pl.Buffered(1))
    wrow = pl.BlockSpec((N_CHIPS, None, fs, d), lambda i: (0, layer, 0, 0), pipeline_mode=pl.Buffered(1))
    hid = pl.BlockSpec((N_CHIPS, tm, fs), lambda i: (0, i, 0))
    return pl.pallas_call(
        body, name=f"fwd_ffn{layer}", grid=(t // tm,),
        in_specs=[_rows(tm, d), wcol, wcol, wrow, _const((1, d)), _const((1, d))],
        out_specs=[hid, hid, _rows(tm, d), _rows(tm, d)],
        out_shape=[jax.ShapeDtypeStruct((N_CHIPS, t, fs), BF16)] * 2 + [jax.ShapeDtypeStruct((t, d), F32)] * 2,
        compiler_params=_cparams(),
    )(x, wg, wu, wd, lng, lnb)


def _attn_bias(nq):
    qi = lax.broadcasted_iota(jnp.int32, (BLOCK, 2 * BLOCK), 0)
    kj = lax.broadcasted_iota(jnp.int32, (BLOCK, 2 * BLOCK), 1)
    delta = qi + BLOCK - kj
    band = jnp.logical_and(delta >= 0, delta < BLOCK)
    return delta.astype(F32), band, kj


def _slope(h, nq):
    return 2.0 ** (-ALIBI_MAX * (h + 1) / nq)


def _softmax_with_sink(qh, kk, slope, delta, valid, sink):
    s = _dot_nt(qh, kk) * (1.0 / math.sqrt(HEAD_DIM)) - slope * delta
    s = jnp.where(valid, s, NEG_INF)
    m = jnp.maximum(jnp.max(s, axis=-1, keepdims=True), sink)
    p = jnp.exp(s - m)
    e_sink = jnp.exp(sink - m)
    den = jnp.sum(p, axis=-1, keepdims=True) + e_sink
    inv = 1.0 / den
    return p * inv, e_sink * inv


def _fwd_attn(x, wq, bq, wkv, bkv, sinks, wo, bo, mixg, mixb, tm):
    t, d = x.shape
    nq = d // HEAD_DIM
    group = nq // N_KV_HEADS
    nb = tm // BLOCK

    def body(sink_ref, x_ref, xh_ref, wq_ref, bq_ref, wkv_ref, bkv_ref, wo_ref, bo_ref, mg_ref, mb_ref,
             q_ref, kv_ref, o_ref, z_ref, y_ref, kvext, o_scr):
        i = pl.program_id(0)
        xv = x_ref[...]
        xb = xv.astype(BF16)
        q_ref[...] = (_dot(xb, wq_ref[...]) + bq_ref[...]).astype(BF16)
        kvb = (_dot(xb, wkv_ref[...]) + bkv_ref[...]).astype(BF16)
        kv_ref[...] = kvb
        kvext[0:BLOCK] = (_dot(xh_ref[...].astype(BF16), wkv_ref[...]) + bkv_ref[...]).astype(BF16)
        kvext[BLOCK:BLOCK + tm] = kvb
        delta, band, kj = _attn_bias(nq)

        def block(b, carry):
            r0 = pl.multiple_of(b * BLOCK, BLOCK)
            first = jnp.logical_and(i == 0, b == 0)
            valid = jnp.logical_and(band, jnp.logical_or(kj >= BLOCK, jnp.logical_not(first)))
            for h in range(nq):
                kvh = h // group
                qh = q_ref[pl.ds(r0, BLOCK), h * HEAD_DIM:(h + 1) * HEAD_DIM]
                kk = kvext[pl.ds(r0, 2 * BLOCK), kvh * HEAD_DIM:(kvh + 1) * HEAD_DIM]
                vv = kvext[pl.ds(r0, 2 * BLOCK), KVD + kvh * HEAD_DIM:KVD + (kvh + 1) * HEAD_DIM]
                p, _ = _softmax_with_sink(qh, kk, _slope(h, nq), delta, valid, sink_ref[h])
                o_scr[pl.ds(r0, BLOCK), h * HEAD_DIM:(h + 1) * HEAD_DIM] = _dot(p.astype(BF16), vv)
            return carry

        lax.fori_loop(0, nb, block, 0)
        ob = o_scr[...].astype(BF16)
        o_ref[...] = ob
        z = ALPHA * xv + _dot(ob, wo_ref[...]) + bo_ref[...]
        z_ref[...] = z
        y_ref[...] = _ln_fwd(z, mg_ref[...], mb_ref[...])

    hb = tm // BLOCK
    vec = _const((1, d))
    return pl.pallas_call(
        body, name="fwd_attn", grid=(t // tm,),
        in_specs=[pl.BlockSpec(memory_space=pltpu.SMEM),
                  _rows(tm, d), pl.BlockSpec((BLOCK, d), lambda i: (jnp.maximum(i * hb - 1, 0), 0)),
                  _const((d, d)), vec, _const((d, 2 * KVD)), _const((1, 2 * KVD)), _const((d, d)), vec, vec, vec],
        out_specs=[_rows(tm, d), _rows(tm, 2 * KVD), _rows(tm, d), _rows(tm, d), _rows(tm, d)],
        out_shape=[jax.ShapeDtypeStruct((t, d), BF16), jax.ShapeDtypeStruct((t, 2 * KVD), BF16),
                   jax.ShapeDtypeStruct((t, d), BF16), jax.ShapeDtypeStruct((t, d), F32),
                   jax.ShapeDtypeStruct((t, d), F32)],
        scratch_shapes=[pltpu.VMEM((tm + BLOCK, 2 * KVD), BF16), pltpu.VMEM((tm, d), F32)],
        compiler_params=_cparams(),
    )(sinks, x, x, wq, bq, wkv, bkv, wo, bo, mixg, mixb)


def _loss_grad(y, target, tm):
    t, d = y.shape
    nt = t // tm

    def body(y_ref, t_ref, dy_ref, loss_ref, acc):
        i = pl.program_id(0)
        _acc_init(i, acc)
        e = y_ref[...] - t_ref[...]
        dy_ref[...] = e * (1.0 / d)
        acc[...] += _colsum8(e * e)

        @pl.when(i == nt - 1)
        def _():
            loss_ref[...] = jnp.sum(acc[...], keepdims=True) * (0.5 / d)

    return pl.pallas_call(
        body, name="loss_grad", grid=(nt,), in_specs=[_rows(tm, d), _rows(tm, d)],
        out_specs=[_rows(tm, d), pl.BlockSpec((1, 1), lambda i: (0, 0))],
        out_shape=[jax.ShapeDtypeStruct((t, d), F32), jax.ShapeDtypeStruct((1, 1), F32)],
        scratch_shapes=[pltpu.VMEM((8, d), F32)], compiler_params=_cparams(),
    )(y, target)


def _write_sums(i, nt, pairs):
    @pl.when(i == nt - 1)
    def _():
        for out_ref, acc in pairs:
            out_ref[...] = jnp.sum(acc[...], axis=0, keepdims=True)


def _bwd_ffn_dx(dy, z, gg, uu, wg, wu, wd, layer, lng, tm):
    t, d = dy.shape
    fs = wg.shape[-1]
    nt = t // tm

    def body(dy_ref, z_ref, gg_ref, uu_ref, wg_ref, wu_ref, wd_ref, g_ref,
             dz_ref, dgg_ref, duu_ref, hm_ref, dx_ref, dlg_ref, dlb_ref, acc_g, acc_b):
        i = pl.program_id(0)
        _acc_init(i, acc_g, acc_b)
        dyv = dy_ref[...]
        dz, zhat = _ln_bwd(dyv, z_ref[...], g_ref[...])
        acc_g[...] += _colsum8(dyv * zhat)
        acc_b[...] += _colsum8(dyv)
        dzb = dz.astype(BF16)
        dz_ref[...] = dzb
        dx = ALPHA * dz
        for j in range(N_CHIPS):
            dh = _dot_nt(dzb, wd_ref[j])
            gj = gg_ref[j].astype(F32)
            uj = uu_ref[j].astype(F32)
            act, dact = _silu_and_grad(gj)
            hm_ref[j] = (act * uj).astype(BF16)
            dgb = (dh * uj * dact).astype(BF16)
            dub = (dh * act).astype(BF16)
            dgg_ref[j] = dgb
            duu_ref[j] = dub
            dx = dx + _dot_nt(dgb, wg_ref[j]) + _dot_nt(dub, wu_ref[j])
        dx_ref[...] = dx
        _write_sums(i, nt, [(dlg_ref, acc_g), (dlb_ref, acc_b)])

    wcol = pl.BlockSpec((N_CHIPS, None, d, fs), lambda i: (0, layer, 0, 0), pipeline_mode=pl.Buffered(1))
    wrow = pl.BlockSpec((N_CHIPS, None, fs, d), lambda i: (0, layer, 0, 0), pipeline_mode=pl.Buffered(1))
    hid = pl.BlockSpec((N_CHIPS, tm, fs), lambda i: (0, i, 0))
    vec = _const((1, d))
    return pl.pallas_call(
        body, name=f"bwd_ffn_dx{layer}", grid=(nt,),
        in_specs=[_rows(tm, d), _rows(tm, d), hid, hid, wcol, wcol, wrow, vec],
        out_specs=[_rows(tm, d), hid, hid, hid, _rows(tm, d), _acc_out((1, d)), _acc_out((1, d))],
        out_shape=[jax.ShapeDtypeStruct((t, d), BF16)] + [jax.ShapeDtypeStruct((N_CHIPS, t, fs), BF16)] * 3
        + [jax.ShapeDtypeStruct((t, d), F32)] + [jax.ShapeDtypeStruct((1, d), F32)] * 2,
        scratch_shapes=[pltpu.VMEM((8, d), F32)] * 2, compiler_params=_cparams(),
    )(dy, z, gg, uu, wg, wu, wd, lng)


def _matmul_tn(a, b, tt, name):
    ja, t, ka = a.shape
    jb, _, nb = b.shape
    nj = max(ja, jb)

    def body(a_ref, b_ref, o_ref):
        @pl.when(pl.program_id(1) == 0)
        def _():
            o_ref[...] = jnp.zeros_like(o_ref)

        o_ref[...] += _dot_tn(a_ref[...].astype(BF16), b_ref[...].astype(BF16))

    return pl.pallas_call(
        body, name=name, grid=(nj, t // tt),
        in_specs=[pl.BlockSpec((None, tt, ka), lambda j, i: (j if ja > 1 else 0, i, 0)),
                  pl.BlockSpec((None, tt, nb), lambda j, i: (j if jb > 1 else 0, i, 0))],
        out_specs=pl.BlockSpec((None, ka, nb), lambda j, i: (j, 0, 0)),
        out_shape=jax.ShapeDtypeStruct((nj, ka, nb), F32), compiler_params=_cparams(2),
    )(a, b)


def _bwd_attn(dy, z, q, kv, sinks, wo, wq, wkv, mixg, tm):
    t, d = dy.shape
    nq = d // HEAD_DIM
    group = nq // N_KV_HEADS
    nb = tm // BLOCK
    nt = t // tm
    hb = tm // BLOCK

    def body(sink_ref, dy_ref, z_ref, q_ref, kv_ref, kvh_ref, wo_ref, wq_ref, wkv_ref, g_ref,
             dz_ref, dqkv_ref, dx_ref, dlg_ref, dlb_ref, dbo_ref, dbq_ref, dbkv_ref, dsink_ref,
             kvext, dkvext, do_scr, dq_scr, carry, acc_g, acc_b, acc_o, acc_q, acc_kv, acc_s):
        i = pl.program_id(0)
        ti = nt - 1 - i
        _acc_init(i, carry, acc_g, acc_b, acc_o, acc_q, acc_kv, acc_s)
        dyv = dy_ref[...]
        dz, zhat = _ln_bwd(dyv, z_ref[...], g_ref[...])
        acc_g[...] += _colsum8(dyv * zhat)
        acc_b[...] += _colsum8(dyv)
        acc_o[...] += _colsum8(dz)
        dzb = dz.astype(BF16)
        dz_ref[...] = dzb
        do_scr[...] = _dot_nt(dzb, wo_ref[...]).astype(BF16)
        kvext[0:BLOCK] = kvh_ref[...]
        kvext[BLOCK:BLOCK + tm] = kv_ref[...]
        dkvext[0:tm] = jnp.zeros((tm, 2 * KVD), F32)
        dkvext[tm:tm + BLOCK] = carry[...]
        delta, band, kj = _attn_bias(nq)

        def block(b, c):
            r0 = pl.multiple_of(b * BLOCK, BLOCK)
            first = jnp.logical_and(ti == 0, b == 0)
            valid = jnp.logical_and(band, jnp.logical_or(kj >= BLOCK, jnp.logical_not(first)))
            for kvh in range(N_KV_HEADS):
                kk = kvext[pl.ds(r0, 2 * BLOCK), kvh * HEAD_DIM:(kvh + 1) * HEAD_DIM]
                vv = kvext[pl.ds(r0, 2 * BLOCK), KVD + kvh * HEAD_DIM:KVD + (kvh + 1) * HEAD_DIM]
                dk = jnp.zeros((2 * BLOCK, HEAD_DIM), F32)
                dv = jnp.zeros((2 * BLOCK, HEAD_DIM), F32)
                for g in range(group):
                    h = kvh * group + g
                    cols = slice(h * HEAD_DIM, (h + 1) * HEAD_DIM)
                    qh = q_ref[pl.ds(r0, BLOCK), cols]
                    doh = do_scr[pl.ds(r0, BLOCK), cols]
                    p, p_sink = _softmax_with_sink(qh, kk, _slope(h, nq), delta, valid, sink_ref[h])
                    dp = _dot_nt(doh, vv)
                    rs = jnp.sum(p * dp, axis=-1, keepdims=True)
                    dsb = (p * (dp - rs) * (1.0 / math.sqrt(HEAD_DIM))).astype(BF16)
                    acc_s[:, h:h + 1] += -(p_sink * rs)
                    dq_scr[pl.ds(r0, BLOCK), cols] = _dot(dsb, kk)
                    dk = dk + _dot_tn(dsb, qh)
                    dv = dv + _dot_tn(p.astype(BF16), doh)
                dkvext[pl.ds(r0, 2 * BLOCK), kvh * HEAD_DIM:(kvh + 1) * HEAD_DIM] += dk
                dkvext[pl.ds(r0, 2 * BLOCK), KVD + kvh * HEAD_DIM:KVD + (kvh + 1) * HEAD_DIM] += dv
            return c

        lax.fori_loop(0, nb, block, 0)
        carry[...] = dkvext[0:BLOCK]
        dq = dq_scr[...]
        dkv = dkvext[BLOCK:BLOCK + tm]
        acc_q[...] += _colsum8(dq)
        acc_kv[...] += _colsum8(dkv)
        dqb = dq.astype(BF16)
        dkvb = dkv.astype(BF16)
        dqkv_ref[:, 0:d] = dqb
        dqkv_ref[:, d:d + 2 * KVD] = dkvb
        dx_ref[...] = ALPHA * dz + _dot_nt(dqb, wq_ref[...]) + _dot_nt(dkvb, wkv_ref[...])
        _write_sums(i, nt, [(dlg_ref, acc_g), (dlb_ref, acc_b), (dbo_ref, acc_o), (dbq_ref, acc_q),
                            (dbkv_ref, acc_kv), (dsink_ref, acc_s)])

    rev = lambda w: pl.BlockSpec((tm, w), lambda i: (nt - 1 - i, 0))
    vec = _const((1, d))
    return pl.pallas_call(
        body, name="bwd_attn", grid=(nt,),
        in_specs=[pl.BlockSpec(memory_space=pltpu.SMEM), rev(d), rev(d), rev(d), rev(2 * KVD),
                  pl.BlockSpec((BLOCK, 2 * KVD), lambda i: (jnp.maximum((nt - 1 - i) * hb - 1, 0), 0)),
                  _const((d, d)), _const((d, d)), _const((d, 2 * KVD)), vec],
        out_specs=[rev(d), rev(d + 2 * KVD), rev(d)] + [_acc_out((1, d))] * 4
        + [_acc_out((1, 2 * KVD)), _acc_out((1, nq))],
        out_shape=[jax.ShapeDtypeStruct((t, d), BF16), jax.ShapeDtypeStruct((t, d + 2 * KVD), BF16),
                   jax.ShapeDtypeStruct((t, d), F32)] + [jax.ShapeDtypeStruct((1, d), F32)] * 4
        + [jax.ShapeDtypeStruct((1, 2 * KVD), F32), jax.ShapeDtypeStruct((1, nq), F32)],
        scratch_shapes=[pltpu.VMEM((tm + BLOCK, 2 * KVD), BF16), pltpu.VMEM((tm + BLOCK, 2 * KVD), F32),
                        pltpu.VMEM((tm, d), BF16), pltpu.VMEM((tm, d), F32), pltpu.VMEM((BLOCK, 2 * KVD), F32),
                        pltpu.VMEM((8, d), F32), pltpu.VMEM((8, d), F32), pltpu.VMEM((8, d), F32),
                        pltpu.VMEM((8, d), F32), pltpu.VMEM((8, 2 * KVD), F32), pltpu.VMEM((BLOCK, nq), F32)],
        compiler_params=_cparams(),
    )(sinks, dy, z, q, kv, kv, wo, wq, wkv, mixg)


def _bwd_conv_head(dy, z, c, w2, mixg, lng, lnb, tm):
    t, d = dy.shape
    nt = t // tm

    def body(dy_ref, z_ref, c_ref, w2_ref, mg_ref, lg_ref, lb_ref,
             dz_ref, s_ref, dc_ref, dmg_ref, dmb_ref, db2_ref, dlg_ref, dlb_ref, a0, a1, a2, a3, a4):
        i = pl.program_id(0)
        _acc_init(i, a0, a1, a2, a3, a4)
        dyv = dy_ref[...]
        dz, zhat = _ln_bwd(dyv, z_ref[...], mg_ref[...])
        a0[...] += _colsum8(dyv * zhat)
        a1[...] += _colsum8(dyv)
        a2[...] += _colsum8(dz)
        dz_ref[...] = dz
        chat, rstd = _ln_stats(c_ref[...])
        n = chat * lg_ref[...] + lb_ref[...]
        act, dact = _silu_and_grad(n)
        s_ref[...] = act.astype(BF16)
        dn = _dot_nt(dz.astype(BF16), w2_ref[...]) * dact
        a3[...] += _colsum8(dn * chat)
        a4[...] += _colsum8(dn)
        dch = dn * lg_ref[...]
        m1 = jnp.mean(dch, axis=-1, keepdims=True)
        m2 = jnp.mean(dch * chat, axis=-1, keepdims=True)
        dc_ref[...] = rstd * (dch - m1 - chat * m2)
        _write_sums(i, nt, [(dmg_ref, a0), (dmb_ref, a1), (db2_ref, a2), (dlg_ref, a3), (dlb_ref, a4)])

    vec = _const((1, d))
    return pl.pallas_call(
        body, name="bwd_conv_head", grid=(nt,),
        in_specs=[_rows(tm, d), _rows(tm, d), _rows(tm, d), _const((d, d)), vec, vec, vec],
        out_specs=[_rows(tm, d), _rows(tm, d), _rows(tm, d)] + [_acc_out((1, d))] * 5,
        out_shape=[jax.ShapeDtypeStruct((t, d), F32), jax.ShapeDtypeStruct((t, d), BF16),
                   jax.ShapeDtypeStruct((t, d), F32)] + [jax.ShapeDtypeStruct((1, d), F32)] * 5,
        scratch_shapes=[pltpu.VMEM((8, d), F32)] * 5, compiler_params=_cparams(),
    )(dy, z, c, w2, mixg, lng, lnb)


def _bwd_conv_glu(dc, u, a, g, dz, wdw, w1s, tm):
    t, d = dc.shape
    dh_w = d // 2
    nt = t // tm
    hb = tm // CONV_HALO
    last_halo = t // CONV_HALO - 1

    def body(dc_ref, dcn_ref, u_ref, up_ref, a_ref, g_ref, dz_ref, w_ref, w1_ref,
             dx_ref, dh_ref, db1_ref, dbdw_ref, dw_ref, ext, sh, du_scr, acc_b1, acc_bdw, acc_w):
        i = pl.program_id(0)
        _acc_init(i, acc_b1, acc_bdw, acc_w)
        dcv = dc_ref[...]
        acc_bdw[...] += _colsum8(dcv)

        ext[0:tm] = dcv
        ext[tm:tm + CONV_HALO] = jnp.where(i == nt - 1, 0.0, dcn_ref[...])
        ext[tm + CONV_HALO:tm + CONV_HALO + 8] = jnp.zeros((8, d), F32)
        _fill_shifted(sh, ext)

        def du_chunk(r, carry):
            base = pl.multiple_of(r * CONV_CHUNK, CONV_CHUNK)
            acc = jnp.zeros((CONV_CHUNK, d), F32)
            for k in range(CONV_WIDTH):
                e = CONV_WIDTH - 1 - k
                acc = acc + w_ref[k:k + 1, :] * sh[e % 8, pl.ds(base + (e // 8) * 8, CONV_CHUNK), :]
            du_scr[pl.ds(base, CONV_CHUNK), :] = acc
            return carry

        lax.fori_loop(0, tm // CONV_CHUNK, du_chunk, 0)

        ext[0:CONV_HALO] = jnp.where(i == 0, 0.0, up_ref[...])
        ext[CONV_HALO:CONV_HALO + tm] = u_ref[...]
        _fill_shifted(sh, ext)
        for k in range(CONV_WIDTH):
            e = k + CONV_HALO - (CONV_WIDTH - 1)

            def dw_chunk(r, acc, e=e):
                base = pl.multiple_of(r * CONV_CHUNK, CONV_CHUNK)
                return acc + dc_ref[pl.ds(base, CONV_CHUNK), :] * sh[e % 8, pl.ds(base + (e // 8) * 8, CONV_CHUNK), :]

            acc = lax.fori_loop(0, tm // CONV_CHUNK, dw_chunk, jnp.zeros((CONV_CHUNK, d), F32))
            acc_w[k] += acc[0:8] + acc[8:16]

        du = du_scr[...]
        av = a_ref[...].astype(F32)
        sg = jax.nn.sigmoid(g_ref[...].astype(F32))
        da = du * sg
        dg = du * av * sg * (1.0 - sg)
        acc_b1[:, 0:d] += _colsum8(da)
        acc_b1[:, d:2 * d] += _colsum8(dg)
        dx = ALPHA * dz_ref[...]
        for j, part in enumerate([da[:, 0:dh_w], da[:, dh_w:d], dg[:, 0:dh_w], dg[:, dh_w:d]]):
            pb = part.astype(BF16)
            dh_ref[j] = pb
            dx = dx + _dot_nt(pb, w1_ref[j])
        dx_ref[...] = dx

        @pl.when(i == nt - 1)
        def _():
            db1_ref[...] = jnp.sum(acc_b1[...], axis=0, keepdims=True)
            dbdw_ref[...] = jnp.sum(acc_bdw[...], axis=0, keepdims=True)
            dw_ref[...] = jnp.sum(acc_w[...], axis=1)

    return pl.pallas_call(
        body, name="bwd_conv_glu", grid=(nt,),
        in_specs=[_rows(tm, d), pl.BlockSpec((CONV_HALO, d), lambda i: (jnp.minimum((i + 1) * hb, last_halo), 0)),
                  _rows(tm, d), pl.BlockSpec((CONV_HALO, d), lambda i: (jnp.maximum(i * hb - 1, 0), 0)),
                  _rows(tm, d), _rows(tm, d), _rows(tm, d), _const((CONV_HALO, d)), _const((4, d, dh_w))],
        out_specs=[_rows(tm, d), pl.BlockSpec((4, tm, dh_w), lambda i: (0, i, 0)), _acc_out((1, 2 * d)),
                   _acc_out((1, d)), _acc_out((CONV_HALO, d))],
        out_shape=[jax.ShapeDtypeStruct((t, d), F32), jax.ShapeDtypeStruct((4, t, dh_w), BF16),
                   jax.ShapeDtypeStruct((1, 2 * d), F32), jax.ShapeDtypeStruct((1, d), F32),
                   jax.ShapeDtypeStruct((CONV_HALO, d), F32)],
        scratch_shapes=[pltpu.VMEM((tm + CONV_HALO + 8, d), F32), pltpu.VMEM((8, tm + CONV_HALO, d), F32),
                        pltpu.VMEM((tm, d), F32), pltpu.VMEM((8, 2 * d), F32), pltpu.VMEM((8, d), F32),
                        pltpu.VMEM((CONV_HALO, 8, d), F32)],
        compiler_params=_cparams(),
    )(dc, dc, u, u, a, g, dz, wdw, w1s)


def _row_block(rows, target):
    best = rows
    for cand in range(8, min(rows, target) + 1, 8):
        if rows % cand == 0:
            best = cand
    return best


def _adamw(w, g, m, v, name):
    rows, lanes = w.shape
    br = _row_block(rows, 512) if rows % 8 == 0 else rows

    def body(w_ref, g_ref, m_ref, v_ref, d_ref, nm_ref, nv_ref):
        gv = g_ref[...]
        nm = ADAM_B1 * m_ref[...] + (1.0 - ADAM_B1) * gv
        nv = ADAM_B2 * v_ref[...] + (1.0 - ADAM_B2) * (gv * gv)
        m_hat = nm / (1.0 - ADAM_B1 ** ADAM_STEP)
        v_hat = nv / (1.0 - ADAM_B2 ** ADAM_STEP)
        d_ref[...] = -ADAM_LR * (m_hat / (jnp.sqrt(v_hat) + ADAM_EPS) + ADAM_WD * w_ref[...])
        nm_ref[...] = nm
        nv_ref[...] = nv

    spec = pl.BlockSpec((br, lanes), lambda i: (i, 0))
    return pl.pallas_call(
        body, name=name, grid=(rows // br,), in_specs=[spec] * 4, out_specs=[spec] * 3,
        out_shape=[jax.ShapeDtypeStruct((rows, lanes), F32)] * 3, compiler_params=_cparams(),
    )(w, g, m, v)


def _pad_to(v, n):
    return jnp.pad(v, (0, n - v.shape[0]))


def _round_up(n, m):
    return (n + m - 1) // m * m


def kernel(x, conv_w_pw1, conv_b_pw1, conv_w_dw, conv_b_dw, conv_ln_g, conv_ln_b, conv_w_pw2, conv_b_pw2, kv_w_k, kv_b_k, kv_w_v, kv_b_v, attn_w_q, attn_b_q, attn_sinks, attn_w_o, attn_b_o, ffn_w_gate, ffn_w_up, ffn_w_down, ln_mix_g, ln_mix_b, ln_ffn_g, ln_ffn_b, loss_target, m_conv_w_pw1, m_conv_b_pw1, m_conv_w_dw, m_conv_b_dw, m_conv_ln_g, m_conv_ln_b, m_conv_w_pw2, m_conv_b_pw2, m_kv_w_k, m_kv_b_k, m_kv_w_v, m_kv_b_v, m_attn_w_q, m_attn_b_q, m_attn_sinks, m_attn_w_o, m_attn_b_o, m_ffn_w_gate, m_ffn_w_up, m_ffn_w_down, m_ln_mix_g, m_ln_mix_b, m_ln_ffn_g, m_ln_ffn_b, v_conv_w_pw1, v_conv_b_pw1, v_conv_w_dw, v_conv_b_dw, v_conv_ln_g, v_conv_ln_b, v_conv_w_pw2, v_conv_b_pw2, v_kv_w_k, v_kv_b_k, v_kv_w_v, v_kv_b_v, v_attn_w_q, v_attn_b_q, v_attn_sinks, v_attn_w_o, v_attn_b_o, v_ffn_w_gate, v_ffn_w_up, v_ffn_w_down, v_ln_mix_g, v_ln_mix_b, v_ln_ffn_g, v_ln_ffn_b):
    args = dict(locals())
    w = {n: args[n] for n in WEIGHTS}
    mom = {n: args["m_" + n] for n in WEIGHTS}
    var = {n: args["v_" + n] for n in WEIGHTS}
    assert x.shape[0] == 1, "one sequence per device"
    t, d = x.shape[1], x.shape[2]
    dq = d // 4
    fs = ffn_w_gate.shape[-1]
    nq = d // HEAD_DIM
    x0 = x.reshape(t, d)
    target = loss_target.reshape(t, d)
    tm_big = min(512, t)
    tm_mid = min(256, t)
    c_idx = lax.axis_index("c")

    big_sizes = [int(w[n].size) for n in BIG]
    n_big = sum(big_sizes)
    rw = _round_up(n_big, 2 * 16 * PACK_LANES) // (2 * PACK_LANES)
    wflat = jnp.concatenate([w[n].astype(BF16).reshape(-1) for n in BIG])
    wpack = _pad_to(wflat, 2 * rw * PACK_LANES).reshape(2, rw, PACK_LANES)
    small_sizes = [int(w[n].size) for n in SMALL_SHARDED]
    rs = _round_up(sum(small_sizes), 8 * 128) // 128
    spack = _pad_to(jnp.concatenate([w[n].reshape(-1) for n in SMALL_SHARDED]), rs * 128).reshape(rs, 128)
    gw, gs = _all_gather_weights(wpack, spack)
    gw = gw.reshape(N_CHIPS, 2 * rw * PACK_LANES)
    gs = gs.reshape(N_CHIPS, rs * 128)
    full, off = {}, 0
    for n, size in zip(BIG, big_sizes):
        full[n] = gw[:, off:off + size].reshape((N_CHIPS,) + w[n].shape)
        off += size
    off = 0
    for n, size in zip(SMALL_SHARDED, small_sizes):
        full[n] = gs[:, off:off + size].reshape((N_CHIPS,) + w[n].shape)
        off += size
    w1s = full['conv_w_pw1'].reshape(N_CHIPS, d, d // 2)
    w2 = full['conv_w_pw2'].reshape(d, d)
    wkv = jnp.concatenate([full['kv_w_k'].reshape(d, KVD), full['kv_w_v'].reshape(d, KVD)], axis=1)
    wq = full['attn_w_q'].reshape(d, d)
    wo = full['attn_w_o'].reshape(d, d)
    wg, wu, wd = full['ffn_w_gate'], full['ffn_w_up'], full['ffn_w_down']
    b1 = full['conv_b_pw1'].reshape(1, 2 * d)
    wdw = jnp.pad(full['conv_w_dw'].reshape(N_CHIPS, CONV_WIDTH, dq).transpose(1, 0, 2).reshape(CONV_WIDTH, d),
                  ((0, CONV_HALO - CONV_WIDTH), (0, 0)))
    bdw = full['conv_b_dw'].reshape(1, d)
    clng = full['conv_ln_g'].reshape(1, d)
    clnb = full['conv_ln_b'].reshape(1, d)
    b2 = full['conv_b_pw2'].reshape(1, d)
    bkv = jnp.concatenate([kv_b_k, kv_b_v]).reshape(1, 2 * KVD)
    sinks = attn_sinks.reshape(nq)
    mixg = [ln_mix_g[l].reshape(1, d) for l in range(DEPTH)]
    mixb = [ln_mix_b[l].reshape(1, d) for l in range(DEPTH)]
    ffng = [ln_ffn_g[l].reshape(1, d) for l in range(DEPTH)]
    ffnb = [ln_ffn_b[l].reshape(1, d) for l in range(DEPTH)]

    a_act, g_act, u_act = _fwd_pw1_glu(x0, w1s, b1, tm_big)
    c_act, z1, x1 = _fwd_conv_tail(u_act, x0, wdw, bdw, clng, clnb, w2, b2, mixg[0], mixb[0], tm_mid)
    gg0, uu0, z2, x2 = _fwd_ffn(x1, wg, wu, wd, 0, ffng[0], ffnb[0], tm_big)
    q_act, kv_act, o_act, z3, x3 = _fwd_attn(x2, wq, attn_b_q, wkv, bkv, sinks, wo, attn_b_o, mixg[1], mixb[1], tm_big)
    gg1, uu1, z4, x4 = _fwd_ffn(x3, wg, wu, wd, 1, ffng[1], ffnb[1], tm_big)
    dx4, loss_part = _loss_grad(x4, target, tm_big)
    loss = lax.psum(loss_part[0, 0], ("x", "y", "c"))

    grads = {}
    dz4, dgg1, duu1, hm1, dx3, d_fg1, d_fb1 = _bwd_ffn_dx(dx4, z4, gg1, uu1, wg, wu, wd, 1, ffng[1], tm_mid)
    dwg1 = _matmul_tn(x3[None], dgg1, tm_big, "dw_gate1")
    dwu1 = _matmul_tn(x3[None], duu1, tm_big, "dw_up1")
    dwd1 = _matmul_tn(hm1, dz4[None], tm_big, "dw_down1")
    (dz3, dqkv, dx2, d_mg1, d_mb1, d_bo, d_bq, d_bkv, d_sinks) = _bwd_attn(
        dx3, z3, q_act, kv_act, sinks, wo, wq, wkv, mixg[1], tm_big)
    dwo = _matmul_tn(o_act[None], dz3[None], tm_big, "dw_o")
    dwqkv = _matmul_tn(x2[None], dqkv[None], tm_big, "dw_qkv")[0]
    dz2, dgg0, duu0, hm0, dx1, d_fg0, d_fb0 = _bwd_ffn_dx(dx2, z2, gg0, uu0, wg, wu, wd, 0, ffng[0], tm_mid)
    dwg0 = _matmul_tn(x1[None], dgg0, tm_big, "dw_gate0")
    dwu0 = _matmul_tn(x1[None], duu0, tm_big, "dw_up0")
    dwd0 = _matmul_tn(hm0, dz2[None], tm_big, "dw_down0")
    dz1, s_act, dc, d_mg0, d_mb0, d_b2, d_clng, d_clnb = _bwd_conv_head(dx1, z1, c_act, w2, mixg[0], clng, clnb, tm_mid)
    dw2 = _matmul_tn(s_act[None], dz1[None], tm_big, "dw_pw2")
    dx0, dh1, d_b1, d_bdw, d_wdw = _bwd_conv_glu(dc, u_act, a_act, g_act, dz1, wdw, w1s, tm_mid)
    dw1 = _matmul_tn(x0[None], dh1, tm_big, "dw_pw1")

    def rows4(v):
        return v.reshape(N_CHIPS, -1)

    def rep4(v):
        return jnp.broadcast_to(v.reshape(1, -1), (N_CHIPS, v.size))

    local = {
        'conv_w_pw1': rows4(dw1), 'conv_w_pw2': rows4(dw2), 'kv_w_k': rows4(dwqkv[:, d:d + KVD]),
        'kv_w_v': rows4(dwqkv[:, d + KVD:d + 2 * KVD]), 'attn_w_q': rows4(dwqkv[:, 0:d]), 'attn_w_o': rows4(dwo),
        'ffn_w_gate': jnp.concatenate([rows4(dwg0), rows4(dwg1)], axis=1),
        'ffn_w_up': jnp.concatenate([rows4(dwu0), rows4(dwu1)], axis=1),
        'ffn_w_down': jnp.concatenate([rows4(dwd0), rows4(dwd1)], axis=1),
        'conv_b_pw1': rows4(d_b1),
        'conv_w_dw': rows4(d_wdw[0:CONV_WIDTH].reshape(CONV_WIDTH, N_CHIPS, dq).transpose(1, 0, 2)),
        'conv_b_dw': rows4(d_bdw), 'conv_ln_g': rows4(d_clng), 'conv_ln_b': rows4(d_clnb), 'conv_b_pw2': rows4(d_b2),
        'kv_b_k': rep4(d_bkv[:, 0:KVD]), 'kv_b_v': rep4(d_bkv[:, KVD:2 * KVD]), 'attn_b_q': rep4(d_bq),
        'attn_sinks': rep4(d_sinks), 'attn_b_o': rep4(d_bo),
        'ln_mix_g': rep4(jnp.concatenate([d_mg0, d_mg1])), 'ln_mix_b': rep4(jnp.concatenate([d_mb0, d_mb1])),
        'ln_ffn_g': rep4(jnp.concatenate([d_fg0, d_fg1])), 'ln_ffn_b': rep4(jnp.concatenate([d_fb0, d_fb1])),
    }
    order = BIG + SMALL
    sizes = [int(w[n].size) for n in order]
    n_all = sum(sizes)
    rg = _round_up(n_all, 2 * PACK_ROWS * PACK_LANES) // (2 * PACK_LANES)
    packed = jnp.concatenate([local[n] for n in order], axis=1)
    packed = jnp.pad(packed, ((0, 0), (0, 2 * rg * PACK_LANES - n_all))).reshape(N_CHIPS, 2, rg, PACK_LANES)

    from_sibling = _pair_exchange_halves(packed)
    pair = _pair_sum(packed, from_sibling, c_idx.reshape(1).astype(jnp.int32))
    from_chips = _chip_scatter(pair)
    mine = _chip_sum(from_chips)
    gsum = _pair_share(mine).reshape(-1)

    g_out, off = {}, 0
    for n, size in zip(order, sizes):
        g_out[n] = gsum[off:off + size].reshape(w[n].shape)
        off += size
    delta, new_m, new_v = {}, {}, {}
    for n in BIG:
        shape = w[n].shape
        two_d = (-1, shape[-1])
        dl, nm, nv = _adamw(w[n].reshape(two_d), g_out[n].reshape(two_d), mom[n].reshape(two_d), var[n].reshape(two_d),
                            "adamw_" + n)
        delta[n], new_m[n], new_v[n] = dl.reshape(shape), nm.reshape(shape), nv.reshape(shape)
    n_small = sum(int(w[n].size) for n in SMALL)
    small_rows = _round_up(n_small, 8 * 128) // 128

    def pack_small(tree):
        return _pad_to(jnp.concatenate([tree[n].reshape(-1) for n in SMALL]), small_rows * 128).reshape(small_rows, 128)

    dl, nm, nv = _adamw(pack_small(w), pack_small(g_out), pack_small(mom), pack_small(var), "adamw_small")
    off = 0
    for n in SMALL:
        size, shape = int(w[n].size), w[n].shape
        for tree, flat in ((delta, dl), (new_m, nm), (new_v, nv)):
            tree[n] = flat.reshape(-1)[off:off + size].reshape(shape)
        off += size

    return (loss, dx0.reshape(x.shape), *[g_out[n] for n in WEIGHTS], *[delta[n] for n in WEIGHTS],
            *[new_m[n] for n in WEIGHTS], *[new_v[n] for n in WEIGHTS])
```

```python
import functools
import math

import jax
import jax.numpy as jnp
from jax import lax
from jax.experimental import pallas as pl
from jax.experimental.pallas import tpu as pltpu

F32 = jnp.float32
BF16 = jnp.bfloat16

DEPTH = 2
ALPHA = (2.0 * DEPTH) ** 0.25
LN_EPS = 1e-5
NEG_INF = -1e30
HEAD_DIM = 64
N_KV_HEADS = 2
KVD = N_KV_HEADS * HEAD_DIM
BLOCK = 128
CONV_WIDTH = 31
CONV_HALO = 32
ALIBI_MAX = 8.0
ADAM_LR, ADAM_B1, ADAM_B2, ADAM_EPS, ADAM_WD, ADAM_STEP = 0.001, 0.9, 0.999, 1e-08, 0.01, 10

N_CHIPS = 4
PACK_ROWS = 256
VMEM_LIMIT = 56 * 1024 * 1024
MESH = pl.DeviceIdType.MESH

NT_DIMS = (((1,), (1,)), ((), ()))
TN_DIMS = (((0,), (0,)), ((), ()))

WEIGHTS = ['conv_w_pw1', 'conv_b_pw1', 'conv_w_dw', 'conv_b_dw', 'conv_ln_g', 'conv_ln_b', 'conv_w_pw2', 'conv_b_pw2',
           'kv_w_k', 'kv_b_k', 'kv_w_v', 'kv_b_v', 'attn_w_q', 'attn_b_q', 'attn_sinks', 'attn_w_o', 'attn_b_o',
           'ffn_w_gate', 'ffn_w_up', 'ffn_w_down', 'ln_mix_g', 'ln_mix_b', 'ln_ffn_g', 'ln_ffn_b']
BIG = ['conv_w_pw1', 'conv_w_pw2', 'kv_w_k', 'kv_w_v', 'attn_w_q', 'attn_w_o', 'ffn_w_gate', 'ffn_w_up', 'ffn_w_down']
SMALL_SHARDED = ['conv_b_pw1', 'conv_w_dw', 'conv_b_dw', 'conv_ln_g', 'conv_ln_b', 'conv_b_pw2']
REPLICATED = ['kv_b_k', 'kv_b_v', 'attn_b_q', 'attn_sinks', 'attn_b_o', 'ln_mix_g', 'ln_mix_b', 'ln_ffn_g', 'ln_ffn_b']
SMALL = SMALL_SHARDED + REPLICATED


def _cparams(n_grid=1):
    return pltpu.CompilerParams(dimension_semantics=("arbitrary",) * n_grid, vmem_limit_bytes=VMEM_LIMIT)


def _rows(tm, width):
    return pl.BlockSpec((tm, width), lambda i: (i, 0))


def _const(shape):
    return pl.BlockSpec(shape, lambda *_: (0,) * len(shape), pipeline_mode=pl.Buffered(1))


def _acc_out(shape):
    return pl.BlockSpec(shape, lambda *_: (0,) * len(shape))


def _dot(a, b):
    return jnp.dot(a, b, preferred_element_type=F32)


def _dot_nt(a, b):
    return lax.dot_general(a, b, NT_DIMS, preferred_element_type=F32)


def _dot_tn(a, b):
    return lax.dot_general(a, b, TN_DIMS, preferred_element_type=F32)


def _colsum8(v):
    m, n = v.shape
    return jnp.sum(v.reshape(m // 8, 8, n), axis=0)


def _ln_stats(z):
    mu = jnp.mean(z, axis=-1, keepdims=True)
    zc = z - mu
    var = jnp.mean(zc * zc, axis=-1, keepdims=True)
    rstd = lax.rsqrt(var + LN_EPS)
    return zc * rstd, rstd


def _ln_fwd(z, g, b):
    zhat, _ = _ln_stats(z)
    return zhat * g + b


def _ln_bwd(dy, z, g):
    zhat, rstd = _ln_stats(z)
    dzh = dy * g
    m1 = jnp.mean(dzh, axis=-1, keepdims=True)
    m2 = jnp.mean(dzh * zhat, axis=-1, keepdims=True)
    return rstd * (dzh - m1 - zhat * m2), zhat


def _silu_and_grad(n):
    sg = jax.nn.sigmoid(n)
    return n * sg, sg * (1.0 + n * (1.0 - sg))


def _acc_init(i, *refs):
    @pl.when(i == 0)
    def _():
        for r in refs:
            r[...] = jnp.zeros_like(r)


def _mesh_pos():
    x, y, c = lax.axis_index("x"), lax.axis_index("y"), lax.axis_index("c")
    chips = [(1 - x, y), (x, 1 - y), (1 - x, 1 - y)]
    return x, y, c, chips


HBM_SPEC = pl.BlockSpec(memory_space=pltpu.HBM)


def _remote(src, dst, send_sems, recv_sems, k, to):
    return pltpu.make_async_remote_copy(src_ref=src, dst_ref=dst, send_sem=send_sems.at[k], recv_sem=recv_sems.at[k],
                                        device_id=to, device_id_type=MESH)


def _all_gather_weights(halves, spack):
    n = len(halves)

    def body(*refs):
        w_refs, sp_ref, g_refs, gs_ref = refs[0:n], refs[n], refs[n + 1:2 * n + 1], refs[2 * n + 1]
        send_sems, recv_sems, local_sems = refs[2 * n + 2:]
        x, y, c, chips = _mesh_pos()
        me = 2 * x + y
        here, sibling = (x, y, c), (x, y, 1 - c)

        def big(p, k, chip_row, half, to, src=None):
            dst = g_refs[p].at[chip_row, half]
            return _remote(dst if src is None else src, dst, send_sems, recv_sems, 6 * p + k, to)

        def small(k, chip_row, to):
            return _remote(sp_ref, gs_ref.at[chip_row], send_sems, recv_sems, 6 * n + k, to)

        own = [pltpu.make_async_copy(w_refs[p], g_refs[p].at[me], local_sems.at[p]) for p in range(n)]
        own.append(pltpu.make_async_copy(sp_ref, gs_ref.at[me], local_sems.at[n]))
        for cp in own:
            cp.start()
        first = [small(j, me, (cx, cy, c)) for j, (cx, cy) in enumerate(chips)]
        first += [big(p, j, me, c, (cx, cy, c), src=w_refs[p].at[c]) for p in range(n)
                  for j, (cx, cy) in enumerate(chips)]
        for cp in first:
            cp.start()
        passed = []
        for p in range(n):
            for j, (cx, cy) in enumerate(chips):
                big(p, j, 2 * cx + cy, c, here).wait_recv()
                passed.append(big(p, 3 + j, 2 * cx + cy, c, sibling))
                passed[-1].start()
        for j, (cx, cy) in enumerate(chips):
            small(j, 2 * cx + cy, here).wait_recv()
        for p in range(n):
            for j, (cx, cy) in enumerate(chips):
                big(p, 3 + j, 2 * cx + cy, 1 - c, here).wait_recv()
        for cp in first + passed:
            cp.wait_send()
        for cp in own:
            cp.wait()

    n_sem = 6 * n + 3
    return pl.pallas_call(
        body, name="all_gather_weights",
        out_shape=tuple(jax.ShapeDtypeStruct((N_CHIPS,) + h.shape, h.dtype) for h in halves)
        + (jax.ShapeDtypeStruct((N_CHIPS,) + spack.shape, spack.dtype),),
        in_specs=[HBM_SPEC] * (n + 1), out_specs=(HBM_SPEC,) * (n + 1),
        scratch_shapes=[pltpu.SemaphoreType.DMA((n_sem,)), pltpu.SemaphoreType.DMA((n_sem,)),
                        pltpu.SemaphoreType.DMA((n + 1,))],
    )(*halves, spack)


def _pair_exchange(plist):
    n = len(plist)

    def body(*refs):
        p_refs, got_refs, send_sems, recv_sems = refs[0:n], refs[n:2 * n], refs[2 * n], refs[2 * n + 1]
        x, y, c, _ = _mesh_pos()
        cps = [_remote(p_refs[k].at[:, 1 - c], got_refs[k], send_sems, recv_sems, k, (x, y, 1 - c)) for k in range(n)]
        for cp in cps:
            cp.start()
        for cp in cps:
            cp.wait()

    return pl.pallas_call(
        body, name="grad_pair_exchange",
        out_shape=tuple(jax.ShapeDtypeStruct((p.shape[0],) + p.shape[2:], p.dtype) for p in plist),
        in_specs=[HBM_SPEC] * n, out_specs=(HBM_SPEC,) * n,
        scratch_shapes=[pltpu.SemaphoreType.DMA((n,)), pltpu.SemaphoreType.DMA((n,))],
    )(*plist)


def _pair_sum(p, got, c, name):
    n, _, r, l = p.shape
    br = _row_block(r, PACK_ROWS)

    def body(c_ref, p_ref, got_ref, out_ref):
        out_ref[...] = p_ref[...] + got_ref[...]

    return pl.pallas_call(
        body, name=name, out_shape=jax.ShapeDtypeStruct((n, r, l), F32),
        grid_spec=pltpu.PrefetchScalarGridSpec(
            num_scalar_prefetch=1, grid=(n, r // br),
            in_specs=[pl.BlockSpec((None, None, br, l), lambda j, i, c_ref: (j, c_ref[0], i, 0)),
                      pl.BlockSpec((None, br, l), lambda j, i, c_ref: (j, i, 0))],
            out_specs=pl.BlockSpec((None, br, l), lambda j, i, c_ref: (j, i, 0))),
        compiler_params=_cparams(2),
    )(c, p, got)


def _chip_scatter(slist):
    n = len(slist)

    def body(*refs):
        s_refs, got_refs = refs[0:n], refs[n:2 * n]
        send_sems, recv_sems, local_sems = refs[2 * n:]
        x, y, c, chips = _mesh_pos()
        me = 2 * x + y
        own = [pltpu.make_async_copy(s_refs[k].at[me], got_refs[k].at[me], local_sems.at[k]) for k in range(n)]
        for cp in own:
            cp.start()
        sends = [_remote(s_refs[k].at[2 * cx + cy], got_refs[k].at[me], send_sems, recv_sems, 3 * k + j, (cx, cy, c))
                 for k in range(n) for j, (cx, cy) in enumerate(chips)]
        for cp in sends:
            cp.start()
        for k in range(n):
            for j, (cx, cy) in enumerate(chips):
                _remote(s_refs[k].at[me], got_refs[k].at[2 * cx + cy], send_sems, recv_sems, 3 * k + j,
                        (x, y, c)).wait_recv()
        for cp in sends:
            cp.wait_send()
        for cp in own:
            cp.wait()

    return pl.pallas_call(
        body, name="grad_chip_scatter", out_shape=tuple(jax.ShapeDtypeStruct(s.shape, s.dtype) for s in slist),
        in_specs=[HBM_SPEC] * n, out_specs=(HBM_SPEC,) * n,
        scratch_shapes=[pltpu.SemaphoreType.DMA((3 * n,)), pltpu.SemaphoreType.DMA((3 * n,)),
                        pltpu.SemaphoreType.DMA((n,))],
    )(*slist)


def _chip_sum(got, name):
    n, r, l = got.shape
    br = _row_block(r, PACK_ROWS)

    def body(got_ref, out_ref):
        out_ref[...] = ((got_ref[0] + got_ref[1]) + got_ref[2]) + got_ref[3]

    return pl.pallas_call(
        body, name=name, out_shape=jax.ShapeDtypeStruct((r, l), F32), grid=(r // br,),
        in_specs=[pl.BlockSpec((n, br, l), lambda i: (0, i, 0))],
        out_specs=pl.BlockSpec((br, l), lambda i: (i, 0)), compiler_params=_cparams(1),
    )(got)


def _pair_share(flist):
    n = len(flist)

    def body(*refs):
        f_refs, out_refs = refs[0:n], refs[n:2 * n]
        send_sems, recv_sems, local_sems = refs[2 * n:]
        x, y, c, _ = _mesh_pos()
        own = [pltpu.make_async_copy(f_refs[k], out_refs[k].at[c], local_sems.at[k]) for k in range(n)]
        for cp in own:
            cp.start()
        sends = [_remote(f_refs[k], out_refs[k].at[c], send_sems, recv_sems, k, (x, y, 1 - c)) for k in range(n)]
        for cp in sends:
            cp.start()
        for k in range(n):
            _remote(f_refs[k], out_refs[k].at[1 - c], send_sems, recv_sems, k, (x, y, c)).wait_recv()
        for cp in sends:
            cp.wait_send()
        for cp in own:
            cp.wait()

    return pl.pallas_call(
        body, name="grad_pair_share", out_shape=tuple(jax.ShapeDtypeStruct((2,) + f.shape, f.dtype) for f in flist),
        in_specs=[HBM_SPEC] * n, out_specs=(HBM_SPEC,) * n,
        scratch_shapes=[pltpu.SemaphoreType.DMA((n,)), pltpu.SemaphoreType.DMA((n,)), pltpu.SemaphoreType.DMA((n,))],
    )(*flist)


def _fwd_pw1_glu(x, w1s, b1, tm):
    t, d = x.shape
    dh = d // 2

    def body(x_ref, w_ref, b_ref, a_ref, g_ref, u_ref):
        xb = x_ref[...].astype(BF16)
        for hh in range(2):
            cs = slice(hh * dh, (hh + 1) * dh)
            a = _dot(xb, w_ref[hh]) + b_ref[:, hh * dh:(hh + 1) * dh]
            g = _dot(xb, w_ref[2 + hh]) + b_ref[:, d + hh * dh:d + (hh + 1) * dh]
            a_ref[:, cs] = a.astype(BF16)
            g_ref[:, cs] = g.astype(BF16)
            u_ref[:, cs] = a * jax.nn.sigmoid(g)

    return pl.pallas_call(
        body, name="fwd_pw1_glu", grid=(t // tm,),
        in_specs=[_rows(tm, d), _const((4, d, dh)), _const((1, 2 * d))],
        out_specs=[_rows(tm, d)] * 3,
        out_shape=[jax.ShapeDtypeStruct((t, d), BF16), jax.ShapeDtypeStruct((t, d), BF16),
                   jax.ShapeDtypeStruct((t, d), F32)],
        compiler_params=_cparams(),
    )(x, w1s, b1)


def _fill_shifted(sh_ref, ext_ref):
    n = sh_ref.shape[1]
    for s in range(8):
        sh_ref[s] = ext_ref[pl.ds(s, n), :]


CONV_CHUNK = 16


def _fwd_conv_tail(u, x0, wdw, bdw, lng, lnb, w2, b2, mixg, mixb, tm):
    t, d = u.shape
    hb = tm // CONV_HALO

    def body(u_ref, uh_ref, x_ref, w_ref, bdw_ref, lng_ref, lnb_ref, w2_ref, b2_ref, mg_ref, mb_ref,
             c_ref, z_ref, y_ref, ext, sh):
        i = pl.program_id(0)
        ext[0:CONV_HALO] = jnp.where(i == 0, 0.0, uh_ref[...])
        ext[CONV_HALO:CONV_HALO + tm] = u_ref[...]
        ext[CONV_HALO + tm:CONV_HALO + tm + 8] = jnp.zeros((8, d), F32)
        _fill_shifted(sh, ext)

        def chunk(r, carry):
            base = pl.multiple_of(r * CONV_CHUNK, CONV_CHUNK)
            acc = jnp.zeros((CONV_CHUNK, d), F32)
            for k in range(CONV_WIDTH):
                e = k + CONV_HALO - (CONV_WIDTH - 1)
                acc = acc + w_ref[k:k + 1, :] * sh[e % 8, pl.ds(base + (e // 8) * 8, CONV_CHUNK), :]
            c_ref[pl.ds(base, CONV_CHUNK), :] = acc + bdw_ref[...]
            return carry

        lax.fori_loop(0, tm // CONV_CHUNK, chunk, 0)
        n = _ln_fwd(c_ref[...], lng_ref[...], lnb_ref[...])
        s = n * jax.nn.sigmoid(n)
        m = _dot(s.astype(BF16), w2_ref[...]) + b2_ref[...]
        z = ALPHA * x_ref[...] + m
        z_ref[...] = z
        y_ref[...] = _ln_fwd(z, mg_ref[...], mb_ref[...])

    vec = _const((1, d))
    return pl.pallas_call(
        body, name="fwd_conv_tail", grid=(t // tm,),
        in_specs=[_rows(tm, d), pl.BlockSpec((CONV_HALO, d), lambda i: (jnp.maximum(i * hb - 1, 0), 0)), _rows(tm, d),
                  _const((CONV_HALO, d)), vec, vec, vec, _const((d, d)), vec, vec, vec],
        out_specs=[_rows(tm, d)] * 3,
        out_shape=[jax.ShapeDtypeStruct((t, d), F32)] * 3,
        scratch_shapes=[pltpu.VMEM((tm + CONV_HALO + 8, d), F32), pltpu.VMEM((8, tm + CONV_HALO, d), F32)],
        compiler_params=_cparams(),
    )(u, u, x0, wdw, bdw, lng, lnb, w2, b2, mixg, mixb)


def _fwd_ffn(x, wg, wu, wd, layer, lng, lnb, tm):
    t, d = x.shape
    fs = wg.shape[-1]

    def body(x_ref, wg_ref, wu_ref, wd_ref, g_ref, b_ref, gg_ref, uu_ref, z_ref, y_ref):
        xv = x_ref[...]
        xb = xv.astype(BF16)
        f = jnp.zeros((tm, d), F32)
        for j in range(N_CHIPS):
            gj = _dot(xb, wg_ref[j])
            uj = _dot(xb, wu_ref[j])
            gg_ref[j] = gj.astype(BF16)
            uu_ref[j] = uj.astype(BF16)
            hm = gj * jax.nn.sigmoid(gj) * uj
            f = f + _dot(hm.astype(BF16), wd_ref[j])
        z = ALPHA * xv + f
        z_ref[...] = z
        y_ref[...] = _ln_fwd(z, g_ref[...], b_ref[...])

    wcol = pl.BlockSpec((N_CHIPS, None, d, fs), lambda i: (0, layer, 0, 0), pipeline_mode=pl.Buffered(1))
    wrow = pl.BlockSpec((N_CHIPS, None, fs, d), lambda i: (0, layer, 0, 0), pipeline_mode=pl.Buffered(1))
    hid = pl.BlockSpec((N_CHIPS, tm, fs), lambda i: (0, i, 0))
    return pl.pallas_call(
        body, name=f"fwd_ffn{layer}", grid=(t // tm,),
        in_specs=[_rows(tm, d), wcol, wcol, wrow, _const((1, d)), _const((1, d))],
        out_specs=[hid, hid, _rows(tm, d), _rows(tm, d)],
        out_shape=[jax.ShapeDtypeStruct((N_CHIPS, t, fs), BF16)] * 2 + [jax.ShapeDtypeStruct((t, d), F32)] * 2,
        compiler_params=_cparams(),
    )(x, wg, wu, wd, lng, lnb)


def _attn_bias(nq):
    qi = lax.broadcasted_iota(jnp.int32, (BLOCK, 2 * BLOCK), 0)
    kj = lax.broadcasted_iota(jnp.int32, (BLOCK, 2 * BLOCK), 1)
    delta = qi + BLOCK - kj
    band = jnp.logical_and(delta >= 0, delta < BLOCK)
    return delta.astype(F32), band, kj


def _slope(h, nq):
    return 2.0 ** (-ALIBI_MAX * (h + 1) / nq)


def _softmax_with_sink(qh, kk, slope, delta, valid, sink):
    s = _dot_nt(qh, kk) * (1.0 / math.sqrt(HEAD_DIM)) - slope * delta
    s = jnp.where(valid, s, NEG_INF)
    m = jnp.maximum(jnp.max(s, axis=-1, keepdims=True), sink)
    p = jnp.exp(s - m)
    e_sink = jnp.exp(sink - m)
    den = jnp.sum(p, axis=-1, keepdims=True) + e_sink
    inv = 1.0 / den
    return p * inv, e_sink * inv


def _fwd_attn(x, wq, bq, wkv, bkv, sinks, wo, bo, mixg, mixb, tm):
    t, d = x.shape
    nq = d // HEAD_DIM
    group = nq // N_KV_HEADS
    nb = tm // BLOCK

    def body(sink_ref, x_ref, xh_ref, wq_ref, bq_ref, wkv_ref, bkv_ref, wo_ref, bo_ref, mg_ref, mb_ref,
             q_ref, kv_ref, o_ref, z_ref, y_ref, kvext, o_scr):
        i = pl.program_id(0)
        xv = x_ref[...]
        xb = xv.astype(BF16)
        q_ref[...] = (_dot(xb, wq_ref[...]) + bq_ref[...]).astype(BF16)
        kvb = (_dot(xb, wkv_ref[...]) + bkv_ref[...]).astype(BF16)
        kv_ref[...] = kvb
        kvext[0:BLOCK] = (_dot(xh_ref[...].astype(BF16), wkv_ref[...]) + bkv_ref[...]).astype(BF16)
        kvext[BLOCK:BLOCK + tm] = kvb
        delta, band, kj = _attn_bias(nq)

        def block(b, carry):
            r0 = pl.multiple_of(b * BLOCK, BLOCK)
            first = jnp.logical_and(i == 0, b == 0)
            valid = jnp.logical_and(band, jnp.logical_or(kj >= BLOCK, jnp.logical_not(first)))
            for h in range(nq):
                kvh = h // group
                qh = q_ref[pl.ds(r0, BLOCK), h * HEAD_DIM:(h + 1) * HEAD_DIM]
                kk = kvext[pl.ds(r0, 2 * BLOCK), kvh * HEAD_DIM:(kvh + 1) * HEAD_DIM]
                vv = kvext[pl.ds(r0, 2 * BLOCK), KVD + kvh * HEAD_DIM:KVD + (kvh + 1) * HEAD_DIM]
                p, _ = _softmax_with_sink(qh, kk, _slope(h, nq), delta, valid, sink_ref[h])
                o_scr[pl.ds(r0, BLOCK), h * HEAD_DIM:(h + 1) * HEAD_DIM] = _dot(p.astype(BF16), vv)
            return carry

        lax.fori_loop(0, nb, block, 0)
        ob = o_scr[...].astype(BF16)
        o_ref[...] = ob
        z = ALPHA * xv + _dot(ob, wo_ref[...]) + bo_ref[...]
        z_ref[...] = z
        y_ref[...] = _ln_fwd(z, mg_ref[...], mb_ref[...])

    hb = tm // BLOCK
    vec = _const((1, d))
    return pl.pallas_call(
        body, name="fwd_attn", grid=(t // tm,),
        in_specs=[pl.BlockSpec(memory_space=pltpu.SMEM),
                  _rows(tm, d), pl.BlockSpec((BLOCK, d), lambda i: (jnp.maximum(i * hb - 1, 0), 0)),
                  _const((d, d)), vec, _const((d, 2 * KVD)), _const((1, 2 * KVD)), _const((d, d)), vec, vec, vec],
        out_specs=[_rows(tm, d), _rows(tm, 2 * KVD), _rows(tm, d), _rows(tm, d), _rows(tm, d)],
        out_shape=[jax.ShapeDtypeStruct((t, d), BF16), jax.ShapeDtypeStruct((t, 2 * KVD), BF16),
                   jax.ShapeDtypeStruct((t, d), BF16), jax.ShapeDtypeStruct((t, d), F32),
                   jax.ShapeDtypeStruct((t, d), F32)],
        scratch_shapes=[pltpu.VMEM((tm + BLOCK, 2 * KVD), BF16), pltpu.VMEM((tm, d), F32)],
        compiler_params=_cparams(),
    )(sinks, x, x, wq, bq, wkv, bkv, wo, bo, mixg, mixb)


def _loss_grad(y, target, tm):
    t, d = y.shape
    nt = t // tm

    def body(y_ref, t_ref, dy_ref, loss_ref, acc):
        i = pl.program_id(0)
        _acc_init(i, acc)
        e = y_ref[...] - t_ref[...]
        dy_ref[...] = e * (1.0 / d)
        acc[...] += _colsum8(e * e)

        @pl.when(i == nt - 1)
        def _():
            loss_ref[...] = jnp.sum(acc[...], keepdims=True) * (0.5 / d)

    return pl.pallas_call(
        body, name="loss_grad", grid=(nt,), in_specs=[_rows(tm, d), _rows(tm, d)],
        out_specs=[_rows(tm, d), pl.BlockSpec((1, 1), lambda i: (0, 0))],
        out_shape=[jax.ShapeDtypeStruct((t, d), F32), jax.ShapeDtypeStruct((1, 1), F32)],
        scratch_shapes=[pltpu.VMEM((8, d), F32)], compiler_params=_cparams(),
    )(y, target)


def _write_sums(i, nt, pairs):
    @pl.when(i == nt - 1)
    def _():
        for out_ref, acc in pairs:
            out_ref[...] = jnp.sum(acc[...], axis=0, keepdims=True)


def _bwd_ffn_dx(dy, z, gg, uu, wg, wu, wd, layer, lng, tm):
    t, d = dy.shape
    fs = wg.shape[-1]
    nt = t // tm

    def body(dy_ref, z_ref, gg_ref, uu_ref, wg_ref, wu_ref, wd_ref, g_ref,
             dz_ref, dgg_ref, duu_ref, hm_ref, dx_ref, dlg_ref, dlb_ref, acc_g, acc_b):
        i = pl.program_id(0)
        _acc_init(i, acc_g, acc_b)
        dyv = dy_ref[...]
        dz, zhat = _ln_bwd(dyv, z_ref[...], g_ref[...])
        acc_g[...] += _colsum8(dyv * zhat)
        acc_b[...] += _colsum8(dyv)
        dzb = dz.astype(BF16)
        dz_ref[...] = dzb
        dx = ALPHA * dz
        for j in range(N_CHIPS):
            dh = _dot_nt(dzb, wd_ref[j])
            gj = gg_ref[j].astype(F32)
            uj = uu_ref[j].astype(F32)
            act, dact = _silu_and_grad(gj)
            hm_ref[j] = (act * uj).astype(BF16)
            dgb = (dh * uj * dact).astype(BF16)
            dub = (dh * act).astype(BF16)
            dgg_ref[j] = dgb
            duu_ref[j] = dub
            dx = dx + _dot_nt(dgb, wg_ref[j]) + _dot_nt(dub, wu_ref[j])
        dx_ref[...] = dx
        _write_sums(i, nt, [(dlg_ref, acc_g), (dlb_ref, acc_b)])

    wcol = pl.BlockSpec((N_CHIPS, None, d, fs), lambda i: (0, layer, 0, 0), pipeline_mode=pl.Buffered(1))
    wrow = pl.BlockSpec((N_CHIPS, None, fs, d), lambda i: (0, layer, 0, 0), pipeline_mode=pl.Buffered(1))
    hid = pl.BlockSpec((N_CHIPS, tm, fs), lambda i: (0, i, 0))
    vec = _const((1, d))
    return pl.pallas_call(
        body, name=f"bwd_ffn_dx{layer}", grid=(nt,),
        in_specs=[_rows(tm, d), _rows(tm, d), hid, hid, wcol, wcol, wrow, vec],
        out_specs=[_rows(tm, d), hid, hid, hid, _rows(tm, d), _acc_out((1, d)), _acc_out((1, d))],
        out_shape=[jax.ShapeDtypeStruct((t, d), BF16)] + [jax.ShapeDtypeStruct((N_CHIPS, t, fs), BF16)] * 3
        + [jax.ShapeDtypeStruct((t, d), F32)] + [jax.ShapeDtypeStruct((1, d), F32)] * 2,
        scratch_shapes=[pltpu.VMEM((8, d), F32)] * 2, compiler_params=_cparams(),
    )(dy, z, gg, uu, wg, wu, wd, lng)


def _matmul_tn(a, b, tt, name, layer=None, into=None):
    ja, t, ka = a.shape
    jb, _, nb = b.shape
    nj = max(ja, jb)

    def body(*refs):
        a_ref, b_ref, o_ref = refs[0], refs[1], refs[-1]

        @pl.when(pl.program_id(0) == 0)
        def _():
            o_ref[...] = jnp.zeros_like(o_ref)

        a0 = a_ref[0].astype(BF16) if ja == 1 else None
        b0 = b_ref[0].astype(BF16) if jb == 1 else None
        for j in range(nj):
            aj = a0 if ja == 1 else a_ref[j].astype(BF16)
            bj = b0 if jb == 1 else b_ref[j].astype(BF16)
            o_ref[j] += _dot_tn(aj, bj)

    in_specs = [pl.BlockSpec((ja, tt, ka), lambda i: (0, i, 0)), pl.BlockSpec((jb, tt, nb), lambda i: (0, i, 0))]
    operands = [a, b]
    if layer is None:
        out_spec = pl.BlockSpec((nj, ka, nb), lambda i: (0, 0, 0))
        out_shape = jax.ShapeDtypeStruct((nj, ka, nb), F32)
    else:
        out_spec = pl.BlockSpec((nj, None, ka, nb), lambda i: (0, layer, 0, 0))
        out_shape = jax.ShapeDtypeStruct((nj, DEPTH, ka, nb), F32)
    aliases = {}
    if into is not None:
        in_specs.append(pl.BlockSpec(memory_space=pl.ANY))
        operands.append(into)
        aliases = {2: 0}
    return pl.pallas_call(
        body, name=name, grid=(t // tt,), in_specs=in_specs, out_specs=out_spec, out_shape=out_shape,
        input_output_aliases=aliases, compiler_params=_cparams(1),
    )(*operands)


def _bwd_attn(dy, z, q, kv, sinks, wo, wq, wkv, mixg, tm):
    t, d = dy.shape
    nq = d // HEAD_DIM
    group = nq // N_KV_HEADS
    nb = tm // BLOCK
    nt = t // tm
    hb = tm // BLOCK

    def body(sink_ref, dy_ref, z_ref, q_ref, kv_ref, kvh_ref, wo_ref, wq_ref, wkv_ref, g_ref,
             dz_ref, dqkv_ref, dx_ref, dlg_ref, dlb_ref, dbo_ref, dbq_ref, dbkv_ref, dsink_ref,
             kvext, dkvext, do_scr, dq_scr, carry, acc_g, acc_b, acc_o, acc_q, acc_kv, acc_s):
        i = pl.program_id(0)
        ti = nt - 1 - i
        _acc_init(i, carry, acc_g, acc_b, acc_o, acc_q, acc_kv, acc_s)
        dyv = dy_ref[...]
        dz, zhat = _ln_bwd(dyv, z_ref[...], g_ref[...])
        acc_g[...] += _colsum8(dyv * zhat)
        acc_b[...] += _colsum8(dyv)
        acc_o[...] += _colsum8(dz)
        dzb = dz.astype(BF16)
        dz_ref[...] = dzb
        do_scr[...] = _dot_nt(dzb, wo_ref[...]).astype(BF16)
        kvext[0:BLOCK] = kvh_ref[...]
        kvext[BLOCK:BLOCK + tm] = kv_ref[...]
        dkvext[0:tm] = jnp.zeros((tm, 2 * KVD), F32)
        dkvext[tm:tm + BLOCK] = carry[...]
        delta, band, kj = _attn_bias(nq)

        def block(b, c):
            r0 = pl.multiple_of(b * BLOCK, BLOCK)
            first = jnp.logical_and(ti == 0, b == 0)
            valid = jnp.logical_and(band, jnp.logical_or(kj >= BLOCK, jnp.logical_not(first)))
            for kvh in range(N_KV_HEADS):
                kk = kvext[pl.ds(r0, 2 * BLOCK), kvh * HEAD_DIM:(kvh + 1) * HEAD_DIM]
                vv = kvext[pl.ds(r0, 2 * BLOCK), KVD + kvh * HEAD_DIM:KVD + (kvh + 1) * HEAD_DIM]
                dk = jnp.zeros((2 * BLOCK, HEAD_DIM), F32)
                dv = jnp.zeros((2 * BLOCK, HEAD_DIM), F32)
                for g in range(group):
                    h = kvh * group + g
                    cols = slice(h * HEAD_DIM, (h + 1) * HEAD_DIM)
                    qh = q_ref[pl.ds(r0, BLOCK), cols]
                    doh = do_scr[pl.ds(r0, BLOCK), cols]
                    p, p_sink = _softmax_with_sink(qh, kk, _slope(h, nq), delta, valid, sink_ref[h])
                    dp = _dot_nt(doh, vv)
                    rs = jnp.sum(p * dp, axis=-1, keepdims=True)
                    dsb = (p * (dp - rs) * (1.0 / math.sqrt(HEAD_DIM))).astype(BF16)
                    acc_s[:, h:h + 1] += -(p_sink * rs)
                    dq_scr[pl.ds(r0, BLOCK), cols] = _dot(dsb, kk)
                    dk = dk + _dot_tn(dsb, qh)
                    dv = dv + _dot_tn(p.astype(BF16), doh)
                dkvext[pl.ds(r0, 2 * BLOCK), kvh * HEAD_DIM:(kvh + 1) * HEAD_DIM] += dk
                dkvext[pl.ds(r0, 2 * BLOCK), KVD + kvh * HEAD_DIM:KVD + (kvh + 1) * HEAD_DIM] += dv
            return c

        lax.fori_loop(0, nb, block, 0)
        carry[...] = dkvext[0:BLOCK]
        dq = dq_scr[...]
        dkv = dkvext[BLOCK:BLOCK + tm]
        acc_q[...] += _colsum8(dq)
        acc_kv[...] += _colsum8(dkv)
        dqb = dq.astype(BF16)
        dkvb = dkv.astype(BF16)
        dqkv_ref[:, 0:d] = dqb
        dqkv_ref[:, d:d + 2 * KVD] = dkvb
        dx_ref[...] = ALPHA * dz + _dot_nt(dqb, wq_ref[...]) + _dot_nt(dkvb, wkv_ref[...])
        _write_sums(i, nt, [(dlg_ref, acc_g), (dlb_ref, acc_b), (dbo_ref, acc_o), (dbq_ref, acc_q),
                            (dbkv_ref, acc_kv), (dsink_ref, acc_s)])

    rev = lambda w: pl.BlockSpec((tm, w), lambda i: (nt - 1 - i, 0))
    vec = _const((1, d))
    return pl.pallas_call(
        body, name="bwd_attn", grid=(nt,),
        in_specs=[pl.BlockSpec(memory_space=pltpu.SMEM), rev(d), rev(d), rev(d), rev(2 * KVD),
                  pl.BlockSpec((BLOCK, 2 * KVD), lambda i: (jnp.maximum((nt - 1 - i) * hb - 1, 0), 0)),
                  _const((d, d)), _const((d, d)), _const((d, 2 * KVD)), vec],
        out_specs=[rev(d), rev(d + 2 * KVD), rev(d)] + [_acc_out((1, d))] * 4
        + [_acc_out((1, 2 * KVD)), _acc_out((1, nq))],
        out_shape=[jax.ShapeDtypeStruct((t, d), BF16), jax.ShapeDtypeStruct((t, d + 2 * KVD), BF16),
                   jax.ShapeDtypeStruct((t, d), F32)] + [jax.ShapeDtypeStruct((1, d), F32)] * 4
        + [jax.ShapeDtypeStruct((1, 2 * KVD), F32), jax.ShapeDtypeStruct((1, nq), F32)],
        scratch_shapes=[pltpu.VMEM((tm + BLOCK, 2 * KVD), BF16), pltpu.VMEM((tm + BLOCK, 2 * KVD), F32),
                        pltpu.VMEM((tm, d), BF16), pltpu.VMEM((tm, d), F32), pltpu.VMEM((BLOCK, 2 * KVD), F32),
                        pltpu.VMEM((8, d), F32), pltpu.VMEM((8, d), F32), pltpu.VMEM((8, d), F32),
                        pltpu.VMEM((8, d), F32), pltpu.VMEM((8, 2 * KVD), F32), pltpu.VMEM((BLOCK, nq), F32)],
        compiler_params=_cparams(),
    )(sinks, dy, z, q, kv, kv, wo, wq, wkv, mixg)


def _bwd_conv_head(dy, z, c, w2, mixg, lng, lnb, tm):
    t, d = dy.shape
    nt = t // tm

    def body(dy_ref, z_ref, c_ref, w2_ref, mg_ref, lg_ref, lb_ref,
             dz_ref, s_ref, dc_ref, dmg_ref, dmb_ref, db2_ref, dlg_ref, dlb_ref, a0, a1, a2, a3, a4):
        i = pl.program_id(0)
        _acc_init(i, a0, a1, a2, a3, a4)
        dyv = dy_ref[...]
        dz, zhat = _ln_bwd(dyv, z_ref[...], mg_ref[...])
        a0[...] += _colsum8(dyv * zhat)
        a1[...] += _colsum8(dyv)
        a2[...] += _colsum8(dz)
        dz_ref[...] = dz
        chat, rstd = _ln_stats(c_ref[...])
        n = chat * lg_ref[...] + lb_ref[...]
        act, dact = _silu_and_grad(n)
        s_ref[...] = act.astype(BF16)
        dn = _dot_nt(dz.astype(BF16), w2_ref[...]) * dact
        a3[...] += _colsum8(dn * chat)
        a4[...] += _colsum8(dn)
        dch = dn * lg_ref[...]
        m1 = jnp.mean(dch, axis=-1, keepdims=True)
        m2 = jnp.mean(dch * chat, axis=-1, keepdims=True)
        dc_ref[...] = rstd * (dch - m1 - chat * m2)
        _write_sums(i, nt, [(dmg_ref, a0), (dmb_ref, a1), (db2_ref, a2), (dlg_ref, a3), (dlb_ref, a4)])

    vec = _const((1, d))
    return pl.pallas_call(
        body, name="bwd_conv_head", grid=(nt,),
        in_specs=[_rows(tm, d), _rows(tm, d), _rows(tm, d), _const((d, d)), vec, vec, vec],
        out_specs=[_rows(tm, d), _rows(tm, d), _rows(tm, d)] + [_acc_out((1, d))] * 5,
        out_shape=[jax.ShapeDtypeStruct((t, d), F32), jax.ShapeDtypeStruct((t, d), BF16),
                   jax.ShapeDtypeStruct((t, d), F32)] + [jax.ShapeDtypeStruct((1, d), F32)] * 5,
        scratch_shapes=[pltpu.VMEM((8, d), F32)] * 5, compiler_params=_cparams(),
    )(dy, z, c, w2, mixg, lng, lnb)


def _bwd_conv_glu(dc, u, a, g, dz, wdw, w1s, tm):
    t, d = dc.shape
    dh_w = d // 2
    nt = t // tm
    hb = tm // CONV_HALO
    last_halo = t // CONV_HALO - 1

    def body(dc_ref, dcn_ref, u_ref, up_ref, a_ref, g_ref, dz_ref, w_ref, w1_ref,
             dx_ref, dh_ref, db1_ref, dbdw_ref, dw_ref, ext, sh, du_scr, acc_b1, acc_bdw, acc_w):
        i = pl.program_id(0)
        _acc_init(i, acc_b1, acc_bdw, acc_w)
        dcv = dc_ref[...]
        acc_bdw[...] += _colsum8(dcv)

        ext[0:tm] = dcv
        ext[tm:tm + CONV_HALO] = jnp.where(i == nt - 1, 0.0, dcn_ref[...])
        ext[tm + CONV_HALO:tm + CONV_HALO + 8] = jnp.zeros((8, d), F32)
        _fill_shifted(sh, ext)

        def du_chunk(r, carry):
            base = pl.multiple_of(r * CONV_CHUNK, CONV_CHUNK)
            acc = jnp.zeros((CONV_CHUNK, d), F32)
            for k in range(CONV_WIDTH):
                e = CONV_WIDTH - 1 - k
                acc = acc + w_ref[k:k + 1, :] * sh[e % 8, pl.ds(base + (e // 8) * 8, CONV_CHUNK), :]
            du_scr[pl.ds(base, CONV_CHUNK), :] = acc
            return carry

        lax.fori_loop(0, tm // CONV_CHUNK, du_chunk, 0)

        ext[0:CONV_HALO] = jnp.where(i == 0, 0.0, up_ref[...])
        ext[CONV_HALO:CONV_HALO + tm] = u_ref[...]
        _fill_shifted(sh, ext)
        for k in range(CONV_WIDTH):
            e = k + CONV_HALO - (CONV_WIDTH - 1)

            def dw_chunk(r, acc, e=e):
                base = pl.multiple_of(r * CONV_CHUNK, CONV_CHUNK)
                return acc + dc_ref[pl.ds(base, CONV_CHUNK), :] * sh[e % 8, pl.ds(base + (e // 8) * 8, CONV_CHUNK), :]

            acc = lax.fori_loop(0, tm // CONV_CHUNK, dw_chunk, jnp.zeros((CONV_CHUNK, d), F32))
            acc_w[k] += acc[0:8] + acc[8:16]

        du = du_scr[...]
        av = a_ref[...].astype(F32)
        sg = jax.nn.sigmoid(g_ref[...].astype(F32))
        da = du * sg
        dg = du * av * sg * (1.0 - sg)
        acc_b1[:, 0:d] += _colsum8(da)
        acc_b1[:, d:2 * d] += _colsum8(dg)
        dx = ALPHA * dz_ref[...]
        for j, part in enumerate([da[:, 0:dh_w], da[:, dh_w:d], dg[:, 0:dh_w], dg[:, dh_w:d]]):
            pb = part.astype(BF16)
            dh_ref[j] = pb
            dx = dx + _dot_nt(pb, w1_ref[j])
        dx_ref[...] = dx

        @pl.when(i == nt - 1)
        def _():
            db1_ref[...] = jnp.sum(acc_b1[...], axis=0, keepdims=True)
            dbdw_ref[...] = jnp.sum(acc_bdw[...], axis=0, keepdims=True)
            dw_ref[...] = jnp.sum(acc_w[...], axis=1)

    return pl.pallas_call(
        body, name="bwd_conv_glu", grid=(nt,),
        in_specs=[_rows(tm, d), pl.BlockSpec((CONV_HALO, d), lambda i: (jnp.minimum((i + 1) * hb, last_halo), 0)),
                  _rows(tm, d), pl.BlockSpec((CONV_HALO, d), lambda i: (jnp.maximum(i * hb - 1, 0), 0)),
                  _rows(tm, d), _rows(tm, d), _rows(tm, d), _const((CONV_HALO, d)), _const((4, d, dh_w))],
        out_specs=[_rows(tm, d), pl.BlockSpec((4, tm, dh_w), lambda i: (0, i, 0)), _acc_out((1, 2 * d)),
                   _acc_out((1, d)), _acc_out((CONV_HALO, d))],
        out_shape=[jax.ShapeDtypeStruct((t, d), F32), jax.ShapeDtypeStruct((4, t, dh_w), BF16),
                   jax.ShapeDtypeStruct((1, 2 * d), F32), jax.ShapeDtypeStruct((1, d), F32),
                   jax.ShapeDtypeStruct((CONV_HALO, d), F32)],
        scratch_shapes=[pltpu.VMEM((tm + CONV_HALO + 8, d), F32), pltpu.VMEM((8, tm + CONV_HALO, d), F32),
                        pltpu.VMEM((tm, d), F32), pltpu.VMEM((8, 2 * d), F32), pltpu.VMEM((8, d), F32),
                        pltpu.VMEM((CONV_HALO, 8, d), F32)],
        compiler_params=_cparams(),
    )(dc, dc, u, u, a, g, dz, wdw, w1s)


def _row_block(rows, target):
    best = rows
    for cand in range(8, min(rows, target) + 1, 8):
        if rows % cand == 0:
            best = cand
    return best


def _adamw(w, g, m, v, name):
    rows, lanes = w.shape
    br = _row_block(rows, 512) if rows % 8 == 0 else rows

    def body(w_ref, g_ref, m_ref, v_ref, d_ref, nm_ref, nv_ref):
        gv = g_ref[...]
        nm = ADAM_B1 * m_ref[...] + (1.0 - ADAM_B1) * gv
        nv = ADAM_B2 * v_ref[...] + (1.0 - ADAM_B2) * (gv * gv)
        m_hat = nm / (1.0 - ADAM_B1 ** ADAM_STEP)
        v_hat = nv / (1.0 - ADAM_B2 ** ADAM_STEP)
        d_ref[...] = -ADAM_LR * (m_hat / (jnp.sqrt(v_hat) + ADAM_EPS) + ADAM_WD * w_ref[...])
        nm_ref[...] = nm
        nv_ref[...] = nv

    spec = pl.BlockSpec((br, lanes), lambda i: (i, 0))
    return pl.pallas_call(
        body, name=name, grid=(rows // br,), in_specs=[spec] * 4, out_specs=[spec] * 3,
        out_shape=[jax.ShapeDtypeStruct((rows, lanes), F32)] * 3, compiler_params=_cparams(),
    )(w, g, m, v)


def _pad_to(v, n):
    return jnp.pad(v, (0, n - v.shape[0]))


def _round_up(n, m):
    return (n + m - 1) // m * m


def kernel(x, conv_w_pw1, conv_b_pw1, conv_w_dw, conv_b_dw, conv_ln_g, conv_ln_b, conv_w_pw2, conv_b_pw2, kv_w_k, kv_b_k, kv_w_v, kv_b_v, attn_w_q, attn_b_q, attn_sinks, attn_w_o, attn_b_o, ffn_w_gate, ffn_w_up, ffn_w_down, ln_mix_g, ln_mix_b, ln_ffn_g, ln_ffn_b, loss_target, m_conv_w_pw1, m_conv_b_pw1, m_conv_w_dw, m_conv_b_dw, m_conv_ln_g, m_conv_ln_b, m_conv_w_pw2, m_conv_b_pw2, m_kv_w_k, m_kv_b_k, m_kv_w_v, m_kv_b_v, m_attn_w_q, m_attn_b_q, m_attn_sinks, m_attn_w_o, m_attn_b_o, m_ffn_w_gate, m_ffn_w_up, m_ffn_w_down, m_ln_mix_g, m_ln_mix_b, m_ln_ffn_g, m_ln_ffn_b, v_conv_w_pw1, v_conv_b_pw1, v_conv_w_dw, v_conv_b_dw, v_conv_ln_g, v_conv_ln_b, v_conv_w_pw2, v_conv_b_pw2, v_kv_w_k, v_kv_b_k, v_kv_w_v, v_kv_b_v, v_attn_w_q, v_attn_b_q, v_attn_sinks, v_attn_w_o, v_attn_b_o, v_ffn_w_gate, v_ffn_w_up, v_ffn_w_down, v_ln_mix_g, v_ln_mix_b, v_ln_ffn_g, v_ln_ffn_b):
    args = dict(locals())
    w = {n: args[n] for n in WEIGHTS}
    mom = {n: args["m_" + n] for n in WEIGHTS}
    var = {n: args["v_" + n] for n in WEIGHTS}
    assert x.shape[0] == 1, "one sequence per device"
    t, d = x.shape[1], x.shape[2]
    dq = d // 4
    fs = ffn_w_gate.shape[-1]
    nq = d // HEAD_DIM
    x0 = x.reshape(t, d)
    target = loss_target.reshape(t, d)
    tm_big = min(512, t)
    tm_mid = min(256, t)
    tm_tn = min(1024, t)
    c_idx = lax.axis_index("c")

    def halves(v):
        return v.reshape(2, -1, v.shape[-1])

    small_sizes = [int(w[n].size) for n in SMALL_SHARDED]
    rs = _round_up(sum(small_sizes), 8 * 128) // 128
    spack = _pad_to(jnp.concatenate([w[n].reshape(-1) for n in SMALL_SHARDED]), rs * 128).reshape(rs, 128)
    *gathered, gs = _all_gather_weights([halves(w[n].astype(BF16)) for n in BIG], spack)
    gs = gs.reshape(N_CHIPS, rs * 128)
    full = {n: g.reshape((N_CHIPS,) + w[n].shape) for n, g in zip(BIG, gathered)}
    off = 0
    for n, size in zip(SMALL_SHARDED, small_sizes):
        full[n] = gs[:, off:off + size].reshape((N_CHIPS,) + w[n].shape)
        off += size
    w1s = full['conv_w_pw1'].reshape(N_CHIPS, d, d // 2)
    w2 = full['conv_w_pw2'].reshape(d, d)
    wkv = jnp.concatenate([full['kv_w_k'].reshape(d, KVD), full['kv_w_v'].reshape(d, KVD)], axis=1)
    wq = full['attn_w_q'].reshape(d, d)
    wo = full['attn_w_o'].reshape(d, d)
    wg, wu, wd = full['ffn_w_gate'], full['ffn_w_up'], full['ffn_w_down']
    b1 = full['conv_b_pw1'].reshape(1, 2 * d)
    wdw = jnp.pad(full['conv_w_dw'].reshape(N_CHIPS, CONV_WIDTH, dq).transpose(1, 0, 2).reshape(CONV_WIDTH, d),
                  ((0, CONV_HALO - CONV_WIDTH), (0, 0)))
    bdw = full['conv_b_dw'].reshape(1, d)
    clng = full['conv_ln_g'].reshape(1, d)
    clnb = full['conv_ln_b'].reshape(1, d)
    b2 = full['conv_b_pw2'].reshape(1, d)
    bkv = jnp.concatenate([kv_b_k, kv_b_v]).reshape(1, 2 * KVD)
    sinks = attn_sinks.reshape(nq)
    mixg = [ln_mix_g[l].reshape(1, d) for l in range(DEPTH)]
    mixb = [ln_mix_b[l].reshape(1, d) for l in range(DEPTH)]
    ffng = [ln_ffn_g[l].reshape(1, d) for l in range(DEPTH)]
    ffnb = [ln_ffn_b[l].reshape(1, d) for l in range(DEPTH)]

    a_act, g_act, u_act = _fwd_pw1_glu(x0, w1s, b1, tm_big)
    c_act, z1, x1 = _fwd_conv_tail(u_act, x0, wdw, bdw, clng, clnb, w2, b2, mixg[0], mixb[0], tm_mid)
    gg0, uu0, z2, x2 = _fwd_ffn(x1, wg, wu, wd, 0, ffng[0], ffnb[0], tm_big)
    q_act, kv_act, o_act, z3, x3 = _fwd_attn(x2, wq, attn_b_q, wkv, bkv, sinks, wo, attn_b_o, mixg[1], mixb[1], tm_big)
    gg1, uu1, z4, x4 = _fwd_ffn(x3, wg, wu, wd, 1, ffng[1], ffnb[1], tm_big)
    dx4, loss_part = _loss_grad(x4, target, tm_big)
    loss = lax.psum(loss_part[0, 0], ("x", "y", "c"))

    dz4, dgg1, duu1, hm1, dx3, d_fg1, d_fb1 = _bwd_ffn_dx(dx4, z4, gg1, uu1, wg, wu, wd, 1, ffng[1], tm_mid)
    dwg = _matmul_tn(x3[None], dgg1, tm_tn, "dw_gate1", layer=1)
    dwu = _matmul_tn(x3[None], duu1, tm_tn, "dw_up1", layer=1)
    dwd = _matmul_tn(hm1, dz4[None], tm_tn, "dw_down1", layer=1)
    (dz3, dqkv, dx2, d_mg1, d_mb1, d_bo, d_bq, d_bkv, d_sinks) = _bwd_attn(
        dx3, z3, q_act, kv_act, sinks, wo, wq, wkv, mixg[1], tm_big)
    dwo = _matmul_tn(o_act[None], dz3[None], tm_tn, "dw_o")
    dwqkv = _matmul_tn(x2[None], dqkv[None], tm_tn, "dw_qkv")[0]
    dz2, dgg0, duu0, hm0, dx1, d_fg0, d_fb0 = _bwd_ffn_dx(dx2, z2, gg0, uu0, wg, wu, wd, 0, ffng[0], tm_mid)
    dwg = _matmul_tn(x1[None], dgg0, tm_tn, "dw_gate0", layer=0, into=dwg)
    dwu = _matmul_tn(x1[None], duu0, tm_tn, "dw_up0", layer=0, into=dwu)
    dwd = _matmul_tn(hm0, dz2[None], tm_tn, "dw_down0", layer=0, into=dwd)
    dz1, s_act, dc, d_mg0, d_mb0, d_b2, d_clng, d_clnb = _bwd_conv_head(dx1, z1, c_act, w2, mixg[0], clng, clnb, tm_mid)
    dw2 = _matmul_tn(s_act[None], dz1[None], tm_tn, "dw_pw2")
    dx0, dh1, d_b1, d_bdw, d_wdw = _bwd_conv_glu(dc, u_act, a_act, g_act, dz1, wdw, w1s, tm_mid)
    dw1 = _matmul_tn(x0[None], dh1, tm_tn, "dw_pw1")

    def rows4(v):
        return v.reshape(N_CHIPS, -1)

    def rep4(v):
        return jnp.broadcast_to(v.reshape(1, -1), (N_CHIPS, v.size))

    big_local = {
        'conv_w_pw1': dw1, 'conv_w_pw2': dw2, 'kv_w_k': dwqkv[:, d:d + KVD], 'kv_w_v': dwqkv[:, d + KVD:d + 2 * KVD],
        'attn_w_q': dwqkv[:, 0:d], 'attn_w_o': dwo, 'ffn_w_gate': dwg, 'ffn_w_up': dwu, 'ffn_w_down': dwd}
    local = {
        'conv_b_pw1': rows4(d_b1),
        'conv_w_dw': rows4(d_wdw[0:CONV_WIDTH].reshape(CONV_WIDTH, N_CHIPS, dq).transpose(1, 0, 2)),
        'conv_b_dw': rows4(d_bdw), 'conv_ln_g': rows4(d_clng), 'conv_ln_b': rows4(d_clnb), 'conv_b_pw2': rows4(d_b2),
        'kv_b_k': rep4(d_bkv[:, 0:KVD]), 'kv_b_v': rep4(d_bkv[:, KVD:2 * KVD]), 'attn_b_q': rep4(d_bq),
        'attn_sinks': rep4(d_sinks), 'attn_b_o': rep4(d_bo),
        'ln_mix_g': rep4(jnp.concatenate([d_mg0, d_mg1])), 'ln_mix_b': rep4(jnp.concatenate([d_mb0, d_mb1])),
        'ln_ffn_g': rep4(jnp.concatenate([d_fg0, d_fg1])), 'ln_ffn_b': rep4(jnp.concatenate([d_fb0, d_fb1])),
    }
    n_small = sum(int(w[n].size) for n in SMALL)
    small_rows = _round_up(n_small, 2 * 8 * 128) // 128
    small_local = jnp.concatenate([local[n] for n in SMALL], axis=1)
    small_local = jnp.pad(small_local, ((0, 0), (0, small_rows * 128 - n_small)))
    names = BIG + ['small']
    plist = [big_local[n].reshape(N_CHIPS, 2, -1, w[n].shape[-1]) for n in BIG]
    plist.append(small_local.reshape(N_CHIPS, 2, small_rows // 2, 128))

    c_arr = c_idx.reshape(1).astype(jnp.int32)
    from_sibling = _pair_exchange(plist)
    pair = [_pair_sum(p, got, c_arr, "grad_pair_sum_" + n) for p, got, n in zip(plist, from_sibling, names)]
    from_chips = _chip_scatter(pair)
    mine = [_chip_sum(got, "grad_chip_sum_" + n) for got, n in zip(from_chips, names)]
    reduced = _pair_share(mine)

    g_out, delta, new_m, new_v = {}, {}, {}, {}
    for n, g in zip(BIG, reduced):
        shape = w[n].shape
        two_d = (-1, shape[-1])
        g_out[n] = g.reshape(shape)
        dl, nm, nv = _adamw(w[n].reshape(two_d), g.reshape(two_d), mom[n].reshape(two_d), var[n].reshape(two_d),
                            "adamw_" + n)
        delta[n], new_m[n], new_v[n] = dl.reshape(shape), nm.reshape(shape), nv.reshape(shape)

    def pack_small(tree):
        return _pad_to(jnp.concatenate([tree[n].reshape(-1) for n in SMALL]), small_rows * 128).reshape(small_rows, 128)

    g_small = reduced[-1].reshape(small_rows, 128)
    dl, nm, nv = _adamw(pack_small(w), g_small, pack_small(mom), pack_small(var), "adamw_small")
    off = 0
    for n in SMALL:
        size, shape = int(w[n].size), w[n].shape
        for tree, flat in ((g_out, g_small), (delta, dl), (new_m, nm), (new_v, nv)):
            tree[n] = flat.reshape(-1)[off:off + size].reshape(shape)
        off += size

    return (loss, dx0.reshape(x.shape), *[g_out[n] for n in WEIGHTS], *[delta[n] for n in WEIGHTS],
            *[new_m[n] for n in WEIGHTS], *[new_v[n] for n in WEIGHTS])
```

```python
import functools
import math

import jax
import jax.numpy as jnp
from jax import lax
from jax.experimental import pallas as pl
from jax.experimental.pallas import tpu as pltpu

F32 = jnp.float32
BF16 = jnp.bfloat16

DEPTH = 2
ALPHA = (2.0 * DEPTH) ** 0.25
LN_EPS = 1e-5
NEG_INF = -1e30
HEAD_DIM = 64
N_KV_HEADS = 2
KVD = N_KV_HEADS * HEAD_DIM
BLOCK = 128
CONV_WIDTH = 31
CONV_HALO = 32
ALIBI_MAX = 8.0
ADAM_LR, ADAM_B1, ADAM_B2, ADAM_EPS, ADAM_WD, ADAM_STEP = 0.001, 0.9, 0.999, 1e-08, 0.01, 10

N_CHIPS = 4
PACK_ROWS = 256
VMEM_LIMIT = 56 * 1024 * 1024
MESH = pl.DeviceIdType.MESH

NT_DIMS = (((1,), (1,)), ((), ()))
TN_DIMS = (((0,), (0,)), ((), ()))

WEIGHTS = ['conv_w_pw1', 'conv_b_pw1', 'conv_w_dw', 'conv_b_dw', 'conv_ln_g', 'conv_ln_b', 'conv_w_pw2', 'conv_b_pw2',
           'kv_w_k', 'kv_b_k', 'kv_w_v', 'kv_b_v', 'attn_w_q', 'attn_b_q', 'attn_sinks', 'attn_w_o', 'attn_b_o',
           'ffn_w_gate', 'ffn_w_up', 'ffn_w_down', 'ln_mix_g', 'ln_mix_b', 'ln_ffn_g', 'ln_ffn_b']
BIG = ['conv_w_pw1', 'conv_w_pw2', 'kv_w_k', 'kv_w_v', 'attn_w_q', 'attn_w_o', 'ffn_w_gate', 'ffn_w_up', 'ffn_w_down']
SMALL_SHARDED = ['conv_b_pw1', 'conv_w_dw', 'conv_b_dw', 'conv_ln_g', 'conv_ln_b', 'conv_b_pw2']
REPLICATED = ['kv_b_k', 'kv_b_v', 'attn_b_q', 'attn_sinks', 'attn_b_o', 'ln_mix_g', 'ln_mix_b', 'ln_ffn_g', 'ln_ffn_b']
SMALL = SMALL_SHARDED + REPLICATED


def _cparams(n_grid=1):
    return pltpu.CompilerParams(dimension_semantics=("arbitrary",) * n_grid, vmem_limit_bytes=VMEM_LIMIT)


def _rows(tm, width):
    return pl.BlockSpec((tm, width), lambda i: (i, 0))


def _const(shape):
    return pl.BlockSpec(shape, lambda *_: (0,) * len(shape), pipeline_mode=pl.Buffered(1))


def _acc_out(shape):
    return pl.BlockSpec(shape, lambda *_: (0,) * len(shape))


def _dot(a, b):
    return jnp.dot(a, b, preferred_element_type=F32)


def _dot_nt(a, b):
    return lax.dot_general(a, b, NT_DIMS, preferred_element_type=F32)


def _dot_tn(a, b):
    return lax.dot_general(a, b, TN_DIMS, preferred_element_type=F32)


def _colsum8(v):
    m, n = v.shape
    return jnp.sum(v.reshape(m // 8, 8, n), axis=0)


def _ln_stats(z):
    mu = jnp.mean(z, axis=-1, keepdims=True)
    zc = z - mu
    var = jnp.mean(zc * zc, axis=-1, keepdims=True)
    rstd = lax.rsqrt(var + LN_EPS)
    return zc * rstd, rstd


def _ln_fwd(z, g, b):
    zhat, _ = _ln_stats(z)
    return zhat * g + b


def _ln_bwd(dy, z, g):
    zhat, rstd = _ln_stats(z)
    dzh = dy * g
    m1 = jnp.mean(dzh, axis=-1, keepdims=True)
    m2 = jnp.mean(dzh * zhat, axis=-1, keepdims=True)
    return rstd * (dzh - m1 - zhat * m2), zhat


def _silu_and_grad(n):
    sg = jax.nn.sigmoid(n)
    return n * sg, sg * (1.0 + n * (1.0 - sg))


def _acc_init(i, *refs):
    @pl.when(i == 0)
    def _():
        for r in refs:
            r[...] = jnp.zeros_like(r)


def _mesh_pos():
    x, y, c = lax.axis_index("x"), lax.axis_index("y"), lax.axis_index("c")
    chips = [(1 - x, y), (x, 1 - y), (1 - x, 1 - y)]
    return x, y, c, chips


HBM_SPEC = pl.BlockSpec(memory_space=pltpu.HBM)


def _remote(src, dst, send_sems, recv_sems, k, to):
    return pltpu.make_async_remote_copy(src_ref=src, dst_ref=dst, send_sem=send_sems.at[k], recv_sem=recv_sems.at[k],
                                        device_id=to, device_id_type=MESH)


class _Rider:
    def __init__(self, operands, out_shapes, sem_shapes, start, finish, mid=None):
        self.operands, self.out_shapes, self.sem_shapes = list(operands), list(out_shapes), list(sem_shapes)
        self.start, self.finish, self.mid = start, finish, mid


def _split(refs, counts):
    parts, k = [], 0
    for n in counts:
        parts.append(refs[k:k + n])
        k += n
    return parts


def _rider_refs(riders, ins, outs, sems):
    return list(zip(riders, _split(ins, [len(r.operands) for r in riders]),
                    _split(outs, [len(r.out_shapes) for r in riders]),
                    _split(sems, [len(r.sem_shapes) for r in riders])))


def _tc_call(body, *, name, nt, in_specs, out_specs, out_shape, operands, scratch_shapes=(), riders=(), mid_frac=0.75):
    n_in, n_out, n_scr = len(in_specs), len(out_specs), len(scratch_shapes)
    r_ops = [o for r in riders for o in r.operands]
    r_outs = [o for r in riders for o in r.out_shapes]
    r_sems = [s for r in riders for s in r.sem_shapes]
    mid_step = min(max(int(nt * mid_frac), 0), nt - 1)

    def full(*refs):
        ins, r_in, outs, r_out, scr, r_sem = _split(refs, [n_in, len(r_ops), n_out, len(r_outs), n_scr, len(r_sems)])
        parts = _rider_refs(riders, r_in, r_out, r_sem)
        i = pl.program_id(0)

        @pl.when(i == 0)
        def _():
            for r, a, b, s in parts:
                r.start(a, b, s)

        body(*ins, *outs, *scr)

        @pl.when(i == mid_step)
        def _():
            for r, a, b, s in parts:
                if r.mid is not None:
                    r.mid(a, b, s)

        @pl.when(i == nt - 1)
        def _():
            for r, a, b, s in parts:
                r.finish(a, b, s)

    res = pl.pallas_call(
        full if riders else body, name=name, grid=(nt,), in_specs=list(in_specs) + [HBM_SPEC] * len(r_ops),
        out_specs=list(out_specs) + [HBM_SPEC] * len(r_outs), out_shape=list(out_shape) + r_outs,
        scratch_shapes=list(scratch_shapes) + r_sems, compiler_params=_cparams(),
    )(*operands, *r_ops)
    return res[:n_out], _split(res[n_out:], [len(r.out_shapes) for r in riders])


def _run_riders(riders, name):
    r_ops = [o for r in riders for o in r.operands]
    r_outs = [o for r in riders for o in r.out_shapes]
    r_sems = [s for r in riders for s in r.sem_shapes]

    def body(*refs):
        r_in, r_out, r_sem = _split(refs, [len(r_ops), len(r_outs), len(r_sems)])
        parts = _rider_refs(riders, r_in, r_out, r_sem)
        for r, a, b, s in parts:
            r.start(a, b, s)
        for r, a, b, s in parts:
            if r.mid is not None:
                r.mid(a, b, s)
        for r, a, b, s in parts:
            r.finish(a, b, s)

    res = pl.pallas_call(body, name=name, out_shape=tuple(r_outs), in_specs=[HBM_SPEC] * len(r_ops),
                         out_specs=(HBM_SPEC,) * len(r_outs), scratch_shapes=r_sems)(*r_ops)
    return _split(list(res), [len(r.out_shapes) for r in riders])


def _all_gather_rider(halves, spack=None):
    n = len(halves)
    n_small = 0 if spack is None else 1

    def copies(ins, outs, sems):
        send_sems, recv_sems, local_sems = sems
        x, y, c, chips = _mesh_pos()
        me = 2 * x + y
        here, sibling = (x, y, c), (x, y, 1 - c)
        rows = [2 * cx + cy for cx, cy in chips]

        def big(p, k, chip_row, half, to, src=None):
            dst = outs[p].at[chip_row, half]
            return _remote(dst if src is None else src, dst, send_sems, recv_sems, 6 * p + k, to)

        own = [pltpu.make_async_copy(ins[p], outs[p].at[me], local_sems.at[p]) for p in range(n + n_small)]
        first = [big(p, j, me, c, (cx, cy, c), src=ins[p].at[c]) for p in range(n) for j, (cx, cy) in enumerate(chips)]
        landed = [big(p, j, rows[j], c, here) for p in range(n) for j in range(3)]
        passed = [big(p, 3 + j, rows[j], c, sibling) for p in range(n) for j in range(3)]
        from_sibling = [big(p, 3 + j, rows[j], 1 - c, here) for p in range(n) for j in range(3)]
        if n_small:
            first = [_remote(ins[n], outs[n].at[me], send_sems, recv_sems, 6 * n + j, (cx, cy, c))
                     for j, (cx, cy) in enumerate(chips)] + first
            from_sibling += [_remote(ins[n], outs[n].at[rows[j]], send_sems, recv_sems, 6 * n + j, here)
                             for j in range(3)]
        return own, first, landed, passed, from_sibling

    def start(ins, outs, sems):
        own, first, _, _, _ = copies(ins, outs, sems)
        for cp in own + first:
            cp.start()

    def mid(ins, outs, sems):
        _, _, landed, passed, _ = copies(ins, outs, sems)
        for got, fwd in zip(landed, passed):
            got.wait_recv()
            fwd.start()

    def finish(ins, outs, sems):
        own, first, _, passed, from_sibling = copies(ins, outs, sems)
        for cp in from_sibling:
            cp.wait_recv()
        for cp in first + passed:
            cp.wait_send()
        for cp in own:
            cp.wait()

    operands = list(halves) + ([spack] if n_small else [])
    n_sem = 6 * n + 3 * n_small
    return _Rider(operands, [jax.ShapeDtypeStruct((N_CHIPS,) + o.shape, o.dtype) for o in operands],
                  [pltpu.SemaphoreType.DMA((n_sem,)), pltpu.SemaphoreType.DMA((n_sem,)),
                   pltpu.SemaphoreType.DMA((n + n_small,))], start, finish, mid)


def _pair_exchange_rider(plist):
    n = len(plist)

    def copies(ins, outs, sems):
        x, y, c, _ = _mesh_pos()
        return [_remote(ins[k].at[:, 1 - c], outs[k], sems[0], sems[1], k, (x, y, 1 - c)) for k in range(n)]

    def start(ins, outs, sems):
        for cp in copies(ins, outs, sems):
            cp.start()

    def finish(ins, outs, sems):
        for cp in copies(ins, outs, sems):
            cp.wait()

    return _Rider(plist, [jax.ShapeDtypeStruct((p.shape[0],) + p.shape[2:], p.dtype) for p in plist],
                  [pltpu.SemaphoreType.DMA((n,)), pltpu.SemaphoreType.DMA((n,))], start, finish)


def _pair_sum(p, got, c, name):
    n, _, r, l = p.shape
    br = _row_block(r, PACK_ROWS)

    def body(c_ref, p_ref, got_ref, out_ref):
        out_ref[...] = p_ref[...] + got_ref[...]

    return pl.pallas_call(
        body, name=name, out_shape=jax.ShapeDtypeStruct((n, r, l), F32),
        grid_spec=pltpu.PrefetchScalarGridSpec(
            num_scalar_prefetch=1, grid=(n, r // br),
            in_specs=[pl.BlockSpec((None, None, br, l), lambda j, i, c_ref: (j, c_ref[0], i, 0)),
                      pl.BlockSpec((None, br, l), lambda j, i, c_ref: (j, i, 0))],
            out_specs=pl.BlockSpec((None, br, l), lambda j, i, c_ref: (j, i, 0))),
        compiler_params=_cparams(2),
    )(c, p, got)


def _chip_scatter_rider(slist):
    n = len(slist)

    def copies(ins, outs, sems):
        send_sems, recv_sems, local_sems = sems
        x, y, c, chips = _mesh_pos()
        me = 2 * x + y
        own = [pltpu.make_async_copy(ins[k].at[me], outs[k].at[me], local_sems.at[k]) for k in range(n)]
        sends = [_remote(ins[k].at[2 * cx + cy], outs[k].at[me], send_sems, recv_sems, 3 * k + j, (cx, cy, c))
                 for k in range(n) for j, (cx, cy) in enumerate(chips)]
        arrivals = [_remote(ins[k].at[me], outs[k].at[2 * cx + cy], send_sems, recv_sems, 3 * k + j, (x, y, c))
                    for k in range(n) for j, (cx, cy) in enumerate(chips)]
        return own, sends, arrivals

    def start(ins, outs, sems):
        own, sends, _ = copies(ins, outs, sems)
        for cp in own + sends:
            cp.start()

    def finish(ins, outs, sems):
        own, sends, arrivals = copies(ins, outs, sems)
        for cp in arrivals:
            cp.wait_recv()
        for cp in sends:
            cp.wait_send()
        for cp in own:
            cp.wait()

    return _Rider(slist, [jax.ShapeDtypeStruct(s.shape, s.dtype) for s in slist],
                  [pltpu.SemaphoreType.DMA((3 * n,)), pltpu.SemaphoreType.DMA((3 * n,)), pltpu.SemaphoreType.DMA((n,))],
                  start, finish)


def _chip_sum(got, name):
    n, r, l = got.shape
    br = _row_block(r, PACK_ROWS)

    def body(got_ref, out_ref):
        out_ref[...] = ((got_ref[0] + got_ref[1]) + got_ref[2]) + got_ref[3]

    return pl.pallas_call(
        body, name=name, out_shape=jax.ShapeDtypeStruct((r, l), F32), grid=(r // br,),
        in_specs=[pl.BlockSpec((n, br, l), lambda i: (0, i, 0))],
        out_specs=pl.BlockSpec((br, l), lambda i: (i, 0)), compiler_params=_cparams(1),
    )(got)


def _pair_share_rider(flist):
    n = len(flist)

    def copies(ins, outs, sems):
        send_sems, recv_sems, local_sems = sems
        x, y, c, _ = _mesh_pos()
        own = [pltpu.make_async_copy(ins[k], outs[k].at[c], local_sems.at[k]) for k in range(n)]
        sends = [_remote(ins[k], outs[k].at[c], send_sems, recv_sems, k, (x, y, 1 - c)) for k in range(n)]
        arrivals = [_remote(ins[k], outs[k].at[1 - c], send_sems, recv_sems, k, (x, y, c)) for k in range(n)]
        return own, sends, arrivals

    def start(ins, outs, sems):
        own, sends, _ = copies(ins, outs, sems)
        for cp in own + sends:
            cp.start()

    def finish(ins, outs, sems):
        own, sends, arrivals = copies(ins, outs, sems)
        for cp in arrivals:
            cp.wait_recv()
        for cp in sends:
            cp.wait_send()
        for cp in own:
            cp.wait()

    return _Rider(flist, [jax.ShapeDtypeStruct((2,) + f.shape, f.dtype) for f in flist],
                  [pltpu.SemaphoreType.DMA((n,)), pltpu.SemaphoreType.DMA((n,)), pltpu.SemaphoreType.DMA((n,))],
                  start, finish)


def _fwd_pw1_glu(x, w1s, b1, tm):
    t, d = x.shape
    dh = d // 2

    def body(x_ref, w_ref, b_ref, a_ref, g_ref, u_ref):
        xb = x_ref[...].astype(BF16)
        for hh in range(2):
            cs = slice(hh * dh, (hh + 1) * dh)
            a = _dot(xb, w_ref[hh]) + b_ref[:, hh * dh:(hh + 1) * dh]
            g = _dot(xb, w_ref[2 + hh]) + b_ref[:, d + hh * dh:d + (hh + 1) * dh]
            a_ref[:, cs] = a.astype(BF16)
            g_ref[:, cs] = g.astype(BF16)
            u_ref[:, cs] = a * jax.nn.sigmoid(g)

    return pl.pallas_call(
        body, name="fwd_pw1_glu", grid=(t // tm,),
        in_specs=[_rows(tm, d), _const((4, d, dh)), _const((1, 2 * d))],
        out_specs=[_rows(tm, d)] * 3,
        out_shape=[jax.ShapeDtypeStruct((t, d), BF16), jax.ShapeDtypeStruct((t, d), BF16),
                   jax.ShapeDtypeStruct((t, d), F32)],
        compiler_params=_cparams(),
    )(x, w1s, b1)


def _fill_shifted(sh_ref, ext_ref):
    n = sh_ref.shape[1]
    for s in range(8):
        sh_ref[s] = ext_ref[pl.ds(s, n), :]


CONV_CHUNK = 16


def _fwd_conv_tail(u, x0, wdw, bdw, lng, lnb, w2, b2, mixg, mixb, tm, riders=()):
    t, d = u.shape
    hb = tm // CONV_HALO

    def body(u_ref, uh_ref, x_ref, w_ref, bdw_ref, lng_ref, lnb_ref, w2_ref, b2_ref, mg_ref, mb_ref,
             c_ref, z_ref, y_ref, ext, sh):
        i = pl.program_id(0)
        ext[0:CONV_HALO] = jnp.where(i == 0, 0.0, uh_ref[...])
        ext[CONV_HALO:CONV_HALO + tm] = u_ref[...]
        ext[CONV_HALO + tm:CONV_HALO + tm + 8] = jnp.zeros((8, d), F32)
        _fill_shifted(sh, ext)

        def chunk(r, carry):
            base = pl.multiple_of(r * CONV_CHUNK, CONV_CHUNK)
            acc = jnp.zeros((CONV_CHUNK, d), F32)
            for k in range(CONV_WIDTH):
                e = k + CONV_HALO - (CONV_WIDTH - 1)
                acc = acc + w_ref[k:k + 1, :] * sh[e % 8, pl.ds(base + (e // 8) * 8, CONV_CHUNK), :]
            c_ref[pl.ds(base, CONV_CHUNK), :] = acc + bdw_ref[...]
            return carry

        lax.fori_loop(0, tm // CONV_CHUNK, chunk, 0)
        n = _ln_fwd(c_ref[...], lng_ref[...], lnb_ref[...])
        s = n * jax.nn.sigmoid(n)
        m = _dot(s.astype(BF16), w2_ref[...]) + b2_ref[...]
        z = ALPHA * x_ref[...] + m
        z_ref[...] = z
        y_ref[...] = _ln_fwd(z, mg_ref[...], mb_ref[...])

    vec = _const((1, d))
    return _tc_call(
        body, name="fwd_conv_tail", nt=t // tm,
        in_specs=[_rows(tm, d), pl.BlockSpec((CONV_HALO, d), lambda i: (jnp.maximum(i * hb - 1, 0), 0)), _rows(tm, d),
                  _const((CONV_HALO, d)), vec, vec, vec, _const((d, d)), vec, vec, vec],
        out_specs=[_rows(tm, d)] * 3,
        out_shape=[jax.ShapeDtypeStruct((t, d), F32)] * 3,
        scratch_shapes=[pltpu.VMEM((tm + CONV_HALO + 8, d), F32), pltpu.VMEM((8, tm + CONV_HALO, d), F32)],
        operands=(u, u, x0, wdw, bdw, lng, lnb, w2, b2, mixg, mixb), riders=riders)


def _fwd_ffn(x, wg, wu, wd, layer, lng, lnb, tm):
    t, d = x.shape
    fs = wg.shape[-1]

    def body(x_ref, wg_ref, wu_ref, wd_ref, g_ref, b_ref, gg_ref, uu_ref, z_ref, y_ref):
        xv = x_ref[...]
        xb = xv.astype(BF16)
        f = jnp.zeros((tm, d), F32)
        for j in range(N_CHIPS):
            gj = _dot(xb, wg_ref[j])
            uj = _dot(xb, wu_ref[j])
            gg_ref[j] = gj.astype(BF16)
            uu_ref[j] = uj.astype(BF16)
            hm = gj * jax.nn.sigmoid(gj) * uj
            f = f + _dot(hm.astype(BF16), wd_ref[j])
        z = ALPHA * xv + f
        z_ref[...] = z
        y_ref[...] = _ln_fwd(z, g_ref[...], b_ref[...])

    wcol = pl.BlockSpec((N_CHIPS, None, d, fs), lambda i: (0, layer, 0, 0), pipeline_mode=pl.Buffered(1))
    wrow = pl.BlockSpec((N_CHIPS, None, fs, d), lambda i: (0, layer, 0, 0), pipeline_mode=pl.Buffered(1))
    hid = pl.BlockSpec((N_CHIPS, tm, fs), lambda i: (0, i, 0))
    return pl.pallas_call(
        body, name=f"fwd_ffn{layer}", grid=(t // tm,),
        in_specs=[_rows(tm, d), wcol, wcol, wrow, _const((1, d)), _const((1, d))],
        out_specs=[hid, hid, _rows(tm, d), _rows(tm, d)],
        out_shape=[jax.ShapeDtypeStruct((N_CHIPS, t, fs), BF16)] * 2 + [jax.ShapeDtypeStruct((t, d), F32)] * 2,
        compiler_params=_cparams(),
    )(x, wg, wu, wd, lng, lnb)


def _attn_bias(nq):
    qi = lax.broadcasted_iota(jnp.int32, (BLOCK, 2 * BLOCK), 0)
    kj = lax.broadcasted_iota(jnp.int32, (BLOCK, 2 * BLOCK), 1)
    delta = qi + BLOCK - kj
    band = jnp.logical_and(delta >= 0, delta < BLOCK)
    return delta.astype(F32), band, kj


def _slope(h, nq):
    return 2.0 ** (-ALIBI_MAX * (h + 1) / nq)


def _softmax_with_sink(qh, kk, slope, delta, valid, sink):
    s = _dot_nt(qh, kk) * (1.0 / math.sqrt(HEAD_DIM)) - slope * delta
    s = jnp.where(valid, s, NEG_INF)
    m = jnp.maximum(jnp.max(s, axis=-1, keepdims=True), sink)
    p = jnp.exp(s - m)
    e_sink = jnp.exp(sink - m)
    den = jnp.sum(p, axis=-1, keepdims=True) + e_sink
    inv = 1.0 / den
    return p * inv, e_sink * inv


def _fwd_attn(x, wq, bq, wkv, bkv, sinks, wo, bo, mixg, mixb, tm):
    t, d = x.shape
    nq = d // HEAD_DIM
    group = nq // N_KV_HEADS
    nb = tm // BLOCK

    def body(sink_ref, x_ref, xh_ref, wq_ref, bq_ref, wkv_ref, bkv_ref, wo_ref, bo_ref, mg_ref, mb_ref,
             q_ref, kv_ref, o_ref, z_ref, y_ref, kvext, o_scr):
        i = pl.program_id(0)
        xv = x_ref[...]
        xb = xv.astype(BF16)
        q_ref[...] = (_dot(xb, wq_ref[...]) + bq_ref[...]).astype(BF16)
        kvb = (_dot(xb, wkv_ref[...]) + bkv_ref[...]).astype(BF16)
        kv_ref[...] = kvb
        kvext[0:BLOCK] = (_dot(xh_ref[...].astype(BF16), wkv_ref[...]) + bkv_ref[...]).astype(BF16)
        kvext[BLOCK:BLOCK + tm] = kvb
        delta, band, kj = _attn_bias(nq)

        def block(b, carry):
            r0 = pl.multiple_of(b * BLOCK, BLOCK)
            first = jnp.logical_and(i == 0, b == 0)
            valid = jnp.logical_and(band, jnp.logical_or(kj >= BLOCK, jnp.logical_not(first)))
            for h in range(nq):
                kvh = h // group
                qh = q_ref[pl.ds(r0, BLOCK), h * HEAD_DIM:(h + 1) * HEAD_DIM]
                kk = kvext[pl.ds(r0, 2 * BLOCK), kvh * HEAD_DIM:(kvh + 1) * HEAD_DIM]
                vv = kvext[pl.ds(r0, 2 * BLOCK), KVD + kvh * HEAD_DIM:KVD + (kvh + 1) * HEAD_DIM]
                p, _ = _softmax_with_sink(qh, kk, _slope(h, nq), delta, valid, sink_ref[h])
                o_scr[pl.ds(r0, BLOCK), h * HEAD_DIM:(h + 1) * HEAD_DIM] = _dot(p.astype(BF16), vv)
            return carry

        lax.fori_loop(0, nb, block, 0)
        ob = o_scr[...].astype(BF16)
        o_ref[...] = ob
        z = ALPHA * xv + _dot(ob, wo_ref[...]) + bo_ref[...]
        z_ref[...] = z
        y_ref[...] = _ln_fwd(z, mg_ref[...], mb_ref[...])

    hb = tm // BLOCK
    vec = _const((1, d))
    return pl.pallas_call(
        body, name="fwd_attn", grid=(t // tm,),
        in_specs=[pl.BlockSpec(memory_space=pltpu.SMEM),
                  _rows(tm, d), pl.BlockSpec((BLOCK, d), lambda i: (jnp.maximum(i * hb - 1, 0), 0)),
                  _const((d, d)), vec, _const((d, 2 * KVD)), _const((1, 2 * KVD)), _const((d, d)), vec, vec, vec],
        out_specs=[_rows(tm, d), _rows(tm, 2 * KVD), _rows(tm, d), _rows(tm, d), _rows(tm, d)],
        out_shape=[jax.ShapeDtypeStruct((t, d), BF16), jax.ShapeDtypeStruct((t, 2 * KVD), BF16),
                   jax.ShapeDtypeStruct((t, d), BF16), jax.ShapeDtypeStruct((t, d), F32),
                   jax.ShapeDtypeStruct((t, d), F32)],
        scratch_shapes=[pltpu.VMEM((tm + BLOCK, 2 * KVD), BF16), pltpu.VMEM((tm, d), F32)],
        compiler_params=_cparams(),
    )(sinks, x, x, wq, bq, wkv, bkv, wo, bo, mixg, mixb)


def _loss_grad(y, target, tm):
    t, d = y.shape
    nt = t // tm

    def body(y_ref, t_ref, dy_ref, loss_ref, acc):
        i = pl.program_id(0)
        _acc_init(i, acc)
        e = y_ref[...] - t_ref[...]
        dy_ref[...] = e * (1.0 / d)
        acc[...] += _colsum8(e * e)

        @pl.when(i == nt - 1)
        def _():
            loss_ref[...] = jnp.sum(acc[...], keepdims=True) * (0.5 / d)

    return pl.pallas_call(
        body, name="loss_grad", grid=(nt,), in_specs=[_rows(tm, d), _rows(tm, d)],
        out_specs=[_rows(tm, d), pl.BlockSpec((1, 1), lambda i: (0, 0))],
        out_shape=[jax.ShapeDtypeStruct((t, d), F32), jax.ShapeDtypeStruct((1, 1), F32)],
        scratch_shapes=[pltpu.VMEM((8, d), F32)], compiler_params=_cparams(),
    )(y, target)


def _write_sums(i, nt, pairs):
    @pl.when(i == nt - 1)
    def _():
        for out_ref, acc in pairs:
            out_ref[...] = jnp.sum(acc[...], axis=0, keepdims=True)


def _bwd_ffn_dx(dy, z, gg, uu, wg, wu, wd, layer, lng, tm, riders=()):
    t, d = dy.shape
    fs = wg.shape[-1]
    nt = t // tm

    def body(dy_ref, z_ref, gg_ref, uu_ref, wg_ref, wu_ref, wd_ref, g_ref,
             dz_ref, dgg_ref, duu_ref, hm_ref, dx_ref, dlg_ref, dlb_ref, acc_g, acc_b):
        i = pl.program_id(0)
        _acc_init(i, acc_g, acc_b)
        dyv = dy_ref[...]
        dz, zhat = _ln_bwd(dyv, z_ref[...], g_ref[...])
        acc_g[...] += _colsum8(dyv * zhat)
        acc_b[...] += _colsum8(dyv)
        dzb = dz.astype(BF16)
        dz_ref[...] = dzb
        dx = ALPHA * dz
        for j in range(N_CHIPS):
            dh = _dot_nt(dzb, wd_ref[j])
            gj = gg_ref[j].astype(F32)
            uj = uu_ref[j].astype(F32)
            act, dact = _silu_and_grad(gj)
            hm_ref[j] = (act * uj).astype(BF16)
            dgb = (dh * uj * dact).astype(BF16)
            dub = (dh * act).astype(BF16)
            dgg_ref[j] = dgb
            duu_ref[j] = dub
            dx = dx + _dot_nt(dgb, wg_ref[j]) + _dot_nt(dub, wu_ref[j])
        dx_ref[...] = dx
        _write_sums(i, nt, [(dlg_ref, acc_g), (dlb_ref, acc_b)])

    wcol = pl.BlockSpec((N_CHIPS, None, d, fs), lambda i: (0, layer, 0, 0), pipeline_mode=pl.Buffered(1))
    wrow = pl.BlockSpec((N_CHIPS, None, fs, d), lambda i: (0, layer, 0, 0), pipeline_mode=pl.Buffered(1))
    hid = pl.BlockSpec((N_CHIPS, tm, fs), lambda i: (0, i, 0))
    vec = _const((1, d))
    return _tc_call(
        body, name=f"bwd_ffn_dx{layer}", nt=nt,
        in_specs=[_rows(tm, d), _rows(tm, d), hid, hid, wcol, wcol, wrow, vec],
        out_specs=[_rows(tm, d), hid, hid, hid, _rows(tm, d), _acc_out((1, d)), _acc_out((1, d))],
        out_shape=[jax.ShapeDtypeStruct((t, d), BF16)] + [jax.ShapeDtypeStruct((N_CHIPS, t, fs), BF16)] * 3
        + [jax.ShapeDtypeStruct((t, d), F32)] + [jax.ShapeDtypeStruct((1, d), F32)] * 2,
        scratch_shapes=[pltpu.VMEM((8, d), F32)] * 2, operands=(dy, z, gg, uu, wg, wu, wd, lng), riders=riders)


def _matmul_tn(a, b, tt, name):
    ja, t, ka = a.shape
    jb, _, nb = b.shape
    nj = max(ja, jb)

    def body(a_ref, b_ref, o_ref):
        @pl.when(pl.program_id(0) == 0)
        def _():
            o_ref[...] = jnp.zeros_like(o_ref)

        a0 = a_ref[0].astype(BF16) if ja == 1 else None
        b0 = b_ref[0].astype(BF16) if jb == 1 else None
        for j in range(nj):
            aj = a0 if ja == 1 else a_ref[j].astype(BF16)
            bj = b0 if jb == 1 else b_ref[j].astype(BF16)
            o_ref[j] += _dot_tn(aj, bj)

    return pl.pallas_call(
        body, name=name, grid=(t // tt,),
        in_specs=[pl.BlockSpec((ja, tt, ka), lambda i: (0, i, 0)), pl.BlockSpec((jb, tt, nb), lambda i: (0, i, 0))],
        out_specs=pl.BlockSpec((nj, ka, nb), lambda i: (0, 0, 0)), out_shape=jax.ShapeDtypeStruct((nj, ka, nb), F32),
        compiler_params=_cparams(1),
    )(a, b)


def _bwd_attn(dy, z, q, kv, sinks, wo, wq, wkv, mixg, tm, riders=()):
    t, d = dy.shape
    nq = d // HEAD_DIM
    group = nq // N_KV_HEADS
    nb = tm // BLOCK
    nt = t // tm
    hb = tm // BLOCK

    def body(sink_ref, dy_ref, z_ref, q_ref, kv_ref, kvh_ref, wo_ref, wq_ref, wkv_ref, g_ref,
             dz_ref, dqkv_ref, dx_ref, dlg_ref, dlb_ref, dbo_ref, dbq_ref, dbkv_ref, dsink_ref,
             kvext, dkvext, do_scr, dq_scr, carry, acc_g, acc_b, acc_o, acc_q, acc_kv, acc_s):
        i = pl.program_id(0)
        ti = nt - 1 - i
        _acc_init(i, carry, acc_g, acc_b, acc_o, acc_q, acc_kv, acc_s)
        dyv = dy_ref[...]
        dz, zhat = _ln_bwd(dyv, z_ref[...], g_ref[...])
        acc_g[...] += _colsum8(dyv * zhat)
        acc_b[...] += _colsum8(dyv)
        acc_o[...] += _colsum8(dz)
        dzb = dz.astype(BF16)
        dz_ref[...] = dzb
        do_scr[...] = _dot_nt(dzb, wo_ref[...]).astype(BF16)
        kvext[0:BLOCK] = kvh_ref[...]
        kvext[BLOCK:BLOCK + tm] = kv_ref[...]
        dkvext[0:tm] = jnp.zeros((tm, 2 * KVD), F32)
        dkvext[tm:tm + BLOCK] = carry[...]
        delta, band, kj = _attn_bias(nq)

        def block(b, c):
            r0 = pl.multiple_of(b * BLOCK, BLOCK)
            first = jnp.logical_and(ti == 0, b == 0)
            valid = jnp.logical_and(band, jnp.logical_or(kj >= BLOCK, jnp.logical_not(first)))
            for kvh in range(N_KV_HEADS):
                kk = kvext[pl.ds(r0, 2 * BLOCK), kvh * HEAD_DIM:(kvh + 1) * HEAD_DIM]
                vv = kvext[pl.ds(r0, 2 * BLOCK), KVD + kvh * HEAD_DIM:KVD + (kvh + 1) * HEAD_DIM]
                dk = jnp.zeros((2 * BLOCK, HEAD_DIM), F32)
                dv = jnp.zeros((2 * BLOCK, HEAD_DIM), F32)
                for g in range(group):
                    h = kvh * group + g
                    cols = slice(h * HEAD_DIM, (h + 1) * HEAD_DIM)
                    qh = q_ref[pl.ds(r0, BLOCK), cols]
                    doh = do_scr[pl.ds(r0, BLOCK), cols]
                    p, p_sink = _softmax_with_sink(qh, kk, _slope(h, nq), delta, valid, sink_ref[h])
                    dp = _dot_nt(doh, vv)
                    rs = jnp.sum(p * dp, axis=-1, keepdims=True)
                    dsb = (p * (dp - rs) * (1.0 / math.sqrt(HEAD_DIM))).astype(BF16)
                    acc_s[:, h:h + 1] += -(p_sink * rs)
                    dq_scr[pl.ds(r0, BLOCK), cols] = _dot(dsb, kk)
                    dk = dk + _dot_tn(dsb, qh)
                    dv = dv + _dot_tn(p.astype(BF16), doh)
                dkvext[pl.ds(r0, 2 * BLOCK), kvh * HEAD_DIM:(kvh + 1) * HEAD_DIM] += dk
                dkvext[pl.ds(r0, 2 * BLOCK), KVD + kvh * HEAD_DIM:KVD + (kvh + 1) * HEAD_DIM] += dv
            return c

        lax.fori_loop(0, nb, block, 0)
        carry[...] = dkvext[0:BLOCK]
        dq = dq_scr[...]
        dkv = dkvext[BLOCK:BLOCK + tm]
        acc_q[...] += _colsum8(dq)
        acc_kv[...] += _colsum8(dkv)
        dqb = dq.astype(BF16)
        dkvb = dkv.astype(BF16)
        dqkv_ref[:, 0:d] = dqb
        dqkv_ref[:, d:d + 2 * KVD] = dkvb
        dx_ref[...] = ALPHA * dz + _dot_nt(dqb, wq_ref[...]) + _dot_nt(dkvb, wkv_ref[...])
        _write_sums(i, nt, [(dlg_ref, acc_g), (dlb_ref, acc_b), (dbo_ref, acc_o), (dbq_ref, acc_q),
                            (dbkv_ref, acc_kv), (dsink_ref, acc_s)])

    rev = lambda w: pl.BlockSpec((tm, w), lambda i: (nt - 1 - i, 0))
    vec = _const((1, d))
    return _tc_call(
        body, name="bwd_attn", nt=nt,
        in_specs=[pl.BlockSpec(memory_space=pltpu.SMEM), rev(d), rev(d), rev(d), rev(2 * KVD),
                  pl.BlockSpec((BLOCK, 2 * KVD), lambda i: (jnp.maximum((nt - 1 - i) * hb - 1, 0), 0)),
                  _const((d, d)), _const((d, d)), _const((d, 2 * KVD)), vec],
        out_specs=[rev(d), rev(d + 2 * KVD), rev(d)] + [_acc_out((1, d))] * 4
        + [_acc_out((1, 2 * KVD)), _acc_out((1, nq))],
        out_shape=[jax.ShapeDtypeStruct((t, d), BF16), jax.ShapeDtypeStruct((t, d + 2 * KVD), BF16),
                   jax.ShapeDtypeStruct((t, d), F32)] + [jax.ShapeDtypeStruct((1, d), F32)] * 4
        + [jax.ShapeDtypeStruct((1, 2 * KVD), F32), jax.ShapeDtypeStruct((1, nq), F32)],
        scratch_shapes=[pltpu.VMEM((tm + BLOCK, 2 * KVD), BF16), pltpu.VMEM((tm + BLOCK, 2 * KVD), F32),
                        pltpu.VMEM((tm, d), BF16), pltpu.VMEM((tm, d), F32), pltpu.VMEM((BLOCK, 2 * KVD), F32),
                        pltpu.VMEM((8, d), F32), pltpu.VMEM((8, d), F32), pltpu.VMEM((8, d), F32),
                        pltpu.VMEM((8, d), F32), pltpu.VMEM((8, 2 * KVD), F32), pltpu.VMEM((BLOCK, nq), F32)],
        operands=(sinks, dy, z, q, kv, kv, wo, wq, wkv, mixg), riders=riders)


def _bwd_conv_head(dy, z, c, w2, mixg, lng, lnb, tm, riders=()):
    t, d = dy.shape
    nt = t // tm

    def body(dy_ref, z_ref, c_ref, w2_ref, mg_ref, lg_ref, lb_ref,
             dz_ref, s_ref, dc_ref, dmg_ref, dmb_ref, db2_ref, dlg_ref, dlb_ref, a0, a1, a2, a3, a4):
        i = pl.program_id(0)
        _acc_init(i, a0, a1, a2, a3, a4)
        dyv = dy_ref[...]
        dz, zhat = _ln_bwd(dyv, z_ref[...], mg_ref[...])
        a0[...] += _colsum8(dyv * zhat)
        a1[...] += _colsum8(dyv)
        a2[...] += _colsum8(dz)
        dz_ref[...] = dz
        chat, rstd = _ln_stats(c_ref[...])
        n = chat * lg_ref[...] + lb_ref[...]
        act, dact = _silu_and_grad(n)
        s_ref[...] = act.astype(BF16)
        dn = _dot_nt(dz.astype(BF16), w2_ref[...]) * dact
        a3[...] += _colsum8(dn * chat)
        a4[...] += _colsum8(dn)
        dch = dn * lg_ref[...]
        m1 = jnp.mean(dch, axis=-1, keepdims=True)
        m2 = jnp.mean(dch * chat, axis=-1, keepdims=True)
        dc_ref[...] = rstd * (dch - m1 - chat * m2)
        _write_sums(i, nt, [(dmg_ref, a0), (dmb_ref, a1), (db2_ref, a2), (dlg_ref, a3), (dlb_ref, a4)])

    vec = _const((1, d))
    return _tc_call(
        body, name="bwd_conv_head", nt=nt,
        in_specs=[_rows(tm, d), _rows(tm, d), _rows(tm, d), _const((d, d)), vec, vec, vec],
        out_specs=[_rows(tm, d), _rows(tm, d), _rows(tm, d)] + [_acc_out((1, d))] * 5,
        out_shape=[jax.ShapeDtypeStruct((t, d), F32), jax.ShapeDtypeStruct((t, d), BF16),
                   jax.ShapeDtypeStruct((t, d), F32)] + [jax.ShapeDtypeStruct((1, d), F32)] * 5,
        scratch_shapes=[pltpu.VMEM((8, d), F32)] * 5, operands=(dy, z, c, w2, mixg, lng, lnb), riders=riders)


def _bwd_conv_glu(dc, u, a, g, dz, wdw, w1s, tm, riders=()):
    t, d = dc.shape
    dh_w = d // 2
    nt = t // tm
    hb = tm // CONV_HALO
    last_halo = t // CONV_HALO - 1

    def body(dc_ref, dcn_ref, u_ref, up_ref, a_ref, g_ref, dz_ref, w_ref, w1_ref,
             dx_ref, dh_ref, db1_ref, dbdw_ref, dw_ref, ext, sh, du_scr, acc_b1, acc_bdw, acc_w):
        i = pl.program_id(0)
        _acc_init(i, acc_b1, acc_bdw, acc_w)
        dcv = dc_ref[...]
        acc_bdw[...] += _colsum8(dcv)

        ext[0:tm] = dcv
        ext[tm:tm + CONV_HALO] = jnp.where(i == nt - 1, 0.0, dcn_ref[...])
        ext[tm + CONV_HALO:tm + CONV_HALO + 8] = jnp.zeros((8, d), F32)
        _fill_shifted(sh, ext)

        def du_chunk(r, carry):
            base = pl.multiple_of(r * CONV_CHUNK, CONV_CHUNK)
            acc = jnp.zeros((CONV_CHUNK, d), F32)
            for k in range(CONV_WIDTH):
                e = CONV_WIDTH - 1 - k
                acc = acc + w_ref[k:k + 1, :] * sh[e % 8, pl.ds(base + (e // 8) * 8, CONV_CHUNK), :]
            du_scr[pl.ds(base, CONV_CHUNK), :] = acc
            return carry

        lax.fori_loop(0, tm // CONV_CHUNK, du_chunk, 0)

        ext[0:CONV_HALO] = jnp.where(i == 0, 0.0, up_ref[...])
        ext[CONV_HALO:CONV_HALO + tm] = u_ref[...]
        _fill_shifted(sh, ext)
        for k in range(CONV_WIDTH):
            e = k + CONV_HALO - (CONV_WIDTH - 1)

            def dw_chunk(r, acc, e=e):
                base = pl.multiple_of(r * CONV_CHUNK, CONV_CHUNK)
                return acc + dc_ref[pl.ds(base, CONV_CHUNK), :] * sh[e % 8, pl.ds(base + (e // 8) * 8, CONV_CHUNK), :]

            acc = lax.fori_loop(0, tm // CONV_CHUNK, dw_chunk, jnp.zeros((CONV_CHUNK, d), F32))
            acc_w[k] += acc[0:8] + acc[8:16]

        du = du_scr[...]
        av = a_ref[...].astype(F32)
        sg = jax.nn.sigmoid(g_ref[...].astype(F32))
        da = du * sg
        dg = du * av * sg * (1.0 - sg)
        acc_b1[:, 0:d] += _colsum8(da)
        acc_b1[:, d:2 * d] += _colsum8(dg)
        dx = ALPHA * dz_ref[...]
        for j, part in enumerate([da[:, 0:dh_w], da[:, dh_w:d], dg[:, 0:dh_w], dg[:, dh_w:d]]):
            pb = part.astype(BF16)
            dh_ref[j] = pb
            dx = dx + _dot_nt(pb, w1_ref[j])
        dx_ref[...] = dx

        @pl.when(i == nt - 1)
        def _():
            db1_ref[...] = jnp.sum(acc_b1[...], axis=0, keepdims=True)
            dbdw_ref[...] = jnp.sum(acc_bdw[...], axis=0, keepdims=True)
            dw_ref[...] = jnp.sum(acc_w[...], axis=1)

    return _tc_call(
        body, name="bwd_conv_glu", nt=nt,
        in_specs=[_rows(tm, d), pl.BlockSpec((CONV_HALO, d), lambda i: (jnp.minimum((i + 1) * hb, last_halo), 0)),
                  _rows(tm, d), pl.BlockSpec((CONV_HALO, d), lambda i: (jnp.maximum(i * hb - 1, 0), 0)),
                  _rows(tm, d), _rows(tm, d), _rows(tm, d), _const((CONV_HALO, d)), _const((4, d, dh_w))],
        out_specs=[_rows(tm, d), pl.BlockSpec((4, tm, dh_w), lambda i: (0, i, 0)), _acc_out((1, 2 * d)),
                   _acc_out((1, d)), _acc_out((CONV_HALO, d))],
        out_shape=[jax.ShapeDtypeStruct((t, d), F32), jax.ShapeDtypeStruct((4, t, dh_w), BF16),
                   jax.ShapeDtypeStruct((1, 2 * d), F32), jax.ShapeDtypeStruct((1, d), F32),
                   jax.ShapeDtypeStruct((CONV_HALO, d), F32)],
        scratch_shapes=[pltpu.VMEM((tm + CONV_HALO + 8, d), F32), pltpu.VMEM((8, tm + CONV_HALO, d), F32),
                        pltpu.VMEM((tm, d), F32), pltpu.VMEM((8, 2 * d), F32), pltpu.VMEM((8, d), F32),
                        pltpu.VMEM((CONV_HALO, 8, d), F32)],
        operands=(dc, dc, u, u, a, g, dz, wdw, w1s), riders=riders)


def _row_block(rows, target):
    best = rows
    for cand in range(8, min(rows, target) + 1, 8):
        if rows % cand == 0:
            best = cand
    return best


def _adamw(w, g, m, v, name):
    rows, lanes = w.shape
    br = _row_block(rows, 512) if rows % 8 == 0 else rows

    def body(w_ref, g_ref, m_ref, v_ref, d_ref, nm_ref, nv_ref):
        gv = g_ref[...]
        nm = ADAM_B1 * m_ref[...] + (1.0 - ADAM_B1) * gv
        nv = ADAM_B2 * v_ref[...] + (1.0 - ADAM_B2) * (gv * gv)
        m_hat = nm / (1.0 - ADAM_B1 ** ADAM_STEP)
        v_hat = nv / (1.0 - ADAM_B2 ** ADAM_STEP)
        d_ref[...] = -ADAM_LR * (m_hat / (jnp.sqrt(v_hat) + ADAM_EPS) + ADAM_WD * w_ref[...])
        nm_ref[...] = nm
        nv_ref[...] = nv

    spec = pl.BlockSpec((br, lanes), lambda i: (i, 0))
    return pl.pallas_call(
        body, name=name, grid=(rows // br,), in_specs=[spec] * 4, out_specs=[spec] * 3,
        out_shape=[jax.ShapeDtypeStruct((rows, lanes), F32)] * 3, compiler_params=_cparams(),
    )(w, g, m, v)


def _pad_to(v, n):
    return jnp.pad(v, (0, n - v.shape[0]))


def _round_up(n, m):
    return (n + m - 1) // m * m


def kernel(x, conv_w_pw1, conv_b_pw1, conv_w_dw, conv_b_dw, conv_ln_g, conv_ln_b, conv_w_pw2, conv_b_pw2, kv_w_k, kv_b_k, kv_w_v, kv_b_v, attn_w_q, attn_b_q, attn_sinks, attn_w_o, attn_b_o, ffn_w_gate, ffn_w_up, ffn_w_down, ln_mix_g, ln_mix_b, ln_ffn_g, ln_ffn_b, loss_target, m_conv_w_pw1, m_conv_b_pw1, m_conv_w_dw, m_conv_b_dw, m_conv_ln_g, m_conv_ln_b, m_conv_w_pw2, m_conv_b_pw2, m_kv_w_k, m_kv_b_k, m_kv_w_v, m_kv_b_v, m_attn_w_q, m_attn_b_q, m_attn_sinks, m_attn_w_o, m_attn_b_o, m_ffn_w_gate, m_ffn_w_up, m_ffn_w_down, m_ln_mix_g, m_ln_mix_b, m_ln_ffn_g, m_ln_ffn_b, v_conv_w_pw1, v_conv_b_pw1, v_conv_w_dw, v_conv_b_dw, v_conv_ln_g, v_conv_ln_b, v_conv_w_pw2, v_conv_b_pw2, v_kv_w_k, v_kv_b_k, v_kv_w_v, v_kv_b_v, v_attn_w_q, v_attn_b_q, v_attn_sinks, v_attn_w_o, v_attn_b_o, v_ffn_w_gate, v_ffn_w_up, v_ffn_w_down, v_ln_mix_g, v_ln_mix_b, v_ln_ffn_g, v_ln_ffn_b):
    args = dict(locals())
    w = {n: args[n] for n in WEIGHTS}
    mom = {n: args["m_" + n] for n in WEIGHTS}
    var = {n: args["v_" + n] for n in WEIGHTS}
    assert x.shape[0] == 1, "one sequence per device"
    t, d = x.shape[1], x.shape[2]
    dq = d // 4
    fs = ffn_w_gate.shape[-1]
    nq = d // HEAD_DIM
    x0 = x.reshape(t, d)
    target = loss_target.reshape(t, d)
    tm_big = min(512, t)
    tm_mid = min(256, t)
    tm_tn = min(1024, t)
    c_idx = lax.axis_index("c")

    def halves(v):
        return v.reshape(2, -1, v.shape[-1])

    small_sizes = [int(w[n].size) for n in SMALL_SHARDED]
    rs = _round_up(sum(small_sizes), 8 * 128) // 128
    spack = _pad_to(jnp.concatenate([w[n].reshape(-1) for n in SMALL_SHARDED]), rs * 128).reshape(rs, 128)
    conv_first = ['conv_w_pw1', 'conv_w_pw2']
    later = [n for n in BIG if n not in conv_first]
    (first_out,) = _run_riders([_all_gather_rider([halves(w[n].astype(BF16)) for n in conv_first], spack)],
                               "all_gather_conv")
    later_rider = _all_gather_rider([halves(w[n].astype(BF16)) for n in later])
    gs = first_out[-1].reshape(N_CHIPS, rs * 128)
    full = {n: g.reshape((N_CHIPS,) + w[n].shape) for n, g in zip(conv_first, first_out)}
    off = 0
    for n, size in zip(SMALL_SHARDED, small_sizes):
        full[n] = gs[:, off:off + size].reshape((N_CHIPS,) + w[n].shape)
        off += size
    w1s = full['conv_w_pw1'].reshape(N_CHIPS, d, d // 2)
    w2 = full['conv_w_pw2'].reshape(d, d)
    b1 = full['conv_b_pw1'].reshape(1, 2 * d)
    wdw = jnp.pad(full['conv_w_dw'].reshape(N_CHIPS, CONV_WIDTH, dq).transpose(1, 0, 2).reshape(CONV_WIDTH, d),
                  ((0, CONV_HALO - CONV_WIDTH), (0, 0)))
    bdw = full['conv_b_dw'].reshape(1, d)
    clng = full['conv_ln_g'].reshape(1, d)
    clnb = full['conv_ln_b'].reshape(1, d)
    b2 = full['conv_b_pw2'].reshape(1, d)
    bkv = jnp.concatenate([kv_b_k, kv_b_v]).reshape(1, 2 * KVD)
    sinks = attn_sinks.reshape(nq)
    mixg = [ln_mix_g[l].reshape(1, d) for l in range(DEPTH)]
    mixb = [ln_mix_b[l].reshape(1, d) for l in range(DEPTH)]
    ffng = [ln_ffn_g[l].reshape(1, d) for l in range(DEPTH)]
    ffnb = [ln_ffn_b[l].reshape(1, d) for l in range(DEPTH)]

    a_act, g_act, u_act = _fwd_pw1_glu(x0, w1s, b1, tm_big)
    (c_act, z1, x1), (later_out,) = _fwd_conv_tail(u_act, x0, wdw, bdw, clng, clnb, w2, b2, mixg[0], mixb[0], tm_mid,
                                                   riders=[later_rider])
    full.update({n: g.reshape((N_CHIPS,) + w[n].shape) for n, g in zip(later, later_out)})
    wkv = jnp.concatenate([full['kv_w_k'].reshape(d, KVD), full['kv_w_v'].reshape(d, KVD)], axis=1)
    wq = full['attn_w_q'].reshape(d, d)
    wo = full['attn_w_o'].reshape(d, d)
    wg, wu, wd = full['ffn_w_gate'], full['ffn_w_up'], full['ffn_w_down']
    gg0, uu0, z2, x2 = _fwd_ffn(x1, wg, wu, wd, 0, ffng[0], ffnb[0], tm_big)
    q_act, kv_act, o_act, z3, x3 = _fwd_attn(x2, wq, attn_b_q, wkv, bkv, sinks, wo, attn_b_o, mixg[1], mixb[1], tm_big)
    gg1, uu1, z4, x4 = _fwd_ffn(x3, wg, wu, wd, 1, ffng[1], ffnb[1], tm_big)
    dx4, loss_part = _loss_grad(x4, target, tm_big)
    loss = lax.psum(loss_part[0, 0], ("x", "y", "c"))

    c_arr = c_idx.reshape(1).astype(jnp.int32)

    def halves4(v):
        return v.reshape(N_CHIPS, 2, -1, v.shape[-1])

    def arrays(group):
        return [p for _, p in group]

    def pair_sums(group, got):
        return [_pair_sum(p, g, c_arr, "grad_pair_sum_" + n) for (n, p), g in zip(group, got)]

    def chip_sums(group, got):
        return [_chip_sum(g, "grad_chip_sum_" + n) for (n, _), g in zip(group, got)]

    (dz4, dgg1, duu1, hm1, dx3, d_fg1, d_fb1), _ = _bwd_ffn_dx(dx4, z4, gg1, uu1, wg, wu, wd, 1, ffng[1], tm_mid)
    g1 = [("ffn_w_gate1", halves4(_matmul_tn(x3[None], dgg1, tm_tn, "dw_gate1"))),
          ("ffn_w_up1", halves4(_matmul_tn(x3[None], duu1, tm_tn, "dw_up1"))),
          ("ffn_w_down1", halves4(_matmul_tn(hm1, dz4[None], tm_tn, "dw_down1")))]
    (dz3, dqkv, dx2, d_mg1, d_mb1, d_bo, d_bq, d_bkv, d_sinks), (got1,) = _bwd_attn(
        dx3, z3, q_act, kv_act, sinks, wo, wq, wkv, mixg[1], tm_big, riders=[_pair_exchange_rider(arrays(g1))])
    s1 = pair_sums(g1, got1)
    dwo = _matmul_tn(o_act[None], dz3[None], tm_tn, "dw_o")
    dwqkv = _matmul_tn(x2[None], dqkv[None], tm_tn, "dw_qkv")[0]
    g2 = [("attn_w_o", halves4(dwo)), ("attn_w_q", halves4(dwqkv[:, 0:d])),
          ("kv_w_k", halves4(dwqkv[:, d:d + KVD])), ("kv_w_v", halves4(dwqkv[:, d + KVD:d + 2 * KVD]))]
    (dz2, dgg0, duu0, hm0, dx1, d_fg0, d_fb0), (from_chips1, got2) = _bwd_ffn_dx(
        dx2, z2, gg0, uu0, wg, wu, wd, 0, ffng[0], tm_mid,
        riders=[_chip_scatter_rider(s1), _pair_exchange_rider(arrays(g2))])
    f1 = chip_sums(g1, from_chips1)
    s2 = pair_sums(g2, got2)
    g3 = [("ffn_w_gate0", halves4(_matmul_tn(x1[None], dgg0, tm_tn, "dw_gate0"))),
          ("ffn_w_up0", halves4(_matmul_tn(x1[None], duu0, tm_tn, "dw_up0"))),
          ("ffn_w_down0", halves4(_matmul_tn(hm0, dz2[None], tm_tn, "dw_down0")))]
    (dz1, s_act, dc, d_mg0, d_mb0, d_b2, d_clng, d_clnb), (got3, shared1) = _bwd_conv_head(
        dx1, z1, c_act, w2, mixg[0], clng, clnb, tm_mid,
        riders=[_pair_exchange_rider(arrays(g3)), _pair_share_rider(f1)])
    s3 = pair_sums(g3, got3)
    dw2 = _matmul_tn(s_act[None], dz1[None], tm_tn, "dw_pw2")
    (dx0, dh1, d_b1, d_bdw, d_wdw), (from_chips2, from_chips3) = _bwd_conv_glu(
        dc, u_act, a_act, g_act, dz1, wdw, w1s, tm_mid, riders=[_chip_scatter_rider(s2), _chip_scatter_rider(s3)])
    f2 = chip_sums(g2, from_chips2)
    f3 = chip_sums(g3, from_chips3)
    dw1 = _matmul_tn(x0[None], dh1, tm_tn, "dw_pw1")

    def rows4(v):
        return v.reshape(N_CHIPS, -1)

    def rep4(v):
        return jnp.broadcast_to(v.reshape(1, -1), (N_CHIPS, v.size))

    local = {
        'conv_b_pw1': rows4(d_b1),
        'conv_w_dw': rows4(d_wdw[0:CONV_WIDTH].reshape(CONV_WIDTH, N_CHIPS, dq).transpose(1, 0, 2)),
        'conv_b_dw': rows4(d_bdw), 'conv_ln_g': rows4(d_clng), 'conv_ln_b': rows4(d_clnb), 'conv_b_pw2': rows4(d_b2),
        'kv_b_k': rep4(d_bkv[:, 0:KVD]), 'kv_b_v': rep4(d_bkv[:, KVD:2 * KVD]), 'attn_b_q': rep4(d_bq),
        'attn_sinks': rep4(d_sinks), 'attn_b_o': rep4(d_bo),
        'ln_mix_g': rep4(jnp.concatenate([d_mg0, d_mg1])), 'ln_mix_b': rep4(jnp.concatenate([d_mb0, d_mb1])),
        'ln_ffn_g': rep4(jnp.concatenate([d_fg0, d_fg1])), 'ln_ffn_b': rep4(jnp.concatenate([d_fb0, d_fb1])),
    }
    n_small = sum(int(w[n].size) for n in SMALL)
    small_rows = _round_up(n_small, 2 * 8 * 128) // 128
    small_local = jnp.concatenate([local[n] for n in SMALL], axis=1)
    small_local = jnp.pad(small_local, ((0, 0), (0, small_rows * 128 - n_small)))
    g4 = [("conv_w_pw1", halves4(dw1)), ("conv_w_pw2", halves4(dw2)),
          ("small", small_local.reshape(N_CHIPS, 2, small_rows // 2, 128))]
    (got4,) = _run_riders([_pair_exchange_rider(arrays(g4))], "grad_pair_exchange_last")
    s4 = pair_sums(g4, got4)
    (from_chips4,) = _run_riders([_chip_scatter_rider(s4)], "grad_chip_scatter_last")
    f4 = chip_sums(g4, from_chips4)
    (shared_rest,) = _run_riders([_pair_share_rider(f2 + f3 + f4)], "grad_pair_share_last")
    reduced = dict(zip([n for n, _ in g1], shared1))
    reduced.update(zip([n for n, _ in g2 + g3 + g4], shared_rest))
    for n in ('ffn_w_gate', 'ffn_w_up', 'ffn_w_down'):
        reduced[n] = jnp.stack([reduced[n + str(layer)].reshape(w[n].shape[1:]) for layer in range(DEPTH)])

    g_out, delta, new_m, new_v = {}, {}, {}, {}
    for n in BIG:
        shape = w[n].shape
        two_d = (-1, shape[-1])
        g_out[n] = reduced[n].reshape(shape)
        dl, nm, nv = _adamw(w[n].reshape(two_d), g_out[n].reshape(two_d), mom[n].reshape(two_d), var[n].reshape(two_d),
                            "adamw_" + n)
        delta[n], new_m[n], new_v[n] = dl.reshape(shape), nm.reshape(shape), nv.reshape(shape)

    def pack_small(tree):
        return _pad_to(jnp.concatenate([tree[n].reshape(-1) for n in SMALL]), small_rows * 128).reshape(small_rows, 128)

    g_small = reduced['small'].reshape(small_rows, 128)
    dl, nm, nv = _adamw(pack_small(w), g_small, pack_small(mom), pack_small(var), "adamw_small")
    off = 0
    for n in SMALL:
        size, shape = int(w[n].size), w[n].shape
        for tree, flat in ((g_out, g_small), (delta, dl), (new_m, nm), (new_v, nv)):
            tree[n] = flat.reshape(-1)[off:off + size].reshape(shape)
        off += size

    return (loss, dx0.reshape(x.shape), *[g_out[n] for n in WEIGHTS], *[delta[n] for n in WEIGHTS],
            *[new_m[n] for n in WEIGHTS], *[new_v[n] for n in WEIGHTS])
```

```python
import functools
import math

import jax
import jax.numpy as jnp
from jax import lax
from jax.experimental import pallas as pl
from jax.experimental.pallas import tpu as pltpu

F32 = jnp.float32
BF16 = jnp.bfloat16

DEPTH = 2
ALPHA = (2.0 * DEPTH) ** 0.25
LN_EPS = 1e-5
NEG_INF = -1e30
HEAD_DIM = 64
N_KV_HEADS = 2
KVD = N_KV_HEADS * HEAD_DIM
BLOCK = 128
CONV_WIDTH = 31
CONV_HALO = 32
ALIBI_MAX = 8.0
ADAM_LR, ADAM_B1, ADAM_B2, ADAM_EPS, ADAM_WD, ADAM_STEP = 0.001, 0.9, 0.999, 1e-08, 0.01, 10

N_CHIPS = 4
PACK_ROWS = 256
VMEM_LIMIT = 56 * 1024 * 1024
MESH = pl.DeviceIdType.MESH

NT_DIMS = (((1,), (1,)), ((), ()))
TN_DIMS = (((0,), (0,)), ((), ()))

WEIGHTS = ['conv_w_pw1', 'conv_b_pw1', 'conv_w_dw', 'conv_b_dw', 'conv_ln_g', 'conv_ln_b', 'conv_w_pw2', 'conv_b_pw2',
           'kv_w_k', 'kv_b_k', 'kv_w_v', 'kv_b_v', 'attn_w_q', 'attn_b_q', 'attn_sinks', 'attn_w_o', 'attn_b_o',
           'ffn_w_gate', 'ffn_w_up', 'ffn_w_down', 'ln_mix_g', 'ln_mix_b', 'ln_ffn_g', 'ln_ffn_b']
BIG = ['conv_w_pw1', 'conv_w_pw2', 'kv_w_k', 'kv_w_v', 'attn_w_q', 'attn_w_o', 'ffn_w_gate', 'ffn_w_up', 'ffn_w_down']
SMALL_SHARDED = ['conv_b_pw1', 'conv_w_dw', 'conv_b_dw', 'conv_ln_g', 'conv_ln_b', 'conv_b_pw2']
REPLICATED = ['kv_b_k', 'kv_b_v', 'attn_b_q', 'attn_sinks', 'attn_b_o', 'ln_mix_g', 'ln_mix_b', 'ln_ffn_g', 'ln_ffn_b']
SMALL = SMALL_SHARDED + REPLICATED


def _cparams(n_grid=1):
    return pltpu.CompilerParams(dimension_semantics=("arbitrary",) * n_grid, vmem_limit_bytes=VMEM_LIMIT)


def _rows(tm, width):
    return pl.BlockSpec((tm, width), lambda i: (i, 0))


def _const(shape):
    return pl.BlockSpec(shape, lambda *_: (0,) * len(shape), pipeline_mode=pl.Buffered(1))


def _acc_out(shape):
    return pl.BlockSpec(shape, lambda *_: (0,) * len(shape))


def _dot(a, b):
    return jnp.dot(a, b, preferred_element_type=F32)


def _dot_nt(a, b):
    return lax.dot_general(a, b, NT_DIMS, preferred_element_type=F32)


def _dot_tn(a, b):
    return lax.dot_general(a, b, TN_DIMS, preferred_element_type=F32)


def _colsum8(v):
    m, n = v.shape
    return jnp.sum(v.reshape(m // 8, 8, n), axis=0)


def _ln_stats(z):
    mu = jnp.mean(z, axis=-1, keepdims=True)
    zc = z - mu
    var = jnp.mean(zc * zc, axis=-1, keepdims=True)
    rstd = lax.rsqrt(var + LN_EPS)
    return zc * rstd, rstd


def _ln_fwd(z, g, b):
    zhat, _ = _ln_stats(z)
    return zhat * g + b


def _ln_bwd(dy, z, g):
    zhat, rstd = _ln_stats(z)
    dzh = dy * g
    m1 = jnp.mean(dzh, axis=-1, keepdims=True)
    m2 = jnp.mean(dzh * zhat, axis=-1, keepdims=True)
    return rstd * (dzh - m1 - zhat * m2), zhat


def _silu_and_grad(n):
    sg = jax.nn.sigmoid(n)
    return n * sg, sg * (1.0 + n * (1.0 - sg))


def _acc_init(i, *refs):
    @pl.when(i == 0)
    def _():
        for r in refs:
            r[...] = jnp.zeros_like(r)


def _mesh_pos():
    x, y, c = lax.axis_index("x"), lax.axis_index("y"), lax.axis_index("c")
    chips = [(1 - x, y), (x, 1 - y), (1 - x, 1 - y)]
    return x, y, c, chips


HBM_SPEC = pl.BlockSpec(memory_space=pltpu.HBM)


def _remote(src, dst, send_sems, recv_sems, k, to):
    return pltpu.make_async_remote_copy(src_ref=src, dst_ref=dst, send_sem=send_sems.at[k], recv_sem=recv_sems.at[k],
                                        device_id=to, device_id_type=MESH)


class _Rider:
    def __init__(self, operands, out_shapes, sem_shapes, start, finish, mid=None, in_place=False):
        self.operands, self.out_shapes, self.sem_shapes = list(operands), list(out_shapes), list(sem_shapes)
        self.start, self.finish, self.mid = start, finish, mid
        self.in_place = in_place


def _rider_aliases(riders, first_in, first_out):
    aliases, k_in, k_out = {}, first_in, first_out
    for r in riders:
        if r.in_place:
            aliases.update({k_in + k: k_out + k for k in range(len(r.operands))})
        k_in += len(r.operands)
        k_out += len(r.out_shapes)
    return aliases


def _split(refs, counts):
    parts, k = [], 0
    for n in counts:
        parts.append(refs[k:k + n])
        k += n
    return parts


def _rider_refs(riders, ins, outs, sems):
    return list(zip(riders, _split(ins, [len(r.operands) for r in riders]),
                    _split(outs, [len(r.out_shapes) for r in riders]),
                    _split(sems, [len(r.sem_shapes) for r in riders])))


def _tc_call(body, *, name, nt, in_specs, out_specs, out_shape, operands, scratch_shapes=(), riders=(), mid_frac=0.75):
    n_in, n_out, n_scr = len(in_specs), len(out_specs), len(scratch_shapes)
    r_ops = [o for r in riders for o in r.operands]
    r_outs = [o for r in riders for o in r.out_shapes]
    r_sems = [s for r in riders for s in r.sem_shapes]
    mid_step = min(max(int(nt * mid_frac), 0), nt - 1)

    def full(*refs):
        ins, r_in, outs, r_out, scr, r_sem = _split(refs, [n_in, len(r_ops), n_out, len(r_outs), n_scr, len(r_sems)])
        parts = _rider_refs(riders, r_in, r_out, r_sem)
        i = pl.program_id(0)

        @pl.when(i == 0)
        def _():
            for r, a, b, s in parts:
                r.start(a, b, s)

        body(*ins, *outs, *scr)

        @pl.when(i == mid_step)
        def _():
            for r, a, b, s in parts:
                if r.mid is not None:
                    r.mid(a, b, s)

        @pl.when(i == nt - 1)
        def _():
            for r, a, b, s in parts:
                r.finish(a, b, s)

    res = pl.pallas_call(
        full if riders else body, name=name, grid=(nt,), in_specs=list(in_specs) + [HBM_SPEC] * len(r_ops),
        out_specs=list(out_specs) + [HBM_SPEC] * len(r_outs), out_shape=list(out_shape) + r_outs,
        scratch_shapes=list(scratch_shapes) + r_sems, input_output_aliases=_rider_aliases(riders, n_in, n_out),
        compiler_params=_cparams(),
    )(*operands, *r_ops)
    return res[:n_out], _split(res[n_out:], [len(r.out_shapes) for r in riders])


def _run_riders(riders, name):
    r_ops = [o for r in riders for o in r.operands]
    r_outs = [o for r in riders for o in r.out_shapes]
    r_sems = [s for r in riders for s in r.sem_shapes]

    def body(*refs):
        r_in, r_out, r_sem = _split(refs, [len(r_ops), len(r_outs), len(r_sems)])
        parts = _rider_refs(riders, r_in, r_out, r_sem)
        for r, a, b, s in parts:
            r.start(a, b, s)
        for r, a, b, s in parts:
            if r.mid is not None:
                r.mid(a, b, s)
        for r, a, b, s in parts:
            r.finish(a, b, s)

    res = pl.pallas_call(body, name=name, out_shape=tuple(r_outs), in_specs=[HBM_SPEC] * len(r_ops),
                         out_specs=(HBM_SPEC,) * len(r_outs), scratch_shapes=r_sems,
                         input_output_aliases=_rider_aliases(riders, 0, 0))(*r_ops)
    return _split(list(res), [len(r.out_shapes) for r in riders])


def _all_gather_rider(bufs, small=None):
    n = len(bufs)
    n_small = 0 if small is None else 1

    def copies(outs, sems):
        send_sems, recv_sems = sems
        x, y, c, chips = _mesh_pos()
        me = 2 * x + y
        here, sibling = (x, y, c), (x, y, 1 - c)
        rows = [2 * cx + cy for cx, cy in chips]

        def big(p, k, chip_row, half, to):
            piece = outs[p].at[chip_row, half]
            return _remote(piece, piece, send_sems, recv_sems, 6 * p + k, to)

        first = [big(p, j, me, c, (cx, cy, c)) for p in range(n) for j, (cx, cy) in enumerate(chips)]
        landed = [big(p, j, rows[j], c, here) for p in range(n) for j in range(3)]
        passed = [big(p, 3 + j, rows[j], c, sibling) for p in range(n) for j in range(3)]
        arrivals = [big(p, 3 + j, rows[j], 1 - c, here) for p in range(n) for j in range(3)]
        if n_small:
            first = [_remote(outs[n].at[me], outs[n].at[me], send_sems, recv_sems, 6 * n + j, (cx, cy, c))
                     for j, (cx, cy) in enumerate(chips)] + first
            arrivals += [_remote(outs[n].at[rows[j]], outs[n].at[rows[j]], send_sems, recv_sems, 6 * n + j, here)
                         for j in range(3)]
        return first, landed, passed, arrivals

    def start(ins, outs, sems):
        for cp in copies(outs, sems)[0]:
            cp.start()

    def mid(ins, outs, sems):
        _, landed, passed, _ = copies(outs, sems)
        for got, fwd in zip(landed, passed):
            got.wait_recv()
            fwd.start()

    def finish(ins, outs, sems):
        first, _, passed, arrivals = copies(outs, sems)
        for cp in arrivals:
            cp.wait_recv()
        for cp in first + passed:
            cp.wait_send()

    operands = list(bufs) + ([small] if n_small else [])
    n_sem = 6 * n + 3 * n_small
    return _Rider(operands, [jax.ShapeDtypeStruct(o.shape, o.dtype) for o in operands],
                  [pltpu.SemaphoreType.DMA((n_sem,)), pltpu.SemaphoreType.DMA((n_sem,))], start, finish, mid,
                  in_place=True)


def _pair_exchange_rider(plist):
    n = len(plist)

    def copies(ins, outs, sems):
        x, y, c, _ = _mesh_pos()
        return [_remote(ins[k].at[:, 1 - c], outs[k], sems[0], sems[1], k, (x, y, 1 - c)) for k in range(n)]

    def start(ins, outs, sems):
        for cp in copies(ins, outs, sems):
            cp.start()

    def finish(ins, outs, sems):
        for cp in copies(ins, outs, sems):
            cp.wait()

    return _Rider(plist, [jax.ShapeDtypeStruct((p.shape[0],) + p.shape[2:], p.dtype) for p in plist],
                  [pltpu.SemaphoreType.DMA((n,)), pltpu.SemaphoreType.DMA((n,))], start, finish)


def _pair_sum(p, got, c, name):
    n, _, r, l = p.shape
    br = _row_block(r, PACK_ROWS)

    def body(c_ref, p_ref, got_ref, out_ref):
        out_ref[...] = p_ref[...] + got_ref[...]

    return pl.pallas_call(
        body, name=name, out_shape=jax.ShapeDtypeStruct((n, r, l), F32),
        grid_spec=pltpu.PrefetchScalarGridSpec(
            num_scalar_prefetch=1, grid=(n, r // br),
            in_specs=[pl.BlockSpec((None, None, br, l), lambda j, i, c_ref: (j, c_ref[0], i, 0)),
                      pl.BlockSpec((None, br, l), lambda j, i, c_ref: (j, i, 0))],
            out_specs=pl.BlockSpec((None, br, l), lambda j, i, c_ref: (j, i, 0))),
        compiler_params=_cparams(2),
    )(c, p, got)


def _chip_scatter_rider(slist):
    n = len(slist)

    def copies(ins, outs, sems):
        send_sems, recv_sems = sems
        x, y, c, chips = _mesh_pos()
        sends = [_remote(ins[k].at[2 * cx + cy], outs[k].at[j], send_sems, recv_sems, 3 * k + j, (cx, cy, c))
                 for k in range(n) for j, (cx, cy) in enumerate(chips)]
        arrivals = [_remote(ins[k].at[0], outs[k].at[j], send_sems, recv_sems, 3 * k + j, (x, y, c))
                    for k in range(n) for j in range(3)]
        return sends, arrivals

    def start(ins, outs, sems):
        for cp in copies(ins, outs, sems)[0]:
            cp.start()

    def finish(ins, outs, sems):
        sends, arrivals = copies(ins, outs, sems)
        for cp in arrivals:
            cp.wait_recv()
        for cp in sends:
            cp.wait_send()

    return _Rider(slist, [jax.ShapeDtypeStruct((3,) + s.shape[1:], s.dtype) for s in slist],
                  [pltpu.SemaphoreType.DMA((3 * n,)), pltpu.SemaphoreType.DMA((3 * n,))], start, finish)


def _chip_sum(s, got, pos, name):
    _, r, l = s.shape
    br = _row_block(r, PACK_ROWS)

    def body(pos_ref, s_ref, got_ref, out_ref):
        me = pos_ref[0]
        total = None
        for chip in range(N_CHIPS):
            flip = jnp.bitwise_xor(me, chip)
            term = jnp.where(flip == 0, s_ref[...],
                             jnp.where(flip == 2, got_ref[0], jnp.where(flip == 1, got_ref[1], got_ref[2])))
            total = term if total is None else total + term
        out_ref[...] = total

    return pl.pallas_call(
        body, name=name, out_shape=jax.ShapeDtypeStruct((2, r, l), F32),
        grid_spec=pltpu.PrefetchScalarGridSpec(
            num_scalar_prefetch=1, grid=(r // br,),
            in_specs=[pl.BlockSpec((None, br, l), lambda i, pos_ref: (pos_ref[0], i, 0)),
                      pl.BlockSpec((3, br, l), lambda i, pos_ref: (0, i, 0))],
            out_specs=pl.BlockSpec((None, br, l), lambda i, pos_ref: (pos_ref[1], i, 0))),
        compiler_params=_cparams(1),
    )(pos, s, got)


def _pair_share_rider(flist):
    n = len(flist)

    def copies(outs, sems):
        x, y, c, _ = _mesh_pos()
        sends = [_remote(outs[k].at[c], outs[k].at[c], sems[0], sems[1], k, (x, y, 1 - c)) for k in range(n)]
        arrivals = [_remote(outs[k].at[1 - c], outs[k].at[1 - c], sems[0], sems[1], k, (x, y, c)) for k in range(n)]
        return sends, arrivals

    def start(ins, outs, sems):
        for cp in copies(outs, sems)[0]:
            cp.start()

    def finish(ins, outs, sems):
        sends, arrivals = copies(outs, sems)
        for cp in arrivals:
            cp.wait_recv()
        for cp in sends:
            cp.wait_send()

    return _Rider(flist, [jax.ShapeDtypeStruct(f.shape, f.dtype) for f in flist],
                  [pltpu.SemaphoreType.DMA((n,)), pltpu.SemaphoreType.DMA((n,))], start, finish, in_place=True)


def _fwd_pw1_glu(x, w1s, b1, tm):
    t, d = x.shape
    dh = d // 2

    def body(x_ref, w_ref, b_ref, a_ref, g_ref, u_ref):
        xb = x_ref[...].astype(BF16)
        for hh in range(2):
            cs = slice(hh * dh, (hh + 1) * dh)
            a = _dot(xb, w_ref[hh]) + b_ref[:, hh * dh:(hh + 1) * dh]
            g = _dot(xb, w_ref[2 + hh]) + b_ref[:, d + hh * dh:d + (hh + 1) * dh]
            a_ref[:, cs] = a.astype(BF16)
            g_ref[:, cs] = g.astype(BF16)
            u_ref[:, cs] = a * jax.nn.sigmoid(g)

    return pl.pallas_call(
        body, name="fwd_pw1_glu", grid=(t // tm,),
        in_specs=[_rows(tm, d), _const((4, d, dh)), _const((1, 2 * d))],
        out_specs=[_rows(tm, d)] * 3,
        out_shape=[jax.ShapeDtypeStruct((t, d), BF16), jax.ShapeDtypeStruct((t, d), BF16),
                   jax.ShapeDtypeStruct((t, d), F32)],
        compiler_params=_cparams(),
    )(x, w1s, b1)


def _fill_shifted(sh_ref, ext_ref):
    n = sh_ref.shape[1]
    for s in range(8):
        sh_ref[s] = ext_ref[pl.ds(s, n), :]


CONV_CHUNK = 16


def _fwd_conv_tail(u, x0, wdw, bdw, lng, lnb, w2, b2, mixg, mixb, tm, riders=()):
    t, d = u.shape
    hb = tm // CONV_HALO

    def body(u_ref, uh_ref, x_ref, w_ref, bdw_ref, lng_ref, lnb_ref, w2_ref, b2_ref, mg_ref, mb_ref,
             c_ref, z_ref, y_ref, ext, sh):
        i = pl.program_id(0)
        ext[0:CONV_HALO] = jnp.where(i == 0, 0.0, uh_ref[...])
        ext[CONV_HALO:CONV_HALO + tm] = u_ref[...]
        ext[CONV_HALO + tm:CONV_HALO + tm + 8] = jnp.zeros((8, d), F32)
        _fill_shifted(sh, ext)

        def chunk(r, carry):
            base = pl.multiple_of(r * CONV_CHUNK, CONV_CHUNK)
            acc = jnp.zeros((CONV_CHUNK, d), F32)
            for k in range(CONV_WIDTH):
                e = k + CONV_HALO - (CONV_WIDTH - 1)
                acc = acc + w_ref[k:k + 1, :] * sh[e % 8, pl.ds(base + (e // 8) * 8, CONV_CHUNK), :]
            c_ref[pl.ds(base, CONV_CHUNK), :] = acc + bdw_ref[...]
            return carry

        lax.fori_loop(0, tm // CONV_CHUNK, chunk, 0)
        n = _ln_fwd(c_ref[...], lng_ref[...], lnb_ref[...])
        s = n * jax.nn.sigmoid(n)
        m = _dot(s.astype(BF16), w2_ref[...]) + b2_ref[...]
        z = ALPHA * x_ref[...] + m
        z_ref[...] = z
        y_ref[...] = _ln_fwd(z, mg_ref[...], mb_ref[...])

    vec = _const((1, d))
    return _tc_call(
        body, name="fwd_conv_tail", nt=t // tm,
        in_specs=[_rows(tm, d), pl.BlockSpec((CONV_HALO, d), lambda i: (jnp.maximum(i * hb - 1, 0), 0)), _rows(tm, d),
                  _const((CONV_HALO, d)), vec, vec, vec, _const((d, d)), vec, vec, vec],
        out_specs=[_rows(tm, d)] * 3,
        out_shape=[jax.ShapeDtypeStruct((t, d), F32)] * 3,
        scratch_shapes=[pltpu.VMEM((tm + CONV_HALO + 8, d), F32), pltpu.VMEM((8, tm + CONV_HALO, d), F32)],
        operands=(u, u, x0, wdw, bdw, lng, lnb, w2, b2, mixg, mixb), riders=riders)


def _fwd_ffn(x, wg, wu, wd, layer, lng, lnb, tm):
    t, d = x.shape
    fs = wg.shape[-1]

    def body(x_ref, wg_ref, wu_ref, wd_ref, g_ref, b_ref, gg_ref, uu_ref, z_ref, y_ref):
        xv = x_ref[...]
        xb = xv.astype(BF16)
        f = jnp.zeros((tm, d), F32)
        for j in range(N_CHIPS):
            gj = _dot(xb, wg_ref[j])
            uj = _dot(xb, wu_ref[j])
            gg_ref[j] = gj.astype(BF16)
            uu_ref[j] = uj.astype(BF16)
            hm = gj * jax.nn.sigmoid(gj) * uj
            f = f + _dot(hm.astype(BF16), wd_ref[j])
        z = ALPHA * xv + f
        z_ref[...] = z
        y_ref[...] = _ln_fwd(z, g_ref[...], b_ref[...])

    wcol = pl.BlockSpec((N_CHIPS, None, d, fs), lambda i: (0, layer, 0, 0), pipeline_mode=pl.Buffered(1))
    wrow = pl.BlockSpec((N_CHIPS, None, fs, d), lambda i: (0, layer, 0, 0), pipeline_mode=pl.Buffered(1))
    hid = pl.BlockSpec((N_CHIPS, tm, fs), lambda i: (0, i, 0))
    return pl.pallas_call(
        body, name=f"fwd_ffn{layer}", grid=(t // tm,),
        in_specs=[_rows(tm, d), wcol, wcol, wrow, _const((1, d)), _const((1, d))],
        out_specs=[hid, hid, _rows(tm, d), _rows(tm, d)],
        out_shape=[jax.ShapeDtypeStruct((N_CHIPS, t, fs), BF16)] * 2 + [jax.ShapeDtypeStruct((t, d), F32)] * 2,
        compiler_params=_cparams(),
    )(x, wg, wu, wd, lng, lnb)


def _attn_bias(nq):
    qi = lax.broadcasted_iota(jnp.int32, (BLOCK, 2 * BLOCK), 0)
    kj = lax.broadcasted_iota(jnp.int32, (BLOCK, 2 * BLOCK), 1)
    delta = qi + BLOCK - kj
    band = jnp.logical_and(delta >= 0, delta < BLOCK)
    return delta.astype(F32), band, kj


def _slope(h, nq):
    return 2.0 ** (-ALIBI_MAX * (h + 1) / nq)


def _softmax_with_sink(qh, kk, slope, delta, valid, sink):
    s = _dot_nt(qh, kk) * (1.0 / math.sqrt(HEAD_DIM)) - slope * delta
    s = jnp.where(valid, s, NEG_INF)
    m = jnp.maximum(jnp.max(s, axis=-1, keepdims=True), sink)
    p = jnp.exp(s - m)
    e_sink = jnp.exp(sink - m)
    den = jnp.sum(p, axis=-1, keepdims=True) + e_sink
    inv = 1.0 / den
    return p * inv, e_sink * inv


def _fwd_attn(x, wq, bq, wkv, bkv, sinks, wo, bo, mixg, mixb, tm):
    t, d = x.shape
    nq = d // HEAD_DIM
    group = nq // N_KV_HEADS
    nb = tm // BLOCK

    def body(sink_ref, x_ref, xh_ref, wq_ref, bq_ref, wkv_ref, bkv_ref, wo_ref, bo_ref, mg_ref, mb_ref,
             q_ref, kv_ref, o_ref, z_ref, y_ref, kvext, o_scr):
        i = pl.program_id(0)
        xv = x_ref[...]
        xb = xv.astype(BF16)
        q_ref[...] = (_dot(xb, wq_ref[...]) + bq_ref[...]).astype(BF16)
        kvb = (_dot(xb, wkv_ref[...]) + bkv_ref[...]).astype(BF16)
        kv_ref[...] = kvb
        kvext[0:BLOCK] = (_dot(xh_ref[...].astype(BF16), wkv_ref[...]) + bkv_ref[...]).astype(BF16)
        kvext[BLOCK:BLOCK + tm] = kvb
        delta, band, kj = _attn_bias(nq)

        def block(b, carry):
            r0 = pl.multiple_of(b * BLOCK, BLOCK)
            first = jnp.logical_and(i == 0, b == 0)
            valid = jnp.logical_and(band, jnp.logical_or(kj >= BLOCK, jnp.logical_not(first)))
            for h in range(nq):
                kvh = h // group
                qh = q_ref[pl.ds(r0, BLOCK), h * HEAD_DIM:(h + 1) * HEAD_DIM]
                kk = kvext[pl.ds(r0, 2 * BLOCK), kvh * HEAD_DIM:(kvh + 1) * HEAD_DIM]
                vv = kvext[pl.ds(r0, 2 * BLOCK), KVD + kvh * HEAD_DIM:KVD + (kvh + 1) * HEAD_DIM]
                p, _ = _softmax_with_sink(qh, kk, _slope(h, nq), delta, valid, sink_ref[h])
                o_scr[pl.ds(r0, BLOCK), h * HEAD_DIM:(h + 1) * HEAD_DIM] = _dot(p.astype(BF16), vv)
            return carry

        lax.fori_loop(0, nb, block, 0)
        ob = o_scr[...].astype(BF16)
        o_ref[...] = ob
        z = ALPHA * xv + _dot(ob, wo_ref[...]) + bo_ref[...]
        z_ref[...] = z
        y_ref[...] = _ln_fwd(z, mg_ref[...], mb_ref[...])

    hb = tm // BLOCK
    vec = _const((1, d))
    return pl.pallas_call(
        body, name="fwd_attn", grid=(t // tm,),
        in_specs=[pl.BlockSpec(memory_space=pltpu.SMEM),
                  _rows(tm, d), pl.BlockSpec((BLOCK, d), lambda i: (jnp.maximum(i * hb - 1, 0), 0)),
                  _const((d, d)), vec, _const((d, 2 * KVD)), _const((1, 2 * KVD)), _const((d, d)), vec, vec, vec],
        out_specs=[_rows(tm, d), _rows(tm, 2 * KVD), _rows(tm, d), _rows(tm, d), _rows(tm, d)],
        out_shape=[jax.ShapeDtypeStruct((t, d), BF16), jax.ShapeDtypeStruct((t, 2 * KVD), BF16),
                   jax.ShapeDtypeStruct((t, d), BF16), jax.ShapeDtypeStruct((t, d), F32),
                   jax.ShapeDtypeStruct((t, d), F32)],
        scratch_shapes=[pltpu.VMEM((tm + BLOCK, 2 * KVD), BF16), pltpu.VMEM((tm, d), F32)],
        compiler_params=_cparams(),
    )(sinks, x, x, wq, bq, wkv, bkv, wo, bo, mixg, mixb)


def _loss_grad(y, target, tm):
    t, d = y.shape
    nt = t // tm

    def body(y_ref, t_ref, dy_ref, loss_ref, acc):
        i = pl.program_id(0)
        _acc_init(i, acc)
        e = y_ref[...] - t_ref[...]
        dy_ref[...] = e * (1.0 / d)
        acc[...] += _colsum8(e * e)

        @pl.when(i == nt - 1)
        def _():
            loss_ref[...] = jnp.sum(acc[...], keepdims=True) * (0.5 / d)

    return pl.pallas_call(
        body, name="loss_grad", grid=(nt,), in_specs=[_rows(tm, d), _rows(tm, d)],
        out_specs=[_rows(tm, d), pl.BlockSpec((1, 1), lambda i: (0, 0))],
        out_shape=[jax.ShapeDtypeStruct((t, d), F32), jax.ShapeDtypeStruct((1, 1), F32)],
        scratch_shapes=[pltpu.VMEM((8, d), F32)], compiler_params=_cparams(),
    )(y, target)


def _write_sums(i, nt, pairs):
    @pl.when(i == nt - 1)
    def _():
        for out_ref, acc in pairs:
            out_ref[...] = jnp.sum(acc[...], axis=0, keepdims=True)


def _bwd_ffn_dx(dy, z, gg, uu, wg, wu, wd, layer, lng, tm, riders=()):
    t, d = dy.shape
    fs = wg.shape[-1]
    nt = t // tm

    def body(dy_ref, z_ref, gg_ref, uu_ref, wg_ref, wu_ref, wd_ref, g_ref,
             dz_ref, dgg_ref, duu_ref, hm_ref, dx_ref, dlg_ref, dlb_ref, acc_g, acc_b):
        i = pl.program_id(0)
        _acc_init(i, acc_g, acc_b)
        dyv = dy_ref[...]
        dz, zhat = _ln_bwd(dyv, z_ref[...], g_ref[...])
        acc_g[...] += _colsum8(dyv * zhat)
        acc_b[...] += _colsum8(dyv)
        dzb = dz.astype(BF16)
        dz_ref[...] = dzb
        dx = ALPHA * dz
        for j in range(N_CHIPS):
            dh = _dot_nt(dzb, wd_ref[j])
            gj = gg_ref[j].astype(F32)
            uj = uu_ref[j].astype(F32)
            act, dact = _silu_and_grad(gj)
            hm_ref[j] = (act * uj).astype(BF16)
            dgb = (dh * uj * dact).astype(BF16)
            dub = (dh * act).astype(BF16)
            dgg_ref[j] = dgb
            duu_ref[j] = dub
            dx = dx + _dot_nt(dgb, wg_ref[j]) + _dot_nt(dub, wu_ref[j])
        dx_ref[...] = dx
        _write_sums(i, nt, [(dlg_ref, acc_g), (dlb_ref, acc_b)])

    wcol = pl.BlockSpec((N_CHIPS, None, d, fs), lambda i: (0, layer, 0, 0), pipeline_mode=pl.Buffered(1))
    wrow = pl.BlockSpec((N_CHIPS, None, fs, d), lambda i: (0, layer, 0, 0), pipeline_mode=pl.Buffered(1))
    hid = pl.BlockSpec((N_CHIPS, tm, fs), lambda i: (0, i, 0))
    vec = _const((1, d))
    return _tc_call(
        body, name=f"bwd_ffn_dx{layer}", nt=nt,
        in_specs=[_rows(tm, d), _rows(tm, d), hid, hid, wcol, wcol, wrow, vec],
        out_specs=[_rows(tm, d), hid, hid, hid, _rows(tm, d), _acc_out((1, d)), _acc_out((1, d))],
        out_shape=[jax.ShapeDtypeStruct((t, d), BF16)] + [jax.ShapeDtypeStruct((N_CHIPS, t, fs), BF16)] * 3
        + [jax.ShapeDtypeStruct((t, d), F32)] + [jax.ShapeDtypeStruct((1, d), F32)] * 2,
        scratch_shapes=[pltpu.VMEM((8, d), F32)] * 2, operands=(dy, z, gg, uu, wg, wu, wd, lng), riders=riders)


def _matmul_tn(a, b, tt, name):
    ja, t, ka = a.shape
    jb, _, nb = b.shape
    nj = max(ja, jb)

    def body(a_ref, b_ref, o_ref):
        @pl.when(pl.program_id(0) == 0)
        def _():
            o_ref[...] = jnp.zeros_like(o_ref)

        a0 = a_ref[0].astype(BF16) if ja == 1 else None
        b0 = b_ref[0].astype(BF16) if jb == 1 else None
        for j in range(nj):
            aj = a0 if ja == 1 else a_ref[j].astype(BF16)
            bj = b0 if jb == 1 else b_ref[j].astype(BF16)
            o_ref[j] += _dot_tn(aj, bj)

    return pl.pallas_call(
        body, name=name, grid=(t // tt,),
        in_specs=[pl.BlockSpec((ja, tt, ka), lambda i: (0, i, 0)), pl.BlockSpec((jb, tt, nb), lambda i: (0, i, 0))],
        out_specs=pl.BlockSpec((nj, ka, nb), lambda i: (0, 0, 0)), out_shape=jax.ShapeDtypeStruct((nj, ka, nb), F32),
        compiler_params=_cparams(1),
    )(a, b)


def _bwd_attn(dy, z, q, kv, sinks, wo, wq, wkv, mixg, tm, riders=()):
    t, d = dy.shape
    nq = d // HEAD_DIM
    group = nq // N_KV_HEADS
    nb = tm // BLOCK
    nt = t // tm
    hb = tm // BLOCK

    def body(sink_ref, dy_ref, z_ref, q_ref, kv_ref, kvh_ref, wo_ref, wq_ref, wkv_ref, g_ref,
             dz_ref, dqkv_ref, dx_ref, dlg_ref, dlb_ref, dbo_ref, dbq_ref, dbkv_ref, dsink_ref,
             kvext, dkvext, do_scr, dq_scr, carry, acc_g, acc_b, acc_o, acc_q, acc_kv, acc_s):
        i = pl.program_id(0)
        ti = nt - 1 - i
        _acc_init(i, carry, acc_g, acc_b, acc_o, acc_q, acc_kv, acc_s)
        dyv = dy_ref[...]
        dz, zhat = _ln_bwd(dyv, z_ref[...], g_ref[...])
        acc_g[...] += _colsum8(dyv * zhat)
        acc_b[...] += _colsum8(dyv)
        acc_o[...] += _colsum8(dz)
        dzb = dz.astype(BF16)
        dz_ref[...] = dzb
        do_scr[...] = _dot_nt(dzb, wo_ref[...]).astype(BF16)
        kvext[0:BLOCK] = kvh_ref[...]
        kvext[BLOCK:BLOCK + tm] = kv_ref[...]
        dkvext[0:tm] = jnp.zeros((tm, 2 * KVD), F32)
        dkvext[tm:tm + BLOCK] = carry[...]
        delta, band, kj = _attn_bias(nq)

        def block(b, c):
            r0 = pl.multiple_of(b * BLOCK, BLOCK)
            first = jnp.logical_and(ti == 0, b == 0)
            valid = jnp.logical_and(band, jnp.logical_or(kj >= BLOCK, jnp.logical_not(first)))
            for kvh in range(N_KV_HEADS):
                kk = kvext[pl.ds(r0, 2 * BLOCK), kvh * HEAD_DIM:(kvh + 1) * HEAD_DIM]
                vv = kvext[pl.ds(r0, 2 * BLOCK), KVD + kvh * HEAD_DIM:KVD + (kvh + 1) * HEAD_DIM]
                dk = jnp.zeros((2 * BLOCK, HEAD_DIM), F32)
                dv = jnp.zeros((2 * BLOCK, HEAD_DIM), F32)
                for g in range(group):
                    h = kvh * group + g
                    cols = slice(h * HEAD_DIM, (h + 1) * HEAD_DIM)
                    qh = q_ref[pl.ds(r0, BLOCK), cols]
                    doh = do_scr[pl.ds(r0, BLOCK), cols]
                    p, p_sink = _softmax_with_sink(qh, kk, _slope(h, nq), delta, valid, sink_ref[h])
                    dp = _dot_nt(doh, vv)
                    rs = jnp.sum(p * dp, axis=-1, keepdims=True)
                    dsb = (p * (dp - rs) * (1.0 / math.sqrt(HEAD_DIM))).astype(BF16)
                    acc_s[:, h:h + 1] += -(p_sink * rs)
                    dq_scr[pl.ds(r0, BLOCK), cols] = _dot(dsb, kk)
                    dk = dk + _dot_tn(dsb, qh)
                    dv = dv + _dot_tn(p.astype(BF16), doh)
                dkvext[pl.ds(r0, 2 * BLOCK), kvh * HEAD_DIM:(kvh + 1) * HEAD_DIM] += dk
                dkvext[pl.ds(r0, 2 * BLOCK), KVD + kvh * HEAD_DIM:KVD + (kvh + 1) * HEAD_DIM] += dv
            return c

        lax.fori_loop(0, nb, block, 0)
        carry[...] = dkvext[0:BLOCK]
        dq = dq_scr[...]
        dkv = dkvext[BLOCK:BLOCK + tm]
        acc_q[...] += _colsum8(dq)
        acc_kv[...] += _colsum8(dkv)
        dqb = dq.astype(BF16)
        dkvb = dkv.astype(BF16)
        dqkv_ref[:, 0:d] = dqb
        dqkv_ref[:, d:d + 2 * KVD] = dkvb
        dx_ref[...] = ALPHA * dz + _dot_nt(dqb, wq_ref[...]) + _dot_nt(dkvb, wkv_ref[...])
        _write_sums(i, nt, [(dlg_ref, acc_g), (dlb_ref, acc_b), (dbo_ref, acc_o), (dbq_ref, acc_q),
                            (dbkv_ref, acc_kv), (dsink_ref, acc_s)])

    rev = lambda w: pl.BlockSpec((tm, w), lambda i: (nt - 1 - i, 0))
    vec = _const((1, d))
    return _tc_call(
        body, name="bwd_attn", nt=nt,
        in_specs=[pl.BlockSpec(memory_space=pltpu.SMEM), rev(d), rev(d), rev(d), rev(2 * KVD),
                  pl.BlockSpec((BLOCK, 2 * KVD), lambda i: (jnp.maximum((nt - 1 - i) * hb - 1, 0), 0)),
                  _const((d, d)), _const((d, d)), _const((d, 2 * KVD)), vec],
        out_specs=[rev(d), rev(d + 2 * KVD), rev(d)] + [_acc_out((1, d))] * 4
        + [_acc_out((1, 2 * KVD)), _acc_out((1, nq))],
        out_shape=[jax.ShapeDtypeStruct((t, d), BF16), jax.ShapeDtypeStruct((t, d + 2 * KVD), BF16),
                   jax.ShapeDtypeStruct((t, d), F32)] + [jax.ShapeDtypeStruct((1, d), F32)] * 4
        + [jax.ShapeDtypeStruct((1, 2 * KVD), F32), jax.ShapeDtypeStruct((1, nq), F32)],
        scratch_shapes=[pltpu.VMEM((tm + BLOCK, 2 * KVD), BF16), pltpu.VMEM((tm + BLOCK, 2 * KVD), F32),
                        pltpu.VMEM((tm, d), BF16), pltpu.VMEM((tm, d), F32), pltpu.VMEM((BLOCK, 2 * KVD), F32),
                        pltpu.VMEM((8, d), F32), pltpu.VMEM((8, d), F32), pltpu.VMEM((8, d), F32),
                        pltpu.VMEM((8, d), F32), pltpu.VMEM((8, 2 * KVD), F32), pltpu.VMEM((BLOCK, nq), F32)],
        operands=(sinks, dy, z, q, kv, kv, wo, wq, wkv, mixg), riders=riders)


def _bwd_conv_head(dy, z, c, w2, mixg, lng, lnb, tm, riders=()):
    t, d = dy.shape
    nt = t // tm

    def body(dy_ref, z_ref, c_ref, w2_ref, mg_ref, lg_ref, lb_ref,
             dz_ref, s_ref, dc_ref, dmg_ref, dmb_ref, db2_ref, dlg_ref, dlb_ref, a0, a1, a2, a3, a4):
        i = pl.program_id(0)
        _acc_init(i, a0, a1, a2, a3, a4)
        dyv = dy_ref[...]
        dz, zhat = _ln_bwd(dyv, z_ref[...], mg_ref[...])
        a0[...] += _colsum8(dyv * zhat)
        a1[...] += _colsum8(dyv)
        a2[...] += _colsum8(dz)
        dz_ref[...] = dz
        chat, rstd = _ln_stats(c_ref[...])
        n = chat * lg_ref[...] + lb_ref[...]
        act, dact = _silu_and_grad(n)
        s_ref[...] = act.astype(BF16)
        dn = _dot_nt(dz.astype(BF16), w2_ref[...]) * dact
        a3[...] += _colsum8(dn * chat)
        a4[...] += _colsum8(dn)
        dch = dn * lg_ref[...]
        m1 = jnp.mean(dch, axis=-1, keepdims=True)
        m2 = jnp.mean(dch * chat, axis=-1, keepdims=True)
        dc_ref[...] = rstd * (dch - m1 - chat * m2)
        _write_sums(i, nt, [(dmg_ref, a0), (dmb_ref, a1), (db2_ref, a2), (dlg_ref, a3), (dlb_ref, a4)])

    vec = _const((1, d))
    return _tc_call(
        body, name="bwd_conv_head", nt=nt,
        in_specs=[_rows(tm, d), _rows(tm, d), _rows(tm, d), _const((d, d)), vec, vec, vec],
        out_specs=[_rows(tm, d), _rows(tm, d), _rows(tm, d)] + [_acc_out((1, d))] * 5,
        out_shape=[jax.ShapeDtypeStruct((t, d), F32), jax.ShapeDtypeStruct((t, d), BF16),
                   jax.ShapeDtypeStruct((t, d), F32)] + [jax.ShapeDtypeStruct((1, d), F32)] * 5,
        scratch_shapes=[pltpu.VMEM((8, d), F32)] * 5, operands=(dy, z, c, w2, mixg, lng, lnb), riders=riders)


def _bwd_conv_glu(dc, u, a, g, dz, wdw, w1s, tm, riders=()):
    t, d = dc.shape
    dh_w = d // 2
    nt = t // tm
    hb = tm // CONV_HALO
    last_halo = t // CONV_HALO - 1

    def body(dc_ref, dcn_ref, u_ref, up_ref, a_ref, g_ref, dz_ref, w_ref, w1_ref,
             dx_ref, dh_ref, db1_ref, dbdw_ref, dw_ref, ext, sh, du_scr, acc_b1, acc_bdw, acc_w):
        i = pl.program_id(0)
        _acc_init(i, acc_b1, acc_bdw, acc_w)
        dcv = dc_ref[...]
        acc_bdw[...] += _colsum8(dcv)

        ext[0:tm] = dcv
        ext[tm:tm + CONV_HALO] = jnp.where(i == nt - 1, 0.0, dcn_ref[...])
        ext[tm + CONV_HALO:tm + CONV_HALO + 8] = jnp.zeros((8, d), F32)
        _fill_shifted(sh, ext)

        def du_chunk(r, carry):
            base = pl.multiple_of(r * CONV_CHUNK, CONV_CHUNK)
            acc = jnp.zeros((CONV_CHUNK, d), F32)
            for k in range(CONV_WIDTH):
                e = CONV_WIDTH - 1 - k
                acc = acc + w_ref[k:k + 1, :] * sh[e % 8, pl.ds(base + (e // 8) * 8, CONV_CHUNK), :]
            du_scr[pl.ds(base, CONV_CHUNK), :] = acc
            return carry

        lax.fori_loop(0, tm // CONV_CHUNK, du_chunk, 0)

        ext[0:CONV_HALO] = jnp.where(i == 0, 0.0, up_ref[...])
        ext[CONV_HALO:CONV_HALO + tm] = u_ref[...]
        _fill_shifted(sh, ext)
        for k in range(CONV_WIDTH):
            e = k + CONV_HALO - (CONV_WIDTH - 1)

            def dw_chunk(r, acc, e=e):
                base = pl.multiple_of(r * CONV_CHUNK, CONV_CHUNK)
                return acc + dc_ref[pl.ds(base, CONV_CHUNK), :] * sh[e % 8, pl.ds(base + (e // 8) * 8, CONV_CHUNK), :]

            acc = lax.fori_loop(0, tm // CONV_CHUNK, dw_chunk, jnp.zeros((CONV_CHUNK, d), F32))
            acc_w[k] += acc[0:8] + acc[8:16]

        du = du_scr[...]
        av = a_ref[...].astype(F32)
        sg = jax.nn.sigmoid(g_ref[...].astype(F32))
        da = du * sg
        dg = du * av * sg * (1.0 - sg)
        acc_b1[:, 0:d] += _colsum8(da)
        acc_b1[:, d:2 * d] += _colsum8(dg)
        dx = ALPHA * dz_ref[...]
        for j, part in enumerate([da[:, 0:dh_w], da[:, dh_w:d], dg[:, 0:dh_w], dg[:, dh_w:d]]):
            pb = part.astype(BF16)
            dh_ref[j] = pb
            dx = dx + _dot_nt(pb, w1_ref[j])
        dx_ref[...] = dx

        @pl.when(i == nt - 1)
        def _():
            db1_ref[...] = jnp.sum(acc_b1[...], axis=0, keepdims=True)
            dbdw_ref[...] = jnp.sum(acc_bdw[...], axis=0, keepdims=True)
            dw_ref[...] = jnp.sum(acc_w[...], axis=1)

    return _tc_call(
        body, name="bwd_conv_glu", nt=nt,
        in_specs=[_rows(tm, d), pl.BlockSpec((CONV_HALO, d), lambda i: (jnp.minimum((i + 1) * hb, last_halo), 0)),
                  _rows(tm, d), pl.BlockSpec((CONV_HALO, d), lambda i: (jnp.maximum(i * hb - 1, 0), 0)),
                  _rows(tm, d), _rows(tm, d), _rows(tm, d), _const((CONV_HALO, d)), _const((4, d, dh_w))],
        out_specs=[_rows(tm, d), pl.BlockSpec((4, tm, dh_w), lambda i: (0, i, 0)), _acc_out((1, 2 * d)),
                   _acc_out((1, d)), _acc_out((CONV_HALO, d))],
        out_shape=[jax.ShapeDtypeStruct((t, d), F32), jax.ShapeDtypeStruct((4, t, dh_w), BF16),
                   jax.ShapeDtypeStruct((1, 2 * d), F32), jax.ShapeDtypeStruct((1, d), F32),
                   jax.ShapeDtypeStruct((CONV_HALO, d), F32)],
        scratch_shapes=[pltpu.VMEM((tm + CONV_HALO + 8, d), F32), pltpu.VMEM((8, tm + CONV_HALO, d), F32),
                        pltpu.VMEM((tm, d), F32), pltpu.VMEM((8, 2 * d), F32), pltpu.VMEM((8, d), F32),
                        pltpu.VMEM((CONV_HALO, 8, d), F32)],
        operands=(dc, dc, u, u, a, g, dz, wdw, w1s), riders=riders)


def _row_block(rows, target):
    best = rows
    for cand in range(8, min(rows, target) + 1, 8):
        if rows % cand == 0:
            best = cand
    return best


def _adamw(w, g, m, v, name):
    rows, lanes = w.shape
    br = _row_block(rows, 512) if rows % 8 == 0 else rows

    def body(w_ref, g_ref, m_ref, v_ref, d_ref, nm_ref, nv_ref):
        gv = g_ref[...]
        nm = ADAM_B1 * m_ref[...] + (1.0 - ADAM_B1) * gv
        nv = ADAM_B2 * v_ref[...] + (1.0 - ADAM_B2) * (gv * gv)
        m_hat = nm / (1.0 - ADAM_B1 ** ADAM_STEP)
        v_hat = nv / (1.0 - ADAM_B2 ** ADAM_STEP)
        d_ref[...] = -ADAM_LR * (m_hat / (jnp.sqrt(v_hat) + ADAM_EPS) + ADAM_WD * w_ref[...])
        nm_ref[...] = nm
        nv_ref[...] = nv

    spec = pl.BlockSpec((br, lanes), lambda i: (i, 0))
    return pl.pallas_call(
        body, name=name, grid=(rows // br,), in_specs=[spec] * 4, out_specs=[spec] * 3,
        out_shape=[jax.ShapeDtypeStruct((rows, lanes), F32)] * 3, compiler_params=_cparams(),
    )(w, g, m, v)


def _pad_to(v, n):
    return jnp.pad(v, (0, n - v.shape[0]))


def _round_up(n, m):
    return (n + m - 1) // m * m


def kernel(x, conv_w_pw1, conv_b_pw1, conv_w_dw, conv_b_dw, conv_ln_g, conv_ln_b, conv_w_pw2, conv_b_pw2, kv_w_k, kv_b_k, kv_w_v, kv_b_v, attn_w_q, attn_b_q, attn_sinks, attn_w_o, attn_b_o, ffn_w_gate, ffn_w_up, ffn_w_down, ln_mix_g, ln_mix_b, ln_ffn_g, ln_ffn_b, loss_target, m_conv_w_pw1, m_conv_b_pw1, m_conv_w_dw, m_conv_b_dw, m_conv_ln_g, m_conv_ln_b, m_conv_w_pw2, m_conv_b_pw2, m_kv_w_k, m_kv_b_k, m_kv_w_v, m_kv_b_v, m_attn_w_q, m_attn_b_q, m_attn_sinks, m_attn_w_o, m_attn_b_o, m_ffn_w_gate, m_ffn_w_up, m_ffn_w_down, m_ln_mix_g, m_ln_mix_b, m_ln_ffn_g, m_ln_ffn_b, v_conv_w_pw1, v_conv_b_pw1, v_conv_w_dw, v_conv_b_dw, v_conv_ln_g, v_conv_ln_b, v_conv_w_pw2, v_conv_b_pw2, v_kv_w_k, v_kv_b_k, v_kv_w_v, v_kv_b_v, v_attn_w_q, v_attn_b_q, v_attn_sinks, v_attn_w_o, v_attn_b_o, v_ffn_w_gate, v_ffn_w_up, v_ffn_w_down, v_ln_mix_g, v_ln_mix_b, v_ln_ffn_g, v_ln_ffn_b):
    args = dict(locals())
    w = {n: args[n] for n in WEIGHTS}
    mom = {n: args["m_" + n] for n in WEIGHTS}
    var = {n: args["v_" + n] for n in WEIGHTS}
    assert x.shape[0] == 1, "one sequence per device"
    t, d = x.shape[1], x.shape[2]
    dq = d // 4
    fs = ffn_w_gate.shape[-1]
    nq = d // HEAD_DIM
    x0 = x.reshape(t, d)
    target = loss_target.reshape(t, d)
    tm_big = min(512, t)
    tm_mid = min(256, t)
    tm_tn = min(1024, t)
    c_idx = lax.axis_index("c")

    me_idx = 2 * lax.axis_index("x") + lax.axis_index("y")

    def gather_buffer(v):
        buf = lax.empty((N_CHIPS,) + v.shape, v.dtype)
        return lax.dynamic_update_slice(buf, v[None], (me_idx,) + (0,) * v.ndim)

    def halves(v):
        return v.reshape(2, -1, v.shape[-1])

    small_sizes = [int(w[n].size) for n in SMALL_SHARDED]
    rs = _round_up(sum(small_sizes), 8 * 128) // 128
    spack = _pad_to(jnp.concatenate([w[n].reshape(-1) for n in SMALL_SHARDED]), rs * 128).reshape(rs, 128)
    conv_first = ['conv_w_pw1', 'conv_w_pw2']
    later = [n for n in BIG if n not in conv_first]
    (first_out,) = _run_riders(
        [_all_gather_rider([gather_buffer(halves(w[n].astype(BF16))) for n in conv_first], gather_buffer(spack))],
        "all_gather_conv")
    later_rider = _all_gather_rider([gather_buffer(halves(w[n].astype(BF16))) for n in later])
    gs = first_out[-1].reshape(N_CHIPS, rs * 128)
    full = {n: g.reshape((N_CHIPS,) + w[n].shape) for n, g in zip(conv_first, first_out)}
    off = 0
    for n, size in zip(SMALL_SHARDED, small_sizes):
        full[n] = gs[:, off:off + size].reshape((N_CHIPS,) + w[n].shape)
        off += size
    w1s = full['conv_w_pw1'].reshape(N_CHIPS, d, d // 2)
    w2 = full['conv_w_pw2'].reshape(d, d)
    b1 = full['conv_b_pw1'].reshape(1, 2 * d)
    wdw = jnp.pad(full['conv_w_dw'].reshape(N_CHIPS, CONV_WIDTH, dq).transpose(1, 0, 2).reshape(CONV_WIDTH, d),
                  ((0, CONV_HALO - CONV_WIDTH), (0, 0)))
    bdw = full['conv_b_dw'].reshape(1, d)
    clng = full['conv_ln_g'].reshape(1, d)
    clnb = full['conv_ln_b'].reshape(1, d)
    b2 = full['conv_b_pw2'].reshape(1, d)
    bkv = jnp.concatenate([kv_b_k, kv_b_v]).reshape(1, 2 * KVD)
    sinks = attn_sinks.reshape(nq)
    mixg = [ln_mix_g[l].reshape(1, d) for l in range(DEPTH)]
    mixb = [ln_mix_b[l].reshape(1, d) for l in range(DEPTH)]
    ffng = [ln_ffn_g[l].reshape(1, d) for l in range(DEPTH)]
    ffnb = [ln_ffn_b[l].reshape(1, d) for l in range(DEPTH)]

    a_act, g_act, u_act = _fwd_pw1_glu(x0, w1s, b1, tm_big)
    (c_act, z1, x1), (later_out,) = _fwd_conv_tail(u_act, x0, wdw, bdw, clng, clnb, w2, b2, mixg[0], mixb[0], tm_mid,
                                                   riders=[later_rider])
    full.update({n: g.reshape((N_CHIPS,) + w[n].shape) for n, g in zip(later, later_out)})
    wkv = jnp.concatenate([full['kv_w_k'].reshape(d, KVD), full['kv_w_v'].reshape(d, KVD)], axis=1)
    wq = full['attn_w_q'].reshape(d, d)
    wo = full['attn_w_o'].reshape(d, d)
    wg, wu, wd = full['ffn_w_gate'], full['ffn_w_up'], full['ffn_w_down']
    gg0, uu0, z2, x2 = _fwd_ffn(x1, wg, wu, wd, 0, ffng[0], ffnb[0], tm_big)
    q_act, kv_act, o_act, z3, x3 = _fwd_attn(x2, wq, attn_b_q, wkv, bkv, sinks, wo, attn_b_o, mixg[1], mixb[1], tm_big)
    gg1, uu1, z4, x4 = _fwd_ffn(x3, wg, wu, wd, 1, ffng[1], ffnb[1], tm_big)
    dx4, loss_part = _loss_grad(x4, target, tm_big)
    loss = lax.psum(loss_part[0, 0], ("x", "y", "c"))

    c_arr = c_idx.reshape(1).astype(jnp.int32)

    def halves4(v):
        return v.reshape(N_CHIPS, 2, -1, v.shape[-1])

    def arrays(group):
        return [p for _, p in group]

    def pair_sums(group, got):
        return [_pair_sum(p, g, c_arr, "grad_pair_sum_" + n) for (n, p), g in zip(group, got)]

    pos_arr = jnp.stack([me_idx, c_idx]).astype(jnp.int32)

    def chip_sums(group, sums, got):
        return [_chip_sum(s, g, pos_arr, "grad_chip_sum_" + n) for (n, _), s, g in zip(group, sums, got)]

    (dz4, dgg1, duu1, hm1, dx3, d_fg1, d_fb1), _ = _bwd_ffn_dx(dx4, z4, gg1, uu1, wg, wu, wd, 1, ffng[1], tm_mid)
    g1 = [("ffn_w_gate1", halves4(_matmul_tn(x3[None], dgg1, tm_tn, "dw_gate1"))),
          ("ffn_w_up1", halves4(_matmul_tn(x3[None], duu1, tm_tn, "dw_up1"))),
          ("ffn_w_down1", halves4(_matmul_tn(hm1, dz4[None], tm_tn, "dw_down1")))]
    (dz3, dqkv, dx2, d_mg1, d_mb1, d_bo, d_bq, d_bkv, d_sinks), (got1,) = _bwd_attn(
        dx3, z3, q_act, kv_act, sinks, wo, wq, wkv, mixg[1], tm_big, riders=[_pair_exchange_rider(arrays(g1))])
    s1 = pair_sums(g1, got1)
    dwo = _matmul_tn(o_act[None], dz3[None], tm_tn, "dw_o")
    dwqkv = _matmul_tn(x2[None], dqkv[None], tm_tn, "dw_qkv")[0]
    g2 = [("attn_w_o", halves4(dwo)), ("attn_w_q", halves4(dwqkv[:, 0:d])),
          ("kv_w_k", halves4(dwqkv[:, d:d + KVD])), ("kv_w_v", halves4(dwqkv[:, d + KVD:d + 2 * KVD]))]
    (dz2, dgg0, duu0, hm0, dx1, d_fg0, d_fb0), (from_chips1, got2) = _bwd_ffn_dx(
        dx2, z2, gg0, uu0, wg, wu, wd, 0, ffng[0], tm_mid,
        riders=[_chip_scatter_rider(s1), _pair_exchange_rider(arrays(g2))])
    f1 = chip_sums(g1, s1, from_chips1)
    s2 = pair_sums(g2, got2)
    g3 = [("ffn_w_gate0", halves4(_matmul_tn(x1[None], dgg0, tm_tn, "dw_gate0"))),
          ("ffn_w_up0", halves4(_matmul_tn(x1[None], duu0, tm_tn, "dw_up0"))),
          ("ffn_w_down0", halves4(_matmul_tn(hm0, dz2[None], tm_tn, "dw_down0")))]
    (dz1, s_act, dc, d_mg0, d_mb0, d_b2, d_clng, d_clnb), (got3, shared1) = _bwd_conv_head(
        dx1, z1, c_act, w2, mixg[0], clng, clnb, tm_mid,
        riders=[_pair_exchange_rider(arrays(g3)), _pair_share_rider(f1)])
    s3 = pair_sums(g3, got3)
    dw2 = _matmul_tn(s_act[None], dz1[None], tm_tn, "dw_pw2")
    (dx0, dh1, d_b1, d_bdw, d_wdw), (from_chips2, from_chips3) = _bwd_conv_glu(
        dc, u_act, a_act, g_act, dz1, wdw, w1s, tm_mid, riders=[_chip_scatter_rider(s2), _chip_scatter_rider(s3)])
    f2 = chip_sums(g2, s2, from_chips2)
    f3 = chip_sums(g3, s3, from_chips3)
    dw1 = _matmul_tn(x0[None], dh1, tm_tn, "dw_pw1")

    def rows4(v):
        return v.reshape(N_CHIPS, -1)

    def rep4(v):
        return jnp.broadcast_to(v.reshape(1, -1), (N_CHIPS, v.size))

    local = {
        'conv_b_pw1': rows4(d_b1),
        'conv_w_dw': rows4(d_wdw[0:CONV_WIDTH].reshape(CONV_WIDTH, N_CHIPS, dq).transpose(1, 0, 2)),
        'conv_b_dw': rows4(d_bdw), 'conv_ln_g': rows4(d_clng), 'conv_ln_b': rows4(d_clnb), 'conv_b_pw2': rows4(d_b2),
        'kv_b_k': rep4(d_bkv[:, 0:KVD]), 'kv_b_v': rep4(d_bkv[:, KVD:2 * KVD]), 'attn_b_q': rep4(d_bq),
        'attn_sinks': rep4(d_sinks), 'attn_b_o': rep4(d_bo),
        'ln_mix_g': rep4(jnp.concatenate([d_mg0, d_mg1])), 'ln_mix_b': rep4(jnp.concatenate([d_mb0, d_mb1])),
        'ln_ffn_g': rep4(jnp.concatenate([d_fg0, d_fg1])), 'ln_ffn_b': rep4(jnp.concatenate([d_fb0, d_fb1])),
    }
    n_small = sum(int(w[n].size) for n in SMALL)
    small_rows = _round_up(n_small, 2 * 8 * 128) // 128
    small_local = jnp.concatenate([local[n] for n in SMALL], axis=1)
    small_local = jnp.pad(small_local, ((0, 0), (0, small_rows * 128 - n_small)))
    g4 = [("conv_w_pw1", halves4(dw1)), ("conv_w_pw2", halves4(dw2)),
          ("small", small_local.reshape(N_CHIPS, 2, small_rows // 2, 128))]
    (got4,) = _run_riders([_pair_exchange_rider(arrays(g4))], "grad_pair_exchange_last")
    s4 = pair_sums(g4, got4)
    (from_chips4,) = _run_riders([_chip_scatter_rider(s4)], "grad_chip_scatter_last")
    f4 = chip_sums(g4, s4, from_chips4)
    (shared_rest,) = _run_riders([_pair_share_rider(f2 + f3 + f4)], "grad_pair_share_last")
    reduced = dict(zip([n for n, _ in g1], shared1))
    reduced.update(zip([n for n, _ in g2 + g3 + g4], shared_rest))
    for n in ('ffn_w_gate', 'ffn_w_up', 'ffn_w_down'):
        reduced[n] = jnp.stack([reduced[n + str(layer)].reshape(w[n].shape[1:]) for layer in range(DEPTH)])

    g_out, delta, new_m, new_v = {}, {}, {}, {}
    for n in BIG:
        shape = w[n].shape
        two_d = (-1, shape[-1])
        g_out[n] = reduced[n].reshape(shape)
        dl, nm, nv = _adamw(w[n].reshape(two_d), g_out[n].reshape(two_d), mom[n].reshape(two_d), var[n].reshape(two_d),
                            "adamw_" + n)
        delta[n], new_m[n], new_v[n] = dl.reshape(shape), nm.reshape(shape), nv.reshape(shape)

    def pack_small(tree):
        return _pad_to(jnp.concatenate([tree[n].reshape(-1) for n in SMALL]), small_rows * 128).reshape(small_rows, 128)

    g_small = reduced['small'].reshape(small_rows, 128)
    dl, nm, nv = _adamw(pack_small(w), g_small, pack_small(mom), pack_small(var), "adamw_small")
    off = 0
    for n in SMALL:
        size, shape = int(w[n].size), w[n].shape
        for tree, flat in ((g_out, g_small), (delta, dl), (new_m, nm), (new_v, nv)):
            tree[n] = flat.reshape(-1)[off:off + size].reshape(shape)
        off += size

    return (loss, dx0.reshape(x.shape), *[g_out[n] for n in WEIGHTS], *[delta[n] for n in WEIGHTS],
            *[new_m[n] for n in WEIGHTS], *[new_v[n] for n in WEIGHTS])
```

```python
import functools
import math

import jax
import jax.numpy as jnp
from jax import lax
from jax.experimental import pallas as pl
from jax.experimental.pallas import tpu as pltpu

F32 = jnp.float32
BF16 = jnp.bfloat16

DEPTH = 2
ALPHA = (2.0 * DEPTH) ** 0.25
LN_EPS = 1e-5
NEG_INF = -1e30
HEAD_DIM = 64
N_KV_HEADS = 2
KVD = N_KV_HEADS * HEAD_DIM
BLOCK = 128
CONV_WIDTH = 31
CONV_HALO = 32
ALIBI_MAX = 8.0
ADAM_LR, ADAM_B1, ADAM_B2, ADAM_EPS, ADAM_WD, ADAM_STEP = 0.001, 0.9, 0.999, 1e-08, 0.01, 10

N_CHIPS = 4
PACK_ROWS = 256
VMEM_LIMIT = 56 * 1024 * 1024
MESH = pl.DeviceIdType.MESH

NT_DIMS = (((1,), (1,)), ((), ()))
TN_DIMS = (((0,), (0,)), ((), ()))

WEIGHTS = ['conv_w_pw1', 'conv_b_pw1', 'conv_w_dw', 'conv_b_dw', 'conv_ln_g', 'conv_ln_b', 'conv_w_pw2', 'conv_b_pw2',
           'kv_w_k', 'kv_b_k', 'kv_w_v', 'kv_b_v', 'attn_w_q', 'attn_b_q', 'attn_sinks', 'attn_w_o', 'attn_b_o',
           'ffn_w_gate', 'ffn_w_up', 'ffn_w_down', 'ln_mix_g', 'ln_mix_b', 'ln_ffn_g', 'ln_ffn_b']
BIG = ['conv_w_pw1', 'conv_w_pw2', 'kv_w_k', 'kv_w_v', 'attn_w_q', 'attn_w_o', 'ffn_w_gate', 'ffn_w_up', 'ffn_w_down']
SMALL_SHARDED = ['conv_b_pw1', 'conv_w_dw', 'conv_b_dw', 'conv_ln_g', 'conv_ln_b', 'conv_b_pw2']
REPLICATED = ['kv_b_k', 'kv_b_v', 'attn_b_q', 'attn_sinks', 'attn_b_o', 'ln_mix_g', 'ln_mix_b', 'ln_ffn_g', 'ln_ffn_b']
SMALL = SMALL_SHARDED + REPLICATED


def _cparams(n_grid=1):
    return pltpu.CompilerParams(dimension_semantics=("arbitrary",) * n_grid, vmem_limit_bytes=VMEM_LIMIT)


def _rows(tm, width):
    return pl.BlockSpec((tm, width), lambda i: (i, 0))


def _const(shape):
    return pl.BlockSpec(shape, lambda *_: (0,) * len(shape), pipeline_mode=pl.Buffered(1))


def _acc_out(shape):
    return pl.BlockSpec(shape, lambda *_: (0,) * len(shape))


def _dot(a, b):
    return jnp.dot(a, b, preferred_element_type=F32)


def _dot_nt(a, b):
    return lax.dot_general(a, b, NT_DIMS, preferred_element_type=F32)


def _dot_tn(a, b):
    return lax.dot_general(a, b, TN_DIMS, preferred_element_type=F32)


def _colsum8(v):
    m, n = v.shape
    return jnp.sum(v.reshape(m // 8, 8, n), axis=0)


def _ln_stats(z):
    mu = jnp.mean(z, axis=-1, keepdims=True)
    zc = z - mu
    var = jnp.mean(zc * zc, axis=-1, keepdims=True)
    rstd = lax.rsqrt(var + LN_EPS)
    return zc * rstd, rstd


def _ln_fwd(z, g, b):
    zhat, _ = _ln_stats(z)
    return zhat * g + b


def _ln_bwd(dy, z, g):
    zhat, rstd = _ln_stats(z)
    dzh = dy * g
    m1 = jnp.mean(dzh, axis=-1, keepdims=True)
    m2 = jnp.mean(dzh * zhat, axis=-1, keepdims=True)
    return rstd * (dzh - m1 - zhat * m2), zhat


def _silu_and_grad(n):
    sg = jax.nn.sigmoid(n)
    return n * sg, sg * (1.0 + n * (1.0 - sg))


def _acc_init(i, *refs):
    @pl.when(i == 0)
    def _():
        for r in refs:
            r[...] = jnp.zeros_like(r)


def _mesh_pos():
    x, y, c = lax.axis_index("x"), lax.axis_index("y"), lax.axis_index("c")
    chips = [(1 - x, y), (x, 1 - y), (1 - x, 1 - y)]
    return x, y, c, chips


HBM_SPEC = pl.BlockSpec(memory_space=pltpu.HBM)


def _remote(src, dst, send_sems, recv_sems, k, to):
    return pltpu.make_async_remote_copy(src_ref=src, dst_ref=dst, send_sem=send_sems.at[k], recv_sem=recv_sems.at[k],
                                        device_id=to, device_id_type=MESH)


class _Rider:
    def __init__(self, operands, out_shapes, sem_shapes, start, finish, mid=None, in_place=False):
        self.operands, self.out_shapes, self.sem_shapes = list(operands), list(out_shapes), list(sem_shapes)
        self.start, self.finish, self.mid = start, finish, mid
        self.in_place = in_place


def _rider_aliases(riders, first_in, first_out):
    aliases, k_in, k_out = {}, first_in, first_out
    for r in riders:
        if r.in_place:
            aliases.update({k_in + k: k_out + k for k in range(len(r.operands))})
        k_in += len(r.operands)
        k_out += len(r.out_shapes)
    return aliases


def _split(refs, counts):
    parts, k = [], 0
    for n in counts:
        parts.append(refs[k:k + n])
        k += n
    return parts


def _rider_refs(riders, ins, outs, sems):
    return list(zip(riders, _split(ins, [len(r.operands) for r in riders]),
                    _split(outs, [len(r.out_shapes) for r in riders]),
                    _split(sems, [len(r.sem_shapes) for r in riders])))


def _tc_call(body, *, name, nt, in_specs, out_specs, out_shape, operands, scratch_shapes=(), riders=(), mid_frac=0.75):
    n_in, n_out, n_scr = len(in_specs), len(out_specs), len(scratch_shapes)
    r_ops = [o for r in riders for o in r.operands]
    r_outs = [o for r in riders for o in r.out_shapes]
    r_sems = [s for r in riders for s in r.sem_shapes]
    mid_step = min(max(int(nt * mid_frac), 0), nt - 1)

    def full(*refs):
        ins, r_in, outs, r_out, scr, r_sem = _split(refs, [n_in, len(r_ops), n_out, len(r_outs), n_scr, len(r_sems)])
        parts = _rider_refs(riders, r_in, r_out, r_sem)
        i = pl.program_id(0)

        @pl.when(i == 0)
        def _():
            for r, a, b, s in parts:
                r.start(a, b, s)

        body(*ins, *outs, *scr)

        @pl.when(i == mid_step)
        def _():
            for r, a, b, s in parts:
                if r.mid is not None:
                    r.mid(a, b, s)

        @pl.when(i == nt - 1)
        def _():
            for r, a, b, s in parts:
                r.finish(a, b, s)

    res = pl.pallas_call(
        full if riders else body, name=name, grid=(nt,), in_specs=list(in_specs) + [HBM_SPEC] * len(r_ops),
        out_specs=list(out_specs) + [HBM_SPEC] * len(r_outs), out_shape=list(out_shape) + r_outs,
        scratch_shapes=list(scratch_shapes) + r_sems, input_output_aliases=_rider_aliases(riders, n_in, n_out),
        compiler_params=_cparams(),
    )(*operands, *r_ops)
    return res[:n_out], _split(res[n_out:], [len(r.out_shapes) for r in riders])


def _run_riders(riders, name):
    r_ops = [o for r in riders for o in r.operands]
    r_outs = [o for r in riders for o in r.out_shapes]
    r_sems = [s for r in riders for s in r.sem_shapes]

    def body(*refs):
        r_in, r_out, r_sem = _split(refs, [len(r_ops), len(r_outs), len(r_sems)])
        parts = _rider_refs(riders, r_in, r_out, r_sem)
        for r, a, b, s in parts:
            r.start(a, b, s)
        for r, a, b, s in parts:
            if r.mid is not None:
                r.mid(a, b, s)
        for r, a, b, s in parts:
            r.finish(a, b, s)

    res = pl.pallas_call(body, name=name, out_shape=tuple(r_outs), in_specs=[HBM_SPEC] * len(r_ops),
                         out_specs=(HBM_SPEC,) * len(r_outs), scratch_shapes=r_sems,
                         input_output_aliases=_rider_aliases(riders, 0, 0))(*r_ops)
    return _split(list(res), [len(r.out_shapes) for r in riders])


def _all_gather_rider(bufs, small=None):
    n = len(bufs)
    n_small = 0 if small is None else 1

    def copies(outs, sems):
        send_sems, recv_sems = sems
        x, y, c, chips = _mesh_pos()
        me = 2 * x + y
        here, sibling = (x, y, c), (x, y, 1 - c)
        rows = [2 * cx + cy for cx, cy in chips]

        def big(p, k, chip_row, half, to):
            piece = outs[p].at[chip_row, half]
            return _remote(piece, piece, send_sems, recv_sems, 6 * p + k, to)

        first = [big(p, j, me, c, (cx, cy, c)) for p in range(n) for j, (cx, cy) in enumerate(chips)]
        landed = [big(p, j, rows[j], c, here) for p in range(n) for j in range(3)]
        passed = [big(p, 3 + j, rows[j], c, sibling) for p in range(n) for j in range(3)]
        arrivals = [big(p, 3 + j, rows[j], 1 - c, here) for p in range(n) for j in range(3)]
        if n_small:
            first = [_remote(outs[n].at[me], outs[n].at[me], send_sems, recv_sems, 6 * n + j, (cx, cy, c))
                     for j, (cx, cy) in enumerate(chips)] + first
            arrivals += [_remote(outs[n].at[rows[j]], outs[n].at[rows[j]], send_sems, recv_sems, 6 * n + j, here)
                         for j in range(3)]
        return first, landed, passed, arrivals

    def start(ins, outs, sems):
        for cp in copies(outs, sems)[0]:
            cp.start()

    def mid(ins, outs, sems):
        _, landed, passed, _ = copies(outs, sems)
        for got, fwd in zip(landed, passed):
            got.wait_recv()
            fwd.start()

    def finish(ins, outs, sems):
        first, _, passed, arrivals = copies(outs, sems)
        for cp in arrivals:
            cp.wait_recv()
        for cp in first + passed:
            cp.wait_send()

    operands = list(bufs) + ([small] if n_small else [])
    n_sem = 6 * n + 3 * n_small
    return _Rider(operands, [jax.ShapeDtypeStruct(o.shape, o.dtype) for o in operands],
                  [pltpu.SemaphoreType.DMA((n_sem,)), pltpu.SemaphoreType.DMA((n_sem,))], start, finish, mid,
                  in_place=True)


def _pair_exchange_rider(plist):
    n = len(plist)

    def copies(ins, outs, sems):
        x, y, c, _ = _mesh_pos()
        return [_remote(ins[k].at[:, 1 - c], outs[k], sems[0], sems[1], k, (x, y, 1 - c)) for k in range(n)]

    def start(ins, outs, sems):
        for cp in copies(ins, outs, sems):
            cp.start()

    def finish(ins, outs, sems):
        for cp in copies(ins, outs, sems):
            cp.wait()

    return _Rider(plist, [jax.ShapeDtypeStruct((p.shape[0],) + p.shape[2:], p.dtype) for p in plist],
                  [pltpu.SemaphoreType.DMA((n,)), pltpu.SemaphoreType.DMA((n,))], start, finish)


def _pair_sum(p, got, c, name):
    n, _, r, l = p.shape
    br = _row_block(r, PACK_ROWS)

    def body(c_ref, p_ref, got_ref, out_ref):
        out_ref[...] = p_ref[...] + got_ref[...]

    return pl.pallas_call(
        body, name=name, out_shape=jax.ShapeDtypeStruct((n, r, l), F32),
        grid_spec=pltpu.PrefetchScalarGridSpec(
            num_scalar_prefetch=1, grid=(n, r // br),
            in_specs=[pl.BlockSpec((None, None, br, l), lambda j, i, c_ref: (j, c_ref[0], i, 0)),
                      pl.BlockSpec((None, br, l), lambda j, i, c_ref: (j, i, 0))],
            out_specs=pl.BlockSpec((None, br, l), lambda j, i, c_ref: (j, i, 0))),
        compiler_params=_cparams(2),
    )(c, p, got)


def _chip_scatter_rider(slist):
    n = len(slist)

    def copies(ins, outs, sems):
        send_sems, recv_sems = sems
        x, y, c, chips = _mesh_pos()
        sends = [_remote(ins[k].at[2 * cx + cy], outs[k].at[j], send_sems, recv_sems, 3 * k + j, (cx, cy, c))
                 for k in range(n) for j, (cx, cy) in enumerate(chips)]
        arrivals = [_remote(ins[k].at[0], outs[k].at[j], send_sems, recv_sems, 3 * k + j, (x, y, c))
                    for k in range(n) for j in range(3)]
        return sends, arrivals

    def start(ins, outs, sems):
        for cp in copies(ins, outs, sems)[0]:
            cp.start()

    def finish(ins, outs, sems):
        sends, arrivals = copies(ins, outs, sems)
        for cp in arrivals:
            cp.wait_recv()
        for cp in sends:
            cp.wait_send()

    return _Rider(slist, [jax.ShapeDtypeStruct((3,) + s.shape[1:], s.dtype) for s in slist],
                  [pltpu.SemaphoreType.DMA((3 * n,)), pltpu.SemaphoreType.DMA((3 * n,))], start, finish)


def _chip_sum(s, got, pos, name):
    _, r, l = s.shape
    br = _row_block(r, PACK_ROWS)

    def body(pos_ref, s_ref, got_ref, out_ref):
        me = pos_ref[0]
        total = None
        for chip in range(N_CHIPS):
            flip = jnp.bitwise_xor(me, chip)
            term = jnp.where(flip == 0, s_ref[...],
                             jnp.where(flip == 2, got_ref[0], jnp.where(flip == 1, got_ref[1], got_ref[2])))
            total = term if total is None else total + term
        out_ref[...] = total

    return pl.pallas_call(
        body, name=name, out_shape=jax.ShapeDtypeStruct((2, r, l), F32),
        grid_spec=pltpu.PrefetchScalarGridSpec(
            num_scalar_prefetch=1, grid=(r // br,),
            in_specs=[pl.BlockSpec((None, br, l), lambda i, pos_ref: (pos_ref[0], i, 0)),
                      pl.BlockSpec((3, br, l), lambda i, pos_ref: (0, i, 0))],
            out_specs=pl.BlockSpec((None, br, l), lambda i, pos_ref: (pos_ref[1], i, 0))),
        compiler_params=_cparams(1),
    )(pos, s, got)


def _pair_share_rider(flist):
    n = len(flist)

    def copies(outs, sems):
        x, y, c, _ = _mesh_pos()
        sends = [_remote(outs[k].at[c], outs[k].at[c], sems[0], sems[1], k, (x, y, 1 - c)) for k in range(n)]
        arrivals = [_remote(outs[k].at[1 - c], outs[k].at[1 - c], sems[0], sems[1], k, (x, y, c)) for k in range(n)]
        return sends, arrivals

    def start(ins, outs, sems):
        for cp in copies(outs, sems)[0]:
            cp.start()

    def finish(ins, outs, sems):
        sends, arrivals = copies(outs, sems)
        for cp in arrivals:
            cp.wait_recv()
        for cp in sends:
            cp.wait_send()

    return _Rider(flist, [jax.ShapeDtypeStruct(f.shape, f.dtype) for f in flist],
                  [pltpu.SemaphoreType.DMA((n,)), pltpu.SemaphoreType.DMA((n,))], start, finish, in_place=True)


def _fwd_pw1_glu(x, w1s, b1, tm):
    t, d = x.shape
    dh = d // 2

    def body(x_ref, w_ref, b_ref, a_ref, g_ref, u_ref):
        xb = x_ref[...].astype(BF16)
        for hh in range(2):
            cs = slice(hh * dh, (hh + 1) * dh)
            a = _dot(xb, w_ref[hh]) + b_ref[:, hh * dh:(hh + 1) * dh]
            g = _dot(xb, w_ref[2 + hh]) + b_ref[:, d + hh * dh:d + (hh + 1) * dh]
            a_ref[:, cs] = a.astype(BF16)
            g_ref[:, cs] = g.astype(BF16)
            u_ref[:, cs] = a * jax.nn.sigmoid(g)

    return pl.pallas_call(
        body, name="fwd_pw1_glu", grid=(t // tm,),
        in_specs=[_rows(tm, d), _const((4, d, dh)), _const((1, 2 * d))],
        out_specs=[_rows(tm, d)] * 3,
        out_shape=[jax.ShapeDtypeStruct((t, d), BF16), jax.ShapeDtypeStruct((t, d), BF16),
                   jax.ShapeDtypeStruct((t, d), F32)],
        compiler_params=_cparams(),
    )(x, w1s, b1)


def _fill_shifted(sh_ref, ext_ref):
    n = sh_ref.shape[1]
    for s in range(8):
        sh_ref[s] = ext_ref[pl.ds(s, n), :]


CONV_CHUNK = 64
LANES = 256


def _tap_sum(w_ref, sh, base, d, tap_row, out_ref, bias_ref=None):
    groups = CONV_CHUNK // 8
    for lg in range(d // LANES):
        ls = slice(lg * LANES, (lg + 1) * LANES)
        acc = jnp.zeros((groups, 8, LANES), F32)
        for k in range(CONV_WIDTH):
            e = tap_row(k)
            x = sh[e % 8, pl.ds(base + (e // 8) * 8, CONV_CHUNK), ls]
            acc = acc + w_ref[k, :, ls] * x.reshape(groups, 8, LANES)
        acc = acc.reshape(CONV_CHUNK, LANES)
        out_ref[pl.ds(base, CONV_CHUNK), ls] = acc if bias_ref is None else acc + bias_ref[:, ls]


def _fwd_conv_tail(u, x0, wdw, bdw, lng, lnb, w2, b2, mixg, mixb, tm, riders=()):
    t, d = u.shape
    hb = tm // CONV_HALO

    def body(u_ref, uh_ref, x_ref, w_ref, bdw_ref, lng_ref, lnb_ref, w2_ref, b2_ref, mg_ref, mb_ref,
             c_ref, z_ref, y_ref, ext, sh):
        i = pl.program_id(0)
        ext[0:CONV_HALO] = jnp.where(i == 0, 0.0, uh_ref[...])
        ext[CONV_HALO:CONV_HALO + tm] = u_ref[...]
        ext[CONV_HALO + tm:CONV_HALO + tm + 8] = jnp.zeros((8, d), F32)
        _fill_shifted(sh, ext)

        def chunk(r, carry):
            base = pl.multiple_of(r * CONV_CHUNK, CONV_CHUNK)
            _tap_sum(w_ref, sh, base, d, lambda k: k + CONV_HALO - (CONV_WIDTH - 1), c_ref, bdw_ref)
            return carry

        lax.fori_loop(0, tm // CONV_CHUNK, chunk, 0)
        n = _ln_fwd(c_ref[...], lng_ref[...], lnb_ref[...])
        s = n * jax.nn.sigmoid(n)
        m = _dot(s.astype(BF16), w2_ref[...]) + b2_ref[...]
        z = ALPHA * x_ref[...] + m
        z_ref[...] = z
        y_ref[...] = _ln_fwd(z, mg_ref[...], mb_ref[...])

    vec = _const((1, d))
    return _tc_call(
        body, name="fwd_conv_tail", nt=t // tm,
        in_specs=[_rows(tm, d), pl.BlockSpec((CONV_HALO, d), lambda i: (jnp.maximum(i * hb - 1, 0), 0)), _rows(tm, d),
                  _const((CONV_HALO, 8, d)), vec, vec, vec, _const((d, d)), vec, vec, vec],
        out_specs=[_rows(tm, d)] * 3,
        out_shape=[jax.ShapeDtypeStruct((t, d), F32)] * 3,
        scratch_shapes=[pltpu.VMEM((tm + CONV_HALO + 8, d), F32), pltpu.VMEM((8, tm + CONV_HALO, d), F32)],
        operands=(u, u, x0, wdw, bdw, lng, lnb, w2, b2, mixg, mixb), riders=riders)


def _fwd_ffn(x, wg, wu, wd, layer, lng, lnb, tm):
    t, d = x.shape
    fs = wg.shape[-1]

    def body(x_ref, wg_ref, wu_ref, wd_ref, g_ref, b_ref, gg_ref, uu_ref, z_ref, y_ref):
        xv = x_ref[...]
        xb = xv.astype(BF16)
        f = jnp.zeros((tm, d), F32)
        for j in range(N_CHIPS):
            gj = _dot(xb, wg_ref[j])
            uj = _dot(xb, wu_ref[j])
            gg_ref[j] = gj.astype(BF16)
            uu_ref[j] = uj.astype(BF16)
            hm = gj * jax.nn.sigmoid(gj) * uj
            f = f + _dot(hm.astype(BF16), wd_ref[j])
        z = ALPHA * xv + f
        z_ref[...] = z
        y_ref[...] = _ln_fwd(z, g_ref[...], b_ref[...])

    wcol = pl.BlockSpec((N_CHIPS, None, d, fs), lambda i: (0, layer, 0, 0), pipeline_mode=pl.Buffered(1))
    wrow = pl.BlockSpec((N_CHIPS, None, fs, d), lambda i: (0, layer, 0, 0), pipeline_mode=pl.Buffered(1))
    hid = pl.BlockSpec((N_CHIPS, tm, fs), lambda i: (0, i, 0))
    return pl.pallas_call(
        body, name=f"fwd_ffn{layer}", grid=(t // tm,),
        in_specs=[_rows(tm, d), wcol, wcol, wrow, _const((1, d)), _const((1, d))],
        out_specs=[hid, hid, _rows(tm, d), _rows(tm, d)],
        out_shape=[jax.ShapeDtypeStruct((N_CHIPS, t, fs), BF16)] * 2 + [jax.ShapeDtypeStruct((t, d), F32)] * 2,
        compiler_params=_cparams(),
    )(x, wg, wu, wd, lng, lnb)


def _attn_bias(nq):
    qi = lax.broadcasted_iota(jnp.int32, (BLOCK, 2 * BLOCK), 0)
    kj = lax.broadcasted_iota(jnp.int32, (BLOCK, 2 * BLOCK), 1)
    delta = qi + BLOCK - kj
    band = jnp.logical_and(delta >= 0, delta < BLOCK)
    return delta.astype(F32), band, kj


def _slope(h, nq):
    return 2.0 ** (-ALIBI_MAX * (h + 1) / nq)


def _softmax_with_sink(qh, kk, slope, delta, valid, sink):
    s = _dot_nt(qh, kk) * (1.0 / math.sqrt(HEAD_DIM)) - slope * delta
    s = jnp.where(valid, s, NEG_INF)
    m = jnp.maximum(jnp.max(s, axis=-1, keepdims=True), sink)
    p = jnp.exp(s - m)
    e_sink = jnp.exp(sink - m)
    den = jnp.sum(p, axis=-1, keepdims=True) + e_sink
    inv = 1.0 / den
    return p * inv, e_sink * inv


def _fwd_attn(x, wq, bq, wkv, bkv, sinks, wo, bo, mixg, mixb, tm):
    t, d = x.shape
    nq = d // HEAD_DIM
    group = nq // N_KV_HEADS
    nb = tm // BLOCK

    def body(sink_ref, x_ref, xh_ref, wq_ref, bq_ref, wkv_ref, bkv_ref, wo_ref, bo_ref, mg_ref, mb_ref,
             q_ref, kv_ref, o_ref, z_ref, y_ref, kvext, o_scr):
        i = pl.program_id(0)
        xv = x_ref[...]
        xb = xv.astype(BF16)
        q_ref[...] = (_dot(xb, wq_ref[...]) + bq_ref[...]).astype(BF16)
        kvb = (_dot(xb, wkv_ref[...]) + bkv_ref[...]).astype(BF16)
        kv_ref[...] = kvb
        kvext[0:BLOCK] = (_dot(xh_ref[...].astype(BF16), wkv_ref[...]) + bkv_ref[...]).astype(BF16)
        kvext[BLOCK:BLOCK + tm] = kvb
        delta, band, kj = _attn_bias(nq)

        def block(b, carry):
            r0 = pl.multiple_of(b * BLOCK, BLOCK)
            first = jnp.logical_and(i == 0, b == 0)
            valid = jnp.logical_and(band, jnp.logical_or(kj >= BLOCK, jnp.logical_not(first)))
            for h in range(nq):
                kvh = h // group
                qh = q_ref[pl.ds(r0, BLOCK), h * HEAD_DIM:(h + 1) * HEAD_DIM]
                kk = kvext[pl.ds(r0, 2 * BLOCK), kvh * HEAD_DIM:(kvh + 1) * HEAD_DIM]
                vv = kvext[pl.ds(r0, 2 * BLOCK), KVD + kvh * HEAD_DIM:KVD + (kvh + 1) * HEAD_DIM]
                p, _ = _softmax_with_sink(qh, kk, _slope(h, nq), delta, valid, sink_ref[h])
                o_scr[pl.ds(r0, BLOCK), h * HEAD_DIM:(h + 1) * HEAD_DIM] = _dot(p.astype(BF16), vv)
            return carry

        lax.fori_loop(0, nb, block, 0)
        ob = o_scr[...].astype(BF16)
        o_ref[...] = ob
        z = ALPHA * xv + _dot(ob, wo_ref[...]) + bo_ref[...]
        z_ref[...] = z
        y_ref[...] = _ln_fwd(z, mg_ref[...], mb_ref[...])

    hb = tm // BLOCK
    vec = _const((1, d))
    return pl.pallas_call(
        body, name="fwd_attn", grid=(t // tm,),
        in_specs=[pl.BlockSpec(memory_space=pltpu.SMEM),
                  _rows(tm, d), pl.BlockSpec((BLOCK, d), lambda i: (jnp.maximum(i * hb - 1, 0), 0)),
                  _const((d, d)), vec, _const((d, 2 * KVD)), _const((1, 2 * KVD)), _const((d, d)), vec, vec, vec],
        out_specs=[_rows(tm, d), _rows(tm, 2 * KVD), _rows(tm, d), _rows(tm, d), _rows(tm, d)],
        out_shape=[jax.ShapeDtypeStruct((t, d), BF16), jax.ShapeDtypeStruct((t, 2 * KVD), BF16),
                   jax.ShapeDtypeStruct((t, d), BF16), jax.ShapeDtypeStruct((t, d), F32),
                   jax.ShapeDtypeStruct((t, d), F32)],
        scratch_shapes=[pltpu.VMEM((tm + BLOCK, 2 * KVD), BF16), pltpu.VMEM((tm, d), F32)],
        compiler_params=_cparams(),
    )(sinks, x, x, wq, bq, wkv, bkv, wo, bo, mixg, mixb)


def _loss_grad(y, target, tm):
    t, d = y.shape
    nt = t // tm

    def body(y_ref, t_ref, dy_ref, loss_ref, acc):
        i = pl.program_id(0)
        _acc_init(i, acc)
        e = y_ref[...] - t_ref[...]
        dy_ref[...] = e * (1.0 / d)
        acc[...] += _colsum8(e * e)

        @pl.when(i == nt - 1)
        def _():
            loss_ref[...] = jnp.sum(acc[...], keepdims=True) * (0.5 / d)

    return pl.pallas_call(
        body, name="loss_grad", grid=(nt,), in_specs=[_rows(tm, d), _rows(tm, d)],
        out_specs=[_rows(tm, d), pl.BlockSpec((1, 1), lambda i: (0, 0))],
        out_shape=[jax.ShapeDtypeStruct((t, d), F32), jax.ShapeDtypeStruct((1, 1), F32)],
        scratch_shapes=[pltpu.VMEM((8, d), F32)], compiler_params=_cparams(),
    )(y, target)


def _write_sums(i, nt, pairs):
    @pl.when(i == nt - 1)
    def _():
        for out_ref, acc in pairs:
            out_ref[...] = jnp.sum(acc[...], axis=0, keepdims=True)


def _bwd_ffn_dx(dy, z, gg, uu, wg, wu, wd, layer, lng, tm, riders=()):
    t, d = dy.shape
    fs = wg.shape[-1]
    nt = t // tm

    def body(dy_ref, z_ref, gg_ref, uu_ref, wg_ref, wu_ref, wd_ref, g_ref,
             dz_ref, dgg_ref, duu_ref, hm_ref, dx_ref, dlg_ref, dlb_ref, acc_g, acc_b):
        i = pl.program_id(0)
        _acc_init(i, acc_g, acc_b)
        dyv = dy_ref[...]
        dz, zhat = _ln_bwd(dyv, z_ref[...], g_ref[...])
        acc_g[...] += _colsum8(dyv * zhat)
        acc_b[...] += _colsum8(dyv)
        dzb = dz.astype(BF16)
        dz_ref[...] = dzb
        dx = ALPHA * dz
        for j in range(N_CHIPS):
            dh = _dot_nt(dzb, wd_ref[j])
            gj = gg_ref[j].astype(F32)
            uj = uu_ref[j].astype(F32)
            act, dact = _silu_and_grad(gj)
            hm_ref[j] = (act * uj).astype(BF16)
            dgb = (dh * uj * dact).astype(BF16)
            dub = (dh * act).astype(BF16)
            dgg_ref[j] = dgb
            duu_ref[j] = dub
            dx = dx + _dot_nt(dgb, wg_ref[j]) + _dot_nt(dub, wu_ref[j])
        dx_ref[...] = dx
        _write_sums(i, nt, [(dlg_ref, acc_g), (dlb_ref, acc_b)])

    wcol = pl.BlockSpec((N_CHIPS, None, d, fs), lambda i: (0, layer, 0, 0), pipeline_mode=pl.Buffered(1))
    wrow = pl.BlockSpec((N_CHIPS, None, fs, d), lambda i: (0, layer, 0, 0), pipeline_mode=pl.Buffered(1))
    hid = pl.BlockSpec((N_CHIPS, tm, fs), lambda i: (0, i, 0))
    vec = _const((1, d))
    return _tc_call(
        body, name=f"bwd_ffn_dx{layer}", nt=nt,
        in_specs=[_rows(tm, d), _rows(tm, d), hid, hid, wcol, wcol, wrow, vec],
        out_specs=[_rows(tm, d), hid, hid, hid, _rows(tm, d), _acc_out((1, d)), _acc_out((1, d))],
        out_shape=[jax.ShapeDtypeStruct((t, d), BF16)] + [jax.ShapeDtypeStruct((N_CHIPS, t, fs), BF16)] * 3
        + [jax.ShapeDtypeStruct((t, d), F32)] + [jax.ShapeDtypeStruct((1, d), F32)] * 2,
        scratch_shapes=[pltpu.VMEM((8, d), F32)] * 2, operands=(dy, z, gg, uu, wg, wu, wd, lng), riders=riders)


def _matmul_tn(a, b, tt, name):
    ja, t, ka = a.shape
    jb, _, nb = b.shape
    nj = max(ja, jb)

    def body(a_ref, b_ref, o_ref):
        @pl.when(pl.program_id(0) == 0)
        def _():
            o_ref[...] = jnp.zeros_like(o_ref)

        a0 = a_ref[0].astype(BF16) if ja == 1 else None
        b0 = b_ref[0].astype(BF16) if jb == 1 else None
        for j in range(nj):
            aj = a0 if ja == 1 else a_ref[j].astype(BF16)
            bj = b0 if jb == 1 else b_ref[j].astype(BF16)
            o_ref[j] += _dot_tn(aj, bj)

    return pl.pallas_call(
        body, name=name, grid=(t // tt,),
        in_specs=[pl.BlockSpec((ja, tt, ka), lambda i: (0, i, 0)), pl.BlockSpec((jb, tt, nb), lambda i: (0, i, 0))],
        out_specs=pl.BlockSpec((nj, ka, nb), lambda i: (0, 0, 0)), out_shape=jax.ShapeDtypeStruct((nj, ka, nb), F32),
        compiler_params=_cparams(1),
    )(a, b)


def _bwd_attn(dy, z, q, kv, sinks, wo, wq, wkv, mixg, tm, riders=()):
    t, d = dy.shape
    nq = d // HEAD_DIM
    group = nq // N_KV_HEADS
    nb = tm // BLOCK
    nt = t // tm
    hb = tm // BLOCK

    def body(sink_ref, dy_ref, z_ref, q_ref, kv_ref, kvh_ref, wo_ref, wq_ref, wkv_ref, g_ref,
             dz_ref, dqkv_ref, dx_ref, dlg_ref, dlb_ref, dbo_ref, dbq_ref, dbkv_ref, dsink_ref,
             kvext, dkvext, do_scr, dq_scr, carry, acc_g, acc_b, acc_o, acc_q, acc_kv, acc_s):
        i = pl.program_id(0)
        ti = nt - 1 - i
        _acc_init(i, carry, acc_g, acc_b, acc_o, acc_q, acc_kv, acc_s)
        dyv = dy_ref[...]
        dz, zhat = _ln_bwd(dyv, z_ref[...], g_ref[...])
        acc_g[...] += _colsum8(dyv * zhat)
        acc_b[...] += _colsum8(dyv)
        acc_o[...] += _colsum8(dz)
        dzb = dz.astype(BF16)
        dz_ref[...] = dzb
        do_scr[...] = _dot_nt(dzb, wo_ref[...]).astype(BF16)
        kvext[0:BLOCK] = kvh_ref[...]
        kvext[BLOCK:BLOCK + tm] = kv_ref[...]
        dkvext[0:tm] = jnp.zeros((tm, 2 * KVD), F32)
        dkvext[tm:tm + BLOCK] = carry[...]
        delta, band, kj = _attn_bias(nq)

        def block(b, c):
            r0 = pl.multiple_of(b * BLOCK, BLOCK)
            first = jnp.logical_and(ti == 0, b == 0)
            valid = jnp.logical_and(band, jnp.logical_or(kj >= BLOCK, jnp.logical_not(first)))
            for kvh in range(N_KV_HEADS):
                kk = kvext[pl.ds(r0, 2 * BLOCK), kvh * HEAD_DIM:(kvh + 1) * HEAD_DIM]
                vv = kvext[pl.ds(r0, 2 * BLOCK), KVD + kvh * HEAD_DIM:KVD + (kvh + 1) * HEAD_DIM]
                dk = jnp.zeros((2 * BLOCK, HEAD_DIM), F32)
                dv = jnp.zeros((2 * BLOCK, HEAD_DIM), F32)
                for g in range(group):
                    h = kvh * group + g
                    cols = slice(h * HEAD_DIM, (h + 1) * HEAD_DIM)
                    qh = q_ref[pl.ds(r0, BLOCK), cols]
                    doh = do_scr[pl.ds(r0, BLOCK), cols]
                    p, p_sink = _softmax_with_sink(qh, kk, _slope(h, nq), delta, valid, sink_ref[h])
                    dp = _dot_nt(doh, vv)
                    rs = jnp.sum(p * dp, axis=-1, keepdims=True)
                    dsb = (p * (dp - rs) * (1.0 / math.sqrt(HEAD_DIM))).astype(BF16)
                    acc_s[:, h:h + 1] += -(p_sink * rs)
                    dq_scr[pl.ds(r0, BLOCK), cols] = _dot(dsb, kk)
                    dk = dk + _dot_tn(dsb, qh)
                    dv = dv + _dot_tn(p.astype(BF16), doh)
                dkvext[pl.ds(r0, 2 * BLOCK), kvh * HEAD_DIM:(kvh + 1) * HEAD_DIM] += dk
                dkvext[pl.ds(r0, 2 * BLOCK), KVD + kvh * HEAD_DIM:KVD + (kvh + 1) * HEAD_DIM] += dv
            return c

        lax.fori_loop(0, nb, block, 0)
        carry[...] = dkvext[0:BLOCK]
        dq = dq_scr[...]
        dkv = dkvext[BLOCK:BLOCK + tm]
        acc_q[...] += _colsum8(dq)
        acc_kv[...] += _colsum8(dkv)
        dqb = dq.astype(BF16)
        dkvb = dkv.astype(BF16)
        dqkv_ref[:, 0:d] = dqb
        dqkv_ref[:, d:d + 2 * KVD] = dkvb
        dx_ref[...] = ALPHA * dz + _dot_nt(dqb, wq_ref[...]) + _dot_nt(dkvb, wkv_ref[...])
        _write_sums(i, nt, [(dlg_ref, acc_g), (dlb_ref, acc_b), (dbo_ref, acc_o), (dbq_ref, acc_q),
                            (dbkv_ref, acc_kv), (dsink_ref, acc_s)])

    rev = lambda w: pl.BlockSpec((tm, w), lambda i: (nt - 1 - i, 0))
    vec = _const((1, d))
    return _tc_call(
        body, name="bwd_attn", nt=nt,
        in_specs=[pl.BlockSpec(memory_space=pltpu.SMEM), rev(d), rev(d), rev(d), rev(2 * KVD),
                  pl.BlockSpec((BLOCK, 2 * KVD), lambda i: (jnp.maximum((nt - 1 - i) * hb - 1, 0), 0)),
                  _const((d, d)), _const((d, d)), _const((d, 2 * KVD)), vec],
        out_specs=[rev(d), rev(d + 2 * KVD), rev(d)] + [_acc_out((1, d))] * 4
        + [_acc_out((1, 2 * KVD)), _acc_out((1, nq))],
        out_shape=[jax.ShapeDtypeStruct((t, d), BF16), jax.ShapeDtypeStruct((t, d + 2 * KVD), BF16),
                   jax.ShapeDtypeStruct((t, d), F32)] + [jax.ShapeDtypeStruct((1, d), F32)] * 4
        + [jax.ShapeDtypeStruct((1, 2 * KVD), F32), jax.ShapeDtypeStruct((1, nq), F32)],
        scratch_shapes=[pltpu.VMEM((tm + BLOCK, 2 * KVD), BF16), pltpu.VMEM((tm + BLOCK, 2 * KVD), F32),
                        pltpu.VMEM((tm, d), BF16), pltpu.VMEM((tm, d), F32), pltpu.VMEM((BLOCK, 2 * KVD), F32),
                        pltpu.VMEM((8, d), F32), pltpu.VMEM((8, d), F32), pltpu.VMEM((8, d), F32),
                        pltpu.VMEM((8, d), F32), pltpu.VMEM((8, 2 * KVD), F32), pltpu.VMEM((BLOCK, nq), F32)],
        operands=(sinks, dy, z, q, kv, kv, wo, wq, wkv, mixg), riders=riders)


def _bwd_conv_head(dy, z, c, w2, mixg, lng, lnb, tm, riders=()):
    t, d = dy.shape
    nt = t // tm

    def body(dy_ref, z_ref, c_ref, w2_ref, mg_ref, lg_ref, lb_ref,
             dz_ref, s_ref, dc_ref, dmg_ref, dmb_ref, db2_ref, dlg_ref, dlb_ref, a0, a1, a2, a3, a4):
        i = pl.program_id(0)
        _acc_init(i, a0, a1, a2, a3, a4)
        dyv = dy_ref[...]
        dz, zhat = _ln_bwd(dyv, z_ref[...], mg_ref[...])
        a0[...] += _colsum8(dyv * zhat)
        a1[...] += _colsum8(dyv)
        a2[...] += _colsum8(dz)
        dz_ref[...] = dz
        chat, rstd = _ln_stats(c_ref[...])
        n = chat * lg_ref[...] + lb_ref[...]
        act, dact = _silu_and_grad(n)
        s_ref[...] = act.astype(BF16)
        dn = _dot_nt(dz.astype(BF16), w2_ref[...]) * dact
        a3[...] += _colsum8(dn * chat)
        a4[...] += _colsum8(dn)
        dch = dn * lg_ref[...]
        m1 = jnp.mean(dch, axis=-1, keepdims=True)
        m2 = jnp.mean(dch * chat, axis=-1, keepdims=True)
        dc_ref[...] = rstd * (dch - m1 - chat * m2)
        _write_sums(i, nt, [(dmg_ref, a0), (dmb_ref, a1), (db2_ref, a2), (dlg_ref, a3), (dlb_ref, a4)])

    vec = _const((1, d))
    return _tc_call(
        body, name="bwd_conv_head", nt=nt,
        in_specs=[_rows(tm, d), _rows(tm, d), _rows(tm, d), _const((d, d)), vec, vec, vec],
        out_specs=[_rows(tm, d), _rows(tm, d), _rows(tm, d)] + [_acc_out((1, d))] * 5,
        out_shape=[jax.ShapeDtypeStruct((t, d), F32), jax.ShapeDtypeStruct((t, d), BF16),
                   jax.ShapeDtypeStruct((t, d), F32)] + [jax.ShapeDtypeStruct((1, d), F32)] * 5,
        scratch_shapes=[pltpu.VMEM((8, d), F32)] * 5, operands=(dy, z, c, w2, mixg, lng, lnb), riders=riders)


def _bwd_conv_glu(dc, u, a, g, dz, wdw, w1s, tm, riders=()):
    t, d = dc.shape
    dh_w = d // 2
    nt = t // tm
    hb = tm // CONV_HALO
    last_halo = t // CONV_HALO - 1

    def body(dc_ref, dcn_ref, u_ref, up_ref, a_ref, g_ref, dz_ref, w_ref, w1_ref,
             dx_ref, dh_ref, db1_ref, dbdw_ref, dw_ref, ext, sh, du_scr, acc_b1, acc_bdw, acc_w):
        i = pl.program_id(0)
        _acc_init(i, acc_b1, acc_bdw, acc_w)
        dcv = dc_ref[...]
        acc_bdw[...] += _colsum8(dcv)

        ext[0:tm] = dcv
        ext[tm:tm + CONV_HALO] = jnp.where(i == nt - 1, 0.0, dcn_ref[...])
        ext[tm + CONV_HALO:tm + CONV_HALO + 8] = jnp.zeros((8, d), F32)
        _fill_shifted(sh, ext)

        def du_chunk(r, carry):
            base = pl.multiple_of(r * CONV_CHUNK, CONV_CHUNK)
            _tap_sum(w_ref, sh, base, d, lambda k: CONV_WIDTH - 1 - k, du_scr)
            return carry

        lax.fori_loop(0, tm // CONV_CHUNK, du_chunk, 0)

        ext[0:CONV_HALO] = jnp.where(i == 0, 0.0, up_ref[...])
        ext[CONV_HALO:CONV_HALO + tm] = u_ref[...]
        _fill_shifted(sh, ext)
        def dw_chunk(r, carry):
            base = pl.multiple_of(r * CONV_CHUNK, CONV_CHUNK)
            groups = CONV_CHUNK // 8
            for lg in range(d // LANES):
                ls = slice(lg * LANES, (lg + 1) * LANES)
                dcv = dc_ref[pl.ds(base, CONV_CHUNK), ls].reshape(groups, 8, LANES)
                for k in range(CONV_WIDTH):
                    e = k + CONV_HALO - (CONV_WIDTH - 1)
                    x = sh[e % 8, pl.ds(base + (e // 8) * 8, CONV_CHUNK), ls].reshape(groups, 8, LANES)
                    acc_w[k, :, ls] += jnp.sum(dcv * x, axis=0)
            return carry

        lax.fori_loop(0, tm // CONV_CHUNK, dw_chunk, 0)

        du = du_scr[...]
        av = a_ref[...].astype(F32)
        sg = jax.nn.sigmoid(g_ref[...].astype(F32))
        da = du * sg
        dg = du * av * sg * (1.0 - sg)
        acc_b1[:, 0:d] += _colsum8(da)
        acc_b1[:, d:2 * d] += _colsum8(dg)
        dx = ALPHA * dz_ref[...]
        for j, part in enumerate([da[:, 0:dh_w], da[:, dh_w:d], dg[:, 0:dh_w], dg[:, dh_w:d]]):
            pb = part.astype(BF16)
            dh_ref[j] = pb
            dx = dx + _dot_nt(pb, w1_ref[j])
        dx_ref[...] = dx

        @pl.when(i == nt - 1)
        def _():
            db1_ref[...] = jnp.sum(acc_b1[...], axis=0, keepdims=True)
            dbdw_ref[...] = jnp.sum(acc_bdw[...], axis=0, keepdims=True)
            dw_ref[...] = jnp.sum(acc_w[...], axis=1)

    return _tc_call(
        body, name="bwd_conv_glu", nt=nt,
        in_specs=[_rows(tm, d), pl.BlockSpec((CONV_HALO, d), lambda i: (jnp.minimum((i + 1) * hb, last_halo), 0)),
                  _rows(tm, d), pl.BlockSpec((CONV_HALO, d), lambda i: (jnp.maximum(i * hb - 1, 0), 0)),
                  _rows(tm, d), _rows(tm, d), _rows(tm, d), _const((CONV_HALO, 8, d)), _const((4, d, dh_w))],
        out_specs=[_rows(tm, d), pl.BlockSpec((4, tm, dh_w), lambda i: (0, i, 0)), _acc_out((1, 2 * d)),
                   _acc_out((1, d)), _acc_out((CONV_HALO, d))],
        out_shape=[jax.ShapeDtypeStruct((t, d), F32), jax.ShapeDtypeStruct((4, t, dh_w), BF16),
                   jax.ShapeDtypeStruct((1, 2 * d), F32), jax.ShapeDtypeStruct((1, d), F32),
                   jax.ShapeDtypeStruct((CONV_HALO, d), F32)],
        scratch_shapes=[pltpu.VMEM((tm + CONV_HALO + 8, d), F32), pltpu.VMEM((8, tm + CONV_HALO, d), F32),
                        pltpu.VMEM((tm, d), F32), pltpu.VMEM((8, 2 * d), F32), pltpu.VMEM((8, d), F32),
                        pltpu.VMEM((CONV_HALO, 8, d), F32)],
        operands=(dc, dc, u, u, a, g, dz, wdw, w1s), riders=riders)


def _row_block(rows, target):
    best = rows
    for cand in range(8, min(rows, target) + 1, 8):
        if rows % cand == 0:
            best = cand
    return best


def _adamw(w, g, m, v, name):
    rows, lanes = w.shape
    br = _row_block(rows, 512) if rows % 8 == 0 else rows

    def body(w_ref, g_ref, m_ref, v_ref, d_ref, nm_ref, nv_ref):
        gv = g_ref[...]
        nm = ADAM_B1 * m_ref[...] + (1.0 - ADAM_B1) * gv
        nv = ADAM_B2 * v_ref[...] + (1.0 - ADAM_B2) * (gv * gv)
        m_hat = nm / (1.0 - ADAM_B1 ** ADAM_STEP)
        v_hat = nv / (1.0 - ADAM_B2 ** ADAM_STEP)
        d_ref[...] = -ADAM_LR * (m_hat / (jnp.sqrt(v_hat) + ADAM_EPS) + ADAM_WD * w_ref[...])
        nm_ref[...] = nm
        nv_ref[...] = nv

    spec = pl.BlockSpec((br, lanes), lambda i: (i, 0))
    return pl.pallas_call(
        body, name=name, grid=(rows // br,), in_specs=[spec] * 4, out_specs=[spec] * 3,
        out_shape=[jax.ShapeDtypeStruct((rows, lanes), F32)] * 3, compiler_params=_cparams(),
    )(w, g, m, v)


def _pad_to(v, n):
    return jnp.pad(v, (0, n - v.shape[0]))


def _round_up(n, m):
    return (n + m - 1) // m * m


def kernel(x, conv_w_pw1, conv_b_pw1, conv_w_dw, conv_b_dw, conv_ln_g, conv_ln_b, conv_w_pw2, conv_b_pw2, kv_w_k, kv_b_k, kv_w_v, kv_b_v, attn_w_q, attn_b_q, attn_sinks, attn_w_o, attn_b_o, ffn_w_gate, ffn_w_up, ffn_w_down, ln_mix_g, ln_mix_b, ln_ffn_g, ln_ffn_b, loss_target, m_conv_w_pw1, m_conv_b_pw1, m_conv_w_dw, m_conv_b_dw, m_conv_ln_g, m_conv_ln_b, m_conv_w_pw2, m_conv_b_pw2, m_kv_w_k, m_kv_b_k, m_kv_w_v, m_kv_b_v, m_attn_w_q, m_attn_b_q, m_attn_sinks, m_attn_w_o, m_attn_b_o, m_ffn_w_gate, m_ffn_w_up, m_ffn_w_down, m_ln_mix_g, m_ln_mix_b, m_ln_ffn_g, m_ln_ffn_b, v_conv_w_pw1, v_conv_b_pw1, v_conv_w_dw, v_conv_b_dw, v_conv_ln_g, v_conv_ln_b, v_conv_w_pw2, v_conv_b_pw2, v_kv_w_k, v_kv_b_k, v_kv_w_v, v_kv_b_v, v_attn_w_q, v_attn_b_q, v_attn_sinks, v_attn_w_o, v_attn_b_o, v_ffn_w_gate, v_ffn_w_up, v_ffn_w_down, v_ln_mix_g, v_ln_mix_b, v_ln_ffn_g, v_ln_ffn_b):
    args = dict(locals())
    w = {n: args[n] for n in WEIGHTS}
    mom = {n: args["m_" + n] for n in WEIGHTS}
    var = {n: args["v_" + n] for n in WEIGHTS}
    assert x.shape[0] == 1, "one sequence per device"
    t, d = x.shape[1], x.shape[2]
    dq = d // 4
    fs = ffn_w_gate.shape[-1]
    nq = d // HEAD_DIM
    x0 = x.reshape(t, d)
    target = loss_target.reshape(t, d)
    tm_big = min(512, t)
    tm_mid = min(256, t)
    tm_tn = min(1024, t)
    c_idx = lax.axis_index("c")

    me_idx = 2 * lax.axis_index("x") + lax.axis_index("y")

    def gather_buffer(v):
        buf = lax.empty((N_CHIPS,) + v.shape, v.dtype)
        return lax.dynamic_update_slice(buf, v[None], (me_idx,) + (0,) * v.ndim)

    def halves(v):
        return v.reshape(2, -1, v.shape[-1])

    small_sizes = [int(w[n].size) for n in SMALL_SHARDED]
    rs = _round_up(sum(small_sizes), 8 * 128) // 128
    spack = _pad_to(jnp.concatenate([w[n].reshape(-1) for n in SMALL_SHARDED]), rs * 128).reshape(rs, 128)
    conv_first = ['conv_w_pw1', 'conv_w_pw2']
    later = [n for n in BIG if n not in conv_first]
    (first_out,) = _run_riders(
        [_all_gather_rider([gather_buffer(halves(w[n].astype(BF16))) for n in conv_first], gather_buffer(spack))],
        "all_gather_conv")
    later_rider = _all_gather_rider([gather_buffer(halves(w[n].astype(BF16))) for n in later])
    gs = first_out[-1].reshape(N_CHIPS, rs * 128)
    full = {n: g.reshape((N_CHIPS,) + w[n].shape) for n, g in zip(conv_first, first_out)}
    off = 0
    for n, size in zip(SMALL_SHARDED, small_sizes):
        full[n] = gs[:, off:off + size].reshape((N_CHIPS,) + w[n].shape)
        off += size
    w1s = full['conv_w_pw1'].reshape(N_CHIPS, d, d // 2)
    w2 = full['conv_w_pw2'].reshape(d, d)
    b1 = full['conv_b_pw1'].reshape(1, 2 * d)
    wdw = jnp.pad(full['conv_w_dw'].reshape(N_CHIPS, CONV_WIDTH, dq).transpose(1, 0, 2).reshape(CONV_WIDTH, d),
                  ((0, CONV_HALO - CONV_WIDTH), (0, 0)))
    wdw = jnp.broadcast_to(wdw[:, None, :], (CONV_HALO, 8, d))
    bdw = full['conv_b_dw'].reshape(1, d)
    clng = full['conv_ln_g'].reshape(1, d)
    clnb = full['conv_ln_b'].reshape(1, d)
    b2 = full['conv_b_pw2'].reshape(1, d)
    bkv = jnp.concatenate([kv_b_k, kv_b_v]).reshape(1, 2 * KVD)
    sinks = attn_sinks.reshape(nq)
    mixg = [ln_mix_g[l].reshape(1, d) for l in range(DEPTH)]
    mixb = [ln_mix_b[l].reshape(1, d) for l in range(DEPTH)]
    ffng = [ln_ffn_g[l].reshape(1, d) for l in range(DEPTH)]
    ffnb = [ln_ffn_b[l].reshape(1, d) for l in range(DEPTH)]

    a_act, g_act, u_act = _fwd_pw1_glu(x0, w1s, b1, tm_big)
    (c_act, z1, x1), (later_out,) = _fwd_conv_tail(u_act, x0, wdw, bdw, clng, clnb, w2, b2, mixg[0], mixb[0], tm_mid,
                                                   riders=[later_rider])
    full.update({n: g.reshape((N_CHIPS,) + w[n].shape) for n, g in zip(later, later_out)})
    wkv = jnp.concatenate([full['kv_w_k'].reshape(d, KVD), full['kv_w_v'].reshape(d, KVD)], axis=1)
    wq = full['attn_w_q'].reshape(d, d)
    wo = full['attn_w_o'].reshape(d, d)
    wg, wu, wd = full['ffn_w_gate'], full['ffn_w_up'], full['ffn_w_down']
    gg0, uu0, z2, x2 = _fwd_ffn(x1, wg, wu, wd, 0, ffng[0], ffnb[0], tm_big)
    q_act, kv_act, o_act, z3, x3 = _fwd_attn(x2, wq, attn_b_q, wkv, bkv, sinks, wo, attn_b_o, mixg[1], mixb[1], tm_big)
    gg1, uu1, z4, x4 = _fwd_ffn(x3, wg, wu, wd, 1, ffng[1], ffnb[1], tm_big)
    dx4, loss_part = _loss_grad(x4, target, tm_big)
    loss = lax.psum(loss_part[0, 0], ("x", "y", "c"))

    c_arr = c_idx.reshape(1).astype(jnp.int32)

    def halves4(v):
        return v.reshape(N_CHIPS, 2, -1, v.shape[-1])

    def arrays(group):
        return [p for _, p in group]

    def pair_sums(group, got):
        return [_pair_sum(p, g, c_arr, "grad_pair_sum_" + n) for (n, p), g in zip(group, got)]

    pos_arr = jnp.stack([me_idx, c_idx]).astype(jnp.int32)

    def chip_sums(group, sums, got):
        return [_chip_sum(s, g, pos_arr, "grad_chip_sum_" + n) for (n, _), s, g in zip(group, sums, got)]

    (dz4, dgg1, duu1, hm1, dx3, d_fg1, d_fb1), _ = _bwd_ffn_dx(dx4, z4, gg1, uu1, wg, wu, wd, 1, ffng[1], tm_mid)
    g1 = [("ffn_w_gate1", halves4(_matmul_tn(x3[None], dgg1, tm_tn, "dw_gate1"))),
          ("ffn_w_up1", halves4(_matmul_tn(x3[None], duu1, tm_tn, "dw_up1"))),
          ("ffn_w_down1", halves4(_matmul_tn(hm1, dz4[None], tm_tn, "dw_down1")))]
    (dz3, dqkv, dx2, d_mg1, d_mb1, d_bo, d_bq, d_bkv, d_sinks), (got1,) = _bwd_attn(
        dx3, z3, q_act, kv_act, sinks, wo, wq, wkv, mixg[1], tm_big, riders=[_pair_exchange_rider(arrays(g1))])
    s1 = pair_sums(g1, got1)
    dwo = _matmul_tn(o_act[None], dz3[None], tm_tn, "dw_o")
    dwqkv = _matmul_tn(x2[None], dqkv[None], tm_tn, "dw_qkv")[0]
    g2 = [("attn_w_o", halves4(dwo)), ("attn_w_q", halves4(dwqkv[:, 0:d])),
          ("kv_w_k", halves4(dwqkv[:, d:d + KVD])), ("kv_w_v", halves4(dwqkv[:, d + KVD:d + 2 * KVD]))]
    (dz2, dgg0, duu0, hm0, dx1, d_fg0, d_fb0), (from_chips1, got2) = _bwd_ffn_dx(
        dx2, z2, gg0, uu0, wg, wu, wd, 0, ffng[0], tm_mid,
        riders=[_chip_scatter_rider(s1), _pair_exchange_rider(arrays(g2))])
    f1 = chip_sums(g1, s1, from_chips1)
    s2 = pair_sums(g2, got2)
    g3 = [("ffn_w_gate0", halves4(_matmul_tn(x1[None], dgg0, tm_tn, "dw_gate0"))),
          ("ffn_w_up0", halves4(_matmul_tn(x1[None], duu0, tm_tn, "dw_up0"))),
          ("ffn_w_down0", halves4(_matmul_tn(hm0, dz2[None], tm_tn, "dw_down0")))]
    (dz1, s_act, dc, d_mg0, d_mb0, d_b2, d_clng, d_clnb), (got3, shared1) = _bwd_conv_head(
        dx1, z1, c_act, w2, mixg[0], clng, clnb, tm_mid,
        riders=[_pair_exchange_rider(arrays(g3)), _pair_share_rider(f1)])
    s3 = pair_sums(g3, got3)
    dw2 = _matmul_tn(s_act[None], dz1[None], tm_tn, "dw_pw2")
    (dx0, dh1, d_b1, d_bdw, d_wdw), (from_chips2, from_chips3) = _bwd_conv_glu(
        dc, u_act, a_act, g_act, dz1, wdw, w1s, tm_mid, riders=[_chip_scatter_rider(s2), _chip_scatter_rider(s3)])
    f2 = chip_sums(g2, s2, from_chips2)
    f3 = chip_sums(g3, s3, from_chips3)
    dw1 = _matmul_tn(x0[None], dh1, tm_tn, "dw_pw1")

    def rows4(v):
        return v.reshape(N_CHIPS, -1)

    def rep4(v):
        return jnp.broadcast_to(v.reshape(1, -1), (N_CHIPS, v.size))

    local = {
        'conv_b_pw1': rows4(d_b1),
        'conv_w_dw': rows4(d_wdw[0:CONV_WIDTH].reshape(CONV_WIDTH, N_CHIPS, dq).transpose(1, 0, 2)),
        'conv_b_dw': rows4(d_bdw), 'conv_ln_g': rows4(d_clng), 'conv_ln_b': rows4(d_clnb), 'conv_b_pw2': rows4(d_b2),
        'kv_b_k': rep4(d_bkv[:, 0:KVD]), 'kv_b_v': rep4(d_bkv[:, KVD:2 * KVD]), 'attn_b_q': rep4(d_bq),
        'attn_sinks': rep4(d_sinks), 'attn_b_o': rep4(d_bo),
        'ln_mix_g': rep4(jnp.concatenate([d_mg0, d_mg1])), 'ln_mix_b': rep4(jnp.concatenate([d_mb0, d_mb1])),
        'ln_ffn_g': rep4(jnp.concatenate([d_fg0, d_fg1])), 'ln_ffn_b': rep4(jnp.concatenate([d_fb0, d_fb1])),
    }
    n_small = sum(int(w[n].size) for n in SMALL)
    small_rows = _round_up(n_small, 2 * 8 * 128) // 128
    small_local = jnp.concatenate([local[n] for n in SMALL], axis=1)
    small_local = jnp.pad(small_local, ((0, 0), (0, small_rows * 128 - n_small)))
    g4 = [("conv_w_pw1", halves4(dw1)), ("conv_w_pw2", halves4(dw2)),
          ("small", small_local.reshape(N_CHIPS, 2, small_rows // 2, 128))]
    (got4,) = _run_riders([_pair_exchange_rider(arrays(g4))], "grad_pair_exchange_last")
    s4 = pair_sums(g4, got4)
    (from_chips4,) = _run_riders([_chip_scatter_rider(s4)], "grad_chip_scatter_last")
    f4 = chip_sums(g4, s4, from_chips4)
    (shared_rest,) = _run_riders([_pair_share_rider(f2 + f3 + f4)], "grad_pair_share_last")
    reduced = dict(zip([n for n, _ in g1], shared1))
    reduced.update(zip([n for n, _ in g2 + g3 + g4], shared_rest))
    for n in ('ffn_w_gate', 'ffn_w_up', 'ffn_w_down'):
        reduced[n] = jnp.stack([reduced[n + str(layer)].reshape(w[n].shape[1:]) for layer in range(DEPTH)])

    g_out, delta, new_m, new_v = {}, {}, {}, {}
    for n in BIG:
        shape = w[n].shape
        two_d = (-1, shape[-1])
        g_out[n] = reduced[n].reshape(shape)
        dl, nm, nv = _adamw(w[n].reshape(two_d), g_out[n].reshape(two_d), mom[n].reshape(two_d), var[n].reshape(two_d),
                            "adamw_" + n)
        delta[n], new_m[n], new_v[n] = dl.reshape(shape), nm.reshape(shape), nv.reshape(shape)

    def pack_small(tree):
        return _pad_to(jnp.concatenate([tree[n].reshape(-1) for n in SMALL]), small_rows * 128).reshape(small_rows, 128)

    g_small = reduced['small'].reshape(small_rows, 128)
    dl, nm, nv = _adamw(pack_small(w), g_small, pack_small(mom), pack_small(var), "adamw_small")
    off = 0
    for n in SMALL:
        size, shape = int(w[n].size), w[n].shape
        for tree, flat in ((g_out, g_small), (delta, dl), (new_m, nm), (new_v, nv)):
            tree[n] = flat.reshape(-1)[off:off + size].reshape(shape)
        off += size

    return (loss, dx0.reshape(x.shape), *[g_out[n] for n in WEIGHTS], *[delta[n] for n in WEIGHTS],
            *[new_m[n] for n in WEIGHTS], *[new_v[n] for n in WEIGHTS])
```

```python
import functools
import math

import jax
import jax.numpy as jnp
from jax import lax
from jax.experimental import pallas as pl
from jax.experimental.pallas import tpu as pltpu

F32 = jnp.float32
BF16 = jnp.bfloat16

DEPTH = 2
ALPHA = (2.0 * DEPTH) ** 0.25
LN_EPS = 1e-5
NEG_INF = -1e30
HEAD_DIM = 64
N_KV_HEADS = 2
KVD = N_KV_HEADS * HEAD_DIM
BLOCK = 128
CONV_WIDTH = 31
CONV_HALO = 32
ALIBI_MAX = 8.0
ADAM_LR, ADAM_B1, ADAM_B2, ADAM_EPS, ADAM_WD, ADAM_STEP = 0.001, 0.9, 0.999, 1e-08, 0.01, 10

N_CHIPS = 4
PACK_ROWS = 256
VMEM_LIMIT = 56 * 1024 * 1024
MESH = pl.DeviceIdType.MESH

NT_DIMS = (((1,), (1,)), ((), ()))
TN_DIMS = (((0,), (0,)), ((), ()))

WEIGHTS = ['conv_w_pw1', 'conv_b_pw1', 'conv_w_dw', 'conv_b_dw', 'conv_ln_g', 'conv_ln_b', 'conv_w_pw2', 'conv_b_pw2',
           'kv_w_k', 'kv_b_k', 'kv_w_v', 'kv_b_v', 'attn_w_q', 'attn_b_q', 'attn_sinks', 'attn_w_o', 'attn_b_o',
           'ffn_w_gate', 'ffn_w_up', 'ffn_w_down', 'ln_mix_g', 'ln_mix_b', 'ln_ffn_g', 'ln_ffn_b']
BIG = ['conv_w_pw1', 'conv_w_pw2', 'kv_w_k', 'kv_w_v', 'attn_w_q', 'attn_w_o', 'ffn_w_gate', 'ffn_w_up', 'ffn_w_down']
SMALL_SHARDED = ['conv_b_pw1', 'conv_w_dw', 'conv_b_dw', 'conv_ln_g', 'conv_ln_b', 'conv_b_pw2']
REPLICATED = ['kv_b_k', 'kv_b_v', 'attn_b_q', 'attn_sinks', 'attn_b_o', 'ln_mix_g', 'ln_mix_b', 'ln_ffn_g', 'ln_ffn_b']
SMALL = SMALL_SHARDED + REPLICATED


def _cparams(n_grid=1):
    return pltpu.CompilerParams(dimension_semantics=("arbitrary",) * n_grid, vmem_limit_bytes=VMEM_LIMIT)


def _rows(tm, width):
    return pl.BlockSpec((tm, width), lambda i: (i, 0))


def _const(shape):
    return pl.BlockSpec(shape, lambda *_: (0,) * len(shape), pipeline_mode=pl.Buffered(1))


def _acc_out(shape):
    return pl.BlockSpec(shape, lambda *_: (0,) * len(shape))


def _dot(a, b):
    return jnp.dot(a, b, preferred_element_type=F32)


def _dot_nt(a, b):
    return lax.dot_general(a, b, NT_DIMS, preferred_element_type=F32)


def _dot_tn(a, b):
    return lax.dot_general(a, b, TN_DIMS, preferred_element_type=F32)


def _colsum8(v):
    m, n = v.shape
    return jnp.sum(v.reshape(m // 8, 8, n), axis=0)


def _ln_stats(z):
    mu = jnp.mean(z, axis=-1, keepdims=True)
    zc = z - mu
    var = jnp.mean(zc * zc, axis=-1, keepdims=True)
    rstd = lax.rsqrt(var + LN_EPS)
    return zc * rstd, rstd


def _ln_fwd(z, g, b):
    zhat, _ = _ln_stats(z)
    return zhat * g + b


def _ln_bwd(dy, z, g):
    zhat, rstd = _ln_stats(z)
    dzh = dy * g
    m1 = jnp.mean(dzh, axis=-1, keepdims=True)
    m2 = jnp.mean(dzh * zhat, axis=-1, keepdims=True)
    return rstd * (dzh - m1 - zhat * m2), zhat


def _silu_and_grad(n):
    sg = jax.nn.sigmoid(n)
    return n * sg, sg * (1.0 + n * (1.0 - sg))


def _acc_init(i, *refs):
    @pl.when(i == 0)
    def _():
        for r in refs:
            r[...] = jnp.zeros_like(r)


def _mesh_pos():
    x, y, c = lax.axis_index("x"), lax.axis_index("y"), lax.axis_index("c")
    chips = [(1 - x, y), (x, 1 - y), (1 - x, 1 - y)]
    return x, y, c, chips


HBM_SPEC = pl.BlockSpec(memory_space=pltpu.HBM)


def _remote(src, dst, send_sems, recv_sems, k, to):
    return pltpu.make_async_remote_copy(src_ref=src, dst_ref=dst, send_sem=send_sems.at[k], recv_sem=recv_sems.at[k],
                                        device_id=to, device_id_type=MESH)


class _Rider:
    def __init__(self, operands, out_shapes, sem_shapes, start, finish, mid=None, in_place=False):
        self.operands, self.out_shapes, self.sem_shapes = list(operands), list(out_shapes), list(sem_shapes)
        self.start, self.finish, self.mid = start, finish, mid
        self.in_place = in_place


def _rider_aliases(riders, first_in, first_out):
    aliases, k_in, k_out = {}, first_in, first_out
    for r in riders:
        if r.in_place:
            aliases.update({k_in + k: k_out + k for k in range(len(r.operands))})
        k_in += len(r.operands)
        k_out += len(r.out_shapes)
    return aliases


def _split(refs, counts):
    parts, k = [], 0
    for n in counts:
        parts.append(refs[k:k + n])
        k += n
    return parts


def _rider_refs(riders, ins, outs, sems):
    return list(zip(riders, _split(ins, [len(r.operands) for r in riders]),
                    _split(outs, [len(r.out_shapes) for r in riders]),
                    _split(sems, [len(r.sem_shapes) for r in riders])))


def _tc_call(body, *, name, nt, in_specs, out_specs, out_shape, operands, scratch_shapes=(), riders=(), mid_frac=0.75):
    n_in, n_out, n_scr = len(in_specs), len(out_specs), len(scratch_shapes)
    r_ops = [o for r in riders for o in r.operands]
    r_outs = [o for r in riders for o in r.out_shapes]
    r_sems = [s for r in riders for s in r.sem_shapes]
    mid_step = min(max(int(nt * mid_frac), 0), nt - 1)

    def full(*refs):
        ins, r_in, outs, r_out, scr, r_sem = _split(refs, [n_in, len(r_ops), n_out, len(r_outs), n_scr, len(r_sems)])
        parts = _rider_refs(riders, r_in, r_out, r_sem)
        i = pl.program_id(0)

        @pl.when(i == 0)
        def _():
            for r, a, b, s in parts:
                r.start(a, b, s)

        body(*ins, *outs, *scr)

        @pl.when(i == mid_step)
        def _():
            for r, a, b, s in parts:
                if r.mid is not None:
                    r.mid(a, b, s)

        @pl.when(i == nt - 1)
        def _():
            for r, a, b, s in parts:
                r.finish(a, b, s)

    res = pl.pallas_call(
        full if riders else body, name=name, grid=(nt,), in_specs=list(in_specs) + [HBM_SPEC] * len(r_ops),
        out_specs=list(out_specs) + [HBM_SPEC] * len(r_outs), out_shape=list(out_shape) + r_outs,
        scratch_shapes=list(scratch_shapes) + r_sems, input_output_aliases=_rider_aliases(riders, n_in, n_out),
        compiler_params=_cparams(),
    )(*operands, *r_ops)
    return res[:n_out], _split(res[n_out:], [len(r.out_shapes) for r in riders])


def _run_riders(riders, name):
    r_ops = [o for r in riders for o in r.operands]
    r_outs = [o for r in riders for o in r.out_shapes]
    r_sems = [s for r in riders for s in r.sem_shapes]

    def body(*refs):
        r_in, r_out, r_sem = _split(refs, [len(r_ops), len(r_outs), len(r_sems)])
        parts = _rider_refs(riders, r_in, r_out, r_sem)
        for r, a, b, s in parts:
            r.start(a, b, s)
        for r, a, b, s in parts:
            if r.mid is not None:
                r.mid(a, b, s)
        for r, a, b, s in parts:
            r.finish(a, b, s)

    res = pl.pallas_call(body, name=name, out_shape=tuple(r_outs), in_specs=[HBM_SPEC] * len(r_ops),
                         out_specs=(HBM_SPEC,) * len(r_outs), scratch_shapes=r_sems,
                         input_output_aliases=_rider_aliases(riders, 0, 0))(*r_ops)
    return _split(list(res), [len(r.out_shapes) for r in riders])


def _all_gather_rider(bufs, small=None):
    n = len(bufs)
    n_small = 0 if small is None else 1

    def copies(outs, sems):
        send_sems, recv_sems = sems
        x, y, c, chips = _mesh_pos()
        me = 2 * x + y
        here, sibling = (x, y, c), (x, y, 1 - c)
        rows = [2 * cx + cy for cx, cy in chips]

        def big(p, k, chip_row, half, to):
            piece = outs[p].at[chip_row, half]
            return _remote(piece, piece, send_sems, recv_sems, 6 * p + k, to)

        first = [big(p, j, me, c, (cx, cy, c)) for p in range(n) for j, (cx, cy) in enumerate(chips)]
        landed = [big(p, j, rows[j], c, here) for p in range(n) for j in range(3)]
        passed = [big(p, 3 + j, rows[j], c, sibling) for p in range(n) for j in range(3)]
        arrivals = [big(p, 3 + j, rows[j], 1 - c, here) for p in range(n) for j in range(3)]
        if n_small:
            first = [_remote(outs[n].at[me], outs[n].at[me], send_sems, recv_sems, 6 * n + j, (cx, cy, c))
                     for j, (cx, cy) in enumerate(chips)] + first
            arrivals += [_remote(outs[n].at[rows[j]], outs[n].at[rows[j]], send_sems, recv_sems, 6 * n + j, here)
                         for j in range(3)]
        return first, landed, passed, arrivals

    def start(ins, outs, sems):
        for cp in copies(outs, sems)[0]:
            cp.start()

    def mid(ins, outs, sems):
        _, landed, passed, _ = copies(outs, sems)
        for got, fwd in zip(landed, passed):
            got.wait_recv()
            fwd.start()

    def finish(ins, outs, sems):
        first, _, passed, arrivals = copies(outs, sems)
        for cp in arrivals:
            cp.wait_recv()
        for cp in first + passed:
            cp.wait_send()

    operands = list(bufs) + ([small] if n_small else [])
    n_sem = 6 * n + 3 * n_small
    return _Rider(operands, [jax.ShapeDtypeStruct(o.shape, o.dtype) for o in operands],
                  [pltpu.SemaphoreType.DMA((n_sem,)), pltpu.SemaphoreType.DMA((n_sem,))], start, finish, mid,
                  in_place=True)


def _pair_exchange_rider(plist):
    n = len(plist)

    def copies(ins, outs, sems):
        x, y, c, _ = _mesh_pos()
        return [_remote(ins[k].at[:, 1 - c], outs[k], sems[0], sems[1], k, (x, y, 1 - c)) for k in range(n)]

    def start(ins, outs, sems):
        for cp in copies(ins, outs, sems):
            cp.start()

    def finish(ins, outs, sems):
        for cp in copies(ins, outs, sems):
            cp.wait()

    return _Rider(plist, [jax.ShapeDtypeStruct((p.shape[0],) + p.shape[2:], p.dtype) for p in plist],
                  [pltpu.SemaphoreType.DMA((n,)), pltpu.SemaphoreType.DMA((n,))], start, finish)


def _pair_sum(p, got, c, name):
    n, _, r, l = p.shape
    br = _row_block(r, PACK_ROWS)

    def body(c_ref, p_ref, got_ref, out_ref):
        out_ref[...] = p_ref[...] + got_ref[...]

    return pl.pallas_call(
        body, name=name, out_shape=jax.ShapeDtypeStruct((n, r, l), F32),
        grid_spec=pltpu.PrefetchScalarGridSpec(
            num_scalar_prefetch=1, grid=(n, r // br),
            in_specs=[pl.BlockSpec((None, None, br, l), lambda j, i, c_ref: (j, c_ref[0], i, 0)),
                      pl.BlockSpec((None, br, l), lambda j, i, c_ref: (j, i, 0))],
            out_specs=pl.BlockSpec((None, br, l), lambda j, i, c_ref: (j, i, 0))),
        compiler_params=_cparams(2),
    )(c, p, got)


def _chip_scatter_rider(slist):
    n = len(slist)

    def copies(ins, outs, sems):
        send_sems, recv_sems = sems
        x, y, c, chips = _mesh_pos()
        sends = [_remote(ins[k].at[2 * cx + cy], outs[k].at[j], send_sems, recv_sems, 3 * k + j, (cx, cy, c))
                 for k in range(n) for j, (cx, cy) in enumerate(chips)]
        arrivals = [_remote(ins[k].at[0], outs[k].at[j], send_sems, recv_sems, 3 * k + j, (x, y, c))
                    for k in range(n) for j in range(3)]
        return sends, arrivals

    def start(ins, outs, sems):
        for cp in copies(ins, outs, sems)[0]:
            cp.start()

    def finish(ins, outs, sems):
        sends, arrivals = copies(ins, outs, sems)
        for cp in arrivals:
            cp.wait_recv()
        for cp in sends:
            cp.wait_send()

    return _Rider(slist, [jax.ShapeDtypeStruct((3,) + s.shape[1:], s.dtype) for s in slist],
                  [pltpu.SemaphoreType.DMA((3 * n,)), pltpu.SemaphoreType.DMA((3 * n,))], start, finish)


def _chip_sum(s, got, pos, name):
    _, r, l = s.shape
    br = _row_block(r, PACK_ROWS)

    def body(pos_ref, s_ref, got_ref, out_ref):
        me = pos_ref[0]
        total = None
        for chip in range(N_CHIPS):
            flip = jnp.bitwise_xor(me, chip)
            term = jnp.where(flip == 0, s_ref[...],
                             jnp.where(flip == 2, got_ref[0], jnp.where(flip == 1, got_ref[1], got_ref[2])))
            total = term if total is None else total + term
        out_ref[...] = total

    return pl.pallas_call(
        body, name=name, out_shape=jax.ShapeDtypeStruct((2, r, l), F32),
        grid_spec=pltpu.PrefetchScalarGridSpec(
            num_scalar_prefetch=1, grid=(r // br,),
            in_specs=[pl.BlockSpec((None, br, l), lambda i, pos_ref: (pos_ref[0], i, 0)),
                      pl.BlockSpec((3, br, l), lambda i, pos_ref: (0, i, 0))],
            out_specs=pl.BlockSpec((None, br, l), lambda i, pos_ref: (pos_ref[1], i, 0))),
        compiler_params=_cparams(1),
    )(pos, s, got)


def _pair_share_rider(flist):
    n = len(flist)

    def copies(outs, sems):
        x, y, c, _ = _mesh_pos()
        sends = [_remote(outs[k].at[c], outs[k].at[c], sems[0], sems[1], k, (x, y, 1 - c)) for k in range(n)]
        arrivals = [_remote(outs[k].at[1 - c], outs[k].at[1 - c], sems[0], sems[1], k, (x, y, c)) for k in range(n)]
        return sends, arrivals

    def start(ins, outs, sems):
        for cp in copies(outs, sems)[0]:
            cp.start()

    def finish(ins, outs, sems):
        sends, arrivals = copies(outs, sems)
        for cp in arrivals:
            cp.wait_recv()
        for cp in sends:
            cp.wait_send()

    return _Rider(flist, [jax.ShapeDtypeStruct(f.shape, f.dtype) for f in flist],
                  [pltpu.SemaphoreType.DMA((n,)), pltpu.SemaphoreType.DMA((n,))], start, finish, in_place=True)


def _fwd_pw1_glu(x, w1s, b1, tm):
    t, d = x.shape
    dh = d // 2

    def body(x_ref, w_ref, b_ref, a_ref, g_ref, u_ref):
        xb = x_ref[...].astype(BF16)
        for hh in range(2):
            cs = slice(hh * dh, (hh + 1) * dh)
            a = _dot(xb, w_ref[hh]) + b_ref[:, hh * dh:(hh + 1) * dh]
            g = _dot(xb, w_ref[2 + hh]) + b_ref[:, d + hh * dh:d + (hh + 1) * dh]
            a_ref[:, cs] = a.astype(BF16)
            g_ref[:, cs] = g.astype(BF16)
            u_ref[:, cs] = a * jax.nn.sigmoid(g)

    return pl.pallas_call(
        body, name="fwd_pw1_glu", grid=(t // tm,),
        in_specs=[_rows(tm, d), _const((4, d, dh)), _const((1, 2 * d))],
        out_specs=[_rows(tm, d)] * 3,
        out_shape=[jax.ShapeDtypeStruct((t, d), BF16), jax.ShapeDtypeStruct((t, d), BF16),
                   jax.ShapeDtypeStruct((t, d), F32)],
        compiler_params=_cparams(),
    )(x, w1s, b1)


def _fill_shifted(sh_ref, ext_ref):
    n = sh_ref.shape[1]
    for s in range(8):
        sh_ref[s] = ext_ref[pl.ds(s, n), :]


CONV_CHUNK = 64
LANES = 256


def _tap_sum(w_ref, sh, base, d, tap_row, out_ref, bias_ref=None):
    groups = CONV_CHUNK // 8
    for lg in range(d // LANES):
        ls = slice(lg * LANES, (lg + 1) * LANES)
        acc = jnp.zeros((groups, 8, LANES), F32)
        for k in range(CONV_WIDTH):
            e = tap_row(k)
            x = sh[e % 8, pl.ds(base + (e // 8) * 8, CONV_CHUNK), ls]
            acc = acc + w_ref[k, :, ls] * x.reshape(groups, 8, LANES)
        acc = acc.reshape(CONV_CHUNK, LANES)
        out_ref[pl.ds(base, CONV_CHUNK), ls] = acc if bias_ref is None else acc + bias_ref[:, ls]


def _fwd_conv_tail(u, x0, wdw, bdw, lng, lnb, w2, b2, mixg, mixb, tm, riders=()):
    t, d = u.shape
    hb = tm // CONV_HALO

    def body(u_ref, uh_ref, x_ref, w_ref, bdw_ref, lng_ref, lnb_ref, w2_ref, b2_ref, mg_ref, mb_ref,
             c_ref, z_ref, y_ref, ext, sh):
        i = pl.program_id(0)
        ext[0:CONV_HALO] = jnp.where(i == 0, 0.0, uh_ref[...])
        ext[CONV_HALO:CONV_HALO + tm] = u_ref[...]
        ext[CONV_HALO + tm:CONV_HALO + tm + 8] = jnp.zeros((8, d), F32)
        _fill_shifted(sh, ext)

        def chunk(r, carry):
            base = pl.multiple_of(r * CONV_CHUNK, CONV_CHUNK)
            _tap_sum(w_ref, sh, base, d, lambda k: k + CONV_HALO - (CONV_WIDTH - 1), c_ref, bdw_ref)
            return carry

        lax.fori_loop(0, tm // CONV_CHUNK, chunk, 0)
        n = _ln_fwd(c_ref[...], lng_ref[...], lnb_ref[...])
        s = n * jax.nn.sigmoid(n)
        m = _dot(s.astype(BF16), w2_ref[...]) + b2_ref[...]
        z = ALPHA * x_ref[...] + m
        z_ref[...] = z
        y_ref[...] = _ln_fwd(z, mg_ref[...], mb_ref[...])

    vec = _const((1, d))
    return _tc_call(
        body, name="fwd_conv_tail", nt=t // tm,
        in_specs=[_rows(tm, d), pl.BlockSpec((CONV_HALO, d), lambda i: (jnp.maximum(i * hb - 1, 0), 0)), _rows(tm, d),
                  _const((CONV_HALO, 8, d)), vec, vec, vec, _const((d, d)), vec, vec, vec],
        out_specs=[_rows(tm, d)] * 3,
        out_shape=[jax.ShapeDtypeStruct((t, d), F32)] * 3,
        scratch_shapes=[pltpu.VMEM((tm + CONV_HALO + 8, d), F32), pltpu.VMEM((8, tm + CONV_HALO, d), F32)],
        operands=(u, u, x0, wdw, bdw, lng, lnb, w2, b2, mixg, mixb), riders=riders)


def _fwd_ffn(x, wg, wu, wd, layer, lng, lnb, tm):
    t, d = x.shape
    fs = wg.shape[-1]

    def body(x_ref, wg_ref, wu_ref, wd_ref, g_ref, b_ref, gg_ref, uu_ref, z_ref, y_ref):
        xv = x_ref[...]
        xb = xv.astype(BF16)
        f = jnp.zeros((tm, d), F32)
        for j in range(N_CHIPS):
            gj = _dot(xb, wg_ref[j])
            uj = _dot(xb, wu_ref[j])
            gg_ref[j] = gj.astype(BF16)
            uu_ref[j] = uj.astype(BF16)
            hm = gj * jax.nn.sigmoid(gj) * uj
            f = f + _dot(hm.astype(BF16), wd_ref[j])
        z = ALPHA * xv + f
        z_ref[...] = z
        y_ref[...] = _ln_fwd(z, g_ref[...], b_ref[...])

    wcol = pl.BlockSpec((N_CHIPS, None, d, fs), lambda i: (0, layer, 0, 0), pipeline_mode=pl.Buffered(1))
    wrow = pl.BlockSpec((N_CHIPS, None, fs, d), lambda i: (0, layer, 0, 0), pipeline_mode=pl.Buffered(1))
    hid = pl.BlockSpec((N_CHIPS, tm, fs), lambda i: (0, i, 0))
    return pl.pallas_call(
        body, name=f"fwd_ffn{layer}", grid=(t // tm,),
        in_specs=[_rows(tm, d), wcol, wcol, wrow, _const((1, d)), _const((1, d))],
        out_specs=[hid, hid, _rows(tm, d), _rows(tm, d)],
        out_shape=[jax.ShapeDtypeStruct((N_CHIPS, t, fs), BF16)] * 2 + [jax.ShapeDtypeStruct((t, d), F32)] * 2,
        compiler_params=_cparams(),
    )(x, wg, wu, wd, lng, lnb)


def _attn_band():
    kt = lax.broadcasted_iota(jnp.int32, (BLOCK, BLOCK), 0)
    qi = lax.broadcasted_iota(jnp.int32, (BLOCK, BLOCK), 1)
    current = kt <= qi
    delta = qi - kt + jnp.where(current, 0, BLOCK)
    return current, delta.astype(F32)


def _fold(full, current):
    return jnp.where(current, full[BLOCK:2 * BLOCK], full[0:BLOCK])


def _unfold(folded, current):
    zero = jnp.zeros_like(folded)
    return jnp.concatenate([jnp.where(current, zero, folded), jnp.where(current, folded, zero)], axis=0)


def _slope(h, nq):
    return 2.0 ** (-ALIBI_MAX * (h + 1) / nq)


def _softmax_with_sink(s_full, slope, band, has_previous, sink):
    current, delta = band
    s = _fold(s_full, current) * (1.0 / math.sqrt(HEAD_DIM)) - slope * delta
    s = jnp.where(jnp.logical_or(current, has_previous), s, NEG_INF)
    m = jnp.maximum(jnp.max(s, axis=0, keepdims=True), sink)
    p = jnp.exp(s - m)
    e_sink = jnp.exp(sink - m)
    inv = 1.0 / (jnp.sum(p, axis=0, keepdims=True) + e_sink)
    return p * inv, e_sink * inv


def _heads_on_lanes(ref, b, g, group):
    first = g * group
    return jnp.concatenate([ref[b, (first + hh) * HEAD_DIM:(first + hh + 1) * HEAD_DIM, :] for hh in range(group)],
                           axis=1)


def _fill_kv(kv_scr, halo, tile, tm):
    for j in range(2 * N_KV_HEADS):
        kv_scr[j, 0:BLOCK] = halo[:, j * HEAD_DIM:(j + 1) * HEAD_DIM]
        kv_scr[j, BLOCK:BLOCK + tm] = tile[:, j * HEAD_DIM:(j + 1) * HEAD_DIM]


def _cols(d, tm):
    return pl.BlockSpec((d, tm), lambda i: (0, i))


def _fwd_attn(x, wqt, bqt, wkv, bkv, sinks, wo, bo, mixg, mixb, tm):
    t, d = x.shape
    nq = d // HEAD_DIM
    group = nq // N_KV_HEADS
    nb = tm // BLOCK

    def body(sink_ref, x_ref, xh_ref, wqt_ref, bqt_ref, wkv_ref, bkv_ref, wo_ref, bo_ref, mg_ref, mb_ref,
             qt_ref, kv_ref, ot_ref, z_ref, y_ref, kv_scr, qt_scr, ot_scr):
        i = pl.program_id(0)
        xv = x_ref[...]
        xb = xv.astype(BF16)
        qt = (_dot_nt(wqt_ref[...], xb) + bqt_ref[...]).astype(BF16)
        qt_ref[...] = qt
        for b in range(nb):
            qt_scr[b] = qt[:, b * BLOCK:(b + 1) * BLOCK]
        kvb = (_dot(xb, wkv_ref[...]) + bkv_ref[...]).astype(BF16)
        kv_ref[...] = kvb
        _fill_kv(kv_scr, (_dot(xh_ref[...].astype(BF16), wkv_ref[...]) + bkv_ref[...]).astype(BF16), kvb, tm)
        band = _attn_band()

        def block(b, carry):
            r0 = pl.multiple_of(b * BLOCK, BLOCK)
            has_previous = jnp.logical_or(i > 0, b > 0)
            for g in range(N_KV_HEADS):
                kk = kv_scr[g, pl.ds(r0, 2 * BLOCK), :]
                vv = kv_scr[N_KV_HEADS + g, pl.ds(r0, 2 * BLOCK), :]
                s_all = _dot(kk, _heads_on_lanes(qt_scr, b, g, group))
                probs = []
                for hh in range(group):
                    h = g * group + hh
                    p, _ = _softmax_with_sink(s_all[:, hh * BLOCK:(hh + 1) * BLOCK], _slope(h, nq), band,
                                              has_previous, sink_ref[h])
                    probs.append(_unfold(p.astype(BF16), band[0]))
                o_all = _dot_tn(vv, jnp.concatenate(probs, axis=1))
                for hh in range(group):
                    h = g * group + hh
                    ot_scr[b, h * HEAD_DIM:(h + 1) * HEAD_DIM, :] = o_all[:, hh * BLOCK:(hh + 1) * BLOCK].astype(BF16)
            return carry

        lax.fori_loop(0, nb, block, 0)
        ot = jnp.concatenate([ot_scr[b] for b in range(nb)], axis=1)
        ot_ref[...] = ot
        z = ALPHA * xv + _dot_tn(ot, wo_ref[...]) + bo_ref[...]
        z_ref[...] = z
        y_ref[...] = _ln_fwd(z, mg_ref[...], mb_ref[...])

    hb = tm // BLOCK
    vec = _const((1, d))
    return pl.pallas_call(
        body, name="fwd_attn", grid=(t // tm,),
        in_specs=[pl.BlockSpec(memory_space=pltpu.SMEM),
                  _rows(tm, d), pl.BlockSpec((BLOCK, d), lambda i: (jnp.maximum(i * hb - 1, 0), 0)),
                  _const((d, d)), _const((d, 1)), _const((d, 2 * KVD)), _const((1, 2 * KVD)), _const((d, d)), vec, vec,
                  vec],
        out_specs=[_cols(d, tm), _rows(tm, 2 * KVD), _cols(d, tm), _rows(tm, d), _rows(tm, d)],
        out_shape=[jax.ShapeDtypeStruct((d, t), BF16), jax.ShapeDtypeStruct((t, 2 * KVD), BF16),
                   jax.ShapeDtypeStruct((d, t), BF16), jax.ShapeDtypeStruct((t, d), F32),
                   jax.ShapeDtypeStruct((t, d), F32)],
        scratch_shapes=[pltpu.VMEM((2 * N_KV_HEADS, tm + BLOCK, HEAD_DIM), BF16), pltpu.VMEM((nb, d, BLOCK), BF16),
                        pltpu.VMEM((nb, d, BLOCK), BF16)],
        compiler_params=_cparams(),
    )(sinks, x, x, wqt, bqt, wkv, bkv, wo, bo, mixg, mixb)


def _loss_grad(y, target, tm):
    t, d = y.shape
    nt = t // tm

    def body(y_ref, t_ref, dy_ref, loss_ref, acc):
        i = pl.program_id(0)
        _acc_init(i, acc)
        e = y_ref[...] - t_ref[...]
        dy_ref[...] = e * (1.0 / d)
        acc[...] += _colsum8(e * e)

        @pl.when(i == nt - 1)
        def _():
            loss_ref[...] = jnp.sum(acc[...], keepdims=True) * (0.5 / d)

    return pl.pallas_call(
        body, name="loss_grad", grid=(nt,), in_specs=[_rows(tm, d), _rows(tm, d)],
        out_specs=[_rows(tm, d), pl.BlockSpec((1, 1), lambda i: (0, 0))],
        out_shape=[jax.ShapeDtypeStruct((t, d), F32), jax.ShapeDtypeStruct((1, 1), F32)],
        scratch_shapes=[pltpu.VMEM((8, d), F32)], compiler_params=_cparams(),
    )(y, target)


def _write_sums(i, nt, pairs):
    @pl.when(i == nt - 1)
    def _():
        for out_ref, acc in pairs:
            out_ref[...] = jnp.sum(acc[...], axis=0, keepdims=True)


def _bwd_ffn_dx(dy, z, gg, uu, wg, wu, wd, layer, lng, tm, riders=()):
    t, d = dy.shape
    fs = wg.shape[-1]
    nt = t // tm

    def body(dy_ref, z_ref, gg_ref, uu_ref, wg_ref, wu_ref, wd_ref, g_ref,
             dz_ref, dgg_ref, duu_ref, hm_ref, dx_ref, dlg_ref, dlb_ref, acc_g, acc_b):
        i = pl.program_id(0)
        _acc_init(i, acc_g, acc_b)
        dyv = dy_ref[...]
        dz, zhat = _ln_bwd(dyv, z_ref[...], g_ref[...])
        acc_g[...] += _colsum8(dyv * zhat)
        acc_b[...] += _colsum8(dyv)
        dzb = dz.astype(BF16)
        dz_ref[...] = dzb
        dx = ALPHA * dz
        for j in range(N_CHIPS):
            dh = _dot_nt(dzb, wd_ref[j])
            gj = gg_ref[j].astype(F32)
            uj = uu_ref[j].astype(F32)
            act, dact = _silu_and_grad(gj)
            hm_ref[j] = (act * uj).astype(BF16)
            dgb = (dh * uj * dact).astype(BF16)
            dub = (dh * act).astype(BF16)
            dgg_ref[j] = dgb
            duu_ref[j] = dub
            dx = dx + _dot_nt(dgb, wg_ref[j]) + _dot_nt(dub, wu_ref[j])
        dx_ref[...] = dx
        _write_sums(i, nt, [(dlg_ref, acc_g), (dlb_ref, acc_b)])

    wcol = pl.BlockSpec((N_CHIPS, None, d, fs), lambda i: (0, layer, 0, 0), pipeline_mode=pl.Buffered(1))
    wrow = pl.BlockSpec((N_CHIPS, None, fs, d), lambda i: (0, layer, 0, 0), pipeline_mode=pl.Buffered(1))
    hid = pl.BlockSpec((N_CHIPS, tm, fs), lambda i: (0, i, 0))
    vec = _const((1, d))
    return _tc_call(
        body, name=f"bwd_ffn_dx{layer}", nt=nt,
        in_specs=[_rows(tm, d), _rows(tm, d), hid, hid, wcol, wcol, wrow, vec],
        out_specs=[_rows(tm, d), hid, hid, hid, _rows(tm, d), _acc_out((1, d)), _acc_out((1, d))],
        out_shape=[jax.ShapeDtypeStruct((t, d), BF16)] + [jax.ShapeDtypeStruct((N_CHIPS, t, fs), BF16)] * 3
        + [jax.ShapeDtypeStruct((t, d), F32)] + [jax.ShapeDtypeStruct((1, d), F32)] * 2,
        scratch_shapes=[pltpu.VMEM((8, d), F32)] * 2, operands=(dy, z, gg, uu, wg, wu, wd, lng), riders=riders)


def _matmul_tn(a, b, tt, name):
    ja, t, ka = a.shape
    jb, _, nb = b.shape
    nj = max(ja, jb)

    def body(a_ref, b_ref, o_ref):
        @pl.when(pl.program_id(0) == 0)
        def _():
            o_ref[...] = jnp.zeros_like(o_ref)

        a0 = a_ref[0].astype(BF16) if ja == 1 else None
        b0 = b_ref[0].astype(BF16) if jb == 1 else None
        for j in range(nj):
            aj = a0 if ja == 1 else a_ref[j].astype(BF16)
            bj = b0 if jb == 1 else b_ref[j].astype(BF16)
            o_ref[j] += _dot_tn(aj, bj)

    return pl.pallas_call(
        body, name=name, grid=(t // tt,),
        in_specs=[pl.BlockSpec((ja, tt, ka), lambda i: (0, i, 0)), pl.BlockSpec((jb, tt, nb), lambda i: (0, i, 0))],
        out_specs=pl.BlockSpec((nj, ka, nb), lambda i: (0, 0, 0)), out_shape=jax.ShapeDtypeStruct((nj, ka, nb), F32),
        compiler_params=_cparams(1),
    )(a, b)


def _matmul_nn(at, b, tt, name):
    ka, t = at.shape
    nb = b.shape[1]

    def body(a_ref, b_ref, o_ref):
        @pl.when(pl.program_id(0) == 0)
        def _():
            o_ref[...] = jnp.zeros_like(o_ref)

        o_ref[...] += _dot(a_ref[...].astype(BF16), b_ref[...].astype(BF16))

    return pl.pallas_call(
        body, name=name, grid=(t // tt,),
        in_specs=[pl.BlockSpec((ka, tt), lambda i: (0, i)), pl.BlockSpec((tt, nb), lambda i: (i, 0))],
        out_specs=pl.BlockSpec((ka, nb), lambda i: (0, 0)), out_shape=jax.ShapeDtypeStruct((ka, nb), F32),
        compiler_params=_cparams(1),
    )(at, b)


def _bwd_attn(dy, z, qt, kv, sinks, wo, wqt, wkv, mixg, tm, riders=()):
    t, d = dy.shape
    nq = d // HEAD_DIM
    group = nq // N_KV_HEADS
    nb = tm // BLOCK
    nt = t // tm
    hb = tm // BLOCK
    n_kv = 2 * N_KV_HEADS

    def body(sink_ref, dy_ref, z_ref, qt_ref, kv_ref, kvh_ref, wo_ref, wqt_ref, wkv_ref, g_ref,
             dz_ref, dqt_ref, dkv_ref, dx_ref, dlg_ref, dlb_ref, dbo_ref, dbq_ref, dbkv_ref, dsink_ref,
             kv_scr, dkv_scr, qt_scr, dot_scr, dqt_scr, carry, acc_g, acc_b, acc_o, acc_q, acc_kv, acc_s):
        i = pl.program_id(0)
        ti = nt - 1 - i
        _acc_init(i, carry, acc_g, acc_b, acc_o, acc_q, acc_kv, acc_s)
        dyv = dy_ref[...]
        dz, zhat = _ln_bwd(dyv, z_ref[...], g_ref[...])
        acc_g[...] += _colsum8(dyv * zhat)
        acc_b[...] += _colsum8(dyv)
        acc_o[...] += _colsum8(dz)
        dzb = dz.astype(BF16)
        dz_ref[...] = dzb
        do_t = _dot_nt(wo_ref[...], dzb).astype(BF16)
        for b in range(nb):
            dot_scr[b] = do_t[:, b * BLOCK:(b + 1) * BLOCK]
            qt_scr[b] = qt_ref[:, b * BLOCK:(b + 1) * BLOCK]
        _fill_kv(kv_scr, kvh_ref[...], kv_ref[...], tm)
        dkv_scr[:, 0:tm] = jnp.zeros((n_kv, tm, HEAD_DIM), F32)
        dkv_scr[:, tm:tm + BLOCK] = carry[...]
        band = _attn_band()

        def block(b, c):
            r0 = pl.multiple_of(b * BLOCK, BLOCK)
            has_previous = jnp.logical_or(ti > 0, b > 0)
            for g in range(N_KV_HEADS):
                kk = kv_scr[g, pl.ds(r0, 2 * BLOCK), :]
                vv = kv_scr[N_KV_HEADS + g, pl.ds(r0, 2 * BLOCK), :]
                q_all = _heads_on_lanes(qt_scr, b, g, group)
                do_all = _heads_on_lanes(dot_scr, b, g, group)
                s_all = _dot(kk, q_all)
                dp_all = _dot(vv, do_all)
                probs, dscores = [], []
                for hh in range(group):
                    h = g * group + hh
                    cols = slice(hh * BLOCK, (hh + 1) * BLOCK)
                    p, p_sink = _softmax_with_sink(s_all[:, cols], _slope(h, nq), band, has_previous, sink_ref[h])
                    dp = _fold(dp_all[:, cols], band[0])
                    rs = jnp.sum(p * dp, axis=0, keepdims=True)
                    acc_s[h:h + 1, :] += -(p_sink * rs)
                    ds = p * (dp - rs) * (1.0 / math.sqrt(HEAD_DIM))
                    probs.append(_unfold(p.astype(BF16), band[0]))
                    dscores.append(_unfold(ds.astype(BF16), band[0]))
                p_all = jnp.concatenate(probs, axis=1)
                ds_all = jnp.concatenate(dscores, axis=1)
                dq_all = _dot_tn(kk, ds_all)
                for hh in range(group):
                    h = g * group + hh
                    dqt_scr[b, h * HEAD_DIM:(h + 1) * HEAD_DIM, :] = dq_all[:, hh * BLOCK:(hh + 1) * BLOCK]
                dkv_scr[g, pl.ds(r0, 2 * BLOCK), :] += _dot_nt(ds_all, q_all)
                dkv_scr[N_KV_HEADS + g, pl.ds(r0, 2 * BLOCK), :] += _dot_nt(p_all, do_all)
            return c

        lax.fori_loop(0, nb, block, 0)
        carry[...] = dkv_scr[:, 0:BLOCK]
        dkv = jnp.concatenate([dkv_scr[j, BLOCK:BLOCK + tm] for j in range(n_kv)], axis=1)
        acc_kv[...] += _colsum8(dkv)
        dkvb = dkv.astype(BF16)
        dkv_ref[...] = dkvb
        dqt = jnp.concatenate([dqt_scr[b] for b in range(nb)], axis=1)
        for b in range(nb):
            acc_q[...] += dqt_scr[b]
        dqtb = dqt.astype(BF16)
        dqt_ref[...] = dqtb
        dx_ref[...] = ALPHA * dz + _dot_tn(dqtb, wqt_ref[...]) + _dot_nt(dkvb, wkv_ref[...])
        _write_sums(i, nt, [(dlg_ref, acc_g), (dlb_ref, acc_b), (dbo_ref, acc_o), (dbkv_ref, acc_kv)])

        @pl.when(i == nt - 1)
        def _():
            dbq_ref[...] = jnp.sum(acc_q[...], axis=1, keepdims=True)
            dsink_ref[...] = jnp.sum(acc_s[...], axis=1, keepdims=True)

    rev = lambda w: pl.BlockSpec((tm, w), lambda i: (nt - 1 - i, 0))
    rev_cols = pl.BlockSpec((d, tm), lambda i: (0, nt - 1 - i))
    vec = _const((1, d))
    return _tc_call(
        body, name="bwd_attn", nt=nt,
        in_specs=[pl.BlockSpec(memory_space=pltpu.SMEM), rev(d), rev(d), rev_cols, rev(2 * KVD),
                  pl.BlockSpec((BLOCK, 2 * KVD), lambda i: (jnp.maximum((nt - 1 - i) * hb - 1, 0), 0)),
                  _const((d, d)), _const((d, d)), _const((d, 2 * KVD)), vec],
        out_specs=[rev(d), rev_cols, rev(2 * KVD), rev(d)] + [_acc_out((1, d))] * 3
        + [_acc_out((d, 1)), _acc_out((1, 2 * KVD)), _acc_out((nq, 1))],
        out_shape=[jax.ShapeDtypeStruct((t, d), BF16), jax.ShapeDtypeStruct((d, t), BF16),
                   jax.ShapeDtypeStruct((t, 2 * KVD), BF16), jax.ShapeDtypeStruct((t, d), F32)]
        + [jax.ShapeDtypeStruct((1, d), F32)] * 3
        + [jax.ShapeDtypeStruct((d, 1), F32), jax.ShapeDtypeStruct((1, 2 * KVD), F32),
           jax.ShapeDtypeStruct((nq, 1), F32)],
        scratch_shapes=[pltpu.VMEM((n_kv, tm + BLOCK, HEAD_DIM), BF16), pltpu.VMEM((n_kv, tm + BLOCK, HEAD_DIM), F32),
                        pltpu.VMEM((nb, d, BLOCK), BF16), pltpu.VMEM((nb, d, BLOCK), BF16),
                        pltpu.VMEM((nb, d, BLOCK), F32), pltpu.VMEM((n_kv, BLOCK, HEAD_DIM), F32),
                        pltpu.VMEM((8, d), F32), pltpu.VMEM((8, d), F32), pltpu.VMEM((8, d), F32),
                        pltpu.VMEM((d, BLOCK), F32), pltpu.VMEM((8, 2 * KVD), F32), pltpu.VMEM((nq, BLOCK), F32)],
        operands=(sinks, dy, z, qt, kv, kv, wo, wqt, wkv, mixg), riders=riders)


def _bwd_conv_head(dy, z, c, w2, mixg, lng, lnb, tm, riders=()):
    t, d = dy.shape
    nt = t // tm

    def body(dy_ref, z_ref, c_ref, w2_ref, mg_ref, lg_ref, lb_ref,
             dz_ref, s_ref, dc_ref, dmg_ref, dmb_ref, db2_ref, dlg_ref, dlb_ref, a0, a1, a2, a3, a4):
        i = pl.program_id(0)
        _acc_init(i, a0, a1, a2, a3, a4)
        dyv = dy_ref[...]
        dz, zhat = _ln_bwd(dyv, z_ref[...], mg_ref[...])
        a0[...] += _colsum8(dyv * zhat)
        a1[...] += _colsum8(dyv)
        a2[...] += _colsum8(dz)
        dz_ref[...] = dz
        chat, rstd = _ln_stats(c_ref[...])
        n = chat * lg_ref[...] + lb_ref[...]
        act, dact = _silu_and_grad(n)
        s_ref[...] = act.astype(BF16)
        dn = _dot_nt(dz.astype(BF16), w2_ref[...]) * dact
        a3[...] += _colsum8(dn * chat)
        a4[...] += _colsum8(dn)
        dch = dn * lg_ref[...]
        m1 = jnp.mean(dch, axis=-1, keepdims=True)
        m2 = jnp.mean(dch * chat, axis=-1, keepdims=True)
        dc_ref[...] = rstd * (dch - m1 - chat * m2)
        _write_sums(i, nt, [(dmg_ref, a0), (dmb_ref, a1), (db2_ref, a2), (dlg_ref, a3), (dlb_ref, a4)])

    vec = _const((1, d))
    return _tc_call(
        body, name="bwd_conv_head", nt=nt,
        in_specs=[_rows(tm, d), _rows(tm, d), _rows(tm, d), _const((d, d)), vec, vec, vec],
        out_specs=[_rows(tm, d), _rows(tm, d), _rows(tm, d)] + [_acc_out((1, d))] * 5,
        out_shape=[jax.ShapeDtypeStruct((t, d), F32), jax.ShapeDtypeStruct((t, d), BF16),
                   jax.ShapeDtypeStruct((t, d), F32)] + [jax.ShapeDtypeStruct((1, d), F32)] * 5,
        scratch_shapes=[pltpu.VMEM((8, d), F32)] * 5, operands=(dy, z, c, w2, mixg, lng, lnb), riders=riders)


def _bwd_conv_glu(dc, u, a, g, dz, wdw, w1s, tm, riders=()):
    t, d = dc.shape
    dh_w = d // 2
    nt = t // tm
    hb = tm // CONV_HALO
    last_halo = t // CONV_HALO - 1

    def body(dc_ref, dcn_ref, u_ref, up_ref, a_ref, g_ref, dz_ref, w_ref, w1_ref,
             dx_ref, dh_ref, db1_ref, dbdw_ref, dw_ref, ext, sh, du_scr, acc_b1, acc_bdw, acc_w):
        i = pl.program_id(0)
        _acc_init(i, acc_b1, acc_bdw, acc_w)
        dcv = dc_ref[...]
        acc_bdw[...] += _colsum8(dcv)

        ext[0:tm] = dcv
        ext[tm:tm + CONV_HALO] = jnp.where(i == nt - 1, 0.0, dcn_ref[...])
        ext[tm + CONV_HALO:tm + CONV_HALO + 8] = jnp.zeros((8, d), F32)
        _fill_shifted(sh, ext)

        def du_chunk(r, carry):
            base = pl.multiple_of(r * CONV_CHUNK, CONV_CHUNK)
            _tap_sum(w_ref, sh, base, d, lambda k: CONV_WIDTH - 1 - k, du_scr)
            return carry

        lax.fori_loop(0, tm // CONV_CHUNK, du_chunk, 0)

        ext[0:CONV_HALO] = jnp.where(i == 0, 0.0, up_ref[...])
        ext[CONV_HALO:CONV_HALO + tm] = u_ref[...]
        _fill_shifted(sh, ext)
        def dw_chunk(r, carry):
            base = pl.multiple_of(r * CONV_CHUNK, CONV_CHUNK)
            groups = CONV_CHUNK // 8
            for lg in range(d // LANES):
                ls = slice(lg * LANES, (lg + 1) * LANES)
                dcv = dc_ref[pl.ds(base, CONV_CHUNK), ls].reshape(groups, 8, LANES)
                for k in range(CONV_WIDTH):
                    e = k + CONV_HALO - (CONV_WIDTH - 1)
                    x = sh[e % 8, pl.ds(base + (e // 8) * 8, CONV_CHUNK), ls].reshape(groups, 8, LANES)
                    acc_w[k, :, ls] += jnp.sum(dcv * x, axis=0)
            return carry

        lax.fori_loop(0, tm // CONV_CHUNK, dw_chunk, 0)

        du = du_scr[...]
        av = a_ref[...].astype(F32)
        sg = jax.nn.sigmoid(g_ref[...].astype(F32))
        da = du * sg
        dg = du * av * sg * (1.0 - sg)
        acc_b1[:, 0:d] += _colsum8(da)
        acc_b1[:, d:2 * d] += _colsum8(dg)
        dx = ALPHA * dz_ref[...]
        for j, part in enumerate([da[:, 0:dh_w], da[:, dh_w:d], dg[:, 0:dh_w], dg[:, dh_w:d]]):
            pb = part.astype(BF16)
            dh_ref[j] = pb
            dx = dx + _dot_nt(pb, w1_ref[j])
        dx_ref[...] = dx

        @pl.when(i == nt - 1)
        def _():
            db1_ref[...] = jnp.sum(acc_b1[...], axis=0, keepdims=True)
            dbdw_ref[...] = jnp.sum(acc_bdw[...], axis=0, keepdims=True)
            dw_ref[...] = jnp.sum(acc_w[...], axis=1)

    return _tc_call(
        body, name="bwd_conv_glu", nt=nt,
        in_specs=[_rows(tm, d), pl.BlockSpec((CONV_HALO, d), lambda i: (jnp.minimum((i + 1) * hb, last_halo), 0)),
                  _rows(tm, d), pl.BlockSpec((CONV_HALO, d), lambda i: (jnp.maximum(i * hb - 1, 0), 0)),
                  _rows(tm, d), _rows(tm, d), _rows(tm, d), _const((CONV_HALO, 8, d)), _const((4, d, dh_w))],
        out_specs=[_rows(tm, d), pl.BlockSpec((4, tm, dh_w), lambda i: (0, i, 0)), _acc_out((1, 2 * d)),
                   _acc_out((1, d)), _acc_out((CONV_HALO, d))],
        out_shape=[jax.ShapeDtypeStruct((t, d), F32), jax.ShapeDtypeStruct((4, t, dh_w), BF16),
                   jax.ShapeDtypeStruct((1, 2 * d), F32), jax.ShapeDtypeStruct((1, d), F32),
                   jax.ShapeDtypeStruct((CONV_HALO, d), F32)],
        scratch_shapes=[pltpu.VMEM((tm + CONV_HALO + 8, d), F32), pltpu.VMEM((8, tm + CONV_HALO, d), F32),
                        pltpu.VMEM((tm, d), F32), pltpu.VMEM((8, 2 * d), F32), pltpu.VMEM((8, d), F32),
                        pltpu.VMEM((CONV_HALO, 8, d), F32)],
        operands=(dc, dc, u, u, a, g, dz, wdw, w1s), riders=riders)


def _row_block(rows, target):
    best = rows
    for cand in range(8, min(rows, target) + 1, 8):
        if rows % cand == 0:
            best = cand
    return best


def _adamw(w, g, m, v, name):
    rows, lanes = w.shape
    br = _row_block(rows, 512) if rows % 8 == 0 else rows

    def body(w_ref, g_ref, m_ref, v_ref, d_ref, nm_ref, nv_ref):
        gv = g_ref[...]
        nm = ADAM_B1 * m_ref[...] + (1.0 - ADAM_B1) * gv
        nv = ADAM_B2 * v_ref[...] + (1.0 - ADAM_B2) * (gv * gv)
        m_hat = nm / (1.0 - ADAM_B1 ** ADAM_STEP)
        v_hat = nv / (1.0 - ADAM_B2 ** ADAM_STEP)
        d_ref[...] = -ADAM_LR * (m_hat / (jnp.sqrt(v_hat) + ADAM_EPS) + ADAM_WD * w_ref[...])
        nm_ref[...] = nm
        nv_ref[...] = nv

    spec = pl.BlockSpec((br, lanes), lambda i: (i, 0))
    return pl.pallas_call(
        body, name=name, grid=(rows // br,), in_specs=[spec] * 4, out_specs=[spec] * 3,
        out_shape=[jax.ShapeDtypeStruct((rows, lanes), F32)] * 3, compiler_params=_cparams(),
    )(w, g, m, v)


def _pad_to(v, n):
    return jnp.pad(v, (0, n - v.shape[0]))


def _round_up(n, m):
    return (n + m - 1) // m * m


def kernel(x, conv_w_pw1, conv_b_pw1, conv_w_dw, conv_b_dw, conv_ln_g, conv_ln_b, conv_w_pw2, conv_b_pw2, kv_w_k, kv_b_k, kv_w_v, kv_b_v, attn_w_q, attn_b_q, attn_sinks, attn_w_o, attn_b_o, ffn_w_gate, ffn_w_up, ffn_w_down, ln_mix_g, ln_mix_b, ln_ffn_g, ln_ffn_b, loss_target, m_conv_w_pw1, m_conv_b_pw1, m_conv_w_dw, m_conv_b_dw, m_conv_ln_g, m_conv_ln_b, m_conv_w_pw2, m_conv_b_pw2, m_kv_w_k, m_kv_b_k, m_kv_w_v, m_kv_b_v, m_attn_w_q, m_attn_b_q, m_attn_sinks, m_attn_w_o, m_attn_b_o, m_ffn_w_gate, m_ffn_w_up, m_ffn_w_down, m_ln_mix_g, m_ln_mix_b, m_ln_ffn_g, m_ln_ffn_b, v_conv_w_pw1, v_conv_b_pw1, v_conv_w_dw, v_conv_b_dw, v_conv_ln_g, v_conv_ln_b, v_conv_w_pw2, v_conv_b_pw2, v_kv_w_k, v_kv_b_k, v_kv_w_v, v_kv_b_v, v_attn_w_q, v_attn_b_q, v_attn_sinks, v_attn_w_o, v_attn_b_o, v_ffn_w_gate, v_ffn_w_up, v_ffn_w_down, v_ln_mix_g, v_ln_mix_b, v_ln_ffn_g, v_ln_ffn_b):
    args = dict(locals())
    w = {n: args[n] for n in WEIGHTS}
    mom = {n: args["m_" + n] for n in WEIGHTS}
    var = {n: args["v_" + n] for n in WEIGHTS}
    assert x.shape[0] == 1, "one sequence per device"
    t, d = x.shape[1], x.shape[2]
    dq = d // 4
    fs = ffn_w_gate.shape[-1]
    nq = d // HEAD_DIM
    x0 = x.reshape(t, d)
    target = loss_target.reshape(t, d)
    tm_big = min(512, t)
    tm_mid = min(256, t)
    tm_tn = min(1024, t)
    c_idx = lax.axis_index("c")

    me_idx = 2 * lax.axis_index("x") + lax.axis_index("y")

    def gather_buffer(v):
        buf = lax.empty((N_CHIPS,) + v.shape, v.dtype)
        return lax.dynamic_update_slice(buf, v[None], (me_idx,) + (0,) * v.ndim)

    def halves(v):
        return v.reshape(2, -1, v.shape[-1])

    small_sizes = [int(w[n].size) for n in SMALL_SHARDED]
    rs = _round_up(sum(small_sizes), 8 * 128) // 128
    spack = _pad_to(jnp.concatenate([w[n].reshape(-1) for n in SMALL_SHARDED]), rs * 128).reshape(rs, 128)
    conv_first = ['conv_w_pw1', 'conv_w_pw2']
    later = [n for n in BIG if n not in conv_first]
    (first_out,) = _run_riders(
        [_all_gather_rider([gather_buffer(halves(w[n].astype(BF16))) for n in conv_first], gather_buffer(spack))],
        "all_gather_conv")
    later_rider = _all_gather_rider([gather_buffer(halves(w[n].astype(BF16))) for n in later])
    gs = first_out[-1].reshape(N_CHIPS, rs * 128)
    full = {n: g.reshape((N_CHIPS,) + w[n].shape) for n, g in zip(conv_first, first_out)}
    off = 0
    for n, size in zip(SMALL_SHARDED, small_sizes):
        full[n] = gs[:, off:off + size].reshape((N_CHIPS,) + w[n].shape)
        off += size
    w1s = full['conv_w_pw1'].reshape(N_CHIPS, d, d // 2)
    w2 = full['conv_w_pw2'].reshape(d, d)
    b1 = full['conv_b_pw1'].reshape(1, 2 * d)
    wdw = jnp.pad(full['conv_w_dw'].reshape(N_CHIPS, CONV_WIDTH, dq).transpose(1, 0, 2).reshape(CONV_WIDTH, d),
                  ((0, CONV_HALO - CONV_WIDTH), (0, 0)))
    wdw = jnp.broadcast_to(wdw[:, None, :], (CONV_HALO, 8, d))
    bdw = full['conv_b_dw'].reshape(1, d)
    clng = full['conv_ln_g'].reshape(1, d)
    clnb = full['conv_ln_b'].reshape(1, d)
    b2 = full['conv_b_pw2'].reshape(1, d)
    bkv = jnp.concatenate([kv_b_k, kv_b_v]).reshape(1, 2 * KVD)
    sinks = attn_sinks.reshape(nq)
    mixg = [ln_mix_g[l].reshape(1, d) for l in range(DEPTH)]
    mixb = [ln_mix_b[l].reshape(1, d) for l in range(DEPTH)]
    ffng = [ln_ffn_g[l].reshape(1, d) for l in range(DEPTH)]
    ffnb = [ln_ffn_b[l].reshape(1, d) for l in range(DEPTH)]

    a_act, g_act, u_act = _fwd_pw1_glu(x0, w1s, b1, tm_big)
    (c_act, z1, x1), (later_out,) = _fwd_conv_tail(u_act, x0, wdw, bdw, clng, clnb, w2, b2, mixg[0], mixb[0], tm_mid,
                                                   riders=[later_rider])
    full.update({n: g.reshape((N_CHIPS,) + w[n].shape) for n, g in zip(later, later_out)})
    wkv = jnp.concatenate([full['kv_w_k'].reshape(d, KVD), full['kv_w_v'].reshape(d, KVD)], axis=1)
    wqt = full['attn_w_q'].reshape(d, d).T
    wo = full['attn_w_o'].reshape(d, d)
    wg, wu, wd = full['ffn_w_gate'], full['ffn_w_up'], full['ffn_w_down']
    gg0, uu0, z2, x2 = _fwd_ffn(x1, wg, wu, wd, 0, ffng[0], ffnb[0], tm_big)
    qt_act, kv_act, ot_act, z3, x3 = _fwd_attn(x2, wqt, attn_b_q.reshape(d, 1), wkv, bkv, sinks, wo, attn_b_o,
                                               mixg[1], mixb[1], tm_big)
    gg1, uu1, z4, x4 = _fwd_ffn(x3, wg, wu, wd, 1, ffng[1], ffnb[1], tm_big)
    dx4, loss_part = _loss_grad(x4, target, tm_big)
    loss = lax.psum(loss_part[0, 0], ("x", "y", "c"))

    c_arr = c_idx.reshape(1).astype(jnp.int32)

    def halves4(v):
        return v.reshape(N_CHIPS, 2, -1, v.shape[-1])

    def arrays(group):
        return [p for _, p in group]

    def pair_sums(group, got):
        return [_pair_sum(p, g, c_arr, "grad_pair_sum_" + n) for (n, p), g in zip(group, got)]

    pos_arr = jnp.stack([me_idx, c_idx]).astype(jnp.int32)

    def chip_sums(group, sums, got):
        return [_chip_sum(s, g, pos_arr, "grad_chip_sum_" + n) for (n, _), s, g in zip(group, sums, got)]

    (dz4, dgg1, duu1, hm1, dx3, d_fg1, d_fb1), _ = _bwd_ffn_dx(dx4, z4, gg1, uu1, wg, wu, wd, 1, ffng[1], tm_mid)
    g1 = [("ffn_w_gate1", halves4(_matmul_tn(x3[None], dgg1, tm_tn, "dw_gate1"))),
          ("ffn_w_up1", halves4(_matmul_tn(x3[None], duu1, tm_tn, "dw_up1"))),
          ("ffn_w_down1", halves4(_matmul_tn(hm1, dz4[None], tm_tn, "dw_down1")))]
    (dz3, dqt, dkv, dx2, d_mg1, d_mb1, d_bo, d_bq, d_bkv, d_sinks), (got1,) = _bwd_attn(
        dx3, z3, qt_act, kv_act, sinks, wo, wqt, wkv, mixg[1], tm_big, riders=[_pair_exchange_rider(arrays(g1))])
    s1 = pair_sums(g1, got1)
    dwo = _matmul_nn(ot_act, dz3, tm_tn, "dw_o")
    dwq = _matmul_nn(dqt, x2, tm_tn, "dw_q").T
    dwkv = _matmul_tn(x2[None], dkv[None], tm_tn, "dw_kv")[0]
    g2 = [("attn_w_o", halves4(dwo)), ("attn_w_q", halves4(dwq)),
          ("kv_w_k", halves4(dwkv[:, 0:KVD])), ("kv_w_v", halves4(dwkv[:, KVD:2 * KVD]))]
    (dz2, dgg0, duu0, hm0, dx1, d_fg0, d_fb0), (from_chips1, got2) = _bwd_ffn_dx(
        dx2, z2, gg0, uu0, wg, wu, wd, 0, ffng[0], tm_mid,
        riders=[_chip_scatter_rider(s1), _pair_exchange_rider(arrays(g2))])
    f1 = chip_sums(g1, s1, from_chips1)
    s2 = pair_sums(g2, got2)
    g3 = [("ffn_w_gate0", halves4(_matmul_tn(x1[None], dgg0, tm_tn, "dw_gate0"))),
          ("ffn_w_up0", halves4(_matmul_tn(x1[None], duu0, tm_tn, "dw_up0"))),
          ("ffn_w_down0", halves4(_matmul_tn(hm0, dz2[None], tm_tn, "dw_down0")))]
    (dz1, s_act, dc, d_mg0, d_mb0, d_b2, d_clng, d_clnb), (got3, shared1) = _bwd_conv_head(
        dx1, z1, c_act, w2, mixg[0], clng, clnb, tm_mid,
        riders=[_pair_exchange_rider(arrays(g3)), _pair_share_rider(f1)])
    s3 = pair_sums(g3, got3)
    dw2 = _matmul_tn(s_act[None], dz1[None], tm_tn, "dw_pw2")
    (dx0, dh1, d_b1, d_bdw, d_wdw), (from_chips2, from_chips3) = _bwd_conv_glu(
        dc, u_act, a_act, g_act, dz1, wdw, w1s, tm_mid, riders=[_chip_scatter_rider(s2), _chip_scatter_rider(s3)])
    f2 = chip_sums(g2, s2, from_chips2)
    f3 = chip_sums(g3, s3, from_chips3)
    dw1 = _matmul_tn(x0[None], dh1, tm_tn, "dw_pw1")

    def rows4(v):
        return v.reshape(N_CHIPS, -1)

    def rep4(v):
        return jnp.broadcast_to(v.reshape(1, -1), (N_CHIPS, v.size))

    local = {
        'conv_b_pw1': rows4(d_b1),
        'conv_w_dw': rows4(d_wdw[0:CONV_WIDTH].reshape(CONV_WIDTH, N_CHIPS, dq).transpose(1, 0, 2)),
        'conv_b_dw': rows4(d_bdw), 'conv_ln_g': rows4(d_clng), 'conv_ln_b': rows4(d_clnb), 'conv_b_pw2': rows4(d_b2),
        'kv_b_k': rep4(d_bkv[:, 0:KVD]), 'kv_b_v': rep4(d_bkv[:, KVD:2 * KVD]), 'attn_b_q': rep4(d_bq),
        'attn_sinks': rep4(d_sinks), 'attn_b_o': rep4(d_bo),
        'ln_mix_g': rep4(jnp.concatenate([d_mg0, d_mg1])), 'ln_mix_b': rep4(jnp.concatenate([d_mb0, d_mb1])),
        'ln_ffn_g': rep4(jnp.concatenate([d_fg0, d_fg1])), 'ln_ffn_b': rep4(jnp.concatenate([d_fb0, d_fb1])),
    }
    n_small = sum(int(w[n].size) for n in SMALL)
    small_rows = _round_up(n_small, 2 * 8 * 128) // 128
    small_local = jnp.concatenate([local[n] for n in SMALL], axis=1)
    small_local = jnp.pad(small_local, ((0, 0), (0, small_rows * 128 - n_small)))
    g4 = [("conv_w_pw1", halves4(dw1)), ("conv_w_pw2", halves4(dw2)),
          ("small", small_local.reshape(N_CHIPS, 2, small_rows // 2, 128))]
    (got4,) = _run_riders([_pair_exchange_rider(arrays(g4))], "grad_pair_exchange_last")
    s4 = pair_sums(g4, got4)
    (from_chips4,) = _run_riders([_chip_scatter_rider(s4)], "grad_chip_scatter_last")
    f4 = chip_sums(g4, s4, from_chips4)
    (shared_rest,) = _run_riders([_pair_share_rider(f2 + f3 + f4)], "grad_pair_share_last")
    reduced = dict(zip([n for n, _ in g1], shared1))
    reduced.update(zip([n for n, _ in g2 + g3 + g4], shared_rest))
    for n in ('ffn_w_gate', 'ffn_w_up', 'ffn_w_down'):
        reduced[n] = jnp.stack([reduced[n + str(layer)].reshape(w[n].shape[1:]) for layer in range(DEPTH)])

    g_out, delta, new_m, new_v = {}, {}, {}, {}
    for n in BIG:
        shape = w[n].shape
        two_d = (-1, shape[-1])
        g_out[n] = reduced[n].reshape(shape)
        dl, nm, nv = _adamw(w[n].reshape(two_d), g_out[n].reshape(two_d), mom[n].reshape(two_d), var[n].reshape(two_d),
                            "adamw_" + n)
        delta[n], new_m[n], new_v[n] = dl.reshape(shape), nm.reshape(shape), nv.reshape(shape)

    def pack_small(tree):
        return _pad_to(jnp.concatenate([tree[n].reshape(-1) for n in SMALL]), small_rows * 128).reshape(small_rows, 128)

    g_small = reduced['small'].reshape(small_rows, 128)
    dl, nm, nv = _adamw(pack_small(w), g_small, pack_small(mom), pack_small(var), "adamw_small")
    off = 0
    for n in SMALL:
        size, shape = int(w[n].size), w[n].shape
        for tree, flat in ((g_out, g_small), (delta, dl), (new_m, nm), (new_v, nv)):
            tree[n] = flat.reshape(-1)[off:off + size].reshape(shape)
        off += size

    return (loss, dx0.reshape(x.shape), *[g_out[n] for n in WEIGHTS], *[delta[n] for n in WEIGHTS],
            *[new_m[n] for n in WEIGHTS], *[new_v[n] for n in WEIGHTS])
```

```python
import functools
import math

import jax
import jax.numpy as jnp
from jax import lax
from jax.experimental import pallas as pl
from jax.experimental.pallas import tpu as pltpu

F32 = jnp.float32
BF16 = jnp.bfloat16

DEPTH = 2
ALPHA = (2.0 * DEPTH) ** 0.25
LN_EPS = 1e-5
NEG_INF = -1e30
HEAD_DIM = 64
N_KV_HEADS = 2
KVD = N_KV_HEADS * HEAD_DIM
BLOCK = 128
CONV_WIDTH = 31
CONV_HALO = 32
ALIBI_MAX = 8.0
ADAM_LR, ADAM_B1, ADAM_B2, ADAM_EPS, ADAM_WD, ADAM_STEP = 0.001, 0.9, 0.999, 1e-08, 0.01, 10

N_CHIPS = 4
PACK_ROWS = 256
VMEM_LIMIT = 56 * 1024 * 1024
MESH = pl.DeviceIdType.MESH

NT_DIMS = (((1,), (1,)), ((), ()))
TN_DIMS = (((0,), (0,)), ((), ()))

WEIGHTS = ['conv_w_pw1', 'conv_b_pw1', 'conv_w_dw', 'conv_b_dw', 'conv_ln_g', 'conv_ln_b', 'conv_w_pw2', 'conv_b_pw2',
           'kv_w_k', 'kv_b_k', 'kv_w_v', 'kv_b_v', 'attn_w_q', 'attn_b_q', 'attn_sinks', 'attn_w_o', 'attn_b_o',
           'ffn_w_gate', 'ffn_w_up', 'ffn_w_down', 'ln_mix_g', 'ln_mix_b', 'ln_ffn_g', 'ln_ffn_b']
BIG = ['conv_w_pw1', 'conv_w_pw2', 'kv_w_k', 'kv_w_v', 'attn_w_q', 'attn_w_o', 'ffn_w_gate', 'ffn_w_up', 'ffn_w_down']
SMALL_SHARDED = ['conv_b_pw1', 'conv_w_dw', 'conv_b_dw', 'conv_ln_g', 'conv_ln_b', 'conv_b_pw2']
REPLICATED = ['kv_b_k', 'kv_b_v', 'attn_b_q', 'attn_sinks', 'attn_b_o', 'ln_mix_g', 'ln_mix_b', 'ln_ffn_g', 'ln_ffn_b']
SMALL = SMALL_SHARDED + REPLICATED


def _cparams(n_grid=1):
    return pltpu.CompilerParams(dimension_semantics=("arbitrary",) * n_grid, vmem_limit_bytes=VMEM_LIMIT)


def _rows(tm, width):
    return pl.BlockSpec((tm, width), lambda i: (i, 0))


def _const(shape):
    return pl.BlockSpec(shape, lambda *_: (0,) * len(shape), pipeline_mode=pl.Buffered(1))


def _acc_out(shape):
    return pl.BlockSpec(shape, lambda *_: (0,) * len(shape))


def _dot(a, b):
    return jnp.dot(a, b, preferred_element_type=F32)


def _dot_nt(a, b):
    return lax.dot_general(a, b, NT_DIMS, preferred_element_type=F32)


def _dot_tn(a, b):
    return lax.dot_general(a, b, TN_DIMS, preferred_element_type=F32)


def _colsum8(v):
    m, n = v.shape
    return jnp.sum(v.reshape(m // 8, 8, n), axis=0)


def _ln_stats(z):
    mu = jnp.mean(z, axis=-1, keepdims=True)
    zc = z - mu
    var = jnp.mean(zc * zc, axis=-1, keepdims=True)
    rstd = lax.rsqrt(var + LN_EPS)
    return zc * rstd, rstd


def _ln_fwd(z, g, b):
    zhat, _ = _ln_stats(z)
    return zhat * g + b


def _ln_bwd(dy, z, g):
    zhat, rstd = _ln_stats(z)
    dzh = dy * g
    m1 = jnp.mean(dzh, axis=-1, keepdims=True)
    m2 = jnp.mean(dzh * zhat, axis=-1, keepdims=True)
    return rstd * (dzh - m1 - zhat * m2), zhat


def _silu_and_grad(n):
    sg = jax.nn.sigmoid(n)
    return n * sg, sg * (1.0 + n * (1.0 - sg))


def _acc_init(i, *refs):
    @pl.when(i == 0)
    def _():
        for r in refs:
            r[...] = jnp.zeros_like(r)


def _mesh_pos():
    x, y, c = lax.axis_index("x"), lax.axis_index("y"), lax.axis_index("c")
    chips = [(1 - x, y), (x, 1 - y), (1 - x, 1 - y)]
    return x, y, c, chips


HBM_SPEC = pl.BlockSpec(memory_space=pltpu.HBM)


def _remote(src, dst, send_sems, recv_sems, k, to):
    return pltpu.make_async_remote_copy(src_ref=src, dst_ref=dst, send_sem=send_sems.at[k], recv_sem=recv_sems.at[k],
                                        device_id=to, device_id_type=MESH)


class _Rider:
    def __init__(self, operands, out_shapes, sem_shapes, start, finish, mid=None, in_place=False):
        self.operands, self.out_shapes, self.sem_shapes = list(operands), list(out_shapes), list(sem_shapes)
        self.start, self.finish, self.mid = start, finish, mid
        self.in_place = in_place


def _rider_aliases(riders, first_in, first_out):
    aliases, k_in, k_out = {}, first_in, first_out
    for r in riders:
        if r.in_place:
            aliases.update({k_in + k: k_out + k for k in range(len(r.operands))})
        k_in += len(r.operands)
        k_out += len(r.out_shapes)
    return aliases


def _split(refs, counts):
    parts, k = [], 0
    for n in counts:
        parts.append(refs[k:k + n])
        k += n
    return parts


def _rider_refs(riders, ins, outs, sems):
    return list(zip(riders, _split(ins, [len(r.operands) for r in riders]),
                    _split(outs, [len(r.out_shapes) for r in riders]),
                    _split(sems, [len(r.sem_shapes) for r in riders])))


def _tc_call(body, *, name, nt, in_specs, out_specs, out_shape, operands, scratch_shapes=(), riders=(), mid_frac=0.75):
    n_in, n_out, n_scr = len(in_specs), len(out_specs), len(scratch_shapes)
    r_ops = [o for r in riders for o in r.operands]
    r_outs = [o for r in riders for o in r.out_shapes]
    r_sems = [s for r in riders for s in r.sem_shapes]
    mid_step = min(max(int(nt * mid_frac), 0), nt - 1)

    def full(*refs):
        ins, r_in, outs, r_out, scr, r_sem = _split(refs, [n_in, len(r_ops), n_out, len(r_outs), n_scr, len(r_sems)])
        parts = _rider_refs(riders, r_in, r_out, r_sem)
        step = pl.program_id(0)

        @pl.when(step == 0)
        def _():
            for r, a, b, s in parts:
                r.start(a, b, s)

        body(*ins, *outs, *scr)

        @pl.when(step == mid_step)
        def _():
            for r, a, b, s in parts:
                if r.mid is not None:
                    r.mid(a, b, s)

        @pl.when(step == nt - 1)
        def _():
            for r, a, b, s in parts:
                r.finish(a, b, s)

    res = pl.pallas_call(
        full if riders else body, name=name, grid=(nt,), in_specs=list(in_specs) + [HBM_SPEC] * len(r_ops),
        out_specs=list(out_specs) + [HBM_SPEC] * len(r_outs), out_shape=list(out_shape) + r_outs,
        scratch_shapes=list(scratch_shapes) + r_sems, input_output_aliases=_rider_aliases(riders, n_in, n_out),
        compiler_params=_cparams(),
    )(*operands, *r_ops)
    return res[:n_out], _split(res[n_out:], [len(r.out_shapes) for r in riders])


def _run_riders(riders, name):
    r_ops = [o for r in riders for o in r.operands]
    r_outs = [o for r in riders for o in r.out_shapes]
    r_sems = [s for r in riders for s in r.sem_shapes]

    def body(*refs):
        r_in, r_out, r_sem = _split(refs, [len(r_ops), len(r_outs), len(r_sems)])
        parts = _rider_refs(riders, r_in, r_out, r_sem)
        for r, a, b, s in parts:
            r.start(a, b, s)
        for r, a, b, s in parts:
            if r.mid is not None:
                r.mid(a, b, s)
        for r, a, b, s in parts:
            r.finish(a, b, s)

    res = pl.pallas_call(body, name=name, out_shape=tuple(r_outs), in_specs=[HBM_SPEC] * len(r_ops),
                         out_specs=(HBM_SPEC,) * len(r_outs), scratch_shapes=r_sems,
                         input_output_aliases=_rider_aliases(riders, 0, 0))(*r_ops)
    return _split(list(res), [len(r.out_shapes) for r in riders])


def _all_gather_rider(bufs, small=None):
    n = len(bufs)
    n_small = 0 if small is None else 1

    def copies(outs, sems):
        send_sems, recv_sems = sems
        x, y, c, chips = _mesh_pos()
        me = 2 * x + y
        here, sibling = (x, y, c), (x, y, 1 - c)
        rows = [2 * cx + cy for cx, cy in chips]

        def big(p, k, chip_row, half, to):
            piece = outs[p].at[chip_row, half]
            return _remote(piece, piece, send_sems, recv_sems, 6 * p + k, to)

        first = [big(p, j, me, c, (cx, cy, c)) for p in range(n) for j, (cx, cy) in enumerate(chips)]
        landed = [big(p, j, rows[j], c, here) for p in range(n) for j in range(3)]
        passed = [big(p, 3 + j, rows[j], c, sibling) for p in range(n) for j in range(3)]
        arrivals = [big(p, 3 + j, rows[j], 1 - c, here) for p in range(n) for j in range(3)]
        if n_small:
            first = [_remote(outs[n].at[me], outs[n].at[me], send_sems, recv_sems, 6 * n + j, (cx, cy, c))
                     for j, (cx, cy) in enumerate(chips)] + first
            arrivals += [_remote(outs[n].at[rows[j]], outs[n].at[rows[j]], send_sems, recv_sems, 6 * n + j, here)
                         for j in range(3)]
        return first, landed, passed, arrivals

    def start(ins, outs, sems):
        for cp in copies(outs, sems)[0]:
            cp.start()

    def mid(ins, outs, sems):
        _, landed, passed, _ = copies(outs, sems)
        for got, fwd in zip(landed, passed):
            got.wait_recv()
            fwd.start()

    def finish(ins, outs, sems):
        first, _, passed, arrivals = copies(outs, sems)
        for cp in arrivals:
            cp.wait_recv()
        for cp in first + passed:
            cp.wait_send()

    operands = list(bufs) + ([small] if n_small else [])
    n_sem = 6 * n + 3 * n_small
    return _Rider(operands, [jax.ShapeDtypeStruct(o.shape, o.dtype) for o in operands],
                  [pltpu.SemaphoreType.DMA((n_sem,)), pltpu.SemaphoreType.DMA((n_sem,))], start, finish, mid,
                  in_place=True)


def _pair_exchange_rider(plist):
    n = len(plist)

    def copies(ins, outs, sems):
        x, y, c, _ = _mesh_pos()
        return [_remote(ins[k].at[:, 1 - c], outs[k], sems[0], sems[1], k, (x, y, 1 - c)) for k in range(n)]

    def start(ins, outs, sems):
        for cp in copies(ins, outs, sems):
            cp.start()

    def finish(ins, outs, sems):
        for cp in copies(ins, outs, sems):
            cp.wait()

    return _Rider(plist, [jax.ShapeDtypeStruct((p.shape[0],) + p.shape[2:], p.dtype) for p in plist],
                  [pltpu.SemaphoreType.DMA((n,)), pltpu.SemaphoreType.DMA((n,))], start, finish)


def _pair_sum(p, got, c, name):
    n, _, r, l = p.shape
    br = _row_block(r, PACK_ROWS)

    def body(c_ref, p_ref, got_ref, out_ref):
        out_ref[...] = p_ref[...] + got_ref[...]

    return pl.pallas_call(
        body, name=name, out_shape=jax.ShapeDtypeStruct((n, r, l), F32),
        grid_spec=pltpu.PrefetchScalarGridSpec(
            num_scalar_prefetch=1, grid=(n, r // br),
            in_specs=[pl.BlockSpec((None, None, br, l), lambda j, i, c_ref: (j, c_ref[0], i, 0)),
                      pl.BlockSpec((None, br, l), lambda j, i, c_ref: (j, i, 0))],
            out_specs=pl.BlockSpec((None, br, l), lambda j, i, c_ref: (j, i, 0))),
        compiler_params=_cparams(2),
    )(c, p, got)


def _chip_scatter_rider(slist):
    n = len(slist)

    def copies(ins, outs, sems):
        send_sems, recv_sems = sems
        x, y, c, chips = _mesh_pos()
        sends = [_remote(ins[k].at[2 * cx + cy], outs[k].at[j], send_sems, recv_sems, 3 * k + j, (cx, cy, c))
                 for k in range(n) for j, (cx, cy) in enumerate(chips)]
        arrivals = [_remote(ins[k].at[0], outs[k].at[j], send_sems, recv_sems, 3 * k + j, (x, y, c))
                    for k in range(n) for j in range(3)]
        return sends, arrivals

    def start(ins, outs, sems):
        for cp in copies(ins, outs, sems)[0]:
            cp.start()

    def finish(ins, outs, sems):
        sends, arrivals = copies(ins, outs, sems)
        for cp in arrivals:
            cp.wait_recv()
        for cp in sends:
            cp.wait_send()

    return _Rider(slist, [jax.ShapeDtypeStruct((3,) + s.shape[1:], s.dtype) for s in slist],
                  [pltpu.SemaphoreType.DMA((3 * n,)), pltpu.SemaphoreType.DMA((3 * n,))], start, finish)


def _chip_sum(s, got, pos, name):
    _, r, l = s.shape
    br = _row_block(r, PACK_ROWS)

    def body(pos_ref, s_ref, got_ref, out_ref):
        me = pos_ref[0]
        total = None
        for chip in range(N_CHIPS):
            flip = jnp.bitwise_xor(me, chip)
            term = jnp.where(flip == 0, s_ref[...],
                             jnp.where(flip == 2, got_ref[0], jnp.where(flip == 1, got_ref[1], got_ref[2])))
            total = term if total is None else total + term
        out_ref[...] = total

    return pl.pallas_call(
        body, name=name, out_shape=jax.ShapeDtypeStruct((2, r, l), F32),
        grid_spec=pltpu.PrefetchScalarGridSpec(
            num_scalar_prefetch=1, grid=(r // br,),
            in_specs=[pl.BlockSpec((None, br, l), lambda i, pos_ref: (pos_ref[0], i, 0)),
                      pl.BlockSpec((3, br, l), lambda i, pos_ref: (0, i, 0))],
            out_specs=pl.BlockSpec((None, br, l), lambda i, pos_ref: (pos_ref[1], i, 0))),
        compiler_params=_cparams(1),
    )(pos, s, got)


def _pair_share_rider(flist):
    n = len(flist)

    def copies(outs, sems):
        x, y, c, _ = _mesh_pos()
        sends = [_remote(outs[k].at[c], outs[k].at[c], sems[0], sems[1], k, (x, y, 1 - c)) for k in range(n)]
        arrivals = [_remote(outs[k].at[1 - c], outs[k].at[1 - c], sems[0], sems[1], k, (x, y, c)) for k in range(n)]
        return sends, arrivals

    def start(ins, outs, sems):
        for cp in copies(outs, sems)[0]:
            cp.start()

    def finish(ins, outs, sems):
        sends, arrivals = copies(outs, sems)
        for cp in arrivals:
            cp.wait_recv()
        for cp in sends:
            cp.wait_send()

    return _Rider(flist, [jax.ShapeDtypeStruct(f.shape, f.dtype) for f in flist],
                  [pltpu.SemaphoreType.DMA((n,)), pltpu.SemaphoreType.DMA((n,))], start, finish, in_place=True)


def _fwd_pw1_glu(x, w1s, b1, tm):
    t, d = x.shape
    dh = d // 2

    def body(x_ref, w_ref, b_ref, a_ref, g_ref, u_ref):
        xb = x_ref[...].astype(BF16)
        for hh in range(2):
            cs = slice(hh * dh, (hh + 1) * dh)
            a = _dot(xb, w_ref[hh]) + b_ref[:, hh * dh:(hh + 1) * dh]
            g = _dot(xb, w_ref[2 + hh]) + b_ref[:, d + hh * dh:d + (hh + 1) * dh]
            a_ref[:, cs] = a.astype(BF16)
            g_ref[:, cs] = g.astype(BF16)
            u_ref[:, cs] = a * jax.nn.sigmoid(g)

    return pl.pallas_call(
        body, name="fwd_pw1_glu", grid=(t // tm,),
        in_specs=[_rows(tm, d), _const((4, d, dh)), _const((1, 2 * d))],
        out_specs=[_rows(tm, d)] * 3,
        out_shape=[jax.ShapeDtypeStruct((t, d), BF16), jax.ShapeDtypeStruct((t, d), BF16),
                   jax.ShapeDtypeStruct((t, d), F32)],
        compiler_params=_cparams(),
    )(x, w1s, b1)


def _fill_shifted(sh_ref, ext_ref):
    n = sh_ref.shape[1]
    for s in range(8):
        sh_ref[s] = ext_ref[pl.ds(s, n), :]


CONV_CHUNK = 64
LANES = 256


def _tap_sum(w_ref, sh, base, d, tap_row, out_ref, bias_ref=None):
    groups = CONV_CHUNK // 8
    for lg in range(d // LANES):
        ls = slice(lg * LANES, (lg + 1) * LANES)
        acc = jnp.zeros((groups, 8, LANES), F32)
        for k in range(CONV_WIDTH):
            e = tap_row(k)
            x = sh[e % 8, pl.ds(base + (e // 8) * 8, CONV_CHUNK), ls]
            acc = acc + w_ref[k, :, ls] * x.reshape(groups, 8, LANES)
        acc = acc.reshape(CONV_CHUNK, LANES)
        out_ref[pl.ds(base, CONV_CHUNK), ls] = acc if bias_ref is None else acc + bias_ref[:, ls]


def _fwd_conv_tail(u, x0, wdw, bdw, lng, lnb, w2, b2, mixg, mixb, tm, riders=()):
    t, d = u.shape
    hb = tm // CONV_HALO

    def body(u_ref, uh_ref, x_ref, w_ref, bdw_ref, lng_ref, lnb_ref, w2_ref, b2_ref, mg_ref, mb_ref,
             c_ref, z_ref, y_ref, ext, sh):
        i = pl.program_id(0)
        ext[0:CONV_HALO] = jnp.where(i == 0, 0.0, uh_ref[...])
        ext[CONV_HALO:CONV_HALO + tm] = u_ref[...]
        ext[CONV_HALO + tm:CONV_HALO + tm + 8] = jnp.zeros((8, d), F32)
        _fill_shifted(sh, ext)

        def chunk(r, carry):
            base = pl.multiple_of(r * CONV_CHUNK, CONV_CHUNK)
            _tap_sum(w_ref, sh, base, d, lambda k: k + CONV_HALO - (CONV_WIDTH - 1), c_ref, bdw_ref)
            return carry

        lax.fori_loop(0, tm // CONV_CHUNK, chunk, 0)
        n = _ln_fwd(c_ref[...], lng_ref[...], lnb_ref[...])
        s = n * jax.nn.sigmoid(n)
        m = _dot(s.astype(BF16), w2_ref[...]) + b2_ref[...]
        z = ALPHA * x_ref[...] + m
        z_ref[...] = z
        y_ref[...] = _ln_fwd(z, mg_ref[...], mb_ref[...])

    vec = _const((1, d))
    return _tc_call(
        body, name="fwd_conv_tail", nt=t // tm,
        in_specs=[_rows(tm, d), pl.BlockSpec((CONV_HALO, d), lambda i: (jnp.maximum(i * hb - 1, 0), 0)), _rows(tm, d),
                  _const((CONV_HALO, 8, d)), vec, vec, vec, _const((d, d)), vec, vec, vec],
        out_specs=[_rows(tm, d)] * 3,
        out_shape=[jax.ShapeDtypeStruct((t, d), F32)] * 3,
        scratch_shapes=[pltpu.VMEM((tm + CONV_HALO + 8, d), F32), pltpu.VMEM((8, tm + CONV_HALO, d), F32)],
        operands=(u, u, x0, wdw, bdw, lng, lnb, w2, b2, mixg, mixb), riders=riders)


def _fwd_ffn(x, wg, wu, wd, layer, lng, lnb, tm):
    t, d = x.shape
    fs = wg.shape[-1]

    def body(x_ref, wg_ref, wu_ref, wd_ref, g_ref, b_ref, gg_ref, uu_ref, z_ref, y_ref):
        xv = x_ref[...]
        xb = xv.astype(BF16)
        f = jnp.zeros((tm, d), F32)
        for j in range(N_CHIPS):
            gj = _dot(xb, wg_ref[j])
            uj = _dot(xb, wu_ref[j])
            gg_ref[j] = gj.astype(BF16)
            uu_ref[j] = uj.astype(BF16)
            hm = gj * jax.nn.sigmoid(gj) * uj
            f = f + _dot(hm.astype(BF16), wd_ref[j])
        z = ALPHA * xv + f
        z_ref[...] = z
        y_ref[...] = _ln_fwd(z, g_ref[...], b_ref[...])

    wcol = pl.BlockSpec((N_CHIPS, None, d, fs), lambda i: (0, layer, 0, 0), pipeline_mode=pl.Buffered(1))
    wrow = pl.BlockSpec((N_CHIPS, None, fs, d), lambda i: (0, layer, 0, 0), pipeline_mode=pl.Buffered(1))
    hid = pl.BlockSpec((N_CHIPS, tm, fs), lambda i: (0, i, 0))
    return pl.pallas_call(
        body, name=f"fwd_ffn{layer}", grid=(t // tm,),
        in_specs=[_rows(tm, d), wcol, wcol, wrow, _const((1, d)), _const((1, d))],
        out_specs=[hid, hid, _rows(tm, d), _rows(tm, d)],
        out_shape=[jax.ShapeDtypeStruct((N_CHIPS, t, fs), BF16)] * 2 + [jax.ShapeDtypeStruct((t, d), F32)] * 2,
        compiler_params=_cparams(),
    )(x, wg, wu, wd, lng, lnb)


def _attn_band():
    kt = lax.broadcasted_iota(jnp.int32, (BLOCK, BLOCK), 0)
    qi = lax.broadcasted_iota(jnp.int32, (BLOCK, BLOCK), 1)
    current = kt <= qi
    delta = qi - kt + jnp.where(current, 0, BLOCK)
    return current, delta.astype(F32)


def _fold(full, current):
    return jnp.where(current, full[BLOCK:2 * BLOCK], full[0:BLOCK])


def _unfold(folded, current):
    zero = jnp.zeros_like(folded)
    return jnp.concatenate([jnp.where(current, zero, folded), jnp.where(current, folded, zero)], axis=0)


def _slope(h, nq):
    return 2.0 ** (-ALIBI_MAX * (h + 1) / nq)


def _softmax_with_sink(s_full, slope, band, has_previous, sink):
    current, delta = band
    s = _fold(s_full, current) * (1.0 / math.sqrt(HEAD_DIM)) - slope * delta
    s = jnp.where(jnp.logical_or(current, has_previous), s, NEG_INF)
    m = jnp.maximum(jnp.max(s, axis=0, keepdims=True), sink)
    p = jnp.exp(s - m)
    e_sink = jnp.exp(sink - m)
    inv = 1.0 / (jnp.sum(p, axis=0, keepdims=True) + e_sink)
    return p * inv, e_sink * inv


def _heads_on_lanes(ref, b, g, group):
    first = g * group
    return jnp.concatenate([ref[b, (first + hh) * HEAD_DIM:(first + hh + 1) * HEAD_DIM, :] for hh in range(group)],
                           axis=1)


def _fill_kv(kv_scr, halo, tile, tm):
    for j in range(2 * N_KV_HEADS):
        kv_scr[j, 0:BLOCK] = halo[:, j * HEAD_DIM:(j + 1) * HEAD_DIM]
        kv_scr[j, BLOCK:BLOCK + tm] = tile[:, j * HEAD_DIM:(j + 1) * HEAD_DIM]


def _cols(d, tm):
    return pl.BlockSpec((d, tm), lambda i: (0, i))


def _fwd_attn(x, wqt, bqt, wkv, bkv, sinks, wo, bo, mixg, mixb, tm):
    t, d = x.shape
    nq = d // HEAD_DIM
    group = nq // N_KV_HEADS
    nb = tm // BLOCK

    def body(sink_ref, x_ref, xh_ref, wqt_ref, bqt_ref, wkv_ref, bkv_ref, wo_ref, bo_ref, mg_ref, mb_ref,
             qt_ref, kv_ref, ot_ref, z_ref, y_ref, kv_scr, qt_scr, ot_scr):
        i = pl.program_id(0)
        xv = x_ref[...]
        xb = xv.astype(BF16)
        qt = (_dot_nt(wqt_ref[...], xb) + bqt_ref[...]).astype(BF16)
        qt_ref[...] = qt
        for b in range(nb):
            qt_scr[b] = qt[:, b * BLOCK:(b + 1) * BLOCK]
        kvb = (_dot(xb, wkv_ref[...]) + bkv_ref[...]).astype(BF16)
        kv_ref[...] = kvb
        _fill_kv(kv_scr, (_dot(xh_ref[...].astype(BF16), wkv_ref[...]) + bkv_ref[...]).astype(BF16), kvb, tm)
        band = _attn_band()

        def block(b, carry):
            r0 = pl.multiple_of(b * BLOCK, BLOCK)
            has_previous = jnp.logical_or(i > 0, b > 0)
            for g in range(N_KV_HEADS):
                kk = kv_scr[g, pl.ds(r0, 2 * BLOCK), :]
                vv = kv_scr[N_KV_HEADS + g, pl.ds(r0, 2 * BLOCK), :]
                s_all = _dot(kk, _heads_on_lanes(qt_scr, b, g, group))
                probs = []
                for hh in range(group):
                    h = g * group + hh
                    p, _ = _softmax_with_sink(s_all[:, hh * BLOCK:(hh + 1) * BLOCK], _slope(h, nq), band,
                                              has_previous, sink_ref[h])
                    probs.append(_unfold(p.astype(BF16), band[0]))
                o_all = _dot_tn(vv, jnp.concatenate(probs, axis=1))
                for hh in range(group):
                    h = g * group + hh
                    ot_scr[b, h * HEAD_DIM:(h + 1) * HEAD_DIM, :] = o_all[:, hh * BLOCK:(hh + 1) * BLOCK].astype(BF16)
            return carry

        lax.fori_loop(0, nb, block, 0)
        ot = jnp.concatenate([ot_scr[b] for b in range(nb)], axis=1)
        ot_ref[...] = ot
        z = ALPHA * xv + _dot_tn(ot, wo_ref[...]) + bo_ref[...]
        z_ref[...] = z
        y_ref[...] = _ln_fwd(z, mg_ref[...], mb_ref[...])

    hb = tm // BLOCK
    vec = _const((1, d))
    return pl.pallas_call(
        body, name="fwd_attn", grid=(t // tm,),
        in_specs=[pl.BlockSpec(memory_space=pltpu.SMEM),
                  _rows(tm, d), pl.BlockSpec((BLOCK, d), lambda i: (jnp.maximum(i * hb - 1, 0), 0)),
                  _const((d, d)), _const((d, 1)), _const((d, 2 * KVD)), _const((1, 2 * KVD)), _const((d, d)), vec, vec,
                  vec],
        out_specs=[_cols(d, tm), _rows(tm, 2 * KVD), _cols(d, tm), _rows(tm, d), _rows(tm, d)],
        out_shape=[jax.ShapeDtypeStruct((d, t), BF16), jax.ShapeDtypeStruct((t, 2 * KVD), BF16),
                   jax.ShapeDtypeStruct((d, t), BF16), jax.ShapeDtypeStruct((t, d), F32),
                   jax.ShapeDtypeStruct((t, d), F32)],
        scratch_shapes=[pltpu.VMEM((2 * N_KV_HEADS, tm + BLOCK, HEAD_DIM), BF16), pltpu.VMEM((nb, d, BLOCK), BF16),
                        pltpu.VMEM((nb, d, BLOCK), BF16)],
        compiler_params=_cparams(),
    )(sinks, x, x, wqt, bqt, wkv, bkv, wo, bo, mixg, mixb)


def _write_sums(i, nt, pairs):
    @pl.when(i == nt - 1)
    def _():
        for out_ref, acc in pairs:
            out_ref[...] = jnp.sum(acc[...], axis=0, keepdims=True)


def _bwd_ffn_dx(dy, z, gg, uu, wg, wu, wd, layer, lng, tm, riders=(), target=None):
    t, d = dy.shape
    fs = wg.shape[-1]
    nt = t // tm
    with_loss = target is not None

    def body(dy_ref, t_ref, z_ref, gg_ref, uu_ref, wg_ref, wu_ref, wd_ref, g_ref,
             dz_ref, dgg_ref, duu_ref, hm_ref, dx_ref, dlg_ref, dlb_ref, loss_ref, acc_g, acc_b, acc_l):
        i = pl.program_id(0)
        _acc_init(i, acc_g, acc_b, acc_l)
        dyv = dy_ref[...]
        if with_loss:
            err = dyv - t_ref[...]
            acc_l[...] += _colsum8(err * err)
            dyv = err * (1.0 / d)
        dz, zhat = _ln_bwd(dyv, z_ref[...], g_ref[...])
        acc_g[...] += _colsum8(dyv * zhat)
        acc_b[...] += _colsum8(dyv)
        dzb = dz.astype(BF16)
        dz_ref[...] = dzb
        dx = ALPHA * dz
        for j in range(N_CHIPS):
            dh = _dot_nt(dzb, wd_ref[j])
            gj = gg_ref[j].astype(F32)
            uj = uu_ref[j].astype(F32)
            act, dact = _silu_and_grad(gj)
            hm_ref[j] = (act * uj).astype(BF16)
            dgb = (dh * uj * dact).astype(BF16)
            dub = (dh * act).astype(BF16)
            dgg_ref[j] = dgb
            duu_ref[j] = dub
            dx = dx + _dot_nt(dgb, wg_ref[j]) + _dot_nt(dub, wu_ref[j])
        dx_ref[...] = dx

        @pl.when(i == nt - 1)
        def _():
            dlg_ref[...] = jnp.sum(acc_g[...], axis=0, keepdims=True)
            dlb_ref[...] = jnp.sum(acc_b[...], axis=0, keepdims=True)
            loss_ref[...] = jnp.sum(acc_l[...], keepdims=True) * (0.5 / d)

    tile = _rows(tm, d)
    wcol = pl.BlockSpec((N_CHIPS, None, d, fs), lambda i: (0, layer, 0, 0), pipeline_mode=pl.Buffered(1))
    wrow = pl.BlockSpec((N_CHIPS, None, fs, d), lambda i: (0, layer, 0, 0), pipeline_mode=pl.Buffered(1))
    hid = pl.BlockSpec((N_CHIPS, tm, fs), lambda i: (0, i, 0))
    vec = _const((1, d))
    return _tc_call(
        body, name=f"bwd_ffn_dx{layer}", nt=nt,
        in_specs=[tile, tile if with_loss else vec, tile, hid, hid, wcol, wcol, wrow, vec],
        out_specs=[tile, hid, hid, hid, tile, _acc_out((1, d)), _acc_out((1, d)), _acc_out((1, 1))],
        out_shape=[jax.ShapeDtypeStruct((t, d), BF16)] + [jax.ShapeDtypeStruct((N_CHIPS, t, fs), BF16)] * 3
        + [jax.ShapeDtypeStruct((t, d), F32)] + [jax.ShapeDtypeStruct((1, d), F32)] * 2
        + [jax.ShapeDtypeStruct((1, 1), F32)],
        scratch_shapes=[pltpu.VMEM((8, d), F32)] * 3,
        operands=(dy, target if with_loss else lng, z, gg, uu, wg, wu, wd, lng), riders=riders)


def _matmul_tn(a, b, tt, name):
    ja, t, ka = a.shape
    jb, _, nb = b.shape
    nj = max(ja, jb)

    def body(a_ref, b_ref, o_ref):
        @pl.when(pl.program_id(0) == 0)
        def _():
            o_ref[...] = jnp.zeros_like(o_ref)

        a0 = a_ref[0].astype(BF16) if ja == 1 else None
        b0 = b_ref[0].astype(BF16) if jb == 1 else None
        for j in range(nj):
            aj = a0 if ja == 1 else a_ref[j].astype(BF16)
            bj = b0 if jb == 1 else b_ref[j].astype(BF16)
            o_ref[j] += _dot_tn(aj, bj)

    return pl.pallas_call(
        body, name=name, grid=(t // tt,),
        in_specs=[pl.BlockSpec((ja, tt, ka), lambda i: (0, i, 0)), pl.BlockSpec((jb, tt, nb), lambda i: (0, i, 0))],
        out_specs=pl.BlockSpec((nj, ka, nb), lambda i: (0, 0, 0)), out_shape=jax.ShapeDtypeStruct((nj, ka, nb), F32),
        compiler_params=_cparams(1),
    )(a, b)


def _matmul_nn(at, b, tt, name):
    ka, t = at.shape
    nb = b.shape[1]

    def body(a_ref, b_ref, o_ref):
        @pl.when(pl.program_id(0) == 0)
        def _():
            o_ref[...] = jnp.zeros_like(o_ref)

        o_ref[...] += _dot(a_ref[...].astype(BF16), b_ref[...].astype(BF16))

    return pl.pallas_call(
        body, name=name, grid=(t // tt,),
        in_specs=[pl.BlockSpec((ka, tt), lambda i: (0, i)), pl.BlockSpec((tt, nb), lambda i: (i, 0))],
        out_specs=pl.BlockSpec((ka, nb), lambda i: (0, 0)), out_shape=jax.ShapeDtypeStruct((ka, nb), F32),
        compiler_params=_cparams(1),
    )(at, b)


def _bwd_attn(dy, z, qt, kv, sinks, wo, wqt, wkv, mixg, tm, riders=()):
    t, d = dy.shape
    nq = d // HEAD_DIM
    group = nq // N_KV_HEADS
    nb = tm // BLOCK
    nt = t // tm
    hb = tm // BLOCK
    n_kv = 2 * N_KV_HEADS

    def body(sink_ref, dy_ref, z_ref, qt_ref, kv_ref, kvh_ref, wo_ref, wqt_ref, wkv_ref, g_ref,
             dz_ref, dqt_ref, dkv_ref, dx_ref, dlg_ref, dlb_ref, dbo_ref, dbq_ref, dbkv_ref, dsink_ref,
             kv_scr, dkv_scr, qt_scr, dot_scr, dqt_scr, carry, acc_g, acc_b, acc_o, acc_q, acc_kv, acc_s):
        i = pl.program_id(0)
        ti = nt - 1 - i
        _acc_init(i, carry, acc_g, acc_b, acc_o, acc_q, acc_kv, acc_s)
        dyv = dy_ref[...]
        dz, zhat = _ln_bwd(dyv, z_ref[...], g_ref[...])
        acc_g[...] += _colsum8(dyv * zhat)
        acc_b[...] += _colsum8(dyv)
        acc_o[...] += _colsum8(dz)
        dzb = dz.astype(BF16)
        dz_ref[...] = dzb
        do_t = _dot_nt(wo_ref[...], dzb).astype(BF16)
        for b in range(nb):
            dot_scr[b] = do_t[:, b * BLOCK:(b + 1) * BLOCK]
            qt_scr[b] = qt_ref[:, b * BLOCK:(b + 1) * BLOCK]
        _fill_kv(kv_scr, kvh_ref[...], kv_ref[...], tm)
        dkv_scr[:, 0:tm] = jnp.zeros((n_kv, tm, HEAD_DIM), F32)
        dkv_scr[:, tm:tm + BLOCK] = carry[...]
        band = _attn_band()

        def block(b, c):
            r0 = pl.multiple_of(b * BLOCK, BLOCK)
            has_previous = jnp.logical_or(ti > 0, b > 0)
            for g in range(N_KV_HEADS):
                kk = kv_scr[g, pl.ds(r0, 2 * BLOCK), :]
                vv = kv_scr[N_KV_HEADS + g, pl.ds(r0, 2 * BLOCK), :]
                q_all = _heads_on_lanes(qt_scr, b, g, group)
                do_all = _heads_on_lanes(dot_scr, b, g, group)
                s_all = _dot(kk, q_all)
                dp_all = _dot(vv, do_all)
                probs, dscores = [], []
                for hh in range(group):
                    h = g * group + hh
                    cols = slice(hh * BLOCK, (hh + 1) * BLOCK)
                    p, p_sink = _softmax_with_sink(s_all[:, cols], _slope(h, nq), band, has_previous, sink_ref[h])
                    dp = _fold(dp_all[:, cols], band[0])
                    rs = jnp.sum(p * dp, axis=0, keepdims=True)
                    acc_s[h:h + 1, :] += -(p_sink * rs)
                    ds = p * (dp - rs) * (1.0 / math.sqrt(HEAD_DIM))
                    probs.append(_unfold(p.astype(BF16), band[0]))
                    dscores.append(_unfold(ds.astype(BF16), band[0]))
                p_all = jnp.concatenate(probs, axis=1)
                ds_all = jnp.concatenate(dscores, axis=1)
                dq_all = _dot_tn(kk, ds_all)
                for hh in range(group):
                    h = g * group + hh
                    dqt_scr[b, h * HEAD_DIM:(h + 1) * HEAD_DIM, :] = dq_all[:, hh * BLOCK:(hh + 1) * BLOCK]
                dkv_scr[g, pl.ds(r0, 2 * BLOCK), :] += _dot_nt(ds_all, q_all)
                dkv_scr[N_KV_HEADS + g, pl.ds(r0, 2 * BLOCK), :] += _dot_nt(p_all, do_all)
            return c

        lax.fori_loop(0, nb, block, 0)
        carry[...] = dkv_scr[:, 0:BLOCK]
        dkv = jnp.concatenate([dkv_scr[j, BLOCK:BLOCK + tm] for j in range(n_kv)], axis=1)
        acc_kv[...] += _colsum8(dkv)
        dkvb = dkv.astype(BF16)
        dkv_ref[...] = dkvb
        dqt = jnp.concatenate([dqt_scr[b] for b in range(nb)], axis=1)
        for b in range(nb):
            acc_q[...] += dqt_scr[b]
        dqtb = dqt.astype(BF16)
        dqt_ref[...] = dqtb
        dx_ref[...] = ALPHA * dz + _dot_tn(dqtb, wqt_ref[...]) + _dot_nt(dkvb, wkv_ref[...])
        _write_sums(i, nt, [(dlg_ref, acc_g), (dlb_ref, acc_b), (dbo_ref, acc_o), (dbkv_ref, acc_kv)])

        @pl.when(i == nt - 1)
        def _():
            dbq_ref[...] = jnp.sum(acc_q[...], axis=1, keepdims=True)
            dsink_ref[...] = jnp.sum(acc_s[...], axis=1, keepdims=True)

    rev = lambda w: pl.BlockSpec((tm, w), lambda i: (nt - 1 - i, 0))
    rev_cols = pl.BlockSpec((d, tm), lambda i: (0, nt - 1 - i))
    vec = _const((1, d))
    return _tc_call(
        body, name="bwd_attn", nt=nt,
        in_specs=[pl.BlockSpec(memory_space=pltpu.SMEM), rev(d), rev(d), rev_cols, rev(2 * KVD),
                  pl.BlockSpec((BLOCK, 2 * KVD), lambda i: (jnp.maximum((nt - 1 - i) * hb - 1, 0), 0)),
                  _const((d, d)), _const((d, d)), _const((d, 2 * KVD)), vec],
        out_specs=[rev(d), rev_cols, rev(2 * KVD), rev(d)] + [_acc_out((1, d))] * 3
        + [_acc_out((d, 1)), _acc_out((1, 2 * KVD)), _acc_out((nq, 1))],
        out_shape=[jax.ShapeDtypeStruct((t, d), BF16), jax.ShapeDtypeStruct((d, t), BF16),
                   jax.ShapeDtypeStruct((t, 2 * KVD), BF16), jax.ShapeDtypeStruct((t, d), F32)]
        + [jax.ShapeDtypeStruct((1, d), F32)] * 3
        + [jax.ShapeDtypeStruct((d, 1), F32), jax.ShapeDtypeStruct((1, 2 * KVD), F32),
           jax.ShapeDtypeStruct((nq, 1), F32)],
        scratch_shapes=[pltpu.VMEM((n_kv, tm + BLOCK, HEAD_DIM), BF16), pltpu.VMEM((n_kv, tm + BLOCK, HEAD_DIM), F32),
                        pltpu.VMEM((nb, d, BLOCK), BF16), pltpu.VMEM((nb, d, BLOCK), BF16),
                        pltpu.VMEM((nb, d, BLOCK), F32), pltpu.VMEM((n_kv, BLOCK, HEAD_DIM), F32),
                        pltpu.VMEM((8, d), F32), pltpu.VMEM((8, d), F32), pltpu.VMEM((8, d), F32),
                        pltpu.VMEM((d, BLOCK), F32), pltpu.VMEM((8, 2 * KVD), F32), pltpu.VMEM((nq, BLOCK), F32)],
        operands=(sinks, dy, z, qt, kv, kv, wo, wqt, wkv, mixg), riders=riders)


def _bwd_conv_head(dy, z, c, w2, mixg, lng, lnb, tm, riders=()):
    t, d = dy.shape
    nt = t // tm

    def body(dy_ref, z_ref, c_ref, w2_ref, mg_ref, lg_ref, lb_ref,
             dz_ref, s_ref, dc_ref, dmg_ref, dmb_ref, db2_ref, dlg_ref, dlb_ref, a0, a1, a2, a3, a4):
        i = pl.program_id(0)
        _acc_init(i, a0, a1, a2, a3, a4)
        dyv = dy_ref[...]
        dz, zhat = _ln_bwd(dyv, z_ref[...], mg_ref[...])
        a0[...] += _colsum8(dyv * zhat)
        a1[...] += _colsum8(dyv)
        a2[...] += _colsum8(dz)
        dz_ref[...] = dz
        chat, rstd = _ln_stats(c_ref[...])
        n = chat * lg_ref[...] + lb_ref[...]
        act, dact = _silu_and_grad(n)
        s_ref[...] = act.astype(BF16)
        dn = _dot_nt(dz.astype(BF16), w2_ref[...]) * dact
        a3[...] += _colsum8(dn * chat)
        a4[...] += _colsum8(dn)
        dch = dn * lg_ref[...]
        m1 = jnp.mean(dch, axis=-1, keepdims=True)
        m2 = jnp.mean(dch * chat, axis=-1, keepdims=True)
        dc_ref[...] = rstd * (dch - m1 - chat * m2)
        _write_sums(i, nt, [(dmg_ref, a0), (dmb_ref, a1), (db2_ref, a2), (dlg_ref, a3), (dlb_ref, a4)])

    vec = _const((1, d))
    return _tc_call(
        body, name="bwd_conv_head", nt=nt,
        in_specs=[_rows(tm, d), _rows(tm, d), _rows(tm, d), _const((d, d)), vec, vec, vec],
        out_specs=[_rows(tm, d), _rows(tm, d), _rows(tm, d)] + [_acc_out((1, d))] * 5,
        out_shape=[jax.ShapeDtypeStruct((t, d), F32), jax.ShapeDtypeStruct((t, d), BF16),
                   jax.ShapeDtypeStruct((t, d), F32)] + [jax.ShapeDtypeStruct((1, d), F32)] * 5,
        scratch_shapes=[pltpu.VMEM((8, d), F32)] * 5, operands=(dy, z, c, w2, mixg, lng, lnb), riders=riders)


def _bwd_conv_glu(dc, u, a, g, dz, wdw, w1s, tm, riders=()):
    t, d = dc.shape
    dh_w = d // 2
    nt = t // tm
    hb = tm // CONV_HALO
    last_halo = t // CONV_HALO - 1

    def body(dc_ref, dcn_ref, u_ref, a_ref, g_ref, dz_ref, w_ref, w1_ref,
             dx_ref, dh_ref, db1_ref, dbdw_ref, dw_ref, ext, sh, du_scr, acc_b1, acc_bdw, acc_w):
        i = pl.program_id(0)
        _acc_init(i, acc_b1, acc_bdw, acc_w)
        dcv = dc_ref[...]
        acc_bdw[...] += _colsum8(dcv)

        ext[0:tm] = dcv
        ext[tm:tm + CONV_HALO] = jnp.where(i == nt - 1, 0.0, dcn_ref[...])
        ext[tm + CONV_HALO:tm + CONV_HALO + 8] = jnp.zeros((8, d), F32)
        _fill_shifted(sh, ext)

        def du_chunk(r, carry):
            base = pl.multiple_of(r * CONV_CHUNK, CONV_CHUNK)
            _tap_sum(w_ref, sh, base, d, lambda k: CONV_WIDTH - 1 - k, du_scr)
            return carry

        lax.fori_loop(0, tm // CONV_CHUNK, du_chunk, 0)

        def dw_chunk(r, carry):
            base = pl.multiple_of(r * CONV_CHUNK, CONV_CHUNK)
            groups = CONV_CHUNK // 8
            for lg in range(d // LANES):
                ls = slice(lg * LANES, (lg + 1) * LANES)
                uv = u_ref[pl.ds(base, CONV_CHUNK), ls].reshape(groups, 8, LANES)
                for k in range(CONV_WIDTH):
                    e = CONV_WIDTH - 1 - k
                    x = sh[e % 8, pl.ds(base + (e // 8) * 8, CONV_CHUNK), ls].reshape(groups, 8, LANES)
                    acc_w[k, :, ls] += jnp.sum(uv * x, axis=0)
            return carry

        lax.fori_loop(0, tm // CONV_CHUNK, dw_chunk, 0)

        du = du_scr[...]
        av = a_ref[...].astype(F32)
        sg = jax.nn.sigmoid(g_ref[...].astype(F32))
        da = du * sg
        dg = du * av * sg * (1.0 - sg)
        acc_b1[:, 0:d] += _colsum8(da)
        acc_b1[:, d:2 * d] += _colsum8(dg)
        dx = ALPHA * dz_ref[...]
        for j, part in enumerate([da[:, 0:dh_w], da[:, dh_w:d], dg[:, 0:dh_w], dg[:, dh_w:d]]):
            pb = part.astype(BF16)
            dh_ref[j] = pb
            dx = dx + _dot_nt(pb, w1_ref[j])
        dx_ref[...] = dx

        @pl.when(i == nt - 1)
        def _():
            db1_ref[...] = jnp.sum(acc_b1[...], axis=0, keepdims=True)
            dbdw_ref[...] = jnp.sum(acc_bdw[...], axis=0, keepdims=True)
            dw_ref[...] = jnp.sum(acc_w[...], axis=1)

    return _tc_call(
        body, name="bwd_conv_glu", nt=nt,
        in_specs=[_rows(tm, d), pl.BlockSpec((CONV_HALO, d), lambda i: (jnp.minimum((i + 1) * hb, last_halo), 0)),
                  _rows(tm, d), _rows(tm, d), _rows(tm, d), _rows(tm, d), _const((CONV_HALO, 8, d)),
                  _const((4, d, dh_w))],
        out_specs=[_rows(tm, d), pl.BlockSpec((4, tm, dh_w), lambda i: (0, i, 0)), _acc_out((1, 2 * d)),
                   _acc_out((1, d)), _acc_out((CONV_HALO, d))],
        out_shape=[jax.ShapeDtypeStruct((t, d), F32), jax.ShapeDtypeStruct((4, t, dh_w), BF16),
                   jax.ShapeDtypeStruct((1, 2 * d), F32), jax.ShapeDtypeStruct((1, d), F32),
                   jax.ShapeDtypeStruct((CONV_HALO, d), F32)],
        scratch_shapes=[pltpu.VMEM((tm + CONV_HALO + 8, d), F32), pltpu.VMEM((8, tm + CONV_HALO, d), F32),
                        pltpu.VMEM((tm, d), F32), pltpu.VMEM((8, 2 * d), F32), pltpu.VMEM((8, d), F32),
                        pltpu.VMEM((CONV_HALO, 8, d), F32)],
        operands=(dc, dc, u, a, g, dz, wdw, w1s), riders=riders)


def _row_block(rows, target):
    best = rows
    for cand in range(8, min(rows, target) + 1, 8):
        if rows % cand == 0:
            best = cand
    return best


def _adamw(w, g, m, v, name):
    rows, lanes = w.shape
    br = _row_block(rows, 512) if rows % 8 == 0 else rows

    def body(w_ref, g_ref, m_ref, v_ref, d_ref, nm_ref, nv_ref):
        gv = g_ref[...]
        nm = ADAM_B1 * m_ref[...] + (1.0 - ADAM_B1) * gv
        nv = ADAM_B2 * v_ref[...] + (1.0 - ADAM_B2) * (gv * gv)
        m_hat = nm / (1.0 - ADAM_B1 ** ADAM_STEP)
        v_hat = nv / (1.0 - ADAM_B2 ** ADAM_STEP)
        d_ref[...] = -ADAM_LR * (m_hat / (jnp.sqrt(v_hat) + ADAM_EPS) + ADAM_WD * w_ref[...])
        nm_ref[...] = nm
        nv_ref[...] = nv

    spec = pl.BlockSpec((br, lanes), lambda i: (i, 0))
    return pl.pallas_call(
        body, name=name, grid=(rows // br,), in_specs=[spec] * 4, out_specs=[spec] * 3,
        out_shape=[jax.ShapeDtypeStruct((rows, lanes), F32)] * 3, compiler_params=_cparams(),
    )(w, g, m, v)


def _pad_to(v, n):
    return jnp.pad(v, (0, n - v.shape[0]))


def _round_up(n, m):
    return (n + m - 1) // m * m


def kernel(x, conv_w_pw1, conv_b_pw1, conv_w_dw, conv_b_dw, conv_ln_g, conv_ln_b, conv_w_pw2, conv_b_pw2, kv_w_k, kv_b_k, kv_w_v, kv_b_v, attn_w_q, attn_b_q, attn_sinks, attn_w_o, attn_b_o, ffn_w_gate, ffn_w_up, ffn_w_down, ln_mix_g, ln_mix_b, ln_ffn_g, ln_ffn_b, loss_target, m_conv_w_pw1, m_conv_b_pw1, m_conv_w_dw, m_conv_b_dw, m_conv_ln_g, m_conv_ln_b, m_conv_w_pw2, m_conv_b_pw2, m_kv_w_k, m_kv_b_k, m_kv_w_v, m_kv_b_v, m_attn_w_q, m_attn_b_q, m_attn_sinks, m_attn_w_o, m_attn_b_o, m_ffn_w_gate, m_ffn_w_up, m_ffn_w_down, m_ln_mix_g, m_ln_mix_b, m_ln_ffn_g, m_ln_ffn_b, v_conv_w_pw1, v_conv_b_pw1, v_conv_w_dw, v_conv_b_dw, v_conv_ln_g, v_conv_ln_b, v_conv_w_pw2, v_conv_b_pw2, v_kv_w_k, v_kv_b_k, v_kv_w_v, v_kv_b_v, v_attn_w_q, v_attn_b_q, v_attn_sinks, v_attn_w_o, v_attn_b_o, v_ffn_w_gate, v_ffn_w_up, v_ffn_w_down, v_ln_mix_g, v_ln_mix_b, v_ln_ffn_g, v_ln_ffn_b):
    args = dict(locals())
    w = {n: args[n] for n in WEIGHTS}
    mom = {n: args["m_" + n] for n in WEIGHTS}
    var = {n: args["v_" + n] for n in WEIGHTS}
    assert x.shape[0] == 1, "one sequence per device"
    t, d = x.shape[1], x.shape[2]
    dq = d // 4
    fs = ffn_w_gate.shape[-1]
    nq = d // HEAD_DIM
    x0 = x.reshape(t, d)
    target = loss_target.reshape(t, d)
    tm_big = min(512, t)
    tm_mid = min(256, t)
    tm_tn = min(1024, t)
    c_idx = lax.axis_index("c")

    me_idx = 2 * lax.axis_index("x") + lax.axis_index("y")

    def gather_buffer(v):
        buf = lax.empty((N_CHIPS,) + v.shape, v.dtype)
        return lax.dynamic_update_slice(buf, v[None], (me_idx,) + (0,) * v.ndim)

    def halves(v):
        return v.reshape(2, -1, v.shape[-1])

    small_sizes = [int(w[n].size) for n in SMALL_SHARDED]
    rs = _round_up(sum(small_sizes), 8 * 128) // 128
    spack = _pad_to(jnp.concatenate([w[n].reshape(-1) for n in SMALL_SHARDED]), rs * 128).reshape(rs, 128)
    conv_first = ['conv_w_pw1', 'conv_w_pw2']
    later = [n for n in BIG if n not in conv_first]
    (first_out,) = _run_riders(
        [_all_gather_rider([gather_buffer(halves(w[n].astype(BF16))) for n in conv_first], gather_buffer(spack))],
        "all_gather_conv")
    later_rider = _all_gather_rider([gather_buffer(halves(w[n].astype(BF16))) for n in later])
    gs = first_out[-1].reshape(N_CHIPS, rs * 128)
    full = {n: g.reshape((N_CHIPS,) + w[n].shape) for n, g in zip(conv_first, first_out)}
    off = 0
    for n, size in zip(SMALL_SHARDED, small_sizes):
        full[n] = gs[:, off:off + size].reshape((N_CHIPS,) + w[n].shape)
        off += size
    w1s = full['conv_w_pw1'].reshape(N_CHIPS, d, d // 2)
    w2 = full['conv_w_pw2'].reshape(d, d)
    b1 = full['conv_b_pw1'].reshape(1, 2 * d)
    wdw = jnp.pad(full['conv_w_dw'].reshape(N_CHIPS, CONV_WIDTH, dq).transpose(1, 0, 2).reshape(CONV_WIDTH, d),
                  ((0, CONV_HALO - CONV_WIDTH), (0, 0)))
    wdw = jnp.broadcast_to(wdw[:, None, :], (CONV_HALO, 8, d))
    bdw = full['conv_b_dw'].reshape(1, d)
    clng = full['conv_ln_g'].reshape(1, d)
    clnb = full['conv_ln_b'].reshape(1, d)
    b2 = full['conv_b_pw2'].reshape(1, d)
    bkv = jnp.concatenate([kv_b_k, kv_b_v]).reshape(1, 2 * KVD)
    sinks = attn_sinks.reshape(nq)
    mixg = [ln_mix_g[l].reshape(1, d) for l in range(DEPTH)]
    mixb = [ln_mix_b[l].reshape(1, d) for l in range(DEPTH)]
    ffng = [ln_ffn_g[l].reshape(1, d) for l in range(DEPTH)]
    ffnb = [ln_ffn_b[l].reshape(1, d) for l in range(DEPTH)]

    a_act, g_act, u_act = _fwd_pw1_glu(x0, w1s, b1, tm_big)
    (c_act, z1, x1), (later_out,) = _fwd_conv_tail(u_act, x0, wdw, bdw, clng, clnb, w2, b2, mixg[0], mixb[0], tm_mid,
                                                   riders=[later_rider])
    full.update({n: g.reshape((N_CHIPS,) + w[n].shape) for n, g in zip(later, later_out)})
    wkv = jnp.concatenate([full['kv_w_k'].reshape(d, KVD), full['kv_w_v'].reshape(d, KVD)], axis=1)
    wqt = full['attn_w_q'].reshape(d, d).T
    wo = full['attn_w_o'].reshape(d, d)
    wg, wu, wd = full['ffn_w_gate'], full['ffn_w_up'], full['ffn_w_down']
    gg0, uu0, z2, x2 = _fwd_ffn(x1, wg, wu, wd, 0, ffng[0], ffnb[0], tm_big)
    qt_act, kv_act, ot_act, z3, x3 = _fwd_attn(x2, wqt, attn_b_q.reshape(d, 1), wkv, bkv, sinks, wo, attn_b_o,
                                               mixg[1], mixb[1], tm_big)
    gg1, uu1, z4, x4 = _fwd_ffn(x3, wg, wu, wd, 1, ffng[1], ffnb[1], tm_big)

    c_arr = c_idx.reshape(1).astype(jnp.int32)

    def halves4(v):
        return v.reshape(N_CHIPS, 2, -1, v.shape[-1])

    def arrays(group):
        return [p for _, p in group]

    def pair_sums(group, got):
        return [_pair_sum(p, g, c_arr, "grad_pair_sum_" + n) for (n, p), g in zip(group, got)]

    pos_arr = jnp.stack([me_idx, c_idx]).astype(jnp.int32)

    def chip_sums(group, sums, got):
        return [_chip_sum(s, g, pos_arr, "grad_chip_sum_" + n) for (n, _), s, g in zip(group, sums, got)]

    (dz4, dgg1, duu1, hm1, dx3, d_fg1, d_fb1, loss_part), _ = _bwd_ffn_dx(
        x4, z4, gg1, uu1, wg, wu, wd, 1, ffng[1], tm_mid, target=target)
    loss = lax.psum(loss_part[0, 0], ("x", "y", "c"))
    g1 = [("ffn_w_gate1", halves4(_matmul_tn(x3[None], dgg1, tm_tn, "dw_gate1"))),
          ("ffn_w_up1", halves4(_matmul_tn(x3[None], duu1, tm_tn, "dw_up1"))),
          ("ffn_w_down1", halves4(_matmul_tn(hm1, dz4[None], tm_tn, "dw_down1")))]
    (dz3, dqt, dkv, dx2, d_mg1, d_mb1, d_bo, d_bq, d_bkv, d_sinks), (got1,) = _bwd_attn(
        dx3, z3, qt_act, kv_act, sinks, wo, wqt, wkv, mixg[1], tm_big, riders=[_pair_exchange_rider(arrays(g1))])
    s1 = pair_sums(g1, got1)
    dwo = _matmul_nn(ot_act, dz3, tm_tn, "dw_o")
    dwq = _matmul_nn(dqt, x2, tm_tn, "dw_q").T
    dwkv = _matmul_tn(x2[None], dkv[None], tm_tn, "dw_kv")[0]
    g2 = [("attn_w_o", halves4(dwo)), ("attn_w_q", halves4(dwq)),
          ("kv_w_k", halves4(dwkv[:, 0:KVD])), ("kv_w_v", halves4(dwkv[:, KVD:2 * KVD]))]
    (dz2, dgg0, duu0, hm0, dx1, d_fg0, d_fb0, _), (from_chips1, got2) = _bwd_ffn_dx(
        dx2, z2, gg0, uu0, wg, wu, wd, 0, ffng[0], tm_mid,
        riders=[_chip_scatter_rider(s1), _pair_exchange_rider(arrays(g2))])
    f1 = chip_sums(g1, s1, from_chips1)
    s2 = pair_sums(g2, got2)
    g3 = [("ffn_w_gate0", halves4(_matmul_tn(x1[None], dgg0, tm_tn, "dw_gate0"))),
          ("ffn_w_up0", halves4(_matmul_tn(x1[None], duu0, tm_tn, "dw_up0"))),
          ("ffn_w_down0", halves4(_matmul_tn(hm0, dz2[None], tm_tn, "dw_down0")))]
    (dz1, s_act, dc, d_mg0, d_mb0, d_b2, d_clng, d_clnb), (got3, shared1) = _bwd_conv_head(
        dx1, z1, c_act, w2, mixg[0], clng, clnb, tm_mid,
        riders=[_pair_exchange_rider(arrays(g3)), _pair_share_rider(f1)])
    s3 = pair_sums(g3, got3)
    dw2 = _matmul_tn(s_act[None], dz1[None], tm_tn, "dw_pw2")
    (dx0, dh1, d_b1, d_bdw, d_wdw), (from_chips2, from_chips3) = _bwd_conv_glu(
        dc, u_act, a_act, g_act, dz1, wdw, w1s, tm_mid, riders=[_chip_scatter_rider(s2), _chip_scatter_rider(s3)])
    f2 = chip_sums(g2, s2, from_chips2)
    f3 = chip_sums(g3, s3, from_chips3)
    dw1 = _matmul_tn(x0[None], dh1, tm_tn, "dw_pw1")

    def rows4(v):
        return v.reshape(N_CHIPS, -1)

    def rep4(v):
        return jnp.broadcast_to(v.reshape(1, -1), (N_CHIPS, v.size))

    local = {
        'conv_b_pw1': rows4(d_b1),
        'conv_w_dw': rows4(d_wdw[0:CONV_WIDTH].reshape(CONV_WIDTH, N_CHIPS, dq).transpose(1, 0, 2)),
        'conv_b_dw': rows4(d_bdw), 'conv_ln_g': rows4(d_clng), 'conv_ln_b': rows4(d_clnb), 'conv_b_pw2': rows4(d_b2),
        'kv_b_k': rep4(d_bkv[:, 0:KVD]), 'kv_b_v': rep4(d_bkv[:, KVD:2 * KVD]), 'attn_b_q': rep4(d_bq),
        'attn_sinks': rep4(d_sinks), 'attn_b_o': rep4(d_bo),
        'ln_mix_g': rep4(jnp.concatenate([d_mg0, d_mg1])), 'ln_mix_b': rep4(jnp.concatenate([d_mb0, d_mb1])),
        'ln_ffn_g': rep4(jnp.concatenate([d_fg0, d_fg1])), 'ln_ffn_b': rep4(jnp.concatenate([d_fb0, d_fb1])),
    }
    n_small = sum(int(w[n].size) for n in SMALL)
    small_rows = _round_up(n_small, 2 * 8 * 128) // 128
    small_local = jnp.concatenate([local[n] for n in SMALL], axis=1)
    small_local = jnp.pad(small_local, ((0, 0), (0, small_rows * 128 - n_small)))
    g4 = [("conv_w_pw1", halves4(dw1)), ("conv_w_pw2", halves4(dw2)),
          ("small", small_local.reshape(N_CHIPS, 2, small_rows // 2, 128))]
    (got4,) = _run_riders([_pair_exchange_rider(arrays(g4))], "grad_pair_exchange_last")
    s4 = pair_sums(g4, got4)
    (from_chips4,) = _run_riders([_chip_scatter_rider(s4)], "grad_chip_scatter_last")
    f4 = chip_sums(g4, s4, from_chips4)
    (shared_rest,) = _run_riders([_pair_share_rider(f2 + f3 + f4)], "grad_pair_share_last")
    reduced = dict(zip([n for n, _ in g1], shared1))
    reduced.update(zip([n for n, _ in g2 + g3 + g4], shared_rest))
    for n in ('ffn_w_gate', 'ffn_w_up', 'ffn_w_down'):
        reduced[n] = jnp.stack([reduced[n + str(layer)].reshape(w[n].shape[1:]) for layer in range(DEPTH)])

    g_out, delta, new_m, new_v = {}, {}, {}, {}
    for n in BIG:
        shape = w[n].shape
        two_d = (-1, shape[-1])
        g_out[n] = reduced[n].reshape(shape)
        dl, nm, nv = _adamw(w[n].reshape(two_d), g_out[n].reshape(two_d), mom[n].reshape(two_d), var[n].reshape(two_d),
                            "adamw_" + n)
        delta[n], new_m[n], new_v[n] = dl.reshape(shape), nm.reshape(shape), nv.reshape(shape)

    def pack_small(tree):
        return _pad_to(jnp.concatenate([tree[n].reshape(-1) for n in SMALL]), small_rows * 128).reshape(small_rows, 128)

    g_small = reduced['small'].reshape(small_rows, 128)
    dl, nm, nv = _adamw(pack_small(w), g_small, pack_small(mom), pack_small(var), "adamw_small")
    off = 0
    for n in SMALL:
        size, shape = int(w[n].size), w[n].shape
        for tree, flat in ((g_out, g_small), (delta, dl), (new_m, nm), (new_v, nv)):
            tree[n] = flat.reshape(-1)[off:off + size].reshape(shape)
        off += size

    return (loss, dx0.reshape(x.shape), *[g_out[n] for n in WEIGHTS], *[delta[n] for n in WEIGHTS],
            *[new_m[n] for n in WEIGHTS], *[new_v[n] for n in WEIGHTS])
```

```python
import functools
import math

import jax
import jax.numpy as jnp
from jax import lax
from jax.experimental import pallas as pl
from jax.experimental.pallas import tpu as pltpu

F32 = jnp.float32
BF16 = jnp.bfloat16

DEPTH = 2
ALPHA = (2.0 * DEPTH) ** 0.25
LN_EPS = 1e-5
NEG_INF = -1e30
HEAD_DIM = 64
N_KV_HEADS = 2
KVD = N_KV_HEADS * HEAD_DIM
BLOCK = 128
CONV_WIDTH = 31
CONV_HALO = 32
ALIBI_MAX = 8.0
ADAM_LR, ADAM_B1, ADAM_B2, ADAM_EPS, ADAM_WD, ADAM_STEP = 0.001, 0.9, 0.999, 1e-08, 0.01, 10

N_CHIPS = 4
PACK_ROWS = 256
VMEM_LIMIT = 56 * 1024 * 1024
MESH = pl.DeviceIdType.MESH

NT_DIMS = (((1,), (1,)), ((), ()))
TN_DIMS = (((0,), (0,)), ((), ()))

WEIGHTS = ['conv_w_pw1', 'conv_b_pw1', 'conv_w_dw', 'conv_b_dw', 'conv_ln_g', 'conv_ln_b', 'conv_w_pw2', 'conv_b_pw2',
           'kv_w_k', 'kv_b_k', 'kv_w_v', 'kv_b_v', 'attn_w_q', 'attn_b_q', 'attn_sinks', 'attn_w_o', 'attn_b_o',
           'ffn_w_gate', 'ffn_w_up', 'ffn_w_down', 'ln_mix_g', 'ln_mix_b', 'ln_ffn_g', 'ln_ffn_b']
BIG = ['conv_w_pw1', 'conv_w_pw2', 'kv_w_k', 'kv_w_v', 'attn_w_q', 'attn_w_o', 'ffn_w_gate', 'ffn_w_up', 'ffn_w_down']
SMALL_SHARDED = ['conv_b_pw1', 'conv_w_dw', 'conv_b_dw', 'conv_ln_g', 'conv_ln_b', 'conv_b_pw2']
REPLICATED = ['kv_b_k', 'kv_b_v', 'attn_b_q', 'attn_sinks', 'attn_b_o', 'ln_mix_g', 'ln_mix_b', 'ln_ffn_g', 'ln_ffn_b']
SMALL = SMALL_SHARDED + REPLICATED


def _cparams(n_grid=1):
    return pltpu.CompilerParams(dimension_semantics=("arbitrary",) * n_grid, vmem_limit_bytes=VMEM_LIMIT)


def _rows(tm, width):
    return pl.BlockSpec((tm, width), lambda i: (i, 0))


def _const(shape):
    return pl.BlockSpec(shape, lambda *_: (0,) * len(shape), pipeline_mode=pl.Buffered(1))


def _acc_out(shape):
    return pl.BlockSpec(shape, lambda *_: (0,) * len(shape))


def _dot(a, b):
    return jnp.dot(a, b, preferred_element_type=F32)


def _dot_nt(a, b):
    return lax.dot_general(a, b, NT_DIMS, preferred_element_type=F32)


def _dot_tn(a, b):
    return lax.dot_general(a, b, TN_DIMS, preferred_element_type=F32)


def _colsum8(v):
    m, n = v.shape
    return jnp.sum(v.reshape(m // 8, 8, n), axis=0)


def _ln_stats(z):
    mu = jnp.mean(z, axis=-1, keepdims=True)
    zc = z - mu
    var = jnp.mean(zc * zc, axis=-1, keepdims=True)
    rstd = lax.rsqrt(var + LN_EPS)
    return zc * rstd, rstd


def _ln_fwd(z, g, b):
    zhat, _ = _ln_stats(z)
    return zhat * g + b


def _ln_bwd(dy, z, g):
    zhat, rstd = _ln_stats(z)
    dzh = dy * g
    m1 = jnp.mean(dzh, axis=-1, keepdims=True)
    m2 = jnp.mean(dzh * zhat, axis=-1, keepdims=True)
    return rstd * (dzh - m1 - zhat * m2), zhat


def _silu_and_grad(n):
    sg = jax.nn.sigmoid(n)
    return n * sg, sg * (1.0 + n * (1.0 - sg))


def _acc_init(i, *refs):
    @pl.when(i == 0)
    def _():
        for r in refs:
            r[...] = jnp.zeros_like(r)


def _mesh_pos():
    x, y, c = lax.axis_index("x"), lax.axis_index("y"), lax.axis_index("c")
    chips = [(1 - x, y), (x, 1 - y), (1 - x, 1 - y)]
    return x, y, c, chips


HBM_SPEC = pl.BlockSpec(memory_space=pltpu.HBM)


def _remote(src, dst, send_sems, recv_sems, k, to):
    return pltpu.make_async_remote_copy(src_ref=src, dst_ref=dst, send_sem=send_sems.at[k], recv_sem=recv_sems.at[k],
                                        device_id=to, device_id_type=MESH)


class _Rider:
    def __init__(self, operands, out_shapes, sem_shapes, start, finish, mid=None, in_place=False):
        self.operands, self.out_shapes, self.sem_shapes = list(operands), list(out_shapes), list(sem_shapes)
        self.start, self.finish, self.mid = start, finish, mid
        self.in_place = in_place


def _rider_aliases(riders, first_in, first_out):
    aliases, k_in, k_out = {}, first_in, first_out
    for r in riders:
        if r.in_place:
            aliases.update({k_in + k: k_out + k for k in range(len(r.operands))})
        k_in += len(r.operands)
        k_out += len(r.out_shapes)
    return aliases


def _split(refs, counts):
    parts, k = [], 0
    for n in counts:
        parts.append(refs[k:k + n])
        k += n
    return parts


def _rider_refs(riders, ins, outs, sems):
    return list(zip(riders, _split(ins, [len(r.operands) for r in riders]),
                    _split(outs, [len(r.out_shapes) for r in riders]),
                    _split(sems, [len(r.sem_shapes) for r in riders])))


def _tc_call(body, *, name, nt, in_specs, out_specs, out_shape, operands, scratch_shapes=(), riders=(), mid_frac=0.75):
    n_in, n_out, n_scr = len(in_specs), len(out_specs), len(scratch_shapes)
    r_ops = [o for r in riders for o in r.operands]
    r_outs = [o for r in riders for o in r.out_shapes]
    r_sems = [s for r in riders for s in r.sem_shapes]
    mid_step = min(max(int(nt * mid_frac), 0), nt - 1)

    def full(*refs):
        ins, r_in, outs, r_out, scr, r_sem = _split(refs, [n_in, len(r_ops), n_out, len(r_outs), n_scr, len(r_sems)])
        parts = _rider_refs(riders, r_in, r_out, r_sem)
        step = pl.program_id(0)

        @pl.when(step == 0)
        def _():
            for r, a, b, s in parts:
                r.start(a, b, s)

        body(*ins, *outs, *scr)

        @pl.when(step == mid_step)
        def _():
            for r, a, b, s in parts:
                if r.mid is not None:
                    r.mid(a, b, s)

        @pl.when(step == nt - 1)
        def _():
            for r, a, b, s in parts:
                r.finish(a, b, s)

    res = pl.pallas_call(
        full if riders else body, name=name, grid=(nt,), in_specs=list(in_specs) + [HBM_SPEC] * len(r_ops),
        out_specs=list(out_specs) + [HBM_SPEC] * len(r_outs), out_shape=list(out_shape) + r_outs,
        scratch_shapes=list(scratch_shapes) + r_sems, input_output_aliases=_rider_aliases(riders, n_in, n_out),
        compiler_params=_cparams(),
    )(*operands, *r_ops)
    return res[:n_out], _split(res[n_out:], [len(r.out_shapes) for r in riders])


def _run_riders(riders, name):
    r_ops = [o for r in riders for o in r.operands]
    r_outs = [o for r in riders for o in r.out_shapes]
    r_sems = [s for r in riders for s in r.sem_shapes]

    def body(*refs):
        r_in, r_out, r_sem = _split(refs, [len(r_ops), len(r_outs), len(r_sems)])
        parts = _rider_refs(riders, r_in, r_out, r_sem)
        for r, a, b, s in parts:
            r.start(a, b, s)
        for r, a, b, s in parts:
            if r.mid is not None:
                r.mid(a, b, s)
        for r, a, b, s in parts:
            r.finish(a, b, s)

    res = pl.pallas_call(body, name=name, out_shape=tuple(r_outs), in_specs=[HBM_SPEC] * len(r_ops),
                         out_specs=(HBM_SPEC,) * len(r_outs), scratch_shapes=r_sems,
                         input_output_aliases=_rider_aliases(riders, 0, 0))(*r_ops)
    return _split(list(res), [len(r.out_shapes) for r in riders])


def _all_gather_rider(bufs, small=None):
    n = len(bufs)
    n_small = 0 if small is None else 1

    def copies(outs, sems):
        send_sems, recv_sems = sems
        x, y, c, chips = _mesh_pos()
        me = 2 * x + y
        here, sibling = (x, y, c), (x, y, 1 - c)
        rows = [2 * cx + cy for cx, cy in chips]

        def big(p, k, chip_row, half, to):
            piece = outs[p].at[chip_row, half]
            return _remote(piece, piece, send_sems, recv_sems, 6 * p + k, to)

        first = [big(p, j, me, c, (cx, cy, c)) for p in range(n) for j, (cx, cy) in enumerate(chips)]
        landed = [big(p, j, rows[j], c, here) for p in range(n) for j in range(3)]
        passed = [big(p, 3 + j, rows[j], c, sibling) for p in range(n) for j in range(3)]
        arrivals = [big(p, 3 + j, rows[j], 1 - c, here) for p in range(n) for j in range(3)]
        if n_small:
            first = [_remote(outs[n].at[me], outs[n].at[me], send_sems, recv_sems, 6 * n + j, (cx, cy, c))
                     for j, (cx, cy) in enumerate(chips)] + first
            arrivals += [_remote(outs[n].at[rows[j]], outs[n].at[rows[j]], send_sems, recv_sems, 6 * n + j, here)
                         for j in range(3)]
        return first, landed, passed, arrivals

    def start(ins, outs, sems):
        for cp in copies(outs, sems)[0]:
            cp.start()

    def mid(ins, outs, sems):
        _, landed, passed, _ = copies(outs, sems)
        for got, fwd in zip(landed, passed):
            got.wait_recv()
            fwd.start()

    def finish(ins, outs, sems):
        first, _, passed, arrivals = copies(outs, sems)
        for cp in arrivals:
            cp.wait_recv()
        for cp in first + passed:
            cp.wait_send()

    operands = list(bufs) + ([small] if n_small else [])
    n_sem = 6 * n + 3 * n_small
    return _Rider(operands, [jax.ShapeDtypeStruct(o.shape, o.dtype) for o in operands],
                  [pltpu.SemaphoreType.DMA((n_sem,)), pltpu.SemaphoreType.DMA((n_sem,))], start, finish, mid,
                  in_place=True)


def _pair_exchange_rider(plist):
    n = len(plist)

    def copies(ins, outs, sems):
        x, y, c, _ = _mesh_pos()
        return [_remote(ins[k].at[:, 1 - c], outs[k], sems[0], sems[1], k, (x, y, 1 - c)) for k in range(n)]

    def start(ins, outs, sems):
        for cp in copies(ins, outs, sems):
            cp.start()

    def finish(ins, outs, sems):
        for cp in copies(ins, outs, sems):
            cp.wait()

    return _Rider(plist, [jax.ShapeDtypeStruct((p.shape[0],) + p.shape[2:], p.dtype) for p in plist],
                  [pltpu.SemaphoreType.DMA((n,)), pltpu.SemaphoreType.DMA((n,))], start, finish)


def _pair_sum(p, got, c, name):
    n, _, r, l = p.shape
    br = _row_block(r, PACK_ROWS)

    def body(c_ref, p_ref, got_ref, out_ref):
        out_ref[...] = p_ref[...] + got_ref[...]

    return pl.pallas_call(
        body, name=name, out_shape=jax.ShapeDtypeStruct((n, r, l), F32),
        grid_spec=pltpu.PrefetchScalarGridSpec(
            num_scalar_prefetch=1, grid=(n, r // br),
            in_specs=[pl.BlockSpec((None, None, br, l), lambda j, i, c_ref: (j, c_ref[0], i, 0)),
                      pl.BlockSpec((None, br, l), lambda j, i, c_ref: (j, i, 0))],
            out_specs=pl.BlockSpec((None, br, l), lambda j, i, c_ref: (j, i, 0))),
        compiler_params=_cparams(2),
    )(c, p, got)


def _chip_scatter_rider(slist):
    n = len(slist)

    def copies(ins, outs, sems):
        send_sems, recv_sems = sems
        x, y, c, chips = _mesh_pos()
        sends = [_remote(ins[k].at[2 * cx + cy], outs[k].at[j], send_sems, recv_sems, 3 * k + j, (cx, cy, c))
                 for k in range(n) for j, (cx, cy) in enumerate(chips)]
        arrivals = [_remote(ins[k].at[0], outs[k].at[j], send_sems, recv_sems, 3 * k + j, (x, y, c))
                    for k in range(n) for j in range(3)]
        return sends, arrivals

    def start(ins, outs, sems):
        for cp in copies(ins, outs, sems)[0]:
            cp.start()

    def finish(ins, outs, sems):
        sends, arrivals = copies(ins, outs, sems)
        for cp in arrivals:
            cp.wait_recv()
        for cp in sends:
            cp.wait_send()

    return _Rider(slist, [jax.ShapeDtypeStruct((3,) + s.shape[1:], s.dtype) for s in slist],
                  [pltpu.SemaphoreType.DMA((3 * n,)), pltpu.SemaphoreType.DMA((3 * n,))], start, finish)


def _chip_sum(s, got, pos, name):
    _, r, l = s.shape
    br = _row_block(r, PACK_ROWS)

    def body(pos_ref, s_ref, got_ref, out_ref):
        me = pos_ref[0]
        total = None
        for chip in range(N_CHIPS):
            flip = jnp.bitwise_xor(me, chip)
            term = jnp.where(flip == 0, s_ref[...],
                             jnp.where(flip == 2, got_ref[0], jnp.where(flip == 1, got_ref[1], got_ref[2])))
            total = term if total is None else total + term
        out_ref[...] = total

    return pl.pallas_call(
        body, name=name, out_shape=jax.ShapeDtypeStruct((2, r, l), F32),
        grid_spec=pltpu.PrefetchScalarGridSpec(
            num_scalar_prefetch=1, grid=(r // br,),
            in_specs=[pl.BlockSpec((None, br, l), lambda i, pos_ref: (pos_ref[0], i, 0)),
                      pl.BlockSpec((3, br, l), lambda i, pos_ref: (0, i, 0))],
            out_specs=pl.BlockSpec((None, br, l), lambda i, pos_ref: (pos_ref[1], i, 0))),
        compiler_params=_cparams(1),
    )(pos, s, got)


def _pair_share_rider(flist):
    n = len(flist)

    def copies(outs, sems):
        x, y, c, _ = _mesh_pos()
        sends = [_remote(outs[k].at[c], outs[k].at[c], sems[0], sems[1], k, (x, y, 1 - c)) for k in range(n)]
        arrivals = [_remote(outs[k].at[1 - c], outs[k].at[1 - c], sems[0], sems[1], k, (x, y, c)) for k in range(n)]
        return sends, arrivals

    def start(ins, outs, sems):
        for cp in copies(outs, sems)[0]:
            cp.start()

    def finish(ins, outs, sems):
        sends, arrivals = copies(outs, sems)
        for cp in arrivals:
            cp.wait_recv()
        for cp in sends:
            cp.wait_send()

    return _Rider(flist, [jax.ShapeDtypeStruct(f.shape, f.dtype) for f in flist],
                  [pltpu.SemaphoreType.DMA((n,)), pltpu.SemaphoreType.DMA((n,))], start, finish, in_place=True)


def _fwd_pw1_glu(x, w1s, b1, tm):
    t, d = x.shape
    dh = d // 2

    def body(x_ref, w_ref, b_ref, a_ref, g_ref, u_ref):
        xb = x_ref[...].astype(BF16)
        for hh in range(2):
            cs = slice(hh * dh, (hh + 1) * dh)
            a = _dot(xb, w_ref[hh]) + b_ref[:, hh * dh:(hh + 1) * dh]
            g = _dot(xb, w_ref[2 + hh]) + b_ref[:, d + hh * dh:d + (hh + 1) * dh]
            a_ref[:, cs] = a.astype(BF16)
            g_ref[:, cs] = g.astype(BF16)
            u_ref[:, cs] = a * jax.nn.sigmoid(g)

    return pl.pallas_call(
        body, name="fwd_pw1_glu", grid=(t // tm,),
        in_specs=[_rows(tm, d), _const((4, d, dh)), _const((1, 2 * d))],
        out_specs=[_rows(tm, d)] * 3,
        out_shape=[jax.ShapeDtypeStruct((t, d), BF16), jax.ShapeDtypeStruct((t, d), BF16),
                   jax.ShapeDtypeStruct((t, d), F32)],
        compiler_params=_cparams(),
    )(x, w1s, b1)


def _fill_shifted(sh_ref, ext_ref):
    n = sh_ref.shape[1]
    for s in range(8):
        sh_ref[s] = ext_ref[pl.ds(s, n), :]


CONV_CHUNK = 64
LANES = 256


def _tap_sum(w_ref, sh, base, d, tap_row, out_ref, bias_ref=None):
    groups = CONV_CHUNK // 8
    for lg in range(d // LANES):
        ls = slice(lg * LANES, (lg + 1) * LANES)
        acc = jnp.zeros((groups, 8, LANES), F32)
        for k in range(CONV_WIDTH):
            e = tap_row(k)
            x = sh[e % 8, pl.ds(base + (e // 8) * 8, CONV_CHUNK), ls]
            acc = acc + w_ref[k, :, ls] * x.reshape(groups, 8, LANES)
        acc = acc.reshape(CONV_CHUNK, LANES)
        out_ref[pl.ds(base, CONV_CHUNK), ls] = acc if bias_ref is None else acc + bias_ref[:, ls]


def _fwd_conv_tail(u, x0, wdw, bdw, lng, lnb, w2, b2, mixg, mixb, tm, riders=()):
    t, d = u.shape
    hb = tm // CONV_HALO

    def body(u_ref, uh_ref, x_ref, w_ref, bdw_ref, lng_ref, lnb_ref, w2_ref, b2_ref, mg_ref, mb_ref,
             c_ref, z_ref, y_ref, ext, sh):
        i = pl.program_id(0)
        ext[0:CONV_HALO] = jnp.where(i == 0, 0.0, uh_ref[...])
        ext[CONV_HALO:CONV_HALO + tm] = u_ref[...]
        ext[CONV_HALO + tm:CONV_HALO + tm + 8] = jnp.zeros((8, d), F32)
        _fill_shifted(sh, ext)

        def chunk(r, carry):
            base = pl.multiple_of(r * CONV_CHUNK, CONV_CHUNK)
            _tap_sum(w_ref, sh, base, d, lambda k: k + CONV_HALO - (CONV_WIDTH - 1), c_ref, bdw_ref)
            return carry

        lax.fori_loop(0, tm // CONV_CHUNK, chunk, 0)
        n = _ln_fwd(c_ref[...], lng_ref[...], lnb_ref[...])
        s = n * jax.nn.sigmoid(n)
        m = _dot(s.astype(BF16), w2_ref[...]) + b2_ref[...]
        z = ALPHA * x_ref[...] + m
        z_ref[...] = z
        y_ref[...] = _ln_fwd(z, mg_ref[...], mb_ref[...])

    vec = _const((1, d))
    return _tc_call(
        body, name="fwd_conv_tail", nt=t // tm,
        in_specs=[_rows(tm, d), pl.BlockSpec((CONV_HALO, d), lambda i: (jnp.maximum(i * hb - 1, 0), 0)), _rows(tm, d),
                  _const((CONV_HALO, 8, d)), vec, vec, vec, _const((d, d)), vec, vec, vec],
        out_specs=[_rows(tm, d)] * 3,
        out_shape=[jax.ShapeDtypeStruct((t, d), F32)] * 3,
        scratch_shapes=[pltpu.VMEM((tm + CONV_HALO + 8, d), F32), pltpu.VMEM((8, tm + CONV_HALO, d), F32)],
        operands=(u, u, x0, wdw, bdw, lng, lnb, w2, b2, mixg, mixb), riders=riders)


def _fwd_ffn(x, wg, wu, wd, layer, lng, lnb, tm, target=None):
    t, d = x.shape
    fs = wg.shape[-1]
    nt = t // tm
    with_loss = target is not None

    def hidden(x_ref, wg_ref, wu_ref, wd_ref, act_ref, bm_ref, hm_ref):
        xv = x_ref[...]
        xb = xv.astype(BF16)
        f = jnp.zeros((tm, d), F32)
        for j in range(N_CHIPS):
            gj = _dot(xb, wg_ref[j])
            uj = _dot(xb, wu_ref[j])
            act, dact = _silu_and_grad(gj)
            act_ref[j] = act.astype(BF16)
            bm_ref[j] = (uj * dact).astype(BF16)
            hmb = (act * uj).astype(BF16)
            hm_ref[j] = hmb
            f = f + _dot(hmb, wd_ref[j])
        return ALPHA * xv + f

    def body(x_ref, wg_ref, wu_ref, wd_ref, g_ref, b_ref, act_ref, bm_ref, hm_ref, z_ref, y_ref):
        z = hidden(x_ref, wg_ref, wu_ref, wd_ref, act_ref, bm_ref, hm_ref)
        z_ref[...] = z
        y_ref[...] = _ln_fwd(z, g_ref[...], b_ref[...])

    def body_loss(x_ref, wg_ref, wu_ref, wd_ref, g_ref, b_ref, t_ref, act_ref, bm_ref, hm_ref, dz_ref,
                  dlg_ref, dlb_ref, loss_ref, acc_g, acc_b, acc_l):
        i = pl.program_id(0)
        _acc_init(i, acc_g, acc_b, acc_l)
        z = hidden(x_ref, wg_ref, wu_ref, wd_ref, act_ref, bm_ref, hm_ref)
        zhat, rstd = _ln_stats(z)
        gain = g_ref[...]
        err = zhat * gain + b_ref[...] - t_ref[...]
        acc_l[...] += _colsum8(err * err)
        dy = err * (1.0 / d)
        acc_g[...] += _colsum8(dy * zhat)
        acc_b[...] += _colsum8(dy)
        dzh = dy * gain
        m1 = jnp.mean(dzh, axis=-1, keepdims=True)
        m2 = jnp.mean(dzh * zhat, axis=-1, keepdims=True)
        dz_ref[...] = rstd * (dzh - m1 - zhat * m2)

        @pl.when(i == nt - 1)
        def _():
            dlg_ref[...] = jnp.sum(acc_g[...], axis=0, keepdims=True)
            dlb_ref[...] = jnp.sum(acc_b[...], axis=0, keepdims=True)
            loss_ref[...] = jnp.sum(acc_l[...], keepdims=True) * (0.5 / d)

    wcol = pl.BlockSpec((N_CHIPS, None, d, fs), lambda i: (0, layer, 0, 0), pipeline_mode=pl.Buffered(1))
    wrow = pl.BlockSpec((N_CHIPS, None, fs, d), lambda i: (0, layer, 0, 0), pipeline_mode=pl.Buffered(1))
    hid = pl.BlockSpec((N_CHIPS, tm, fs), lambda i: (0, i, 0))
    in_specs = [_rows(tm, d), wcol, wcol, wrow, _const((1, d)), _const((1, d))]
    hid_shapes = [jax.ShapeDtypeStruct((N_CHIPS, t, fs), BF16)] * 3
    if not with_loss:
        return pl.pallas_call(
            body, name=f"fwd_ffn{layer}", grid=(nt,), in_specs=in_specs,
            out_specs=[hid, hid, hid, _rows(tm, d), _rows(tm, d)],
            out_shape=hid_shapes + [jax.ShapeDtypeStruct((t, d), F32)] * 2, compiler_params=_cparams(),
        )(x, wg, wu, wd, lng, lnb)
    return pl.pallas_call(
        body_loss, name=f"fwd_ffn{layer}_loss", grid=(nt,), in_specs=in_specs + [_rows(tm, d)],
        out_specs=[hid, hid, hid, _rows(tm, d), _acc_out((1, d)), _acc_out((1, d)), _acc_out((1, 1))],
        out_shape=hid_shapes + [jax.ShapeDtypeStruct((t, d), F32)] + [jax.ShapeDtypeStruct((1, d), F32)] * 2
        + [jax.ShapeDtypeStruct((1, 1), F32)],
        scratch_shapes=[pltpu.VMEM((8, d), F32)] * 3, compiler_params=_cparams(),
    )(x, wg, wu, wd, lng, lnb, target)


def _attn_band():
    kt = lax.broadcasted_iota(jnp.int32, (BLOCK, BLOCK), 0)
    qi = lax.broadcasted_iota(jnp.int32, (BLOCK, BLOCK), 1)
    current = kt <= qi
    delta = qi - kt + jnp.where(current, 0, BLOCK)
    return current, delta.astype(F32)


def _fold(full, current):
    return jnp.where(current, full[BLOCK:2 * BLOCK], full[0:BLOCK])


def _unfold(folded, current):
    zero = jnp.zeros_like(folded)
    return jnp.concatenate([jnp.where(current, zero, folded), jnp.where(current, folded, zero)], axis=0)


def _slope(h, nq):
    return 2.0 ** (-ALIBI_MAX * (h + 1) / nq)


def _softmax_with_sink(s_full, slope, band, has_previous, sink):
    current, delta = band
    s = _fold(s_full, current) * (1.0 / math.sqrt(HEAD_DIM)) - slope * delta
    s = jnp.where(jnp.logical_or(current, has_previous), s, NEG_INF)
    m = jnp.maximum(jnp.max(s, axis=0, keepdims=True), sink)
    p = jnp.exp(s - m)
    e_sink = jnp.exp(sink - m)
    inv = 1.0 / (jnp.sum(p, axis=0, keepdims=True) + e_sink)
    return p * inv, e_sink * inv


def _heads_on_lanes(ref, b, g, group):
    first = g * group
    return jnp.concatenate([ref[b, (first + hh) * HEAD_DIM:(first + hh + 1) * HEAD_DIM, :] for hh in range(group)],
                           axis=1)


def _fill_kv(kv_scr, halo, tile, tm):
    for j in range(2 * N_KV_HEADS):
        kv_scr[j, 0:BLOCK] = halo[:, j * HEAD_DIM:(j + 1) * HEAD_DIM]
        kv_scr[j, BLOCK:BLOCK + tm] = tile[:, j * HEAD_DIM:(j + 1) * HEAD_DIM]


def _cols(d, tm):
    return pl.BlockSpec((d, tm), lambda i: (0, i))


def _fwd_attn(x, wqt, bqt, wkv, bkv, sinks, wo, bo, mixg, mixb, tm):
    t, d = x.shape
    nq = d // HEAD_DIM
    group = nq // N_KV_HEADS
    nb = tm // BLOCK

    def body(sink_ref, x_ref, xh_ref, wqt_ref, bqt_ref, wkv_ref, bkv_ref, wo_ref, bo_ref, mg_ref, mb_ref,
             qt_ref, kv_ref, ot_ref, z_ref, y_ref, kv_scr, qt_scr, ot_scr):
        i = pl.program_id(0)
        xv = x_ref[...]
        xb = xv.astype(BF16)
        qt = (_dot_nt(wqt_ref[...], xb) + bqt_ref[...]).astype(BF16)
        qt_ref[...] = qt
        for b in range(nb):
            qt_scr[b] = qt[:, b * BLOCK:(b + 1) * BLOCK]
        kvb = (_dot(xb, wkv_ref[...]) + bkv_ref[...]).astype(BF16)
        kv_ref[...] = kvb
        _fill_kv(kv_scr, (_dot(xh_ref[...].astype(BF16), wkv_ref[...]) + bkv_ref[...]).astype(BF16), kvb, tm)
        band = _attn_band()

        def block(b, carry):
            r0 = pl.multiple_of(b * BLOCK, BLOCK)
            has_previous = jnp.logical_or(i > 0, b > 0)
            for g in range(N_KV_HEADS):
                kk = kv_scr[g, pl.ds(r0, 2 * BLOCK), :]
                vv = kv_scr[N_KV_HEADS + g, pl.ds(r0, 2 * BLOCK), :]
                s_all = _dot(kk, _heads_on_lanes(qt_scr, b, g, group))
                probs = []
                for hh in range(group):
                    h = g * group + hh
                    p, _ = _softmax_with_sink(s_all[:, hh * BLOCK:(hh + 1) * BLOCK], _slope(h, nq), band,
                                              has_previous, sink_ref[h])
                    probs.append(_unfold(p.astype(BF16), band[0]))
                o_all = _dot_tn(vv, jnp.concatenate(probs, axis=1))
                for hh in range(group):
                    h = g * group + hh
                    ot_scr[b, h * HEAD_DIM:(h + 1) * HEAD_DIM, :] = o_all[:, hh * BLOCK:(hh + 1) * BLOCK].astype(BF16)
            return carry

        lax.fori_loop(0, nb, block, 0)
        ot = jnp.concatenate([ot_scr[b] for b in range(nb)], axis=1)
        ot_ref[...] = ot
        z = ALPHA * xv + _dot_tn(ot, wo_ref[...]) + bo_ref[...]
        z_ref[...] = z
        y_ref[...] = _ln_fwd(z, mg_ref[...], mb_ref[...])

    hb = tm // BLOCK
    vec = _const((1, d))
    return pl.pallas_call(
        body, name="fwd_attn", grid=(t // tm,),
        in_specs=[pl.BlockSpec(memory_space=pltpu.SMEM),
                  _rows(tm, d), pl.BlockSpec((BLOCK, d), lambda i: (jnp.maximum(i * hb - 1, 0), 0)),
                  _const((d, d)), _const((d, 1)), _const((d, 2 * KVD)), _const((1, 2 * KVD)), _const((d, d)), vec, vec,
                  vec],
        out_specs=[_cols(d, tm), _rows(tm, 2 * KVD), _cols(d, tm), _rows(tm, d), _rows(tm, d)],
        out_shape=[jax.ShapeDtypeStruct((d, t), BF16), jax.ShapeDtypeStruct((t, 2 * KVD), BF16),
                   jax.ShapeDtypeStruct((d, t), BF16), jax.ShapeDtypeStruct((t, d), F32),
                   jax.ShapeDtypeStruct((t, d), F32)],
        scratch_shapes=[pltpu.VMEM((2 * N_KV_HEADS, tm + BLOCK, HEAD_DIM), BF16), pltpu.VMEM((nb, d, BLOCK), BF16),
                        pltpu.VMEM((nb, d, BLOCK), BF16)],
        compiler_params=_cparams(),
    )(sinks, x, x, wqt, bqt, wkv, bkv, wo, bo, mixg, mixb)


def _write_sums(i, nt, pairs):
    @pl.when(i == nt - 1)
    def _():
        for out_ref, acc in pairs:
            out_ref[...] = jnp.sum(acc[...], axis=0, keepdims=True)


def _bwd_ffn_dx(dz, act, bm, wg, wu, wd, layer, tm, riders=()):
    t, d = dz.shape
    fs = wg.shape[-1]

    def body(dz_ref, act_ref, bm_ref, wg_ref, wu_ref, wd_ref, dgg_ref, duu_ref, dx_ref):
        dzv = dz_ref[...]
        dzb = dzv.astype(BF16)
        dx = ALPHA * dzv
        for j in range(N_CHIPS):
            dh = _dot_nt(dzb, wd_ref[j])
            dgb = (dh * bm_ref[j].astype(F32)).astype(BF16)
            dub = (dh * act_ref[j].astype(F32)).astype(BF16)
            dgg_ref[j] = dgb
            duu_ref[j] = dub
            dx = dx + _dot_nt(dgb, wg_ref[j]) + _dot_nt(dub, wu_ref[j])
        dx_ref[...] = dx

    wcol = pl.BlockSpec((N_CHIPS, None, d, fs), lambda i: (0, layer, 0, 0), pipeline_mode=pl.Buffered(1))
    wrow = pl.BlockSpec((N_CHIPS, None, fs, d), lambda i: (0, layer, 0, 0), pipeline_mode=pl.Buffered(1))
    hid = pl.BlockSpec((N_CHIPS, tm, fs), lambda i: (0, i, 0))
    return _tc_call(
        body, name=f"bwd_ffn_dx{layer}", nt=t // tm, in_specs=[_rows(tm, d), hid, hid, wcol, wcol, wrow],
        out_specs=[hid, hid, _rows(tm, d)],
        out_shape=[jax.ShapeDtypeStruct((N_CHIPS, t, fs), BF16)] * 2 + [jax.ShapeDtypeStruct((t, d), F32)],
        operands=(dz, act, bm, wg, wu, wd), riders=riders)


def _matmul_tn(a, b, tt, name):
    ja, t, ka = a.shape
    jb, _, nb = b.shape
    nj = max(ja, jb)

    def body(a_ref, b_ref, o_ref):
        @pl.when(pl.program_id(0) == 0)
        def _():
            o_ref[...] = jnp.zeros_like(o_ref)

        a0 = a_ref[0].astype(BF16) if ja == 1 else None
        b0 = b_ref[0].astype(BF16) if jb == 1 else None
        for j in range(nj):
            aj = a0 if ja == 1 else a_ref[j].astype(BF16)
            bj = b0 if jb == 1 else b_ref[j].astype(BF16)
            o_ref[j] += _dot_tn(aj, bj)

    return pl.pallas_call(
        body, name=name, grid=(t // tt,),
        in_specs=[pl.BlockSpec((ja, tt, ka), lambda i: (0, i, 0)), pl.BlockSpec((jb, tt, nb), lambda i: (0, i, 0))],
        out_specs=pl.BlockSpec((nj, ka, nb), lambda i: (0, 0, 0)), out_shape=jax.ShapeDtypeStruct((nj, ka, nb), F32),
        compiler_params=_cparams(1),
    )(a, b)


def _matmul_nn(at, b, tt, name):
    ka, t = at.shape
    nb = b.shape[1]

    def body(a_ref, b_ref, o_ref):
        @pl.when(pl.program_id(0) == 0)
        def _():
            o_ref[...] = jnp.zeros_like(o_ref)

        o_ref[...] += _dot(a_ref[...].astype(BF16), b_ref[...].astype(BF16))

    return pl.pallas_call(
        body, name=name, grid=(t // tt,),
        in_specs=[pl.BlockSpec((ka, tt), lambda i: (0, i)), pl.BlockSpec((tt, nb), lambda i: (i, 0))],
        out_specs=pl.BlockSpec((ka, nb), lambda i: (0, 0)), out_shape=jax.ShapeDtypeStruct((ka, nb), F32),
        compiler_params=_cparams(1),
    )(at, b)


def _bwd_attn(dy, z, qt, kv, sinks, wo, wqt, wkv, mixg, z_in, g_in, tm, riders=()):
    t, d = dy.shape
    nq = d // HEAD_DIM
    group = nq // N_KV_HEADS
    nb = tm // BLOCK
    nt = t // tm
    hb = tm // BLOCK
    n_kv = 2 * N_KV_HEADS

    def body(sink_ref, dy_ref, z_ref, qt_ref, kv_ref, kvh_ref, wo_ref, wqt_ref, wkv_ref, g_ref, zin_ref, gin_ref,
             dz_ref, dqt_ref, dkv_ref, dx_ref, dlg_ref, dlb_ref, dbo_ref, dbq_ref, dbkv_ref, dsink_ref, ding_ref,
             dinb_ref, kv_scr, dkv_scr, qt_scr, dot_scr, dqt_scr, carry, acc_g, acc_b, acc_o, acc_q, acc_kv, acc_s,
             acc_ig, acc_ib):
        i = pl.program_id(0)
        ti = nt - 1 - i
        _acc_init(i, carry, acc_g, acc_b, acc_o, acc_q, acc_kv, acc_s, acc_ig, acc_ib)
        dyv = dy_ref[...]
        dz, zhat = _ln_bwd(dyv, z_ref[...], g_ref[...])
        acc_g[...] += _colsum8(dyv * zhat)
        acc_b[...] += _colsum8(dyv)
        acc_o[...] += _colsum8(dz)
        dzb = dz.astype(BF16)
        dz_ref[...] = dzb
        do_t = _dot_nt(wo_ref[...], dzb).astype(BF16)
        for b in range(nb):
            dot_scr[b] = do_t[:, b * BLOCK:(b + 1) * BLOCK]
            qt_scr[b] = qt_ref[:, b * BLOCK:(b + 1) * BLOCK]
        _fill_kv(kv_scr, kvh_ref[...], kv_ref[...], tm)
        dkv_scr[:, 0:tm] = jnp.zeros((n_kv, tm, HEAD_DIM), F32)
        dkv_scr[:, tm:tm + BLOCK] = carry[...]
        band = _attn_band()

        def block(b, c):
            r0 = pl.multiple_of(b * BLOCK, BLOCK)
            has_previous = jnp.logical_or(ti > 0, b > 0)
            for g in range(N_KV_HEADS):
                kk = kv_scr[g, pl.ds(r0, 2 * BLOCK), :]
                vv = kv_scr[N_KV_HEADS + g, pl.ds(r0, 2 * BLOCK), :]
                q_all = _heads_on_lanes(qt_scr, b, g, group)
                do_all = _heads_on_lanes(dot_scr, b, g, group)
                s_all = _dot(kk, q_all)
                dp_all = _dot(vv, do_all)
                probs, dscores = [], []
                for hh in range(group):
                    h = g * group + hh
                    cols = slice(hh * BLOCK, (hh + 1) * BLOCK)
                    p, p_sink = _softmax_with_sink(s_all[:, cols], _slope(h, nq), band, has_previous, sink_ref[h])
                    dp = _fold(dp_all[:, cols], band[0])
                    rs = jnp.sum(p * dp, axis=0, keepdims=True)
                    acc_s[h:h + 1, :] += -(p_sink * rs)
                    ds = p * (dp - rs) * (1.0 / math.sqrt(HEAD_DIM))
                    probs.append(_unfold(p.astype(BF16), band[0]))
                    dscores.append(_unfold(ds.astype(BF16), band[0]))
                p_all = jnp.concatenate(probs, axis=1)
                ds_all = jnp.concatenate(dscores, axis=1)
                dq_all = _dot_tn(kk, ds_all)
                for hh in range(group):
                    h = g * group + hh
                    dqt_scr[b, h * HEAD_DIM:(h + 1) * HEAD_DIM, :] = dq_all[:, hh * BLOCK:(hh + 1) * BLOCK]
                dkv_scr[g, pl.ds(r0, 2 * BLOCK), :] += _dot_nt(ds_all, q_all)
                dkv_scr[N_KV_HEADS + g, pl.ds(r0, 2 * BLOCK), :] += _dot_nt(p_all, do_all)
            return c

        lax.fori_loop(0, nb, block, 0)
        carry[...] = dkv_scr[:, 0:BLOCK]
        dkv = jnp.concatenate([dkv_scr[j, BLOCK:BLOCK + tm] for j in range(n_kv)], axis=1)
        acc_kv[...] += _colsum8(dkv)
        dkvb = dkv.astype(BF16)
        dkv_ref[...] = dkvb
        dqt = jnp.concatenate([dqt_scr[b] for b in range(nb)], axis=1)
        for b in range(nb):
            acc_q[...] += dqt_scr[b]
        dqtb = dqt.astype(BF16)
        dqt_ref[...] = dqtb
        dx = ALPHA * dz + _dot_tn(dqtb, wqt_ref[...]) + _dot_nt(dkvb, wkv_ref[...])
        dz_in, zhat_in = _ln_bwd(dx, zin_ref[...], gin_ref[...])
        acc_ig[...] += _colsum8(dx * zhat_in)
        acc_ib[...] += _colsum8(dx)
        dx_ref[...] = dz_in
        _write_sums(i, nt, [(dlg_ref, acc_g), (dlb_ref, acc_b), (dbo_ref, acc_o), (dbkv_ref, acc_kv),
                            (ding_ref, acc_ig), (dinb_ref, acc_ib)])

        @pl.when(i == nt - 1)
        def _():
            dbq_ref[...] = jnp.sum(acc_q[...], axis=1, keepdims=True)
            dsink_ref[...] = jnp.sum(acc_s[...], axis=1, keepdims=True)

    rev = lambda w: pl.BlockSpec((tm, w), lambda i: (nt - 1 - i, 0))
    rev_cols = pl.BlockSpec((d, tm), lambda i: (0, nt - 1 - i))
    vec = _const((1, d))
    return _tc_call(
        body, name="bwd_attn", nt=nt,
        in_specs=[pl.BlockSpec(memory_space=pltpu.SMEM), rev(d), rev(d), rev_cols, rev(2 * KVD),
                  pl.BlockSpec((BLOCK, 2 * KVD), lambda i: (jnp.maximum((nt - 1 - i) * hb - 1, 0), 0)),
                  _const((d, d)), _const((d, d)), _const((d, 2 * KVD)), vec, rev(d), vec],
        out_specs=[rev(d), rev_cols, rev(2 * KVD), rev(d)] + [_acc_out((1, d))] * 3
        + [_acc_out((d, 1)), _acc_out((1, 2 * KVD)), _acc_out((nq, 1)), _acc_out((1, d)), _acc_out((1, d))],
        out_shape=[jax.ShapeDtypeStruct((t, d), BF16), jax.ShapeDtypeStruct((d, t), BF16),
                   jax.ShapeDtypeStruct((t, 2 * KVD), BF16), jax.ShapeDtypeStruct((t, d), F32)]
        + [jax.ShapeDtypeStruct((1, d), F32)] * 3
        + [jax.ShapeDtypeStruct((d, 1), F32), jax.ShapeDtypeStruct((1, 2 * KVD), F32),
           jax.ShapeDtypeStruct((nq, 1), F32)] + [jax.ShapeDtypeStruct((1, d), F32)] * 2,
        scratch_shapes=[pltpu.VMEM((n_kv, tm + BLOCK, HEAD_DIM), BF16), pltpu.VMEM((n_kv, tm + BLOCK, HEAD_DIM), F32),
                        pltpu.VMEM((nb, d, BLOCK), BF16), pltpu.VMEM((nb, d, BLOCK), BF16),
                        pltpu.VMEM((nb, d, BLOCK), F32), pltpu.VMEM((n_kv, BLOCK, HEAD_DIM), F32),
                        pltpu.VMEM((8, d), F32), pltpu.VMEM((8, d), F32), pltpu.VMEM((8, d), F32),
                        pltpu.VMEM((d, BLOCK), F32), pltpu.VMEM((8, 2 * KVD), F32), pltpu.VMEM((nq, BLOCK), F32),
                        pltpu.VMEM((8, d), F32), pltpu.VMEM((8, d), F32)],
        operands=(sinks, dy, z, qt, kv, kv, wo, wqt, wkv, mixg, z_in, g_in), riders=riders)


def _bwd_conv_head(dy, z, c, w2, mixg, lng, lnb, tm, riders=()):
    t, d = dy.shape
    nt = t // tm

    def body(dy_ref, z_ref, c_ref, w2_ref, mg_ref, lg_ref, lb_ref,
             dz_ref, s_ref, dc_ref, dmg_ref, dmb_ref, db2_ref, dlg_ref, dlb_ref, a0, a1, a2, a3, a4):
        i = pl.program_id(0)
        _acc_init(i, a0, a1, a2, a3, a4)
        dyv = dy_ref[...]
        dz, zhat = _ln_bwd(dyv, z_ref[...], mg_ref[...])
        a0[...] += _colsum8(dyv * zhat)
        a1[...] += _colsum8(dyv)
        a2[...] += _colsum8(dz)
        dz_ref[...] = dz
        chat, rstd = _ln_stats(c_ref[...])
        n = chat * lg_ref[...] + lb_ref[...]
        act, dact = _silu_and_grad(n)
        s_ref[...] = act.astype(BF16)
        dn = _dot_nt(dz.astype(BF16), w2_ref[...]) * dact
        a3[...] += _colsum8(dn * chat)
        a4[...] += _colsum8(dn)
        dch = dn * lg_ref[...]
        m1 = jnp.mean(dch, axis=-1, keepdims=True)
        m2 = jnp.mean(dch * chat, axis=-1, keepdims=True)
        dc_ref[...] = rstd * (dch - m1 - chat * m2)
        _write_sums(i, nt, [(dmg_ref, a0), (dmb_ref, a1), (db2_ref, a2), (dlg_ref, a3), (dlb_ref, a4)])

    vec = _const((1, d))
    return _tc_call(
        body, name="bwd_conv_head", nt=nt,
        in_specs=[_rows(tm, d), _rows(tm, d), _rows(tm, d), _const((d, d)), vec, vec, vec],
        out_specs=[_rows(tm, d), _rows(tm, d), _rows(tm, d)] + [_acc_out((1, d))] * 5,
        out_shape=[jax.ShapeDtypeStruct((t, d), F32), jax.ShapeDtypeStruct((t, d), BF16),
                   jax.ShapeDtypeStruct((t, d), F32)] + [jax.ShapeDtypeStruct((1, d), F32)] * 5,
        scratch_shapes=[pltpu.VMEM((8, d), F32)] * 5, operands=(dy, z, c, w2, mixg, lng, lnb), riders=riders)


def _bwd_conv_glu(dc, u, a, g, dz, wdw, w1s, tm, riders=()):
    t, d = dc.shape
    dh_w = d // 2
    nt = t // tm
    hb = tm // CONV_HALO
    last_halo = t // CONV_HALO - 1

    def body(dc_ref, dcn_ref, u_ref, a_ref, g_ref, dz_ref, w_ref, w1_ref,
             dx_ref, dh_ref, db1_ref, dbdw_ref, dw_ref, ext, sh, du_scr, acc_b1, acc_bdw, acc_w):
        i = pl.program_id(0)
        _acc_init(i, acc_b1, acc_bdw, acc_w)
        dcv = dc_ref[...]
        acc_bdw[...] += _colsum8(dcv)

        ext[0:tm] = dcv
        ext[tm:tm + CONV_HALO] = jnp.where(i == nt - 1, 0.0, dcn_ref[...])
        ext[tm + CONV_HALO:tm + CONV_HALO + 8] = jnp.zeros((8, d), F32)
        _fill_shifted(sh, ext)

        def du_chunk(r, carry):
            base = pl.multiple_of(r * CONV_CHUNK, CONV_CHUNK)
            _tap_sum(w_ref, sh, base, d, lambda k: CONV_WIDTH - 1 - k, du_scr)
            return carry

        lax.fori_loop(0, tm // CONV_CHUNK, du_chunk, 0)

        def dw_chunk(r, carry):
            base = pl.multiple_of(r * CONV_CHUNK, CONV_CHUNK)
            groups = CONV_CHUNK // 8
            for lg in range(d // LANES):
                ls = slice(lg * LANES, (lg + 1) * LANES)
                uv = u_ref[pl.ds(base, CONV_CHUNK), ls].reshape(groups, 8, LANES)
                for k in range(CONV_WIDTH):
                    e = CONV_WIDTH - 1 - k
                    x = sh[e % 8, pl.ds(base + (e // 8) * 8, CONV_CHUNK), ls].reshape(groups, 8, LANES)
                    acc_w[k, :, ls] += jnp.sum(uv * x, axis=0)
            return carry

        lax.fori_loop(0, tm // CONV_CHUNK, dw_chunk, 0)

        du = du_scr[...]
        av = a_ref[...].astype(F32)
        sg = jax.nn.sigmoid(g_ref[...].astype(F32))
        da = du * sg
        dg = du * av * sg * (1.0 - sg)
        acc_b1[:, 0:d] += _colsum8(da)
        acc_b1[:, d:2 * d] += _colsum8(dg)
        dx = ALPHA * dz_ref[...]
        for j, part in enumerate([da[:, 0:dh_w], da[:, dh_w:d], dg[:, 0:dh_w], dg[:, dh_w:d]]):
            pb = part.astype(BF16)
            dh_ref[j] = pb
            dx = dx + _dot_nt(pb, w1_ref[j])
        dx_ref[...] = dx

        @pl.when(i == nt - 1)
        def _():
            db1_ref[...] = jnp.sum(acc_b1[...], axis=0, keepdims=True)
            dbdw_ref[...] = jnp.sum(acc_bdw[...], axis=0, keepdims=True)
            dw_ref[...] = jnp.sum(acc_w[...], axis=1)

    return _tc_call(
        body, name="bwd_conv_glu", nt=nt,
        in_specs=[_rows(tm, d), pl.BlockSpec((CONV_HALO, d), lambda i: (jnp.minimum((i + 1) * hb, last_halo), 0)),
                  _rows(tm, d), _rows(tm, d), _rows(tm, d), _rows(tm, d), _const((CONV_HALO, 8, d)),
                  _const((4, d, dh_w))],
        out_specs=[_rows(tm, d), pl.BlockSpec((4, tm, dh_w), lambda i: (0, i, 0)), _acc_out((1, 2 * d)),
                   _acc_out((1, d)), _acc_out((CONV_HALO, d))],
        out_shape=[jax.ShapeDtypeStruct((t, d), F32), jax.ShapeDtypeStruct((4, t, dh_w), BF16),
                   jax.ShapeDtypeStruct((1, 2 * d), F32), jax.ShapeDtypeStruct((1, d), F32),
                   jax.ShapeDtypeStruct((CONV_HALO, d), F32)],
        scratch_shapes=[pltpu.VMEM((tm + CONV_HALO + 8, d), F32), pltpu.VMEM((8, tm + CONV_HALO, d), F32),
                        pltpu.VMEM((tm, d), F32), pltpu.VMEM((8, 2 * d), F32), pltpu.VMEM((8, d), F32),
                        pltpu.VMEM((CONV_HALO, 8, d), F32)],
        operands=(dc, dc, u, a, g, dz, wdw, w1s), riders=riders)


def _row_block(rows, target):
    best = rows
    for cand in range(8, min(rows, target) + 1, 8):
        if rows % cand == 0:
            best = cand
    return best


def _adamw(w, g, m, v, name):
    rows, lanes = w.shape
    br = _row_block(rows, 512) if rows % 8 == 0 else rows

    def body(w_ref, g_ref, m_ref, v_ref, d_ref, nm_ref, nv_ref):
        gv = g_ref[...]
        nm = ADAM_B1 * m_ref[...] + (1.0 - ADAM_B1) * gv
        nv = ADAM_B2 * v_ref[...] + (1.0 - ADAM_B2) * (gv * gv)
        m_hat = nm / (1.0 - ADAM_B1 ** ADAM_STEP)
        v_hat = nv / (1.0 - ADAM_B2 ** ADAM_STEP)
        d_ref[...] = -ADAM_LR * (m_hat / (jnp.sqrt(v_hat) + ADAM_EPS) + ADAM_WD * w_ref[...])
        nm_ref[...] = nm
        nv_ref[...] = nv

    spec = pl.BlockSpec((br, lanes), lambda i: (i, 0))
    return pl.pallas_call(
        body, name=name, grid=(rows // br,), in_specs=[spec] * 4, out_specs=[spec] * 3,
        out_shape=[jax.ShapeDtypeStruct((rows, lanes), F32)] * 3, compiler_params=_cparams(),
    )(w, g, m, v)


def _pad_to(v, n):
    return jnp.pad(v, (0, n - v.shape[0]))


def _round_up(n, m):
    return (n + m - 1) // m * m


def kernel(x, conv_w_pw1, conv_b_pw1, conv_w_dw, conv_b_dw, conv_ln_g, conv_ln_b, conv_w_pw2, conv_b_pw2, kv_w_k, kv_b_k, kv_w_v, kv_b_v, attn_w_q, attn_b_q, attn_sinks, attn_w_o, attn_b_o, ffn_w_gate, ffn_w_up, ffn_w_down, ln_mix_g, ln_mix_b, ln_ffn_g, ln_ffn_b, loss_target, m_conv_w_pw1, m_conv_b_pw1, m_conv_w_dw, m_conv_b_dw, m_conv_ln_g, m_conv_ln_b, m_conv_w_pw2, m_conv_b_pw2, m_kv_w_k, m_kv_b_k, m_kv_w_v, m_kv_b_v, m_attn_w_q, m_attn_b_q, m_attn_sinks, m_attn_w_o, m_attn_b_o, m_ffn_w_gate, m_ffn_w_up, m_ffn_w_down, m_ln_mix_g, m_ln_mix_b, m_ln_ffn_g, m_ln_ffn_b, v_conv_w_pw1, v_conv_b_pw1, v_conv_w_dw, v_conv_b_dw, v_conv_ln_g, v_conv_ln_b, v_conv_w_pw2, v_conv_b_pw2, v_kv_w_k, v_kv_b_k, v_kv_w_v, v_kv_b_v, v_attn_w_q, v_attn_b_q, v_attn_sinks, v_attn_w_o, v_attn_b_o, v_ffn_w_gate, v_ffn_w_up, v_ffn_w_down, v_ln_mix_g, v_ln_mix_b, v_ln_ffn_g, v_ln_ffn_b):
    args = dict(locals())
    w = {n: args[n] for n in WEIGHTS}
    mom = {n: args["m_" + n] for n in WEIGHTS}
    var = {n: args["v_" + n] for n in WEIGHTS}
    assert x.shape[0] == 1, "one sequence per device"
    t, d = x.shape[1], x.shape[2]
    dq = d // 4
    fs = ffn_w_gate.shape[-1]
    nq = d // HEAD_DIM
    x0 = x.reshape(t, d)
    target = loss_target.reshape(t, d)
    tm_big = min(512, t)
    tm_mid = min(256, t)
    tm_tn = min(1024, t)
    c_idx = lax.axis_index("c")

    me_idx = 2 * lax.axis_index("x") + lax.axis_index("y")

    def gather_buffer(v):
        buf = lax.empty((N_CHIPS,) + v.shape, v.dtype)
        return lax.dynamic_update_slice(buf, v[None], (me_idx,) + (0,) * v.ndim)

    def halves(v):
        return v.reshape(2, -1, v.shape[-1])

    small_sizes = [int(w[n].size) for n in SMALL_SHARDED]
    rs = _round_up(sum(small_sizes), 8 * 128) // 128
    spack = _pad_to(jnp.concatenate([w[n].reshape(-1) for n in SMALL_SHARDED]), rs * 128).reshape(rs, 128)
    conv_first = ['conv_w_pw1', 'conv_w_pw2']
    later = [n for n in BIG if n not in conv_first]
    (first_out,) = _run_riders(
        [_all_gather_rider([gather_buffer(halves(w[n].astype(BF16))) for n in conv_first], gather_buffer(spack))],
        "all_gather_conv")
    later_rider = _all_gather_rider([gather_buffer(halves(w[n].astype(BF16))) for n in later])
    gs = first_out[-1].reshape(N_CHIPS, rs * 128)
    full = {n: g.reshape((N_CHIPS,) + w[n].shape) for n, g in zip(conv_first, first_out)}
    off = 0
    for n, size in zip(SMALL_SHARDED, small_sizes):
        full[n] = gs[:, off:off + size].reshape((N_CHIPS,) + w[n].shape)
        off += size
    w1s = full['conv_w_pw1'].reshape(N_CHIPS, d, d // 2)
    w2 = full['conv_w_pw2'].reshape(d, d)
    b1 = full['conv_b_pw1'].reshape(1, 2 * d)
    wdw = jnp.pad(full['conv_w_dw'].reshape(N_CHIPS, CONV_WIDTH, dq).transpose(1, 0, 2).reshape(CONV_WIDTH, d),
                  ((0, CONV_HALO - CONV_WIDTH), (0, 0)))
    wdw = jnp.broadcast_to(wdw[:, None, :], (CONV_HALO, 8, d))
    bdw = full['conv_b_dw'].reshape(1, d)
    clng = full['conv_ln_g'].reshape(1, d)
    clnb = full['conv_ln_b'].reshape(1, d)
    b2 = full['conv_b_pw2'].reshape(1, d)
    bkv = jnp.concatenate([kv_b_k, kv_b_v]).reshape(1, 2 * KVD)
    sinks = attn_sinks.reshape(nq)
    mixg = [ln_mix_g[l].reshape(1, d) for l in range(DEPTH)]
    mixb = [ln_mix_b[l].reshape(1, d) for l in range(DEPTH)]
    ffng = [ln_ffn_g[l].reshape(1, d) for l in range(DEPTH)]
    ffnb = [ln_ffn_b[l].reshape(1, d) for l in range(DEPTH)]

    a_act, g_act, u_act = _fwd_pw1_glu(x0, w1s, b1, tm_big)
    (c_act, z1, x1), (later_out,) = _fwd_conv_tail(u_act, x0, wdw, bdw, clng, clnb, w2, b2, mixg[0], mixb[0], tm_mid,
                                                   riders=[later_rider])
    full.update({n: g.reshape((N_CHIPS,) + w[n].shape) for n, g in zip(later, later_out)})
    wkv = jnp.concatenate([full['kv_w_k'].reshape(d, KVD), full['kv_w_v'].reshape(d, KVD)], axis=1)
    wqt = full['attn_w_q'].reshape(d, d).T
    wo = full['attn_w_o'].reshape(d, d)
    wg, wu, wd = full['ffn_w_gate'], full['ffn_w_up'], full['ffn_w_down']
    act0, bm0, hm0, z2, x2 = _fwd_ffn(x1, wg, wu, wd, 0, ffng[0], ffnb[0], tm_big)
    qt_act, kv_act, ot_act, z3, x3 = _fwd_attn(x2, wqt, attn_b_q.reshape(d, 1), wkv, bkv, sinks, wo, attn_b_o,
                                               mixg[1], mixb[1], tm_big)
    act1, bm1, hm1, dz4, d_fg1, d_fb1, loss_part = _fwd_ffn(x3, wg, wu, wd, 1, ffng[1], ffnb[1], tm_big, target=target)
    loss = lax.psum(loss_part[0, 0], ("x", "y", "c"))

    c_arr = c_idx.reshape(1).astype(jnp.int32)

    def halves4(v):
        return v.reshape(N_CHIPS, 2, -1, v.shape[-1])

    def arrays(group):
        return [p for _, p in group]

    def pair_sums(group, got):
        return [_pair_sum(p, g, c_arr, "grad_pair_sum_" + n) for (n, p), g in zip(group, got)]

    pos_arr = jnp.stack([me_idx, c_idx]).astype(jnp.int32)

    def chip_sums(group, sums, got):
        return [_chip_sum(s, g, pos_arr, "grad_chip_sum_" + n) for (n, _), s, g in zip(group, sums, got)]

    (dgg1, duu1, dx3), _ = _bwd_ffn_dx(dz4, act1, bm1, wg, wu, wd, 1, tm_big)
    g1 =[("ffn_w_gate1", halves4(_matmul_tn(x3[None], dgg1, tm_tn, "dw_gate1"))),
          ("ffn_w_up1", halves4(_matmul_tn(x3[None], duu1, tm_tn, "dw_up1"))),
          ("ffn_w_down1", halves4(_matmul_tn(hm1, dz4[None], tm_tn, "dw_down1")))]
    (dz3, dqt, dkv, dz2, d_mg1, d_mb1, d_bo, d_bq, d_bkv, d_sinks, d_fg0, d_fb0), (got1,) = _bwd_attn(
        dx3, z3, qt_act, kv_act, sinks, wo, wqt, wkv, mixg[1], z2, ffng[0], tm_big,
        riders=[_pair_exchange_rider(arrays(g1))])
    s1 = pair_sums(g1, got1)
    dwo = _matmul_nn(ot_act, dz3, tm_tn, "dw_o")
    dwq = _matmul_nn(dqt, x2, tm_tn, "dw_q").T
    dwkv = _matmul_tn(x2[None], dkv[None], tm_tn, "dw_kv")[0]
    g2 = [("attn_w_o", halves4(dwo)), ("attn_w_q", halves4(dwq)),
          ("kv_w_k", halves4(dwkv[:, 0:KVD])), ("kv_w_v", halves4(dwkv[:, KVD:2 * KVD]))]
    (dgg0, duu0, dx1), (from_chips1, got2) = _bwd_ffn_dx(
        dz2, act0, bm0, wg, wu, wd, 0, tm_big, riders=[_chip_scatter_rider(s1), _pair_exchange_rider(arrays(g2))])
    f1 = chip_sums(g1, s1, from_chips1)
    s2 = pair_sums(g2, got2)
    g3 = [("ffn_w_gate0", halves4(_matmul_tn(x1[None], dgg0, tm_tn, "dw_gate0"))),
          ("ffn_w_up0", halves4(_matmul_tn(x1[None], duu0, tm_tn, "dw_up0"))),
          ("ffn_w_down0", halves4(_matmul_tn(hm0, dz2[None], tm_tn, "dw_down0")))]
    (dz1, s_act, dc, d_mg0, d_mb0, d_b2, d_clng, d_clnb), (got3, shared1) = _bwd_conv_head(
        dx1, z1, c_act, w2, mixg[0], clng, clnb, tm_mid,
        riders=[_pair_exchange_rider(arrays(g3)), _pair_share_rider(f1)])
    s3 = pair_sums(g3, got3)
    dw2 = _matmul_tn(s_act[None], dz1[None], tm_tn, "dw_pw2")
    (dx0, dh1, d_b1, d_bdw, d_wdw), (from_chips2, from_chips3) = _bwd_conv_glu(
        dc, u_act, a_act, g_act, dz1, wdw, w1s, tm_mid, riders=[_chip_scatter_rider(s2), _chip_scatter_rider(s3)])
    f2 = chip_sums(g2, s2, from_chips2)
    f3 = chip_sums(g3, s3, from_chips3)
    dw1 = _matmul_tn(x0[None], dh1, tm_tn, "dw_pw1")

    def rows4(v):
        return v.reshape(N_CHIPS, -1)

    def rep4(v):
        return jnp.broadcast_to(v.reshape(1, -1), (N_CHIPS, v.size))

    local = {
        'conv_b_pw1': rows4(d_b1),
        'conv_w_dw': rows4(d_wdw[0:CONV_WIDTH].reshape(CONV_WIDTH, N_CHIPS, dq).transpose(1, 0, 2)),
        'conv_b_dw': rows4(d_bdw), 'conv_ln_g': rows4(d_clng), 'conv_ln_b': rows4(d_clnb), 'conv_b_pw2': rows4(d_b2),
        'kv_b_k': rep4(d_bkv[:, 0:KVD]), 'kv_b_v': rep4(d_bkv[:, KVD:2 * KVD]), 'attn_b_q': rep4(d_bq),
        'attn_sinks': rep4(d_sinks), 'attn_b_o': rep4(d_bo),
        'ln_mix_g': rep4(jnp.concatenate([d_mg0, d_mg1])), 'ln_mix_b': rep4(jnp.concatenate([d_mb0, d_mb1])),
        'ln_ffn_g': rep4(jnp.concatenate([d_fg0, d_fg1])), 'ln_ffn_b': rep4(jnp.concatenate([d_fb0, d_fb1])),
    }
    n_small = sum(int(w[n].size) for n in SMALL)
    small_rows = _round_up(n_small, 2 * 8 * 128) // 128
    small_local = jnp.concatenate([local[n] for n in SMALL], axis=1)
    small_local = jnp.pad(small_local, ((0, 0), (0, small_rows * 128 - n_small)))
    g4 = [("conv_w_pw1", halves4(dw1)), ("conv_w_pw2", halves4(dw2)),
          ("small", small_local.reshape(N_CHIPS, 2, small_rows // 2, 128))]
    (got4,) = _run_riders([_pair_exchange_rider(arrays(g4))], "grad_pair_exchange_last")
    s4 = pair_sums(g4, got4)
    (from_chips4,) = _run_riders([_chip_scatter_rider(s4)], "grad_chip_scatter_last")
    f4 = chip_sums(g4, s4, from_chips4)
    (shared_rest,) = _run_riders([_pair_share_rider(f2 + f3 + f4)], "grad_pair_share_last")
    reduced = dict(zip([n for n, _ in g1], shared1))
    reduced.update(zip([n for n, _ in g2 + g3 + g4], shared_rest))
    for n in ('ffn_w_gate', 'ffn_w_up', 'ffn_w_down'):
        reduced[n] = jnp.stack([reduced[n + str(layer)].reshape(w[n].shape[1:]) for layer in range(DEPTH)])

    g_out, delta, new_m, new_v = {}, {}, {}, {}
    for n in BIG:
        shape = w[n].shape
        two_d = (-1, shape[-1])
        g_out[n] = reduced[n].reshape(shape)
        dl, nm, nv = _adamw(w[n].reshape(two_d), g_out[n].reshape(two_d), mom[n].reshape(two_d), var[n].reshape(two_d),
                            "adamw_" + n)
        delta[n], new_m[n], new_v[n] = dl.reshape(shape), nm.reshape(shape), nv.reshape(shape)

    def pack_small(tree):
        return _pad_to(jnp.concatenate([tree[n].reshape(-1) for n in SMALL]), small_rows * 128).reshape(small_rows, 128)

    g_small = reduced['small'].reshape(small_rows, 128)
    dl, nm, nv = _adamw(pack_small(w), g_small, pack_small(mom), pack_small(var), "adamw_small")
    off = 0
    for n in SMALL:
        size, shape = int(w[n].size), w[n].shape
        for tree, flat in ((g_out, g_small), (delta, dl), (new_m, nm), (new_v, nv)):
            tree[n] = flat.reshape(-1)[off:off + size].reshape(shape)
        off += size

    return (loss, dx0.reshape(x.shape), *[g_out[n] for n in WEIGHTS], *[delta[n] for n in WEIGHTS],
            *[new_m[n] for n in WEIGHTS], *[new_v[n] for n in WEIGHTS])
```

```python
import functools
import math

import jax
import jax.numpy as jnp
from jax import lax
from jax.experimental import pallas as pl
from jax.experimental.pallas import tpu as pltpu

F32 = jnp.float32
BF16 = jnp.bfloat16

DEPTH = 2
ALPHA = (2.0 * DEPTH) ** 0.25
LN_EPS = 1e-5
NEG_INF = -1e30
HEAD_DIM = 64
N_KV_HEADS = 2
KVD = N_KV_HEADS * HEAD_DIM
BLOCK = 128
CONV_WIDTH = 31
CONV_HALO = 32
ALIBI_MAX = 8.0
ADAM_LR, ADAM_B1, ADAM_B2, ADAM_EPS, ADAM_WD, ADAM_STEP = 0.001, 0.9, 0.999, 1e-08, 0.01, 10

N_CHIPS = 4
PACK_ROWS = 256
VMEM_LIMIT = 60 * 1024 * 1024
MESH = pl.DeviceIdType.MESH

NT_DIMS = (((1,), (1,)), ((), ()))
TN_DIMS = (((0,), (0,)), ((), ()))

WEIGHTS = ['conv_w_pw1', 'conv_b_pw1', 'conv_w_dw', 'conv_b_dw', 'conv_ln_g', 'conv_ln_b', 'conv_w_pw2', 'conv_b_pw2',
           'kv_w_k', 'kv_b_k', 'kv_w_v', 'kv_b_v', 'attn_w_q', 'attn_b_q', 'attn_sinks', 'attn_w_o', 'attn_b_o',
           'ffn_w_gate', 'ffn_w_up', 'ffn_w_down', 'ln_mix_g', 'ln_mix_b', 'ln_ffn_g', 'ln_ffn_b']
BIG = ['conv_w_pw1', 'conv_w_pw2', 'kv_w_k', 'kv_w_v', 'attn_w_q', 'attn_w_o', 'ffn_w_gate', 'ffn_w_up', 'ffn_w_down']
SMALL_SHARDED = ['conv_b_pw1', 'conv_w_dw', 'conv_b_dw', 'conv_ln_g', 'conv_ln_b', 'conv_b_pw2']
REPLICATED = ['kv_b_k', 'kv_b_v', 'attn_b_q', 'attn_sinks', 'attn_b_o', 'ln_mix_g', 'ln_mix_b', 'ln_ffn_g', 'ln_ffn_b']
SMALL = SMALL_SHARDED + REPLICATED


def _cparams(n_grid=1):
    return pltpu.CompilerParams(dimension_semantics=("arbitrary",) * n_grid, vmem_limit_bytes=VMEM_LIMIT)


def _rows(tm, width):
    return pl.BlockSpec((tm, width), lambda i: (i, 0))


def _const(shape):
    return pl.BlockSpec(shape, lambda *_: (0,) * len(shape), pipeline_mode=pl.Buffered(1))


def _acc_out(shape):
    return pl.BlockSpec(shape, lambda *_: (0,) * len(shape))


def _dot(a, b):
    return jnp.dot(a, b, preferred_element_type=F32)


def _dot_nt(a, b):
    return lax.dot_general(a, b, NT_DIMS, preferred_element_type=F32)


def _dot_tn(a, b):
    return lax.dot_general(a, b, TN_DIMS, preferred_element_type=F32)


def _colsum8(v):
    m, n = v.shape
    return jnp.sum(v.reshape(m // 8, 8, n), axis=0)


def _ln_stats(z):
    mu = jnp.mean(z, axis=-1, keepdims=True)
    zc = z - mu
    var = jnp.mean(zc * zc, axis=-1, keepdims=True)
    rstd = lax.rsqrt(var + LN_EPS)
    return zc * rstd, rstd


def _ln_fwd(z, g, b):
    zhat, _ = _ln_stats(z)
    return zhat * g + b


def _ln_bwd(dy, z, g):
    zhat, rstd = _ln_stats(z)
    dzh = dy * g
    m1 = jnp.mean(dzh, axis=-1, keepdims=True)
    m2 = jnp.mean(dzh * zhat, axis=-1, keepdims=True)
    return rstd * (dzh - m1 - zhat * m2), zhat


def _silu_and_grad(n):
    sg = jax.nn.sigmoid(n)
    return n * sg, sg * (1.0 + n * (1.0 - sg))


def _acc_init(i, *refs):
    @pl.when(i == 0)
    def _():
        for r in refs:
            r[...] = jnp.zeros_like(r)


def _mesh_pos():
    x, y, c = lax.axis_index("x"), lax.axis_index("y"), lax.axis_index("c")
    chips = [(1 - x, y), (x, 1 - y), (1 - x, 1 - y)]
    return x, y, c, chips


HBM_SPEC = pl.BlockSpec(memory_space=pltpu.HBM)


def _remote(src, dst, send_sems, recv_sems, k, to):
    return pltpu.make_async_remote_copy(src_ref=src, dst_ref=dst, send_sem=send_sems.at[k], recv_sem=recv_sems.at[k],
                                        device_id=to, device_id_type=MESH)


class _Rider:
    def __init__(self, operands, out_shapes, sem_shapes, start, finish, mid=None, in_place=False):
        self.operands, self.out_shapes, self.sem_shapes = list(operands), list(out_shapes), list(sem_shapes)
        self.start, self.finish, self.mid = start, finish, mid
        self.in_place = in_place


def _rider_aliases(riders, first_in, first_out):
    aliases, k_in, k_out = {}, first_in, first_out
    for r in riders:
        if r.in_place:
            aliases.update({k_in + k: k_out + k for k in range(len(r.operands))})
        k_in += len(r.operands)
        k_out += len(r.out_shapes)
    return aliases


def _split(refs, counts):
    parts, k = [], 0
    for n in counts:
        parts.append(refs[k:k + n])
        k += n
    return parts


def _rider_refs(riders, ins, outs, sems):
    return list(zip(riders, _split(ins, [len(r.operands) for r in riders]),
                    _split(outs, [len(r.out_shapes) for r in riders]),
                    _split(sems, [len(r.sem_shapes) for r in riders])))


def _tc_call(body, *, name, nt, in_specs, out_specs, out_shape, operands, scratch_shapes=(), riders=(), mid_frac=0.75):
    n_in, n_out, n_scr = len(in_specs), len(out_specs), len(scratch_shapes)
    r_ops = [o for r in riders for o in r.operands]
    r_outs = [o for r in riders for o in r.out_shapes]
    r_sems = [s for r in riders for s in r.sem_shapes]
    mid_step = min(max(int(nt * mid_frac), 0), nt - 1)

    def full(*refs):
        ins, r_in, outs, r_out, scr, r_sem = _split(refs, [n_in, len(r_ops), n_out, len(r_outs), n_scr, len(r_sems)])
        parts = _rider_refs(riders, r_in, r_out, r_sem)
        step = pl.program_id(0)

        @pl.when(step == 0)
        def _():
            for r, a, b, s in parts:
                r.start(a, b, s)

        body(*ins, *outs, *scr)

        @pl.when(step == mid_step)
        def _():
            for r, a, b, s in parts:
                if r.mid is not None:
                    r.mid(a, b, s)

        @pl.when(step == nt - 1)
        def _():
            for r, a, b, s in parts:
                r.finish(a, b, s)

    res = pl.pallas_call(
        full if riders else body, name=name, grid=(nt,), in_specs=list(in_specs) + [HBM_SPEC] * len(r_ops),
        out_specs=list(out_specs) + [HBM_SPEC] * len(r_outs), out_shape=list(out_shape) + r_outs,
        scratch_shapes=list(scratch_shapes) + r_sems, input_output_aliases=_rider_aliases(riders, n_in, n_out),
        compiler_params=_cparams(),
    )(*operands, *r_ops)
    return res[:n_out], _split(res[n_out:], [len(r.out_shapes) for r in riders])


def _run_riders(riders, name):
    r_ops = [o for r in riders for o in r.operands]
    r_outs = [o for r in riders for o in r.out_shapes]
    r_sems = [s for r in riders for s in r.sem_shapes]

    def body(*refs):
        r_in, r_out, r_sem = _split(refs, [len(r_ops), len(r_outs), len(r_sems)])
        parts = _rider_refs(riders, r_in, r_out, r_sem)
        for r, a, b, s in parts:
            r.start(a, b, s)
        for r, a, b, s in parts:
            if r.mid is not None:
                r.mid(a, b, s)
        for r, a, b, s in parts:
            r.finish(a, b, s)

    res = pl.pallas_call(body, name=name, out_shape=tuple(r_outs), in_specs=[HBM_SPEC] * len(r_ops),
                         out_specs=(HBM_SPEC,) * len(r_outs), scratch_shapes=r_sems,
                         input_output_aliases=_rider_aliases(riders, 0, 0))(*r_ops)
    return _split(list(res), [len(r.out_shapes) for r in riders])


def _all_gather_rider(bufs, small=None):
    n = len(bufs)
    n_small = 0 if small is None else 1

    def copies(outs, sems):
        send_sems, recv_sems = sems
        x, y, c, chips = _mesh_pos()
        me = 2 * x + y
        here, sibling = (x, y, c), (x, y, 1 - c)
        rows = [2 * cx + cy for cx, cy in chips]

        def big(p, k, chip_row, half, to):
            piece = outs[p].at[chip_row, half]
            return _remote(piece, piece, send_sems, recv_sems, 6 * p + k, to)

        first = [big(p, j, me, c, (cx, cy, c)) for p in range(n) for j, (cx, cy) in enumerate(chips)]
        landed = [big(p, j, rows[j], c, here) for p in range(n) for j in range(3)]
        passed = [big(p, 3 + j, rows[j], c, sibling) for p in range(n) for j in range(3)]
        arrivals = [big(p, 3 + j, rows[j], 1 - c, here) for p in range(n) for j in range(3)]
        if n_small:
            first = [_remote(outs[n].at[me], outs[n].at[me], send_sems, recv_sems, 6 * n + j, (cx, cy, c))
                     for j, (cx, cy) in enumerate(chips)] + first
            arrivals += [_remote(outs[n].at[rows[j]], outs[n].at[rows[j]], send_sems, recv_sems, 6 * n + j, here)
                         for j in range(3)]
        return first, landed, passed, arrivals

    def start(ins, outs, sems):
        for cp in copies(outs, sems)[0]:
            cp.start()

    def mid(ins, outs, sems):
        _, landed, passed, _ = copies(outs, sems)
        for got, fwd in zip(landed, passed):
            got.wait_recv()
            fwd.start()

    def finish(ins, outs, sems):
        first, _, passed, arrivals = copies(outs, sems)
        for cp in arrivals:
            cp.wait_recv()
        for cp in first + passed:
            cp.wait_send()

    operands = list(bufs) + ([small] if n_small else [])
    n_sem = 6 * n + 3 * n_small
    return _Rider(operands, [jax.ShapeDtypeStruct(o.shape, o.dtype) for o in operands],
                  [pltpu.SemaphoreType.DMA((n_sem,)), pltpu.SemaphoreType.DMA((n_sem,))], start, finish, mid,
                  in_place=True)


def _pair_exchange_rider(plist):
    n = len(plist)

    def copies(ins, outs, sems):
        x, y, c, _ = _mesh_pos()
        return [_remote(ins[k].at[:, 1 - c], outs[k], sems[0], sems[1], k, (x, y, 1 - c)) for k in range(n)]

    def start(ins, outs, sems):
        for cp in copies(ins, outs, sems):
            cp.start()

    def finish(ins, outs, sems):
        for cp in copies(ins, outs, sems):
            cp.wait()

    return _Rider(plist, [jax.ShapeDtypeStruct((p.shape[0],) + p.shape[2:], p.dtype) for p in plist],
                  [pltpu.SemaphoreType.DMA((n,)), pltpu.SemaphoreType.DMA((n,))], start, finish)


def _pair_sum(p, got, c, name):
    n, _, r, l = p.shape
    br = _row_block(r, PACK_ROWS)

    def body(c_ref, p_ref, got_ref, out_ref):
        out_ref[...] = p_ref[...] + got_ref[...]

    return pl.pallas_call(
        body, name=name, out_shape=jax.ShapeDtypeStruct((n, r, l), F32),
        grid_spec=pltpu.PrefetchScalarGridSpec(
            num_scalar_prefetch=1, grid=(n, r // br),
            in_specs=[pl.BlockSpec((None, None, br, l), lambda j, i, c_ref: (j, c_ref[0], i, 0)),
                      pl.BlockSpec((None, br, l), lambda j, i, c_ref: (j, i, 0))],
            out_specs=pl.BlockSpec((None, br, l), lambda j, i, c_ref: (j, i, 0))),
        compiler_params=_cparams(2),
    )(c, p, got)


def _chip_scatter_rider(slist):
    n = len(slist)

    def copies(ins, outs, sems):
        send_sems, recv_sems = sems
        x, y, c, chips = _mesh_pos()
        sends = [_remote(ins[k].at[2 * cx + cy], outs[k].at[j], send_sems, recv_sems, 3 * k + j, (cx, cy, c))
                 for k in range(n) for j, (cx, cy) in enumerate(chips)]
        arrivals = [_remote(ins[k].at[0], outs[k].at[j], send_sems, recv_sems, 3 * k + j, (x, y, c))
                    for k in range(n) for j in range(3)]
        return sends, arrivals

    def start(ins, outs, sems):
        for cp in copies(ins, outs, sems)[0]:
            cp.start()

    def finish(ins, outs, sems):
        sends, arrivals = copies(ins, outs, sems)
        for cp in arrivals:
            cp.wait_recv()
        for cp in sends:
            cp.wait_send()

    return _Rider(slist, [jax.ShapeDtypeStruct((3,) + s.shape[1:], s.dtype) for s in slist],
                  [pltpu.SemaphoreType.DMA((3 * n,)), pltpu.SemaphoreType.DMA((3 * n,))], start, finish)


def _chip_sum(s, got, pos, name):
    _, r, l = s.shape
    br = _row_block(r, PACK_ROWS)

    def body(pos_ref, s_ref, got_ref, out_ref):
        me = pos_ref[0]
        total = None
        for chip in range(N_CHIPS):
            flip = jnp.bitwise_xor(me, chip)
            term = jnp.where(flip == 0, s_ref[...],
                             jnp.where(flip == 2, got_ref[0], jnp.where(flip == 1, got_ref[1], got_ref[2])))
            total = term if total is None else total + term
        out_ref[...] = total

    return pl.pallas_call(
        body, name=name, out_shape=jax.ShapeDtypeStruct((2, r, l), F32),
        grid_spec=pltpu.PrefetchScalarGridSpec(
            num_scalar_prefetch=1, grid=(r // br,),
            in_specs=[pl.BlockSpec((None, br, l), lambda i, pos_ref: (pos_ref[0], i, 0)),
                      pl.BlockSpec((3, br, l), lambda i, pos_ref: (0, i, 0))],
            out_specs=pl.BlockSpec((None, br, l), lambda i, pos_ref: (pos_ref[1], i, 0))),
        compiler_params=_cparams(1),
    )(pos, s, got)


def _pair_share_rider(flist):
    n = len(flist)

    def copies(outs, sems):
        x, y, c, _ = _mesh_pos()
        sends = [_remote(outs[k].at[c], outs[k].at[c], sems[0], sems[1], k, (x, y, 1 - c)) for k in range(n)]
        arrivals = [_remote(outs[k].at[1 - c], outs[k].at[1 - c], sems[0], sems[1], k, (x, y, c)) for k in range(n)]
        return sends, arrivals

    def start(ins, outs, sems):
        for cp in copies(outs, sems)[0]:
            cp.start()

    def finish(ins, outs, sems):
        sends, arrivals = copies(outs, sems)
        for cp in arrivals:
            cp.wait_recv()
        for cp in sends:
            cp.wait_send()

    return _Rider(flist, [jax.ShapeDtypeStruct(f.shape, f.dtype) for f in flist],
                  [pltpu.SemaphoreType.DMA((n,)), pltpu.SemaphoreType.DMA((n,))], start, finish, in_place=True)


def _fwd_pw1_glu(x, w1s, b1, tm):
    t, d = x.shape
    dh = d // 2

    def body(x_ref, w_ref, b_ref, a_ref, g_ref, u_ref):
        xb = x_ref[...].astype(BF16)
        for hh in range(2):
            cs = slice(hh * dh, (hh + 1) * dh)
            a = _dot(xb, w_ref[hh]) + b_ref[:, hh * dh:(hh + 1) * dh]
            g = _dot(xb, w_ref[2 + hh]) + b_ref[:, d + hh * dh:d + (hh + 1) * dh]
            a_ref[:, cs] = a.astype(BF16)
            g_ref[:, cs] = g.astype(BF16)
            u_ref[:, cs] = a * jax.nn.sigmoid(g)

    return pl.pallas_call(
        body, name="fwd_pw1_glu", grid=(t // tm,),
        in_specs=[_rows(tm, d), _const((4, d, dh)), _const((1, 2 * d))],
        out_specs=[_rows(tm, d)] * 3,
        out_shape=[jax.ShapeDtypeStruct((t, d), BF16), jax.ShapeDtypeStruct((t, d), BF16),
                   jax.ShapeDtypeStruct((t, d), F32)],
        compiler_params=_cparams(),
    )(x, w1s, b1)


def _fill_shifted(sh_ref, ext_ref):
    n = sh_ref.shape[1]
    for s in range(8):
        sh_ref[s] = ext_ref[pl.ds(s, n), :]


CONV_CHUNK = 64
LANES = 256


def _tap_sum(w_ref, sh, base, d, tap_row, out_ref, bias_ref=None):
    groups = CONV_CHUNK // 8
    for lg in range(d // LANES):
        ls = slice(lg * LANES, (lg + 1) * LANES)
        acc = jnp.zeros((groups, 8, LANES), F32)
        for k in range(CONV_WIDTH):
            e = tap_row(k)
            x = sh[e % 8, pl.ds(base + (e // 8) * 8, CONV_CHUNK), ls]
            acc = acc + w_ref[k, :, ls] * x.reshape(groups, 8, LANES)
        acc = acc.reshape(CONV_CHUNK, LANES)
        out_ref[pl.ds(base, CONV_CHUNK), ls] = acc if bias_ref is None else acc + bias_ref[:, ls]


def _fwd_conv_tail(u, x0, wdw, bdw, lng, lnb, w2, b2, mixg, mixb, tm, riders=()):
    t, d = u.shape
    hb = tm // CONV_HALO

    def body(u_ref, uh_ref, x_ref, w_ref, bdw_ref, lng_ref, lnb_ref, w2_ref, b2_ref, mg_ref, mb_ref,
             c_ref, z_ref, y_ref, ext, sh):
        i = pl.program_id(0)
        ext[0:CONV_HALO] = jnp.where(i == 0, 0.0, uh_ref[...])
        ext[CONV_HALO:CONV_HALO + tm] = u_ref[...]
        ext[CONV_HALO + tm:CONV_HALO + tm + 8] = jnp.zeros((8, d), F32)
        _fill_shifted(sh, ext)

        def chunk(r, carry):
            base = pl.multiple_of(r * CONV_CHUNK, CONV_CHUNK)
            _tap_sum(w_ref, sh, base, d, lambda k: k + CONV_HALO - (CONV_WIDTH - 1), c_ref, bdw_ref)
            return carry

        lax.fori_loop(0, tm // CONV_CHUNK, chunk, 0)
        n = _ln_fwd(c_ref[...], lng_ref[...], lnb_ref[...])
        s = n * jax.nn.sigmoid(n)
        m = _dot(s.astype(BF16), w2_ref[...]) + b2_ref[...]
        z = ALPHA * x_ref[...] + m
        z_ref[...] = z
        y_ref[...] = _ln_fwd(z, mg_ref[...], mb_ref[...])

    vec = _const((1, d))
    return _tc_call(
        body, name="fwd_conv_tail", nt=t // tm,
        in_specs=[_rows(tm, d), pl.BlockSpec((CONV_HALO, d), lambda i: (jnp.maximum(i * hb - 1, 0), 0)), _rows(tm, d),
                  _const((CONV_HALO, 8, d)), vec, vec, vec, _const((d, d)), vec, vec, vec],
        out_specs=[_rows(tm, d)] * 3,
        out_shape=[jax.ShapeDtypeStruct((t, d), F32)] * 3,
        scratch_shapes=[pltpu.VMEM((tm + CONV_HALO + 8, d), F32), pltpu.VMEM((8, tm + CONV_HALO, d), F32)],
        operands=(u, u, x0, wdw, bdw, lng, lnb, w2, b2, mixg, mixb), riders=riders)


def _fwd_ffn(x, wg, wu, wd, layer, lng, lnb, tm, target=None):
    t, d = x.shape
    fs = wg.shape[-1]
    nt = t // tm
    with_loss = target is not None

    def hidden(x_ref, wg_ref, wu_ref, wd_ref, act_ref, bm_ref, hm_ref):
        xv = x_ref[...]
        xb = xv.astype(BF16)
        f = jnp.zeros((tm, d), F32)
        for j in range(N_CHIPS):
            gj = _dot(xb, wg_ref[j])
            uj = _dot(xb, wu_ref[j])
            act, dact = _silu_and_grad(gj)
            act_ref[j] = act.astype(BF16)
            bm_ref[j] = (uj * dact).astype(BF16)
            hmb = (act * uj).astype(BF16)
            hm_ref[j] = hmb
            f = f + _dot(hmb, wd_ref[j])
        return ALPHA * xv + f

    def body(x_ref, wg_ref, wu_ref, wd_ref, g_ref, b_ref, act_ref, bm_ref, hm_ref, z_ref, y_ref):
        z = hidden(x_ref, wg_ref, wu_ref, wd_ref, act_ref, bm_ref, hm_ref)
        z_ref[...] = z
        y_ref[...] = _ln_fwd(z, g_ref[...], b_ref[...])

    def body_loss(x_ref, wg_ref, wu_ref, wd_ref, g_ref, b_ref, t_ref, act_ref, bm_ref, hm_ref, dz_ref,
                  dlg_ref, dlb_ref, loss_ref, acc_g, acc_b, acc_l):
        i = pl.program_id(0)
        _acc_init(i, acc_g, acc_b, acc_l)
        z = hidden(x_ref, wg_ref, wu_ref, wd_ref, act_ref, bm_ref, hm_ref)
        zhat, rstd = _ln_stats(z)
        gain = g_ref[...]
        err = zhat * gain + b_ref[...] - t_ref[...]
        acc_l[...] += _colsum8(err * err)
        dy = err * (1.0 / d)
        acc_g[...] += _colsum8(dy * zhat)
        acc_b[...] += _colsum8(dy)
        dzh = dy * gain
        m1 = jnp.mean(dzh, axis=-1, keepdims=True)
        m2 = jnp.mean(dzh * zhat, axis=-1, keepdims=True)
        dz_ref[...] = rstd * (dzh - m1 - zhat * m2)

        @pl.when(i == nt - 1)
        def _():
            dlg_ref[...] = jnp.sum(acc_g[...], axis=0, keepdims=True)
            dlb_ref[...] = jnp.sum(acc_b[...], axis=0, keepdims=True)
            loss_ref[...] = jnp.sum(acc_l[...], keepdims=True) * (0.5 / d)

    wcol = pl.BlockSpec((N_CHIPS, None, d, fs), lambda i: (0, layer, 0, 0), pipeline_mode=pl.Buffered(1))
    wrow = pl.BlockSpec((N_CHIPS, None, fs, d), lambda i: (0, layer, 0, 0), pipeline_mode=pl.Buffered(1))
    hid = pl.BlockSpec((N_CHIPS, tm, fs), lambda i: (0, i, 0))
    in_specs = [_rows(tm, d), wcol, wcol, wrow, _const((1, d)), _const((1, d))]
    hid_shapes = [jax.ShapeDtypeStruct((N_CHIPS, t, fs), BF16)] * 3
    if not with_loss:
        return pl.pallas_call(
            body, name=f"fwd_ffn{layer}", grid=(nt,), in_specs=in_specs,
            out_specs=[hid, hid, hid, _rows(tm, d), _rows(tm, d)],
            out_shape=hid_shapes + [jax.ShapeDtypeStruct((t, d), F32)] * 2, compiler_params=_cparams(),
        )(x, wg, wu, wd, lng, lnb)
    return pl.pallas_call(
        body_loss, name=f"fwd_ffn{layer}_loss", grid=(nt,), in_specs=in_specs + [_rows(tm, d)],
        out_specs=[hid, hid, hid, _rows(tm, d), _acc_out((1, d)), _acc_out((1, d)), _acc_out((1, 1))],
        out_shape=hid_shapes + [jax.ShapeDtypeStruct((t, d), F32)] + [jax.ShapeDtypeStruct((1, d), F32)] * 2
        + [jax.ShapeDtypeStruct((1, 1), F32)],
        scratch_shapes=[pltpu.VMEM((8, d), F32)] * 3, compiler_params=_cparams(),
    )(x, wg, wu, wd, lng, lnb, target)


def _attn_band():
    kt = lax.broadcasted_iota(jnp.int32, (BLOCK, BLOCK), 0)
    qi = lax.broadcasted_iota(jnp.int32, (BLOCK, BLOCK), 1)
    current = kt <= qi
    delta = qi - kt + jnp.where(current, 0, BLOCK)
    return current, delta.astype(F32)


def _fold(full, current):
    return jnp.where(current, full[BLOCK:2 * BLOCK], full[0:BLOCK])


def _unfold(folded, current):
    zero = jnp.zeros_like(folded)
    return jnp.concatenate([jnp.where(current, zero, folded), jnp.where(current, folded, zero)], axis=0)


def _slope(h, nq):
    return 2.0 ** (-ALIBI_MAX * (h + 1) / nq)


def _softmax_with_sink(s_full, slope, band, has_previous, sink):
    current, delta = band
    s = _fold(s_full, current) * (1.0 / math.sqrt(HEAD_DIM)) - slope * delta
    s = jnp.where(jnp.logical_or(current, has_previous), s, NEG_INF)
    m = jnp.maximum(jnp.max(s, axis=0, keepdims=True), sink)
    p = jnp.exp(s - m)
    e_sink = jnp.exp(sink - m)
    inv = 1.0 / (jnp.sum(p, axis=0, keepdims=True) + e_sink)
    return p * inv, e_sink * inv


def _heads_on_lanes(ref, b, g, group):
    first = g * group
    return jnp.concatenate([ref[b, (first + hh) * HEAD_DIM:(first + hh + 1) * HEAD_DIM, :] for hh in range(group)],
                           axis=1)


def _fill_kv(kv_scr, halo, tile, tm):
    for j in range(2 * N_KV_HEADS):
        kv_scr[j, 0:BLOCK] = halo[:, j * HEAD_DIM:(j + 1) * HEAD_DIM]
        kv_scr[j, BLOCK:BLOCK + tm] = tile[:, j * HEAD_DIM:(j + 1) * HEAD_DIM]


def _cols(d, tm):
    return pl.BlockSpec((d, tm), lambda i: (0, i))


def _fwd_attn(x, wqt, bqt, wkv, bkv, sinks, wo, bo, mixg, mixb, tm):
    t, d = x.shape
    nq = d // HEAD_DIM
    group = nq // N_KV_HEADS
    nb = tm // BLOCK

    def body(sink_ref, x_ref, xh_ref, wqt_ref, bqt_ref, wkv_ref, bkv_ref, wo_ref, bo_ref, mg_ref, mb_ref,
             qt_ref, kv_ref, ot_ref, z_ref, y_ref, kv_scr, qt_scr, ot_scr):
        i = pl.program_id(0)
        xv = x_ref[...]
        xb = xv.astype(BF16)
        qt = (_dot_nt(wqt_ref[...], xb) + bqt_ref[...]).astype(BF16)
        qt_ref[...] = qt
        for b in range(nb):
            qt_scr[b] = qt[:, b * BLOCK:(b + 1) * BLOCK]
        kvb = (_dot(xb, wkv_ref[...]) + bkv_ref[...]).astype(BF16)
        kv_ref[...] = kvb
        _fill_kv(kv_scr, (_dot(xh_ref[...].astype(BF16), wkv_ref[...]) + bkv_ref[...]).astype(BF16), kvb, tm)
        band = _attn_band()

        def block(b, carry):
            r0 = pl.multiple_of(b * BLOCK, BLOCK)
            has_previous = jnp.logical_or(i > 0, b > 0)
            for g in range(N_KV_HEADS):
                kk = kv_scr[g, pl.ds(r0, 2 * BLOCK), :]
                vv = kv_scr[N_KV_HEADS + g, pl.ds(r0, 2 * BLOCK), :]
                s_all = _dot(kk, _heads_on_lanes(qt_scr, b, g, group))
                probs = []
                for hh in range(group):
                    h = g * group + hh
                    p, _ = _softmax_with_sink(s_all[:, hh * BLOCK:(hh + 1) * BLOCK], _slope(h, nq), band,
                                              has_previous, sink_ref[h])
                    probs.append(_unfold(p.astype(BF16), band[0]))
                o_all = _dot_tn(vv, jnp.concatenate(probs, axis=1))
                for hh in range(group):
                    h = g * group + hh
                    ot_scr[b, h * HEAD_DIM:(h + 1) * HEAD_DIM, :] = o_all[:, hh * BLOCK:(hh + 1) * BLOCK].astype(BF16)
            return carry

        lax.fori_loop(0, nb, block, 0)
        ot = jnp.concatenate([ot_scr[b] for b in range(nb)], axis=1)
        ot_ref[...] = ot
        z = ALPHA * xv + _dot_tn(ot, wo_ref[...]) + bo_ref[...]
        z_ref[...] = z
        y_ref[...] = _ln_fwd(z, mg_ref[...], mb_ref[...])

    hb = tm // BLOCK
    vec = _const((1, d))
    return pl.pallas_call(
        body, name="fwd_attn", grid=(t // tm,),
        in_specs=[pl.BlockSpec(memory_space=pltpu.SMEM),
                  _rows(tm, d), pl.BlockSpec((BLOCK, d), lambda i: (jnp.maximum(i * hb - 1, 0), 0)),
                  _const((d, d)), _const((d, 1)), _const((d, 2 * KVD)), _const((1, 2 * KVD)), _const((d, d)), vec, vec,
                  vec],
        out_specs=[_cols(d, tm), _rows(tm, 2 * KVD), _cols(d, tm), _rows(tm, d), _rows(tm, d)],
        out_shape=[jax.ShapeDtypeStruct((d, t), BF16), jax.ShapeDtypeStruct((t, 2 * KVD), BF16),
                   jax.ShapeDtypeStruct((d, t), BF16), jax.ShapeDtypeStruct((t, d), F32),
                   jax.ShapeDtypeStruct((t, d), F32)],
        scratch_shapes=[pltpu.VMEM((2 * N_KV_HEADS, tm + BLOCK, HEAD_DIM), BF16), pltpu.VMEM((nb, d, BLOCK), BF16),
                        pltpu.VMEM((nb, d, BLOCK), BF16)],
        compiler_params=_cparams(),
    )(sinks, x, x, wqt, bqt, wkv, bkv, wo, bo, mixg, mixb)


def _write_sums(i, nt, pairs):
    @pl.when(i == nt - 1)
    def _():
        for out_ref, acc in pairs:
            out_ref[...] = jnp.sum(acc[...], axis=0, keepdims=True)


def _bwd_ffn_dx(dz, act, bm, wg, wu, wd, layer, z_in, g_in, tm, riders=()):
    t, d = dz.shape
    fs = wg.shape[-1]
    nt = t // tm

    def body(dz_ref, act_ref, bm_ref, wg_ref, wu_ref, wd_ref, zin_ref, gin_ref,
             dgg_ref, duu_ref, dzin_ref, dg_ref, db_ref, dsum_ref, acc_g, acc_b, acc_s):
        i = pl.program_id(0)
        _acc_init(i, acc_g, acc_b, acc_s)
        dzv = dz_ref[...]
        dzb = dzv.astype(BF16)
        dx = ALPHA * dzv
        for j in range(N_CHIPS):
            dh = _dot_nt(dzb, wd_ref[j])
            dgb = (dh * bm_ref[j].astype(F32)).astype(BF16)
            dub = (dh * act_ref[j].astype(F32)).astype(BF16)
            dgg_ref[j] = dgb
            duu_ref[j] = dub
            dx = dx + _dot_nt(dgb, wg_ref[j]) + _dot_nt(dub, wu_ref[j])
        dz_in, zhat = _ln_bwd(dx, zin_ref[...], gin_ref[...])
        acc_g[...] += _colsum8(dx * zhat)
        acc_b[...] += _colsum8(dx)
        acc_s[...] += _colsum8(dz_in)
        dzin_ref[...] = dz_in
        _write_sums(i, nt, [(dg_ref, acc_g), (db_ref, acc_b), (dsum_ref, acc_s)])

    wcol = pl.BlockSpec((N_CHIPS, None, d, fs), lambda i: (0, layer, 0, 0), pipeline_mode=pl.Buffered(1))
    wrow = pl.BlockSpec((N_CHIPS, None, fs, d), lambda i: (0, layer, 0, 0), pipeline_mode=pl.Buffered(1))
    hid = pl.BlockSpec((N_CHIPS, tm, fs), lambda i: (0, i, 0))
    return _tc_call(
        body, name=f"bwd_ffn_dx{layer}", nt=nt,
        in_specs=[_rows(tm, d), hid, hid, wcol, wcol, wrow, _rows(tm, d), _const((1, d))],
        out_specs=[hid, hid, _rows(tm, d)] + [_acc_out((1, d))] * 3,
        out_shape=[jax.ShapeDtypeStruct((N_CHIPS, t, fs), BF16)] * 2 + [jax.ShapeDtypeStruct((t, d), F32)]
        + [jax.ShapeDtypeStruct((1, d), F32)] * 3,
        scratch_shapes=[pltpu.VMEM((8, d), F32)] * 3,
        operands=(dz, act, bm, wg, wu, wd, z_in, g_in), riders=riders)


def _matmul_tn(a, b, tt, name):
    ja, t, ka = a.shape
    jb, _, nb = b.shape
    nj = max(ja, jb)

    def body(a_ref, b_ref, o_ref):
        @pl.when(pl.program_id(0) == 0)
        def _():
            o_ref[...] = jnp.zeros_like(o_ref)

        a0 = a_ref[0].astype(BF16) if ja == 1 else None
        b0 = b_ref[0].astype(BF16) if jb == 1 else None
        for j in range(nj):
            aj = a0 if ja == 1 else a_ref[j].astype(BF16)
            bj = b0 if jb == 1 else b_ref[j].astype(BF16)
            o_ref[j] += _dot_tn(aj, bj)

    return pl.pallas_call(
        body, name=name, grid=(t // tt,),
        in_specs=[pl.BlockSpec((ja, tt, ka), lambda i: (0, i, 0)), pl.BlockSpec((jb, tt, nb), lambda i: (0, i, 0))],
        out_specs=pl.BlockSpec((nj, ka, nb), lambda i: (0, 0, 0)), out_shape=jax.ShapeDtypeStruct((nj, ka, nb), F32),
        compiler_params=_cparams(1),
    )(a, b)


def _matmul_nn(at, b, tt, name):
    ka, t = at.shape
    nb = b.shape[1]

    def body(a_ref, b_ref, o_ref):
        @pl.when(pl.program_id(0) == 0)
        def _():
            o_ref[...] = jnp.zeros_like(o_ref)

        o_ref[...] += _dot(a_ref[...].astype(BF16), b_ref[...].astype(BF16))

    return pl.pallas_call(
        body, name=name, grid=(t // tt,),
        in_specs=[pl.BlockSpec((ka, tt), lambda i: (0, i)), pl.BlockSpec((tt, nb), lambda i: (i, 0))],
        out_specs=pl.BlockSpec((ka, nb), lambda i: (0, 0)), out_shape=jax.ShapeDtypeStruct((ka, nb), F32),
        compiler_params=_cparams(1),
    )(at, b)


def _bwd_attn(dz_all, qt, kv, sinks, wo, wqt, wkv, z_in, g_in, tm, riders=()):
    t, d = dz_all.shape
    nq = d // HEAD_DIM
    group = nq // N_KV_HEADS
    nb = tm // BLOCK
    nt = t // tm
    hb = tm // BLOCK
    n_kv = 2 * N_KV_HEADS

    def body(sink_ref, dz_ref, qt_ref, kv_ref, kvh_ref, wo_ref, wqt_ref, wkv_ref, zin_ref, gin_ref,
             dqt_ref, dkv_ref, dx_ref, dbq_ref, dbkv_ref, dsink_ref, ding_ref, dinb_ref,
             kv_scr, dkv_scr, qt_scr, dot_scr, dqt_scr, carry, acc_q, acc_kv, acc_s, acc_ig, acc_ib):
        i = pl.program_id(0)
        ti = nt - 1 - i
        _acc_init(i, carry, acc_q, acc_kv, acc_s, acc_ig, acc_ib)
        dz = dz_ref[...]
        do_t = _dot_nt(wo_ref[...], dz.astype(BF16)).astype(BF16)
        for b in range(nb):
            dot_scr[b] = do_t[:, b * BLOCK:(b + 1) * BLOCK]
            qt_scr[b] = qt_ref[:, b * BLOCK:(b + 1) * BLOCK]
        _fill_kv(kv_scr, kvh_ref[...], kv_ref[...], tm)
        dkv_scr[:, 0:tm] = jnp.zeros((n_kv, tm, HEAD_DIM), F32)
        dkv_scr[:, tm:tm + BLOCK] = carry[...]
        band = _attn_band()

        def block(b, c):
            r0 = pl.multiple_of(b * BLOCK, BLOCK)
            has_previous = jnp.logical_or(ti > 0, b > 0)
            for g in range(N_KV_HEADS):
                kk = kv_scr[g, pl.ds(r0, 2 * BLOCK), :]
                vv = kv_scr[N_KV_HEADS + g, pl.ds(r0, 2 * BLOCK), :]
                q_all = _heads_on_lanes(qt_scr, b, g, group)
                do_all = _heads_on_lanes(dot_scr, b, g, group)
                s_all = _dot(kk, q_all)
                dp_all = _dot(vv, do_all)
                probs, dscores = [], []
                for hh in range(group):
                    h = g * group + hh
                    cols = slice(hh * BLOCK, (hh + 1) * BLOCK)
                    p, p_sink = _softmax_with_sink(s_all[:, cols], _slope(h, nq), band, has_previous, sink_ref[h])
                    dp = _fold(dp_all[:, cols], band[0])
                    rs = jnp.sum(p * dp, axis=0, keepdims=True)
                    acc_s[h:h + 1, :] += -(p_sink * rs)
                    ds = p * (dp - rs) * (1.0 / math.sqrt(HEAD_DIM))
                    probs.append(_unfold(p.astype(BF16), band[0]))
                    dscores.append(_unfold(ds.astype(BF16), band[0]))
                p_all = jnp.concatenate(probs, axis=1)
                ds_all = jnp.concatenate(dscores, axis=1)
                dq_all = _dot_tn(kk, ds_all)
                for hh in range(group):
                    h = g * group + hh
                    dqt_scr[b, h * HEAD_DIM:(h + 1) * HEAD_DIM, :] = dq_all[:, hh * BLOCK:(hh + 1) * BLOCK]
                dkv_scr[g, pl.ds(r0, 2 * BLOCK), :] += _dot_nt(ds_all, q_all)
                dkv_scr[N_KV_HEADS + g, pl.ds(r0, 2 * BLOCK), :] += _dot_nt(p_all, do_all)
            return c

        lax.fori_loop(0, nb, block, 0)
        carry[...] = dkv_scr[:, 0:BLOCK]
        dkv = jnp.concatenate([dkv_scr[j, BLOCK:BLOCK + tm] for j in range(n_kv)], axis=1)
        acc_kv[...] += _colsum8(dkv)
        dkvb = dkv.astype(BF16)
        dkv_ref[...] = dkvb
        dqt = jnp.concatenate([dqt_scr[b] for b in range(nb)], axis=1)
        for b in range(nb):
            acc_q[...] += dqt_scr[b]
        dqtb = dqt.astype(BF16)
        dqt_ref[...] = dqtb
        dx = ALPHA * dz + _dot_tn(dqtb, wqt_ref[...]) + _dot_nt(dkvb, wkv_ref[...])
        dz_in, zhat_in = _ln_bwd(dx, zin_ref[...], gin_ref[...])
        acc_ig[...] += _colsum8(dx * zhat_in)
        acc_ib[...] += _colsum8(dx)
        dx_ref[...] = dz_in
        _write_sums(i, nt, [(dbkv_ref, acc_kv), (ding_ref, acc_ig), (dinb_ref, acc_ib)])

        @pl.when(i == nt - 1)
        def _():
            dbq_ref[...] = jnp.sum(acc_q[...], axis=1, keepdims=True)
            dsink_ref[...] = jnp.sum(acc_s[...], axis=1, keepdims=True)

    rev = lambda w: pl.BlockSpec((tm, w), lambda i: (nt - 1 - i, 0))
    rev_cols = pl.BlockSpec((d, tm), lambda i: (0, nt - 1 - i))
    vec = _const((1, d))
    return _tc_call(
        body, name="bwd_attn", nt=nt,
        in_specs=[pl.BlockSpec(memory_space=pltpu.SMEM), rev(d), rev_cols, rev(2 * KVD),
                  pl.BlockSpec((BLOCK, 2 * KVD), lambda i: (jnp.maximum((nt - 1 - i) * hb - 1, 0), 0)),
                  _const((d, d)), _const((d, d)), _const((d, 2 * KVD)), rev(d), vec],
        out_specs=[rev_cols, rev(2 * KVD), rev(d), _acc_out((d, 1)), _acc_out((1, 2 * KVD)), _acc_out((nq, 1)),
                   _acc_out((1, d)), _acc_out((1, d))],
        out_shape=[jax.ShapeDtypeStruct((d, t), BF16), jax.ShapeDtypeStruct((t, 2 * KVD), BF16),
                   jax.ShapeDtypeStruct((t, d), F32), jax.ShapeDtypeStruct((d, 1), F32),
                   jax.ShapeDtypeStruct((1, 2 * KVD), F32), jax.ShapeDtypeStruct((nq, 1), F32)]
        + [jax.ShapeDtypeStruct((1, d), F32)] * 2,
        scratch_shapes=[pltpu.VMEM((n_kv, tm + BLOCK, HEAD_DIM), BF16), pltpu.VMEM((n_kv, tm + BLOCK, HEAD_DIM), F32),
                        pltpu.VMEM((nb, d, BLOCK), BF16), pltpu.VMEM((nb, d, BLOCK), BF16),
                        pltpu.VMEM((nb, d, BLOCK), F32), pltpu.VMEM((n_kv, BLOCK, HEAD_DIM), F32),
                        pltpu.VMEM((d, BLOCK), F32), pltpu.VMEM((8, 2 * KVD), F32), pltpu.VMEM((nq, BLOCK), F32),
                        pltpu.VMEM((8, d), F32), pltpu.VMEM((8, d), F32)],
        operands=(sinks, dz_all, qt, kv, kv, wo, wqt, wkv, z_in, g_in), riders=riders)


def _bwd_conv_head(dz, c, w2, lng, lnb, tm, riders=()):
    t, d = dz.shape
    nt = t // tm

    def body(dz_ref, c_ref, w2_ref, lg_ref, lb_ref, s_ref, dc_ref, dlg_ref, dlb_ref, a3, a4):
        i = pl.program_id(0)
        _acc_init(i, a3, a4)
        dz = dz_ref[...]
        chat, rstd = _ln_stats(c_ref[...])
        n = chat * lg_ref[...] + lb_ref[...]
        act, dact = _silu_and_grad(n)
        s_ref[...] = act.astype(BF16)
        dn = _dot_nt(dz.astype(BF16), w2_ref[...]) * dact
        a3[...] += _colsum8(dn * chat)
        a4[...] += _colsum8(dn)
        dch = dn * lg_ref[...]
        m1 = jnp.mean(dch, axis=-1, keepdims=True)
        m2 = jnp.mean(dch * chat, axis=-1, keepdims=True)
        dc_ref[...] = rstd * (dch - m1 - chat * m2)
        _write_sums(i, nt, [(dlg_ref, a3), (dlb_ref, a4)])

    vec = _const((1, d))
    return _tc_call(
        body, name="bwd_conv_head", nt=nt,
        in_specs=[_rows(tm, d), _rows(tm, d), _const((d, d)), vec, vec],
        out_specs=[_rows(tm, d), _rows(tm, d)] + [_acc_out((1, d))] * 2,
        out_shape=[jax.ShapeDtypeStruct((t, d), BF16), jax.ShapeDtypeStruct((t, d), F32)]
        + [jax.ShapeDtypeStruct((1, d), F32)] * 2,
        scratch_shapes=[pltpu.VMEM((8, d), F32)] * 2, operands=(dz, c, w2, lng, lnb), riders=riders)


def _bwd_conv_glu(dc, u, a, g, dz, wdw, w1s, tm, riders=()):
    t, d = dc.shape
    dh_w = d // 2
    nt = t // tm
    hb = tm // CONV_HALO
    last_halo = t // CONV_HALO - 1

    def body(dc_ref, dcn_ref, u_ref, a_ref, g_ref, dz_ref, w_ref, w1_ref,
             dx_ref, dh_ref, db1_ref, dbdw_ref, dw_ref, ext, sh, du_scr, acc_b1, acc_bdw, acc_w):
        i = pl.program_id(0)
        _acc_init(i, acc_b1, acc_bdw, acc_w)
        dcv = dc_ref[...]
        acc_bdw[...] += _colsum8(dcv)

        ext[0:tm] = dcv
        ext[tm:tm + CONV_HALO] = jnp.where(i == nt - 1, 0.0, dcn_ref[...])
        ext[tm + CONV_HALO:tm + CONV_HALO + 8] = jnp.zeros((8, d), F32)
        _fill_shifted(sh, ext)

        def du_chunk(r, carry):
            base = pl.multiple_of(r * CONV_CHUNK, CONV_CHUNK)
            _tap_sum(w_ref, sh, base, d, lambda k: CONV_WIDTH - 1 - k, du_scr)
            return carry

        lax.fori_loop(0, tm // CONV_CHUNK, du_chunk, 0)

        def dw_chunk(r, carry):
            base = pl.multiple_of(r * CONV_CHUNK, CONV_CHUNK)
            groups = CONV_CHUNK // 8
            for lg in range(d // LANES):
                ls = slice(lg * LANES, (lg + 1) * LANES)
                uv = u_ref[pl.ds(base, CONV_CHUNK), ls].reshape(groups, 8, LANES)
                for k in range(CONV_WIDTH):
                    e = CONV_WIDTH - 1 - k
                    x = sh[e % 8, pl.ds(base + (e // 8) * 8, CONV_CHUNK), ls].reshape(groups, 8, LANES)
                    acc_w[k, :, ls] += jnp.sum(uv * x, axis=0)
            return carry

        lax.fori_loop(0, tm // CONV_CHUNK, dw_chunk, 0)

        du = du_scr[...]
        av = a_ref[...].astype(F32)
        sg = jax.nn.sigmoid(g_ref[...].astype(F32))
        da = du * sg
        dg = du * av * sg * (1.0 - sg)
        acc_b1[:, 0:d] += _colsum8(da)
        acc_b1[:, d:2 * d] += _colsum8(dg)
        dx = ALPHA * dz_ref[...]
        for j, part in enumerate([da[:, 0:dh_w], da[:, dh_w:d], dg[:, 0:dh_w], dg[:, dh_w:d]]):
            pb = part.astype(BF16)
            dh_ref[j] = pb
            dx = dx + _dot_nt(pb, w1_ref[j])
        dx_ref[...] = dx

        @pl.when(i == nt - 1)
        def _():
            db1_ref[...] = jnp.sum(acc_b1[...], axis=0, keepdims=True)
            dbdw_ref[...] = jnp.sum(acc_bdw[...], axis=0, keepdims=True)
            dw_ref[...] = jnp.sum(acc_w[...], axis=1)

    return _tc_call(
        body, name="bwd_conv_glu", nt=nt,
        in_specs=[_rows(tm, d), pl.BlockSpec((CONV_HALO, d), lambda i: (jnp.minimum((i + 1) * hb, last_halo), 0)),
                  _rows(tm, d), _rows(tm, d), _rows(tm, d), _rows(tm, d), _const((CONV_HALO, 8, d)),
                  _const((4, d, dh_w))],
        out_specs=[_rows(tm, d), pl.BlockSpec((4, tm, dh_w), lambda i: (0, i, 0)), _acc_out((1, 2 * d)),
                   _acc_out((1, d)), _acc_out((CONV_HALO, d))],
        out_shape=[jax.ShapeDtypeStruct((t, d), F32), jax.ShapeDtypeStruct((4, t, dh_w), BF16),
                   jax.ShapeDtypeStruct((1, 2 * d), F32), jax.ShapeDtypeStruct((1, d), F32),
                   jax.ShapeDtypeStruct((CONV_HALO, d), F32)],
        scratch_shapes=[pltpu.VMEM((tm + CONV_HALO + 8, d), F32), pltpu.VMEM((8, tm + CONV_HALO, d), F32),
                        pltpu.VMEM((tm, d), F32), pltpu.VMEM((8, 2 * d), F32), pltpu.VMEM((8, d), F32),
                        pltpu.VMEM((CONV_HALO, 8, d), F32)],
        operands=(dc, dc, u, a, g, dz, wdw, w1s), riders=riders)


def _row_block(rows, target):
    best = rows
    for cand in range(8, min(rows, target) + 1, 8):
        if rows % cand == 0:
            best = cand
    return best


def _adamw(w, g, m, v, name):
    rows, lanes = w.shape
    br = _row_block(rows, 512) if rows % 8 == 0 else rows

    def body(w_ref, g_ref, m_ref, v_ref, d_ref, nm_ref, nv_ref):
        gv = g_ref[...]
        nm = ADAM_B1 * m_ref[...] + (1.0 - ADAM_B1) * gv
        nv = ADAM_B2 * v_ref[...] + (1.0 - ADAM_B2) * (gv * gv)
        m_hat = nm / (1.0 - ADAM_B1 ** ADAM_STEP)
        v_hat = nv / (1.0 - ADAM_B2 ** ADAM_STEP)
        d_ref[...] = -ADAM_LR * (m_hat / (jnp.sqrt(v_hat) + ADAM_EPS) + ADAM_WD * w_ref[...])
        nm_ref[...] = nm
        nv_ref[...] = nv

    spec = pl.BlockSpec((br, lanes), lambda i: (i, 0))
    return pl.pallas_call(
        body, name=name, grid=(rows // br,), in_specs=[spec] * 4, out_specs=[spec] * 3,
        out_shape=[jax.ShapeDtypeStruct((rows, lanes), F32)] * 3, compiler_params=_cparams(),
    )(w, g, m, v)


def _pad_to(v, n):
    return jnp.pad(v, (0, n - v.shape[0]))


def _round_up(n, m):
    return (n + m - 1) // m * m


def kernel(x, conv_w_pw1, conv_b_pw1, conv_w_dw, conv_b_dw, conv_ln_g, conv_ln_b, conv_w_pw2, conv_b_pw2, kv_w_k, kv_b_k, kv_w_v, kv_b_v, attn_w_q, attn_b_q, attn_sinks, attn_w_o, attn_b_o, ffn_w_gate, ffn_w_up, ffn_w_down, ln_mix_g, ln_mix_b, ln_ffn_g, ln_ffn_b, loss_target, m_conv_w_pw1, m_conv_b_pw1, m_conv_w_dw, m_conv_b_dw, m_conv_ln_g, m_conv_ln_b, m_conv_w_pw2, m_conv_b_pw2, m_kv_w_k, m_kv_b_k, m_kv_w_v, m_kv_b_v, m_attn_w_q, m_attn_b_q, m_attn_sinks, m_attn_w_o, m_attn_b_o, m_ffn_w_gate, m_ffn_w_up, m_ffn_w_down, m_ln_mix_g, m_ln_mix_b, m_ln_ffn_g, m_ln_ffn_b, v_conv_w_pw1, v_conv_b_pw1, v_conv_w_dw, v_conv_b_dw, v_conv_ln_g, v_conv_ln_b, v_conv_w_pw2, v_conv_b_pw2, v_kv_w_k, v_kv_b_k, v_kv_w_v, v_kv_b_v, v_attn_w_q, v_attn_b_q, v_attn_sinks, v_attn_w_o, v_attn_b_o, v_ffn_w_gate, v_ffn_w_up, v_ffn_w_down, v_ln_mix_g, v_ln_mix_b, v_ln_ffn_g, v_ln_ffn_b):
    args = dict(locals())
    w = {n: args[n] for n in WEIGHTS}
    mom = {n: args["m_" + n] for n in WEIGHTS}
    var = {n: args["v_" + n] for n in WEIGHTS}
    assert x.shape[0] == 1, "one sequence per device"
    t, d = x.shape[1], x.shape[2]
    dq = d // 4
    fs = ffn_w_gate.shape[-1]
    nq = d // HEAD_DIM
    x0 = x.reshape(t, d)
    target = loss_target.reshape(t, d)
    tm_big = min(512, t)
    tm_mid = min(256, t)
    tm_tn = min(1024, t)
    c_idx = lax.axis_index("c")

    me_idx = 2 * lax.axis_index("x") + lax.axis_index("y")

    def gather_buffer(v):
        buf = lax.empty((N_CHIPS,) + v.shape, v.dtype)
        return lax.dynamic_update_slice(buf, v[None], (me_idx,) + (0,) * v.ndim)

    def halves(v):
        return v.reshape(2, -1, v.shape[-1])

    small_sizes = [int(w[n].size) for n in SMALL_SHARDED]
    rs = _round_up(sum(small_sizes), 8 * 128) // 128
    spack = _pad_to(jnp.concatenate([w[n].reshape(-1) for n in SMALL_SHARDED]), rs * 128).reshape(rs, 128)
    conv_first = ['conv_w_pw1', 'conv_w_pw2']
    later = [n for n in BIG if n not in conv_first]
    (first_out,) = _run_riders(
        [_all_gather_rider([gather_buffer(halves(w[n].astype(BF16))) for n in conv_first], gather_buffer(spack))],
        "all_gather_conv")
    later_rider = _all_gather_rider([gather_buffer(halves(w[n].astype(BF16))) for n in later])
    gs = first_out[-1].reshape(N_CHIPS, rs * 128)
    full = {n: g.reshape((N_CHIPS,) + w[n].shape) for n, g in zip(conv_first, first_out)}
    off = 0
    for n, size in zip(SMALL_SHARDED, small_sizes):
        full[n] = gs[:, off:off + size].reshape((N_CHIPS,) + w[n].shape)
        off += size
    w1s = full['conv_w_pw1'].reshape(N_CHIPS, d, d // 2)
    w2 = full['conv_w_pw2'].reshape(d, d)
    b1 = full['conv_b_pw1'].reshape(1, 2 * d)
    wdw = jnp.pad(full['conv_w_dw'].reshape(N_CHIPS, CONV_WIDTH, dq).transpose(1, 0, 2).reshape(CONV_WIDTH, d),
                  ((0, CONV_HALO - CONV_WIDTH), (0, 0)))
    wdw = jnp.broadcast_to(wdw[:, None, :], (CONV_HALO, 8, d))
    bdw = full['conv_b_dw'].reshape(1, d)
    clng = full['conv_ln_g'].reshape(1, d)
    clnb = full['conv_ln_b'].reshape(1, d)
    b2 = full['conv_b_pw2'].reshape(1, d)
    bkv = jnp.concatenate([kv_b_k, kv_b_v]).reshape(1, 2 * KVD)
    sinks = attn_sinks.reshape(nq)
    mixg = [ln_mix_g[l].reshape(1, d) for l in range(DEPTH)]
    mixb = [ln_mix_b[l].reshape(1, d) for l in range(DEPTH)]
    ffng = [ln_ffn_g[l].reshape(1, d) for l in range(DEPTH)]
    ffnb = [ln_ffn_b[l].reshape(1, d) for l in range(DEPTH)]

    a_act, g_act, u_act = _fwd_pw1_glu(x0, w1s, b1, tm_big)
    (c_act, z1, x1), (later_out,) = _fwd_conv_tail(u_act, x0, wdw, bdw, clng, clnb, w2, b2, mixg[0], mixb[0], tm_mid,
                                                   riders=[later_rider])
    full.update({n: g.reshape((N_CHIPS,) + w[n].shape) for n, g in zip(later, later_out)})
    wkv = jnp.concatenate([full['kv_w_k'].reshape(d, KVD), full['kv_w_v'].reshape(d, KVD)], axis=1)
    wqt = full['attn_w_q'].reshape(d, d).T
    wo = full['attn_w_o'].reshape(d, d)
    wg, wu, wd = full['ffn_w_gate'], full['ffn_w_up'], full['ffn_w_down']
    act0, bm0, hm0, z2, x2 = _fwd_ffn(x1, wg, wu, wd, 0, ffng[0], ffnb[0], tm_big)
    qt_act, kv_act, ot_act, z3, x3 = _fwd_attn(x2, wqt, attn_b_q.reshape(d, 1), wkv, bkv, sinks, wo, attn_b_o,
                                               mixg[1], mixb[1], tm_big)
    act1, bm1, hm1, dz4, d_fg1, d_fb1, loss_part = _fwd_ffn(x3, wg, wu, wd, 1, ffng[1], ffnb[1], tm_big, target=target)
    loss = lax.psum(loss_part[0, 0], ("x", "y", "c"))

    c_arr = c_idx.reshape(1).astype(jnp.int32)

    def halves4(v):
        return v.reshape(N_CHIPS, 2, -1, v.shape[-1])

    def arrays(group):
        return [p for _, p in group]

    def pair_sums(group, got):
        return [_pair_sum(p, g, c_arr, "grad_pair_sum_" + n) for (n, p), g in zip(group, got)]

    pos_arr = jnp.stack([me_idx, c_idx]).astype(jnp.int32)

    def chip_sums(group, sums, got):
        return [_chip_sum(s, g, pos_arr, "grad_chip_sum_" + n) for (n, _), s, g in zip(group, sums, got)]

    (dgg1, duu1, dz3, d_mg1, d_mb1, d_bo), _ = _bwd_ffn_dx(dz4, act1, bm1, wg, wu, wd, 1, z3, mixg[1], tm_big)
    g1 = [("ffn_w_gate1", halves4(_matmul_tn(x3[None], dgg1, tm_tn, "dw_gate1"))),
          ("ffn_w_up1", halves4(_matmul_tn(x3[None], duu1, tm_tn, "dw_up1"))),
          ("ffn_w_down1", halves4(_matmul_tn(hm1, dz4[None], tm_tn, "dw_down1")))]
    (dqt, dkv, dz2, d_bq, d_bkv, d_sinks, d_fg0, d_fb0), (got1,) = _bwd_attn(
        dz3, qt_act, kv_act, sinks, wo, wqt, wkv, z2, ffng[0], tm_big, riders=[_pair_exchange_rider(arrays(g1))])
    s1 = pair_sums(g1, got1)
    dwo = _matmul_nn(ot_act, dz3, tm_tn, "dw_o")
    dwq = _matmul_nn(dqt, x2, tm_tn, "dw_q").T
    dwkv = _matmul_tn(x2[None], dkv[None], tm_tn, "dw_kv")[0]
    g2 = [("attn_w_o", halves4(dwo)), ("attn_w_q", halves4(dwq)),
          ("kv_w_k", halves4(dwkv[:, 0:KVD])), ("kv_w_v", halves4(dwkv[:, KVD:2 * KVD]))]
    (dgg0, duu0, dz1, d_mg0, d_mb0, d_b2), (from_chips1, got2) = _bwd_ffn_dx(
        dz2, act0, bm0, wg, wu, wd, 0, z1, mixg[0], tm_big,
        riders=[_chip_scatter_rider(s1), _pair_exchange_rider(arrays(g2))])
    f1 = chip_sums(g1, s1, from_chips1)
    s2 = pair_sums(g2, got2)
    g3 = [("ffn_w_gate0", halves4(_matmul_tn(x1[None], dgg0, tm_tn, "dw_gate0"))),
          ("ffn_w_up0", halves4(_matmul_tn(x1[None], duu0, tm_tn, "dw_up0"))),
          ("ffn_w_down0", halves4(_matmul_tn(hm0, dz2[None], tm_tn, "dw_down0")))]
    (s_act, dc, d_clng, d_clnb), (got3, shared1) = _bwd_conv_head(
        dz1, c_act, w2, clng, clnb, tm_mid, riders=[_pair_exchange_rider(arrays(g3)), _pair_share_rider(f1)])
    s3 = pair_sums(g3, got3)
    dw2 = _matmul_tn(s_act[None], dz1[None], tm_tn, "dw_pw2")
    (dx0, dh1, d_b1, d_bdw, d_wdw), (from_chips2, from_chips3) = _bwd_conv_glu(
        dc, u_act, a_act, g_act, dz1, wdw, w1s, tm_mid, riders=[_chip_scatter_rider(s2), _chip_scatter_rider(s3)])
    f2 = chip_sums(g2, s2, from_chips2)
    f3 = chip_sums(g3, s3, from_chips3)
    dw1 = _matmul_tn(x0[None], dh1, tm_tn, "dw_pw1")

    def rows4(v):
        return v.reshape(N_CHIPS, -1)

    def rep4(v):
        return jnp.broadcast_to(v.reshape(1, -1), (N_CHIPS, v.size))

    local = {
        'conv_b_pw1': rows4(d_b1),
        'conv_w_dw': rows4(d_wdw[0:CONV_WIDTH].reshape(CONV_WIDTH, N_CHIPS, dq).transpose(1, 0, 2)),
        'conv_b_dw': rows4(d_bdw), 'conv_ln_g': rows4(d_clng), 'conv_ln_b': rows4(d_clnb), 'conv_b_pw2': rows4(d_b2),
        'kv_b_k': rep4(d_bkv[:, 0:KVD]), 'kv_b_v': rep4(d_bkv[:, KVD:2 * KVD]), 'attn_b_q': rep4(d_bq),
        'attn_sinks': rep4(d_sinks), 'attn_b_o': rep4(d_bo),
        'ln_mix_g': rep4(jnp.concatenate([d_mg0, d_mg1])), 'ln_mix_b': rep4(jnp.concatenate([d_mb0, d_mb1])),
        'ln_ffn_g': rep4(jnp.concatenate([d_fg0, d_fg1])), 'ln_ffn_b': rep4(jnp.concatenate([d_fb0, d_fb1])),
    }
    n_small = sum(int(w[n].size) for n in SMALL)
    small_rows = _round_up(n_small, 2 * 8 * 128) // 128
    small_local = jnp.concatenate([local[n] for n in SMALL], axis=1)
    small_local = jnp.pad(small_local, ((0, 0), (0, small_rows * 128 - n_small)))
    g4 = [("conv_w_pw1", halves4(dw1)), ("conv_w_pw2", halves4(dw2)),
          ("small", small_local.reshape(N_CHIPS, 2, small_rows // 2, 128))]
    (got4,) = _run_riders([_pair_exchange_rider(arrays(g4))], "grad_pair_exchange_last")
    s4 = pair_sums(g4, got4)
    (from_chips4,) = _run_riders([_chip_scatter_rider(s4)], "grad_chip_scatter_last")
    f4 = chip_sums(g4, s4, from_chips4)
    (shared_rest,) = _run_riders([_pair_share_rider(f2 + f3 + f4)], "grad_pair_share_last")
    reduced = dict(zip([n for n, _ in g1], shared1))
    reduced.update(zip([n for n, _ in g2 + g3 + g4], shared_rest))
    for n in ('ffn_w_gate', 'ffn_w_up', 'ffn_w_down'):
        reduced[n] = jnp.stack([reduced[n + str(layer)].reshape(w[n].shape[1:]) for layer in range(DEPTH)])

    g_out, delta, new_m, new_v = {}, {}, {}, {}
    for n in BIG:
        shape = w[n].shape
        two_d = (-1, shape[-1])
        g_out[n] = reduced[n].reshape(shape)
        dl, nm, nv = _adamw(w[n].reshape(two_d), g_out[n].reshape(two_d), mom[n].reshape(two_d), var[n].reshape(two_d),
                            "adamw_" + n)
        delta[n], new_m[n], new_v[n] = dl.reshape(shape), nm.reshape(shape), nv.reshape(shape)

    def pack_small(tree):
        return _pad_to(jnp.concatenate([tree[n].reshape(-1) for n in SMALL]), small_rows * 128).reshape(small_rows, 128)

    g_small = reduced['small'].reshape(small_rows, 128)
    dl, nm, nv = _adamw(pack_small(w), g_small, pack_small(mom), pack_small(var), "adamw_small")
    off = 0
    for n in SMALL:
        size, shape = int(w[n].size), w[n].shape
        for tree, flat in ((g_out, g_small), (delta, dl), (new_m, nm), (new_v, nv)):
            tree[n] = flat.reshape(-1)[off:off + size].reshape(shape)
        off += size

    return (loss, dx0.reshape(x.shape), *[g_out[n] for n in WEIGHTS], *[delta[n] for n in WEIGHTS],
            *[new_m[n] for n in WEIGHTS], *[new_v[n] for n in WEIGHTS])
```

```python
import functools
import math

import jax
import jax.numpy as jnp
from jax import lax
from jax.experimental import pallas as pl
from jax.experimental.pallas import tpu as pltpu

F32 = jnp.float32
BF16 = jnp.bfloat16

DEPTH = 2
ALPHA = (2.0 * DEPTH) ** 0.25
LN_EPS = 1e-5
NEG_INF = -1e30
HEAD_DIM = 64
N_KV_HEADS = 2
KVD = N_KV_HEADS * HEAD_DIM
BLOCK = 128
CONV_WIDTH = 31
CONV_HALO = 32
ALIBI_MAX = 8.0
ADAM_LR, ADAM_B1, ADAM_B2, ADAM_EPS, ADAM_WD, ADAM_STEP = 0.001, 0.9, 0.999, 1e-08, 0.01, 10

N_CHIPS = 4
PACK_ROWS = 256
VMEM_LIMIT = 60 * 1024 * 1024
MESH = pl.DeviceIdType.MESH

NT_DIMS = (((1,), (1,)), ((), ()))
TN_DIMS = (((0,), (0,)), ((), ()))

WEIGHTS = ['conv_w_pw1', 'conv_b_pw1', 'conv_w_dw', 'conv_b_dw', 'conv_ln_g', 'conv_ln_b', 'conv_w_pw2', 'conv_b_pw2',
           'kv_w_k', 'kv_b_k', 'kv_w_v', 'kv_b_v', 'attn_w_q', 'attn_b_q', 'attn_sinks', 'attn_w_o', 'attn_b_o',
           'ffn_w_gate', 'ffn_w_up', 'ffn_w_down', 'ln_mix_g', 'ln_mix_b', 'ln_ffn_g', 'ln_ffn_b']
BIG = ['conv_w_pw1', 'conv_w_pw2', 'kv_w_k', 'kv_w_v', 'attn_w_q', 'attn_w_o', 'ffn_w_gate', 'ffn_w_up', 'ffn_w_down']
SMALL_SHARDED = ['conv_b_pw1', 'conv_w_dw', 'conv_b_dw', 'conv_ln_g', 'conv_ln_b', 'conv_b_pw2']
REPLICATED = ['kv_b_k', 'kv_b_v', 'attn_b_q', 'attn_sinks', 'attn_b_o', 'ln_mix_g', 'ln_mix_b', 'ln_ffn_g', 'ln_ffn_b']
SMALL = SMALL_SHARDED + REPLICATED


def _cparams(n_grid=1):
    return pltpu.CompilerParams(dimension_semantics=("arbitrary",) * n_grid, vmem_limit_bytes=VMEM_LIMIT)


def _rows(tm, width):
    return pl.BlockSpec((tm, width), lambda i: (i, 0))


def _const(shape):
    return pl.BlockSpec(shape, lambda *_: (0,) * len(shape), pipeline_mode=pl.Buffered(1))


def _acc_out(shape):
    return pl.BlockSpec(shape, lambda *_: (0,) * len(shape))


def _dot(a, b):
    return jnp.dot(a, b, preferred_element_type=F32)


def _dot_nt(a, b):
    return lax.dot_general(a, b, NT_DIMS, preferred_element_type=F32)


def _dot_tn(a, b):
    return lax.dot_general(a, b, TN_DIMS, preferred_element_type=F32)


def _colsum8(v):
    m, n = v.shape
    return jnp.sum(v.reshape(m // 8, 8, n), axis=0)


def _ln_stats(z):
    mu = jnp.mean(z, axis=-1, keepdims=True)
    zc = z - mu
    var = jnp.mean(zc * zc, axis=-1, keepdims=True)
    rstd = lax.rsqrt(var + LN_EPS)
    return zc * rstd, rstd


def _ln_fwd(z, g, b):
    zhat, _ = _ln_stats(z)
    return zhat * g + b


def _ln_bwd(dy, z, g):
    zhat, rstd = _ln_stats(z)
    dzh = dy * g
    m1 = jnp.mean(dzh, axis=-1, keepdims=True)
    m2 = jnp.mean(dzh * zhat, axis=-1, keepdims=True)
    return rstd * (dzh - m1 - zhat * m2), zhat


def _silu_and_grad(n):
    sg = jax.nn.sigmoid(n)
    return n * sg, sg * (1.0 + n * (1.0 - sg))


def _acc_init(i, *refs):
    @pl.when(i == 0)
    def _():
        for r in refs:
            r[...] = jnp.zeros_like(r)


def _mesh_pos():
    x, y, c = lax.axis_index("x"), lax.axis_index("y"), lax.axis_index("c")
    chips = [(1 - x, y), (x, 1 - y), (1 - x, 1 - y)]
    return x, y, c, chips


HBM_SPEC = pl.BlockSpec(memory_space=pltpu.HBM)


def _remote(src, dst, send_sems, recv_sems, k, to):
    return pltpu.make_async_remote_copy(src_ref=src, dst_ref=dst, send_sem=send_sems.at[k], recv_sem=recv_sems.at[k],
                                        device_id=to, device_id_type=MESH)


class _Rider:
    def __init__(self, operands, out_shapes, sem_shapes, start, finish, mid=None, in_place=False):
        self.operands, self.out_shapes, self.sem_shapes = list(operands), list(out_shapes), list(sem_shapes)
        self.start, self.finish, self.mid = start, finish, mid
        self.in_place = in_place


def _rider_aliases(riders, first_in, first_out):
    aliases, k_in, k_out = {}, first_in, first_out
    for r in riders:
        if r.in_place:
            aliases.update({k_in + k: k_out + k for k in range(len(r.operands))})
        k_in += len(r.operands)
        k_out += len(r.out_shapes)
    return aliases


def _split(refs, counts):
    parts, k = [], 0
    for n in counts:
        parts.append(refs[k:k + n])
        k += n
    return parts


def _rider_refs(riders, ins, outs, sems):
    return list(zip(riders, _split(ins, [len(r.operands) for r in riders]),
                    _split(outs, [len(r.out_shapes) for r in riders]),
                    _split(sems, [len(r.sem_shapes) for r in riders])))


def _tc_call(body, *, name, nt, in_specs, out_specs, out_shape, operands, scratch_shapes=(), riders=(), mid_frac=0.75):
    n_in, n_out, n_scr = len(in_specs), len(out_specs), len(scratch_shapes)
    r_ops = [o for r in riders for o in r.operands]
    r_outs = [o for r in riders for o in r.out_shapes]
    r_sems = [s for r in riders for s in r.sem_shapes]
    mid_step = min(max(int(nt * mid_frac), 0), nt - 1)

    def full(*refs):
        ins, r_in, outs, r_out, scr, r_sem = _split(refs, [n_in, len(r_ops), n_out, len(r_outs), n_scr, len(r_sems)])
        parts = _rider_refs(riders, r_in, r_out, r_sem)
        step = pl.program_id(0)

        @pl.when(step == 0)
        def _():
            for r, a, b, s in parts:
                r.start(a, b, s)

        body(*ins, *outs, *scr)

        @pl.when(step == mid_step)
        def _():
            for r, a, b, s in parts:
                if r.mid is not None:
                    r.mid(a, b, s)

        @pl.when(step == nt - 1)
        def _():
            for r, a, b, s in parts:
                r.finish(a, b, s)

    res = pl.pallas_call(
        full if riders else body, name=name, grid=(nt,), in_specs=list(in_specs) + [HBM_SPEC] * len(r_ops),
        out_specs=list(out_specs) + [HBM_SPEC] * len(r_outs), out_shape=list(out_shape) + r_outs,
        scratch_shapes=list(scratch_shapes) + r_sems, input_output_aliases=_rider_aliases(riders, n_in, n_out),
        compiler_params=_cparams(),
    )(*operands, *r_ops)
    return res[:n_out], _split(res[n_out:], [len(r.out_shapes) for r in riders])


def _run_riders(riders, name):
    r_ops = [o for r in riders for o in r.operands]
    r_outs = [o for r in riders for o in r.out_shapes]
    r_sems = [s for r in riders for s in r.sem_shapes]

    def body(*refs):
        r_in, r_out, r_sem = _split(refs, [len(r_ops), len(r_outs), len(r_sems)])
        parts = _rider_refs(riders, r_in, r_out, r_sem)
        for r, a, b, s in parts:
            r.start(a, b, s)
        for r, a, b, s in parts:
            if r.mid is not None:
                r.mid(a, b, s)
        for r, a, b, s in parts:
            r.finish(a, b, s)

    res = pl.pallas_call(body, name=name, out_shape=tuple(r_outs), in_specs=[HBM_SPEC] * len(r_ops),
                         out_specs=(HBM_SPEC,) * len(r_outs), scratch_shapes=r_sems,
                         input_output_aliases=_rider_aliases(riders, 0, 0))(*r_ops)
    return _split(list(res), [len(r.out_shapes) for r in riders])


def _all_gather_rider(bufs, small=None):
    n = len(bufs)
    n_small = 0 if small is None else 1

    def copies(outs, sems):
        send_sems, recv_sems = sems
        x, y, c, chips = _mesh_pos()
        me = 2 * x + y
        here, sibling = (x, y, c), (x, y, 1 - c)
        rows = [2 * cx + cy for cx, cy in chips]

        def big(p, k, chip_row, half, to):
            piece = outs[p].at[chip_row, half]
            return _remote(piece, piece, send_sems, recv_sems, 6 * p + k, to)

        first = [big(p, j, me, c, (cx, cy, c)) for p in range(n) for j, (cx, cy) in enumerate(chips)]
        landed = [big(p, j, rows[j], c, here) for p in range(n) for j in range(3)]
        passed = [big(p, 3 + j, rows[j], c, sibling) for p in range(n) for j in range(3)]
        arrivals = [big(p, 3 + j, rows[j], 1 - c, here) for p in range(n) for j in range(3)]
        if n_small:
            first = [_remote(outs[n].at[me], outs[n].at[me], send_sems, recv_sems, 6 * n + j, (cx, cy, c))
                     for j, (cx, cy) in enumerate(chips)] + first
            arrivals += [_remote(outs[n].at[rows[j]], outs[n].at[rows[j]], send_sems, recv_sems, 6 * n + j, here)
                         for j in range(3)]
        return first, landed, passed, arrivals

    def start(ins, outs, sems):
        for cp in copies(outs, sems)[0]:
            cp.start()

    def mid(ins, outs, sems):
        _, landed, passed, _ = copies(outs, sems)
        for got, fwd in zip(landed, passed):
            got.wait_recv()
            fwd.start()

    def finish(ins, outs, sems):
        first, _, passed, arrivals = copies(outs, sems)
        for cp in arrivals:
            cp.wait_recv()
        for cp in first + passed:
            cp.wait_send()

    operands = list(bufs) + ([small] if n_small else [])
    n_sem = 6 * n + 3 * n_small
    return _Rider(operands, [jax.ShapeDtypeStruct(o.shape, o.dtype) for o in operands],
                  [pltpu.SemaphoreType.DMA((n_sem,)), pltpu.SemaphoreType.DMA((n_sem,))], start, finish, mid,
                  in_place=True)


def _pair_exchange_rider(plist):
    n = len(plist)

    def copies(ins, outs, sems):
        x, y, c, _ = _mesh_pos()
        return [_remote(ins[k].at[:, 1 - c], outs[k], sems[0], sems[1], k, (x, y, 1 - c)) for k in range(n)]

    def start(ins, outs, sems):
        for cp in copies(ins, outs, sems):
            cp.start()

    def finish(ins, outs, sems):
        for cp in copies(ins, outs, sems):
            cp.wait()

    return _Rider(plist, [jax.ShapeDtypeStruct((p.shape[0],) + p.shape[2:], p.dtype) for p in plist],
                  [pltpu.SemaphoreType.DMA((n,)), pltpu.SemaphoreType.DMA((n,))], start, finish)


def _pair_sum(p, got, c, name):
    n, _, r, l = p.shape
    br = _row_block(r, PACK_ROWS)

    def body(c_ref, p_ref, got_ref, out_ref):
        out_ref[...] = p_ref[...] + got_ref[...]

    return pl.pallas_call(
        body, name=name, out_shape=jax.ShapeDtypeStruct((n, r, l), F32),
        grid_spec=pltpu.PrefetchScalarGridSpec(
            num_scalar_prefetch=1, grid=(n, r // br),
            in_specs=[pl.BlockSpec((None, None, br, l), lambda j, i, c_ref: (j, c_ref[0], i, 0)),
                      pl.BlockSpec((None, br, l), lambda j, i, c_ref: (j, i, 0))],
            out_specs=pl.BlockSpec((None, br, l), lambda j, i, c_ref: (j, i, 0))),
        compiler_params=_cparams(2),
    )(c, p, got)


def _chip_scatter_rider(slist):
    n = len(slist)

    def copies(ins, outs, sems):
        send_sems, recv_sems = sems
        x, y, c, chips = _mesh_pos()
        sends = [_remote(ins[k].at[2 * cx + cy], outs[k].at[j], send_sems, recv_sems, 3 * k + j, (cx, cy, c))
                 for k in range(n) for j, (cx, cy) in enumerate(chips)]
        arrivals = [_remote(ins[k].at[0], outs[k].at[j], send_sems, recv_sems, 3 * k + j, (x, y, c))
                    for k in range(n) for j in range(3)]
        return sends, arrivals

    def start(ins, outs, sems):
        for cp in copies(ins, outs, sems)[0]:
            cp.start()

    def finish(ins, outs, sems):
        sends, arrivals = copies(ins, outs, sems)
        for cp in arrivals:
            cp.wait_recv()
        for cp in sends:
            cp.wait_send()

    return _Rider(slist, [jax.ShapeDtypeStruct((3,) + s.shape[1:], s.dtype) for s in slist],
                  [pltpu.SemaphoreType.DMA((3 * n,)), pltpu.SemaphoreType.DMA((3 * n,))], start, finish)


def _chip_sum(s, got, pos, name):
    _, r, l = s.shape
    br = _row_block(r, PACK_ROWS)

    def body(pos_ref, s_ref, got_ref, out_ref):
        me = pos_ref[0]
        total = None
        for chip in range(N_CHIPS):
            flip = jnp.bitwise_xor(me, chip)
            term = jnp.where(flip == 0, s_ref[...],
                             jnp.where(flip == 2, got_ref[0], jnp.where(flip == 1, got_ref[1], got_ref[2])))
            total = term if total is None else total + term
        out_ref[...] = total

    return pl.pallas_call(
        body, name=name, out_shape=jax.ShapeDtypeStruct((2, r, l), F32),
        grid_spec=pltpu.PrefetchScalarGridSpec(
            num_scalar_prefetch=1, grid=(r // br,),
            in_specs=[pl.BlockSpec((None, br, l), lambda i, pos_ref: (pos_ref[0], i, 0)),
                      pl.BlockSpec((3, br, l), lambda i, pos_ref: (0, i, 0))],
            out_specs=pl.BlockSpec((None, br, l), lambda i, pos_ref: (pos_ref[1], i, 0))),
        compiler_params=_cparams(1),
    )(pos, s, got)


def _pair_share_rider(flist):
    n = len(flist)

    def copies(outs, sems):
        x, y, c, _ = _mesh_pos()
        sends = [_remote(outs[k].at[c], outs[k].at[c], sems[0], sems[1], k, (x, y, 1 - c)) for k in range(n)]
        arrivals = [_remote(outs[k].at[1 - c], outs[k].at[1 - c], sems[0], sems[1], k, (x, y, c)) for k in range(n)]
        return sends, arrivals

    def start(ins, outs, sems):
        for cp in copies(outs, sems)[0]:
            cp.start()

    def finish(ins, outs, sems):
        sends, arrivals = copies(outs, sems)
        for cp in arrivals:
            cp.wait_recv()
        for cp in sends:
            cp.wait_send()

    return _Rider(flist, [jax.ShapeDtypeStruct(f.shape, f.dtype) for f in flist],
                  [pltpu.SemaphoreType.DMA((n,)), pltpu.SemaphoreType.DMA((n,))], start, finish, in_place=True)


def _fwd_pw1_glu(x, w1s, b1, tm):
    t, d = x.shape
    dh = d // 2

    def body(x_ref, w_ref, b_ref, a_ref, g_ref, u_ref):
        xb = x_ref[...].astype(BF16)
        for hh in range(2):
            cs = slice(hh * dh, (hh + 1) * dh)
            a = _dot(xb, w_ref[hh]) + b_ref[:, hh * dh:(hh + 1) * dh]
            g = _dot(xb, w_ref[2 + hh]) + b_ref[:, d + hh * dh:d + (hh + 1) * dh]
            a_ref[:, cs] = a.astype(BF16)
            g_ref[:, cs] = g.astype(BF16)
            u_ref[:, cs] = a * jax.nn.sigmoid(g)

    return pl.pallas_call(
        body, name="fwd_pw1_glu", grid=(t // tm,),
        in_specs=[_rows(tm, d), _const((4, d, dh)), _const((1, 2 * d))],
        out_specs=[_rows(tm, d)] * 3,
        out_shape=[jax.ShapeDtypeStruct((t, d), BF16), jax.ShapeDtypeStruct((t, d), BF16),
                   jax.ShapeDtypeStruct((t, d), F32)],
        compiler_params=_cparams(),
    )(x, w1s, b1)


def _fill_shifted(sh_ref, ext_ref):
    n = sh_ref.shape[1]
    for s in range(8):
        sh_ref[s] = ext_ref[pl.ds(s, n), :]


CONV_CHUNK = 64
LANES = 256


def _tap_sum(w_ref, sh, base, d, tap_row, out_ref, bias_ref=None):
    groups = CONV_CHUNK // 8
    for lg in range(d // LANES):
        ls = slice(lg * LANES, (lg + 1) * LANES)
        acc = jnp.zeros((groups, 8, LANES), F32)
        for k in range(CONV_WIDTH):
            e = tap_row(k)
            x = sh[e % 8, pl.ds(base + (e // 8) * 8, CONV_CHUNK), ls]
            acc = acc + w_ref[k, :, ls] * x.reshape(groups, 8, LANES)
        acc = acc.reshape(CONV_CHUNK, LANES)
        out_ref[pl.ds(base, CONV_CHUNK), ls] = acc if bias_ref is None else acc + bias_ref[:, ls]


def _fwd_conv_tail(u, x0, wdw, bdw, lng, lnb, w2, b2, mixg, mixb, tm, riders=()):
    t, d = u.shape
    hb = tm // CONV_HALO

    def body(u_ref, uh_ref, x_ref, w_ref, bdw_ref, lng_ref, lnb_ref, w2_ref, b2_ref, mg_ref, mb_ref,
             c_ref, z_ref, y_ref, ext, sh):
        i = pl.program_id(0)
        ext[0:CONV_HALO] = jnp.where(i == 0, 0.0, uh_ref[...])
        ext[CONV_HALO:CONV_HALO + tm] = u_ref[...]
        ext[CONV_HALO + tm:CONV_HALO + tm + 8] = jnp.zeros((8, d), F32)
        _fill_shifted(sh, ext)

        def chunk(r, carry):
            base = pl.multiple_of(r * CONV_CHUNK, CONV_CHUNK)
            _tap_sum(w_ref, sh, base, d, lambda k: k + CONV_HALO - (CONV_WIDTH - 1), c_ref, bdw_ref)
            return carry

        lax.fori_loop(0, tm // CONV_CHUNK, chunk, 0)
        n = _ln_fwd(c_ref[...], lng_ref[...], lnb_ref[...])
        s = n * jax.nn.sigmoid(n)
        m = _dot(s.astype(BF16), w2_ref[...]) + b2_ref[...]
        z = ALPHA * x_ref[...] + m
        z_ref[...] = z
        y_ref[...] = _ln_fwd(z, mg_ref[...], mb_ref[...])

    vec = _const((1, d))
    return _tc_call(
        body, name="fwd_conv_tail", nt=t // tm,
        in_specs=[_rows(tm, d), pl.BlockSpec((CONV_HALO, d), lambda i: (jnp.maximum(i * hb - 1, 0), 0)), _rows(tm, d),
                  _const((CONV_HALO, 8, d)), vec, vec, vec, _const((d, d)), vec, vec, vec],
        out_specs=[_rows(tm, d)] * 3,
        out_shape=[jax.ShapeDtypeStruct((t, d), F32)] * 3,
        scratch_shapes=[pltpu.VMEM((tm + CONV_HALO + 8, d), F32), pltpu.VMEM((8, tm + CONV_HALO, d), F32)],
        operands=(u, u, x0, wdw, bdw, lng, lnb, w2, b2, mixg, mixb), riders=riders)


def _fwd_ffn(x, wg, wu, wd, layer, lng, lnb, tm, target=None):
    t, d = x.shape
    fs = wg.shape[-1]
    nt = t // tm
    with_loss = target is not None

    def hidden(x_ref, wg_ref, wu_ref, wd_ref, act_ref, bm_ref, hm_ref):
        xv = x_ref[...]
        xb = xv.astype(BF16)
        f = jnp.zeros((tm, d), F32)
        for j in range(N_CHIPS):
            gj = _dot(xb, wg_ref[j])
            uj = _dot(xb, wu_ref[j])
            act, dact = _silu_and_grad(gj)
            act_ref[j] = act.astype(BF16)
            bm_ref[j] = (uj * dact).astype(BF16)
            hmb = (act * uj).astype(BF16)
            hm_ref[j] = hmb
            f = f + _dot(hmb, wd_ref[j])
        return ALPHA * xv + f

    def body(x_ref, wg_ref, wu_ref, wd_ref, g_ref, b_ref, act_ref, bm_ref, hm_ref, z_ref, y_ref):
        z = hidden(x_ref, wg_ref, wu_ref, wd_ref, act_ref, bm_ref, hm_ref)
        z_ref[...] = z
        y_ref[...] = _ln_fwd(z, g_ref[...], b_ref[...])

    def body_loss(x_ref, wg_ref, wu_ref, wd_ref, g_ref, b_ref, t_ref, act_ref, bm_ref, hm_ref, dz_ref,
                  dlg_ref, dlb_ref, loss_ref, acc_g, acc_b, acc_l):
        i = pl.program_id(0)
        _acc_init(i, acc_g, acc_b, acc_l)
        z = hidden(x_ref, wg_ref, wu_ref, wd_ref, act_ref, bm_ref, hm_ref)
        zhat, rstd = _ln_stats(z)
        gain = g_ref[...]
        err = zhat * gain + b_ref[...] - t_ref[...]
        acc_l[...] += _colsum8(err * err)
        dy = err * (1.0 / d)
        acc_g[...] += _colsum8(dy * zhat)
        acc_b[...] += _colsum8(dy)
        dzh = dy * gain
        m1 = jnp.mean(dzh, axis=-1, keepdims=True)
        m2 = jnp.mean(dzh * zhat, axis=-1, keepdims=True)
        dz_ref[...] = rstd * (dzh - m1 - zhat * m2)

        @pl.when(i == nt - 1)
        def _():
            dlg_ref[...] = jnp.sum(acc_g[...], axis=0, keepdims=True)
            dlb_ref[...] = jnp.sum(acc_b[...], axis=0, keepdims=True)
            loss_ref[...] = jnp.sum(acc_l[...], keepdims=True) * (0.5 / d)

    wcol = pl.BlockSpec((N_CHIPS, None, d, fs), lambda i: (0, layer, 0, 0), pipeline_mode=pl.Buffered(1))
    wrow = pl.BlockSpec((N_CHIPS, None, fs, d), lambda i: (0, layer, 0, 0), pipeline_mode=pl.Buffered(1))
    hid = pl.BlockSpec((N_CHIPS, tm, fs), lambda i: (0, i, 0))
    in_specs = [_rows(tm, d), wcol, wcol, wrow, _const((1, d)), _const((1, d))]
    hid_shapes = [jax.ShapeDtypeStruct((N_CHIPS, t, fs), BF16)] * 3
    if not with_loss:
        return pl.pallas_call(
            body, name=f"fwd_ffn{layer}", grid=(nt,), in_specs=in_specs,
            out_specs=[hid, hid, hid, _rows(tm, d), _rows(tm, d)],
            out_shape=hid_shapes + [jax.ShapeDtypeStruct((t, d), F32)] * 2, compiler_params=_cparams(),
        )(x, wg, wu, wd, lng, lnb)
    return pl.pallas_call(
        body_loss, name=f"fwd_ffn{layer}_loss", grid=(nt,), in_specs=in_specs + [_rows(tm, d)],
        out_specs=[hid, hid, hid, _rows(tm, d), _acc_out((1, d)), _acc_out((1, d)), _acc_out((1, 1))],
        out_shape=hid_shapes + [jax.ShapeDtypeStruct((t, d), F32)] + [jax.ShapeDtypeStruct((1, d), F32)] * 2
        + [jax.ShapeDtypeStruct((1, 1), F32)],
        scratch_shapes=[pltpu.VMEM((8, d), F32)] * 3, compiler_params=_cparams(),
    )(x, wg, wu, wd, lng, lnb, target)


def _attn_band():
    kt = lax.broadcasted_iota(jnp.int32, (BLOCK, BLOCK), 0)
    qi = lax.broadcasted_iota(jnp.int32, (BLOCK, BLOCK), 1)
    current = kt <= qi
    delta = qi - kt + jnp.where(current, 0, BLOCK)
    return current, delta.astype(F32)


def _fold(full, current):
    return jnp.where(current, full[BLOCK:2 * BLOCK], full[0:BLOCK])


def _unfold(folded, current):
    zero = jnp.zeros_like(folded)
    return jnp.concatenate([jnp.where(current, zero, folded), jnp.where(current, folded, zero)], axis=0)


def _slope(h, nq):
    return 2.0 ** (-ALIBI_MAX * (h + 1) / nq)


def _softmax_with_sink(s_full, slope, band, has_previous, sink):
    current, delta = band
    s = _fold(s_full, current) * (1.0 / math.sqrt(HEAD_DIM)) - slope * delta
    s = jnp.where(jnp.logical_or(current, has_previous), s, NEG_INF)
    m = jnp.maximum(jnp.max(s, axis=0, keepdims=True), sink)
    p = jnp.exp(s - m)
    e_sink = jnp.exp(sink - m)
    inv = 1.0 / (jnp.sum(p, axis=0, keepdims=True) + e_sink)
    return p * inv, e_sink * inv


def _heads_on_lanes(ref, b, g, group):
    first = g * group
    return jnp.concatenate([ref[b, (first + hh) * HEAD_DIM:(first + hh + 1) * HEAD_DIM, :] for hh in range(group)],
                           axis=1)


def _fill_kv(kv_scr, halo, tile, tm):
    for j in range(2 * N_KV_HEADS):
        kv_scr[j, 0:BLOCK] = halo[:, j * HEAD_DIM:(j + 1) * HEAD_DIM]
        kv_scr[j, BLOCK:BLOCK + tm] = tile[:, j * HEAD_DIM:(j + 1) * HEAD_DIM]


def _cols(d, tm):
    return pl.BlockSpec((d, tm), lambda i: (0, i))


def _fwd_attn(x, wqt, bqt, wkv, bkv, sinks, wo, bo, mixg, mixb, tm):
    t, d = x.shape
    nq = d // HEAD_DIM
    group = nq // N_KV_HEADS
    nb = tm // BLOCK

    def body(sink_ref, x_ref, xh_ref, wqt_ref, bqt_ref, wkv_ref, bkv_ref, wo_ref, bo_ref, mg_ref, mb_ref,
             qt_ref, kv_ref, ot_ref, z_ref, y_ref, kv_scr, qt_scr, ot_scr):
        i = pl.program_id(0)
        xv = x_ref[...]
        xb = xv.astype(BF16)
        qt = (_dot_nt(wqt_ref[...], xb) + bqt_ref[...]).astype(BF16)
        qt_ref[...] = qt
        for b in range(nb):
            qt_scr[b] = qt[:, b * BLOCK:(b + 1) * BLOCK]
        kvb = (_dot(xb, wkv_ref[...]) + bkv_ref[...]).astype(BF16)
        kv_ref[...] = kvb
        _fill_kv(kv_scr, (_dot(xh_ref[...].astype(BF16), wkv_ref[...]) + bkv_ref[...]).astype(BF16), kvb, tm)
        band = _attn_band()

        def block(b, carry):
            r0 = pl.multiple_of(b * BLOCK, BLOCK)
            has_previous = jnp.logical_or(i > 0, b > 0)
            for g in range(N_KV_HEADS):
                kk = kv_scr[g, pl.ds(r0, 2 * BLOCK), :]
                vv = kv_scr[N_KV_HEADS + g, pl.ds(r0, 2 * BLOCK), :]
                s_all = _dot(kk, _heads_on_lanes(qt_scr, b, g, group))
                probs = []
                for hh in range(group):
                    h = g * group + hh
                    p, _ = _softmax_with_sink(s_all[:, hh * BLOCK:(hh + 1) * BLOCK], _slope(h, nq), band,
                                              has_previous, sink_ref[h])
                    probs.append(_unfold(p.astype(BF16), band[0]))
                o_all = _dot_tn(vv, jnp.concatenate(probs, axis=1))
                for hh in range(group):
                    h = g * group + hh
                    ot_scr[b, h * HEAD_DIM:(h + 1) * HEAD_DIM, :] = o_all[:, hh * BLOCK:(hh + 1) * BLOCK].astype(BF16)
            return carry

        lax.fori_loop(0, nb, block, 0)
        ot = jnp.concatenate([ot_scr[b] for b in range(nb)], axis=1)
        ot_ref[...] = ot
        z = ALPHA * xv + _dot_tn(ot, wo_ref[...]) + bo_ref[...]
        z_ref[...] = z
        y_ref[...] = _ln_fwd(z, mg_ref[...], mb_ref[...])

    hb = tm // BLOCK
    vec = _const((1, d))
    return pl.pallas_call(
        body, name="fwd_attn", grid=(t // tm,),
        in_specs=[pl.BlockSpec(memory_space=pltpu.SMEM),
                  _rows(tm, d), pl.BlockSpec((BLOCK, d), lambda i: (jnp.maximum(i * hb - 1, 0), 0)),
                  _const((d, d)), _const((d, 1)), _const((d, 2 * KVD)), _const((1, 2 * KVD)), _const((d, d)), vec, vec,
                  vec],
        out_specs=[_cols(d, tm), _rows(tm, 2 * KVD), _cols(d, tm), _rows(tm, d), _rows(tm, d)],
        out_shape=[jax.ShapeDtypeStruct((d, t), BF16), jax.ShapeDtypeStruct((t, 2 * KVD), BF16),
                   jax.ShapeDtypeStruct((d, t), BF16), jax.ShapeDtypeStruct((t, d), F32),
                   jax.ShapeDtypeStruct((t, d), F32)],
        scratch_shapes=[pltpu.VMEM((2 * N_KV_HEADS, tm + BLOCK, HEAD_DIM), BF16), pltpu.VMEM((nb, d, BLOCK), BF16),
                        pltpu.VMEM((nb, d, BLOCK), BF16)],
        compiler_params=_cparams(),
    )(sinks, x, x, wqt, bqt, wkv, bkv, wo, bo, mixg, mixb)


def _write_sums(i, nt, pairs):
    @pl.when(i == nt - 1)
    def _():
        for out_ref, acc in pairs:
            out_ref[...] = jnp.sum(acc[...], axis=0, keepdims=True)


def _bwd_ffn_dx(dz, act, bm, wg, wu, wd, layer, z_in, g_in, tm, riders=()):
    t, d = dz.shape
    fs = wg.shape[-1]
    nt = t // tm

    def body(dz_ref, act_ref, bm_ref, wg_ref, wu_ref, wd_ref, zin_ref, gin_ref,
             dgg_ref, duu_ref, dzin_ref, dg_ref, db_ref, dsum_ref, acc_g, acc_b, acc_s):
        i = pl.program_id(0)
        _acc_init(i, acc_g, acc_b, acc_s)
        dzv = dz_ref[...]
        dzb = dzv.astype(BF16)
        dx = ALPHA * dzv
        for j in range(N_CHIPS):
            dh = _dot_nt(dzb, wd_ref[j])
            dgb = (dh * bm_ref[j].astype(F32)).astype(BF16)
            dub = (dh * act_ref[j].astype(F32)).astype(BF16)
            dgg_ref[j] = dgb
            duu_ref[j] = dub
            dx = dx + _dot_nt(dgb, wg_ref[j]) + _dot_nt(dub, wu_ref[j])
        dz_in, zhat = _ln_bwd(dx, zin_ref[...], gin_ref[...])
        acc_g[...] += _colsum8(dx * zhat)
        acc_b[...] += _colsum8(dx)
        acc_s[...] += _colsum8(dz_in)
        dzin_ref[...] = dz_in
        _write_sums(i, nt, [(dg_ref, acc_g), (db_ref, acc_b), (dsum_ref, acc_s)])

    wcol = pl.BlockSpec((N_CHIPS, None, d, fs), lambda i: (0, layer, 0, 0), pipeline_mode=pl.Buffered(1))
    wrow = pl.BlockSpec((N_CHIPS, None, fs, d), lambda i: (0, layer, 0, 0), pipeline_mode=pl.Buffered(1))
    hid = pl.BlockSpec((N_CHIPS, tm, fs), lambda i: (0, i, 0))
    return _tc_call(
        body, name=f"bwd_ffn_dx{layer}", nt=nt,
        in_specs=[_rows(tm, d), hid, hid, wcol, wcol, wrow, _rows(tm, d), _const((1, d))],
        out_specs=[hid, hid, _rows(tm, d)] + [_acc_out((1, d))] * 3,
        out_shape=[jax.ShapeDtypeStruct((N_CHIPS, t, fs), BF16)] * 2 + [jax.ShapeDtypeStruct((t, d), F32)]
        + [jax.ShapeDtypeStruct((1, d), F32)] * 3,
        scratch_shapes=[pltpu.VMEM((8, d), F32)] * 3,
        operands=(dz, act, bm, wg, wu, wd, z_in, g_in), riders=riders)


def _matmul_tn(a, b, tt, name, riders=None):
    ja, t, ka = a.shape
    jb, _, nb = b.shape
    nj = max(ja, jb)

    def body(a_ref, b_ref, o_ref):
        @pl.when(pl.program_id(0) == 0)
        def _():
            o_ref[...] = jnp.zeros_like(o_ref)

        a0 = a_ref[0].astype(BF16) if ja == 1 else None
        b0 = b_ref[0].astype(BF16) if jb == 1 else None
        for j in range(nj):
            aj = a0 if ja == 1 else a_ref[j].astype(BF16)
            bj = b0 if jb == 1 else b_ref[j].astype(BF16)
            o_ref[j] += _dot_tn(aj, bj)

    (out,), rider_outs = _tc_call(
        body, name=name, nt=t // tt,
        in_specs=[pl.BlockSpec((ja, tt, ka), lambda i: (0, i, 0)), pl.BlockSpec((jb, tt, nb), lambda i: (0, i, 0))],
        out_specs=[pl.BlockSpec((nj, ka, nb), lambda i: (0, 0, 0))],
        out_shape=[jax.ShapeDtypeStruct((nj, ka, nb), F32)], operands=(a, b), riders=riders or ())
    return out if riders is None else (out, rider_outs)


def _matmul_nn(at, b, tt, name):
    ka, t = at.shape
    nb = b.shape[1]

    def body(a_ref, b_ref, o_ref):
        @pl.when(pl.program_id(0) == 0)
        def _():
            o_ref[...] = jnp.zeros_like(o_ref)

        o_ref[...] += _dot(a_ref[...].astype(BF16), b_ref[...].astype(BF16))

    return pl.pallas_call(
        body, name=name, grid=(t // tt,),
        in_specs=[pl.BlockSpec((ka, tt), lambda i: (0, i)), pl.BlockSpec((tt, nb), lambda i: (i, 0))],
        out_specs=pl.BlockSpec((ka, nb), lambda i: (0, 0)), out_shape=jax.ShapeDtypeStruct((ka, nb), F32),
        compiler_params=_cparams(1),
    )(at, b)


def _bwd_attn(dz_all, qt, kv, sinks, wo, wqt, wkv, z_in, g_in, tm, riders=()):
    t, d = dz_all.shape
    nq = d // HEAD_DIM
    group = nq // N_KV_HEADS
    nb = tm // BLOCK
    nt = t // tm
    hb = tm // BLOCK
    n_kv = 2 * N_KV_HEADS

    def body(sink_ref, dz_ref, qt_ref, kv_ref, kvh_ref, wo_ref, wqt_ref, wkv_ref, zin_ref, gin_ref,
             dqt_ref, dkv_ref, dx_ref, dbq_ref, dbkv_ref, dsink_ref, ding_ref, dinb_ref,
             kv_scr, dkv_scr, qt_scr, dot_scr, dqt_scr, carry, acc_q, acc_kv, acc_s, acc_ig, acc_ib):
        i = pl.program_id(0)
        ti = nt - 1 - i
        _acc_init(i, carry, acc_q, acc_kv, acc_s, acc_ig, acc_ib)
        dz = dz_ref[...]
        do_t = _dot_nt(wo_ref[...], dz.astype(BF16)).astype(BF16)
        for b in range(nb):
            dot_scr[b] = do_t[:, b * BLOCK:(b + 1) * BLOCK]
            qt_scr[b] = qt_ref[:, b * BLOCK:(b + 1) * BLOCK]
        _fill_kv(kv_scr, kvh_ref[...], kv_ref[...], tm)
        dkv_scr[:, 0:tm] = jnp.zeros((n_kv, tm, HEAD_DIM), F32)
        dkv_scr[:, tm:tm + BLOCK] = carry[...]
        band = _attn_band()

        def block(b, c):
            r0 = pl.multiple_of(b * BLOCK, BLOCK)
            has_previous = jnp.logical_or(ti > 0, b > 0)
            for g in range(N_KV_HEADS):
                kk = kv_scr[g, pl.ds(r0, 2 * BLOCK), :]
                vv = kv_scr[N_KV_HEADS + g, pl.ds(r0, 2 * BLOCK), :]
                q_all = _heads_on_lanes(qt_scr, b, g, group)
                do_all = _heads_on_lanes(dot_scr, b, g, group)
                s_all = _dot(kk, q_all)
                dp_all = _dot(vv, do_all)
                probs, dscores = [], []
                for hh in range(group):
                    h = g * group + hh
                    cols = slice(hh * BLOCK, (hh + 1) * BLOCK)
                    p, p_sink = _softmax_with_sink(s_all[:, cols], _slope(h, nq), band, has_previous, sink_ref[h])
                    dp = _fold(dp_all[:, cols], band[0])
                    rs = jnp.sum(p * dp, axis=0, keepdims=True)
                    acc_s[h:h + 1, :] += -(p_sink * rs)
                    ds = p * (dp - rs) * (1.0 / math.sqrt(HEAD_DIM))
                    probs.append(_unfold(p.astype(BF16), band[0]))
                    dscores.append(_unfold(ds.astype(BF16), band[0]))
                p_all = jnp.concatenate(probs, axis=1)
                ds_all = jnp.concatenate(dscores, axis=1)
                dq_all = _dot_tn(kk, ds_all)
                for hh in range(group):
                    h = g * group + hh
                    dqt_scr[b, h * HEAD_DIM:(h + 1) * HEAD_DIM, :] = dq_all[:, hh * BLOCK:(hh + 1) * BLOCK]
                dkv_scr[g, pl.ds(r0, 2 * BLOCK), :] += _dot_nt(ds_all, q_all)
                dkv_scr[N_KV_HEADS + g, pl.ds(r0, 2 * BLOCK), :] += _dot_nt(p_all, do_all)
            return c

        lax.fori_loop(0, nb, block, 0)
        carry[...] = dkv_scr[:, 0:BLOCK]
        dkv = jnp.concatenate([dkv_scr[j, BLOCK:BLOCK + tm] for j in range(n_kv)], axis=1)
        acc_kv[...] += _colsum8(dkv)
        dkvb = dkv.astype(BF16)
        dkv_ref[...] = dkvb
        dqt = jnp.concatenate([dqt_scr[b] for b in range(nb)], axis=1)
        for b in range(nb):
            acc_q[...] += dqt_scr[b]
        dqtb = dqt.astype(BF16)
        dqt_ref[...] = dqtb
        dx = ALPHA * dz + _dot_tn(dqtb, wqt_ref[...]) + _dot_nt(dkvb, wkv_ref[...])
        dz_in, zhat_in = _ln_bwd(dx, zin_ref[...], gin_ref[...])
        acc_ig[...] += _colsum8(dx * zhat_in)
        acc_ib[...] += _colsum8(dx)
        dx_ref[...] = dz_in
        _write_sums(i, nt, [(dbkv_ref, acc_kv), (ding_ref, acc_ig), (dinb_ref, acc_ib)])

        @pl.when(i == nt - 1)
        def _():
            dbq_ref[...] = jnp.sum(acc_q[...], axis=1, keepdims=True)
            dsink_ref[...] = jnp.sum(acc_s[...], axis=1, keepdims=True)

    rev = lambda w: pl.BlockSpec((tm, w), lambda i: (nt - 1 - i, 0))
    rev_cols = pl.BlockSpec((d, tm), lambda i: (0, nt - 1 - i))
    vec = _const((1, d))
    return _tc_call(
        body, name="bwd_attn", nt=nt,
        in_specs=[pl.BlockSpec(memory_space=pltpu.SMEM), rev(d), rev_cols, rev(2 * KVD),
                  pl.BlockSpec((BLOCK, 2 * KVD), lambda i: (jnp.maximum((nt - 1 - i) * hb - 1, 0), 0)),
                  _const((d, d)), _const((d, d)), _const((d, 2 * KVD)), rev(d), vec],
        out_specs=[rev_cols, rev(2 * KVD), rev(d), _acc_out((d, 1)), _acc_out((1, 2 * KVD)), _acc_out((nq, 1)),
                   _acc_out((1, d)), _acc_out((1, d))],
        out_shape=[jax.ShapeDtypeStruct((d, t), BF16), jax.ShapeDtypeStruct((t, 2 * KVD), BF16),
                   jax.ShapeDtypeStruct((t, d), F32), jax.ShapeDtypeStruct((d, 1), F32),
                   jax.ShapeDtypeStruct((1, 2 * KVD), F32), jax.ShapeDtypeStruct((nq, 1), F32)]
        + [jax.ShapeDtypeStruct((1, d), F32)] * 2,
        scratch_shapes=[pltpu.VMEM((n_kv, tm + BLOCK, HEAD_DIM), BF16), pltpu.VMEM((n_kv, tm + BLOCK, HEAD_DIM), F32),
                        pltpu.VMEM((nb, d, BLOCK), BF16), pltpu.VMEM((nb, d, BLOCK), BF16),
                        pltpu.VMEM((nb, d, BLOCK), F32), pltpu.VMEM((n_kv, BLOCK, HEAD_DIM), F32),
                        pltpu.VMEM((d, BLOCK), F32), pltpu.VMEM((8, 2 * KVD), F32), pltpu.VMEM((nq, BLOCK), F32),
                        pltpu.VMEM((8, d), F32), pltpu.VMEM((8, d), F32)],
        operands=(sinks, dz_all, qt, kv, kv, wo, wqt, wkv, z_in, g_in), riders=riders)


def _bwd_conv_head(dz, c, w2, lng, lnb, tm, riders=()):
    t, d = dz.shape
    nt = t // tm

    def body(dz_ref, c_ref, w2_ref, lg_ref, lb_ref, s_ref, dc_ref, dlg_ref, dlb_ref, a3, a4):
        i = pl.program_id(0)
        _acc_init(i, a3, a4)
        dz = dz_ref[...]
        chat, rstd = _ln_stats(c_ref[...])
        n = chat * lg_ref[...] + lb_ref[...]
        act, dact = _silu_and_grad(n)
        s_ref[...] = act.astype(BF16)
        dn = _dot_nt(dz.astype(BF16), w2_ref[...]) * dact
        a3[...] += _colsum8(dn * chat)
        a4[...] += _colsum8(dn)
        dch = dn * lg_ref[...]
        m1 = jnp.mean(dch, axis=-1, keepdims=True)
        m2 = jnp.mean(dch * chat, axis=-1, keepdims=True)
        dc_ref[...] = rstd * (dch - m1 - chat * m2)
        _write_sums(i, nt, [(dlg_ref, a3), (dlb_ref, a4)])

    vec = _const((1, d))
    return _tc_call(
        body, name="bwd_conv_head", nt=nt,
        in_specs=[_rows(tm, d), _rows(tm, d), _const((d, d)), vec, vec],
        out_specs=[_rows(tm, d), _rows(tm, d)] + [_acc_out((1, d))] * 2,
        out_shape=[jax.ShapeDtypeStruct((t, d), BF16), jax.ShapeDtypeStruct((t, d), F32)]
        + [jax.ShapeDtypeStruct((1, d), F32)] * 2,
        scratch_shapes=[pltpu.VMEM((8, d), F32)] * 2, operands=(dz, c, w2, lng, lnb), riders=riders)


def _bwd_conv_glu(dc, u, a, g, dz, wdw, w1s, tm, riders=()):
    t, d = dc.shape
    dh_w = d // 2
    nt = t // tm
    hb = tm // CONV_HALO
    last_halo = t // CONV_HALO - 1

    def body(dc_ref, dcn_ref, u_ref, a_ref, g_ref, dz_ref, w_ref, w1_ref,
             dx_ref, dh_ref, db1_ref, dbdw_ref, dw_ref, ext, sh, du_scr, acc_b1, acc_bdw, acc_w):
        i = pl.program_id(0)
        _acc_init(i, acc_b1, acc_bdw, acc_w)
        dcv = dc_ref[...]
        acc_bdw[...] += _colsum8(dcv)

        ext[0:tm] = dcv
        ext[tm:tm + CONV_HALO] = jnp.where(i == nt - 1, 0.0, dcn_ref[...])
        ext[tm + CONV_HALO:tm + CONV_HALO + 8] = jnp.zeros((8, d), F32)
        _fill_shifted(sh, ext)

        def du_chunk(r, carry):
            base = pl.multiple_of(r * CONV_CHUNK, CONV_CHUNK)
            _tap_sum(w_ref, sh, base, d, lambda k: CONV_WIDTH - 1 - k, du_scr)
            return carry

        lax.fori_loop(0, tm // CONV_CHUNK, du_chunk, 0)

        def dw_chunk(r, carry):
            base = pl.multiple_of(r * CONV_CHUNK, CONV_CHUNK)
            groups = CONV_CHUNK // 8
            for lg in range(d // LANES):
                ls = slice(lg * LANES, (lg + 1) * LANES)
                uv = u_ref[pl.ds(base, CONV_CHUNK), ls].reshape(groups, 8, LANES)
                for k in range(CONV_WIDTH):
                    e = CONV_WIDTH - 1 - k
                    x = sh[e % 8, pl.ds(base + (e // 8) * 8, CONV_CHUNK), ls].reshape(groups, 8, LANES)
                    acc_w[k, :, ls] += jnp.sum(uv * x, axis=0)
            return carry

        lax.fori_loop(0, tm // CONV_CHUNK, dw_chunk, 0)

        du = du_scr[...]
        av = a_ref[...].astype(F32)
        sg = jax.nn.sigmoid(g_ref[...].astype(F32))
        da = du * sg
        dg = du * av * sg * (1.0 - sg)
        acc_b1[:, 0:d] += _colsum8(da)
        acc_b1[:, d:2 * d] += _colsum8(dg)
        dx = ALPHA * dz_ref[...]
        for j, part in enumerate([da[:, 0:dh_w], da[:, dh_w:d], dg[:, 0:dh_w], dg[:, dh_w:d]]):
            pb = part.astype(BF16)
            dh_ref[j] = pb
            dx = dx + _dot_nt(pb, w1_ref[j])
        dx_ref[...] = dx

        @pl.when(i == nt - 1)
        def _():
            db1_ref[...] = jnp.sum(acc_b1[...], axis=0, keepdims=True)
            dbdw_ref[...] = jnp.sum(acc_bdw[...], axis=0, keepdims=True)
            dw_ref[...] = jnp.sum(acc_w[...], axis=1)

    return _tc_call(
        body, name="bwd_conv_glu", nt=nt,
        in_specs=[_rows(tm, d), pl.BlockSpec((CONV_HALO, d), lambda i: (jnp.minimum((i + 1) * hb, last_halo), 0)),
                  _rows(tm, d), _rows(tm, d), _rows(tm, d), _rows(tm, d), _const((CONV_HALO, 8, d)),
                  _const((4, d, dh_w))],
        out_specs=[_rows(tm, d), pl.BlockSpec((4, tm, dh_w), lambda i: (0, i, 0)), _acc_out((1, 2 * d)),
                   _acc_out((1, d)), _acc_out((CONV_HALO, d))],
        out_shape=[jax.ShapeDtypeStruct((t, d), F32), jax.ShapeDtypeStruct((4, t, dh_w), BF16),
                   jax.ShapeDtypeStruct((1, 2 * d), F32), jax.ShapeDtypeStruct((1, d), F32),
                   jax.ShapeDtypeStruct((CONV_HALO, d), F32)],
        scratch_shapes=[pltpu.VMEM((tm + CONV_HALO + 8, d), F32), pltpu.VMEM((8, tm + CONV_HALO, d), F32),
                        pltpu.VMEM((tm, d), F32), pltpu.VMEM((8, 2 * d), F32), pltpu.VMEM((8, d), F32),
                        pltpu.VMEM((CONV_HALO, 8, d), F32)],
        operands=(dc, dc, u, a, g, dz, wdw, w1s), riders=riders)


def _row_block(rows, target):
    best = rows
    for cand in range(8, min(rows, target) + 1, 8):
        if rows % cand == 0:
            best = cand
    return best


def _adamw_update(w_ref, g_ref, m_ref, v_ref, d_ref, nm_ref, nv_ref):
    gv = g_ref[...]
    nm = ADAM_B1 * m_ref[...] + (1.0 - ADAM_B1) * gv
    nv = ADAM_B2 * v_ref[...] + (1.0 - ADAM_B2) * (gv * gv)
    m_hat = nm / (1.0 - ADAM_B1 ** ADAM_STEP)
    v_hat = nv / (1.0 - ADAM_B2 ** ADAM_STEP)
    d_ref[...] = -ADAM_LR * (m_hat / (jnp.sqrt(v_hat) + ADAM_EPS) + ADAM_WD * w_ref[...])
    nm_ref[...] = nm
    nv_ref[...] = nv


ADAMW_STEPS = 8


def _adamw(params, name, riders=()):
    n = len(params)
    steps = ADAMW_STEPS if all(p[0].shape[0] % (8 * ADAMW_STEPS) == 0 for p in params) else 1

    def body(*refs):
        for k in range(n):
            _adamw_update(*refs[4 * k:4 * k + 4], *refs[4 * n + 3 * k:4 * n + 3 * k + 3])

    specs = [pl.BlockSpec((p[0].shape[0] // steps, p[0].shape[1]), lambda i: (i, 0)) for p in params]
    outs, rider_outs = _tc_call(
        body, name=name, nt=steps, in_specs=[s for s in specs for _ in range(4)],
        out_specs=[s for s in specs for _ in range(3)],
        out_shape=[jax.ShapeDtypeStruct(p[0].shape, F32) for p in params for _ in range(3)],
        operands=[a for p in params for a in p], riders=riders)
    return [tuple(outs[3 * k:3 * k + 3]) for k in range(n)], rider_outs


def _pad_to(v, n):
    return jnp.pad(v, (0, n - v.shape[0]))


def _round_up(n, m):
    return (n + m - 1) // m * m


def kernel(x, conv_w_pw1, conv_b_pw1, conv_w_dw, conv_b_dw, conv_ln_g, conv_ln_b, conv_w_pw2, conv_b_pw2, kv_w_k, kv_b_k, kv_w_v, kv_b_v, attn_w_q, attn_b_q, attn_sinks, attn_w_o, attn_b_o, ffn_w_gate, ffn_w_up, ffn_w_down, ln_mix_g, ln_mix_b, ln_ffn_g, ln_ffn_b, loss_target, m_conv_w_pw1, m_conv_b_pw1, m_conv_w_dw, m_conv_b_dw, m_conv_ln_g, m_conv_ln_b, m_conv_w_pw2, m_conv_b_pw2, m_kv_w_k, m_kv_b_k, m_kv_w_v, m_kv_b_v, m_attn_w_q, m_attn_b_q, m_attn_sinks, m_attn_w_o, m_attn_b_o, m_ffn_w_gate, m_ffn_w_up, m_ffn_w_down, m_ln_mix_g, m_ln_mix_b, m_ln_ffn_g, m_ln_ffn_b, v_conv_w_pw1, v_conv_b_pw1, v_conv_w_dw, v_conv_b_dw, v_conv_ln_g, v_conv_ln_b, v_conv_w_pw2, v_conv_b_pw2, v_kv_w_k, v_kv_b_k, v_kv_w_v, v_kv_b_v, v_attn_w_q, v_attn_b_q, v_attn_sinks, v_attn_w_o, v_attn_b_o, v_ffn_w_gate, v_ffn_w_up, v_ffn_w_down, v_ln_mix_g, v_ln_mix_b, v_ln_ffn_g, v_ln_ffn_b):
    args = dict(locals())
    w = {n: args[n] for n in WEIGHTS}
    mom = {n: args["m_" + n] for n in WEIGHTS}
    var = {n: args["v_" + n] for n in WEIGHTS}
    assert x.shape[0] == 1, "one sequence per device"
    t, d = x.shape[1], x.shape[2]
    dq = d // 4
    fs = ffn_w_gate.shape[-1]
    nq = d // HEAD_DIM
    x0 = x.reshape(t, d)
    target = loss_target.reshape(t, d)
    tm_big = min(512, t)
    tm_mid = min(256, t)
    tm_tn = min(1024, t)
    c_idx = lax.axis_index("c")

    me_idx = 2 * lax.axis_index("x") + lax.axis_index("y")

    def gather_buffer(v):
        buf = lax.empty((N_CHIPS,) + v.shape, v.dtype)
        return lax.dynamic_update_slice(buf, v[None], (me_idx,) + (0,) * v.ndim)

    def halves(v):
        return v.reshape(2, -1, v.shape[-1])

    small_sizes = [int(w[n].size) for n in SMALL_SHARDED]
    rs = _round_up(sum(small_sizes), 8 * 128) // 128
    spack = _pad_to(jnp.concatenate([w[n].reshape(-1) for n in SMALL_SHARDED]), rs * 128).reshape(rs, 128)
    conv_first = ['conv_w_pw1', 'conv_w_pw2']
    later = [n for n in BIG if n not in conv_first]
    (first_out,) = _run_riders(
        [_all_gather_rider([gather_buffer(halves(w[n].astype(BF16))) for n in conv_first], gather_buffer(spack))],
        "all_gather_conv")
    later_rider = _all_gather_rider([gather_buffer(halves(w[n].astype(BF16))) for n in later])
    gs = first_out[-1].reshape(N_CHIPS, rs * 128)
    full = {n: g.reshape((N_CHIPS,) + w[n].shape) for n, g in zip(conv_first, first_out)}
    off = 0
    for n, size in zip(SMALL_SHARDED, small_sizes):
        full[n] = gs[:, off:off + size].reshape((N_CHIPS,) + w[n].shape)
        off += size
    w1s = full['conv_w_pw1'].reshape(N_CHIPS, d, d // 2)
    w2 = full['conv_w_pw2'].reshape(d, d)
    b1 = full['conv_b_pw1'].reshape(1, 2 * d)
    wdw = jnp.pad(full['conv_w_dw'].reshape(N_CHIPS, CONV_WIDTH, dq).transpose(1, 0, 2).reshape(CONV_WIDTH, d),
                  ((0, CONV_HALO - CONV_WIDTH), (0, 0)))
    wdw = jnp.broadcast_to(wdw[:, None, :], (CONV_HALO, 8, d))
    bdw = full['conv_b_dw'].reshape(1, d)
    clng = full['conv_ln_g'].reshape(1, d)
    clnb = full['conv_ln_b'].reshape(1, d)
    b2 = full['conv_b_pw2'].reshape(1, d)
    bkv = jnp.concatenate([kv_b_k, kv_b_v]).reshape(1, 2 * KVD)
    sinks = attn_sinks.reshape(nq)
    mixg = [ln_mix_g[l].reshape(1, d) for l in range(DEPTH)]
    mixb = [ln_mix_b[l].reshape(1, d) for l in range(DEPTH)]
    ffng = [ln_ffn_g[l].reshape(1, d) for l in range(DEPTH)]
    ffnb = [ln_ffn_b[l].reshape(1, d) for l in range(DEPTH)]

    a_act, g_act, u_act = _fwd_pw1_glu(x0, w1s, b1, tm_big)
    (c_act, z1, x1), (later_out,) = _fwd_conv_tail(u_act, x0, wdw, bdw, clng, clnb, w2, b2, mixg[0], mixb[0], tm_mid,
                                                   riders=[later_rider])
    full.update({n: g.reshape((N_CHIPS,) + w[n].shape) for n, g in zip(later, later_out)})
    wkv = jnp.concatenate([full['kv_w_k'].reshape(d, KVD), full['kv_w_v'].reshape(d, KVD)], axis=1)
    wqt = full['attn_w_q'].reshape(d, d).T
    wo = full['attn_w_o'].reshape(d, d)
    wg, wu, wd = full['ffn_w_gate'], full['ffn_w_up'], full['ffn_w_down']
    act0, bm0, hm0, z2, x2 = _fwd_ffn(x1, wg, wu, wd, 0, ffng[0], ffnb[0], tm_big)
    qt_act, kv_act, ot_act, z3, x3 = _fwd_attn(x2, wqt, attn_b_q.reshape(d, 1), wkv, bkv, sinks, wo, attn_b_o,
                                               mixg[1], mixb[1], tm_big)
    act1, bm1, hm1, dz4, d_fg1, d_fb1, loss_part = _fwd_ffn(x3, wg, wu, wd, 1, ffng[1], ffnb[1], tm_big, target=target)
    loss = lax.psum(loss_part[0, 0], ("x", "y", "c"))

    c_arr = c_idx.reshape(1).astype(jnp.int32)

    def halves4(v):
        return v.reshape(N_CHIPS, 2, -1, v.shape[-1])

    def arrays(group):
        return [p for _, p in group]

    def pair_sums(group, got):
        return [_pair_sum(p, g, c_arr, "grad_pair_sum_" + n) for (n, p), g in zip(group, got)]

    pos_arr = jnp.stack([me_idx, c_idx]).astype(jnp.int32)

    def chip_sums(group, sums, got):
        return [_chip_sum(s, g, pos_arr, "grad_chip_sum_" + n) for (n, _), s, g in zip(group, sums, got)]

    (dgg1, duu1, dz3, d_mg1, d_mb1, d_bo), _ = _bwd_ffn_dx(dz4, act1, bm1, wg, wu, wd, 1, z3, mixg[1], tm_big)
    g1 = [("ffn_w_gate1", halves4(_matmul_tn(x3[None], dgg1, tm_tn, "dw_gate1"))),
          ("ffn_w_up1", halves4(_matmul_tn(x3[None], duu1, tm_tn, "dw_up1"))),
          ("ffn_w_down1", halves4(_matmul_tn(hm1, dz4[None], tm_tn, "dw_down1")))]
    (dqt, dkv, dz2, d_bq, d_bkv, d_sinks, d_fg0, d_fb0), (got1,) = _bwd_attn(
        dz3, qt_act, kv_act, sinks, wo, wqt, wkv, z2, ffng[0], tm_big, riders=[_pair_exchange_rider(arrays(g1))])
    s1 = pair_sums(g1, got1)
    dwo = _matmul_nn(ot_act, dz3, tm_tn, "dw_o")
    dwq = _matmul_nn(dqt, x2, tm_tn, "dw_q").T
    dwkv = _matmul_tn(x2[None], dkv[None], tm_tn, "dw_kv")[0]
    g2 = [("attn_w_o", halves4(dwo)), ("attn_w_q", halves4(dwq)),
          ("kv_w_k", halves4(dwkv[:, 0:KVD])), ("kv_w_v", halves4(dwkv[:, KVD:2 * KVD]))]
    (dgg0, duu0, dz1, d_mg0, d_mb0, d_b2), (from_chips1, got2) = _bwd_ffn_dx(
        dz2, act0, bm0, wg, wu, wd, 0, z1, mixg[0], tm_big,
        riders=[_chip_scatter_rider(s1), _pair_exchange_rider(arrays(g2))])
    f1 = chip_sums(g1, s1, from_chips1)
    s2 = pair_sums(g2, got2)
    g3 = [("ffn_w_gate0", halves4(_matmul_tn(x1[None], dgg0, tm_tn, "dw_gate0"))),
          ("ffn_w_up0", halves4(_matmul_tn(x1[None], duu0, tm_tn, "dw_up0"))),
          ("ffn_w_down0", halves4(_matmul_tn(hm0, dz2[None], tm_tn, "dw_down0")))]
    (s_act, dc, d_clng, d_clnb), (got3, shared1) = _bwd_conv_head(
        dz1, c_act, w2, clng, clnb, tm_mid, riders=[_pair_exchange_rider(arrays(g3)), _pair_share_rider(f1)])
    s3 = pair_sums(g3, got3)
    dw2 = _matmul_tn(s_act[None], dz1[None], tm_tn, "dw_pw2")
    (dx0, dh1, d_b1, d_bdw, d_wdw), (from_chips2, from_chips3) = _bwd_conv_glu(
        dc, u_act, a_act, g_act, dz1, wdw, w1s, tm_mid, riders=[_chip_scatter_rider(s2), _chip_scatter_rider(s3)])
    f2 = chip_sums(g2, s2, from_chips2)
    f3 = chip_sums(g3, s3, from_chips3)
    dw1, (shared23,) = _matmul_tn(x0[None], dh1, tm_tn, "dw_pw1", riders=[_pair_share_rider(f2 + f3)])

    def rows4(v):
        return v.reshape(N_CHIPS, -1)

    def rep4(v):
        return jnp.broadcast_to(v.reshape(1, -1), (N_CHIPS, v.size))

    local = {
        'conv_b_pw1': rows4(d_b1),
        'conv_w_dw': rows4(d_wdw[0:CONV_WIDTH].reshape(CONV_WIDTH, N_CHIPS, dq).transpose(1, 0, 2)),
        'conv_b_dw': rows4(d_bdw), 'conv_ln_g': rows4(d_clng), 'conv_ln_b': rows4(d_clnb), 'conv_b_pw2': rows4(d_b2),
        'kv_b_k': rep4(d_bkv[:, 0:KVD]), 'kv_b_v': rep4(d_bkv[:, KVD:2 * KVD]), 'attn_b_q': rep4(d_bq),
        'attn_sinks': rep4(d_sinks), 'attn_b_o': rep4(d_bo),
        'ln_mix_g': rep4(jnp.concatenate([d_mg0, d_mg1])), 'ln_mix_b': rep4(jnp.concatenate([d_mb0, d_mb1])),
        'ln_ffn_g': rep4(jnp.concatenate([d_fg0, d_fg1])), 'ln_ffn_b': rep4(jnp.concatenate([d_fb0, d_fb1])),
    }
    n_small = sum(int(w[n].size) for n in SMALL)
    small_rows = _round_up(n_small, 2 * 8 * 128) // 128
    small_local = jnp.concatenate([local[n] for n in SMALL], axis=1)
    small_local = jnp.pad(small_local, ((0, 0), (0, small_rows * 128 - n_small)))
    g4 = [("conv_w_pw1", halves4(dw1)), ("conv_w_pw2", halves4(dw2)),
          ("small", small_local.reshape(N_CHIPS, 2, small_rows // 2, 128))]
    (got4,) = _run_riders([_pair_exchange_rider(arrays(g4))], "grad_pair_exchange_last")
    s4 = pair_sums(g4, got4)
    reduced = dict(zip([n for n, _ in g1], shared1))
    reduced.update(zip([n for n, _ in g2 + g3], shared23))
    for n in ('ffn_w_gate', 'ffn_w_up', 'ffn_w_down'):
        reduced[n] = jnp.stack([reduced[n + str(layer)].reshape(w[n].shape[1:]) for layer in range(DEPTH)])

    g_out, delta, new_m, new_v = {}, {}, {}, {}

    def adamw_matrices(names, name, riders=()):
        for n in names:
            g_out[n] = reduced[n].reshape(w[n].shape)
        two_d = [tuple(tree[n].reshape(-1, w[n].shape[-1]) for tree in (w, g_out, mom, var)) for n in names]
        results, rider_outs = _adamw(two_d, name, riders)
        for n, (dl, nm, nv) in zip(names, results):
            delta[n], new_m[n], new_v[n] = (r.reshape(w[n].shape) for r in (dl, nm, nv))
        return rider_outs

    (from_chips4,) = adamw_matrices([n for n in BIG if n not in conv_first], "adamw_attn_ffn",
                                    riders=[_chip_scatter_rider(s4)])
    f4 = chip_sums(g4, s4, from_chips4)
    (shared4,) = _run_riders([_pair_share_rider(f4)], "grad_pair_share_last")
    reduced.update(zip([n for n, _ in g4], shared4))
    adamw_matrices(conv_first, "adamw_conv")

    def pack_small(tree):
        return _pad_to(jnp.concatenate([tree[n].reshape(-1) for n in SMALL]), small_rows * 128).reshape(small_rows, 128)

    g_small = reduced['small'].reshape(small_rows, 128)
    ((dl, nm, nv),), _ = _adamw([(pack_small(w), g_small, pack_small(mom), pack_small(var))], "adamw_small")
    off = 0
    for n in SMALL:
        size, shape = int(w[n].size), w[n].shape
        for tree, flat in ((g_out, g_small), (delta, dl), (new_m, nm), (new_v, nv)):
            tree[n] = flat.reshape(-1)[off:off + size].reshape(shape)
        off += size

    return (loss, dx0.reshape(x.shape), *[g_out[n] for n in WEIGHTS], *[delta[n] for n in WEIGHTS],
            *[new_m[n] for n in WEIGHTS], *[new_v[n] for n in WEIGHTS])
```

```python
import functools
import math

import jax
import jax.numpy as jnp
from jax import lax
from jax.experimental import pallas as pl
from jax.experimental.pallas import tpu as pltpu

F32 = jnp.float32
BF16 = jnp.bfloat16

DEPTH = 2
ALPHA = (2.0 * DEPTH) ** 0.25
LN_EPS = 1e-5
NEG_INF = -1e30
HEAD_DIM = 64
N_KV_HEADS = 2
KVD = N_KV_HEADS * HEAD_DIM
BLOCK = 128
CONV_WIDTH = 31
CONV_HALO = 32
ALIBI_MAX = 8.0
ADAM_LR, ADAM_B1, ADAM_B2, ADAM_EPS, ADAM_WD, ADAM_STEP = 0.001, 0.9, 0.999, 1e-08, 0.01, 10

N_CHIPS = 4
PACK_ROWS = 256
VMEM_LIMIT = 60 * 1024 * 1024
MESH = pl.DeviceIdType.MESH

NT_DIMS = (((1,), (1,)), ((), ()))
TN_DIMS = (((0,), (0,)), ((), ()))

WEIGHTS = ['conv_w_pw1', 'conv_b_pw1', 'conv_w_dw', 'conv_b_dw', 'conv_ln_g', 'conv_ln_b', 'conv_w_pw2', 'conv_b_pw2',
           'kv_w_k', 'kv_b_k', 'kv_w_v', 'kv_b_v', 'attn_w_q', 'attn_b_q', 'attn_sinks', 'attn_w_o', 'attn_b_o',
           'ffn_w_gate', 'ffn_w_up', 'ffn_w_down', 'ln_mix_g', 'ln_mix_b', 'ln_ffn_g', 'ln_ffn_b']
BIG = ['conv_w_pw1', 'conv_w_pw2', 'kv_w_k', 'kv_w_v', 'attn_w_q', 'attn_w_o', 'ffn_w_gate', 'ffn_w_up', 'ffn_w_down']
SMALL_SHARDED = ['conv_b_pw1', 'conv_w_dw', 'conv_b_dw', 'conv_ln_g', 'conv_ln_b', 'conv_b_pw2']
REPLICATED = ['kv_b_k', 'kv_b_v', 'attn_b_q', 'attn_sinks', 'attn_b_o', 'ln_mix_g', 'ln_mix_b', 'ln_ffn_g', 'ln_ffn_b']
SMALL = SMALL_SHARDED + REPLICATED


def _cparams(n_grid=1):
    return pltpu.CompilerParams(dimension_semantics=("arbitrary",) * n_grid, vmem_limit_bytes=VMEM_LIMIT)


def _rows(tm, width):
    return pl.BlockSpec((tm, width), lambda i: (i, 0))


def _const(shape):
    return pl.BlockSpec(shape, lambda *_: (0,) * len(shape), pipeline_mode=pl.Buffered(1))


def _acc_out(shape):
    return pl.BlockSpec(shape, lambda *_: (0,) * len(shape))


def _dot(a, b):
    return jnp.dot(a, b, preferred_element_type=F32)


def _dot_nt(a, b):
    return lax.dot_general(a, b, NT_DIMS, preferred_element_type=F32)


def _dot_tn(a, b):
    return lax.dot_general(a, b, TN_DIMS, preferred_element_type=F32)


def _colsum8(v):
    m, n = v.shape
    return jnp.sum(v.reshape(m // 8, 8, n), axis=0)


def _ln_stats(z):
    mu = jnp.mean(z, axis=-1, keepdims=True)
    zc = z - mu
    var = jnp.mean(zc * zc, axis=-1, keepdims=True)
    rstd = lax.rsqrt(var + LN_EPS)
    return zc * rstd, rstd


def _ln_fwd(z, g, b):
    zhat, _ = _ln_stats(z)
    return zhat * g + b


def _ln_bwd(dy, z, g):
    zhat, rstd = _ln_stats(z)
    dzh = dy * g
    m1 = jnp.mean(dzh, axis=-1, keepdims=True)
    m2 = jnp.mean(dzh * zhat, axis=-1, keepdims=True)
    return rstd * (dzh - m1 - zhat * m2), zhat


def _silu_and_grad(n):
    sg = jax.nn.sigmoid(n)
    return n * sg, sg * (1.0 + n * (1.0 - sg))


def _acc_init(i, *refs):
    @pl.when(i == 0)
    def _():
        for r in refs:
            r[...] = jnp.zeros_like(r)


def _mesh_pos():
    x, y, c = lax.axis_index("x"), lax.axis_index("y"), lax.axis_index("c")
    chips = [(1 - x, y), (x, 1 - y), (1 - x, 1 - y)]
    return x, y, c, chips


HBM_SPEC = pl.BlockSpec(memory_space=pltpu.HBM)


def _remote(src, dst, send_sems, recv_sems, k, to):
    return pltpu.make_async_remote_copy(src_ref=src, dst_ref=dst, send_sem=send_sems.at[k], recv_sem=recv_sems.at[k],
                                        device_id=to, device_id_type=MESH)


class _Rider:
    def __init__(self, operands, out_shapes, sem_shapes, start, finish, mid=None, in_place=False):
        self.operands, self.out_shapes, self.sem_shapes = list(operands), list(out_shapes), list(sem_shapes)
        self.start, self.finish, self.mid = start, finish, mid
        self.in_place = in_place


def _rider_aliases(riders, first_in, first_out):
    aliases, k_in, k_out = {}, first_in, first_out
    for r in riders:
        if r.in_place:
            aliases.update({k_in + k: k_out + k for k in range(len(r.operands))})
        k_in += len(r.operands)
        k_out += len(r.out_shapes)
    return aliases


def _split(refs, counts):
    parts, k = [], 0
    for n in counts:
        parts.append(refs[k:k + n])
        k += n
    return parts


def _rider_refs(riders, ins, outs, sems):
    return list(zip(riders, _split(ins, [len(r.operands) for r in riders]),
                    _split(outs, [len(r.out_shapes) for r in riders]),
                    _split(sems, [len(r.sem_shapes) for r in riders])))


def _tc_call(body, *, name, nt, in_specs, out_specs, out_shape, operands, scratch_shapes=(), riders=(), mid_frac=0.75):
    n_in, n_out, n_scr = len(in_specs), len(out_specs), len(scratch_shapes)
    r_ops = [o for r in riders for o in r.operands]
    r_outs = [o for r in riders for o in r.out_shapes]
    r_sems = [s for r in riders for s in r.sem_shapes]
    mid_step = min(max(int(nt * mid_frac), 0), nt - 1)

    def full(*refs):
        ins, r_in, outs, r_out, scr, r_sem = _split(refs, [n_in, len(r_ops), n_out, len(r_outs), n_scr, len(r_sems)])
        parts = _rider_refs(riders, r_in, r_out, r_sem)
        step = pl.program_id(0)

        @pl.when(step == 0)
        def _():
            for r, a, b, s in parts:
                r.start(a, b, s)

        body(*ins, *outs, *scr)

        @pl.when(step == mid_step)
        def _():
            for r, a, b, s in parts:
                if r.mid is not None:
                    r.mid(a, b, s)

        @pl.when(step == nt - 1)
        def _():
            for r, a, b, s in parts:
                r.finish(a, b, s)

    res = pl.pallas_call(
        full if riders else body, name=name, grid=(nt,), in_specs=list(in_specs) + [HBM_SPEC] * len(r_ops),
        out_specs=list(out_specs) + [HBM_SPEC] * len(r_outs), out_shape=list(out_shape) + r_outs,
        scratch_shapes=list(scratch_shapes) + r_sems, input_output_aliases=_rider_aliases(riders, n_in, n_out),
        compiler_params=_cparams(),
    )(*operands, *r_ops)
    return res[:n_out], _split(res[n_out:], [len(r.out_shapes) for r in riders])


def _run_riders(riders, name):
    r_ops = [o for r in riders for o in r.operands]
    r_outs = [o for r in riders for o in r.out_shapes]
    r_sems = [s for r in riders for s in r.sem_shapes]

    def body(*refs):
        r_in, r_out, r_sem = _split(refs, [len(r_ops), len(r_outs), len(r_sems)])
        parts = _rider_refs(riders, r_in, r_out, r_sem)
        for r, a, b, s in parts:
            r.start(a, b, s)
        for r, a, b, s in parts:
            if r.mid is not None:
                r.mid(a, b, s)
        for r, a, b, s in parts:
            r.finish(a, b, s)

    res = pl.pallas_call(body, name=name, out_shape=tuple(r_outs), in_specs=[HBM_SPEC] * len(r_ops),
                         out_specs=(HBM_SPEC,) * len(r_outs), scratch_shapes=r_sems,
                         input_output_aliases=_rider_aliases(riders, 0, 0))(*r_ops)
    return _split(list(res), [len(r.out_shapes) for r in riders])


def _all_gather_rider(bufs, small=None):
    n = len(bufs)
    n_small = 0 if small is None else 1

    def copies(outs, sems):
        send_sems, recv_sems = sems
        x, y, c, chips = _mesh_pos()
        me = 2 * x + y
        here, sibling = (x, y, c), (x, y, 1 - c)
        rows = [2 * cx + cy for cx, cy in chips]

        def big(p, k, chip_row, half, to):
            piece = outs[p].at[chip_row, half]
            return _remote(piece, piece, send_sems, recv_sems, 6 * p + k, to)

        first = [big(p, j, me, c, (cx, cy, c)) for p in range(n) for j, (cx, cy) in enumerate(chips)]
        landed = [big(p, j, rows[j], c, here) for p in range(n) for j in range(3)]
        passed = [big(p, 3 + j, rows[j], c, sibling) for p in range(n) for j in range(3)]
        arrivals = [big(p, 3 + j, rows[j], 1 - c, here) for p in range(n) for j in range(3)]
        if n_small:
            first = [_remote(outs[n].at[me], outs[n].at[me], send_sems, recv_sems, 6 * n + j, (cx, cy, c))
                     for j, (cx, cy) in enumerate(chips)] + first
            arrivals += [_remote(outs[n].at[rows[j]], outs[n].at[rows[j]], send_sems, recv_sems, 6 * n + j, here)
                         for j in range(3)]
        return first, landed, passed, arrivals

    def start(ins, outs, sems):
        for cp in copies(outs, sems)[0]:
            cp.start()

    def mid(ins, outs, sems):
        _, landed, passed, _ = copies(outs, sems)
        for got, fwd in zip(landed, passed):
            got.wait_recv()
            fwd.start()

    def finish(ins, outs, sems):
        first, _, passed, arrivals = copies(outs, sems)
        for cp in arrivals:
            cp.wait_recv()
        for cp in first + passed:
            cp.wait_send()

    operands = list(bufs) + ([small] if n_small else [])
    n_sem = 6 * n + 3 * n_small
    return _Rider(operands, [jax.ShapeDtypeStruct(o.shape, o.dtype) for o in operands],
                  [pltpu.SemaphoreType.DMA((n_sem,)), pltpu.SemaphoreType.DMA((n_sem,))], start, finish, mid,
                  in_place=True)


def _pair_exchange_rider(plist):
    n = len(plist)

    def copies(ins, outs, sems):
        x, y, c, _ = _mesh_pos()
        return [_remote(ins[k].at[:, 1 - c], outs[k], sems[0], sems[1], k, (x, y, 1 - c)) for k in range(n)]

    def start(ins, outs, sems):
        for cp in copies(ins, outs, sems):
            cp.start()

    def finish(ins, outs, sems):
        for cp in copies(ins, outs, sems):
            cp.wait()

    return _Rider(plist, [jax.ShapeDtypeStruct((p.shape[0],) + p.shape[2:], p.dtype) for p in plist],
                  [pltpu.SemaphoreType.DMA((n,)), pltpu.SemaphoreType.DMA((n,))], start, finish)


def _pair_sum(p, got, c, name):
    n, _, r, l = p.shape
    br = _row_block(r, PACK_ROWS)

    def body(c_ref, p_ref, got_ref, out_ref):
        out_ref[...] = p_ref[...] + got_ref[...]

    return pl.pallas_call(
        body, name=name, out_shape=jax.ShapeDtypeStruct((n, r, l), F32),
        grid_spec=pltpu.PrefetchScalarGridSpec(
            num_scalar_prefetch=1, grid=(n, r // br),
            in_specs=[pl.BlockSpec((None, None, br, l), lambda j, i, c_ref: (j, c_ref[0], i, 0)),
                      pl.BlockSpec((None, br, l), lambda j, i, c_ref: (j, i, 0))],
            out_specs=pl.BlockSpec((None, br, l), lambda j, i, c_ref: (j, i, 0))),
        compiler_params=_cparams(2),
    )(c, p, got)


def _chip_scatter_rider(slist):
    n = len(slist)

    def copies(ins, outs, sems):
        send_sems, recv_sems = sems
        x, y, c, chips = _mesh_pos()
        sends = [_remote(ins[k].at[2 * cx + cy], outs[k].at[j], send_sems, recv_sems, 3 * k + j, (cx, cy, c))
                 for k in range(n) for j, (cx, cy) in enumerate(chips)]
        arrivals = [_remote(ins[k].at[0], outs[k].at[j], send_sems, recv_sems, 3 * k + j, (x, y, c))
                    for k in range(n) for j in range(3)]
        return sends, arrivals

    def start(ins, outs, sems):
        for cp in copies(ins, outs, sems)[0]:
            cp.start()

    def finish(ins, outs, sems):
        sends, arrivals = copies(ins, outs, sems)
        for cp in arrivals:
            cp.wait_recv()
        for cp in sends:
            cp.wait_send()

    return _Rider(slist, [jax.ShapeDtypeStruct((3,) + s.shape[1:], s.dtype) for s in slist],
                  [pltpu.SemaphoreType.DMA((3 * n,)), pltpu.SemaphoreType.DMA((3 * n,))], start, finish)


def _chip_sum(s, got, pos, name):
    _, r, l = s.shape
    br = _row_block(r, PACK_ROWS)

    def body(pos_ref, s_ref, got_ref, out_ref):
        me = pos_ref[0]
        total = None
        for chip in range(N_CHIPS):
            flip = jnp.bitwise_xor(me, chip)
            term = jnp.where(flip == 0, s_ref[...],
                             jnp.where(flip == 2, got_ref[0], jnp.where(flip == 1, got_ref[1], got_ref[2])))
            total = term if total is None else total + term
        out_ref[...] = total

    return pl.pallas_call(
        body, name=name, out_shape=jax.ShapeDtypeStruct((2, r, l), F32),
        grid_spec=pltpu.PrefetchScalarGridSpec(
            num_scalar_prefetch=1, grid=(r // br,),
            in_specs=[pl.BlockSpec((None, br, l), lambda i, pos_ref: (pos_ref[0], i, 0)),
                      pl.BlockSpec((3, br, l), lambda i, pos_ref: (0, i, 0))],
            out_specs=pl.BlockSpec((None, br, l), lambda i, pos_ref: (pos_ref[1], i, 0))),
        compiler_params=_cparams(1),
    )(pos, s, got)


def _pair_share_rider(flist):
    n = len(flist)

    def copies(outs, sems):
        x, y, c, _ = _mesh_pos()
        sends = [_remote(outs[k].at[c], outs[k].at[c], sems[0], sems[1], k, (x, y, 1 - c)) for k in range(n)]
        arrivals = [_remote(outs[k].at[1 - c], outs[k].at[1 - c], sems[0], sems[1], k, (x, y, c)) for k in range(n)]
        return sends, arrivals

    def start(ins, outs, sems):
        for cp in copies(outs, sems)[0]:
            cp.start()

    def finish(ins, outs, sems):
        sends, arrivals = copies(outs, sems)
        for cp in arrivals:
            cp.wait_recv()
        for cp in sends:
            cp.wait_send()

    return _Rider(flist, [jax.ShapeDtypeStruct(f.shape, f.dtype) for f in flist],
                  [pltpu.SemaphoreType.DMA((n,)), pltpu.SemaphoreType.DMA((n,))], start, finish, in_place=True)


def _fwd_pw1_glu(x, w1s, b1, tm):
    t, d = x.shape
    dh = d // 2

    def body(x_ref, w_ref, b_ref, a_ref, g_ref, u_ref):
        xb = x_ref[...].astype(BF16)
        for hh in range(2):
            cs = slice(hh * dh, (hh + 1) * dh)
            a = _dot(xb, w_ref[hh]) + b_ref[:, hh * dh:(hh + 1) * dh]
            g = _dot(xb, w_ref[2 + hh]) + b_ref[:, d + hh * dh:d + (hh + 1) * dh]
            a_ref[:, cs] = a.astype(BF16)
            g_ref[:, cs] = g.astype(BF16)
            u_ref[:, cs] = a * jax.nn.sigmoid(g)

    return pl.pallas_call(
        body, name="fwd_pw1_glu", grid=(t // tm,),
        in_specs=[_rows(tm, d), _const((4, d, dh)), _const((1, 2 * d))],
        out_specs=[_rows(tm, d)] * 3,
        out_shape=[jax.ShapeDtypeStruct((t, d), BF16), jax.ShapeDtypeStruct((t, d), BF16),
                   jax.ShapeDtypeStruct((t, d), F32)],
        compiler_params=_cparams(),
    )(x, w1s, b1)


def _fill_shifted(sh_ref, ext_ref):
    n = sh_ref.shape[1]
    for s in range(8):
        sh_ref[s] = ext_ref[pl.ds(s, n), :]


CONV_CHUNK = 64
LANES = 256


def _tap_sum(w_ref, sh, base, d, tap_row, out_ref, bias_ref=None):
    groups = CONV_CHUNK // 8
    for lg in range(d // LANES):
        ls = slice(lg * LANES, (lg + 1) * LANES)
        acc = jnp.zeros((groups, 8, LANES), F32)
        for k in range(CONV_WIDTH):
            e = tap_row(k)
            x = sh[e % 8, pl.ds(base + (e // 8) * 8, CONV_CHUNK), ls]
            acc = acc + w_ref[k, :, ls] * x.reshape(groups, 8, LANES)
        acc = acc.reshape(CONV_CHUNK, LANES)
        out_ref[pl.ds(base, CONV_CHUNK), ls] = acc if bias_ref is None else acc + bias_ref[:, ls]


def _fwd_conv_tail(u, x0, wdw, bdw, lng, lnb, w2, b2, mixg, mixb, tm, riders=()):
    t, d = u.shape
    hb = tm // CONV_HALO

    def body(u_ref, uh_ref, x_ref, w_ref, bdw_ref, lng_ref, lnb_ref, w2_ref, b2_ref, mg_ref, mb_ref,
             c_ref, z_ref, y_ref, ext, sh):
        i = pl.program_id(0)
        ext[0:CONV_HALO] = jnp.where(i == 0, 0.0, uh_ref[...])
        ext[CONV_HALO:CONV_HALO + tm] = u_ref[...]
        ext[CONV_HALO + tm:CONV_HALO + tm + 8] = jnp.zeros((8, d), F32)
        _fill_shifted(sh, ext)

        def chunk(r, carry):
            base = pl.multiple_of(r * CONV_CHUNK, CONV_CHUNK)
            _tap_sum(w_ref, sh, base, d, lambda k: k + CONV_HALO - (CONV_WIDTH - 1), c_ref, bdw_ref)
            return carry

        lax.fori_loop(0, tm // CONV_CHUNK, chunk, 0)
        n = _ln_fwd(c_ref[...], lng_ref[...], lnb_ref[...])
        s = n * jax.nn.sigmoid(n)
        m = _dot(s.astype(BF16), w2_ref[...]) + b2_ref[...]
        z = ALPHA * x_ref[...] + m
        z_ref[...] = z
        y_ref[...] = _ln_fwd(z, mg_ref[...], mb_ref[...])

    vec = _const((1, d))
    return _tc_call(
        body, name="fwd_conv_tail", nt=t // tm,
        in_specs=[_rows(tm, d), pl.BlockSpec((CONV_HALO, d), lambda i: (jnp.maximum(i * hb - 1, 0), 0)), _rows(tm, d),
                  _const((CONV_HALO, 8, d)), vec, vec, vec, _const((d, d)), vec, vec, vec],
        out_specs=[_rows(tm, d)] * 3,
        out_shape=[jax.ShapeDtypeStruct((t, d), F32)] * 3,
        scratch_shapes=[pltpu.VMEM((tm + CONV_HALO + 8, d), F32), pltpu.VMEM((8, tm + CONV_HALO, d), F32)],
        operands=(u, u, x0, wdw, bdw, lng, lnb, w2, b2, mixg, mixb), riders=riders)


def _fwd_ffn(x, wg, wu, wd, layer, lng, lnb, tm, target=None):
    t, d = x.shape
    fs = wg.shape[-1]
    nt = t // tm
    with_loss = target is not None

    def hidden(x_ref, wg_ref, wu_ref, wd_ref, act_ref, bm_ref, hm_ref):
        xv = x_ref[...]
        xb = xv.astype(BF16)
        f = jnp.zeros((tm, d), F32)
        for j in range(N_CHIPS):
            gj = _dot(xb, wg_ref[j])
            uj = _dot(xb, wu_ref[j])
            act, dact = _silu_and_grad(gj)
            act_ref[j] = act.astype(BF16)
            bm_ref[j] = (uj * dact).astype(BF16)
            hmb = (act * uj).astype(BF16)
            hm_ref[j] = hmb
            f = f + _dot(hmb, wd_ref[j])
        return ALPHA * xv + f

    def body(x_ref, wg_ref, wu_ref, wd_ref, g_ref, b_ref, act_ref, bm_ref, hm_ref, z_ref, y_ref):
        z = hidden(x_ref, wg_ref, wu_ref, wd_ref, act_ref, bm_ref, hm_ref)
        z_ref[...] = z
        y_ref[...] = _ln_fwd(z, g_ref[...], b_ref[...])

    def body_loss(x_ref, wg_ref, wu_ref, wd_ref, g_ref, b_ref, t_ref, act_ref, bm_ref, hm_ref, dz_ref,
                  dlg_ref, dlb_ref, loss_ref, acc_g, acc_b, acc_l):
        i = pl.program_id(0)
        _acc_init(i, acc_g, acc_b, acc_l)
        z = hidden(x_ref, wg_ref, wu_ref, wd_ref, act_ref, bm_ref, hm_ref)
        zhat, rstd = _ln_stats(z)
        gain = g_ref[...]
        err = zhat * gain + b_ref[...] - t_ref[...]
        acc_l[...] += _colsum8(err * err)
        dy = err * (1.0 / d)
        acc_g[...] += _colsum8(dy * zhat)
        acc_b[...] += _colsum8(dy)
        dzh = dy * gain
        m1 = jnp.mean(dzh, axis=-1, keepdims=True)
        m2 = jnp.mean(dzh * zhat, axis=-1, keepdims=True)
        dz_ref[...] = rstd * (dzh - m1 - zhat * m2)

        @pl.when(i == nt - 1)
        def _():
            dlg_ref[...] = jnp.sum(acc_g[...], axis=0, keepdims=True)
            dlb_ref[...] = jnp.sum(acc_b[...], axis=0, keepdims=True)
            loss_ref[...] = jnp.sum(acc_l[...], keepdims=True) * (0.5 / d)

    wcol = pl.BlockSpec((N_CHIPS, None, d, fs), lambda i: (0, layer, 0, 0), pipeline_mode=pl.Buffered(1))
    wrow = pl.BlockSpec((N_CHIPS, None, fs, d), lambda i: (0, layer, 0, 0), pipeline_mode=pl.Buffered(1))
    hid = pl.BlockSpec((N_CHIPS, tm, fs), lambda i: (0, i, 0))
    in_specs = [_rows(tm, d), wcol, wcol, wrow, _const((1, d)), _const((1, d))]
    hid_shapes = [jax.ShapeDtypeStruct((N_CHIPS, t, fs), BF16)] * 3
    if not with_loss:
        return pl.pallas_call(
            body, name=f"fwd_ffn{layer}", grid=(nt,), in_specs=in_specs,
            out_specs=[hid, hid, hid, _rows(tm, d), _rows(tm, d)],
            out_shape=hid_shapes + [jax.ShapeDtypeStruct((t, d), F32)] * 2, compiler_params=_cparams(),
        )(x, wg, wu, wd, lng, lnb)
    return pl.pallas_call(
        body_loss, name=f"fwd_ffn{layer}_loss", grid=(nt,), in_specs=in_specs + [_rows(tm, d)],
        out_specs=[hid, hid, hid, _rows(tm, d), _acc_out((1, d)), _acc_out((1, d)), _acc_out((1, 1))],
        out_shape=hid_shapes + [jax.ShapeDtypeStruct((t, d), F32)] + [jax.ShapeDtypeStruct((1, d), F32)] * 2
        + [jax.ShapeDtypeStruct((1, 1), F32)],
        scratch_shapes=[pltpu.VMEM((8, d), F32)] * 3, compiler_params=_cparams(),
    )(x, wg, wu, wd, lng, lnb, target)


def _attn_band():
    kt = lax.broadcasted_iota(jnp.int32, (BLOCK, BLOCK), 0)
    qi = lax.broadcasted_iota(jnp.int32, (BLOCK, BLOCK), 1)
    current = kt <= qi
    delta = qi - kt + jnp.where(current, 0, BLOCK)
    return current, delta.astype(F32)


def _fold(full, current):
    return jnp.where(current, full[BLOCK:2 * BLOCK], full[0:BLOCK])


def _unfold(folded, current):
    zero = jnp.zeros_like(folded)
    return jnp.concatenate([jnp.where(current, zero, folded), jnp.where(current, folded, zero)], axis=0)


def _slope(h, nq):
    return 2.0 ** (-ALIBI_MAX * (h + 1) / nq)


def _softmax_with_sink(s_full, slope, band, has_previous, sink):
    current, delta = band
    s = _fold(s_full, current) * (1.0 / math.sqrt(HEAD_DIM)) - slope * delta
    s = jnp.where(jnp.logical_or(current, has_previous), s, NEG_INF)
    m = jnp.maximum(jnp.max(s, axis=0, keepdims=True), sink)
    p = jnp.exp(s - m)
    e_sink = jnp.exp(sink - m)
    inv = 1.0 / (jnp.sum(p, axis=0, keepdims=True) + e_sink)
    return p * inv, e_sink * inv


def _heads_on_lanes(ref, b, g, group):
    first = g * group
    return jnp.concatenate([ref[b, (first + hh) * HEAD_DIM:(first + hh + 1) * HEAD_DIM, :] for hh in range(group)],
                           axis=1)


def _fill_kv(kv_scr, halo, tile, tm):
    for j in range(2 * N_KV_HEADS):
        kv_scr[j, 0:BLOCK] = halo[:, j * HEAD_DIM:(j + 1) * HEAD_DIM]
        kv_scr[j, BLOCK:BLOCK + tm] = tile[:, j * HEAD_DIM:(j + 1) * HEAD_DIM]


def _cols(d, tm):
    return pl.BlockSpec((d, tm), lambda i: (0, i))


def _fwd_attn(x, wqt, bqt, wkv, bkv, sinks, wo, bo, mixg, mixb, tm):
    t, d = x.shape
    nq = d // HEAD_DIM
    group = nq // N_KV_HEADS
    nb = tm // BLOCK

    def body(sink_ref, x_ref, xh_ref, wqt_ref, bqt_ref, wkv_ref, bkv_ref, wo_ref, bo_ref, mg_ref, mb_ref,
             qt_ref, kv_ref, ot_ref, z_ref, y_ref, kv_scr, qt_scr, ot_scr):
        i = pl.program_id(0)
        xv = x_ref[...]
        xb = xv.astype(BF16)
        qt = (_dot_nt(wqt_ref[...], xb) + bqt_ref[...]).astype(BF16)
        qt_ref[...] = qt
        for b in range(nb):
            qt_scr[b] = qt[:, b * BLOCK:(b + 1) * BLOCK]
        kvb = (_dot(xb, wkv_ref[...]) + bkv_ref[...]).astype(BF16)
        kv_ref[...] = kvb
        _fill_kv(kv_scr, (_dot(xh_ref[...].astype(BF16), wkv_ref[...]) + bkv_ref[...]).astype(BF16), kvb, tm)
        band = _attn_band()

        def block(b, carry):
            r0 = pl.multiple_of(b * BLOCK, BLOCK)
            has_previous = jnp.logical_or(i > 0, b > 0)
            for g in range(N_KV_HEADS):
                kk = kv_scr[g, pl.ds(r0, 2 * BLOCK), :]
                vv = kv_scr[N_KV_HEADS + g, pl.ds(r0, 2 * BLOCK), :]
                s_all = _dot(kk, _heads_on_lanes(qt_scr, b, g, group))
                probs = []
                for hh in range(group):
                    h = g * group + hh
                    p, _ = _softmax_with_sink(s_all[:, hh * BLOCK:(hh + 1) * BLOCK], _slope(h, nq), band,
                                              has_previous, sink_ref[h])
                    probs.append(_unfold(p.astype(BF16), band[0]))
                o_all = _dot_tn(vv, jnp.concatenate(probs, axis=1))
                for hh in range(group):
                    h = g * group + hh
                    ot_scr[b, h * HEAD_DIM:(h + 1) * HEAD_DIM, :] = o_all[:, hh * BLOCK:(hh + 1) * BLOCK].astype(BF16)
            return carry

        lax.fori_loop(0, nb, block, 0)
        ot = jnp.concatenate([ot_scr[b] for b in range(nb)], axis=1)
        ot_ref[...] = ot
        z = ALPHA * xv + _dot_tn(ot, wo_ref[...]) + bo_ref[...]
        z_ref[...] = z
        y_ref[...] = _ln_fwd(z, mg_ref[...], mb_ref[...])

    hb = tm // BLOCK
    vec = _const((1, d))
    return pl.pallas_call(
        body, name="fwd_attn", grid=(t // tm,),
        in_specs=[pl.BlockSpec(memory_space=pltpu.SMEM),
                  _rows(tm, d), pl.BlockSpec((BLOCK, d), lambda i: (jnp.maximum(i * hb - 1, 0), 0)),
                  _const((d, d)), _const((d, 1)), _const((d, 2 * KVD)), _const((1, 2 * KVD)), _const((d, d)), vec, vec,
                  vec],
        out_specs=[_cols(d, tm), _rows(tm, 2 * KVD), _cols(d, tm), _rows(tm, d), _rows(tm, d)],
        out_shape=[jax.ShapeDtypeStruct((d, t), BF16), jax.ShapeDtypeStruct((t, 2 * KVD), BF16),
                   jax.ShapeDtypeStruct((d, t), BF16), jax.ShapeDtypeStruct((t, d), F32),
                   jax.ShapeDtypeStruct((t, d), F32)],
        scratch_shapes=[pltpu.VMEM((2 * N_KV_HEADS, tm + BLOCK, HEAD_DIM), BF16), pltpu.VMEM((nb, d, BLOCK), BF16),
                        pltpu.VMEM((nb, d, BLOCK), BF16)],
        compiler_params=_cparams(),
    )(sinks, x, x, wqt, bqt, wkv, bkv, wo, bo, mixg, mixb)


def _write_sums(i, nt, pairs):
    @pl.when(i == nt - 1)
    def _():
        for out_ref, acc in pairs:
            out_ref[...] = jnp.sum(acc[...], axis=0, keepdims=True)


def _bwd_ffn_dx(dz, act, bm, wg, wu, wd, layer, z_in, g_in, tm, riders=()):
    t, d = dz.shape
    fs = wg.shape[-1]
    nt = t // tm

    def body(dz_ref, act_ref, bm_ref, wg_ref, wu_ref, wd_ref, zin_ref, gin_ref,
             dgg_ref, duu_ref, dzin_ref, dg_ref, db_ref, dsum_ref, acc_g, acc_b, acc_s):
        i = pl.program_id(0)
        _acc_init(i, acc_g, acc_b, acc_s)
        dzv = dz_ref[...]
        dzb = dzv.astype(BF16)
        dx = ALPHA * dzv
        for j in range(N_CHIPS):
            dh = _dot_nt(dzb, wd_ref[j])
            dgb = (dh * bm_ref[j].astype(F32)).astype(BF16)
            dub = (dh * act_ref[j].astype(F32)).astype(BF16)
            dgg_ref[j] = dgb
            duu_ref[j] = dub
            dx = dx + _dot_nt(dgb, wg_ref[j]) + _dot_nt(dub, wu_ref[j])
        dz_in, zhat = _ln_bwd(dx, zin_ref[...], gin_ref[...])
        acc_g[...] += _colsum8(dx * zhat)
        acc_b[...] += _colsum8(dx)
        acc_s[...] += _colsum8(dz_in)
        dzin_ref[...] = dz_in
        _write_sums(i, nt, [(dg_ref, acc_g), (db_ref, acc_b), (dsum_ref, acc_s)])

    wcol = pl.BlockSpec((N_CHIPS, None, d, fs), lambda i: (0, layer, 0, 0), pipeline_mode=pl.Buffered(1))
    wrow = pl.BlockSpec((N_CHIPS, None, fs, d), lambda i: (0, layer, 0, 0), pipeline_mode=pl.Buffered(1))
    hid = pl.BlockSpec((N_CHIPS, tm, fs), lambda i: (0, i, 0))
    return _tc_call(
        body, name=f"bwd_ffn_dx{layer}", nt=nt,
        in_specs=[_rows(tm, d), hid, hid, wcol, wcol, wrow, _rows(tm, d), _const((1, d))],
        out_specs=[hid, hid, _rows(tm, d)] + [_acc_out((1, d))] * 3,
        out_shape=[jax.ShapeDtypeStruct((N_CHIPS, t, fs), BF16)] * 2 + [jax.ShapeDtypeStruct((t, d), F32)]
        + [jax.ShapeDtypeStruct((1, d), F32)] * 3,
        scratch_shapes=[pltpu.VMEM((8, d), F32)] * 3,
        operands=(dz, act, bm, wg, wu, wd, z_in, g_in), riders=riders)


def _matmul_tn(a, b, tt, name, carry=None):
    ja, t, ka = a.shape
    jb, _, nb = b.shape
    nj = max(ja, jb)

    def body(a_ref, b_ref, *rest):
        o_ref = rest[-1] if carry is None else rest[1]

        @pl.when(pl.program_id(0) == 0)
        def _():
            o_ref[...] = jnp.zeros_like(o_ref)

        a0 = a_ref[0].astype(BF16) if ja == 1 else None
        b0 = b_ref[0].astype(BF16) if jb == 1 else None
        for j in range(nj):
            aj = a0 if ja == 1 else a_ref[j].astype(BF16)
            bj = b0 if jb == 1 else b_ref[j].astype(BF16)
            o_ref[j] += _dot_tn(aj, bj)
        if carry is not None:
            rest[2][...] = rest[0][...]

    in_specs = [pl.BlockSpec((ja, tt, ka), lambda i: (0, i, 0)), pl.BlockSpec((jb, tt, nb), lambda i: (0, i, 0))]
    out_specs = [pl.BlockSpec((nj, ka, nb), lambda i: (0, 0, 0))]
    out_shape = [jax.ShapeDtypeStruct((nj, ka, nb), F32)]
    operands = [a, b]
    if carry is not None:
        in_specs.append(_rows(tt, carry.shape[1]))
        out_specs.append(_rows(tt, carry.shape[1]))
        out_shape.append(jax.ShapeDtypeStruct(carry.shape, carry.dtype))
        operands.append(carry)
    res = pl.pallas_call(body, name=name, grid=(t // tt,), in_specs=in_specs, out_specs=out_specs,
                         out_shape=out_shape, compiler_params=_cparams())(*operands)
    return res[0] if carry is None else (res[0], res[1])


def _matmul_nn(at, b, tt, name):
    ka, t = at.shape
    nb = b.shape[1]

    def body(a_ref, b_ref, o_ref):
        @pl.when(pl.program_id(0) == 0)
        def _():
            o_ref[...] = jnp.zeros_like(o_ref)

        o_ref[...] += _dot(a_ref[...].astype(BF16), b_ref[...].astype(BF16))

    return pl.pallas_call(
        body, name=name, grid=(t // tt,),
        in_specs=[pl.BlockSpec((ka, tt), lambda i: (0, i)), pl.BlockSpec((tt, nb), lambda i: (i, 0))],
        out_specs=pl.BlockSpec((ka, nb), lambda i: (0, 0)), out_shape=jax.ShapeDtypeStruct((ka, nb), F32),
        compiler_params=_cparams(1),
    )(at, b)


def _bwd_attn(dz_all, qt, kv, sinks, wo, wqt, wkv, z_in, g_in, tm, riders=()):
    t, d = dz_all.shape
    nq = d // HEAD_DIM
    group = nq // N_KV_HEADS
    nb = tm // BLOCK
    nt = t // tm
    hb = tm // BLOCK
    n_kv = 2 * N_KV_HEADS

    def body(sink_ref, dz_ref, qt_ref, kv_ref, kvh_ref, wo_ref, wqt_ref, wkv_ref, zin_ref, gin_ref,
             dqt_ref, dkv_ref, dx_ref, dbq_ref, dbkv_ref, dsink_ref, ding_ref, dinb_ref,
             kv_scr, dkv_scr, qt_scr, dot_scr, dqt_scr, carry, acc_q, acc_kv, acc_s, acc_ig, acc_ib):
        i = pl.program_id(0)
        ti = nt - 1 - i
        _acc_init(i, carry, acc_q, acc_kv, acc_s, acc_ig, acc_ib)
        dz = dz_ref[...]
        do_t = _dot_nt(wo_ref[...], dz.astype(BF16)).astype(BF16)
        for b in range(nb):
            dot_scr[b] = do_t[:, b * BLOCK:(b + 1) * BLOCK]
            qt_scr[b] = qt_ref[:, b * BLOCK:(b + 1) * BLOCK]
        _fill_kv(kv_scr, kvh_ref[...], kv_ref[...], tm)
        dkv_scr[:, 0:tm] = jnp.zeros((n_kv, tm, HEAD_DIM), F32)
        dkv_scr[:, tm:tm + BLOCK] = carry[...]
        band = _attn_band()

        def block(b, c):
            r0 = pl.multiple_of(b * BLOCK, BLOCK)
            has_previous = jnp.logical_or(ti > 0, b > 0)
            for g in range(N_KV_HEADS):
                kk = kv_scr[g, pl.ds(r0, 2 * BLOCK), :]
                vv = kv_scr[N_KV_HEADS + g, pl.ds(r0, 2 * BLOCK), :]
                q_all = _heads_on_lanes(qt_scr, b, g, group)
                do_all = _heads_on_lanes(dot_scr, b, g, group)
                s_all = _dot(kk, q_all)
                dp_all = _dot(vv, do_all)
                probs, dscores = [], []
                for hh in range(group):
                    h = g * group + hh
                    cols = slice(hh * BLOCK, (hh + 1) * BLOCK)
                    p, p_sink = _softmax_with_sink(s_all[:, cols], _slope(h, nq), band, has_previous, sink_ref[h])
                    dp = _fold(dp_all[:, cols], band[0])
                    rs = jnp.sum(p * dp, axis=0, keepdims=True)
                    acc_s[h:h + 1, :] += -(p_sink * rs)
                    ds = p * (dp - rs) * (1.0 / math.sqrt(HEAD_DIM))
                    probs.append(_unfold(p.astype(BF16), band[0]))
                    dscores.append(_unfold(ds.astype(BF16), band[0]))
                p_all = jnp.concatenate(probs, axis=1)
                ds_all = jnp.concatenate(dscores, axis=1)
                dq_all = _dot_tn(kk, ds_all)
                for hh in range(group):
                    h = g * group + hh
                    dqt_scr[b, h * HEAD_DIM:(h + 1) * HEAD_DIM, :] = dq_all[:, hh * BLOCK:(hh + 1) * BLOCK]
                dkv_scr[g, pl.ds(r0, 2 * BLOCK), :] += _dot_nt(ds_all, q_all)
                dkv_scr[N_KV_HEADS + g, pl.ds(r0, 2 * BLOCK), :] += _dot_nt(p_all, do_all)
            return c

        lax.fori_loop(0, nb, block, 0)
        carry[...] = dkv_scr[:, 0:BLOCK]
        dkv = jnp.concatenate([dkv_scr[j, BLOCK:BLOCK + tm] for j in range(n_kv)], axis=1)
        acc_kv[...] += _colsum8(dkv)
        dkvb = dkv.astype(BF16)
        dkv_ref[...] = dkvb
        dqt = jnp.concatenate([dqt_scr[b] for b in range(nb)], axis=1)
        for b in range(nb):
            acc_q[...] += dqt_scr[b]
        dqtb = dqt.astype(BF16)
        dqt_ref[...] = dqtb
        dx = ALPHA * dz + _dot_tn(dqtb, wqt_ref[...]) + _dot_nt(dkvb, wkv_ref[...])
        dz_in, zhat_in = _ln_bwd(dx, zin_ref[...], gin_ref[...])
        acc_ig[...] += _colsum8(dx * zhat_in)
        acc_ib[...] += _colsum8(dx)
        dx_ref[...] = dz_in
        _write_sums(i, nt, [(dbkv_ref, acc_kv), (ding_ref, acc_ig), (dinb_ref, acc_ib)])

        @pl.when(i == nt - 1)
        def _():
            dbq_ref[...] = jnp.sum(acc_q[...], axis=1, keepdims=True)
            dsink_ref[...] = jnp.sum(acc_s[...], axis=1, keepdims=True)

    rev = lambda w: pl.BlockSpec((tm, w), lambda i: (nt - 1 - i, 0))
    rev_cols = pl.BlockSpec((d, tm), lambda i: (0, nt - 1 - i))
    vec = _const((1, d))
    return _tc_call(
        body, name="bwd_attn", nt=nt,
        in_specs=[pl.BlockSpec(memory_space=pltpu.SMEM), rev(d), rev_cols, rev(2 * KVD),
                  pl.BlockSpec((BLOCK, 2 * KVD), lambda i: (jnp.maximum((nt - 1 - i) * hb - 1, 0), 0)),
                  _const((d, d)), _const((d, d)), _const((d, 2 * KVD)), rev(d), vec],
        out_specs=[rev_cols, rev(2 * KVD), rev(d), _acc_out((d, 1)), _acc_out((1, 2 * KVD)), _acc_out((nq, 1)),
                   _acc_out((1, d)), _acc_out((1, d))],
        out_shape=[jax.ShapeDtypeStruct((d, t), BF16), jax.ShapeDtypeStruct((t, 2 * KVD), BF16),
                   jax.ShapeDtypeStruct((t, d), F32), jax.ShapeDtypeStruct((d, 1), F32),
                   jax.ShapeDtypeStruct((1, 2 * KVD), F32), jax.ShapeDtypeStruct((nq, 1), F32)]
        + [jax.ShapeDtypeStruct((1, d), F32)] * 2,
        scratch_shapes=[pltpu.VMEM((n_kv, tm + BLOCK, HEAD_DIM), BF16), pltpu.VMEM((n_kv, tm + BLOCK, HEAD_DIM), F32),
                        pltpu.VMEM((nb, d, BLOCK), BF16), pltpu.VMEM((nb, d, BLOCK), BF16),
                        pltpu.VMEM((nb, d, BLOCK), F32), pltpu.VMEM((n_kv, BLOCK, HEAD_DIM), F32),
                        pltpu.VMEM((d, BLOCK), F32), pltpu.VMEM((8, 2 * KVD), F32), pltpu.VMEM((nq, BLOCK), F32),
                        pltpu.VMEM((8, d), F32), pltpu.VMEM((8, d), F32)],
        operands=(sinks, dz_all, qt, kv, kv, wo, wqt, wkv, z_in, g_in), riders=riders)


def _bwd_conv_head(dz, c, w2, lng, lnb, tm, riders=()):
    t, d = dz.shape
    nt = t // tm

    def body(dz_ref, c_ref, w2_ref, lg_ref, lb_ref, s_ref, dc_ref, dlg_ref, dlb_ref, a3, a4):
        i = pl.program_id(0)
        _acc_init(i, a3, a4)
        dz = dz_ref[...]
        chat, rstd = _ln_stats(c_ref[...])
        n = chat * lg_ref[...] + lb_ref[...]
        act, dact = _silu_and_grad(n)
        s_ref[...] = act.astype(BF16)
        dn = _dot_nt(dz.astype(BF16), w2_ref[...]) * dact
        a3[...] += _colsum8(dn * chat)
        a4[...] += _colsum8(dn)
        dch = dn * lg_ref[...]
        m1 = jnp.mean(dch, axis=-1, keepdims=True)
        m2 = jnp.mean(dch * chat, axis=-1, keepdims=True)
        dc_ref[...] = rstd * (dch - m1 - chat * m2)
        _write_sums(i, nt, [(dlg_ref, a3), (dlb_ref, a4)])

    vec = _const((1, d))
    return _tc_call(
        body, name="bwd_conv_head", nt=nt,
        in_specs=[_rows(tm, d), _rows(tm, d), _const((d, d)), vec, vec],
        out_specs=[_rows(tm, d), _rows(tm, d)] + [_acc_out((1, d))] * 2,
        out_shape=[jax.ShapeDtypeStruct((t, d), BF16), jax.ShapeDtypeStruct((t, d), F32)]
        + [jax.ShapeDtypeStruct((1, d), F32)] * 2,
        scratch_shapes=[pltpu.VMEM((8, d), F32)] * 2, operands=(dz, c, w2, lng, lnb), riders=riders)


def _bwd_conv_glu(dc, u, a, g, dz, wdw, w1s, tm, riders=()):
    t, d = dc.shape
    dh_w = d // 2
    nt = t // tm
    hb = tm // CONV_HALO
    last_halo = t // CONV_HALO - 1

    def body(dc_ref, dcn_ref, u_ref, a_ref, g_ref, dz_ref, w_ref, w1_ref,
             dx_ref, dh_ref, db1_ref, dbdw_ref, dw_ref, ext, sh, du_scr, acc_b1, acc_bdw, acc_w):
        i = pl.program_id(0)
        _acc_init(i, acc_b1, acc_bdw, acc_w)
        dcv = dc_ref[...]
        acc_bdw[...] += _colsum8(dcv)

        ext[0:tm] = dcv
        ext[tm:tm + CONV_HALO] = jnp.where(i == nt - 1, 0.0, dcn_ref[...])
        ext[tm + CONV_HALO:tm + CONV_HALO + 8] = jnp.zeros((8, d), F32)
        _fill_shifted(sh, ext)

        def du_chunk(r, carry):
            base = pl.multiple_of(r * CONV_CHUNK, CONV_CHUNK)
            _tap_sum(w_ref, sh, base, d, lambda k: CONV_WIDTH - 1 - k, du_scr)
            return carry

        lax.fori_loop(0, tm // CONV_CHUNK, du_chunk, 0)

        def dw_chunk(r, carry):
            base = pl.multiple_of(r * CONV_CHUNK, CONV_CHUNK)
            groups = CONV_CHUNK // 8
            for lg in range(d // LANES):
                ls = slice(lg * LANES, (lg + 1) * LANES)
                uv = u_ref[pl.ds(base, CONV_CHUNK), ls].reshape(groups, 8, LANES)
                for k in range(CONV_WIDTH):
                    e = CONV_WIDTH - 1 - k
                    x = sh[e % 8, pl.ds(base + (e // 8) * 8, CONV_CHUNK), ls].reshape(groups, 8, LANES)
                    acc_w[k, :, ls] += jnp.sum(uv * x, axis=0)
            return carry

        lax.fori_loop(0, tm // CONV_CHUNK, dw_chunk, 0)

        du = du_scr[...]
        av = a_ref[...].astype(F32)
        sg = jax.nn.sigmoid(g_ref[...].astype(F32))
        da = du * sg
        dg = du * av * sg * (1.0 - sg)
        acc_b1[:, 0:d] += _colsum8(da)
        acc_b1[:, d:2 * d] += _colsum8(dg)
        dx = ALPHA * dz_ref[...]
        for j, part in enumerate([da[:, 0:dh_w], da[:, dh_w:d], dg[:, 0:dh_w], dg[:, dh_w:d]]):
            pb = part.astype(BF16)
            dh_ref[j] = pb
            dx = dx + _dot_nt(pb, w1_ref[j])
        dx_ref[...] = dx

        @pl.when(i == nt - 1)
        def _():
            db1_ref[...] = jnp.sum(acc_b1[...], axis=0, keepdims=True)
            dbdw_ref[...] = jnp.sum(acc_bdw[...], axis=0, keepdims=True)
            dw_ref[...] = jnp.sum(acc_w[...], axis=1)

    return _tc_call(
        body, name="bwd_conv_glu", nt=nt,
        in_specs=[_rows(tm, d), pl.BlockSpec((CONV_HALO, d), lambda i: (jnp.minimum((i + 1) * hb, last_halo), 0)),
                  _rows(tm, d), _rows(tm, d), _rows(tm, d), _rows(tm, d), _const((CONV_HALO, 8, d)),
                  _const((4, d, dh_w))],
        out_specs=[_rows(tm, d), pl.BlockSpec((4, tm, dh_w), lambda i: (0, i, 0)), _acc_out((1, 2 * d)),
                   _acc_out((1, d)), _acc_out((CONV_HALO, d))],
        out_shape=[jax.ShapeDtypeStruct((t, d), F32), jax.ShapeDtypeStruct((4, t, dh_w), BF16),
                   jax.ShapeDtypeStruct((1, 2 * d), F32), jax.ShapeDtypeStruct((1, d), F32),
                   jax.ShapeDtypeStruct((CONV_HALO, d), F32)],
        scratch_shapes=[pltpu.VMEM((tm + CONV_HALO + 8, d), F32), pltpu.VMEM((8, tm + CONV_HALO, d), F32),
                        pltpu.VMEM((tm, d), F32), pltpu.VMEM((8, 2 * d), F32), pltpu.VMEM((8, d), F32),
                        pltpu.VMEM((CONV_HALO, 8, d), F32)],
        operands=(dc, dc, u, a, g, dz, wdw, w1s), riders=riders)


def _row_block(rows, target):
    best = rows
    for cand in range(8, min(rows, target) + 1, 8):
        if rows % cand == 0:
            best = cand
    return best


def _adamw_update(w_ref, g_ref, m_ref, v_ref, d_ref, nm_ref, nv_ref):
    gv = g_ref[...]
    nm = ADAM_B1 * m_ref[...] + (1.0 - ADAM_B1) * gv
    nv = ADAM_B2 * v_ref[...] + (1.0 - ADAM_B2) * (gv * gv)
    m_hat = nm / (1.0 - ADAM_B1 ** ADAM_STEP)
    v_hat = nv / (1.0 - ADAM_B2 ** ADAM_STEP)
    d_ref[...] = -ADAM_LR * (m_hat / (jnp.sqrt(v_hat) + ADAM_EPS) + ADAM_WD * w_ref[...])
    nm_ref[...] = nm
    nv_ref[...] = nv


ADAMW_STEPS = 8


def _adamw(params, name, riders=()):
    n = len(params)
    steps = ADAMW_STEPS if all(p[0].shape[0] % (8 * ADAMW_STEPS) == 0 for p in params) else 1

    def body(*refs):
        for k in range(n):
            _adamw_update(*refs[4 * k:4 * k + 4], *refs[4 * n + 3 * k:4 * n + 3 * k + 3])

    specs = [pl.BlockSpec((p[0].shape[0] // steps, p[0].shape[1]), lambda i: (i, 0)) for p in params]
    outs, rider_outs = _tc_call(
        body, name=name, nt=steps, in_specs=[s for s in specs for _ in range(4)],
        out_specs=[s for s in specs for _ in range(3)],
        out_shape=[jax.ShapeDtypeStruct(p[0].shape, F32) for p in params for _ in range(3)],
        operands=[a for p in params for a in p], riders=riders)
    return [tuple(outs[3 * k:3 * k + 3]) for k in range(n)], rider_outs


def _pad_to(v, n):
    return jnp.pad(v, (0, n - v.shape[0]))


def _round_up(n, m):
    return (n + m - 1) // m * m


def kernel(x, conv_w_pw1, conv_b_pw1, conv_w_dw, conv_b_dw, conv_ln_g, conv_ln_b, conv_w_pw2, conv_b_pw2, kv_w_k, kv_b_k, kv_w_v, kv_b_v, attn_w_q, attn_b_q, attn_sinks, attn_w_o, attn_b_o, ffn_w_gate, ffn_w_up, ffn_w_down, ln_mix_g, ln_mix_b, ln_ffn_g, ln_ffn_b, loss_target, m_conv_w_pw1, m_conv_b_pw1, m_conv_w_dw, m_conv_b_dw, m_conv_ln_g, m_conv_ln_b, m_conv_w_pw2, m_conv_b_pw2, m_kv_w_k, m_kv_b_k, m_kv_w_v, m_kv_b_v, m_attn_w_q, m_attn_b_q, m_attn_sinks, m_attn_w_o, m_attn_b_o, m_ffn_w_gate, m_ffn_w_up, m_ffn_w_down, m_ln_mix_g, m_ln_mix_b, m_ln_ffn_g, m_ln_ffn_b, v_conv_w_pw1, v_conv_b_pw1, v_conv_w_dw, v_conv_b_dw, v_conv_ln_g, v_conv_ln_b, v_conv_w_pw2, v_conv_b_pw2, v_kv_w_k, v_kv_b_k, v_kv_w_v, v_kv_b_v, v_attn_w_q, v_attn_b_q, v_attn_sinks, v_attn_w_o, v_attn_b_o, v_ffn_w_gate, v_ffn_w_up, v_ffn_w_down, v_ln_mix_g, v_ln_mix_b, v_ln_ffn_g, v_ln_ffn_b):
    args = dict(locals())
    w = {n: args[n] for n in WEIGHTS}
    mom = {n: args["m_" + n] for n in WEIGHTS}
    var = {n: args["v_" + n] for n in WEIGHTS}
    assert x.shape[0] == 1, "one sequence per device"
    t, d = x.shape[1], x.shape[2]
    dq = d // 4
    fs = ffn_w_gate.shape[-1]
    nq = d // HEAD_DIM
    x0 = x.reshape(t, d)
    target = loss_target.reshape(t, d)
    tm_big = min(512, t)
    tm_mid = min(256, t)
    tm_tn = min(1024, t)
    c_idx = lax.axis_index("c")

    me_idx = 2 * lax.axis_index("x") + lax.axis_index("y")

    def gather_buffer(v):
        buf = lax.empty((N_CHIPS,) + v.shape, v.dtype)
        return lax.dynamic_update_slice(buf, v[None], (me_idx,) + (0,) * v.ndim)

    def halves(v):
        return v.reshape(2, -1, v.shape[-1])

    small_sizes = [int(w[n].size) for n in SMALL_SHARDED]
    rs = _round_up(sum(small_sizes), 8 * 128) // 128
    spack = _pad_to(jnp.concatenate([w[n].reshape(-1) for n in SMALL_SHARDED]), rs * 128).reshape(rs, 128)
    conv_first = ['conv_w_pw1', 'conv_w_pw2']
    later = [n for n in BIG if n not in conv_first]
    (first_out,) = _run_riders(
        [_all_gather_rider([gather_buffer(halves(w[n].astype(BF16))) for n in conv_first], gather_buffer(spack))],
        "all_gather_conv")
    later_rider = _all_gather_rider([gather_buffer(halves(w[n].astype(BF16))) for n in later])
    gs = first_out[-1].reshape(N_CHIPS, rs * 128)
    full = {n: g.reshape((N_CHIPS,) + w[n].shape) for n, g in zip(conv_first, first_out)}
    off = 0
    for n, size in zip(SMALL_SHARDED, small_sizes):
        full[n] = gs[:, off:off + size].reshape((N_CHIPS,) + w[n].shape)
        off += size
    w1s = full['conv_w_pw1'].reshape(N_CHIPS, d, d // 2)
    w2 = full['conv_w_pw2'].reshape(d, d)
    b1 = full['conv_b_pw1'].reshape(1, 2 * d)
    wdw = jnp.pad(full['conv_w_dw'].reshape(N_CHIPS, CONV_WIDTH, dq).transpose(1, 0, 2).reshape(CONV_WIDTH, d),
                  ((0, CONV_HALO - CONV_WIDTH), (0, 0)))
    wdw = jnp.broadcast_to(wdw[:, None, :], (CONV_HALO, 8, d))
    bdw = full['conv_b_dw'].reshape(1, d)
    clng = full['conv_ln_g'].reshape(1, d)
    clnb = full['conv_ln_b'].reshape(1, d)
    b2 = full['conv_b_pw2'].reshape(1, d)
    bkv = jnp.concatenate([kv_b_k, kv_b_v]).reshape(1, 2 * KVD)
    sinks = attn_sinks.reshape(nq)
    mixg = [ln_mix_g[l].reshape(1, d) for l in range(DEPTH)]
    mixb = [ln_mix_b[l].reshape(1, d) for l in range(DEPTH)]
    ffng = [ln_ffn_g[l].reshape(1, d) for l in range(DEPTH)]
    ffnb = [ln_ffn_b[l].reshape(1, d) for l in range(DEPTH)]

    a_act, g_act, u_act = _fwd_pw1_glu(x0, w1s, b1, tm_big)
    (c_act, z1, x1), (later_out,) = _fwd_conv_tail(u_act, x0, wdw, bdw, clng, clnb, w2, b2, mixg[0], mixb[0], tm_mid,
                                                   riders=[later_rider])
    full.update({n: g.reshape((N_CHIPS,) + w[n].shape) for n, g in zip(later, later_out)})
    wkv = jnp.concatenate([full['kv_w_k'].reshape(d, KVD), full['kv_w_v'].reshape(d, KVD)], axis=1)
    wqt = full['attn_w_q'].reshape(d, d).T
    wo = full['attn_w_o'].reshape(d, d)
    wg, wu, wd = full['ffn_w_gate'], full['ffn_w_up'], full['ffn_w_down']
    act0, bm0, hm0, z2, x2 = _fwd_ffn(x1, wg, wu, wd, 0, ffng[0], ffnb[0], tm_big)
    qt_act, kv_act, ot_act, z3, x3 = _fwd_attn(x2, wqt, attn_b_q.reshape(d, 1), wkv, bkv, sinks, wo, attn_b_o,
                                               mixg[1], mixb[1], tm_big)
    act1, bm1, hm1, dz4, d_fg1, d_fb1, loss_part = _fwd_ffn(x3, wg, wu, wd, 1, ffng[1], ffnb[1], tm_big, target=target)
    loss = lax.psum(loss_part[0, 0], ("x", "y", "c"))

    c_arr = c_idx.reshape(1).astype(jnp.int32)

    def halves4(v):
        return v.reshape(N_CHIPS, 2, -1, v.shape[-1])

    def arrays(group):
        return [p for _, p in group]

    def pair_sums(group, got):
        return [_pair_sum(p, g, c_arr, "grad_pair_sum_" + n) for (n, p), g in zip(group, got)]

    pos_arr = jnp.stack([me_idx, c_idx]).astype(jnp.int32)

    def chip_sums(group, sums, got):
        return [_chip_sum(s, g, pos_arr, "grad_chip_sum_" + n) for (n, _), s, g in zip(group, sums, got)]

    (dgg1, duu1, dz3, d_mg1, d_mb1, d_bo), _ = _bwd_ffn_dx(dz4, act1, bm1, wg, wu, wd, 1, z3, mixg[1], tm_big)
    g1 = [("ffn_w_gate1", halves4(_matmul_tn(x3[None], dgg1, tm_tn, "dw_gate1"))),
          ("ffn_w_up1", halves4(_matmul_tn(x3[None], duu1, tm_tn, "dw_up1"))),
          ("ffn_w_down1", halves4(_matmul_tn(hm1, dz4[None], tm_tn, "dw_down1")))]
    (dqt, dkv, dz2, d_bq, d_bkv, d_sinks, d_fg0, d_fb0), (got1,) = _bwd_attn(
        dz3, qt_act, kv_act, sinks, wo, wqt, wkv, z2, ffng[0], tm_big, riders=[_pair_exchange_rider(arrays(g1))])
    s1 = pair_sums(g1, got1)
    dwo = _matmul_nn(ot_act, dz3, tm_tn, "dw_o")
    dwq = _matmul_nn(dqt, x2, tm_tn, "dw_q").T
    dwkv = _matmul_tn(x2[None], dkv[None], tm_tn, "dw_kv")[0]
    g2 = [("attn_w_o", halves4(dwo)), ("attn_w_q", halves4(dwq)),
          ("kv_w_k", halves4(dwkv[:, 0:KVD])), ("kv_w_v", halves4(dwkv[:, KVD:2 * KVD]))]
    (dgg0, duu0, dz1, d_mg0, d_mb0, d_b2), (from_chips1, got2) = _bwd_ffn_dx(
        dz2, act0, bm0, wg, wu, wd, 0, z1, mixg[0], tm_big,
        riders=[_chip_scatter_rider(s1), _pair_exchange_rider(arrays(g2))])
    f1 = chip_sums(g1, s1, from_chips1)
    s2 = pair_sums(g2, got2)
    g3 = [("ffn_w_gate0", halves4(_matmul_tn(x1[None], dgg0, tm_tn, "dw_gate0"))),
          ("ffn_w_up0", halves4(_matmul_tn(x1[None], duu0, tm_tn, "dw_up0"))),
          ("ffn_w_down0", halves4(_matmul_tn(hm0, dz2[None], tm_tn, "dw_down0")))]
    (s_act, dc, d_clng, d_clnb), (got3, shared1) = _bwd_conv_head(
        dz1, c_act, w2, clng, clnb, tm_mid, riders=[_pair_exchange_rider(arrays(g3)), _pair_share_rider(f1)])
    s3 = pair_sums(g3, got3)
    dw2 = _matmul_tn(s_act[None], dz1[None], tm_tn, "dw_pw2")
    (dx0, dh1, d_b1, d_bdw, d_wdw), (from_chips2, from_chips3) = _bwd_conv_glu(
        dc, u_act, a_act, g_act, dz1, wdw, w1s, tm_mid, riders=[_chip_scatter_rider(s2), _chip_scatter_rider(s3)])
    f2 = chip_sums(g2, s2, from_chips2)
    f3 = chip_sums(g3, s3, from_chips3)
    dw1, grad_x = _matmul_tn(x0[None], dh1, tm_tn, "dw_pw1", carry=dx0)

    def rows4(v):
        return v.reshape(N_CHIPS, -1)

    def rep4(v):
        return jnp.broadcast_to(v.reshape(1, -1), (N_CHIPS, v.size))

    local = {
        'conv_b_pw1': rows4(d_b1),
        'conv_w_dw': rows4(d_wdw[0:CONV_WIDTH].reshape(CONV_WIDTH, N_CHIPS, dq).transpose(1, 0, 2)),
        'conv_b_dw': rows4(d_bdw), 'conv_ln_g': rows4(d_clng), 'conv_ln_b': rows4(d_clnb), 'conv_b_pw2': rows4(d_b2),
        'kv_b_k': rep4(d_bkv[:, 0:KVD]), 'kv_b_v': rep4(d_bkv[:, KVD:2 * KVD]), 'attn_b_q': rep4(d_bq),
        'attn_sinks': rep4(d_sinks), 'attn_b_o': rep4(d_bo),
        'ln_mix_g': rep4(jnp.concatenate([d_mg0, d_mg1])), 'ln_mix_b': rep4(jnp.concatenate([d_mb0, d_mb1])),
        'ln_ffn_g': rep4(jnp.concatenate([d_fg0, d_fg1])), 'ln_ffn_b': rep4(jnp.concatenate([d_fb0, d_fb1])),
    }
    n_small = sum(int(w[n].size) for n in SMALL)
    small_rows = _round_up(n_small, 2 * 8 * 128) // 128
    small_local = jnp.concatenate([local[n] for n in SMALL], axis=1)
    small_local = jnp.pad(small_local, ((0, 0), (0, small_rows * 128 - n_small)))
    g4 = [("conv_w_pw1", halves4(dw1)), ("conv_w_pw2", halves4(dw2)),
          ("small", small_local.reshape(N_CHIPS, 2, small_rows // 2, 128))]
    got4, shared23 = _run_riders([_pair_exchange_rider(arrays(g4)), _pair_share_rider(f2 + f3)],
                                 "grad_pair_exchange_last")
    s4 = pair_sums(g4, got4)
    reduced = dict(zip([n for n, _ in g1], shared1))
    reduced.update(zip([n for n, _ in g2 + g3], shared23))
    for n in ('ffn_w_gate', 'ffn_w_up', 'ffn_w_down'):
        reduced[n] = jnp.stack([reduced[n + str(layer)].reshape(w[n].shape[1:]) for layer in range(DEPTH)])

    g_out, delta, new_m, new_v = {}, {}, {}, {}

    def adamw_matrices(names, name, riders=()):
        for n in names:
            g_out[n] = reduced[n].reshape(w[n].shape)
        two_d = [tuple(tree[n].reshape(-1, w[n].shape[-1]) for tree in (w, g_out, mom, var)) for n in names]
        results, rider_outs = _adamw(two_d, name, riders)
        for n, (dl, nm, nv) in zip(names, results):
            delta[n], new_m[n], new_v[n] = (r.reshape(w[n].shape) for r in (dl, nm, nv))
        return rider_outs

    (from_chips4,) = adamw_matrices([n for n in BIG if n not in conv_first], "adamw_attn_ffn",
                                    riders=[_chip_scatter_rider(s4)])
    f4 = chip_sums(g4, s4, from_chips4)
    (shared4,) = _run_riders([_pair_share_rider(f4)], "grad_pair_share_last")
    reduced.update(zip([n for n, _ in g4], shared4))
    adamw_matrices(conv_first, "adamw_conv")

    def pack_small(tree):
        return _pad_to(jnp.concatenate([tree[n].reshape(-1) for n in SMALL]), small_rows * 128).reshape(small_rows, 128)

    g_small = reduced['small'].reshape(small_rows, 128)
    ((dl, nm, nv),), _ = _adamw([(pack_small(w), g_small, pack_small(mom), pack_small(var))], "adamw_small")
    off = 0
    for n in SMALL:
        size, shape = int(w[n].size), w[n].shape
        for tree, flat in ((g_out, g_small), (delta, dl), (new_m, nm), (new_v, nv)):
            tree[n] = flat.reshape(-1)[off:off + size].reshape(shape)
        off += size

    return (loss, grad_x.reshape(x.shape), *[g_out[n] for n in WEIGHTS], *[delta[n] for n in WEIGHTS],
            *[new_m[n] for n in WEIGHTS], *[new_v[n] for n in WEIGHTS])
```

```python
import functools
import math

import jax
import jax.numpy as jnp
from jax import lax
from jax.experimental import pallas as pl
from jax.experimental.pallas import tpu as pltpu

F32 = jnp.float32
BF16 = jnp.bfloat16

DEPTH = 2
ALPHA = (2.0 * DEPTH) ** 0.25
LN_EPS = 1e-5
NEG_INF = -1e30
HEAD_DIM = 64
N_KV_HEADS = 2
KVD = N_KV_HEADS * HEAD_DIM
BLOCK = 128
CONV_WIDTH = 31
CONV_HALO = 32
ALIBI_MAX = 8.0
ADAM_LR, ADAM_B1, ADAM_B2, ADAM_EPS, ADAM_WD, ADAM_STEP = 0.001, 0.9, 0.999, 1e-08, 0.01, 10

N_CHIPS = 4
SUM_STEPS = 4
VMEM_LIMIT = 60 * 1024 * 1024
MESH = pl.DeviceIdType.MESH

NT_DIMS = (((1,), (1,)), ((), ()))
TN_DIMS = (((0,), (0,)), ((), ()))

WEIGHTS = ['conv_w_pw1', 'conv_b_pw1', 'conv_w_dw', 'conv_b_dw', 'conv_ln_g', 'conv_ln_b', 'conv_w_pw2', 'conv_b_pw2',
           'kv_w_k', 'kv_b_k', 'kv_w_v', 'kv_b_v', 'attn_w_q', 'attn_b_q', 'attn_sinks', 'attn_w_o', 'attn_b_o',
           'ffn_w_gate', 'ffn_w_up', 'ffn_w_down', 'ln_mix_g', 'ln_mix_b', 'ln_ffn_g', 'ln_ffn_b']
BIG = ['conv_w_pw1', 'conv_w_pw2', 'kv_w_k', 'kv_w_v', 'attn_w_q', 'attn_w_o', 'ffn_w_gate', 'ffn_w_up', 'ffn_w_down']
SMALL_SHARDED = ['conv_b_pw1', 'conv_w_dw', 'conv_b_dw', 'conv_ln_g', 'conv_ln_b', 'conv_b_pw2']
REPLICATED = ['kv_b_k', 'kv_b_v', 'attn_b_q', 'attn_sinks', 'attn_b_o', 'ln_mix_g', 'ln_mix_b', 'ln_ffn_g', 'ln_ffn_b']
SMALL = SMALL_SHARDED + REPLICATED


def _cparams(n_grid=1):
    return pltpu.CompilerParams(dimension_semantics=("arbitrary",) * n_grid, vmem_limit_bytes=VMEM_LIMIT)


def _rows(tm, width):
    return pl.BlockSpec((tm, width), lambda i: (i, 0))


def _const(shape):
    return pl.BlockSpec(shape, lambda *_: (0,) * len(shape), pipeline_mode=pl.Buffered(1))


def _acc_out(shape):
    return pl.BlockSpec(shape, lambda *_: (0,) * len(shape))


def _dot(a, b):
    return jnp.dot(a, b, preferred_element_type=F32)


def _dot_nt(a, b):
    return lax.dot_general(a, b, NT_DIMS, preferred_element_type=F32)


def _dot_tn(a, b):
    return lax.dot_general(a, b, TN_DIMS, preferred_element_type=F32)


def _colsum8(v):
    m, n = v.shape
    return jnp.sum(v.reshape(m // 8, 8, n), axis=0)


def _ln_stats(z):
    mu = jnp.mean(z, axis=-1, keepdims=True)
    zc = z - mu
    var = jnp.mean(zc * zc, axis=-1, keepdims=True)
    rstd = lax.rsqrt(var + LN_EPS)
    return zc * rstd, rstd


def _ln_fwd(z, g, b):
    zhat, _ = _ln_stats(z)
    return zhat * g + b


def _ln_bwd(dy, z, g):
    zhat, rstd = _ln_stats(z)
    dzh = dy * g
    m1 = jnp.mean(dzh, axis=-1, keepdims=True)
    m2 = jnp.mean(dzh * zhat, axis=-1, keepdims=True)
    return rstd * (dzh - m1 - zhat * m2), zhat


def _silu_and_grad(n):
    sg = jax.nn.sigmoid(n)
    return n * sg, sg * (1.0 + n * (1.0 - sg))


def _acc_init(i, *refs):
    @pl.when(i == 0)
    def _():
        for r in refs:
            r[...] = jnp.zeros_like(r)


def _mesh_pos():
    x, y, c = lax.axis_index("x"), lax.axis_index("y"), lax.axis_index("c")
    chips = [(1 - x, y), (x, 1 - y), (1 - x, 1 - y)]
    return x, y, c, chips


HBM_SPEC = pl.BlockSpec(memory_space=pltpu.HBM)


def _remote(src, dst, send_sems, recv_sems, k, to):
    return pltpu.make_async_remote_copy(src_ref=src, dst_ref=dst, send_sem=send_sems.at[k], recv_sem=recv_sems.at[k],
                                        device_id=to, device_id_type=MESH)


class _Rider:
    def __init__(self, operands, out_shapes, sem_shapes, start, finish, mid=None, in_place=False):
        self.operands, self.out_shapes, self.sem_shapes = list(operands), list(out_shapes), list(sem_shapes)
        self.start, self.finish, self.mid = start, finish, mid
        self.in_place = in_place


def _rider_aliases(riders, first_in, first_out):
    aliases, k_in, k_out = {}, first_in, first_out
    for r in riders:
        if r.in_place:
            aliases.update({k_in + k: k_out + k for k in range(len(r.operands))})
        k_in += len(r.operands)
        k_out += len(r.out_shapes)
    return aliases


def _split(refs, counts):
    parts, k = [], 0
    for n in counts:
        parts.append(refs[k:k + n])
        k += n
    return parts


def _rider_refs(riders, ins, outs, sems):
    return list(zip(riders, _split(ins, [len(r.operands) for r in riders]),
                    _split(outs, [len(r.out_shapes) for r in riders]),
                    _split(sems, [len(r.sem_shapes) for r in riders])))


def _tc_call(body, *, name, nt, in_specs, out_specs, out_shape, operands, scratch_shapes=(), riders=(), mid_frac=0.75):
    n_in, n_out, n_scr = len(in_specs), len(out_specs), len(scratch_shapes)
    r_ops = [o for r in riders for o in r.operands]
    r_outs = [o for r in riders for o in r.out_shapes]
    r_sems = [s for r in riders for s in r.sem_shapes]
    mid_step = min(max(int(nt * mid_frac), 0), nt - 1)

    def full(*refs):
        ins, r_in, outs, r_out, scr, r_sem = _split(refs, [n_in, len(r_ops), n_out, len(r_outs), n_scr, len(r_sems)])
        parts = _rider_refs(riders, r_in, r_out, r_sem)
        step = pl.program_id(0)

        @pl.when(step == 0)
        def _():
            for r, a, b, s in parts:
                r.start(a, b, s)

        body(*ins, *outs, *scr)

        @pl.when(step == mid_step)
        def _():
            for r, a, b, s in parts:
                if r.mid is not None:
                    r.mid(a, b, s)

        @pl.when(step == nt - 1)
        def _():
            for r, a, b, s in parts:
                r.finish(a, b, s)

    res = pl.pallas_call(
        full if riders else body, name=name, grid=(nt,), in_specs=list(in_specs) + [HBM_SPEC] * len(r_ops),
        out_specs=list(out_specs) + [HBM_SPEC] * len(r_outs), out_shape=list(out_shape) + r_outs,
        scratch_shapes=list(scratch_shapes) + r_sems, input_output_aliases=_rider_aliases(riders, n_in, n_out),
        compiler_params=_cparams(),
    )(*operands, *r_ops)
    return res[:n_out], _split(res[n_out:], [len(r.out_shapes) for r in riders])


def _run_riders(riders, name):
    r_ops = [o for r in riders for o in r.operands]
    r_outs = [o for r in riders for o in r.out_shapes]
    r_sems = [s for r in riders for s in r.sem_shapes]

    def body(*refs):
        r_in, r_out, r_sem = _split(refs, [len(r_ops), len(r_outs), len(r_sems)])
        parts = _rider_refs(riders, r_in, r_out, r_sem)
        for r, a, b, s in parts:
            r.start(a, b, s)
        for r, a, b, s in parts:
            if r.mid is not None:
                r.mid(a, b, s)
        for r, a, b, s in parts:
            r.finish(a, b, s)

    res = pl.pallas_call(body, name=name, out_shape=tuple(r_outs), in_specs=[HBM_SPEC] * len(r_ops),
                         out_specs=(HBM_SPEC,) * len(r_outs), scratch_shapes=r_sems,
                         input_output_aliases=_rider_aliases(riders, 0, 0))(*r_ops)
    return _split(list(res), [len(r.out_shapes) for r in riders])


def _all_gather_rider(bufs, small=None):
    n = len(bufs)
    n_small = 0 if small is None else 1

    def copies(outs, sems):
        send_sems, recv_sems = sems
        x, y, c, chips = _mesh_pos()
        me = 2 * x + y
        here, sibling = (x, y, c), (x, y, 1 - c)
        rows = [2 * cx + cy for cx, cy in chips]

        def big(p, k, chip_row, half, to):
            piece = outs[p].at[chip_row, half]
            return _remote(piece, piece, send_sems, recv_sems, 6 * p + k, to)

        first = [big(p, j, me, c, (cx, cy, c)) for p in range(n) for j, (cx, cy) in enumerate(chips)]
        landed = [big(p, j, rows[j], c, here) for p in range(n) for j in range(3)]
        passed = [big(p, 3 + j, rows[j], c, sibling) for p in range(n) for j in range(3)]
        arrivals = [big(p, 3 + j, rows[j], 1 - c, here) for p in range(n) for j in range(3)]
        if n_small:
            first = [_remote(outs[n].at[me], outs[n].at[me], send_sems, recv_sems, 6 * n + j, (cx, cy, c))
                     for j, (cx, cy) in enumerate(chips)] + first
            arrivals += [_remote(outs[n].at[rows[j]], outs[n].at[rows[j]], send_sems, recv_sems, 6 * n + j, here)
                         for j in range(3)]
        return first, landed, passed, arrivals

    def start(ins, outs, sems):
        for cp in copies(outs, sems)[0]:
            cp.start()

    def mid(ins, outs, sems):
        _, landed, passed, _ = copies(outs, sems)
        for got, fwd in zip(landed, passed):
            got.wait_recv()
            fwd.start()

    def finish(ins, outs, sems):
        first, _, passed, arrivals = copies(outs, sems)
        for cp in arrivals:
            cp.wait_recv()
        for cp in first + passed:
            cp.wait_send()

    operands = list(bufs) + ([small] if n_small else [])
    n_sem = 6 * n + 3 * n_small
    return _Rider(operands, [jax.ShapeDtypeStruct(o.shape, o.dtype) for o in operands],
                  [pltpu.SemaphoreType.DMA((n_sem,)), pltpu.SemaphoreType.DMA((n_sem,))], start, finish, mid,
                  in_place=True)


def _pair_exchange_rider(plist):
    n = len(plist)

    def copies(ins, outs, sems):
        x, y, c, _ = _mesh_pos()
        return [_remote(ins[k].at[:, 1 - c], outs[k], sems[0], sems[1], k, (x, y, 1 - c)) for k in range(n)]

    def start(ins, outs, sems):
        for cp in copies(ins, outs, sems):
            cp.start()

    def finish(ins, outs, sems):
        for cp in copies(ins, outs, sems):
            cp.wait()

    return _Rider(plist, [jax.ShapeDtypeStruct((p.shape[0],) + p.shape[2:], p.dtype) for p in plist],
                  [pltpu.SemaphoreType.DMA((n,)), pltpu.SemaphoreType.DMA((n,))], start, finish)


def _pair_sums(plist, gots, c, name):
    n = len(plist)

    def body(c_ref, *refs):
        for k in range(n):
            refs[2 * n + k][...] = refs[k][...] + refs[n + k][...]

    rows = [p.shape[2] // SUM_STEPS for p in plist]
    return pl.pallas_call(
        body, name=name, out_shape=[jax.ShapeDtypeStruct(g.shape, F32) for g in gots],
        grid_spec=pltpu.PrefetchScalarGridSpec(
            num_scalar_prefetch=1, grid=(N_CHIPS, SUM_STEPS),
            in_specs=[pl.BlockSpec((None, None, br, p.shape[3]), lambda j, i, c_ref: (j, c_ref[0], i, 0))
                      for p, br in zip(plist, rows)]
            + [pl.BlockSpec((None, br, p.shape[3]), lambda j, i, c_ref: (j, i, 0)) for p, br in zip(plist, rows)],
            out_specs=[pl.BlockSpec((None, br, p.shape[3]), lambda j, i, c_ref: (j, i, 0))
                       for p, br in zip(plist, rows)]),
        compiler_params=_cparams(2),
    )(c, *plist, *gots)


def _chip_scatter_rider(slist):
    n = len(slist)

    def copies(ins, outs, sems):
        send_sems, recv_sems = sems
        x, y, c, chips = _mesh_pos()
        sends = [_remote(ins[k].at[2 * cx + cy], outs[k].at[j], send_sems, recv_sems, 3 * k + j, (cx, cy, c))
                 for k in range(n) for j, (cx, cy) in enumerate(chips)]
        arrivals = [_remote(ins[k].at[0], outs[k].at[j], send_sems, recv_sems, 3 * k + j, (x, y, c))
                    for k in range(n) for j in range(3)]
        return sends, arrivals

    def start(ins, outs, sems):
        for cp in copies(ins, outs, sems)[0]:
            cp.start()

    def finish(ins, outs, sems):
        sends, arrivals = copies(ins, outs, sems)
        for cp in arrivals:
            cp.wait_recv()
        for cp in sends:
            cp.wait_send()

    return _Rider(slist, [jax.ShapeDtypeStruct((3,) + s.shape[1:], s.dtype) for s in slist],
                  [pltpu.SemaphoreType.DMA((3 * n,)), pltpu.SemaphoreType.DMA((3 * n,))], start, finish)


def _chip_sums(slist, gots, pos, name):
    n = len(slist)

    def body(pos_ref, *refs):
        me = pos_ref[0]
        for k in range(n):
            s_ref, got_ref, out_ref = refs[k], refs[n + k], refs[2 * n + k]
            total = None
            for chip in range(N_CHIPS):
                flip = jnp.bitwise_xor(me, chip)
                term = jnp.where(flip == 0, s_ref[...],
                                 jnp.where(flip == 2, got_ref[0], jnp.where(flip == 1, got_ref[1], got_ref[2])))
                total = term if total is None else total + term
            out_ref[...] = total

    rows = [s.shape[1] // SUM_STEPS for s in slist]
    return pl.pallas_call(
        body, name=name, out_shape=[jax.ShapeDtypeStruct((2,) + s.shape[1:], F32) for s in slist],
        grid_spec=pltpu.PrefetchScalarGridSpec(
            num_scalar_prefetch=1, grid=(SUM_STEPS,),
            in_specs=[pl.BlockSpec((None, br, s.shape[2]), lambda i, pos_ref: (pos_ref[0], i, 0))
                      for s, br in zip(slist, rows)]
            + [pl.BlockSpec((3, br, s.shape[2]), lambda i, pos_ref: (0, i, 0)) for s, br in zip(slist, rows)],
            out_specs=[pl.BlockSpec((None, br, s.shape[2]), lambda i, pos_ref: (pos_ref[1], i, 0))
                       for s, br in zip(slist, rows)]),
        compiler_params=_cparams(1),
    )(pos, *slist, *gots)


def _pair_share_rider(flist):
    n = len(flist)

    def copies(outs, sems):
        x, y, c, _ = _mesh_pos()
        sends = [_remote(outs[k].at[c], outs[k].at[c], sems[0], sems[1], k, (x, y, 1 - c)) for k in range(n)]
        arrivals = [_remote(outs[k].at[1 - c], outs[k].at[1 - c], sems[0], sems[1], k, (x, y, c)) for k in range(n)]
        return sends, arrivals

    def start(ins, outs, sems):
        for cp in copies(outs, sems)[0]:
            cp.start()

    def finish(ins, outs, sems):
        sends, arrivals = copies(outs, sems)
        for cp in arrivals:
            cp.wait_recv()
        for cp in sends:
            cp.wait_send()

    return _Rider(flist, [jax.ShapeDtypeStruct(f.shape, f.dtype) for f in flist],
                  [pltpu.SemaphoreType.DMA((n,)), pltpu.SemaphoreType.DMA((n,))], start, finish, in_place=True)


def _fwd_pw1_glu(x, w1s, b1, tm):
    t, d = x.shape
    dh = d // 2

    def body(x_ref, w_ref, b_ref, a_ref, g_ref, u_ref):
        xb = x_ref[...].astype(BF16)
        for hh in range(2):
            cs = slice(hh * dh, (hh + 1) * dh)
            a = _dot(xb, w_ref[hh]) + b_ref[:, hh * dh:(hh + 1) * dh]
            g = _dot(xb, w_ref[2 + hh]) + b_ref[:, d + hh * dh:d + (hh + 1) * dh]
            a_ref[:, cs] = a.astype(BF16)
            g_ref[:, cs] = g.astype(BF16)
            u_ref[:, cs] = a * jax.nn.sigmoid(g)

    return pl.pallas_call(
        body, name="fwd_pw1_glu", grid=(t // tm,),
        in_specs=[_rows(tm, d), _const((4, d, dh)), _const((1, 2 * d))],
        out_specs=[_rows(tm, d)] * 3,
        out_shape=[jax.ShapeDtypeStruct((t, d), BF16), jax.ShapeDtypeStruct((t, d), BF16),
                   jax.ShapeDtypeStruct((t, d), F32)],
        compiler_params=_cparams(),
    )(x, w1s, b1)


def _fill_shifted(sh_ref, ext_ref):
    n = sh_ref.shape[1]
    for s in range(8):
        sh_ref[s] = ext_ref[pl.ds(s, n), :]


CONV_CHUNK = 64
LANES = 256


def _tap_sum(w_ref, sh, base, d, tap_row, out_ref, bias_ref=None):
    groups = CONV_CHUNK // 8
    for lg in range(d // LANES):
        ls = slice(lg * LANES, (lg + 1) * LANES)
        acc = jnp.zeros((groups, 8, LANES), F32)
        for k in range(CONV_WIDTH):
            e = tap_row(k)
            x = sh[e % 8, pl.ds(base + (e // 8) * 8, CONV_CHUNK), ls]
            acc = acc + w_ref[k, :, ls] * x.reshape(groups, 8, LANES)
        acc = acc.reshape(CONV_CHUNK, LANES)
        out_ref[pl.ds(base, CONV_CHUNK), ls] = acc if bias_ref is None else acc + bias_ref[:, ls]


def _fwd_conv_tail(u, x0, wdw, bdw, lng, lnb, w2, b2, mixg, mixb, tm, riders=()):
    t, d = u.shape
    hb = tm // CONV_HALO

    def body(u_ref, uh_ref, x_ref, w_ref, bdw_ref, lng_ref, lnb_ref, w2_ref, b2_ref, mg_ref, mb_ref,
             c_ref, z_ref, y_ref, ext, sh):
        i = pl.program_id(0)
        ext[0:CONV_HALO] = jnp.where(i == 0, 0.0, uh_ref[...])
        ext[CONV_HALO:CONV_HALO + tm] = u_ref[...]
        ext[CONV_HALO + tm:CONV_HALO + tm + 8] = jnp.zeros((8, d), F32)
        _fill_shifted(sh, ext)

        def chunk(r, carry):
            base = pl.multiple_of(r * CONV_CHUNK, CONV_CHUNK)
            _tap_sum(w_ref, sh, base, d, lambda k: k + CONV_HALO - (CONV_WIDTH - 1), c_ref, bdw_ref)
            return carry

        lax.fori_loop(0, tm // CONV_CHUNK, chunk, 0)
        n = _ln_fwd(c_ref[...], lng_ref[...], lnb_ref[...])
        s = n * jax.nn.sigmoid(n)
        m = _dot(s.astype(BF16), w2_ref[...]) + b2_ref[...]
        z = ALPHA * x_ref[...] + m
        z_ref[...] = z
        y_ref[...] = _ln_fwd(z, mg_ref[...], mb_ref[...])

    vec = _const((1, d))
    return _tc_call(
        body, name="fwd_conv_tail", nt=t // tm,
        in_specs=[_rows(tm, d), pl.BlockSpec((CONV_HALO, d), lambda i: (jnp.maximum(i * hb - 1, 0), 0)), _rows(tm, d),
                  _const((CONV_HALO, 8, d)), vec, vec, vec, _const((d, d)), vec, vec, vec],
        out_specs=[_rows(tm, d)] * 3,
        out_shape=[jax.ShapeDtypeStruct((t, d), F32)] * 3,
        scratch_shapes=[pltpu.VMEM((tm + CONV_HALO + 8, d), F32), pltpu.VMEM((8, tm + CONV_HALO, d), F32)],
        operands=(u, u, x0, wdw, bdw, lng, lnb, w2, b2, mixg, mixb), riders=riders)


def _fwd_ffn(x, wg, wu, wd, layer, lng, lnb, tm, target=None):
    t, d = x.shape
    fs = wg.shape[-1]
    nt = t // tm
    with_loss = target is not None

    def hidden(x_ref, wg_ref, wu_ref, wd_ref, act_ref, bm_ref, hm_ref):
        xv = x_ref[...]
        xb = xv.astype(BF16)
        f = jnp.zeros((tm, d), F32)
        for j in range(N_CHIPS):
            gj = _dot(xb, wg_ref[j])
            uj = _dot(xb, wu_ref[j])
            act, dact = _silu_and_grad(gj)
            act_ref[j] = act.astype(BF16)
            bm_ref[j] = (uj * dact).astype(BF16)
            hmb = (act * uj).astype(BF16)
            hm_ref[j] = hmb
            f = f + _dot(hmb, wd_ref[j])
        return ALPHA * xv + f

    def body(x_ref, wg_ref, wu_ref, wd_ref, g_ref, b_ref, act_ref, bm_ref, hm_ref, z_ref, y_ref):
        z = hidden(x_ref, wg_ref, wu_ref, wd_ref, act_ref, bm_ref, hm_ref)
        z_ref[...] = z
        y_ref[...] = _ln_fwd(z, g_ref[...], b_ref[...])

    def body_loss(x_ref, wg_ref, wu_ref, wd_ref, g_ref, b_ref, t_ref, act_ref, bm_ref, hm_ref, dz_ref,
                  dlg_ref, dlb_ref, loss_ref, acc_g, acc_b, acc_l):
        i = pl.program_id(0)
        _acc_init(i, acc_g, acc_b, acc_l)
        z = hidden(x_ref, wg_ref, wu_ref, wd_ref, act_ref, bm_ref, hm_ref)
        zhat, rstd = _ln_stats(z)
        gain = g_ref[...]
        err = zhat * gain + b_ref[...] - t_ref[...]
        acc_l[...] += _colsum8(err * err)
        dy = err * (1.0 / d)
        acc_g[...] += _colsum8(dy * zhat)
        acc_b[...] += _colsum8(dy)
        dzh = dy * gain
        m1 = jnp.mean(dzh, axis=-1, keepdims=True)
        m2 = jnp.mean(dzh * zhat, axis=-1, keepdims=True)
        dz_ref[...] = rstd * (dzh - m1 - zhat * m2)

        @pl.when(i == nt - 1)
        def _():
            dlg_ref[...] = jnp.sum(acc_g[...], axis=0, keepdims=True)
            dlb_ref[...] = jnp.sum(acc_b[...], axis=0, keepdims=True)
            loss_ref[...] = jnp.sum(acc_l[...], keepdims=True) * (0.5 / d)

    wcol = pl.BlockSpec((N_CHIPS, None, d, fs), lambda i: (0, layer, 0, 0), pipeline_mode=pl.Buffered(1))
    wrow = pl.BlockSpec((N_CHIPS, None, fs, d), lambda i: (0, layer, 0, 0), pipeline_mode=pl.Buffered(1))
    hid = pl.BlockSpec((N_CHIPS, tm, fs), lambda i: (0, i, 0))
    in_specs = [_rows(tm, d), wcol, wcol, wrow, _const((1, d)), _const((1, d))]
    hid_shapes = [jax.ShapeDtypeStruct((N_CHIPS, t, fs), BF16)] * 3
    if not with_loss:
        return pl.pallas_call(
            body, name=f"fwd_ffn{layer}", grid=(nt,), in_specs=in_specs,
            out_specs=[hid, hid, hid, _rows(tm, d), _rows(tm, d)],
            out_shape=hid_shapes + [jax.ShapeDtypeStruct((t, d), F32)] * 2, compiler_params=_cparams(),
        )(x, wg, wu, wd, lng, lnb)
    return pl.pallas_call(
        body_loss, name=f"fwd_ffn{layer}_loss", grid=(nt,), in_specs=in_specs + [_rows(tm, d)],
        out_specs=[hid, hid, hid, _rows(tm, d), _acc_out((1, d)), _acc_out((1, d)), _acc_out((1, 1))],
        out_shape=hid_shapes + [jax.ShapeDtypeStruct((t, d), F32)] + [jax.ShapeDtypeStruct((1, d), F32)] * 2
        + [jax.ShapeDtypeStruct((1, 1), F32)],
        scratch_shapes=[pltpu.VMEM((8, d), F32)] * 3, compiler_params=_cparams(),
    )(x, wg, wu, wd, lng, lnb, target)


def _attn_band():
    kt = lax.broadcasted_iota(jnp.int32, (BLOCK, BLOCK), 0)
    qi = lax.broadcasted_iota(jnp.int32, (BLOCK, BLOCK), 1)
    current = kt <= qi
    delta = qi - kt + jnp.where(current, 0, BLOCK)
    return current, delta.astype(F32)


def _fold(full, current):
    return jnp.where(current, full[BLOCK:2 * BLOCK], full[0:BLOCK])


def _unfold(folded, current):
    zero = jnp.zeros_like(folded)
    return jnp.concatenate([jnp.where(current, zero, folded), jnp.where(current, folded, zero)], axis=0)


def _slope(h, nq):
    return 2.0 ** (-ALIBI_MAX * (h + 1) / nq)


def _softmax_with_sink(s_full, slope, band, has_previous, sink):
    current, delta = band
    s = _fold(s_full, current) * (1.0 / math.sqrt(HEAD_DIM)) - slope * delta
    s = jnp.where(jnp.logical_or(current, has_previous), s, NEG_INF)
    m = jnp.maximum(jnp.max(s, axis=0, keepdims=True), sink)
    p = jnp.exp(s - m)
    e_sink = jnp.exp(sink - m)
    inv = 1.0 / (jnp.sum(p, axis=0, keepdims=True) + e_sink)
    return p * inv, e_sink * inv


def _heads_on_lanes(ref, b, g, group):
    first = g * group
    return jnp.concatenate([ref[b, (first + hh) * HEAD_DIM:(first + hh + 1) * HEAD_DIM, :] for hh in range(group)],
                           axis=1)


def _fill_kv(kv_scr, halo, tile, tm):
    for j in range(2 * N_KV_HEADS):
        kv_scr[j, 0:BLOCK] = halo[:, j * HEAD_DIM:(j + 1) * HEAD_DIM]
        kv_scr[j, BLOCK:BLOCK + tm] = tile[:, j * HEAD_DIM:(j + 1) * HEAD_DIM]


def _cols(d, tm):
    return pl.BlockSpec((d, tm), lambda i: (0, i))


def _fwd_attn(x, wqt, bqt, wkv, bkv, sinks, wo, bo, mixg, mixb, tm):
    t, d = x.shape
    nq = d // HEAD_DIM
    group = nq // N_KV_HEADS
    nb = tm // BLOCK

    def body(sink_ref, x_ref, xh_ref, wqt_ref, bqt_ref, wkv_ref, bkv_ref, wo_ref, bo_ref, mg_ref, mb_ref,
             qt_ref, kv_ref, ot_ref, z_ref, y_ref, kv_scr, qt_scr, ot_scr):
        i = pl.program_id(0)
        xv = x_ref[...]
        xb = xv.astype(BF16)
        qt = (_dot_nt(wqt_ref[...], xb) + bqt_ref[...]).astype(BF16)
        qt_ref[...] = qt
        for b in range(nb):
            qt_scr[b] = qt[:, b * BLOCK:(b + 1) * BLOCK]
        kvb = (_dot(xb, wkv_ref[...]) + bkv_ref[...]).astype(BF16)
        kv_ref[...] = kvb
        _fill_kv(kv_scr, (_dot(xh_ref[...].astype(BF16), wkv_ref[...]) + bkv_ref[...]).astype(BF16), kvb, tm)
        band = _attn_band()

        def block(b, carry):
            r0 = pl.multiple_of(b * BLOCK, BLOCK)
            has_previous = jnp.logical_or(i > 0, b > 0)
            for g in range(N_KV_HEADS):
                kk = kv_scr[g, pl.ds(r0, 2 * BLOCK), :]
                vv = kv_scr[N_KV_HEADS + g, pl.ds(r0, 2 * BLOCK), :]
                s_all = _dot(kk, _heads_on_lanes(qt_scr, b, g, group))
                probs = []
                for hh in range(group):
                    h = g * group + hh
                    p, _ = _softmax_with_sink(s_all[:, hh * BLOCK:(hh + 1) * BLOCK], _slope(h, nq), band,
                                              has_previous, sink_ref[h])
                    probs.append(_unfold(p.astype(BF16), band[0]))
                o_all = _dot_tn(vv, jnp.concatenate(probs, axis=1))
                for hh in range(group):
                    h = g * group + hh
                    ot_scr[b, h * HEAD_DIM:(h + 1) * HEAD_DIM, :] = o_all[:, hh * BLOCK:(hh + 1) * BLOCK].astype(BF16)
            return carry

        lax.fori_loop(0, nb, block, 0)
        ot = jnp.concatenate([ot_scr[b] for b in range(nb)], axis=1)
        ot_ref[...] = ot
        z = ALPHA * xv + _dot_tn(ot, wo_ref[...]) + bo_ref[...]
        z_ref[...] = z
        y_ref[...] = _ln_fwd(z, mg_ref[...], mb_ref[...])

    hb = tm // BLOCK
    vec = _const((1, d))
    return pl.pallas_call(
        body, name="fwd_attn", grid=(t // tm,),
        in_specs=[pl.BlockSpec(memory_space=pltpu.SMEM),
                  _rows(tm, d), pl.BlockSpec((BLOCK, d), lambda i: (jnp.maximum(i * hb - 1, 0), 0)),
                  _const((d, d)), _const((d, 1)), _const((d, 2 * KVD)), _const((1, 2 * KVD)), _const((d, d)), vec, vec,
                  vec],
        out_specs=[_cols(d, tm), _rows(tm, 2 * KVD), _cols(d, tm), _rows(tm, d), _rows(tm, d)],
        out_shape=[jax.ShapeDtypeStruct((d, t), BF16), jax.ShapeDtypeStruct((t, 2 * KVD), BF16),
                   jax.ShapeDtypeStruct((d, t), BF16), jax.ShapeDtypeStruct((t, d), F32),
                   jax.ShapeDtypeStruct((t, d), F32)],
        scratch_shapes=[pltpu.VMEM((2 * N_KV_HEADS, tm + BLOCK, HEAD_DIM), BF16), pltpu.VMEM((nb, d, BLOCK), BF16),
                        pltpu.VMEM((nb, d, BLOCK), BF16)],
        compiler_params=_cparams(),
    )(sinks, x, x, wqt, bqt, wkv, bkv, wo, bo, mixg, mixb)


def _write_sums(i, nt, pairs):
    @pl.when(i == nt - 1)
    def _():
        for out_ref, acc in pairs:
            out_ref[...] = jnp.sum(acc[...], axis=0, keepdims=True)


def _bwd_ffn_dx(dz, act, bm, wg, wu, wd, layer, z_in, g_in, tm, riders=()):
    t, d = dz.shape
    fs = wg.shape[-1]
    nt = t // tm

    def body(dz_ref, act_ref, bm_ref, wg_ref, wu_ref, wd_ref, zin_ref, gin_ref,
             dgg_ref, duu_ref, dzin_ref, dg_ref, db_ref, dsum_ref, acc_g, acc_b, acc_s):
        i = pl.program_id(0)
        _acc_init(i, acc_g, acc_b, acc_s)
        dzv = dz_ref[...]
        dzb = dzv.astype(BF16)
        dx = ALPHA * dzv
        for j in range(N_CHIPS):
            dh = _dot_nt(dzb, wd_ref[j])
            dgb = (dh * bm_ref[j].astype(F32)).astype(BF16)
            dub = (dh * act_ref[j].astype(F32)).astype(BF16)
            dgg_ref[j] = dgb
            duu_ref[j] = dub
            dx = dx + _dot_nt(dgb, wg_ref[j]) + _dot_nt(dub, wu_ref[j])
        dz_in, zhat = _ln_bwd(dx, zin_ref[...], gin_ref[...])
        acc_g[...] += _colsum8(dx * zhat)
        acc_b[...] += _colsum8(dx)
        acc_s[...] += _colsum8(dz_in)
        dzin_ref[...] = dz_in
        _write_sums(i, nt, [(dg_ref, acc_g), (db_ref, acc_b), (dsum_ref, acc_s)])

    wcol = pl.BlockSpec((N_CHIPS, None, d, fs), lambda i: (0, layer, 0, 0), pipeline_mode=pl.Buffered(1))
    wrow = pl.BlockSpec((N_CHIPS, None, fs, d), lambda i: (0, layer, 0, 0), pipeline_mode=pl.Buffered(1))
    hid = pl.BlockSpec((N_CHIPS, tm, fs), lambda i: (0, i, 0))
    return _tc_call(
        body, name=f"bwd_ffn_dx{layer}", nt=nt,
        in_specs=[_rows(tm, d), hid, hid, wcol, wcol, wrow, _rows(tm, d), _const((1, d))],
        out_specs=[hid, hid, _rows(tm, d)] + [_acc_out((1, d))] * 3,
        out_shape=[jax.ShapeDtypeStruct((N_CHIPS, t, fs), BF16)] * 2 + [jax.ShapeDtypeStruct((t, d), F32)]
        + [jax.ShapeDtypeStruct((1, d), F32)] * 3,
        scratch_shapes=[pltpu.VMEM((8, d), F32)] * 3,
        operands=(dz, act, bm, wg, wu, wd, z_in, g_in), riders=riders)


def _matmul_tn(a, b, tt, name, carry=None):
    ja, t, ka = a.shape
    jb, _, nb = b.shape
    nj = max(ja, jb)

    def body(a_ref, b_ref, *rest):
        o_ref = rest[-1] if carry is None else rest[1]

        @pl.when(pl.program_id(0) == 0)
        def _():
            o_ref[...] = jnp.zeros_like(o_ref)

        a0 = a_ref[0].astype(BF16) if ja == 1 else None
        b0 = b_ref[0].astype(BF16) if jb == 1 else None
        for j in range(nj):
            aj = a0 if ja == 1 else a_ref[j].astype(BF16)
            bj = b0 if jb == 1 else b_ref[j].astype(BF16)
            o_ref[j] += _dot_tn(aj, bj)
        if carry is not None:
            rest[2][...] = rest[0][...]

    in_specs = [pl.BlockSpec((ja, tt, ka), lambda i: (0, i, 0)), pl.BlockSpec((jb, tt, nb), lambda i: (0, i, 0))]
    out_specs = [pl.BlockSpec((nj, ka, nb), lambda i: (0, 0, 0))]
    out_shape = [jax.ShapeDtypeStruct((nj, ka, nb), F32)]
    operands = [a, b]
    if carry is not None:
        in_specs.append(_rows(tt, carry.shape[1]))
        out_specs.append(_rows(tt, carry.shape[1]))
        out_shape.append(jax.ShapeDtypeStruct(carry.shape, carry.dtype))
        operands.append(carry)
    res = pl.pallas_call(body, name=name, grid=(t // tt,), in_specs=in_specs, out_specs=out_specs,
                         out_shape=out_shape, compiler_params=_cparams())(*operands)
    return res[0] if carry is None else (res[0], res[1])


def _matmul_nn(at, b, tt, name):
    ka, t = at.shape
    nb = b.shape[1]

    def body(a_ref, b_ref, o_ref):
        @pl.when(pl.program_id(0) == 0)
        def _():
            o_ref[...] = jnp.zeros_like(o_ref)

        o_ref[...] += _dot(a_ref[...].astype(BF16), b_ref[...].astype(BF16))

    return pl.pallas_call(
        body, name=name, grid=(t // tt,),
        in_specs=[pl.BlockSpec((ka, tt), lambda i: (0, i)), pl.BlockSpec((tt, nb), lambda i: (i, 0))],
        out_specs=pl.BlockSpec((ka, nb), lambda i: (0, 0)), out_shape=jax.ShapeDtypeStruct((ka, nb), F32),
        compiler_params=_cparams(1),
    )(at, b)


def _bwd_attn(dz_all, qt, kv, sinks, wo, wqt, wkv, z_in, g_in, tm, riders=()):
    t, d = dz_all.shape
    nq = d // HEAD_DIM
    group = nq // N_KV_HEADS
    nb = tm // BLOCK
    nt = t // tm
    hb = tm // BLOCK
    n_kv = 2 * N_KV_HEADS

    def body(sink_ref, dz_ref, qt_ref, kv_ref, kvh_ref, wo_ref, wqt_ref, wkv_ref, zin_ref, gin_ref,
             dqt_ref, dkv_ref, dx_ref, dbq_ref, dbkv_ref, dsink_ref, ding_ref, dinb_ref,
             kv_scr, dkv_scr, qt_scr, dot_scr, dqt_scr, carry, acc_q, acc_kv, acc_s, acc_ig, acc_ib):
        i = pl.program_id(0)
        ti = nt - 1 - i
        _acc_init(i, carry, acc_q, acc_kv, acc_s, acc_ig, acc_ib)
        dz = dz_ref[...]
        do_t = _dot_nt(wo_ref[...], dz.astype(BF16)).astype(BF16)
        for b in range(nb):
            dot_scr[b] = do_t[:, b * BLOCK:(b + 1) * BLOCK]
            qt_scr[b] = qt_ref[:, b * BLOCK:(b + 1) * BLOCK]
        _fill_kv(kv_scr, kvh_ref[...], kv_ref[...], tm)
        dkv_scr[:, 0:tm] = jnp.zeros((n_kv, tm, HEAD_DIM), F32)
        dkv_scr[:, tm:tm + BLOCK] = carry[...]
        band = _attn_band()

        def block(b, c):
            r0 = pl.multiple_of(b * BLOCK, BLOCK)
            has_previous = jnp.logical_or(ti > 0, b > 0)
            for g in range(N_KV_HEADS):
                kk = kv_scr[g, pl.ds(r0, 2 * BLOCK), :]
                vv = kv_scr[N_KV_HEADS + g, pl.ds(r0, 2 * BLOCK), :]
                q_all = _heads_on_lanes(qt_scr, b, g, group)
                do_all = _heads_on_lanes(dot_scr, b, g, group)
                s_all = _dot(kk, q_all)
                dp_all = _dot(vv, do_all)
                probs, dscores = [], []
                for hh in range(group):
                    h = g * group + hh
                    cols = slice(hh * BLOCK, (hh + 1) * BLOCK)
                    p, p_sink = _softmax_with_sink(s_all[:, cols], _slope(h, nq), band, has_previous, sink_ref[h])
                    dp = _fold(dp_all[:, cols], band[0])
                    rs = jnp.sum(p * dp, axis=0, keepdims=True)
                    acc_s[h:h + 1, :] += -(p_sink * rs)
                    ds = p * (dp - rs) * (1.0 / math.sqrt(HEAD_DIM))
                    probs.append(_unfold(p.astype(BF16), band[0]))
                    dscores.append(_unfold(ds.astype(BF16), band[0]))
                p_all = jnp.concatenate(probs, axis=1)
                ds_all = jnp.concatenate(dscores, axis=1)
                dq_all = _dot_tn(kk, ds_all)
                for hh in range(group):
                    h = g * group + hh
                    dqt_scr[b, h * HEAD_DIM:(h + 1) * HEAD_DIM, :] = dq_all[:, hh * BLOCK:(hh + 1) * BLOCK]
                dkv_scr[g, pl.ds(r0, 2 * BLOCK), :] += _dot_nt(ds_all, q_all)
                dkv_scr[N_KV_HEADS + g, pl.ds(r0, 2 * BLOCK), :] += _dot_nt(p_all, do_all)
            return c

        lax.fori_loop(0, nb, block, 0)
        carry[...] = dkv_scr[:, 0:BLOCK]
        dkv = jnp.concatenate([dkv_scr[j, BLOCK:BLOCK + tm] for j in range(n_kv)], axis=1)
        acc_kv[...] += _colsum8(dkv)
        dkvb = dkv.astype(BF16)
        dkv_ref[...] = dkvb
        dqt = jnp.concatenate([dqt_scr[b] for b in range(nb)], axis=1)
        for b in range(nb):
            acc_q[...] += dqt_scr[b]
        dqtb = dqt.astype(BF16)
        dqt_ref[...] = dqtb
        dx = ALPHA * dz + _dot_tn(dqtb, wqt_ref[...]) + _dot_nt(dkvb, wkv_ref[...])
        dz_in, zhat_in = _ln_bwd(dx, zin_ref[...], gin_ref[...])
        acc_ig[...] += _colsum8(dx * zhat_in)
        acc_ib[...] += _colsum8(dx)
        dx_ref[...] = dz_in
        _write_sums(i, nt, [(dbkv_ref, acc_kv), (ding_ref, acc_ig), (dinb_ref, acc_ib)])

        @pl.when(i == nt - 1)
        def _():
            dbq_ref[...] = jnp.sum(acc_q[...], axis=1, keepdims=True)
            dsink_ref[...] = jnp.sum(acc_s[...], axis=1, keepdims=True)

    rev = lambda w: pl.BlockSpec((tm, w), lambda i: (nt - 1 - i, 0))
    rev_cols = pl.BlockSpec((d, tm), lambda i: (0, nt - 1 - i))
    vec = _const((1, d))
    return _tc_call(
        body, name="bwd_attn", nt=nt,
        in_specs=[pl.BlockSpec(memory_space=pltpu.SMEM), rev(d), rev_cols, rev(2 * KVD),
                  pl.BlockSpec((BLOCK, 2 * KVD), lambda i: (jnp.maximum((nt - 1 - i) * hb - 1, 0), 0)),
                  _const((d, d)), _const((d, d)), _const((d, 2 * KVD)), rev(d), vec],
        out_specs=[rev_cols, rev(2 * KVD), rev(d), _acc_out((d, 1)), _acc_out((1, 2 * KVD)), _acc_out((nq, 1)),
                   _acc_out((1, d)), _acc_out((1, d))],
        out_shape=[jax.ShapeDtypeStruct((d, t), BF16), jax.ShapeDtypeStruct((t, 2 * KVD), BF16),
                   jax.ShapeDtypeStruct((t, d), F32), jax.ShapeDtypeStruct((d, 1), F32),
                   jax.ShapeDtypeStruct((1, 2 * KVD), F32), jax.ShapeDtypeStruct((nq, 1), F32)]
        + [jax.ShapeDtypeStruct((1, d), F32)] * 2,
        scratch_shapes=[pltpu.VMEM((n_kv, tm + BLOCK, HEAD_DIM), BF16), pltpu.VMEM((n_kv, tm + BLOCK, HEAD_DIM), F32),
                        pltpu.VMEM((nb, d, BLOCK), BF16), pltpu.VMEM((nb, d, BLOCK), BF16),
                        pltpu.VMEM((nb, d, BLOCK), F32), pltpu.VMEM((n_kv, BLOCK, HEAD_DIM), F32),
                        pltpu.VMEM((d, BLOCK), F32), pltpu.VMEM((8, 2 * KVD), F32), pltpu.VMEM((nq, BLOCK), F32),
                        pltpu.VMEM((8, d), F32), pltpu.VMEM((8, d), F32)],
        operands=(sinks, dz_all, qt, kv, kv, wo, wqt, wkv, z_in, g_in), riders=riders)


def _bwd_conv_head(dz, c, w2, lng, lnb, tm, riders=()):
    t, d = dz.shape
    nt = t // tm

    def body(dz_ref, c_ref, w2_ref, lg_ref, lb_ref, s_ref, dc_ref, dlg_ref, dlb_ref, a3, a4):
        i = pl.program_id(0)
        _acc_init(i, a3, a4)
        dz = dz_ref[...]
        chat, rstd = _ln_stats(c_ref[...])
        n = chat * lg_ref[...] + lb_ref[...]
        act, dact = _silu_and_grad(n)
        s_ref[...] = act.astype(BF16)
        dn = _dot_nt(dz.astype(BF16), w2_ref[...]) * dact
        a3[...] += _colsum8(dn * chat)
        a4[...] += _colsum8(dn)
        dch = dn * lg_ref[...]
        m1 = jnp.mean(dch, axis=-1, keepdims=True)
        m2 = jnp.mean(dch * chat, axis=-1, keepdims=True)
        dc_ref[...] = rstd * (dch - m1 - chat * m2)
        _write_sums(i, nt, [(dlg_ref, a3), (dlb_ref, a4)])

    vec = _const((1, d))
    return _tc_call(
        body, name="bwd_conv_head", nt=nt,
        in_specs=[_rows(tm, d), _rows(tm, d), _const((d, d)), vec, vec],
        out_specs=[_rows(tm, d), _rows(tm, d)] + [_acc_out((1, d))] * 2,
        out_shape=[jax.ShapeDtypeStruct((t, d), BF16), jax.ShapeDtypeStruct((t, d), F32)]
        + [jax.ShapeDtypeStruct((1, d), F32)] * 2,
        scratch_shapes=[pltpu.VMEM((8, d), F32)] * 2, operands=(dz, c, w2, lng, lnb), riders=riders)


def _bwd_conv_glu(dc, u, a, g, dz, wdw, w1s, tm, riders=()):
    t, d = dc.shape
    dh_w = d // 2
    nt = t // tm
    hb = tm // CONV_HALO
    last_halo = t // CONV_HALO - 1

    def body(dc_ref, dcn_ref, u_ref, a_ref, g_ref, dz_ref, w_ref, w1_ref,
             dx_ref, dh_ref, db1_ref, dbdw_ref, dw_ref, ext, sh, du_scr, acc_b1, acc_bdw, acc_w):
        i = pl.program_id(0)
        _acc_init(i, acc_b1, acc_bdw, acc_w)
        dcv = dc_ref[...]
        acc_bdw[...] += _colsum8(dcv)

        ext[0:tm] = dcv
        ext[tm:tm + CONV_HALO] = jnp.where(i == nt - 1, 0.0, dcn_ref[...])
        ext[tm + CONV_HALO:tm + CONV_HALO + 8] = jnp.zeros((8, d), F32)
        _fill_shifted(sh, ext)

        def du_chunk(r, carry):
            base = pl.multiple_of(r * CONV_CHUNK, CONV_CHUNK)
            _tap_sum(w_ref, sh, base, d, lambda k: CONV_WIDTH - 1 - k, du_scr)
            return carry

        lax.fori_loop(0, tm // CONV_CHUNK, du_chunk, 0)

        def dw_chunk(r, carry):
            base = pl.multiple_of(r * CONV_CHUNK, CONV_CHUNK)
            groups = CONV_CHUNK // 8
            for lg in range(d // LANES):
                ls = slice(lg * LANES, (lg + 1) * LANES)
                uv = u_ref[pl.ds(base, CONV_CHUNK), ls].reshape(groups, 8, LANES)
                for k in range(CONV_WIDTH):
                    e = CONV_WIDTH - 1 - k
                    x = sh[e % 8, pl.ds(base + (e // 8) * 8, CONV_CHUNK), ls].reshape(groups, 8, LANES)
                    acc_w[k, :, ls] += jnp.sum(uv * x, axis=0)
            return carry

        lax.fori_loop(0, tm // CONV_CHUNK, dw_chunk, 0)

        du = du_scr[...]
        av = a_ref[...].astype(F32)
        sg = jax.nn.sigmoid(g_ref[...].astype(F32))
        da = du * sg
        dg = du * av * sg * (1.0 - sg)
        acc_b1[:, 0:d] += _colsum8(da)
        acc_b1[:, d:2 * d] += _colsum8(dg)
        dx = ALPHA * dz_ref[...]
        for j, part in enumerate([da[:, 0:dh_w], da[:, dh_w:d], dg[:, 0:dh_w], dg[:, dh_w:d]]):
            pb = part.astype(BF16)
            dh_ref[j] = pb
            dx = dx + _dot_nt(pb, w1_ref[j])
        dx_ref[...] = dx

        @pl.when(i == nt - 1)
        def _():
            db1_ref[...] = jnp.sum(acc_b1[...], axis=0, keepdims=True)
            dbdw_ref[...] = jnp.sum(acc_bdw[...], axis=0, keepdims=True)
            dw_ref[...] = jnp.sum(acc_w[...], axis=1)

    return _tc_call(
        body, name="bwd_conv_glu", nt=nt,
        in_specs=[_rows(tm, d), pl.BlockSpec((CONV_HALO, d), lambda i: (jnp.minimum((i + 1) * hb, last_halo), 0)),
                  _rows(tm, d), _rows(tm, d), _rows(tm, d), _rows(tm, d), _const((CONV_HALO, 8, d)),
                  _const((4, d, dh_w))],
        out_specs=[_rows(tm, d), pl.BlockSpec((4, tm, dh_w), lambda i: (0, i, 0)), _acc_out((1, 2 * d)),
                   _acc_out((1, d)), _acc_out((CONV_HALO, d))],
        out_shape=[jax.ShapeDtypeStruct((t, d), F32), jax.ShapeDtypeStruct((4, t, dh_w), BF16),
                   jax.ShapeDtypeStruct((1, 2 * d), F32), jax.ShapeDtypeStruct((1, d), F32),
                   jax.ShapeDtypeStruct((CONV_HALO, d), F32)],
        scratch_shapes=[pltpu.VMEM((tm + CONV_HALO + 8, d), F32), pltpu.VMEM((8, tm + CONV_HALO, d), F32),
                        pltpu.VMEM((tm, d), F32), pltpu.VMEM((8, 2 * d), F32), pltpu.VMEM((8, d), F32),
                        pltpu.VMEM((CONV_HALO, 8, d), F32)],
        operands=(dc, dc, u, a, g, dz, wdw, w1s), riders=riders)


def _adamw_update(w_ref, g_ref, m_ref, v_ref, d_ref, nm_ref, nv_ref):
    gv = g_ref[...]
    nm = ADAM_B1 * m_ref[...] + (1.0 - ADAM_B1) * gv
    nv = ADAM_B2 * v_ref[...] + (1.0 - ADAM_B2) * (gv * gv)
    m_hat = nm / (1.0 - ADAM_B1 ** ADAM_STEP)
    v_hat = nv / (1.0 - ADAM_B2 ** ADAM_STEP)
    d_ref[...] = -ADAM_LR * (m_hat / (jnp.sqrt(v_hat) + ADAM_EPS) + ADAM_WD * w_ref[...])
    nm_ref[...] = nm
    nv_ref[...] = nv


ADAMW_STEPS = 8


def _adamw(params, name, riders=()):
    n = len(params)
    steps = ADAMW_STEPS if all(p[0].shape[0] % (8 * ADAMW_STEPS) == 0 for p in params) else 1

    def body(*refs):
        for k in range(n):
            _adamw_update(*refs[4 * k:4 * k + 4], *refs[4 * n + 3 * k:4 * n + 3 * k + 3])

    specs = [pl.BlockSpec((p[0].shape[0] // steps, p[0].shape[1]), lambda i: (i, 0)) for p in params]
    outs, rider_outs = _tc_call(
        body, name=name, nt=steps, in_specs=[s for s in specs for _ in range(4)],
        out_specs=[s for s in specs for _ in range(3)],
        out_shape=[jax.ShapeDtypeStruct(p[0].shape, F32) for p in params for _ in range(3)],
        operands=[a for p in params for a in p], riders=riders)
    return [tuple(outs[3 * k:3 * k + 3]) for k in range(n)], rider_outs


def _pad_to(v, n):
    return jnp.pad(v, (0, n - v.shape[0]))


def _round_up(n, m):
    return (n + m - 1) // m * m


def kernel(x, conv_w_pw1, conv_b_pw1, conv_w_dw, conv_b_dw, conv_ln_g, conv_ln_b, conv_w_pw2, conv_b_pw2, kv_w_k, kv_b_k, kv_w_v, kv_b_v, attn_w_q, attn_b_q, attn_sinks, attn_w_o, attn_b_o, ffn_w_gate, ffn_w_up, ffn_w_down, ln_mix_g, ln_mix_b, ln_ffn_g, ln_ffn_b, loss_target, m_conv_w_pw1, m_conv_b_pw1, m_conv_w_dw, m_conv_b_dw, m_conv_ln_g, m_conv_ln_b, m_conv_w_pw2, m_conv_b_pw2, m_kv_w_k, m_kv_b_k, m_kv_w_v, m_kv_b_v, m_attn_w_q, m_attn_b_q, m_attn_sinks, m_attn_w_o, m_attn_b_o, m_ffn_w_gate, m_ffn_w_up, m_ffn_w_down, m_ln_mix_g, m_ln_mix_b, m_ln_ffn_g, m_ln_ffn_b, v_conv_w_pw1, v_conv_b_pw1, v_conv_w_dw, v_conv_b_dw, v_conv_ln_g, v_conv_ln_b, v_conv_w_pw2, v_conv_b_pw2, v_kv_w_k, v_kv_b_k, v_kv_w_v, v_kv_b_v, v_attn_w_q, v_attn_b_q, v_attn_sinks, v_attn_w_o, v_attn_b_o, v_ffn_w_gate, v_ffn_w_up, v_ffn_w_down, v_ln_mix_g, v_ln_mix_b, v_ln_ffn_g, v_ln_ffn_b):
    args = dict(locals())
    w = {n: args[n] for n in WEIGHTS}
    mom = {n: args["m_" + n] for n in WEIGHTS}
    var = {n: args["v_" + n] for n in WEIGHTS}
    assert x.shape[0] == 1, "one sequence per device"
    t, d = x.shape[1], x.shape[2]
    dq = d // 4
    fs = ffn_w_gate.shape[-1]
    nq = d // HEAD_DIM
    x0 = x.reshape(t, d)
    target = loss_target.reshape(t, d)
    tm_big = min(512, t)
    tm_mid = min(256, t)
    tm_tn = min(1024, t)
    c_idx = lax.axis_index("c")

    me_idx = 2 * lax.axis_index("x") + lax.axis_index("y")

    def gather_buffer(v):
        buf = lax.empty((N_CHIPS,) + v.shape, v.dtype)
        return lax.dynamic_update_slice(buf, v[None], (me_idx,) + (0,) * v.ndim)

    def halves(v):
        return v.reshape(2, -1, v.shape[-1])

    small_sizes = [int(w[n].size) for n in SMALL_SHARDED]
    rs = _round_up(sum(small_sizes), 8 * 128) // 128
    spack = _pad_to(jnp.concatenate([w[n].reshape(-1) for n in SMALL_SHARDED]), rs * 128).reshape(rs, 128)
    conv_first = ['conv_w_pw1', 'conv_w_pw2']
    later = [n for n in BIG if n not in conv_first]
    (first_out,) = _run_riders(
        [_all_gather_rider([gather_buffer(halves(w[n].astype(BF16))) for n in conv_first], gather_buffer(spack))],
        "all_gather_conv")
    later_rider = _all_gather_rider([gather_buffer(halves(w[n].astype(BF16))) for n in later])
    gs = first_out[-1].reshape(N_CHIPS, rs * 128)
    full = {n: g.reshape((N_CHIPS,) + w[n].shape) for n, g in zip(conv_first, first_out)}
    off = 0
    for n, size in zip(SMALL_SHARDED, small_sizes):
        full[n] = gs[:, off:off + size].reshape((N_CHIPS,) + w[n].shape)
        off += size
    w1s = full['conv_w_pw1'].reshape(N_CHIPS, d, d // 2)
    w2 = full['conv_w_pw2'].reshape(d, d)
    b1 = full['conv_b_pw1'].reshape(1, 2 * d)
    wdw = jnp.pad(full['conv_w_dw'].reshape(N_CHIPS, CONV_WIDTH, dq).transpose(1, 0, 2).reshape(CONV_WIDTH, d),
                  ((0, CONV_HALO - CONV_WIDTH), (0, 0)))
    wdw = jnp.broadcast_to(wdw[:, None, :], (CONV_HALO, 8, d))
    bdw = full['conv_b_dw'].reshape(1, d)
    clng = full['conv_ln_g'].reshape(1, d)
    clnb = full['conv_ln_b'].reshape(1, d)
    b2 = full['conv_b_pw2'].reshape(1, d)
    bkv = jnp.concatenate([kv_b_k, kv_b_v]).reshape(1, 2 * KVD)
    sinks = attn_sinks.reshape(nq)
    mixg = [ln_mix_g[l].reshape(1, d) for l in range(DEPTH)]
    mixb = [ln_mix_b[l].reshape(1, d) for l in range(DEPTH)]
    ffng = [ln_ffn_g[l].reshape(1, d) for l in range(DEPTH)]
    ffnb = [ln_ffn_b[l].reshape(1, d) for l in range(DEPTH)]

    a_act, g_act, u_act = _fwd_pw1_glu(x0, w1s, b1, tm_big)
    (c_act, z1, x1), (later_out,) = _fwd_conv_tail(u_act, x0, wdw, bdw, clng, clnb, w2, b2, mixg[0], mixb[0], tm_mid,
                                                   riders=[later_rider])
    full.update({n: g.reshape((N_CHIPS,) + w[n].shape) for n, g in zip(later, later_out)})
    wkv = jnp.concatenate([full['kv_w_k'].reshape(d, KVD), full['kv_w_v'].reshape(d, KVD)], axis=1)
    wqt = full['attn_w_q'].reshape(d, d).T
    wo = full['attn_w_o'].reshape(d, d)
    wg, wu, wd = full['ffn_w_gate'], full['ffn_w_up'], full['ffn_w_down']
    act0, bm0, hm0, z2, x2 = _fwd_ffn(x1, wg, wu, wd, 0, ffng[0], ffnb[0], tm_big)
    qt_act, kv_act, ot_act, z3, x3 = _fwd_attn(x2, wqt, attn_b_q.reshape(d, 1), wkv, bkv, sinks, wo, attn_b_o,
                                               mixg[1], mixb[1], tm_big)
    act1, bm1, hm1, dz4, d_fg1, d_fb1, loss_part = _fwd_ffn(x3, wg, wu, wd, 1, ffng[1], ffnb[1], tm_big, target=target)
    loss = lax.psum(loss_part[0, 0], ("x", "y", "c"))

    c_arr = c_idx.reshape(1).astype(jnp.int32)

    def halves4(v):
        return v.reshape(N_CHIPS, 2, -1, v.shape[-1])

    def arrays(group):
        return [p for _, p in group]

    def pair_sums(group, got):
        return _pair_sums(arrays(group), got, c_arr, "grad_pair_sum_" + group[0][0])

    pos_arr = jnp.stack([me_idx, c_idx]).astype(jnp.int32)

    def chip_sums(group, sums, got):
        return _chip_sums(sums, got, pos_arr, "grad_chip_sum_" + group[0][0])

    (dgg1, duu1, dz3, d_mg1, d_mb1, d_bo), _ = _bwd_ffn_dx(dz4, act1, bm1, wg, wu, wd, 1, z3, mixg[1], tm_big)
    g1 = [("ffn_w_gate1", halves4(_matmul_tn(x3[None], dgg1, tm_tn, "dw_gate1"))),
          ("ffn_w_up1", halves4(_matmul_tn(x3[None], duu1, tm_tn, "dw_up1"))),
          ("ffn_w_down1", halves4(_matmul_tn(hm1, dz4[None], tm_tn, "dw_down1")))]
    (dqt, dkv, dz2, d_bq, d_bkv, d_sinks, d_fg0, d_fb0), (got1,) = _bwd_attn(
        dz3, qt_act, kv_act, sinks, wo, wqt, wkv, z2, ffng[0], tm_big, riders=[_pair_exchange_rider(arrays(g1))])
    s1 = pair_sums(g1, got1)
    dwo = _matmul_nn(ot_act, dz3, tm_tn, "dw_o")
    dwq = _matmul_nn(dqt, x2, tm_tn, "dw_q").T
    dwkv = _matmul_tn(x2[None], dkv[None], tm_tn, "dw_kv")[0]
    g2 = [("attn_w_o", halves4(dwo)), ("attn_w_q", halves4(dwq)),
          ("kv_w_k", halves4(dwkv[:, 0:KVD])), ("kv_w_v", halves4(dwkv[:, KVD:2 * KVD]))]
    (dgg0, duu0, dz1, d_mg0, d_mb0, d_b2), (from_chips1, got2) = _bwd_ffn_dx(
        dz2, act0, bm0, wg, wu, wd, 0, z1, mixg[0], tm_big,
        riders=[_chip_scatter_rider(s1), _pair_exchange_rider(arrays(g2))])
    f1 = chip_sums(g1, s1, from_chips1)
    s2 = pair_sums(g2, got2)
    g3 = [("ffn_w_gate0", halves4(_matmul_tn(x1[None], dgg0, tm_tn, "dw_gate0"))),
          ("ffn_w_up0", halves4(_matmul_tn(x1[None], duu0, tm_tn, "dw_up0"))),
          ("ffn_w_down0", halves4(_matmul_tn(hm0, dz2[None], tm_tn, "dw_down0")))]
    (s_act, dc, d_clng, d_clnb), (got3, shared1) = _bwd_conv_head(
        dz1, c_act, w2, clng, clnb, tm_mid, riders=[_pair_exchange_rider(arrays(g3)), _pair_share_rider(f1)])
    s3 = pair_sums(g3, got3)
    dw2 = _matmul_tn(s_act[None], dz1[None], tm_tn, "dw_pw2")
    (dx0, dh1, d_b1, d_bdw, d_wdw), (from_chips2, from_chips3) = _bwd_conv_glu(
        dc, u_act, a_act, g_act, dz1, wdw, w1s, tm_mid, riders=[_chip_scatter_rider(s2), _chip_scatter_rider(s3)])
    f2 = chip_sums(g2, s2, from_chips2)
    f3 = chip_sums(g3, s3, from_chips3)
    dw1, grad_x = _matmul_tn(x0[None], dh1, tm_tn, "dw_pw1", carry=dx0)

    def rows4(v):
        return v.reshape(N_CHIPS, -1)

    def rep4(v):
        return jnp.broadcast_to(v.reshape(1, -1), (N_CHIPS, v.size))

    local = {
        'conv_b_pw1': rows4(d_b1),
        'conv_w_dw': rows4(d_wdw[0:CONV_WIDTH].reshape(CONV_WIDTH, N_CHIPS, dq).transpose(1, 0, 2)),
        'conv_b_dw': rows4(d_bdw), 'conv_ln_g': rows4(d_clng), 'conv_ln_b': rows4(d_clnb), 'conv_b_pw2': rows4(d_b2),
        'kv_b_k': rep4(d_bkv[:, 0:KVD]), 'kv_b_v': rep4(d_bkv[:, KVD:2 * KVD]), 'attn_b_q': rep4(d_bq),
        'attn_sinks': rep4(d_sinks), 'attn_b_o': rep4(d_bo),
        'ln_mix_g': rep4(jnp.concatenate([d_mg0, d_mg1])), 'ln_mix_b': rep4(jnp.concatenate([d_mb0, d_mb1])),
        'ln_ffn_g': rep4(jnp.concatenate([d_fg0, d_fg1])), 'ln_ffn_b': rep4(jnp.concatenate([d_fb0, d_fb1])),
    }
    n_small = sum(int(w[n].size) for n in SMALL)
    small_rows = _round_up(n_small, 2 * SUM_STEPS * 8 * 128) // 128
    small_local = jnp.concatenate([local[n] for n in SMALL], axis=1)
    small_local = jnp.pad(small_local, ((0, 0), (0, small_rows * 128 - n_small)))
    g4 = [("conv_w_pw1", halves4(dw1)), ("conv_w_pw2", halves4(dw2)),
          ("small", small_local.reshape(N_CHIPS, 2, small_rows // 2, 128))]
    got4, shared23 = _run_riders([_pair_exchange_rider(arrays(g4)), _pair_share_rider(f2 + f3)],
                                 "grad_pair_exchange_last")
    s4 = pair_sums(g4, got4)
    reduced = dict(zip([n for n, _ in g1], shared1))
    reduced.update(zip([n for n, _ in g2 + g3], shared23))
    for n in ('ffn_w_gate', 'ffn_w_up', 'ffn_w_down'):
        reduced[n] = jnp.stack([reduced[n + str(layer)].reshape(w[n].shape[1:]) for layer in range(DEPTH)])

    g_out, delta, new_m, new_v = {}, {}, {}, {}

    def adamw_matrices(names, name, riders=()):
        for n in names:
            g_out[n] = reduced[n].reshape(w[n].shape)
        two_d = [tuple(tree[n].reshape(-1, w[n].shape[-1]) for tree in (w, g_out, mom, var)) for n in names]
        results, rider_outs = _adamw(two_d, name, riders)
        for n, (dl, nm, nv) in zip(names, results):
            delta[n], new_m[n], new_v[n] = (r.reshape(w[n].shape) for r in (dl, nm, nv))
        return rider_outs

    (from_chips4,) = adamw_matrices([n for n in BIG if n not in conv_first], "adamw_attn_ffn",
                                    riders=[_chip_scatter_rider(s4)])
    f4 = chip_sums(g4, s4, from_chips4)
    (shared4,) = _run_riders([_pair_share_rider(f4)], "grad_pair_share_last")
    reduced.update(zip([n for n, _ in g4], shared4))
    adamw_matrices(conv_first, "adamw_conv")

    def pack_small(tree):
        return _pad_to(jnp.concatenate([tree[n].reshape(-1) for n in SMALL]), small_rows * 128).reshape(small_rows, 128)

    g_small = reduced['small'].reshape(small_rows, 128)
    ((dl, nm, nv),), _ = _adamw([(pack_small(w), g_small, pack_small(mom), pack_small(var))], "adamw_small")
    off = 0
    for n in SMALL:
        size, shape = int(w[n].size), w[n].shape
        for tree, flat in ((g_out, g_small), (delta, dl), (new_m, nm), (new_v, nv)):
            tree[n] = flat.reshape(-1)[off:off + size].reshape(shape)
        off += size

    return (loss, grad_x.reshape(x.shape), *[g_out[n] for n in WEIGHTS], *[delta[n] for n in WEIGHTS],
            *[new_m[n] for n in WEIGHTS], *[new_v[n] for n in WEIGHTS])
```

```python
import functools
import math

import jax
import jax.numpy as jnp
from jax import lax
from jax.experimental import pallas as pl
from jax.experimental.pallas import tpu as pltpu

F32 = jnp.float32
BF16 = jnp.bfloat16

DEPTH = 2
ALPHA = (2.0 * DEPTH) ** 0.25
LN_EPS = 1e-5
NEG_INF = -1e30
HEAD_DIM = 64
N_KV_HEADS = 2
KVD = N_KV_HEADS * HEAD_DIM
BLOCK = 128
CONV_WIDTH = 31
CONV_HALO = 32
ALIBI_MAX = 8.0
ADAM_LR, ADAM_B1, ADAM_B2, ADAM_EPS, ADAM_WD, ADAM_STEP = 0.001, 0.9, 0.999, 1e-08, 0.01, 10

N_CHIPS = 4
SUM_STEPS = 4
VMEM_LIMIT = 60 * 1024 * 1024
MESH = pl.DeviceIdType.MESH

NT_DIMS = (((1,), (1,)), ((), ()))
TN_DIMS = (((0,), (0,)), ((), ()))

WEIGHTS = ['conv_w_pw1', 'conv_b_pw1', 'conv_w_dw', 'conv_b_dw', 'conv_ln_g', 'conv_ln_b', 'conv_w_pw2', 'conv_b_pw2',
           'kv_w_k', 'kv_b_k', 'kv_w_v', 'kv_b_v', 'attn_w_q', 'attn_b_q', 'attn_sinks', 'attn_w_o', 'attn_b_o',
           'ffn_w_gate', 'ffn_w_up', 'ffn_w_down', 'ln_mix_g', 'ln_mix_b', 'ln_ffn_g', 'ln_ffn_b']
BIG = ['conv_w_pw1', 'conv_w_pw2', 'kv_w_k', 'kv_w_v', 'attn_w_q', 'attn_w_o', 'ffn_w_gate', 'ffn_w_up', 'ffn_w_down']
SMALL_SHARDED = ['conv_b_pw1', 'conv_w_dw', 'conv_b_dw', 'conv_ln_g', 'conv_ln_b', 'conv_b_pw2']
REPLICATED = ['kv_b_k', 'kv_b_v', 'attn_b_q', 'attn_sinks', 'attn_b_o', 'ln_mix_g', 'ln_mix_b', 'ln_ffn_g', 'ln_ffn_b']
SMALL = SMALL_SHARDED + REPLICATED


def _cparams(n_grid=1):
    return pltpu.CompilerParams(dimension_semantics=("arbitrary",) * n_grid, vmem_limit_bytes=VMEM_LIMIT)


def _rows(tm, width):
    return pl.BlockSpec((tm, width), lambda i: (i, 0))


def _const(shape):
    return pl.BlockSpec(shape, lambda *_: (0,) * len(shape), pipeline_mode=pl.Buffered(1))


def _acc_out(shape):
    return pl.BlockSpec(shape, lambda *_: (0,) * len(shape))


def _dot(a, b):
    return jnp.dot(a, b, preferred_element_type=F32)


def _dot_nt(a, b):
    return lax.dot_general(a, b, NT_DIMS, preferred_element_type=F32)


def _dot_tn(a, b):
    return lax.dot_general(a, b, TN_DIMS, preferred_element_type=F32)


def _colsum8(v):
    m, n = v.shape
    return jnp.sum(v.reshape(m // 8, 8, n), axis=0)


def _ln_stats(z):
    mu = jnp.mean(z, axis=-1, keepdims=True)
    zc = z - mu
    var = jnp.mean(zc * zc, axis=-1, keepdims=True)
    rstd = lax.rsqrt(var + LN_EPS)
    return zc * rstd, rstd


def _ln_fwd(z, g, b):
    zhat, _ = _ln_stats(z)
    return zhat * g + b


def _ln_bwd(dy, z, g):
    zhat, rstd = _ln_stats(z)
    dzh = dy * g
    m1 = jnp.mean(dzh, axis=-1, keepdims=True)
    m2 = jnp.mean(dzh * zhat, axis=-1, keepdims=True)
    return rstd * (dzh - m1 - zhat * m2), zhat


def _silu_and_grad(n):
    sg = jax.nn.sigmoid(n)
    return n * sg, sg * (1.0 + n * (1.0 - sg))


def _acc_init(i, *refs):
    @pl.when(i == 0)
    def _():
        for r in refs:
            r[...] = jnp.zeros_like(r)


def _mesh_pos():
    x, y, c = lax.axis_index("x"), lax.axis_index("y"), lax.axis_index("c")
    chips = [(1 - x, y), (x, 1 - y), (1 - x, 1 - y)]
    return x, y, c, chips


HBM_SPEC = pl.BlockSpec(memory_space=pltpu.HBM)


def _remote(src, dst, send_sems, recv_sems, k, to):
    return pltpu.make_async_remote_copy(src_ref=src, dst_ref=dst, send_sem=send_sems.at[k], recv_sem=recv_sems.at[k],
                                        device_id=to, device_id_type=MESH)


class _Rider:
    def __init__(self, operands, out_shapes, sem_shapes, start, finish, mid=None, in_place=False):
        self.operands, self.out_shapes, self.sem_shapes = list(operands), list(out_shapes), list(sem_shapes)
        self.start, self.finish, self.mid = start, finish, mid
        self.in_place = in_place


def _rider_aliases(riders, first_in, first_out):
    aliases, k_in, k_out = {}, first_in, first_out
    for r in riders:
        if r.in_place:
            aliases.update({k_in + k: k_out + k for k in range(len(r.operands))})
        k_in += len(r.operands)
        k_out += len(r.out_shapes)
    return aliases


def _split(refs, counts):
    parts, k = [], 0
    for n in counts:
        parts.append(refs[k:k + n])
        k += n
    return parts


def _rider_refs(riders, ins, outs, sems):
    return list(zip(riders, _split(ins, [len(r.operands) for r in riders]),
                    _split(outs, [len(r.out_shapes) for r in riders]),
                    _split(sems, [len(r.sem_shapes) for r in riders])))


def _tc_call(body, *, name, nt, in_specs, out_specs, out_shape, operands, scratch_shapes=(), riders=(), mid_frac=0.75):
    n_in, n_out, n_scr = len(in_specs), len(out_specs), len(scratch_shapes)
    r_ops = [o for r in riders for o in r.operands]
    r_outs = [o for r in riders for o in r.out_shapes]
    r_sems = [s for r in riders for s in r.sem_shapes]
    mid_step = min(max(int(nt * mid_frac), 0), nt - 1)

    def full(*refs):
        ins, r_in, outs, r_out, scr, r_sem = _split(refs, [n_in, len(r_ops), n_out, len(r_outs), n_scr, len(r_sems)])
        parts = _rider_refs(riders, r_in, r_out, r_sem)
        step = pl.program_id(0)

        @pl.when(step == 0)
        def _():
            for r, a, b, s in parts:
                r.start(a, b, s)

        body(*ins, *outs, *scr)

        @pl.when(step == mid_step)
        def _():
            for r, a, b, s in parts:
                if r.mid is not None:
                    r.mid(a, b, s)

        @pl.when(step == nt - 1)
        def _():
            for r, a, b, s in parts:
                r.finish(a, b, s)

    res = pl.pallas_call(
        full if riders else body, name=name, grid=(nt,), in_specs=list(in_specs) + [HBM_SPEC] * len(r_ops),
        out_specs=list(out_specs) + [HBM_SPEC] * len(r_outs), out_shape=list(out_shape) + r_outs,
        scratch_shapes=list(scratch_shapes) + r_sems, input_output_aliases=_rider_aliases(riders, n_in, n_out),
        compiler_params=_cparams(),
    )(*operands, *r_ops)
    return res[:n_out], _split(res[n_out:], [len(r.out_shapes) for r in riders])


def _run_riders(riders, name):
    r_ops = [o for r in riders for o in r.operands]
    r_outs = [o for r in riders for o in r.out_shapes]
    r_sems = [s for r in riders for s in r.sem_shapes]

    def body(*refs):
        r_in, r_out, r_sem = _split(refs, [len(r_ops), len(r_outs), len(r_sems)])
        parts = _rider_refs(riders, r_in, r_out, r_sem)
        for r, a, b, s in parts:
            r.start(a, b, s)
        for r, a, b, s in parts:
            if r.mid is not None:
                r.mid(a, b, s)
        for r, a, b, s in parts:
            r.finish(a, b, s)

    res = pl.pallas_call(body, name=name, out_shape=tuple(r_outs), in_specs=[HBM_SPEC] * len(r_ops),
                         out_specs=(HBM_SPEC,) * len(r_outs), scratch_shapes=r_sems,
                         input_output_aliases=_rider_aliases(riders, 0, 0))(*r_ops)
    return _split(list(res), [len(r.out_shapes) for r in riders])


def _all_gather_rider(bufs, small=None):
    n = len(bufs)
    n_small = 0 if small is None else 1

    def copies(outs, sems):
        send_sems, recv_sems = sems
        x, y, c, chips = _mesh_pos()
        me = 2 * x + y
        here, sibling = (x, y, c), (x, y, 1 - c)
        rows = [2 * cx + cy for cx, cy in chips]

        def big(p, k, chip_row, half, to):
            piece = outs[p].at[chip_row, half]
            return _remote(piece, piece, send_sems, recv_sems, 6 * p + k, to)

        first = [big(p, j, me, c, (cx, cy, c)) for p in range(n) for j, (cx, cy) in enumerate(chips)]
        landed = [big(p, j, rows[j], c, here) for p in range(n) for j in range(3)]
        passed = [big(p, 3 + j, rows[j], c, sibling) for p in range(n) for j in range(3)]
        arrivals = [big(p, 3 + j, rows[j], 1 - c, here) for p in range(n) for j in range(3)]
        if n_small:
            first = [_remote(outs[n].at[me], outs[n].at[me], send_sems, recv_sems, 6 * n + j, (cx, cy, c))
                     for j, (cx, cy) in enumerate(chips)] + first
            arrivals += [_remote(outs[n].at[rows[j]], outs[n].at[rows[j]], send_sems, recv_sems, 6 * n + j, here)
                         for j in range(3)]
        return first, landed, passed, arrivals

    def start(ins, outs, sems):
        for cp in copies(outs, sems)[0]:
            cp.start()

    def mid(ins, outs, sems):
        _, landed, passed, _ = copies(outs, sems)
        for got, fwd in zip(landed, passed):
            got.wait_recv()
            fwd.start()

    def finish(ins, outs, sems):
        first, _, passed, arrivals = copies(outs, sems)
        for cp in arrivals:
            cp.wait_recv()
        for cp in first + passed:
            cp.wait_send()

    operands = list(bufs) + ([small] if n_small else [])
    n_sem = 6 * n + 3 * n_small
    return _Rider(operands, [jax.ShapeDtypeStruct(o.shape, o.dtype) for o in operands],
                  [pltpu.SemaphoreType.DMA((n_sem,)), pltpu.SemaphoreType.DMA((n_sem,))], start, finish, mid,
                  in_place=True)


def _pair_exchange_rider(plist):
    n = len(plist)

    def copies(ins, outs, sems):
        x, y, c, _ = _mesh_pos()
        return [_remote(ins[k].at[:, 1 - c], outs[k], sems[0], sems[1], k, (x, y, 1 - c)) for k in range(n)]

    def start(ins, outs, sems):
        for cp in copies(ins, outs, sems):
            cp.start()

    def finish(ins, outs, sems):
        for cp in copies(ins, outs, sems):
            cp.wait()

    return _Rider(plist, [jax.ShapeDtypeStruct((p.shape[0],) + p.shape[2:], p.dtype) for p in plist],
                  [pltpu.SemaphoreType.DMA((n,)), pltpu.SemaphoreType.DMA((n,))], start, finish)


def _pair_sums(plist, gots, c, name):
    n = len(plist)

    def body(c_ref, *refs):
        for k in range(n):
            refs[2 * n + k][...] = refs[k][...] + refs[n + k][...]

    rows = [p.shape[2] // SUM_STEPS for p in plist]
    return pl.pallas_call(
        body, name=name, out_shape=[jax.ShapeDtypeStruct(g.shape, F32) for g in gots],
        grid_spec=pltpu.PrefetchScalarGridSpec(
            num_scalar_prefetch=1, grid=(N_CHIPS, SUM_STEPS),
            in_specs=[pl.BlockSpec((None, None, br, p.shape[3]), lambda j, i, c_ref: (j, c_ref[0], i, 0))
                      for p, br in zip(plist, rows)]
            + [pl.BlockSpec((None, br, p.shape[3]), lambda j, i, c_ref: (j, i, 0)) for p, br in zip(plist, rows)],
            out_specs=[pl.BlockSpec((None, br, p.shape[3]), lambda j, i, c_ref: (j, i, 0))
                       for p, br in zip(plist, rows)]),
        compiler_params=_cparams(2),
    )(c, *plist, *gots)


def _chip_scatter_rider(slist):
    n = len(slist)

    def copies(ins, outs, sems):
        send_sems, recv_sems = sems
        x, y, c, chips = _mesh_pos()
        sends = [_remote(ins[k].at[2 * cx + cy], outs[k].at[j], send_sems, recv_sems, 3 * k + j, (cx, cy, c))
                 for k in range(n) for j, (cx, cy) in enumerate(chips)]
        arrivals = [_remote(ins[k].at[0], outs[k].at[j], send_sems, recv_sems, 3 * k + j, (x, y, c))
                    for k in range(n) for j in range(3)]
        return sends, arrivals

    def start(ins, outs, sems):
        for cp in copies(ins, outs, sems)[0]:
            cp.start()

    def finish(ins, outs, sems):
        sends, arrivals = copies(ins, outs, sems)
        for cp in arrivals:
            cp.wait_recv()
        for cp in sends:
            cp.wait_send()

    return _Rider(slist, [jax.ShapeDtypeStruct((3,) + s.shape[1:], s.dtype) for s in slist],
                  [pltpu.SemaphoreType.DMA((3 * n,)), pltpu.SemaphoreType.DMA((3 * n,))], start, finish)


def _chip_sums(slist, gots, pos, name):
    n = len(slist)

    def body(pos_ref, *refs):
        me = pos_ref[0]
        for k in range(n):
            s_ref, got_ref, out_ref = refs[k], refs[n + k], refs[2 * n + k]
            total = None
            for chip in range(N_CHIPS):
                flip = jnp.bitwise_xor(me, chip)
                term = jnp.where(flip == 0, s_ref[...],
                                 jnp.where(flip == 2, got_ref[0], jnp.where(flip == 1, got_ref[1], got_ref[2])))
                total = term if total is None else total + term
            out_ref[...] = total

    rows = [s.shape[1] // SUM_STEPS for s in slist]
    return pl.pallas_call(
        body, name=name, out_shape=[jax.ShapeDtypeStruct((2,) + s.shape[1:], F32) for s in slist],
        grid_spec=pltpu.PrefetchScalarGridSpec(
            num_scalar_prefetch=1, grid=(SUM_STEPS,),
            in_specs=[pl.BlockSpec((None, br, s.shape[2]), lambda i, pos_ref: (pos_ref[0], i, 0))
                      for s, br in zip(slist, rows)]
            + [pl.BlockSpec((3, br, s.shape[2]), lambda i, pos_ref: (0, i, 0)) for s, br in zip(slist, rows)],
            out_specs=[pl.BlockSpec((None, br, s.shape[2]), lambda i, pos_ref: (pos_ref[1], i, 0))
                       for s, br in zip(slist, rows)]),
        compiler_params=_cparams(1),
    )(pos, *slist, *gots)


def _pair_share_rider(flist):
    n = len(flist)

    def copies(outs, sems):
        x, y, c, _ = _mesh_pos()
        sends = [_remote(outs[k].at[c], outs[k].at[c], sems[0], sems[1], k, (x, y, 1 - c)) for k in range(n)]
        arrivals = [_remote(outs[k].at[1 - c], outs[k].at[1 - c], sems[0], sems[1], k, (x, y, c)) for k in range(n)]
        return sends, arrivals

    def start(ins, outs, sems):
        for cp in copies(outs, sems)[0]:
            cp.start()

    def finish(ins, outs, sems):
        sends, arrivals = copies(outs, sems)
        for cp in arrivals:
            cp.wait_recv()
        for cp in sends:
            cp.wait_send()

    return _Rider(flist, [jax.ShapeDtypeStruct(f.shape, f.dtype) for f in flist],
                  [pltpu.SemaphoreType.DMA((n,)), pltpu.SemaphoreType.DMA((n,))], start, finish, in_place=True)


def _fwd_pw1_glu(x, w1s, b1, tm):
    t, d = x.shape
    dh = d // 2

    def body(x_ref, w_ref, b_ref, a_ref, g_ref, u_ref):
        xb = x_ref[...].astype(BF16)
        for hh in range(2):
            cs = slice(hh * dh, (hh + 1) * dh)
            a = _dot(xb, w_ref[hh]) + b_ref[:, hh * dh:(hh + 1) * dh]
            g = _dot(xb, w_ref[2 + hh]) + b_ref[:, d + hh * dh:d + (hh + 1) * dh]
            a_ref[:, cs] = a.astype(BF16)
            g_ref[:, cs] = g.astype(BF16)
            u_ref[:, cs] = a * jax.nn.sigmoid(g)

    return pl.pallas_call(
        body, name="fwd_pw1_glu", grid=(t // tm,),
        in_specs=[_rows(tm, d), _const((4, d, dh)), _const((1, 2 * d))],
        out_specs=[_rows(tm, d)] * 3,
        out_shape=[jax.ShapeDtypeStruct((t, d), BF16), jax.ShapeDtypeStruct((t, d), BF16),
                   jax.ShapeDtypeStruct((t, d), F32)],
        compiler_params=_cparams(),
    )(x, w1s, b1)


def _fill_shifted(sh_ref, ext_ref):
    n = sh_ref.shape[1]
    for s in range(1, 8):
        sh_ref[s - 1] = ext_ref[pl.ds(s, n), :]


def _ext_rows(ext, sh, e, base, rows, ls):
    if e % 8 == 0:
        return ext[pl.ds(base + e, rows), ls]
    return sh[e % 8 - 1, pl.ds(base + (e // 8) * 8, rows), ls]


CONV_CHUNK = 64
LANES = 256


def _tap_sum(w_ref, ext, sh, base, d, tap_row, out_ref, bias_ref=None):
    groups = CONV_CHUNK // 8
    for lg in range(d // LANES):
        ls = slice(lg * LANES, (lg + 1) * LANES)
        acc = jnp.zeros((groups, 8, LANES), F32)
        for k in range(CONV_WIDTH):
            x = _ext_rows(ext, sh, tap_row(k), base, CONV_CHUNK, ls)
            acc = acc + w_ref[k, :, ls] * x.reshape(groups, 8, LANES)
        acc = acc.reshape(CONV_CHUNK, LANES)
        out_ref[pl.ds(base, CONV_CHUNK), ls] = acc if bias_ref is None else acc + bias_ref[:, ls]


def _fwd_conv_tail(u, x0, wdw, bdw, lng, lnb, w2, b2, mixg, mixb, tm, riders=()):
    t, d = u.shape
    hb = tm // CONV_HALO

    def body(u_ref, uh_ref, x_ref, w_ref, bdw_ref, lng_ref, lnb_ref, w2_ref, b2_ref, mg_ref, mb_ref,
             c_ref, z_ref, y_ref, ext, sh):
        i = pl.program_id(0)
        ext[0:CONV_HALO] = jnp.where(i == 0, 0.0, uh_ref[...])
        ext[CONV_HALO:CONV_HALO + tm] = u_ref[...]
        ext[CONV_HALO + tm:CONV_HALO + tm + 8] = jnp.zeros((8, d), F32)
        _fill_shifted(sh, ext)

        def chunk(r, carry):
            base = pl.multiple_of(r * CONV_CHUNK, CONV_CHUNK)
            _tap_sum(w_ref, ext, sh, base, d, lambda k: k + CONV_HALO - (CONV_WIDTH - 1), c_ref, bdw_ref)
            return carry

        lax.fori_loop(0, tm // CONV_CHUNK, chunk, 0)
        n = _ln_fwd(c_ref[...], lng_ref[...], lnb_ref[...])
        s = n * jax.nn.sigmoid(n)
        m = _dot(s.astype(BF16), w2_ref[...]) + b2_ref[...]
        z = ALPHA * x_ref[...] + m
        z_ref[...] = z
        y_ref[...] = _ln_fwd(z, mg_ref[...], mb_ref[...])

    vec = _const((1, d))
    return _tc_call(
        body, name="fwd_conv_tail", nt=t // tm,
        in_specs=[_rows(tm, d), pl.BlockSpec((CONV_HALO, d), lambda i: (jnp.maximum(i * hb - 1, 0), 0)), _rows(tm, d),
                  _const((CONV_HALO, 8, d)), vec, vec, vec, _const((d, d)), vec, vec, vec],
        out_specs=[_rows(tm, d)] * 3,
        out_shape=[jax.ShapeDtypeStruct((t, d), F32)] * 3,
        scratch_shapes=[pltpu.VMEM((tm + CONV_HALO + 8, d), F32), pltpu.VMEM((7, tm + CONV_HALO, d), F32)],
        operands=(u, u, x0, wdw, bdw, lng, lnb, w2, b2, mixg, mixb), riders=riders)


def _fwd_ffn(x, wg, wu, wd, layer, lng, lnb, tm, target=None):
    t, d = x.shape
    fs = wg.shape[-1]
    nt = t // tm
    with_loss = target is not None

    def hidden(x_ref, wg_ref, wu_ref, wd_ref, act_ref, bm_ref, hm_ref):
        xv = x_ref[...]
        xb = xv.astype(BF16)
        f = jnp.zeros((tm, d), F32)
        for j in range(N_CHIPS):
            gj = _dot(xb, wg_ref[j])
            uj = _dot(xb, wu_ref[j])
            act, dact = _silu_and_grad(gj)
            act_ref[j] = act.astype(BF16)
            bm_ref[j] = (uj * dact).astype(BF16)
            hmb = (act * uj).astype(BF16)
            hm_ref[j] = hmb
            f = f + _dot(hmb, wd_ref[j])
        return ALPHA * xv + f

    def body(x_ref, wg_ref, wu_ref, wd_ref, g_ref, b_ref, act_ref, bm_ref, hm_ref, z_ref, y_ref):
        z = hidden(x_ref, wg_ref, wu_ref, wd_ref, act_ref, bm_ref, hm_ref)
        z_ref[...] = z
        y_ref[...] = _ln_fwd(z, g_ref[...], b_ref[...])

    def body_loss(x_ref, wg_ref, wu_ref, wd_ref, g_ref, b_ref, t_ref, act_ref, bm_ref, hm_ref, dz_ref,
                  dlg_ref, dlb_ref, loss_ref, acc_g, acc_b, acc_l):
        i = pl.program_id(0)
        _acc_init(i, acc_g, acc_b, acc_l)
        z = hidden(x_ref, wg_ref, wu_ref, wd_ref, act_ref, bm_ref, hm_ref)
        zhat, rstd = _ln_stats(z)
        gain = g_ref[...]
        err = zhat * gain + b_ref[...] - t_ref[...]
        acc_l[...] += _colsum8(err * err)
        dy = err * (1.0 / d)
        acc_g[...] += _colsum8(dy * zhat)
        acc_b[...] += _colsum8(dy)
        dzh = dy * gain
        m1 = jnp.mean(dzh, axis=-1, keepdims=True)
        m2 = jnp.mean(dzh * zhat, axis=-1, keepdims=True)
        dz_ref[...] = rstd * (dzh - m1 - zhat * m2)

        @pl.when(i == nt - 1)
        def _():
            dlg_ref[...] = jnp.sum(acc_g[...], axis=0, keepdims=True)
            dlb_ref[...] = jnp.sum(acc_b[...], axis=0, keepdims=True)
            loss_ref[...] = jnp.sum(acc_l[...], keepdims=True) * (0.5 / d)

    wcol = pl.BlockSpec((N_CHIPS, None, d, fs), lambda i: (0, layer, 0, 0), pipeline_mode=pl.Buffered(1))
    wrow = pl.BlockSpec((N_CHIPS, None, fs, d), lambda i: (0, layer, 0, 0), pipeline_mode=pl.Buffered(1))
    hid = pl.BlockSpec((N_CHIPS, tm, fs), lambda i: (0, i, 0))
    in_specs = [_rows(tm, d), wcol, wcol, wrow, _const((1, d)), _const((1, d))]
    hid_shapes = [jax.ShapeDtypeStruct((N_CHIPS, t, fs), BF16)] * 3
    if not with_loss:
        return pl.pallas_call(
            body, name=f"fwd_ffn{layer}", grid=(nt,), in_specs=in_specs,
            out_specs=[hid, hid, hid, _rows(tm, d), _rows(tm, d)],
            out_shape=hid_shapes + [jax.ShapeDtypeStruct((t, d), F32)] * 2, compiler_params=_cparams(),
        )(x, wg, wu, wd, lng, lnb)
    return pl.pallas_call(
        body_loss, name=f"fwd_ffn{layer}_loss", grid=(nt,), in_specs=in_specs + [_rows(tm, d)],
        out_specs=[hid, hid, hid, _rows(tm, d), _acc_out((1, d)), _acc_out((1, d)), _acc_out((1, 1))],
        out_shape=hid_shapes + [jax.ShapeDtypeStruct((t, d), F32)] + [jax.ShapeDtypeStruct((1, d), F32)] * 2
        + [jax.ShapeDtypeStruct((1, 1), F32)],
        scratch_shapes=[pltpu.VMEM((8, d), F32)] * 3, compiler_params=_cparams(),
    )(x, wg, wu, wd, lng, lnb, target)


def _attn_band():
    kt = lax.broadcasted_iota(jnp.int32, (BLOCK, BLOCK), 0)
    qi = lax.broadcasted_iota(jnp.int32, (BLOCK, BLOCK), 1)
    current = kt <= qi
    delta = qi - kt + jnp.where(current, 0, BLOCK)
    return current, delta.astype(F32)


def _fold(full, current):
    return jnp.where(current, full[BLOCK:2 * BLOCK], full[0:BLOCK])


def _unfold(folded, current):
    zero = jnp.zeros_like(folded)
    return jnp.concatenate([jnp.where(current, zero, folded), jnp.where(current, folded, zero)], axis=0)


def _slope(h, nq):
    return 2.0 ** (-ALIBI_MAX * (h + 1) / nq)


def _softmax_with_sink(s_full, slope, band, has_previous, sink):
    current, delta = band
    s = _fold(s_full, current) * (1.0 / math.sqrt(HEAD_DIM)) - slope * delta
    s = jnp.where(jnp.logical_or(current, has_previous), s, NEG_INF)
    m = jnp.maximum(jnp.max(s, axis=0, keepdims=True), sink)
    p = jnp.exp(s - m)
    e_sink = jnp.exp(sink - m)
    inv = 1.0 / (jnp.sum(p, axis=0, keepdims=True) + e_sink)
    return p * inv, e_sink * inv


def _heads_on_lanes(ref, b, g, group):
    first = g * group
    return jnp.concatenate([ref[b, (first + hh) * HEAD_DIM:(first + hh + 1) * HEAD_DIM, :] for hh in range(group)],
                           axis=1)


def _fill_kv(kv_scr, halo, tile, tm):
    for j in range(2 * N_KV_HEADS):
        kv_scr[j, 0:BLOCK] = halo[:, j * HEAD_DIM:(j + 1) * HEAD_DIM]
        kv_scr[j, BLOCK:BLOCK + tm] = tile[:, j * HEAD_DIM:(j + 1) * HEAD_DIM]


def _cols(d, tm):
    return pl.BlockSpec((d, tm), lambda i: (0, i))


def _fwd_attn(x, wqt, bqt, wkv, bkv, sinks, wo, bo, mixg, mixb, tm):
    t, d = x.shape
    nq = d // HEAD_DIM
    group = nq // N_KV_HEADS
    nb = tm // BLOCK

    def body(sink_ref, x_ref, xh_ref, wqt_ref, bqt_ref, wkv_ref, bkv_ref, wo_ref, bo_ref, mg_ref, mb_ref,
             qt_ref, kv_ref, ot_ref, z_ref, y_ref, kv_scr, qt_scr, ot_scr):
        i = pl.program_id(0)
        xv = x_ref[...]
        xb = xv.astype(BF16)
        qt = (_dot_nt(wqt_ref[...], xb) + bqt_ref[...]).astype(BF16)
        qt_ref[...] = qt
        for b in range(nb):
            qt_scr[b] = qt[:, b * BLOCK:(b + 1) * BLOCK]
        kvb = (_dot(xb, wkv_ref[...]) + bkv_ref[...]).astype(BF16)
        kv_ref[...] = kvb
        _fill_kv(kv_scr, (_dot(xh_ref[...].astype(BF16), wkv_ref[...]) + bkv_ref[...]).astype(BF16), kvb, tm)
        band = _attn_band()

        def block(b, carry):
            r0 = pl.multiple_of(b * BLOCK, BLOCK)
            has_previous = jnp.logical_or(i > 0, b > 0)
            for g in range(N_KV_HEADS):
                kk = kv_scr[g, pl.ds(r0, 2 * BLOCK), :]
                vv = kv_scr[N_KV_HEADS + g, pl.ds(r0, 2 * BLOCK), :]
                s_all = _dot(kk, _heads_on_lanes(qt_scr, b, g, group))
                probs = []
                for hh in range(group):
                    h = g * group + hh
                    p, _ = _softmax_with_sink(s_all[:, hh * BLOCK:(hh + 1) * BLOCK], _slope(h, nq), band,
                                              has_previous, sink_ref[h])
                    probs.append(_unfold(p.astype(BF16), band[0]))
                o_all = _dot_tn(vv, jnp.concatenate(probs, axis=1))
                for hh in range(group):
                    h = g * group + hh
                    ot_scr[b, h * HEAD_DIM:(h + 1) * HEAD_DIM, :] = o_all[:, hh * BLOCK:(hh + 1) * BLOCK].astype(BF16)
            return carry

        lax.fori_loop(0, nb, block, 0)
        ot = jnp.concatenate([ot_scr[b] for b in range(nb)], axis=1)
        ot_ref[...] = ot
        z = ALPHA * xv + _dot_tn(ot, wo_ref[...]) + bo_ref[...]
        z_ref[...] = z
        y_ref[...] = _ln_fwd(z, mg_ref[...], mb_ref[...])

    hb = tm // BLOCK
    vec = _const((1, d))
    return pl.pallas_call(
        body, name="fwd_attn", grid=(t // tm,),
        in_specs=[pl.BlockSpec(memory_space=pltpu.SMEM),
                  _rows(tm, d), pl.BlockSpec((BLOCK, d), lambda i: (jnp.maximum(i * hb - 1, 0), 0)),
                  _const((d, d)), _const((d, 1)), _const((d, 2 * KVD)), _const((1, 2 * KVD)), _const((d, d)), vec, vec,
                  vec],
        out_specs=[_cols(d, tm), _rows(tm, 2 * KVD), _cols(d, tm), _rows(tm, d), _rows(tm, d)],
        out_shape=[jax.ShapeDtypeStruct((d, t), BF16), jax.ShapeDtypeStruct((t, 2 * KVD), BF16),
                   jax.ShapeDtypeStruct((d, t), BF16), jax.ShapeDtypeStruct((t, d), F32),
                   jax.ShapeDtypeStruct((t, d), F32)],
        scratch_shapes=[pltpu.VMEM((2 * N_KV_HEADS, tm + BLOCK, HEAD_DIM), BF16), pltpu.VMEM((nb, d, BLOCK), BF16),
                        pltpu.VMEM((nb, d, BLOCK), BF16)],
        compiler_params=_cparams(),
    )(sinks, x, x, wqt, bqt, wkv, bkv, wo, bo, mixg, mixb)


def _write_sums(i, nt, pairs):
    @pl.when(i == nt - 1)
    def _():
        for out_ref, acc in pairs:
            out_ref[...] = jnp.sum(acc[...], axis=0, keepdims=True)


def _bwd_ffn_dx(dz, act, bm, wg, wu, wd, layer, z_in, g_in, tm, riders=()):
    t, d = dz.shape
    fs = wg.shape[-1]
    nt = t // tm

    def body(dz_ref, act_ref, bm_ref, wg_ref, wu_ref, wd_ref, zin_ref, gin_ref,
             dgg_ref, duu_ref, dzin_ref, dg_ref, db_ref, dsum_ref, acc_g, acc_b, acc_s):
        i = pl.program_id(0)
        _acc_init(i, acc_g, acc_b, acc_s)
        dzv = dz_ref[...]
        dzb = dzv.astype(BF16)
        dx = ALPHA * dzv
        for j in range(N_CHIPS):
            dh = _dot_nt(dzb, wd_ref[j])
            dgb = (dh * bm_ref[j].astype(F32)).astype(BF16)
            dub = (dh * act_ref[j].astype(F32)).astype(BF16)
            dgg_ref[j] = dgb
            duu_ref[j] = dub
            dx = dx + _dot_nt(dgb, wg_ref[j]) + _dot_nt(dub, wu_ref[j])
        dz_in, zhat = _ln_bwd(dx, zin_ref[...], gin_ref[...])
        acc_g[...] += _colsum8(dx * zhat)
        acc_b[...] += _colsum8(dx)
        acc_s[...] += _colsum8(dz_in)
        dzin_ref[...] = dz_in
        _write_sums(i, nt, [(dg_ref, acc_g), (db_ref, acc_b), (dsum_ref, acc_s)])

    wcol = pl.BlockSpec((N_CHIPS, None, d, fs), lambda i: (0, layer, 0, 0), pipeline_mode=pl.Buffered(1))
    wrow = pl.BlockSpec((N_CHIPS, None, fs, d), lambda i: (0, layer, 0, 0), pipeline_mode=pl.Buffered(1))
    hid = pl.BlockSpec((N_CHIPS, tm, fs), lambda i: (0, i, 0))
    return _tc_call(
        body, name=f"bwd_ffn_dx{layer}", nt=nt,
        in_specs=[_rows(tm, d), hid, hid, wcol, wcol, wrow, _rows(tm, d), _const((1, d))],
        out_specs=[hid, hid, _rows(tm, d)] + [_acc_out((1, d))] * 3,
        out_shape=[jax.ShapeDtypeStruct((N_CHIPS, t, fs), BF16)] * 2 + [jax.ShapeDtypeStruct((t, d), F32)]
        + [jax.ShapeDtypeStruct((1, d), F32)] * 3,
        scratch_shapes=[pltpu.VMEM((8, d), F32)] * 3,
        operands=(dz, act, bm, wg, wu, wd, z_in, g_in), riders=riders)


def _matmul_tn(a, b, tt, name, carry=None):
    ja, t, ka = a.shape
    jb, _, nb = b.shape
    nj = max(ja, jb)

    def body(a_ref, b_ref, *rest):
        o_ref = rest[-1] if carry is None else rest[1]

        @pl.when(pl.program_id(0) == 0)
        def _():
            o_ref[...] = jnp.zeros_like(o_ref)

        a0 = a_ref[0].astype(BF16) if ja == 1 else None
        b0 = b_ref[0].astype(BF16) if jb == 1 else None
        for j in range(nj):
            aj = a0 if ja == 1 else a_ref[j].astype(BF16)
            bj = b0 if jb == 1 else b_ref[j].astype(BF16)
            o_ref[j] += _dot_tn(aj, bj)
        if carry is not None:
            rest[2][...] = rest[0][...]

    in_specs = [pl.BlockSpec((ja, tt, ka), lambda i: (0, i, 0)), pl.BlockSpec((jb, tt, nb), lambda i: (0, i, 0))]
    out_specs = [pl.BlockSpec((nj, ka, nb), lambda i: (0, 0, 0))]
    out_shape = [jax.ShapeDtypeStruct((nj, ka, nb), F32)]
    operands = [a, b]
    if carry is not None:
        in_specs.append(_rows(tt, carry.shape[1]))
        out_specs.append(_rows(tt, carry.shape[1]))
        out_shape.append(jax.ShapeDtypeStruct(carry.shape, carry.dtype))
        operands.append(carry)
    res = pl.pallas_call(body, name=name, grid=(t // tt,), in_specs=in_specs, out_specs=out_specs,
                         out_shape=out_shape, compiler_params=_cparams())(*operands)
    return res[0] if carry is None else (res[0], res[1])


def _matmul_nn(at, b, tt, name):
    ka, t = at.shape
    nb = b.shape[1]

    def body(a_ref, b_ref, o_ref):
        @pl.when(pl.program_id(0) == 0)
        def _():
            o_ref[...] = jnp.zeros_like(o_ref)

        o_ref[...] += _dot(a_ref[...].astype(BF16), b_ref[...].astype(BF16))

    return pl.pallas_call(
        body, name=name, grid=(t // tt,),
        in_specs=[pl.BlockSpec((ka, tt), lambda i: (0, i)), pl.BlockSpec((tt, nb), lambda i: (i, 0))],
        out_specs=pl.BlockSpec((ka, nb), lambda i: (0, 0)), out_shape=jax.ShapeDtypeStruct((ka, nb), F32),
        compiler_params=_cparams(1),
    )(at, b)


def _bwd_attn(dz_all, qt, kv, sinks, wo, wqt, wkv, z_in, g_in, tm, riders=()):
    t, d = dz_all.shape
    nq = d // HEAD_DIM
    group = nq // N_KV_HEADS
    nb = tm // BLOCK
    nt = t // tm
    hb = tm // BLOCK
    n_kv = 2 * N_KV_HEADS

    def body(sink_ref, dz_ref, qt_ref, kv_ref, kvh_ref, wo_ref, wqt_ref, wkv_ref, zin_ref, gin_ref,
             dqt_ref, dkv_ref, dx_ref, dbq_ref, dbkv_ref, dsink_ref, ding_ref, dinb_ref,
             kv_scr, dkv_scr, qt_scr, dot_scr, dqt_scr, carry, acc_q, acc_kv, acc_s, acc_ig, acc_ib):
        i = pl.program_id(0)
        ti = nt - 1 - i
        _acc_init(i, carry, acc_q, acc_kv, acc_s, acc_ig, acc_ib)
        dz = dz_ref[...]
        do_t = _dot_nt(wo_ref[...], dz.astype(BF16)).astype(BF16)
        for b in range(nb):
            dot_scr[b] = do_t[:, b * BLOCK:(b + 1) * BLOCK]
            qt_scr[b] = qt_ref[:, b * BLOCK:(b + 1) * BLOCK]
        _fill_kv(kv_scr, kvh_ref[...], kv_ref[...], tm)
        dkv_scr[:, 0:tm] = jnp.zeros((n_kv, tm, HEAD_DIM), F32)
        dkv_scr[:, tm:tm + BLOCK] = carry[...]
        band = _attn_band()

        def block(b, c):
            r0 = pl.multiple_of(b * BLOCK, BLOCK)
            has_previous = jnp.logical_or(ti > 0, b > 0)
            for g in range(N_KV_HEADS):
                kk = kv_scr[g, pl.ds(r0, 2 * BLOCK), :]
                vv = kv_scr[N_KV_HEADS + g, pl.ds(r0, 2 * BLOCK), :]
                q_all = _heads_on_lanes(qt_scr, b, g, group)
                do_all = _heads_on_lanes(dot_scr, b, g, group)
                s_all = _dot(kk, q_all)
                dp_all = _dot(vv, do_all)
                probs, dscores = [], []
                for hh in range(group):
                    h = g * group + hh
                    cols = slice(hh * BLOCK, (hh + 1) * BLOCK)
                    p, p_sink = _softmax_with_sink(s_all[:, cols], _slope(h, nq), band, has_previous, sink_ref[h])
                    dp = _fold(dp_all[:, cols], band[0])
                    rs = jnp.sum(p * dp, axis=0, keepdims=True)
                    acc_s[h:h + 1, :] += -(p_sink * rs)
                    ds = p * (dp - rs) * (1.0 / math.sqrt(HEAD_DIM))
                    probs.append(_unfold(p.astype(BF16), band[0]))
                    dscores.append(_unfold(ds.astype(BF16), band[0]))
                p_all = jnp.concatenate(probs, axis=1)
                ds_all = jnp.concatenate(dscores, axis=1)
                dq_all = _dot_tn(kk, ds_all)
                for hh in range(group):
                    h = g * group + hh
                    dqt_scr[b, h * HEAD_DIM:(h + 1) * HEAD_DIM, :] = dq_all[:, hh * BLOCK:(hh + 1) * BLOCK]
                dkv_scr[g, pl.ds(r0, 2 * BLOCK), :] += _dot_nt(ds_all, q_all)
                dkv_scr[N_KV_HEADS + g, pl.ds(r0, 2 * BLOCK), :] += _dot_nt(p_all, do_all)
            return c

        lax.fori_loop(0, nb, block, 0)
        carry[...] = dkv_scr[:, 0:BLOCK]
        dkv = jnp.concatenate([dkv_scr[j, BLOCK:BLOCK + tm] for j in range(n_kv)], axis=1)
        acc_kv[...] += _colsum8(dkv)
        dkvb = dkv.astype(BF16)
        dkv_ref[...] = dkvb
        dqt = jnp.concatenate([dqt_scr[b] for b in range(nb)], axis=1)
        for b in range(nb):
            acc_q[...] += dqt_scr[b]
        dqtb = dqt.astype(BF16)
        dqt_ref[...] = dqtb
        dx = ALPHA * dz + _dot_tn(dqtb, wqt_ref[...]) + _dot_nt(dkvb, wkv_ref[...])
        dz_in, zhat_in = _ln_bwd(dx, zin_ref[...], gin_ref[...])
        acc_ig[...] += _colsum8(dx * zhat_in)
        acc_ib[...] += _colsum8(dx)
        dx_ref[...] = dz_in
        _write_sums(i, nt, [(dbkv_ref, acc_kv), (ding_ref, acc_ig), (dinb_ref, acc_ib)])

        @pl.when(i == nt - 1)
        def _():
            dbq_ref[...] = jnp.sum(acc_q[...], axis=1, keepdims=True)
            dsink_ref[...] = jnp.sum(acc_s[...], axis=1, keepdims=True)

    rev = lambda w: pl.BlockSpec((tm, w), lambda i: (nt - 1 - i, 0))
    rev_cols = pl.BlockSpec((d, tm), lambda i: (0, nt - 1 - i))
    vec = _const((1, d))
    return _tc_call(
        body, name="bwd_attn", nt=nt,
        in_specs=[pl.BlockSpec(memory_space=pltpu.SMEM), rev(d), rev_cols, rev(2 * KVD),
                  pl.BlockSpec((BLOCK, 2 * KVD), lambda i: (jnp.maximum((nt - 1 - i) * hb - 1, 0), 0)),
                  _const((d, d)), _const((d, d)), _const((d, 2 * KVD)), rev(d), vec],
        out_specs=[rev_cols, rev(2 * KVD), rev(d), _acc_out((d, 1)), _acc_out((1, 2 * KVD)), _acc_out((nq, 1)),
                   _acc_out((1, d)), _acc_out((1, d))],
        out_shape=[jax.ShapeDtypeStruct((d, t), BF16), jax.ShapeDtypeStruct((t, 2 * KVD), BF16),
                   jax.ShapeDtypeStruct((t, d), F32), jax.ShapeDtypeStruct((d, 1), F32),
                   jax.ShapeDtypeStruct((1, 2 * KVD), F32), jax.ShapeDtypeStruct((nq, 1), F32)]
        + [jax.ShapeDtypeStruct((1, d), F32)] * 2,
        scratch_shapes=[pltpu.VMEM((n_kv, tm + BLOCK, HEAD_DIM), BF16), pltpu.VMEM((n_kv, tm + BLOCK, HEAD_DIM), F32),
                        pltpu.VMEM((nb, d, BLOCK), BF16), pltpu.VMEM((nb, d, BLOCK), BF16),
                        pltpu.VMEM((nb, d, BLOCK), F32), pltpu.VMEM((n_kv, BLOCK, HEAD_DIM), F32),
                        pltpu.VMEM((d, BLOCK), F32), pltpu.VMEM((8, 2 * KVD), F32), pltpu.VMEM((nq, BLOCK), F32),
                        pltpu.VMEM((8, d), F32), pltpu.VMEM((8, d), F32)],
        operands=(sinks, dz_all, qt, kv, kv, wo, wqt, wkv, z_in, g_in), riders=riders)


def _bwd_conv_head(dz, c, w2, lng, lnb, tm, riders=()):
    t, d = dz.shape
    nt = t // tm

    def body(dz_ref, c_ref, w2_ref, lg_ref, lb_ref, s_ref, dc_ref, dlg_ref, dlb_ref, a3, a4):
        i = pl.program_id(0)
        _acc_init(i, a3, a4)
        dz = dz_ref[...]
        chat, rstd = _ln_stats(c_ref[...])
        n = chat * lg_ref[...] + lb_ref[...]
        act, dact = _silu_and_grad(n)
        s_ref[...] = act.astype(BF16)
        dn = _dot_nt(dz.astype(BF16), w2_ref[...]) * dact
        a3[...] += _colsum8(dn * chat)
        a4[...] += _colsum8(dn)
        dch = dn * lg_ref[...]
        m1 = jnp.mean(dch, axis=-1, keepdims=True)
        m2 = jnp.mean(dch * chat, axis=-1, keepdims=True)
        dc_ref[...] = rstd * (dch - m1 - chat * m2)
        _write_sums(i, nt, [(dlg_ref, a3), (dlb_ref, a4)])

    vec = _const((1, d))
    return _tc_call(
        body, name="bwd_conv_head", nt=nt,
        in_specs=[_rows(tm, d), _rows(tm, d), _const((d, d)), vec, vec],
        out_specs=[_rows(tm, d), _rows(tm, d)] + [_acc_out((1, d))] * 2,
        out_shape=[jax.ShapeDtypeStruct((t, d), BF16), jax.ShapeDtypeStruct((t, d), F32)]
        + [jax.ShapeDtypeStruct((1, d), F32)] * 2,
        scratch_shapes=[pltpu.VMEM((8, d), F32)] * 2, operands=(dz, c, w2, lng, lnb), riders=riders)


def _bwd_conv_glu(dc, u, a, g, dz, wdw, w1s, tm, riders=()):
    t, d = dc.shape
    dh_w = d // 2
    nt = t // tm
    hb = tm // CONV_HALO
    last_halo = t // CONV_HALO - 1

    def body(dc_ref, dcn_ref, u_ref, a_ref, g_ref, dz_ref, w_ref, w1_ref,
             dx_ref, dh_ref, db1_ref, dbdw_ref, dw_ref, ext, sh, du_scr, acc_b1, acc_bdw, acc_w):
        i = pl.program_id(0)
        _acc_init(i, acc_b1, acc_bdw, acc_w)
        dcv = dc_ref[...]
        acc_bdw[...] += _colsum8(dcv)

        ext[0:tm] = dcv
        ext[tm:tm + CONV_HALO] = jnp.where(i == nt - 1, 0.0, dcn_ref[...])
        ext[tm + CONV_HALO:tm + CONV_HALO + 8] = jnp.zeros((8, d), F32)
        _fill_shifted(sh, ext)

        def du_chunk(r, carry):
            base = pl.multiple_of(r * CONV_CHUNK, CONV_CHUNK)
            _tap_sum(w_ref, ext, sh, base, d, lambda k: CONV_WIDTH - 1 - k, du_scr)
            return carry

        lax.fori_loop(0, tm // CONV_CHUNK, du_chunk, 0)

        def dw_chunk(r, carry):
            base = pl.multiple_of(r * CONV_CHUNK, CONV_CHUNK)
            groups = CONV_CHUNK // 8
            for lg in range(d // LANES):
                ls = slice(lg * LANES, (lg + 1) * LANES)
                uv = u_ref[pl.ds(base, CONV_CHUNK), ls].reshape(groups, 8, LANES)
                for k in range(CONV_WIDTH):
                    x = _ext_rows(ext, sh, CONV_WIDTH - 1 - k, base, CONV_CHUNK, ls).reshape(groups, 8, LANES)
                    acc_w[k, :, ls] += jnp.sum(uv * x, axis=0)
            return carry

        lax.fori_loop(0, tm // CONV_CHUNK, dw_chunk, 0)

        du = du_scr[...]
        av = a_ref[...].astype(F32)
        sg = jax.nn.sigmoid(g_ref[...].astype(F32))
        da = du * sg
        dg = du * av * sg * (1.0 - sg)
        acc_b1[:, 0:d] += _colsum8(da)
        acc_b1[:, d:2 * d] += _colsum8(dg)
        dx = ALPHA * dz_ref[...]
        for j, part in enumerate([da[:, 0:dh_w], da[:, dh_w:d], dg[:, 0:dh_w], dg[:, dh_w:d]]):
            pb = part.astype(BF16)
            dh_ref[j] = pb
            dx = dx + _dot_nt(pb, w1_ref[j])
        dx_ref[...] = dx

        @pl.when(i == nt - 1)
        def _():
            db1_ref[...] = jnp.sum(acc_b1[...], axis=0, keepdims=True)
            dbdw_ref[...] = jnp.sum(acc_bdw[...], axis=0, keepdims=True)
            dw_ref[...] = jnp.sum(acc_w[...], axis=1)

    return _tc_call(
        body, name="bwd_conv_glu", nt=nt,
        in_specs=[_rows(tm, d), pl.BlockSpec((CONV_HALO, d), lambda i: (jnp.minimum((i + 1) * hb, last_halo), 0)),
                  _rows(tm, d), _rows(tm, d), _rows(tm, d), _rows(tm, d), _const((CONV_HALO, 8, d)),
                  _const((4, d, dh_w))],
        out_specs=[_rows(tm, d), pl.BlockSpec((4, tm, dh_w), lambda i: (0, i, 0)), _acc_out((1, 2 * d)),
                   _acc_out((1, d)), _acc_out((CONV_HALO, d))],
        out_shape=[jax.ShapeDtypeStruct((t, d), F32), jax.ShapeDtypeStruct((4, t, dh_w), BF16),
                   jax.ShapeDtypeStruct((1, 2 * d), F32), jax.ShapeDtypeStruct((1, d), F32),
                   jax.ShapeDtypeStruct((CONV_HALO, d), F32)],
        scratch_shapes=[pltpu.VMEM((tm + CONV_HALO + 8, d), F32), pltpu.VMEM((7, tm + CONV_HALO, d), F32),
                        pltpu.VMEM((tm, d), F32), pltpu.VMEM((8, 2 * d), F32), pltpu.VMEM((8, d), F32),
                        pltpu.VMEM((CONV_HALO, 8, d), F32)],
        operands=(dc, dc, u, a, g, dz, wdw, w1s), riders=riders)


def _adamw_update(w_ref, g_ref, m_ref, v_ref, d_ref, nm_ref, nv_ref):
    gv = g_ref[...]
    nm = ADAM_B1 * m_ref[...] + (1.0 - ADAM_B1) * gv
    nv = ADAM_B2 * v_ref[...] + (1.0 - ADAM_B2) * (gv * gv)
    m_hat = nm / (1.0 - ADAM_B1 ** ADAM_STEP)
    v_hat = nv / (1.0 - ADAM_B2 ** ADAM_STEP)
    d_ref[...] = -ADAM_LR * (m_hat / (jnp.sqrt(v_hat) + ADAM_EPS) + ADAM_WD * w_ref[...])
    nm_ref[...] = nm
    nv_ref[...] = nv


ADAMW_STEPS = 8


def _adamw(params, name, riders=()):
    n = len(params)
    steps = ADAMW_STEPS if all(p[0].shape[0] % (8 * ADAMW_STEPS) == 0 for p in params) else 1

    def body(*refs):
        for k in range(n):
            _adamw_update(*refs[4 * k:4 * k + 4], *refs[4 * n + 3 * k:4 * n + 3 * k + 3])

    specs = [pl.BlockSpec((p[0].shape[0] // steps, p[0].shape[1]), lambda i: (i, 0)) for p in params]
    outs, rider_outs = _tc_call(
        body, name=name, nt=steps, in_specs=[s for s in specs for _ in range(4)],
        out_specs=[s for s in specs for _ in range(3)],
        out_shape=[jax.ShapeDtypeStruct(p[0].shape, F32) for p in params for _ in range(3)],
        operands=[a for p in params for a in p], riders=riders)
    return [tuple(outs[3 * k:3 * k + 3]) for k in range(n)], rider_outs


def _pad_to(v, n):
    return jnp.pad(v, (0, n - v.shape[0]))


def _round_up(n, m):
    return (n + m - 1) // m * m


def kernel(x, conv_w_pw1, conv_b_pw1, conv_w_dw, conv_b_dw, conv_ln_g, conv_ln_b, conv_w_pw2, conv_b_pw2, kv_w_k, kv_b_k, kv_w_v, kv_b_v, attn_w_q, attn_b_q, attn_sinks, attn_w_o, attn_b_o, ffn_w_gate, ffn_w_up, ffn_w_down, ln_mix_g, ln_mix_b, ln_ffn_g, ln_ffn_b, loss_target, m_conv_w_pw1, m_conv_b_pw1, m_conv_w_dw, m_conv_b_dw, m_conv_ln_g, m_conv_ln_b, m_conv_w_pw2, m_conv_b_pw2, m_kv_w_k, m_kv_b_k, m_kv_w_v, m_kv_b_v, m_attn_w_q, m_attn_b_q, m_attn_sinks, m_attn_w_o, m_attn_b_o, m_ffn_w_gate, m_ffn_w_up, m_ffn_w_down, m_ln_mix_g, m_ln_mix_b, m_ln_ffn_g, m_ln_ffn_b, v_conv_w_pw1, v_conv_b_pw1, v_conv_w_dw, v_conv_b_dw, v_conv_ln_g, v_conv_ln_b, v_conv_w_pw2, v_conv_b_pw2, v_kv_w_k, v_kv_b_k, v_kv_w_v, v_kv_b_v, v_attn_w_q, v_attn_b_q, v_attn_sinks, v_attn_w_o, v_attn_b_o, v_ffn_w_gate, v_ffn_w_up, v_ffn_w_down, v_ln_mix_g, v_ln_mix_b, v_ln_ffn_g, v_ln_ffn_b):
    args = dict(locals())
    w = {n: args[n] for n in WEIGHTS}
    mom = {n: args["m_" + n] for n in WEIGHTS}
    var = {n: args["v_" + n] for n in WEIGHTS}
    assert x.shape[0] == 1, "one sequence per device"
    t, d = x.shape[1], x.shape[2]
    dq = d // 4
    fs = ffn_w_gate.shape[-1]
    nq = d // HEAD_DIM
    x0 = x.reshape(t, d)
    target = loss_target.reshape(t, d)
    tm_big = min(512, t)
    tm_mid = min(256, t)
    tm_tn = min(1024, t)
    c_idx = lax.axis_index("c")

    me_idx = 2 * lax.axis_index("x") + lax.axis_index("y")

    def gather_buffer(v):
        buf = lax.empty((N_CHIPS,) + v.shape, v.dtype)
        return lax.dynamic_update_slice(buf, v[None], (me_idx,) + (0,) * v.ndim)

    def halves(v):
        return v.reshape(2, -1, v.shape[-1])

    small_sizes = [int(w[n].size) for n in SMALL_SHARDED]
    rs = _round_up(sum(small_sizes), 8 * 128) // 128
    spack = _pad_to(jnp.concatenate([w[n].reshape(-1) for n in SMALL_SHARDED]), rs * 128).reshape(rs, 128)
    conv_first = ['conv_w_pw1', 'conv_w_pw2']
    later = [n for n in BIG if n not in conv_first]
    (first_out,) = _run_riders(
        [_all_gather_rider([gather_buffer(halves(w[n].astype(BF16))) for n in conv_first], gather_buffer(spack))],
        "all_gather_conv")
    later_rider = _all_gather_rider([gather_buffer(halves(w[n].astype(BF16))) for n in later])
    gs = first_out[-1].reshape(N_CHIPS, rs * 128)
    full = {n: g.reshape((N_CHIPS,) + w[n].shape) for n, g in zip(conv_first, first_out)}
    off = 0
    for n, size in zip(SMALL_SHARDED, small_sizes):
        full[n] = gs[:, off:off + size].reshape((N_CHIPS,) + w[n].shape)
        off += size
    w1s = full['conv_w_pw1'].reshape(N_CHIPS, d, d // 2)
    w2 = full['conv_w_pw2'].reshape(d, d)
    b1 = full['conv_b_pw1'].reshape(1, 2 * d)
    wdw = jnp.pad(full['conv_w_dw'].reshape(N_CHIPS, CONV_WIDTH, dq).transpose(1, 0, 2).reshape(CONV_WIDTH, d),
                  ((0, CONV_HALO - CONV_WIDTH), (0, 0)))
    wdw = jnp.broadcast_to(wdw[:, None, :], (CONV_HALO, 8, d))
    bdw = full['conv_b_dw'].reshape(1, d)
    clng = full['conv_ln_g'].reshape(1, d)
    clnb = full['conv_ln_b'].reshape(1, d)
    b2 = full['conv_b_pw2'].reshape(1, d)
    bkv = jnp.concatenate([kv_b_k, kv_b_v]).reshape(1, 2 * KVD)
    sinks = attn_sinks.reshape(nq)
    mixg = [ln_mix_g[l].reshape(1, d) for l in range(DEPTH)]
    mixb = [ln_mix_b[l].reshape(1, d) for l in range(DEPTH)]
    ffng = [ln_ffn_g[l].reshape(1, d) for l in range(DEPTH)]
    ffnb = [ln_ffn_b[l].reshape(1, d) for l in range(DEPTH)]

    a_act, g_act, u_act = _fwd_pw1_glu(x0, w1s, b1, tm_big)
    (c_act, z1, x1), (later_out,) = _fwd_conv_tail(u_act, x0, wdw, bdw, clng, clnb, w2, b2, mixg[0], mixb[0], tm_big,
                                                   riders=[later_rider])
    full.update({n: g.reshape((N_CHIPS,) + w[n].shape) for n, g in zip(later, later_out)})
    wkv = jnp.concatenate([full['kv_w_k'].reshape(d, KVD), full['kv_w_v'].reshape(d, KVD)], axis=1)
    wqt = full['attn_w_q'].reshape(d, d).T
    wo = full['attn_w_o'].reshape(d, d)
    wg, wu, wd = full['ffn_w_gate'], full['ffn_w_up'], full['ffn_w_down']
    act0, bm0, hm0, z2, x2 = _fwd_ffn(x1, wg, wu, wd, 0, ffng[0], ffnb[0], tm_big)
    qt_act, kv_act, ot_act, z3, x3 = _fwd_attn(x2, wqt, attn_b_q.reshape(d, 1), wkv, bkv, sinks, wo, attn_b_o,
                                               mixg[1], mixb[1], tm_big)
    act1, bm1, hm1, dz4, d_fg1, d_fb1, loss_part = _fwd_ffn(x3, wg, wu, wd, 1, ffng[1], ffnb[1], tm_big, target=target)
    loss = lax.psum(loss_part[0, 0], ("x", "y", "c"))

    c_arr = c_idx.reshape(1).astype(jnp.int32)

    def halves4(v):
        return v.reshape(N_CHIPS, 2, -1, v.shape[-1])

    def arrays(group):
        return [p for _, p in group]

    def pair_sums(group, got):
        return _pair_sums(arrays(group), got, c_arr, "grad_pair_sum_" + group[0][0])

    pos_arr = jnp.stack([me_idx, c_idx]).astype(jnp.int32)

    def chip_sums(group, sums, got):
        return _chip_sums(sums, got, pos_arr, "grad_chip_sum_" + group[0][0])

    (dgg1, duu1, dz3, d_mg1, d_mb1, d_bo), _ = _bwd_ffn_dx(dz4, act1, bm1, wg, wu, wd, 1, z3, mixg[1], tm_big)
    g1 = [("ffn_w_gate1", halves4(_matmul_tn(x3[None], dgg1, tm_tn, "dw_gate1"))),
          ("ffn_w_up1", halves4(_matmul_tn(x3[None], duu1, tm_tn, "dw_up1"))),
          ("ffn_w_down1", halves4(_matmul_tn(hm1, dz4[None], tm_tn, "dw_down1")))]
    (dqt, dkv, dz2, d_bq, d_bkv, d_sinks, d_fg0, d_fb0), (got1,) = _bwd_attn(
        dz3, qt_act, kv_act, sinks, wo, wqt, wkv, z2, ffng[0], tm_big, riders=[_pair_exchange_rider(arrays(g1))])
    s1 = pair_sums(g1, got1)
    dwo = _matmul_nn(ot_act, dz3, tm_tn, "dw_o")
    dwq = _matmul_nn(dqt, x2, tm_tn, "dw_q").T
    dwkv = _matmul_tn(x2[None], dkv[None], tm_tn, "dw_kv")[0]
    g2 = [("attn_w_o", halves4(dwo)), ("attn_w_q", halves4(dwq)),
          ("kv_w_k", halves4(dwkv[:, 0:KVD])), ("kv_w_v", halves4(dwkv[:, KVD:2 * KVD]))]
    (dgg0, duu0, dz1, d_mg0, d_mb0, d_b2), (from_chips1, got2) = _bwd_ffn_dx(
        dz2, act0, bm0, wg, wu, wd, 0, z1, mixg[0], tm_big,
        riders=[_chip_scatter_rider(s1), _pair_exchange_rider(arrays(g2))])
    f1 = chip_sums(g1, s1, from_chips1)
    s2 = pair_sums(g2, got2)
    g3 = [("ffn_w_gate0", halves4(_matmul_tn(x1[None], dgg0, tm_tn, "dw_gate0"))),
          ("ffn_w_up0", halves4(_matmul_tn(x1[None], duu0, tm_tn, "dw_up0"))),
          ("ffn_w_down0", halves4(_matmul_tn(hm0, dz2[None], tm_tn, "dw_down0")))]
    (s_act, dc, d_clng, d_clnb), (got3, shared1) = _bwd_conv_head(
        dz1, c_act, w2, clng, clnb, tm_big, riders=[_pair_exchange_rider(arrays(g3)), _pair_share_rider(f1)])
    s3 = pair_sums(g3, got3)
    dw2 = _matmul_tn(s_act[None], dz1[None], tm_tn, "dw_pw2")
    (dx0, dh1, d_b1, d_bdw, d_wdw), (from_chips2, from_chips3) = _bwd_conv_glu(
        dc, u_act, a_act, g_act, dz1, wdw, w1s, tm_big, riders=[_chip_scatter_rider(s2), _chip_scatter_rider(s3)])
    f2 = chip_sums(g2, s2, from_chips2)
    f3 = chip_sums(g3, s3, from_chips3)
    dw1, grad_x = _matmul_tn(x0[None], dh1, tm_tn, "dw_pw1", carry=dx0)

    def rows4(v):
        return v.reshape(N_CHIPS, -1)

    def rep4(v):
        return jnp.broadcast_to(v.reshape(1, -1), (N_CHIPS, v.size))

    local = {
        'conv_b_pw1': rows4(d_b1),
        'conv_w_dw': rows4(d_wdw[0:CONV_WIDTH].reshape(CONV_WIDTH, N_CHIPS, dq).transpose(1, 0, 2)),
        'conv_b_dw': rows4(d_bdw), 'conv_ln_g': rows4(d_clng), 'conv_ln_b': rows4(d_clnb), 'conv_b_pw2': rows4(d_b2),
        'kv_b_k': rep4(d_bkv[:, 0:KVD]), 'kv_b_v': rep4(d_bkv[:, KVD:2 * KVD]), 'attn_b_q': rep4(d_bq),
        'attn_sinks': rep4(d_sinks), 'attn_b_o': rep4(d_bo),
        'ln_mix_g': rep4(jnp.concatenate([d_mg0, d_mg1])), 'ln_mix_b': rep4(jnp.concatenate([d_mb0, d_mb1])),
        'ln_ffn_g': rep4(jnp.concatenate([d_fg0, d_fg1])), 'ln_ffn_b': rep4(jnp.concatenate([d_fb0, d_fb1])),
    }
    n_small = sum(int(w[n].size) for n in SMALL)
    small_rows = _round_up(n_small, 2 * SUM_STEPS * 8 * 128) // 128
    small_local = jnp.concatenate([local[n] for n in SMALL], axis=1)
    small_local = jnp.pad(small_local, ((0, 0), (0, small_rows * 128 - n_small)))
    g4 = [("conv_w_pw1", halves4(dw1)), ("conv_w_pw2", halves4(dw2)),
          ("small", small_local.reshape(N_CHIPS, 2, small_rows // 2, 128))]
    got4, shared23 = _run_riders([_pair_exchange_rider(arrays(g4)), _pair_share_rider(f2 + f3)],
                                 "grad_pair_exchange_last")
    s4 = pair_sums(g4, got4)
    reduced = dict(zip([n for n, _ in g1], shared1))
    reduced.update(zip([n for n, _ in g2 + g3], shared23))
    for n in ('ffn_w_gate', 'ffn_w_up', 'ffn_w_down'):
        reduced[n] = jnp.stack([reduced[n + str(layer)].reshape(w[n].shape[1:]) for layer in range(DEPTH)])

    g_out, delta, new_m, new_v = {}, {}, {}, {}

    def adamw_matrices(names, name, riders=()):
        for n in names:
            g_out[n] = reduced[n].reshape(w[n].shape)
        two_d = [tuple(tree[n].reshape(-1, w[n].shape[-1]) for tree in (w, g_out, mom, var)) for n in names]
        results, rider_outs = _adamw(two_d, name, riders)
        for n, (dl, nm, nv) in zip(names, results):
            delta[n], new_m[n], new_v[n] = (r.reshape(w[n].shape) for r in (dl, nm, nv))
        return rider_outs

    (from_chips4,) = adamw_matrices([n for n in BIG if n not in conv_first], "adamw_attn_ffn",
                                    riders=[_chip_scatter_rider(s4)])
    f4 = chip_sums(g4, s4, from_chips4)
    (shared4,) = _run_riders([_pair_share_rider(f4)], "grad_pair_share_last")
    reduced.update(zip([n for n, _ in g4], shared4))
    adamw_matrices(conv_first, "adamw_conv")

    def pack_small(tree):
        return _pad_to(jnp.concatenate([tree[n].reshape(-1) for n in SMALL]), small_rows * 128).reshape(small_rows, 128)

    g_small = reduced['small'].reshape(small_rows, 128)
    ((dl, nm, nv),), _ = _adamw([(pack_small(w), g_small, pack_small(mom), pack_small(var))], "adamw_small")
    off = 0
    for n in SMALL:
        size, shape = int(w[n].size), w[n].shape
        for tree, flat in ((g_out, g_small), (delta, dl), (new_m, nm), (new_v, nv)):
            tree[n] = flat.reshape(-1)[off:off + size].reshape(shape)
        off += size

    return (loss, grad_x.reshape(x.shape), *[g_out[n] for n in WEIGHTS], *[delta[n] for n in WEIGHTS],
            *[new_m[n] for n in WEIGHTS], *[new_v[n] for n in WEIGHTS])
```

```python
import math

import jax
import jax.numpy as jnp
from jax import lax
from jax.experimental import pallas as pl
from jax.experimental.pallas import tpu as pltpu

F32 = jnp.float32
BF16 = jnp.bfloat16

DEPTH = 2
ALPHA = (2.0 * DEPTH) ** 0.25
LN_EPS = 1e-5
NEG_INF = -1e30
HEAD_DIM = 64
N_KV_HEADS = 2
KVD = N_KV_HEADS * HEAD_DIM
BLOCK = 128
CONV_WIDTH = 31
CONV_HALO = 32
ALIBI_MAX = 8.0
ADAM_LR, ADAM_B1, ADAM_B2, ADAM_EPS, ADAM_WD, ADAM_STEP = 0.001, 0.9, 0.999, 1e-08, 0.01, 10

N_CHIPS = 4
SUM_STEPS = 4
VMEM_LIMIT = 60 * 1024 * 1024
MESH = pl.DeviceIdType.MESH

NT_DIMS = (((1,), (1,)), ((), ()))
TN_DIMS = (((0,), (0,)), ((), ()))

WEIGHTS = ['conv_w_pw1', 'conv_b_pw1', 'conv_w_dw', 'conv_b_dw', 'conv_ln_g', 'conv_ln_b', 'conv_w_pw2', 'conv_b_pw2',
           'kv_w_k', 'kv_b_k', 'kv_w_v', 'kv_b_v', 'attn_w_q', 'attn_b_q', 'attn_sinks', 'attn_w_o', 'attn_b_o',
           'ffn_w_gate', 'ffn_w_up', 'ffn_w_down', 'ln_mix_g', 'ln_mix_b', 'ln_ffn_g', 'ln_ffn_b']
BIG = ['conv_w_pw1', 'conv_w_pw2', 'kv_w_k', 'kv_w_v', 'attn_w_q', 'attn_w_o', 'ffn_w_gate', 'ffn_w_up', 'ffn_w_down']
SMALL_SHARDED = ['conv_b_pw1', 'conv_w_dw', 'conv_b_dw', 'conv_ln_g', 'conv_ln_b', 'conv_b_pw2']
REPLICATED = ['kv_b_k', 'kv_b_v', 'attn_b_q', 'attn_sinks', 'attn_b_o', 'ln_mix_g', 'ln_mix_b', 'ln_ffn_g', 'ln_ffn_b']
SMALL = SMALL_SHARDED + REPLICATED


def _cparams(n_grid=1):
    return pltpu.CompilerParams(dimension_semantics=("arbitrary",) * n_grid, vmem_limit_bytes=VMEM_LIMIT)


def _rows(tm, width):
    return pl.BlockSpec((tm, width), lambda i: (i, 0))


def _const(shape):
    return pl.BlockSpec(shape, lambda *_: (0,) * len(shape), pipeline_mode=pl.Buffered(1))


def _acc_out(shape):
    return pl.BlockSpec(shape, lambda *_: (0,) * len(shape))


def _dot(a, b):
    return jnp.dot(a, b, preferred_element_type=F32)


def _dot_nt(a, b):
    return lax.dot_general(a, b, NT_DIMS, preferred_element_type=F32)


def _dot_tn(a, b):
    return lax.dot_general(a, b, TN_DIMS, preferred_element_type=F32)


def _colsum8(v):
    m, n = v.shape
    return jnp.sum(v.reshape(m // 8, 8, n), axis=0)


def _ln_stats(z):
    mu = jnp.mean(z, axis=-1, keepdims=True)
    zc = z - mu
    var = jnp.mean(zc * zc, axis=-1, keepdims=True)
    rstd = lax.rsqrt(var + LN_EPS)
    return zc * rstd, rstd


def _ln_fwd(z, g, b):
    zhat, _ = _ln_stats(z)
    return zhat * g + b


def _ln_bwd(dy, z, g):
    zhat, rstd = _ln_stats(z)
    dzh = dy * g
    m1 = jnp.mean(dzh, axis=-1, keepdims=True)
    m2 = jnp.mean(dzh * zhat, axis=-1, keepdims=True)
    return rstd * (dzh - m1 - zhat * m2), zhat


def _silu_and_grad(n):
    sg = jax.nn.sigmoid(n)
    return n * sg, sg * (1.0 + n * (1.0 - sg))


def _acc_init(i, *refs):
    @pl.when(i == 0)
    def _():
        for r in refs:
            r[...] = jnp.zeros_like(r)


def _mesh_pos():
    x, y, c = lax.axis_index("x"), lax.axis_index("y"), lax.axis_index("c")
    chips = [(1 - x, y), (x, 1 - y), (1 - x, 1 - y)]
    return x, y, c, chips


HBM_SPEC = pl.BlockSpec(memory_space=pltpu.HBM)


def _remote(src, dst, send_sems, recv_sems, k, to):
    return pltpu.make_async_remote_copy(src_ref=src, dst_ref=dst, send_sem=send_sems.at[k], recv_sem=recv_sems.at[k],
                                        device_id=to, device_id_type=MESH)


class _Rider:
    def __init__(self, operands, out_shapes, sem_shapes, start, finish, mid=None, in_place=False):
        self.operands, self.out_shapes, self.sem_shapes = list(operands), list(out_shapes), list(sem_shapes)
        self.start, self.finish, self.mid = start, finish, mid
        self.in_place = in_place


def _rider_aliases(riders, first_in, first_out):
    aliases, k_in, k_out = {}, first_in, first_out
    for r in riders:
        if r.in_place:
            aliases.update({k_in + k: k_out + k for k in range(len(r.operands))})
        k_in += len(r.operands)
        k_out += len(r.out_shapes)
    return aliases


def _split(refs, counts):
    parts, k = [], 0
    for n in counts:
        parts.append(refs[k:k + n])
        k += n
    return parts


def _rider_refs(riders, ins, outs, sems):
    return list(zip(riders, _split(ins, [len(r.operands) for r in riders]),
                    _split(outs, [len(r.out_shapes) for r in riders]),
                    _split(sems, [len(r.sem_shapes) for r in riders])))


def _tc_call(body, *, name, nt, in_specs, out_specs, out_shape, operands, scratch_shapes=(), riders=(), mid_frac=0.75):
    n_in, n_out, n_scr = len(in_specs), len(out_specs), len(scratch_shapes)
    r_ops = [o for r in riders for o in r.operands]
    r_outs = [o for r in riders for o in r.out_shapes]
    r_sems = [s for r in riders for s in r.sem_shapes]
    mid_step = min(max(int(nt * mid_frac), 0), nt - 1)

    def full(*refs):
        ins, r_in, outs, r_out, scr, r_sem = _split(refs, [n_in, len(r_ops), n_out, len(r_outs), n_scr, len(r_sems)])
        parts = _rider_refs(riders, r_in, r_out, r_sem)
        step = pl.program_id(0)

        @pl.when(step == 0)
        def _():
            for r, a, b, s in parts:
                r.start(a, b, s)

        body(*ins, *outs, *scr)

        @pl.when(step == mid_step)
        def _():
            for r, a, b, s in parts:
                if r.mid is not None:
                    r.mid(a, b, s)

        @pl.when(step == nt - 1)
        def _():
            for r, a, b, s in parts:
                r.finish(a, b, s)

    res = pl.pallas_call(
        full if riders else body, name=name, grid=(nt,), in_specs=list(in_specs) + [HBM_SPEC] * len(r_ops),
        out_specs=list(out_specs) + [HBM_SPEC] * len(r_outs), out_shape=list(out_shape) + r_outs,
        scratch_shapes=list(scratch_shapes) + r_sems, input_output_aliases=_rider_aliases(riders, n_in, n_out),
        compiler_params=_cparams(),
    )(*operands, *r_ops)
    return res[:n_out], _split(res[n_out:], [len(r.out_shapes) for r in riders])


def _run_riders(riders, name):
    r_ops = [o for r in riders for o in r.operands]
    r_outs = [o for r in riders for o in r.out_shapes]
    r_sems = [s for r in riders for s in r.sem_shapes]

    def body(*refs):
        r_in, r_out, r_sem = _split(refs, [len(r_ops), len(r_outs), len(r_sems)])
        parts = _rider_refs(riders, r_in, r_out, r_sem)
        for r, a, b, s in parts:
            r.start(a, b, s)
        for r, a, b, s in parts:
            if r.mid is not None:
                r.mid(a, b, s)
        for r, a, b, s in parts:
            r.finish(a, b, s)

    res = pl.pallas_call(body, name=name, out_shape=tuple(r_outs), in_specs=[HBM_SPEC] * len(r_ops),
                         out_specs=(HBM_SPEC,) * len(r_outs), scratch_shapes=r_sems,
                         input_output_aliases=_rider_aliases(riders, 0, 0))(*r_ops)
    return _split(list(res), [len(r.out_shapes) for r in riders])


def _all_gather_rider(bufs, small=None):
    n = len(bufs)
    n_small = 0 if small is None else 1

    def copies(outs, sems):
        send_sems, recv_sems = sems
        x, y, c, chips = _mesh_pos()
        me = 2 * x + y
        here, sibling = (x, y, c), (x, y, 1 - c)
        rows = [2 * cx + cy for cx, cy in chips]

        def big(p, k, chip_row, half, to):
            piece = outs[p].at[chip_row, half]
            return _remote(piece, piece, send_sems, recv_sems, 6 * p + k, to)

        first = [big(p, j, me, c, (cx, cy, c)) for p in range(n) for j, (cx, cy) in enumerate(chips)]
        landed = [big(p, j, rows[j], c, here) for p in range(n) for j in range(3)]
        passed = [big(p, 3 + j, rows[j], c, sibling) for p in range(n) for j in range(3)]
        arrivals = [big(p, 3 + j, rows[j], 1 - c, here) for p in range(n) for j in range(3)]
        if n_small:
            first = [_remote(outs[n].at[me], outs[n].at[me], send_sems, recv_sems, 6 * n + j, (cx, cy, c))
                     for j, (cx, cy) in enumerate(chips)] + first
            arrivals += [_remote(outs[n].at[rows[j]], outs[n].at[rows[j]], send_sems, recv_sems, 6 * n + j, here)
                         for j in range(3)]
        return first, landed, passed, arrivals

    def start(ins, outs, sems):
        for cp in copies(outs, sems)[0]:
            cp.start()

    def mid(ins, outs, sems):
        _, landed, passed, _ = copies(outs, sems)
        for got, fwd in zip(landed, passed):
            got.wait_recv()
            fwd.start()

    def finish(ins, outs, sems):
        first, _, passed, arrivals = copies(outs, sems)
        for cp in arrivals:
            cp.wait_recv()
        for cp in first + passed:
            cp.wait_send()

    operands = list(bufs) + ([small] if n_small else [])
    n_sem = 6 * n + 3 * n_small
    return _Rider(operands, [jax.ShapeDtypeStruct(o.shape, o.dtype) for o in operands],
                  [pltpu.SemaphoreType.DMA((n_sem,)), pltpu.SemaphoreType.DMA((n_sem,))], start, finish, mid,
                  in_place=True)


def _pair_exchange_rider(plist):
    n = len(plist)

    def copies(ins, outs, sems):
        x, y, c, _ = _mesh_pos()
        return [_remote(ins[k].at[:, 1 - c], outs[k], sems[0], sems[1], k, (x, y, 1 - c)) for k in range(n)]

    def start(ins, outs, sems):
        for cp in copies(ins, outs, sems):
            cp.start()

    def finish(ins, outs, sems):
        for cp in copies(ins, outs, sems):
            cp.wait()

    return _Rider(plist, [jax.ShapeDtypeStruct((p.shape[0],) + p.shape[2:], p.dtype) for p in plist],
                  [pltpu.SemaphoreType.DMA((n,)), pltpu.SemaphoreType.DMA((n,))], start, finish)


def _pair_sums(plist, gots, c, name):
    n = len(plist)

    def body(c_ref, *refs):
        for k in range(n):
            refs[2 * n + k][...] = refs[k][...] + refs[n + k][...]

    rows = [p.shape[2] // SUM_STEPS for p in plist]
    return pl.pallas_call(
        body, name=name, out_shape=[jax.ShapeDtypeStruct(g.shape, F32) for g in gots],
        grid_spec=pltpu.PrefetchScalarGridSpec(
            num_scalar_prefetch=1, grid=(N_CHIPS, SUM_STEPS),
            in_specs=[pl.BlockSpec((None, None, br, p.shape[3]), lambda j, i, c_ref: (j, c_ref[0], i, 0))
                      for p, br in zip(plist, rows)]
            + [pl.BlockSpec((None, br, p.shape[3]), lambda j, i, c_ref: (j, i, 0)) for p, br in zip(plist, rows)],
            out_specs=[pl.BlockSpec((None, br, p.shape[3]), lambda j, i, c_ref: (j, i, 0))
                       for p, br in zip(plist, rows)]),
        compiler_params=_cparams(2),
    )(c, *plist, *gots)


def _chip_scatter_rider(slist):
    n = len(slist)

    def copies(ins, outs, sems):
        send_sems, recv_sems = sems
        x, y, c, chips = _mesh_pos()
        sends = [_remote(ins[k].at[2 * cx + cy], outs[k].at[j], send_sems, recv_sems, 3 * k + j, (cx, cy, c))
                 for k in range(n) for j, (cx, cy) in enumerate(chips)]
        arrivals = [_remote(ins[k].at[0], outs[k].at[j], send_sems, recv_sems, 3 * k + j, (x, y, c))
                    for k in range(n) for j in range(3)]
        return sends, arrivals

    def start(ins, outs, sems):
        for cp in copies(ins, outs, sems)[0]:
            cp.start()

    def finish(ins, outs, sems):
        sends, arrivals = copies(ins, outs, sems)
        for cp in arrivals:
            cp.wait_recv()
        for cp in sends:
            cp.wait_send()

    return _Rider(slist, [jax.ShapeDtypeStruct((3,) + s.shape[1:], s.dtype) for s in slist],
                  [pltpu.SemaphoreType.DMA((3 * n,)), pltpu.SemaphoreType.DMA((3 * n,))], start, finish)


def _chip_sums(slist, gots, pos, name):
    n = len(slist)

    def body(pos_ref, *refs):
        me = pos_ref[0]
        for k in range(n):
            s_ref, got_ref, out_ref = refs[k], refs[n + k], refs[2 * n + k]
            total = None
            for chip in range(N_CHIPS):
                flip = jnp.bitwise_xor(me, chip)
                term = jnp.where(flip == 0, s_ref[...],
                                 jnp.where(flip == 2, got_ref[0], jnp.where(flip == 1, got_ref[1], got_ref[2])))
                total = term if total is None else total + term
            out_ref[...] = total

    rows = [s.shape[1] // SUM_STEPS for s in slist]
    return pl.pallas_call(
        body, name=name, out_shape=[jax.ShapeDtypeStruct((2,) + s.shape[1:], F32) for s in slist],
        grid_spec=pltpu.PrefetchScalarGridSpec(
            num_scalar_prefetch=1, grid=(SUM_STEPS,),
            in_specs=[pl.BlockSpec((None, br, s.shape[2]), lambda i, pos_ref: (pos_ref[0], i, 0))
                      for s, br in zip(slist, rows)]
            + [pl.BlockSpec((3, br, s.shape[2]), lambda i, pos_ref: (0, i, 0)) for s, br in zip(slist, rows)],
            out_specs=[pl.BlockSpec((None, br, s.shape[2]), lambda i, pos_ref: (pos_ref[1], i, 0))
                       for s, br in zip(slist, rows)]),
        compiler_params=_cparams(1),
    )(pos, *slist, *gots)


def _pair_share_rider(flist):
    n = len(flist)

    def copies(outs, sems):
        x, y, c, _ = _mesh_pos()
        sends = [_remote(outs[k].at[c], outs[k].at[c], sems[0], sems[1], k, (x, y, 1 - c)) for k in range(n)]
        arrivals = [_remote(outs[k].at[1 - c], outs[k].at[1 - c], sems[0], sems[1], k, (x, y, c)) for k in range(n)]
        return sends, arrivals

    def start(ins, outs, sems):
        for cp in copies(outs, sems)[0]:
            cp.start()

    def finish(ins, outs, sems):
        sends, arrivals = copies(outs, sems)
        for cp in arrivals:
            cp.wait_recv()
        for cp in sends:
            cp.wait_send()

    return _Rider(flist, [jax.ShapeDtypeStruct(f.shape, f.dtype) for f in flist],
                  [pltpu.SemaphoreType.DMA((n,)), pltpu.SemaphoreType.DMA((n,))], start, finish, in_place=True)


def _fwd_pw1_glu(x, w1s, b1, tm):
    t, d = x.shape
    dh = d // 2

    def body(x_ref, w_ref, b_ref, a_ref, g_ref, u_ref):
        xb = x_ref[...].astype(BF16)
        for hh in range(2):
            cs = slice(hh * dh, (hh + 1) * dh)
            a = _dot(xb, w_ref[hh]) + b_ref[:, hh * dh:(hh + 1) * dh]
            g = _dot(xb, w_ref[2 + hh]) + b_ref[:, d + hh * dh:d + (hh + 1) * dh]
            a_ref[:, cs] = a.astype(BF16)
            g_ref[:, cs] = g.astype(BF16)
            u_ref[:, cs] = a * jax.nn.sigmoid(g)

    return pl.pallas_call(
        body, name="fwd_pw1_glu", grid=(t // tm,),
        in_specs=[_rows(tm, d), _const((4, d, dh)), _const((1, 2 * d))],
        out_specs=[_rows(tm, d)] * 3,
        out_shape=[jax.ShapeDtypeStruct((t, d), BF16), jax.ShapeDtypeStruct((t, d), BF16),
                   jax.ShapeDtypeStruct((t, d), F32)],
        compiler_params=_cparams(),
    )(x, w1s, b1)


def _fill_shifted(sh_ref, ext_ref):
    n = sh_ref.shape[1]
    for s in range(1, 8):
        sh_ref[s - 1] = ext_ref[pl.ds(s, n), :]


def _ext_rows(ext, sh, e, base, rows, ls):
    if e % 8 == 0:
        return ext[pl.ds(base + e, rows), ls]
    return sh[e % 8 - 1, pl.ds(base + (e // 8) * 8, rows), ls]


CONV_CHUNK = 64
LANES = 256


def _tap_sum(w_ref, ext, sh, base, d, tap_row, out_ref, bias_ref=None):
    groups = CONV_CHUNK // 8
    for lg in range(d // LANES):
        ls = slice(lg * LANES, (lg + 1) * LANES)
        acc = jnp.zeros((groups, 8, LANES), F32)
        for k in range(CONV_WIDTH):
            x = _ext_rows(ext, sh, tap_row(k), base, CONV_CHUNK, ls)
            acc = acc + w_ref[k, :, ls] * x.reshape(groups, 8, LANES)
        acc = acc.reshape(CONV_CHUNK, LANES)
        out_ref[pl.ds(base, CONV_CHUNK), ls] = acc if bias_ref is None else acc + bias_ref[:, ls]


def _fwd_conv_tail(u, x0, wdw, bdw, lng, lnb, w2, b2, mixg, mixb, tm, riders=()):
    t, d = u.shape
    hb = tm // CONV_HALO

    def body(u_ref, uh_ref, x_ref, w_ref, bdw_ref, lng_ref, lnb_ref, w2_ref, b2_ref, mg_ref, mb_ref,
             c_ref, z_ref, y_ref, ext, sh):
        i = pl.program_id(0)
        ext[0:CONV_HALO] = jnp.where(i == 0, 0.0, uh_ref[...])
        ext[CONV_HALO:CONV_HALO + tm] = u_ref[...]
        ext[CONV_HALO + tm:CONV_HALO + tm + 8] = jnp.zeros((8, d), F32)
        _fill_shifted(sh, ext)

        def chunk(r, carry):
            base = pl.multiple_of(r * CONV_CHUNK, CONV_CHUNK)
            _tap_sum(w_ref, ext, sh, base, d, lambda k: k + CONV_HALO - (CONV_WIDTH - 1), c_ref, bdw_ref)
            return carry

        lax.fori_loop(0, tm // CONV_CHUNK, chunk, 0)
        n = _ln_fwd(c_ref[...], lng_ref[...], lnb_ref[...])
        s = n * jax.nn.sigmoid(n)
        m = _dot(s.astype(BF16), w2_ref[...]) + b2_ref[...]
        z = ALPHA * x_ref[...] + m
        z_ref[...] = z
        y_ref[...] = _ln_fwd(z, mg_ref[...], mb_ref[...])

    vec = _const((1, d))
    return _tc_call(
        body, name="fwd_conv_tail", nt=t // tm,
        in_specs=[_rows(tm, d), pl.BlockSpec((CONV_HALO, d), lambda i: (jnp.maximum(i * hb - 1, 0), 0)), _rows(tm, d),
                  _const((CONV_HALO, 8, d)), vec, vec, vec, _const((d, d)), vec, vec, vec],
        out_specs=[_rows(tm, d)] * 3,
        out_shape=[jax.ShapeDtypeStruct((t, d), F32)] * 3,
        scratch_shapes=[pltpu.VMEM((tm + CONV_HALO + 8, d), F32), pltpu.VMEM((7, tm + CONV_HALO, d), F32)],
        operands=(u, u, x0, wdw, bdw, lng, lnb, w2, b2, mixg, mixb), riders=riders)


def _fwd_ffn(x, wg, wu, wd, layer, lng, lnb, tm, target=None):
    t, d = x.shape
    fs = wg.shape[-1]
    nt = t // tm
    with_loss = target is not None

    def hidden(x_ref, wg_ref, wu_ref, wd_ref, act_ref, bm_ref, hm_ref):
        xv = x_ref[...]
        xb = xv.astype(BF16)
        f = jnp.zeros((tm, d), F32)
        for j in range(N_CHIPS):
            gj = _dot(xb, wg_ref[j])
            uj = _dot(xb, wu_ref[j])
            act, dact = _silu_and_grad(gj)
            act_ref[j] = act.astype(BF16)
            bm_ref[j] = (uj * dact).astype(BF16)
            hmb = (act * uj).astype(BF16)
            hm_ref[j] = hmb
            f = f + _dot(hmb, wd_ref[j])
        return ALPHA * xv + f

    def body(x_ref, wg_ref, wu_ref, wd_ref, g_ref, b_ref, act_ref, bm_ref, hm_ref, z_ref, y_ref):
        z = hidden(x_ref, wg_ref, wu_ref, wd_ref, act_ref, bm_ref, hm_ref)
        z_ref[...] = z
        y_ref[...] = _ln_fwd(z, g_ref[...], b_ref[...])

    def body_loss(x_ref, wg_ref, wu_ref, wd_ref, g_ref, b_ref, t_ref, act_ref, bm_ref, hm_ref, dz_ref,
                  dlg_ref, dlb_ref, loss_ref, acc_g, acc_b, acc_l):
        i = pl.program_id(0)
        _acc_init(i, acc_g, acc_b, acc_l)
        z = hidden(x_ref, wg_ref, wu_ref, wd_ref, act_ref, bm_ref, hm_ref)
        zhat, rstd = _ln_stats(z)
        gain = g_ref[...]
        err = zhat * gain + b_ref[...] - t_ref[...]
        acc_l[...] += _colsum8(err * err)
        dy = err * (1.0 / d)
        acc_g[...] += _colsum8(dy * zhat)
        acc_b[...] += _colsum8(dy)
        dzh = dy * gain
        m1 = jnp.mean(dzh, axis=-1, keepdims=True)
        m2 = jnp.mean(dzh * zhat, axis=-1, keepdims=True)
        dz_ref[...] = rstd * (dzh - m1 - zhat * m2)

        @pl.when(i == nt - 1)
        def _():
            dlg_ref[...] = jnp.sum(acc_g[...], axis=0, keepdims=True)
            dlb_ref[...] = jnp.sum(acc_b[...], axis=0, keepdims=True)
            loss_ref[...] = jnp.sum(acc_l[...], keepdims=True) * (0.5 / d)

    wcol = pl.BlockSpec((N_CHIPS, None, d, fs), lambda i: (0, layer, 0, 0), pipeline_mode=pl.Buffered(1))
    wrow = pl.BlockSpec((N_CHIPS, None, fs, d), lambda i: (0, layer, 0, 0), pipeline_mode=pl.Buffered(1))
    hid = pl.BlockSpec((N_CHIPS, tm, fs), lambda i: (0, i, 0))
    in_specs = [_rows(tm, d), wcol, wcol, wrow, _const((1, d)), _const((1, d))]
    hid_shapes = [jax.ShapeDtypeStruct((N_CHIPS, t, fs), BF16)] * 3
    if not with_loss:
        return pl.pallas_call(
            body, name=f"fwd_ffn{layer}", grid=(nt,), in_specs=in_specs,
            out_specs=[hid, hid, hid, _rows(tm, d), _rows(tm, d)],
            out_shape=hid_shapes + [jax.ShapeDtypeStruct((t, d), F32)] * 2, compiler_params=_cparams(),
        )(x, wg, wu, wd, lng, lnb)
    return pl.pallas_call(
        body_loss, name=f"fwd_ffn{layer}_loss", grid=(nt,), in_specs=in_specs + [_rows(tm, d)],
        out_specs=[hid, hid, hid, _rows(tm, d), _acc_out((1, d)), _acc_out((1, d)), _acc_out((1, 1))],
        out_shape=hid_shapes + [jax.ShapeDtypeStruct((t, d), F32)] + [jax.ShapeDtypeStruct((1, d), F32)] * 2
        + [jax.ShapeDtypeStruct((1, 1), F32)],
        scratch_shapes=[pltpu.VMEM((8, d), F32)] * 3, compiler_params=_cparams(),
    )(x, wg, wu, wd, lng, lnb, target)


def _attn_band():
    kt = lax.broadcasted_iota(jnp.int32, (BLOCK, BLOCK), 0)
    qi = lax.broadcasted_iota(jnp.int32, (BLOCK, BLOCK), 1)
    current = kt <= qi
    delta = qi - kt + jnp.where(current, 0, BLOCK)
    return current, delta.astype(F32)


def _fold(full, current):
    return jnp.where(current, full[BLOCK:2 * BLOCK], full[0:BLOCK])


def _unfold(folded, current):
    zero = jnp.zeros_like(folded)
    return jnp.concatenate([jnp.where(current, zero, folded), jnp.where(current, folded, zero)], axis=0)


def _slope(h, nq):
    return 2.0 ** (-ALIBI_MAX * (h + 1) / nq)


def _softmax_with_sink(s_full, slope, band, has_previous, sink):
    current, delta = band
    s = _fold(s_full, current) * (1.0 / math.sqrt(HEAD_DIM)) - slope * delta
    s = jnp.where(jnp.logical_or(current, has_previous), s, NEG_INF)
    m = jnp.maximum(jnp.max(s, axis=0, keepdims=True), sink)
    p = jnp.exp(s - m)
    e_sink = jnp.exp(sink - m)
    inv = 1.0 / (jnp.sum(p, axis=0, keepdims=True) + e_sink)
    return p * inv, e_sink * inv


def _heads_on_lanes(ref, b, g, group):
    first = g * group
    return jnp.concatenate([ref[b, (first + hh) * HEAD_DIM:(first + hh + 1) * HEAD_DIM, :] for hh in range(group)],
                           axis=1)


def _fill_kv(kv_scr, halo, tile, tm):
    for j in range(2 * N_KV_HEADS):
        kv_scr[j, 0:BLOCK] = halo[:, j * HEAD_DIM:(j + 1) * HEAD_DIM]
        kv_scr[j, BLOCK:BLOCK + tm] = tile[:, j * HEAD_DIM:(j + 1) * HEAD_DIM]


def _cols(d, tm):
    return pl.BlockSpec((d, tm), lambda i: (0, i))


def _fwd_attn(x, wqt, bqt, wkv, bkv, sinks, wo, bo, mixg, mixb, tm):
    t, d = x.shape
    nq = d // HEAD_DIM
    group = nq // N_KV_HEADS
    nb = tm // BLOCK

    def body(sink_ref, x_ref, xh_ref, wqt_ref, bqt_ref, wkv_ref, bkv_ref, wo_ref, bo_ref, mg_ref, mb_ref,
             qt_ref, kv_ref, ot_ref, z_ref, y_ref, kv_scr, qt_scr, ot_scr):
        i = pl.program_id(0)
        xv = x_ref[...]
        xb = xv.astype(BF16)
        qt = (_dot_nt(wqt_ref[...], xb) + bqt_ref[...]).astype(BF16)
        qt_ref[...] = qt
        for b in range(nb):
            qt_scr[b] = qt[:, b * BLOCK:(b + 1) * BLOCK]
        kvb = (_dot(xb, wkv_ref[...]) + bkv_ref[...]).astype(BF16)
        kv_ref[...] = kvb
        _fill_kv(kv_scr, (_dot(xh_ref[...].astype(BF16), wkv_ref[...]) + bkv_ref[...]).astype(BF16), kvb, tm)
        band = _attn_band()

        def block(b, carry):
            r0 = pl.multiple_of(b * BLOCK, BLOCK)
            has_previous = jnp.logical_or(i > 0, b > 0)
            for g in range(N_KV_HEADS):
                kk = kv_scr[g, pl.ds(r0, 2 * BLOCK), :]
                vv = kv_scr[N_KV_HEADS + g, pl.ds(r0, 2 * BLOCK), :]
                s_all = _dot(kk, _heads_on_lanes(qt_scr, b, g, group))
                probs = []
                for hh in range(group):
                    h = g * group + hh
                    p, _ = _softmax_with_sink(s_all[:, hh * BLOCK:(hh + 1) * BLOCK], _slope(h, nq), band,
                                              has_previous, sink_ref[h])
                    probs.append(_unfold(p.astype(BF16), band[0]))
                o_all = _dot_tn(vv, jnp.concatenate(probs, axis=1))
                for hh in range(group):
                    h = g * group + hh
                    ot_scr[b, h * HEAD_DIM:(h + 1) * HEAD_DIM, :] = o_all[:, hh * BLOCK:(hh + 1) * BLOCK].astype(BF16)
            return carry

        lax.fori_loop(0, nb, block, 0, unroll=2)
        ot = jnp.concatenate([ot_scr[b] for b in range(nb)], axis=1)
        ot_ref[...] = ot
        z = ALPHA * xv + _dot_tn(ot, wo_ref[...]) + bo_ref[...]
        z_ref[...] = z
        y_ref[...] = _ln_fwd(z, mg_ref[...], mb_ref[...])

    hb = tm // BLOCK
    vec = _const((1, d))
    return pl.pallas_call(
        body, name="fwd_attn", grid=(t // tm,),
        in_specs=[pl.BlockSpec(memory_space=pltpu.SMEM),
                  _rows(tm, d), pl.BlockSpec((BLOCK, d), lambda i: (jnp.maximum(i * hb - 1, 0), 0)),
                  _const((d, d)), _const((d, 1)), _const((d, 2 * KVD)), _const((1, 2 * KVD)), _const((d, d)), vec, vec,
                  vec],
        out_specs=[_cols(d, tm), _rows(tm, 2 * KVD), _cols(d, tm), _rows(tm, d), _rows(tm, d)],
        out_shape=[jax.ShapeDtypeStruct((d, t), BF16), jax.ShapeDtypeStruct((t, 2 * KVD), BF16),
                   jax.ShapeDtypeStruct((d, t), BF16), jax.ShapeDtypeStruct((t, d), F32),
                   jax.ShapeDtypeStruct((t, d), F32)],
        scratch_shapes=[pltpu.VMEM((2 * N_KV_HEADS, tm + BLOCK, HEAD_DIM), BF16), pltpu.VMEM((nb, d, BLOCK), BF16),
                        pltpu.VMEM((nb, d, BLOCK), BF16)],
        compiler_params=_cparams(),
    )(sinks, x, x, wqt, bqt, wkv, bkv, wo, bo, mixg, mixb)


def _write_sums(i, nt, pairs):
    @pl.when(i == nt - 1)
    def _():
        for out_ref, acc in pairs:
            out_ref[...] = jnp.sum(acc[...], axis=0, keepdims=True)


def _bwd_ffn_dx(dz, act, bm, wg, wu, wd, layer, z_in, g_in, tm, riders=()):
    t, d = dz.shape
    fs = wg.shape[-1]
    nt = t // tm

    def body(dz_ref, act_ref, bm_ref, wg_ref, wu_ref, wd_ref, zin_ref, gin_ref,
             dgg_ref, duu_ref, dzin_ref, dg_ref, db_ref, dsum_ref, acc_g, acc_b, acc_s):
        i = pl.program_id(0)
        _acc_init(i, acc_g, acc_b, acc_s)
        dzv = dz_ref[...]
        dzb = dzv.astype(BF16)
        dx = ALPHA * dzv
        for j in range(N_CHIPS):
            dh = _dot_nt(dzb, wd_ref[j])
            dgb = (dh * bm_ref[j].astype(F32)).astype(BF16)
            dub = (dh * act_ref[j].astype(F32)).astype(BF16)
            dgg_ref[j] = dgb
            duu_ref[j] = dub
            dx = dx + _dot_nt(dgb, wg_ref[j]) + _dot_nt(dub, wu_ref[j])
        dz_in, zhat = _ln_bwd(dx, zin_ref[...], gin_ref[...])
        acc_g[...] += _colsum8(dx * zhat)
        acc_b[...] += _colsum8(dx)
        acc_s[...] += _colsum8(dz_in)
        dzin_ref[...] = dz_in
        _write_sums(i, nt, [(dg_ref, acc_g), (db_ref, acc_b), (dsum_ref, acc_s)])

    wcol = pl.BlockSpec((N_CHIPS, None, d, fs), lambda i: (0, layer, 0, 0), pipeline_mode=pl.Buffered(1))
    wrow = pl.BlockSpec((N_CHIPS, None, fs, d), lambda i: (0, layer, 0, 0), pipeline_mode=pl.Buffered(1))
    hid = pl.BlockSpec((N_CHIPS, tm, fs), lambda i: (0, i, 0))
    return _tc_call(
        body, name=f"bwd_ffn_dx{layer}", nt=nt,
        in_specs=[_rows(tm, d), hid, hid, wcol, wcol, wrow, _rows(tm, d), _const((1, d))],
        out_specs=[hid, hid, _rows(tm, d)] + [_acc_out((1, d))] * 3,
        out_shape=[jax.ShapeDtypeStruct((N_CHIPS, t, fs), BF16)] * 2 + [jax.ShapeDtypeStruct((t, d), F32)]
        + [jax.ShapeDtypeStruct((1, d), F32)] * 3,
        scratch_shapes=[pltpu.VMEM((8, d), F32)] * 3,
        operands=(dz, act, bm, wg, wu, wd, z_in, g_in), riders=riders)


def _matmul_tn(a, b, tt, name, carry=None):
    ja, t, ka = a.shape
    jb, _, nb = b.shape
    nj = max(ja, jb)

    def body(a_ref, b_ref, *rest):
        o_ref = rest[-1] if carry is None else rest[1]

        @pl.when(pl.program_id(0) == 0)
        def _():
            o_ref[...] = jnp.zeros_like(o_ref)

        a0 = a_ref[0].astype(BF16) if ja == 1 else None
        b0 = b_ref[0].astype(BF16) if jb == 1 else None
        for j in range(nj):
            aj = a0 if ja == 1 else a_ref[j].astype(BF16)
            bj = b0 if jb == 1 else b_ref[j].astype(BF16)
            o_ref[j] += _dot_tn(aj, bj)
        if carry is not None:
            rest[2][...] = rest[0][...]

    in_specs = [pl.BlockSpec((ja, tt, ka), lambda i: (0, i, 0)), pl.BlockSpec((jb, tt, nb), lambda i: (0, i, 0))]
    out_specs = [pl.BlockSpec((nj, ka, nb), lambda i: (0, 0, 0))]
    out_shape = [jax.ShapeDtypeStruct((nj, ka, nb), F32)]
    operands = [a, b]
    if carry is not None:
        in_specs.append(_rows(tt, carry.shape[1]))
        out_specs.append(_rows(tt, carry.shape[1]))
        out_shape.append(jax.ShapeDtypeStruct(carry.shape, carry.dtype))
        operands.append(carry)
    res = pl.pallas_call(body, name=name, grid=(t // tt,), in_specs=in_specs, out_specs=out_specs,
                         out_shape=out_shape, compiler_params=_cparams())(*operands)
    return res[0] if carry is None else (res[0], res[1])


def _matmul_nn(at, b, tt, name):
    ka, t = at.shape
    nb = b.shape[1]

    def body(a_ref, b_ref, o_ref):
        @pl.when(pl.program_id(0) == 0)
        def _():
            o_ref[...] = jnp.zeros_like(o_ref)

        o_ref[...] += _dot(a_ref[...].astype(BF16), b_ref[...].astype(BF16))

    return pl.pallas_call(
        body, name=name, grid=(t // tt,),
        in_specs=[pl.BlockSpec((ka, tt), lambda i: (0, i)), pl.BlockSpec((tt, nb), lambda i: (i, 0))],
        out_specs=pl.BlockSpec((ka, nb), lambda i: (0, 0)), out_shape=jax.ShapeDtypeStruct((ka, nb), F32),
        compiler_params=_cparams(1),
    )(at, b)


def _bwd_attn(dz_all, qt, kv, sinks, wo, wqt, wkv, z_in, g_in, tm, riders=()):
    t, d = dz_all.shape
    nq = d // HEAD_DIM
    group = nq // N_KV_HEADS
    nb = tm // BLOCK
    nt = t // tm
    hb = tm // BLOCK
    n_kv = 2 * N_KV_HEADS

    def body(sink_ref, dz_ref, qt_ref, kv_ref, kvh_ref, wo_ref, wqt_ref, wkv_ref, zin_ref, gin_ref,
             dqt_ref, dkv_ref, dx_ref, dbq_ref, dbkv_ref, dsink_ref, ding_ref, dinb_ref,
             kv_scr, dkv_scr, qt_scr, dot_scr, dqt_scr, carry, acc_q, acc_kv, acc_s, acc_ig, acc_ib):
        i = pl.program_id(0)
        ti = nt - 1 - i
        _acc_init(i, carry, acc_q, acc_kv, acc_s, acc_ig, acc_ib)
        dz = dz_ref[...]
        do_t = _dot_nt(wo_ref[...], dz.astype(BF16)).astype(BF16)
        for b in range(nb):
            dot_scr[b] = do_t[:, b * BLOCK:(b + 1) * BLOCK]
            qt_scr[b] = qt_ref[:, b * BLOCK:(b + 1) * BLOCK]
        _fill_kv(kv_scr, kvh_ref[...], kv_ref[...], tm)
        dkv_scr[:, 0:tm] = jnp.zeros((n_kv, tm, HEAD_DIM), F32)
        dkv_scr[:, tm:tm + BLOCK] = carry[...]
        band = _attn_band()

        def block(b, c):
            r0 = pl.multiple_of(b * BLOCK, BLOCK)
            has_previous = jnp.logical_or(ti > 0, b > 0)
            for g in range(N_KV_HEADS):
                kk = kv_scr[g, pl.ds(r0, 2 * BLOCK), :]
                vv = kv_scr[N_KV_HEADS + g, pl.ds(r0, 2 * BLOCK), :]
                q_all = _heads_on_lanes(qt_scr, b, g, group)
                do_all = _heads_on_lanes(dot_scr, b, g, group)
                s_all = _dot(kk, q_all)
                dp_all = _dot(vv, do_all)
                probs, dscores = [], []
                for hh in range(group):
                    h = g * group + hh
                    cols = slice(hh * BLOCK, (hh + 1) * BLOCK)
                    p, p_sink = _softmax_with_sink(s_all[:, cols], _slope(h, nq), band, has_previous, sink_ref[h])
                    dp = _fold(dp_all[:, cols], band[0])
                    rs = jnp.sum(p * dp, axis=0, keepdims=True)
                    acc_s[h:h + 1, :] += -(p_sink * rs)
                    ds = p * (dp - rs) * (1.0 / math.sqrt(HEAD_DIM))
                    probs.append(_unfold(p.astype(BF16), band[0]))
                    dscores.append(_unfold(ds.astype(BF16), band[0]))
                p_all = jnp.concatenate(probs, axis=1)
                ds_all = jnp.concatenate(dscores, axis=1)
                dq_all = _dot_tn(kk, ds_all)
                for hh in range(group):
                    h = g * group + hh
                    dqt_scr[b, h * HEAD_DIM:(h + 1) * HEAD_DIM, :] = dq_all[:, hh * BLOCK:(hh + 1) * BLOCK]
                dkv_scr[g, pl.ds(r0, 2 * BLOCK), :] += _dot_nt(ds_all, q_all)
                dkv_scr[N_KV_HEADS + g, pl.ds(r0, 2 * BLOCK), :] += _dot_nt(p_all, do_all)
            return c

        lax.fori_loop(0, nb, block, 0, unroll=2)
        carry[...] = dkv_scr[:, 0:BLOCK]
        dkv = jnp.concatenate([dkv_scr[j, BLOCK:BLOCK + tm] for j in range(n_kv)], axis=1)
        acc_kv[...] += _colsum8(dkv)
        dkvb = dkv.astype(BF16)
        dkv_ref[...] = dkvb
        dqt = jnp.concatenate([dqt_scr[b] for b in range(nb)], axis=1)
        for b in range(nb):
            acc_q[...] += dqt_scr[b]
        dqtb = dqt.astype(BF16)
        dqt_ref[...] = dqtb
        dx = ALPHA * dz + _dot_tn(dqtb, wqt_ref[...]) + _dot_nt(dkvb, wkv_ref[...])
        dz_in, zhat_in = _ln_bwd(dx, zin_ref[...], gin_ref[...])
        acc_ig[...] += _colsum8(dx * zhat_in)
        acc_ib[...] += _colsum8(dx)
        dx_ref[...] = dz_in
        _write_sums(i, nt, [(dbkv_ref, acc_kv), (ding_ref, acc_ig), (dinb_ref, acc_ib)])

        @pl.when(i == nt - 1)
        def _():
            dbq_ref[...] = jnp.sum(acc_q[...], axis=1, keepdims=True)
            dsink_ref[...] = jnp.sum(acc_s[...], axis=1, keepdims=True)

    rev = lambda w: pl.BlockSpec((tm, w), lambda i: (nt - 1 - i, 0))
    rev_cols = pl.BlockSpec((d, tm), lambda i: (0, nt - 1 - i))
    vec = _const((1, d))
    return _tc_call(
        body, name="bwd_attn", nt=nt,
        in_specs=[pl.BlockSpec(memory_space=pltpu.SMEM), rev(d), rev_cols, rev(2 * KVD),
                  pl.BlockSpec((BLOCK, 2 * KVD), lambda i: (jnp.maximum((nt - 1 - i) * hb - 1, 0), 0)),
                  _const((d, d)), _const((d, d)), _const((d, 2 * KVD)), rev(d), vec],
        out_specs=[rev_cols, rev(2 * KVD), rev(d), _acc_out((d, 1)), _acc_out((1, 2 * KVD)), _acc_out((nq, 1)),
                   _acc_out((1, d)), _acc_out((1, d))],
        out_shape=[jax.ShapeDtypeStruct((d, t), BF16), jax.ShapeDtypeStruct((t, 2 * KVD), BF16),
                   jax.ShapeDtypeStruct((t, d), F32), jax.ShapeDtypeStruct((d, 1), F32),
                   jax.ShapeDtypeStruct((1, 2 * KVD), F32), jax.ShapeDtypeStruct((nq, 1), F32)]
        + [jax.ShapeDtypeStruct((1, d), F32)] * 2,
        scratch_shapes=[pltpu.VMEM((n_kv, tm + BLOCK, HEAD_DIM), BF16), pltpu.VMEM((n_kv, tm + BLOCK, HEAD_DIM), F32),
                        pltpu.VMEM((nb, d, BLOCK), BF16), pltpu.VMEM((nb, d, BLOCK), BF16),
                        pltpu.VMEM((nb, d, BLOCK), F32), pltpu.VMEM((n_kv, BLOCK, HEAD_DIM), F32),
                        pltpu.VMEM((d, BLOCK), F32), pltpu.VMEM((8, 2 * KVD), F32), pltpu.VMEM((nq, BLOCK), F32),
                        pltpu.VMEM((8, d), F32), pltpu.VMEM((8, d), F32)],
        operands=(sinks, dz_all, qt, kv, kv, wo, wqt, wkv, z_in, g_in), riders=riders)


def _bwd_conv_head(dz, c, w2, lng, lnb, tm, riders=()):
    t, d = dz.shape
    nt = t // tm

    def body(dz_ref, c_ref, w2_ref, lg_ref, lb_ref, s_ref, dc_ref, dlg_ref, dlb_ref, a3, a4):
        i = pl.program_id(0)
        _acc_init(i, a3, a4)
        dz = dz_ref[...]
        chat, rstd = _ln_stats(c_ref[...])
        n = chat * lg_ref[...] + lb_ref[...]
        act, dact = _silu_and_grad(n)
        s_ref[...] = act.astype(BF16)
        dn = _dot_nt(dz.astype(BF16), w2_ref[...]) * dact
        a3[...] += _colsum8(dn * chat)
        a4[...] += _colsum8(dn)
        dch = dn * lg_ref[...]
        m1 = jnp.mean(dch, axis=-1, keepdims=True)
        m2 = jnp.mean(dch * chat, axis=-1, keepdims=True)
        dc_ref[...] = rstd * (dch - m1 - chat * m2)
        _write_sums(i, nt, [(dlg_ref, a3), (dlb_ref, a4)])

    vec = _const((1, d))
    return _tc_call(
        body, name="bwd_conv_head", nt=nt,
        in_specs=[_rows(tm, d), _rows(tm, d), _const((d, d)), vec, vec],
        out_specs=[_rows(tm, d), _rows(tm, d)] + [_acc_out((1, d))] * 2,
        out_shape=[jax.ShapeDtypeStruct((t, d), BF16), jax.ShapeDtypeStruct((t, d), F32)]
        + [jax.ShapeDtypeStruct((1, d), F32)] * 2,
        scratch_shapes=[pltpu.VMEM((8, d), F32)] * 2, operands=(dz, c, w2, lng, lnb), riders=riders)


def _bwd_conv_glu(dc, u, a, g, dz, wdw, w1s, tm, riders=()):
    t, d = dc.shape
    dh_w = d // 2
    nt = t // tm
    hb = tm // CONV_HALO
    last_halo = t // CONV_HALO - 1

    def body(dc_ref, dcn_ref, u_ref, a_ref, g_ref, dz_ref, w_ref, w1_ref,
             dx_ref, dh_ref, db1_ref, dbdw_ref, dw_ref, ext, sh, du_scr, acc_b1, acc_bdw, acc_w):
        i = pl.program_id(0)
        _acc_init(i, acc_b1, acc_bdw, acc_w)
        dcv = dc_ref[...]
        acc_bdw[...] += _colsum8(dcv)

        ext[0:tm] = dcv
        ext[tm:tm + CONV_HALO] = jnp.where(i == nt - 1, 0.0, dcn_ref[...])
        ext[tm + CONV_HALO:tm + CONV_HALO + 8] = jnp.zeros((8, d), F32)
        _fill_shifted(sh, ext)

        def du_chunk(r, carry):
            base = pl.multiple_of(r * CONV_CHUNK, CONV_CHUNK)
            _tap_sum(w_ref, ext, sh, base, d, lambda k: CONV_WIDTH - 1 - k, du_scr)
            return carry

        lax.fori_loop(0, tm // CONV_CHUNK, du_chunk, 0)

        def dw_chunk(r, carry):
            base = pl.multiple_of(r * CONV_CHUNK, CONV_CHUNK)
            groups = CONV_CHUNK // 8
            for lg in range(d // LANES):
                ls = slice(lg * LANES, (lg + 1) * LANES)
                uv = u_ref[pl.ds(base, CONV_CHUNK), ls].reshape(groups, 8, LANES)
                for k in range(CONV_WIDTH):
                    x = _ext_rows(ext, sh, CONV_WIDTH - 1 - k, base, CONV_CHUNK, ls).reshape(groups, 8, LANES)
                    acc_w[k, :, ls] += jnp.sum(uv * x, axis=0)
            return carry

        lax.fori_loop(0, tm // CONV_CHUNK, dw_chunk, 0)

        du = du_scr[...]
        av = a_ref[...].astype(F32)
        sg = jax.nn.sigmoid(g_ref[...].astype(F32))
        da = du * sg
        dg = du * av * sg * (1.0 - sg)
        acc_b1[:, 0:d] += _colsum8(da)
        acc_b1[:, d:2 * d] += _colsum8(dg)
        dx = ALPHA * dz_ref[...]
        for j, part in enumerate([da[:, 0:dh_w], da[:, dh_w:d], dg[:, 0:dh_w], dg[:, dh_w:d]]):
            pb = part.astype(BF16)
            dh_ref[j] = pb
            dx = dx + _dot_nt(pb, w1_ref[j])
        dx_ref[...] = dx

        @pl.when(i == nt - 1)
        def _():
            db1_ref[...] = jnp.sum(acc_b1[...], axis=0, keepdims=True)
            dbdw_ref[...] = jnp.sum(acc_bdw[...], axis=0, keepdims=True)
            dw_ref[...] = jnp.sum(acc_w[...], axis=1)

    return _tc_call(
        body, name="bwd_conv_glu", nt=nt,
        in_specs=[_rows(tm, d), pl.BlockSpec((CONV_HALO, d), lambda i: (jnp.minimum((i + 1) * hb, last_halo), 0)),
                  _rows(tm, d), _rows(tm, d), _rows(tm, d), _rows(tm, d), _const((CONV_HALO, 8, d)),
                  _const((4, d, dh_w))],
        out_specs=[_rows(tm, d), pl.BlockSpec((4, tm, dh_w), lambda i: (0, i, 0)), _acc_out((1, 2 * d)),
                   _acc_out((1, d)), _acc_out((CONV_HALO, d))],
        out_shape=[jax.ShapeDtypeStruct((t, d), F32), jax.ShapeDtypeStruct((4, t, dh_w), BF16),
                   jax.ShapeDtypeStruct((1, 2 * d), F32), jax.ShapeDtypeStruct((1, d), F32),
                   jax.ShapeDtypeStruct((CONV_HALO, d), F32)],
        scratch_shapes=[pltpu.VMEM((tm + CONV_HALO + 8, d), F32), pltpu.VMEM((7, tm + CONV_HALO, d), F32),
                        pltpu.VMEM((tm, d), F32), pltpu.VMEM((8, 2 * d), F32), pltpu.VMEM((8, d), F32),
                        pltpu.VMEM((CONV_HALO, 8, d), F32)],
        operands=(dc, dc, u, a, g, dz, wdw, w1s), riders=riders)


def _adamw_update(w_ref, g_ref, m_ref, v_ref, d_ref, nm_ref, nv_ref):
    gv = g_ref[...]
    nm = ADAM_B1 * m_ref[...] + (1.0 - ADAM_B1) * gv
    nv = ADAM_B2 * v_ref[...] + (1.0 - ADAM_B2) * (gv * gv)
    m_hat = nm / (1.0 - ADAM_B1 ** ADAM_STEP)
    v_hat = nv / (1.0 - ADAM_B2 ** ADAM_STEP)
    d_ref[...] = -ADAM_LR * (m_hat / (jnp.sqrt(v_hat) + ADAM_EPS) + ADAM_WD * w_ref[...])
    nm_ref[...] = nm
    nv_ref[...] = nv


ADAMW_STEPS = 8


def _adamw(params, name, riders=()):
    n = len(params)
    steps = ADAMW_STEPS if all(p[0].shape[0] % (8 * ADAMW_STEPS) == 0 for p in params) else 1

    def body(*refs):
        for k in range(n):
            _adamw_update(*refs[4 * k:4 * k + 4], *refs[4 * n + 3 * k:4 * n + 3 * k + 3])

    specs = [pl.BlockSpec((p[0].shape[0] // steps, p[0].shape[1]), lambda i: (i, 0)) for p in params]
    outs, rider_outs = _tc_call(
        body, name=name, nt=steps, in_specs=[s for s in specs for _ in range(4)],
        out_specs=[s for s in specs for _ in range(3)],
        out_shape=[jax.ShapeDtypeStruct(p[0].shape, F32) for p in params for _ in range(3)],
        operands=[a for p in params for a in p], riders=riders)
    return [tuple(outs[3 * k:3 * k + 3]) for k in range(n)], rider_outs


def _pad_to(v, n):
    return jnp.pad(v, (0, n - v.shape[0]))


def _round_up(n, m):
    return (n + m - 1) // m * m


def kernel(x, conv_w_pw1, conv_b_pw1, conv_w_dw, conv_b_dw, conv_ln_g, conv_ln_b, conv_w_pw2, conv_b_pw2, kv_w_k, kv_b_k, kv_w_v, kv_b_v, attn_w_q, attn_b_q, attn_sinks, attn_w_o, attn_b_o, ffn_w_gate, ffn_w_up, ffn_w_down, ln_mix_g, ln_mix_b, ln_ffn_g, ln_ffn_b, loss_target, m_conv_w_pw1, m_conv_b_pw1, m_conv_w_dw, m_conv_b_dw, m_conv_ln_g, m_conv_ln_b, m_conv_w_pw2, m_conv_b_pw2, m_kv_w_k, m_kv_b_k, m_kv_w_v, m_kv_b_v, m_attn_w_q, m_attn_b_q, m_attn_sinks, m_attn_w_o, m_attn_b_o, m_ffn_w_gate, m_ffn_w_up, m_ffn_w_down, m_ln_mix_g, m_ln_mix_b, m_ln_ffn_g, m_ln_ffn_b, v_conv_w_pw1, v_conv_b_pw1, v_conv_w_dw, v_conv_b_dw, v_conv_ln_g, v_conv_ln_b, v_conv_w_pw2, v_conv_b_pw2, v_kv_w_k, v_kv_b_k, v_kv_w_v, v_kv_b_v, v_attn_w_q, v_attn_b_q, v_attn_sinks, v_attn_w_o, v_attn_b_o, v_ffn_w_gate, v_ffn_w_up, v_ffn_w_down, v_ln_mix_g, v_ln_mix_b, v_ln_ffn_g, v_ln_ffn_b):
    args = dict(locals())
    w = {n: args[n] for n in WEIGHTS}
    mom = {n: args["m_" + n] for n in WEIGHTS}
    var = {n: args["v_" + n] for n in WEIGHTS}
    assert x.shape[0] == 1, "one sequence per device"
    t, d = x.shape[1], x.shape[2]
    dq = d // 4
    fs = ffn_w_gate.shape[-1]
    nq = d // HEAD_DIM
    x0 = x.reshape(t, d)
    target = loss_target.reshape(t, d)
    tm_big = min(512, t)
    tm_tn = min(1024, t)
    c_idx = lax.axis_index("c")

    me_idx = 2 * lax.axis_index("x") + lax.axis_index("y")

    def gather_buffer(v):
        buf = lax.empty((N_CHIPS,) + v.shape, v.dtype)
        return lax.dynamic_update_slice(buf, v[None], (me_idx,) + (0,) * v.ndim)

    def halves(v):
        return v.reshape(2, -1, v.shape[-1])

    small_sizes = [int(w[n].size) for n in SMALL_SHARDED]
    rs = _round_up(sum(small_sizes), 8 * 128) // 128
    spack = _pad_to(jnp.concatenate([w[n].reshape(-1) for n in SMALL_SHARDED]), rs * 128).reshape(rs, 128)
    conv_first = ['conv_w_pw1', 'conv_w_pw2']
    later = [n for n in BIG if n not in conv_first]
    (first_out,) = _run_riders(
        [_all_gather_rider([gather_buffer(halves(w[n].astype(BF16))) for n in conv_first], gather_buffer(spack))],
        "all_gather_conv")
    later_rider = _all_gather_rider([gather_buffer(halves(w[n].astype(BF16))) for n in later])
    gs = first_out[-1].reshape(N_CHIPS, rs * 128)
    full = {n: g.reshape((N_CHIPS,) + w[n].shape) for n, g in zip(conv_first, first_out)}
    off = 0
    for n, size in zip(SMALL_SHARDED, small_sizes):
        full[n] = gs[:, off:off + size].reshape((N_CHIPS,) + w[n].shape)
        off += size
    w1s = full['conv_w_pw1'].reshape(N_CHIPS, d, d // 2)
    w2 = full['conv_w_pw2'].reshape(d, d)
    b1 = full['conv_b_pw1'].reshape(1, 2 * d)
    wdw = jnp.pad(full['conv_w_dw'].reshape(N_CHIPS, CONV_WIDTH, dq).transpose(1, 0, 2).reshape(CONV_WIDTH, d),
                  ((0, CONV_HALO - CONV_WIDTH), (0, 0)))
    wdw = jnp.broadcast_to(wdw[:, None, :], (CONV_HALO, 8, d))
    bdw = full['conv_b_dw'].reshape(1, d)
    clng = full['conv_ln_g'].reshape(1, d)
    clnb = full['conv_ln_b'].reshape(1, d)
    b2 = full['conv_b_pw2'].reshape(1, d)
    bkv = jnp.concatenate([kv_b_k, kv_b_v]).reshape(1, 2 * KVD)
    sinks = attn_sinks.reshape(nq)
    mixg = [ln_mix_g[l].reshape(1, d) for l in range(DEPTH)]
    mixb = [ln_mix_b[l].reshape(1, d) for l in range(DEPTH)]
    ffng = [ln_ffn_g[l].reshape(1, d) for l in range(DEPTH)]
    ffnb = [ln_ffn_b[l].reshape(1, d) for l in range(DEPTH)]

    a_act, g_act, u_act = _fwd_pw1_glu(x0, w1s, b1, tm_big)
    (c_act, z1, x1), (later_out,) = _fwd_conv_tail(u_act, x0, wdw, bdw, clng, clnb, w2, b2, mixg[0], mixb[0], tm_big,
                                                   riders=[later_rider])
    full.update({n: g.reshape((N_CHIPS,) + w[n].shape) for n, g in zip(later, later_out)})
    wkv = jnp.concatenate([full['kv_w_k'].reshape(d, KVD), full['kv_w_v'].reshape(d, KVD)], axis=1)
    wqt = full['attn_w_q'].reshape(d, d).T
    wo = full['attn_w_o'].reshape(d, d)
    wg, wu, wd = full['ffn_w_gate'], full['ffn_w_up'], full['ffn_w_down']
    act0, bm0, hm0, z2, x2 = _fwd_ffn(x1, wg, wu, wd, 0, ffng[0], ffnb[0], tm_big)
    qt_act, kv_act, ot_act, z3, x3 = _fwd_attn(x2, wqt, attn_b_q.reshape(d, 1), wkv, bkv, sinks, wo, attn_b_o,
                                               mixg[1], mixb[1], tm_big)
    act1, bm1, hm1, dz4, d_fg1, d_fb1, loss_part = _fwd_ffn(x3, wg, wu, wd, 1, ffng[1], ffnb[1], tm_big, target=target)
    loss = lax.psum(loss_part[0, 0], ("x", "y", "c"))

    c_arr = c_idx.reshape(1).astype(jnp.int32)

    def halves4(v):
        return v.reshape(N_CHIPS, 2, -1, v.shape[-1])

    def arrays(group):
        return [p for _, p in group]

    def pair_sums(group, got):
        return _pair_sums(arrays(group), got, c_arr, "grad_pair_sum_" + group[0][0])

    pos_arr = jnp.stack([me_idx, c_idx]).astype(jnp.int32)

    def chip_sums(group, sums, got):
        return _chip_sums(sums, got, pos_arr, "grad_chip_sum_" + group[0][0])

    (dgg1, duu1, dz3, d_mg1, d_mb1, d_bo), _ = _bwd_ffn_dx(dz4, act1, bm1, wg, wu, wd, 1, z3, mixg[1], tm_big)
    g1 = [("ffn_w_gate1", halves4(_matmul_tn(x3[None], dgg1, tm_tn, "dw_gate1"))),
          ("ffn_w_up1", halves4(_matmul_tn(x3[None], duu1, tm_tn, "dw_up1"))),
          ("ffn_w_down1", halves4(_matmul_tn(hm1, dz4[None], tm_tn, "dw_down1")))]
    (dqt, dkv, dz2, d_bq, d_bkv, d_sinks, d_fg0, d_fb0), (got1,) = _bwd_attn(
        dz3, qt_act, kv_act, sinks, wo, wqt, wkv, z2, ffng[0], tm_big, riders=[_pair_exchange_rider(arrays(g1))])
    s1 = pair_sums(g1, got1)
    dwo = _matmul_nn(ot_act, dz3, tm_tn, "dw_o")
    dwq = _matmul_nn(dqt, x2, tm_tn, "dw_q").T
    dwkv = _matmul_tn(x2[None], dkv[None], tm_tn, "dw_kv")[0]
    g2 = [("attn_w_o", halves4(dwo)), ("attn_w_q", halves4(dwq)),
          ("kv_w_k", halves4(dwkv[:, 0:KVD])), ("kv_w_v", halves4(dwkv[:, KVD:2 * KVD]))]
    (dgg0, duu0, dz1, d_mg0, d_mb0, d_b2), (from_chips1, got2) = _bwd_ffn_dx(
        dz2, act0, bm0, wg, wu, wd, 0, z1, mixg[0], tm_big,
        riders=[_chip_scatter_rider(s1), _pair_exchange_rider(arrays(g2))])
    f1 = chip_sums(g1, s1, from_chips1)
    s2 = pair_sums(g2, got2)
    g3 = [("ffn_w_gate0", halves4(_matmul_tn(x1[None], dgg0, tm_tn, "dw_gate0"))),
          ("ffn_w_up0", halves4(_matmul_tn(x1[None], duu0, tm_tn, "dw_up0"))),
          ("ffn_w_down0", halves4(_matmul_tn(hm0, dz2[None], tm_tn, "dw_down0")))]
    (s_act, dc, d_clng, d_clnb), (got3, shared1) = _bwd_conv_head(
        dz1, c_act, w2, clng, clnb, tm_big, riders=[_pair_exchange_rider(arrays(g3)), _pair_share_rider(f1)])
    s3 = pair_sums(g3, got3)
    dw2 = _matmul_tn(s_act[None], dz1[None], tm_tn, "dw_pw2")
    (dx0, dh1, d_b1, d_bdw, d_wdw), (from_chips2, from_chips3) = _bwd_conv_glu(
        dc, u_act, a_act, g_act, dz1, wdw, w1s, tm_big, riders=[_chip_scatter_rider(s2), _chip_scatter_rider(s3)])
    f2 = chip_sums(g2, s2, from_chips2)
    f3 = chip_sums(g3, s3, from_chips3)
    dw1, grad_x = _matmul_tn(x0[None], dh1, tm_tn, "dw_pw1", carry=dx0)

    def rows4(v):
        return v.reshape(N_CHIPS, -1)

    def rep4(v):
        return jnp.broadcast_to(v.reshape(1, -1), (N_CHIPS, v.size))

    local = {
        'conv_b_pw1': rows4(d_b1),
        'conv_w_dw': rows4(d_wdw[0:CONV_WIDTH].reshape(CONV_WIDTH, N_CHIPS, dq).transpose(1, 0, 2)),
        'conv_b_dw': rows4(d_bdw), 'conv_ln_g': rows4(d_clng), 'conv_ln_b': rows4(d_clnb), 'conv_b_pw2': rows4(d_b2),
        'kv_b_k': rep4(d_bkv[:, 0:KVD]), 'kv_b_v': rep4(d_bkv[:, KVD:2 * KVD]), 'attn_b_q': rep4(d_bq),
        'attn_sinks': rep4(d_sinks), 'attn_b_o': rep4(d_bo),
        'ln_mix_g': rep4(jnp.concatenate([d_mg0, d_mg1])), 'ln_mix_b': rep4(jnp.concatenate([d_mb0, d_mb1])),
        'ln_ffn_g': rep4(jnp.concatenate([d_fg0, d_fg1])), 'ln_ffn_b': rep4(jnp.concatenate([d_fb0, d_fb1])),
    }
    n_small = sum(int(w[n].size) for n in SMALL)
    small_rows = _round_up(n_small, 2 * SUM_STEPS * 8 * 128) // 128
    small_local = jnp.concatenate([local[n] for n in SMALL], axis=1)
    small_local = jnp.pad(small_local, ((0, 0), (0, small_rows * 128 - n_small)))
    g4 = [("conv_w_pw1", halves4(dw1)), ("conv_w_pw2", halves4(dw2)),
          ("small", small_local.reshape(N_CHIPS, 2, small_rows // 2, 128))]
    got4, shared23 = _run_riders([_pair_exchange_rider(arrays(g4)), _pair_share_rider(f2 + f3)],
                                 "grad_pair_exchange_last")
    s4 = pair_sums(g4, got4)
    reduced = dict(zip([n for n, _ in g1], shared1))
    reduced.update(zip([n for n, _ in g2 + g3], shared23))
    for n in ('ffn_w_gate', 'ffn_w_up', 'ffn_w_down'):
        reduced[n] = jnp.stack([reduced[n + str(layer)].reshape(w[n].shape[1:]) for layer in range(DEPTH)])

    g_out, delta, new_m, new_v = {}, {}, {}, {}

    def adamw_matrices(names, name, riders=()):
        for n in names:
            g_out[n] = reduced[n].reshape(w[n].shape)
        two_d = [tuple(tree[n].reshape(-1, w[n].shape[-1]) for tree in (w, g_out, mom, var)) for n in names]
        results, rider_outs = _adamw(two_d, name, riders)
        for n, (dl, nm, nv) in zip(names, results):
            delta[n], new_m[n], new_v[n] = (r.reshape(w[n].shape) for r in (dl, nm, nv))
        return rider_outs

    (from_chips4,) = adamw_matrices([n for n in BIG if n not in conv_first], "adamw_attn_ffn",
                                    riders=[_chip_scatter_rider(s4)])
    f4 = chip_sums(g4, s4, from_chips4)
    (shared4,) = _run_riders([_pair_share_rider(f4)], "grad_pair_share_last")
    reduced.update(zip([n for n, _ in g4], shared4))
    adamw_matrices(conv_first, "adamw_conv")

    def pack_small(tree):
        return _pad_to(jnp.concatenate([tree[n].reshape(-1) for n in SMALL]), small_rows * 128).reshape(small_rows, 128)

    g_small = reduced['small'].reshape(small_rows, 128)
    ((dl, nm, nv),), _ = _adamw([(pack_small(w), g_small, pack_small(mom), pack_small(var))], "adamw_small")
    off = 0
    for n in SMALL:
        size, shape = int(w[n].size), w[n].shape
        for tree, flat in ((g_out, g_small), (delta, dl), (new_m, nm), (new_v, nv)):
            tree[n] = flat.reshape(-1)[off:off + size].reshape(shape)
        off += size

    return (loss, grad_x.reshape(x.shape), *[g_out[n] for n in WEIGHTS], *[delta[n] for n in WEIGHTS],
            *[new_m[n] for n in WEIGHTS], *[new_v[n] for n in WEIGHTS])
```

```python
import math

import jax
import jax.numpy as jnp
from jax import lax
from jax.experimental import pallas as pl
from jax.experimental.pallas import tpu as pltpu

F32 = jnp.float32
BF16 = jnp.bfloat16

DEPTH = 2
ALPHA = (2.0 * DEPTH) ** 0.25
LN_EPS = 1e-5
NEG_INF = -1e30
HEAD_DIM = 64
N_KV_HEADS = 2
KVD = N_KV_HEADS * HEAD_DIM
BLOCK = 128
CONV_WIDTH = 31
CONV_HALO = 32
ALIBI_MAX = 8.0
ADAM_LR, ADAM_B1, ADAM_B2, ADAM_EPS, ADAM_WD, ADAM_STEP = 0.001, 0.9, 0.999, 1e-08, 0.01, 10

N_CHIPS = 4
SUM_STEPS = 4
VMEM_LIMIT = 60 * 1024 * 1024
MESH = pl.DeviceIdType.MESH

NT_DIMS = (((1,), (1,)), ((), ()))
TN_DIMS = (((0,), (0,)), ((), ()))

WEIGHTS = ['conv_w_pw1', 'conv_b_pw1', 'conv_w_dw', 'conv_b_dw', 'conv_ln_g', 'conv_ln_b', 'conv_w_pw2', 'conv_b_pw2',
           'kv_w_k', 'kv_b_k', 'kv_w_v', 'kv_b_v', 'attn_w_q', 'attn_b_q', 'attn_sinks', 'attn_w_o', 'attn_b_o',
           'ffn_w_gate', 'ffn_w_up', 'ffn_w_down', 'ln_mix_g', 'ln_mix_b', 'ln_ffn_g', 'ln_ffn_b']
BIG = ['conv_w_pw1', 'conv_w_pw2', 'kv_w_k', 'kv_w_v', 'attn_w_q', 'attn_w_o', 'ffn_w_gate', 'ffn_w_up', 'ffn_w_down']
SMALL_SHARDED = ['conv_b_pw1', 'conv_w_dw', 'conv_b_dw', 'conv_ln_g', 'conv_ln_b', 'conv_b_pw2']
REPLICATED = ['kv_b_k', 'kv_b_v', 'attn_b_q', 'attn_sinks', 'attn_b_o', 'ln_mix_g', 'ln_mix_b', 'ln_ffn_g', 'ln_ffn_b']
SMALL = SMALL_SHARDED + REPLICATED


def _cparams(n_grid=1):
    return pltpu.CompilerParams(dimension_semantics=("arbitrary",) * n_grid, vmem_limit_bytes=VMEM_LIMIT)


def _rows(tm, width):
    return pl.BlockSpec((tm, width), lambda i: (i, 0))


def _const(shape):
    return pl.BlockSpec(shape, lambda *_: (0,) * len(shape), pipeline_mode=pl.Buffered(1))


def _acc_out(shape):
    return pl.BlockSpec(shape, lambda *_: (0,) * len(shape))


def _dot(a, b):
    return jnp.dot(a, b, preferred_element_type=F32)


def _dot_nt(a, b):
    return lax.dot_general(a, b, NT_DIMS, preferred_element_type=F32)


def _dot_tn(a, b):
    return lax.dot_general(a, b, TN_DIMS, preferred_element_type=F32)


def _colsum8(v):
    m, n = v.shape
    return jnp.sum(v.reshape(m // 8, 8, n), axis=0)


def _ln_stats(z):
    mu = jnp.mean(z, axis=-1, keepdims=True)
    zc = z - mu
    var = jnp.mean(zc * zc, axis=-1, keepdims=True)
    rstd = lax.rsqrt(var + LN_EPS)
    return zc * rstd, rstd


def _ln_fwd(z, g, b):
    zhat, _ = _ln_stats(z)
    return zhat * g + b


def _ln_bwd(dy, z, g):
    zhat, rstd = _ln_stats(z)
    dzh = dy * g
    m1 = jnp.mean(dzh, axis=-1, keepdims=True)
    m2 = jnp.mean(dzh * zhat, axis=-1, keepdims=True)
    return rstd * (dzh - m1 - zhat * m2), zhat


def _silu_and_grad(n):
    sg = jax.nn.sigmoid(n)
    return n * sg, sg * (1.0 + n * (1.0 - sg))


def _acc_init(i, *refs):
    @pl.when(i == 0)
    def _():
        for r in refs:
            r[...] = jnp.zeros_like(r)


def _mesh_pos():
    x, y, c = lax.axis_index("x"), lax.axis_index("y"), lax.axis_index("c")
    chips = [(1 - x, y), (x, 1 - y), (1 - x, 1 - y)]
    return x, y, c, chips


HBM_SPEC = pl.BlockSpec(memory_space=pltpu.HBM)


def _remote(src, dst, send_sems, recv_sems, k, to):
    return pltpu.make_async_remote_copy(src_ref=src, dst_ref=dst, send_sem=send_sems.at[k], recv_sem=recv_sems.at[k],
                                        device_id=to, device_id_type=MESH)


class _Rider:
    def __init__(self, operands, out_shapes, sem_shapes, start, finish, mid=None, in_place=False):
        self.operands, self.out_shapes, self.sem_shapes = list(operands), list(out_shapes), list(sem_shapes)
        self.start, self.finish, self.mid = start, finish, mid
        self.in_place = in_place


def _rider_aliases(riders, first_in, first_out):
    aliases, k_in, k_out = {}, first_in, first_out
    for r in riders:
        if r.in_place:
            aliases.update({k_in + k: k_out + k for k in range(len(r.operands))})
        k_in += len(r.operands)
        k_out += len(r.out_shapes)
    return aliases


def _split(refs, counts):
    parts, k = [], 0
    for n in counts:
        parts.append(refs[k:k + n])
        k += n
    return parts


def _rider_refs(riders, ins, outs, sems):
    return list(zip(riders, _split(ins, [len(r.operands) for r in riders]),
                    _split(outs, [len(r.out_shapes) for r in riders]),
                    _split(sems, [len(r.sem_shapes) for r in riders])))


def _tc_call(body, *, name, nt, in_specs, out_specs, out_shape, operands, scratch_shapes=(), riders=(), mid_frac=0.75):
    n_in, n_out, n_scr = len(in_specs), len(out_specs), len(scratch_shapes)
    r_ops = [o for r in riders for o in r.operands]
    r_outs = [o for r in riders for o in r.out_shapes]
    r_sems = [s for r in riders for s in r.sem_shapes]
    mid_step = min(max(int(nt * mid_frac), 0), nt - 1)

    def full(*refs):
        ins, r_in, outs, r_out, scr, r_sem = _split(refs, [n_in, len(r_ops), n_out, len(r_outs), n_scr, len(r_sems)])
        parts = _rider_refs(riders, r_in, r_out, r_sem)
        step = pl.program_id(0)

        @pl.when(step == 0)
        def _():
            for r, a, b, s in parts:
                r.start(a, b, s)

        body(*ins, *outs, *scr)

        @pl.when(step == mid_step)
        def _():
            for r, a, b, s in parts:
                if r.mid is not None:
                    r.mid(a, b, s)

        @pl.when(step == nt - 1)
        def _():
            for r, a, b, s in parts:
                r.finish(a, b, s)

    res = pl.pallas_call(
        full if riders else body, name=name, grid=(nt,), in_specs=list(in_specs) + [HBM_SPEC] * len(r_ops),
        out_specs=list(out_specs) + [HBM_SPEC] * len(r_outs), out_shape=list(out_shape) + r_outs,
        scratch_shapes=list(scratch_shapes) + r_sems, input_output_aliases=_rider_aliases(riders, n_in, n_out),
        compiler_params=_cparams(),
    )(*operands, *r_ops)
    return res[:n_out], _split(res[n_out:], [len(r.out_shapes) for r in riders])


def _run_riders(riders, name):
    r_ops = [o for r in riders for o in r.operands]
    r_outs = [o for r in riders for o in r.out_shapes]
    r_sems = [s for r in riders for s in r.sem_shapes]

    def body(*refs):
        r_in, r_out, r_sem = _split(refs, [len(r_ops), len(r_outs), len(r_sems)])
        parts = _rider_refs(riders, r_in, r_out, r_sem)
        for r, a, b, s in parts:
            r.start(a, b, s)
        for r, a, b, s in parts:
            if r.mid is not None:
                r.mid(a, b, s)
        for r, a, b, s in parts:
            r.finish(a, b, s)

    res = pl.pallas_call(body, name=name, out_shape=tuple(r_outs), in_specs=[HBM_SPEC] * len(r_ops),
                         out_specs=(HBM_SPEC,) * len(r_outs), scratch_shapes=r_sems,
                         input_output_aliases=_rider_aliases(riders, 0, 0))(*r_ops)
    return _split(list(res), [len(r.out_shapes) for r in riders])


def _all_gather_rider(bufs, small=None):
    n = len(bufs)
    n_small = 0 if small is None else 1

    def copies(outs, sems):
        send_sems, recv_sems = sems
        x, y, c, chips = _mesh_pos()
        me = 2 * x + y
        here, sibling = (x, y, c), (x, y, 1 - c)
        rows = [2 * cx + cy for cx, cy in chips]

        def big(p, k, chip_row, half, to):
            piece = outs[p].at[chip_row, half]
            return _remote(piece, piece, send_sems, recv_sems, 6 * p + k, to)

        first = [big(p, j, me, c, (cx, cy, c)) for p in range(n) for j, (cx, cy) in enumerate(chips)]
        landed = [big(p, j, rows[j], c, here) for p in range(n) for j in range(3)]
        passed = [big(p, 3 + j, rows[j], c, sibling) for p in range(n) for j in range(3)]
        arrivals = [big(p, 3 + j, rows[j], 1 - c, here) for p in range(n) for j in range(3)]
        if n_small:
            first = [_remote(outs[n].at[me], outs[n].at[me], send_sems, recv_sems, 6 * n + j, (cx, cy, c))
                     for j, (cx, cy) in enumerate(chips)] + first
            arrivals += [_remote(outs[n].at[rows[j]], outs[n].at[rows[j]], send_sems, recv_sems, 6 * n + j, here)
                         for j in range(3)]
        return first, landed, passed, arrivals

    def start(ins, outs, sems):
        for cp in copies(outs, sems)[0]:
            cp.start()

    def mid(ins, outs, sems):
        _, landed, passed, _ = copies(outs, sems)
        for got, fwd in zip(landed, passed):
            got.wait_recv()
            fwd.start()

    def finish(ins, outs, sems):
        first, _, passed, arrivals = copies(outs, sems)
        for cp in arrivals:
            cp.wait_recv()
        for cp in first + passed:
            cp.wait_send()

    operands = list(bufs) + ([small] if n_small else [])
    n_sem = 6 * n + 3 * n_small
    return _Rider(operands, [jax.ShapeDtypeStruct(o.shape, o.dtype) for o in operands],
                  [pltpu.SemaphoreType.DMA((n_sem,)), pltpu.SemaphoreType.DMA((n_sem,))], start, finish, mid,
                  in_place=True)


def _pair_exchange_rider(plist):
    n = len(plist)

    def copies(ins, outs, sems):
        x, y, c, _ = _mesh_pos()
        return [_remote(ins[k].at[:, 1 - c], outs[k], sems[0], sems[1], k, (x, y, 1 - c)) for k in range(n)]

    def start(ins, outs, sems):
        for cp in copies(ins, outs, sems):
            cp.start()

    def finish(ins, outs, sems):
        for cp in copies(ins, outs, sems):
            cp.wait()

    return _Rider(plist, [jax.ShapeDtypeStruct((p.shape[0],) + p.shape[2:], p.dtype) for p in plist],
                  [pltpu.SemaphoreType.DMA((n,)), pltpu.SemaphoreType.DMA((n,))], start, finish)


def _pair_sums(plist, gots, c, name):
    n = len(plist)

    def body(c_ref, *refs):
        for k in range(n):
            refs[2 * n + k][...] = refs[k][...] + refs[n + k][...]

    rows = [p.shape[2] // SUM_STEPS for p in plist]
    return pl.pallas_call(
        body, name=name, out_shape=[jax.ShapeDtypeStruct(g.shape, F32) for g in gots],
        grid_spec=pltpu.PrefetchScalarGridSpec(
            num_scalar_prefetch=1, grid=(N_CHIPS, SUM_STEPS),
            in_specs=[pl.BlockSpec((None, None, br, p.shape[3]), lambda j, i, c_ref: (j, c_ref[0], i, 0))
                      for p, br in zip(plist, rows)]
            + [pl.BlockSpec((None, br, p.shape[3]), lambda j, i, c_ref: (j, i, 0)) for p, br in zip(plist, rows)],
            out_specs=[pl.BlockSpec((None, br, p.shape[3]), lambda j, i, c_ref: (j, i, 0))
                       for p, br in zip(plist, rows)]),
        compiler_params=_cparams(2),
    )(c, *plist, *gots)


def _chip_scatter_rider(slist):
    n = len(slist)

    def copies(ins, outs, sems):
        send_sems, recv_sems = sems
        x, y, c, chips = _mesh_pos()
        sends = [_remote(ins[k].at[2 * cx + cy], outs[k].at[j], send_sems, recv_sems, 3 * k + j, (cx, cy, c))
                 for k in range(n) for j, (cx, cy) in enumerate(chips)]
        arrivals = [_remote(ins[k].at[0], outs[k].at[j], send_sems, recv_sems, 3 * k + j, (x, y, c))
                    for k in range(n) for j in range(3)]
        return sends, arrivals

    def start(ins, outs, sems):
        for cp in copies(ins, outs, sems)[0]:
            cp.start()

    def finish(ins, outs, sems):
        sends, arrivals = copies(ins, outs, sems)
        for cp in arrivals:
            cp.wait_recv()
        for cp in sends:
            cp.wait_send()

    return _Rider(slist, [jax.ShapeDtypeStruct((3,) + s.shape[1:], s.dtype) for s in slist],
                  [pltpu.SemaphoreType.DMA((3 * n,)), pltpu.SemaphoreType.DMA((3 * n,))], start, finish)


def _chip_sums(slist, gots, pos, name):
    n = len(slist)

    def body(pos_ref, *refs):
        me = pos_ref[0]
        for k in range(n):
            s_ref, got_ref, out_ref = refs[k], refs[n + k], refs[2 * n + k]
            total = None
            for chip in range(N_CHIPS):
                flip = jnp.bitwise_xor(me, chip)
                term = jnp.where(flip == 0, s_ref[...],
                                 jnp.where(flip == 2, got_ref[0], jnp.where(flip == 1, got_ref[1], got_ref[2])))
                total = term if total is None else total + term
            out_ref[...] = total

    rows = [s.shape[1] // SUM_STEPS for s in slist]
    return pl.pallas_call(
        body, name=name, out_shape=[jax.ShapeDtypeStruct((2,) + s.shape[1:], F32) for s in slist],
        grid_spec=pltpu.PrefetchScalarGridSpec(
            num_scalar_prefetch=1, grid=(SUM_STEPS,),
            in_specs=[pl.BlockSpec((None, br, s.shape[2]), lambda i, pos_ref: (pos_ref[0], i, 0))
                      for s, br in zip(slist, rows)]
            + [pl.BlockSpec((3, br, s.shape[2]), lambda i, pos_ref: (0, i, 0)) for s, br in zip(slist, rows)],
            out_specs=[pl.BlockSpec((None, br, s.shape[2]), lambda i, pos_ref: (pos_ref[1], i, 0))
                       for s, br in zip(slist, rows)]),
        compiler_params=_cparams(1),
    )(pos, *slist, *gots)


def _pair_share_rider(flist):
    n = len(flist)

    def copies(outs, sems):
        x, y, c, _ = _mesh_pos()
        sends = [_remote(outs[k].at[c], outs[k].at[c], sems[0], sems[1], k, (x, y, 1 - c)) for k in range(n)]
        arrivals = [_remote(outs[k].at[1 - c], outs[k].at[1 - c], sems[0], sems[1], k, (x, y, c)) for k in range(n)]
        return sends, arrivals

    def start(ins, outs, sems):
        for cp in copies(outs, sems)[0]:
            cp.start()

    def finish(ins, outs, sems):
        sends, arrivals = copies(outs, sems)
        for cp in arrivals:
            cp.wait_recv()
        for cp in sends:
            cp.wait_send()

    return _Rider(flist, [jax.ShapeDtypeStruct(f.shape, f.dtype) for f in flist],
                  [pltpu.SemaphoreType.DMA((n,)), pltpu.SemaphoreType.DMA((n,))], start, finish, in_place=True)


def _fwd_pw1_glu(x, w1s, b1, tm):
    t, d = x.shape
    dh = d // 2

    def body(x_ref, w_ref, b_ref, a_ref, g_ref, u_ref):
        xb = x_ref[...].astype(BF16)
        for hh in range(2):
            cs = slice(hh * dh, (hh + 1) * dh)
            a = _dot(xb, w_ref[hh]) + b_ref[:, hh * dh:(hh + 1) * dh]
            g = _dot(xb, w_ref[2 + hh]) + b_ref[:, d + hh * dh:d + (hh + 1) * dh]
            a_ref[:, cs] = a.astype(BF16)
            g_ref[:, cs] = g.astype(BF16)
            u_ref[:, cs] = a * jax.nn.sigmoid(g)

    return pl.pallas_call(
        body, name="fwd_pw1_glu", grid=(t // tm,),
        in_specs=[_rows(tm, d), _const((4, d, dh)), _const((1, 2 * d))],
        out_specs=[_rows(tm, d)] * 3,
        out_shape=[jax.ShapeDtypeStruct((t, d), BF16), jax.ShapeDtypeStruct((t, d), BF16),
                   jax.ShapeDtypeStruct((t, d), F32)],
        compiler_params=_cparams(),
    )(x, w1s, b1)


def _fill_shifted(sh_ref, ext_ref):
    n = sh_ref.shape[1]
    for s in range(1, 8):
        sh_ref[s - 1] = ext_ref[pl.ds(s, n), :]


def _ext_rows(ext, sh, e, base, rows, ls):
    if e % 8 == 0:
        return ext[pl.ds(base + e, rows), ls]
    return sh[e % 8 - 1, pl.ds(base + (e // 8) * 8, rows), ls]


CONV_CHUNK = 64
LANES = 256


def _tap_sum(w_ref, ext, sh, base, d, tap_row, out_ref, bias_ref=None):
    groups = CONV_CHUNK // 8
    for lg in range(d // LANES):
        ls = slice(lg * LANES, (lg + 1) * LANES)
        acc = jnp.zeros((groups, 8, LANES), F32)
        for k in range(CONV_WIDTH):
            x = _ext_rows(ext, sh, tap_row(k), base, CONV_CHUNK, ls)
            acc = acc + w_ref[k, :, ls] * x.reshape(groups, 8, LANES)
        acc = acc.reshape(CONV_CHUNK, LANES)
        out_ref[pl.ds(base, CONV_CHUNK), ls] = acc if bias_ref is None else acc + bias_ref[:, ls]


def _fwd_conv_tail(u, x0, wdw, bdw, lng, lnb, w2, b2, mixg, mixb, tm, riders=()):
    t, d = u.shape
    hb = tm // CONV_HALO

    def body(u_ref, uh_ref, x_ref, w_ref, bdw_ref, lng_ref, lnb_ref, w2_ref, b2_ref, mg_ref, mb_ref,
             c_ref, z_ref, y_ref, ext, sh):
        i = pl.program_id(0)
        ext[0:CONV_HALO] = jnp.where(i == 0, 0.0, uh_ref[...])
        ext[CONV_HALO:CONV_HALO + tm] = u_ref[...]
        ext[CONV_HALO + tm:CONV_HALO + tm + 8] = jnp.zeros((8, d), F32)
        _fill_shifted(sh, ext)

        def chunk(r, carry):
            base = pl.multiple_of(r * CONV_CHUNK, CONV_CHUNK)
            _tap_sum(w_ref, ext, sh, base, d, lambda k: k + CONV_HALO - (CONV_WIDTH - 1), c_ref, bdw_ref)
            return carry

        lax.fori_loop(0, tm // CONV_CHUNK, chunk, 0)
        n = _ln_fwd(c_ref[...], lng_ref[...], lnb_ref[...])
        s = n * jax.nn.sigmoid(n)
        m = _dot(s.astype(BF16), w2_ref[...]) + b2_ref[...]
        z = ALPHA * x_ref[...] + m
        z_ref[...] = z
        y_ref[...] = _ln_fwd(z, mg_ref[...], mb_ref[...])

    vec = _const((1, d))
    return _tc_call(
        body, name="fwd_conv_tail", nt=t // tm,
        in_specs=[_rows(tm, d), pl.BlockSpec((CONV_HALO, d), lambda i: (jnp.maximum(i * hb - 1, 0), 0)), _rows(tm, d),
                  _const((CONV_HALO, 8, d)), vec, vec, vec, _const((d, d)), vec, vec, vec],
        out_specs=[_rows(tm, d)] * 3,
        out_shape=[jax.ShapeDtypeStruct((t, d), F32)] * 3,
        scratch_shapes=[pltpu.VMEM((tm + CONV_HALO + 8, d), F32), pltpu.VMEM((7, tm + CONV_HALO, d), F32)],
        operands=(u, u, x0, wdw, bdw, lng, lnb, w2, b2, mixg, mixb), riders=riders)


def _fwd_ffn(x, wg, wu, wd, layer, lng, lnb, tm, target=None):
    t, d = x.shape
    fs = wg.shape[-1]
    nt = t // tm
    with_loss = target is not None

    def hidden(x_ref, wg_ref, wu_ref, wd_ref, act_ref, bm_ref, hm_ref):
        xv = x_ref[...]
        xb = xv.astype(BF16)
        f = jnp.zeros((tm, d), F32)
        for j in range(N_CHIPS):
            gj = _dot(xb, wg_ref[j])
            uj = _dot(xb, wu_ref[j])
            act, dact = _silu_and_grad(gj)
            act_ref[j] = act.astype(BF16)
            bm_ref[j] = (uj * dact).astype(BF16)
            hmb = (act * uj).astype(BF16)
            hm_ref[j] = hmb
            f = f + _dot(hmb, wd_ref[j])
        return ALPHA * xv + f

    def body(x_ref, wg_ref, wu_ref, wd_ref, g_ref, b_ref, act_ref, bm_ref, hm_ref, z_ref, y_ref):
        z = hidden(x_ref, wg_ref, wu_ref, wd_ref, act_ref, bm_ref, hm_ref)
        z_ref[...] = z
        y_ref[...] = _ln_fwd(z, g_ref[...], b_ref[...])

    def body_loss(x_ref, wg_ref, wu_ref, wd_ref, g_ref, b_ref, t_ref, act_ref, bm_ref, hm_ref, dz_ref,
                  dlg_ref, dlb_ref, loss_ref, acc_g, acc_b, acc_l):
        i = pl.program_id(0)
        _acc_init(i, acc_g, acc_b, acc_l)
        z = hidden(x_ref, wg_ref, wu_ref, wd_ref, act_ref, bm_ref, hm_ref)
        zhat, rstd = _ln_stats(z)
        gain = g_ref[...]
        err = zhat * gain + b_ref[...] - t_ref[...]
        acc_l[...] += _colsum8(err * err)
        dy = err * (1.0 / d)
        acc_g[...] += _colsum8(dy * zhat)
        acc_b[...] += _colsum8(dy)
        dzh = dy * gain
        m1 = jnp.mean(dzh, axis=-1, keepdims=True)
        m2 = jnp.mean(dzh * zhat, axis=-1, keepdims=True)
        dz_ref[...] = rstd * (dzh - m1 - zhat * m2)

        @pl.when(i == nt - 1)
        def _():
            dlg_ref[...] = jnp.sum(acc_g[...], axis=0, keepdims=True)
            dlb_ref[...] = jnp.sum(acc_b[...], axis=0, keepdims=True)
            loss_ref[...] = jnp.sum(acc_l[...], keepdims=True) * (0.5 / d)

    wcol = pl.BlockSpec((N_CHIPS, None, d, fs), lambda i: (0, layer, 0, 0), pipeline_mode=pl.Buffered(1))
    wrow = pl.BlockSpec((N_CHIPS, None, fs, d), lambda i: (0, layer, 0, 0), pipeline_mode=pl.Buffered(1))
    hid = pl.BlockSpec((N_CHIPS, tm, fs), lambda i: (0, i, 0))
    in_specs = [_rows(tm, d), wcol, wcol, wrow, _const((1, d)), _const((1, d))]
    hid_shapes = [jax.ShapeDtypeStruct((N_CHIPS, t, fs), BF16)] * 3
    if not with_loss:
        return pl.pallas_call(
            body, name=f"fwd_ffn{layer}", grid=(nt,), in_specs=in_specs,
            out_specs=[hid, hid, hid, _rows(tm, d), _rows(tm, d)],
            out_shape=hid_shapes + [jax.ShapeDtypeStruct((t, d), F32)] * 2, compiler_params=_cparams(),
        )(x, wg, wu, wd, lng, lnb)
    return pl.pallas_call(
        body_loss, name=f"fwd_ffn{layer}_loss", grid=(nt,), in_specs=in_specs + [_rows(tm, d)],
        out_specs=[hid, hid, hid, _rows(tm, d), _acc_out((1, d)), _acc_out((1, d)), _acc_out((1, 1))],
        out_shape=hid_shapes + [jax.ShapeDtypeStruct((t, d), F32)] + [jax.ShapeDtypeStruct((1, d), F32)] * 2
        + [jax.ShapeDtypeStruct((1, 1), F32)],
        scratch_shapes=[pltpu.VMEM((8, d), F32)] * 3, compiler_params=_cparams(),
    )(x, wg, wu, wd, lng, lnb, target)


def _attn_band():
    kt = lax.broadcasted_iota(jnp.int32, (BLOCK, BLOCK), 0)
    qi = lax.broadcasted_iota(jnp.int32, (BLOCK, BLOCK), 1)
    current = kt <= qi
    delta = qi - kt + jnp.where(current, 0, BLOCK)
    return current, delta.astype(F32)


def _fold(full, current):
    return jnp.where(current, full[BLOCK:2 * BLOCK], full[0:BLOCK])


def _unfold(folded, current):
    zero = jnp.zeros_like(folded)
    return jnp.concatenate([jnp.where(current, zero, folded), jnp.where(current, folded, zero)], axis=0)


def _slope(h, nq):
    return 2.0 ** (-ALIBI_MAX * (h + 1) / nq)


def _softmax_with_sink(s_full, slope, band, has_previous, sink):
    current, delta = band
    s = _fold(s_full, current) * (1.0 / math.sqrt(HEAD_DIM)) - slope * delta
    s = jnp.where(jnp.logical_or(current, has_previous), s, NEG_INF)
    m = jnp.maximum(jnp.max(s, axis=0, keepdims=True), sink)
    p = jnp.exp(s - m)
    e_sink = jnp.exp(sink - m)
    inv = 1.0 / (jnp.sum(p, axis=0, keepdims=True) + e_sink)
    return p * inv, e_sink * inv


def _heads_on_lanes(ref, b, g, group):
    first = g * group
    return jnp.concatenate([ref[b, (first + hh) * HEAD_DIM:(first + hh + 1) * HEAD_DIM, :] for hh in range(group)],
                           axis=1)


def _fill_kv(kv_scr, halo, tile, tm):
    for j in range(2 * N_KV_HEADS):
        kv_scr[j, 0:BLOCK] = halo[:, j * HEAD_DIM:(j + 1) * HEAD_DIM]
        kv_scr[j, BLOCK:BLOCK + tm] = tile[:, j * HEAD_DIM:(j + 1) * HEAD_DIM]


def _cols(d, tm):
    return pl.BlockSpec((d, tm), lambda i: (0, i))


def _fwd_attn(x, wqt, bqt, wkv, bkv, sinks, wo, bo, mixg, mixb, tm):
    t, d = x.shape
    nq = d // HEAD_DIM
    group = nq // N_KV_HEADS
    nb = tm // BLOCK

    def body(sink_ref, x_ref, xh_ref, wqt_ref, bqt_ref, wkv_ref, bkv_ref, wo_ref, bo_ref, mg_ref, mb_ref,
             qt_ref, kv_ref, ot_ref, z_ref, y_ref, kv_scr, qt_scr, ot_scr):
        i = pl.program_id(0)
        xv = x_ref[...]
        xb = xv.astype(BF16)
        qt = (_dot_nt(wqt_ref[...], xb) + bqt_ref[...]).astype(BF16)
        qt_ref[...] = qt
        for b in range(nb):
            qt_scr[b] = qt[:, b * BLOCK:(b + 1) * BLOCK]
        kvb = (_dot(xb, wkv_ref[...]) + bkv_ref[...]).astype(BF16)
        kv_ref[...] = kvb
        _fill_kv(kv_scr, (_dot(xh_ref[...].astype(BF16), wkv_ref[...]) + bkv_ref[...]).astype(BF16), kvb, tm)
        band = _attn_band()

        def block(b, carry):
            r0 = pl.multiple_of(b * BLOCK, BLOCK)
            has_previous = jnp.logical_or(i > 0, b > 0)
            for g in range(N_KV_HEADS):
                kk = kv_scr[g, pl.ds(r0, 2 * BLOCK), :]
                vv = kv_scr[N_KV_HEADS + g, pl.ds(r0, 2 * BLOCK), :]
                s_all = _dot(kk, _heads_on_lanes(qt_scr, b, g, group))
                probs = []
                for hh in range(group):
                    h = g * group + hh
                    p, _ = _softmax_with_sink(s_all[:, hh * BLOCK:(hh + 1) * BLOCK], _slope(h, nq), band,
                                              has_previous, sink_ref[h])
                    probs.append(_unfold(p.astype(BF16), band[0]))
                o_all = _dot_tn(vv, jnp.concatenate(probs, axis=1))
                for hh in range(group):
                    h = g * group + hh
                    ot_scr[b, h * HEAD_DIM:(h + 1) * HEAD_DIM, :] = o_all[:, hh * BLOCK:(hh + 1) * BLOCK].astype(BF16)
            return carry

        lax.fori_loop(0, nb, block, 0, unroll=True)
        ot = jnp.concatenate([ot_scr[b] for b in range(nb)], axis=1)
        ot_ref[...] = ot
        z = ALPHA * xv + _dot_tn(ot, wo_ref[...]) + bo_ref[...]
        z_ref[...] = z
        y_ref[...] = _ln_fwd(z, mg_ref[...], mb_ref[...])

    hb = tm // BLOCK
    vec = _const((1, d))
    return pl.pallas_call(
        body, name="fwd_attn", grid=(t // tm,),
        in_specs=[pl.BlockSpec(memory_space=pltpu.SMEM),
                  _rows(tm, d), pl.BlockSpec((BLOCK, d), lambda i: (jnp.maximum(i * hb - 1, 0), 0)),
                  _const((d, d)), _const((d, 1)), _const((d, 2 * KVD)), _const((1, 2 * KVD)), _const((d, d)), vec, vec,
                  vec],
        out_specs=[_cols(d, tm), _rows(tm, 2 * KVD), _cols(d, tm), _rows(tm, d), _rows(tm, d)],
        out_shape=[jax.ShapeDtypeStruct((d, t), BF16), jax.ShapeDtypeStruct((t, 2 * KVD), BF16),
                   jax.ShapeDtypeStruct((d, t), BF16), jax.ShapeDtypeStruct((t, d), F32),
                   jax.ShapeDtypeStruct((t, d), F32)],
        scratch_shapes=[pltpu.VMEM((2 * N_KV_HEADS, tm + BLOCK, HEAD_DIM), BF16), pltpu.VMEM((nb, d, BLOCK), BF16),
                        pltpu.VMEM((nb, d, BLOCK), BF16)],
        compiler_params=_cparams(),
    )(sinks, x, x, wqt, bqt, wkv, bkv, wo, bo, mixg, mixb)


def _write_sums(i, nt, pairs):
    @pl.when(i == nt - 1)
    def _():
        for out_ref, acc in pairs:
            out_ref[...] = jnp.sum(acc[...], axis=0, keepdims=True)


def _bwd_ffn_dx(dz, act, bm, wg, wu, wd, layer, z_in, g_in, tm, riders=()):
    t, d = dz.shape
    fs = wg.shape[-1]
    nt = t // tm

    def body(dz_ref, act_ref, bm_ref, wg_ref, wu_ref, wd_ref, zin_ref, gin_ref,
             dgg_ref, duu_ref, dzin_ref, dg_ref, db_ref, dsum_ref, acc_g, acc_b, acc_s):
        i = pl.program_id(0)
        _acc_init(i, acc_g, acc_b, acc_s)
        dzv = dz_ref[...]
        dzb = dzv.astype(BF16)
        dx = ALPHA * dzv
        for j in range(N_CHIPS):
            dh = _dot_nt(dzb, wd_ref[j])
            dgb = (dh * bm_ref[j].astype(F32)).astype(BF16)
            dub = (dh * act_ref[j].astype(F32)).astype(BF16)
            dgg_ref[j] = dgb
            duu_ref[j] = dub
            dx = dx + _dot_nt(dgb, wg_ref[j]) + _dot_nt(dub, wu_ref[j])
        dz_in, zhat = _ln_bwd(dx, zin_ref[...], gin_ref[...])
        acc_g[...] += _colsum8(dx * zhat)
        acc_b[...] += _colsum8(dx)
        acc_s[...] += _colsum8(dz_in)
        dzin_ref[...] = dz_in
        _write_sums(i, nt, [(dg_ref, acc_g), (db_ref, acc_b), (dsum_ref, acc_s)])

    wcol = pl.BlockSpec((N_CHIPS, None, d, fs), lambda i: (0, layer, 0, 0), pipeline_mode=pl.Buffered(1))
    wrow = pl.BlockSpec((N_CHIPS, None, fs, d), lambda i: (0, layer, 0, 0), pipeline_mode=pl.Buffered(1))
    hid = pl.BlockSpec((N_CHIPS, tm, fs), lambda i: (0, i, 0))
    return _tc_call(
        body, name=f"bwd_ffn_dx{layer}", nt=nt,
        in_specs=[_rows(tm, d), hid, hid, wcol, wcol, wrow, _rows(tm, d), _const((1, d))],
        out_specs=[hid, hid, _rows(tm, d)] + [_acc_out((1, d))] * 3,
        out_shape=[jax.ShapeDtypeStruct((N_CHIPS, t, fs), BF16)] * 2 + [jax.ShapeDtypeStruct((t, d), F32)]
        + [jax.ShapeDtypeStruct((1, d), F32)] * 3,
        scratch_shapes=[pltpu.VMEM((8, d), F32)] * 3,
        operands=(dz, act, bm, wg, wu, wd, z_in, g_in), riders=riders)


def _matmul_tn(a, b, tt, name, carry=None):
    ja, t, ka = a.shape
    jb, _, nb = b.shape
    nj = max(ja, jb)

    def body(a_ref, b_ref, *rest):
        o_ref = rest[-1] if carry is None else rest[1]

        @pl.when(pl.program_id(0) == 0)
        def _():
            o_ref[...] = jnp.zeros_like(o_ref)

        a0 = a_ref[0].astype(BF16) if ja == 1 else None
        b0 = b_ref[0].astype(BF16) if jb == 1 else None
        for j in range(nj):
            aj = a0 if ja == 1 else a_ref[j].astype(BF16)
            bj = b0 if jb == 1 else b_ref[j].astype(BF16)
            o_ref[j] += _dot_tn(aj, bj)
        if carry is not None:
            rest[2][...] = rest[0][...]

    in_specs = [pl.BlockSpec((ja, tt, ka), lambda i: (0, i, 0)), pl.BlockSpec((jb, tt, nb), lambda i: (0, i, 0))]
    out_specs = [pl.BlockSpec((nj, ka, nb), lambda i: (0, 0, 0))]
    out_shape = [jax.ShapeDtypeStruct((nj, ka, nb), F32)]
    operands = [a, b]
    if carry is not None:
        in_specs.append(_rows(tt, carry.shape[1]))
        out_specs.append(_rows(tt, carry.shape[1]))
        out_shape.append(jax.ShapeDtypeStruct(carry.shape, carry.dtype))
        operands.append(carry)
    res = pl.pallas_call(body, name=name, grid=(t // tt,), in_specs=in_specs, out_specs=out_specs,
                         out_shape=out_shape, compiler_params=_cparams())(*operands)
    return res[0] if carry is None else (res[0], res[1])


def _matmul_nn(at, b, tt, name):
    ka, t = at.shape
    nb = b.shape[1]

    def body(a_ref, b_ref, o_ref):
        @pl.when(pl.program_id(0) == 0)
        def _():
            o_ref[...] = jnp.zeros_like(o_ref)

        o_ref[...] += _dot(a_ref[...].astype(BF16), b_ref[...].astype(BF16))

    return pl.pallas_call(
        body, name=name, grid=(t // tt,),
        in_specs=[pl.BlockSpec((ka, tt), lambda i: (0, i)), pl.BlockSpec((tt, nb), lambda i: (i, 0))],
        out_specs=pl.BlockSpec((ka, nb), lambda i: (0, 0)), out_shape=jax.ShapeDtypeStruct((ka, nb), F32),
        compiler_params=_cparams(1),
    )(at, b)


def _bwd_attn(dz_all, qt, kv, sinks, wo, wqt, wkv, z_in, g_in, tm, riders=()):
    t, d = dz_all.shape
    nq = d // HEAD_DIM
    group = nq // N_KV_HEADS
    nb = tm // BLOCK
    nt = t // tm
    hb = tm // BLOCK
    n_kv = 2 * N_KV_HEADS

    def body(sink_ref, dz_ref, qt_ref, kv_ref, kvh_ref, wo_ref, wqt_ref, wkv_ref, zin_ref, gin_ref,
             dqt_ref, dkv_ref, dx_ref, dbq_ref, dbkv_ref, dsink_ref, ding_ref, dinb_ref,
             kv_scr, dkv_scr, qt_scr, dot_scr, dqt_scr, carry, acc_q, acc_kv, acc_s, acc_ig, acc_ib):
        i = pl.program_id(0)
        ti = nt - 1 - i
        _acc_init(i, carry, acc_q, acc_kv, acc_s, acc_ig, acc_ib)
        dz = dz_ref[...]
        do_t = _dot_nt(wo_ref[...], dz.astype(BF16)).astype(BF16)
        for b in range(nb):
            dot_scr[b] = do_t[:, b * BLOCK:(b + 1) * BLOCK]
            qt_scr[b] = qt_ref[:, b * BLOCK:(b + 1) * BLOCK]
        _fill_kv(kv_scr, kvh_ref[...], kv_ref[...], tm)
        dkv_scr[:, 0:tm] = jnp.zeros((n_kv, tm, HEAD_DIM), F32)
        dkv_scr[:, tm:tm + BLOCK] = carry[...]
        band = _attn_band()

        def block(b, c):
            r0 = pl.multiple_of(b * BLOCK, BLOCK)
            has_previous = jnp.logical_or(ti > 0, b > 0)
            for g in range(N_KV_HEADS):
                kk = kv_scr[g, pl.ds(r0, 2 * BLOCK), :]
                vv = kv_scr[N_KV_HEADS + g, pl.ds(r0, 2 * BLOCK), :]
                q_all = _heads_on_lanes(qt_scr, b, g, group)
                do_all = _heads_on_lanes(dot_scr, b, g, group)
                s_all = _dot(kk, q_all)
                dp_all = _dot(vv, do_all)
                probs, dscores = [], []
                for hh in range(group):
                    h = g * group + hh
                    cols = slice(hh * BLOCK, (hh + 1) * BLOCK)
                    p, p_sink = _softmax_with_sink(s_all[:, cols], _slope(h, nq), band, has_previous, sink_ref[h])
                    dp = _fold(dp_all[:, cols], band[0])
                    rs = jnp.sum(p * dp, axis=0, keepdims=True)
                    acc_s[h:h + 1, :] += -(p_sink * rs)
                    ds = p * (dp - rs) * (1.0 / math.sqrt(HEAD_DIM))
                    probs.append(_unfold(p.astype(BF16), band[0]))
                    dscores.append(_unfold(ds.astype(BF16), band[0]))
                p_all = jnp.concatenate(probs, axis=1)
                ds_all = jnp.concatenate(dscores, axis=1)
                dq_all = _dot_tn(kk, ds_all)
                for hh in range(group):
                    h = g * group + hh
                    dqt_scr[b, h * HEAD_DIM:(h + 1) * HEAD_DIM, :] = dq_all[:, hh * BLOCK:(hh + 1) * BLOCK]
                dkv_scr[g, pl.ds(r0, 2 * BLOCK), :] += _dot_nt(ds_all, q_all)
                dkv_scr[N_KV_HEADS + g, pl.ds(r0, 2 * BLOCK), :] += _dot_nt(p_all, do_all)
            return c

        lax.fori_loop(0, nb, block, 0, unroll=True)
        carry[...] = dkv_scr[:, 0:BLOCK]
        dkv = jnp.concatenate([dkv_scr[j, BLOCK:BLOCK + tm] for j in range(n_kv)], axis=1)
        acc_kv[...] += _colsum8(dkv)
        dkvb = dkv.astype(BF16)
        dkv_ref[...] = dkvb
        dqt = jnp.concatenate([dqt_scr[b] for b in range(nb)], axis=1)
        for b in range(nb):
            acc_q[...] += dqt_scr[b]
        dqtb = dqt.astype(BF16)
        dqt_ref[...] = dqtb
        dx = ALPHA * dz + _dot_tn(dqtb, wqt_ref[...]) + _dot_nt(dkvb, wkv_ref[...])
        dz_in, zhat_in = _ln_bwd(dx, zin_ref[...], gin_ref[...])
        acc_ig[...] += _colsum8(dx * zhat_in)
        acc_ib[...] += _colsum8(dx)
        dx_ref[...] = dz_in
        _write_sums(i, nt, [(dbkv_ref, acc_kv), (ding_ref, acc_ig), (dinb_ref, acc_ib)])

        @pl.when(i == nt - 1)
        def _():
            dbq_ref[...] = jnp.sum(acc_q[...], axis=1, keepdims=True)
            dsink_ref[...] = jnp.sum(acc_s[...], axis=1, keepdims=True)

    rev = lambda w: pl.BlockSpec((tm, w), lambda i: (nt - 1 - i, 0))
    rev_cols = pl.BlockSpec((d, tm), lambda i: (0, nt - 1 - i))
    vec = _const((1, d))
    return _tc_call(
        body, name="bwd_attn", nt=nt,
        in_specs=[pl.BlockSpec(memory_space=pltpu.SMEM), rev(d), rev_cols, rev(2 * KVD),
                  pl.BlockSpec((BLOCK, 2 * KVD), lambda i: (jnp.maximum((nt - 1 - i) * hb - 1, 0), 0)),
                  _const((d, d)), _const((d, d)), _const((d, 2 * KVD)), rev(d), vec],
        out_specs=[rev_cols, rev(2 * KVD), rev(d), _acc_out((d, 1)), _acc_out((1, 2 * KVD)), _acc_out((nq, 1)),
                   _acc_out((1, d)), _acc_out((1, d))],
        out_shape=[jax.ShapeDtypeStruct((d, t), BF16), jax.ShapeDtypeStruct((t, 2 * KVD), BF16),
                   jax.ShapeDtypeStruct((t, d), F32), jax.ShapeDtypeStruct((d, 1), F32),
                   jax.ShapeDtypeStruct((1, 2 * KVD), F32), jax.ShapeDtypeStruct((nq, 1), F32)]
        + [jax.ShapeDtypeStruct((1, d), F32)] * 2,
        scratch_shapes=[pltpu.VMEM((n_kv, tm + BLOCK, HEAD_DIM), BF16), pltpu.VMEM((n_kv, tm + BLOCK, HEAD_DIM), F32),
                        pltpu.VMEM((nb, d, BLOCK), BF16), pltpu.VMEM((nb, d, BLOCK), BF16),
                        pltpu.VMEM((nb, d, BLOCK), F32), pltpu.VMEM((n_kv, BLOCK, HEAD_DIM), F32),
                        pltpu.VMEM((d, BLOCK), F32), pltpu.VMEM((8, 2 * KVD), F32), pltpu.VMEM((nq, BLOCK), F32),
                        pltpu.VMEM((8, d), F32), pltpu.VMEM((8, d), F32)],
        operands=(sinks, dz_all, qt, kv, kv, wo, wqt, wkv, z_in, g_in), riders=riders)


def _bwd_conv_head(dz, c, w2, lng, lnb, tm, riders=()):
    t, d = dz.shape
    nt = t // tm

    def body(dz_ref, c_ref, w2_ref, lg_ref, lb_ref, s_ref, dc_ref, dlg_ref, dlb_ref, a3, a4):
        i = pl.program_id(0)
        _acc_init(i, a3, a4)
        dz = dz_ref[...]
        chat, rstd = _ln_stats(c_ref[...])
        n = chat * lg_ref[...] + lb_ref[...]
        act, dact = _silu_and_grad(n)
        s_ref[...] = act.astype(BF16)
        dn = _dot_nt(dz.astype(BF16), w2_ref[...]) * dact
        a3[...] += _colsum8(dn * chat)
        a4[...] += _colsum8(dn)
        dch = dn * lg_ref[...]
        m1 = jnp.mean(dch, axis=-1, keepdims=True)
        m2 = jnp.mean(dch * chat, axis=-1, keepdims=True)
        dc_ref[...] = rstd * (dch - m1 - chat * m2)
        _write_sums(i, nt, [(dlg_ref, a3), (dlb_ref, a4)])

    vec = _const((1, d))
    return _tc_call(
        body, name="bwd_conv_head", nt=nt,
        in_specs=[_rows(tm, d), _rows(tm, d), _const((d, d)), vec, vec],
        out_specs=[_rows(tm, d), _rows(tm, d)] + [_acc_out((1, d))] * 2,
        out_shape=[jax.ShapeDtypeStruct((t, d), BF16), jax.ShapeDtypeStruct((t, d), F32)]
        + [jax.ShapeDtypeStruct((1, d), F32)] * 2,
        scratch_shapes=[pltpu.VMEM((8, d), F32)] * 2, operands=(dz, c, w2, lng, lnb), riders=riders)


def _bwd_conv_glu(dc, u, a, g, dz, wdw, w1s, tm, riders=()):
    t, d = dc.shape
    dh_w = d // 2
    nt = t // tm
    hb = tm // CONV_HALO
    last_halo = t // CONV_HALO - 1

    def body(dc_ref, dcn_ref, u_ref, a_ref, g_ref, dz_ref, w_ref, w1_ref,
             dx_ref, dh_ref, db1_ref, dbdw_ref, dw_ref, ext, sh, du_scr, acc_b1, acc_bdw, acc_w):
        i = pl.program_id(0)
        _acc_init(i, acc_b1, acc_bdw, acc_w)
        dcv = dc_ref[...]
        acc_bdw[...] += _colsum8(dcv)

        ext[0:tm] = dcv
        ext[tm:tm + CONV_HALO] = jnp.where(i == nt - 1, 0.0, dcn_ref[...])
        ext[tm + CONV_HALO:tm + CONV_HALO + 8] = jnp.zeros((8, d), F32)
        _fill_shifted(sh, ext)

        def du_chunk(r, carry):
            base = pl.multiple_of(r * CONV_CHUNK, CONV_CHUNK)
            _tap_sum(w_ref, ext, sh, base, d, lambda k: CONV_WIDTH - 1 - k, du_scr)
            return carry

        lax.fori_loop(0, tm // CONV_CHUNK, du_chunk, 0)

        def dw_chunk(r, carry):
            base = pl.multiple_of(r * CONV_CHUNK, CONV_CHUNK)
            groups = CONV_CHUNK // 8
            for lg in range(d // LANES):
                ls = slice(lg * LANES, (lg + 1) * LANES)
                uv = u_ref[pl.ds(base, CONV_CHUNK), ls].reshape(groups, 8, LANES)
                for k in range(CONV_WIDTH):
                    x = _ext_rows(ext, sh, CONV_WIDTH - 1 - k, base, CONV_CHUNK, ls).reshape(groups, 8, LANES)
                    acc_w[k, :, ls] += jnp.sum(uv * x, axis=0)
            return carry

        lax.fori_loop(0, tm // CONV_CHUNK, dw_chunk, 0)

        du = du_scr[...]
        av = a_ref[...].astype(F32)
        sg = jax.nn.sigmoid(g_ref[...].astype(F32))
        da = du * sg
        dg = du * av * sg * (1.0 - sg)
        acc_b1[:, 0:d] += _colsum8(da)
        acc_b1[:, d:2 * d] += _colsum8(dg)
        dx = ALPHA * dz_ref[...]
        for j, part in enumerate([da[:, 0:dh_w], da[:, dh_w:d], dg[:, 0:dh_w], dg[:, dh_w:d]]):
            pb = part.astype(BF16)
            dh_ref[j] = pb
            dx = dx + _dot_nt(pb, w1_ref[j])
        dx_ref[...] = dx

        @pl.when(i == nt - 1)
        def _():
            db1_ref[...] = jnp.sum(acc_b1[...], axis=0, keepdims=True)
            dbdw_ref[...] = jnp.sum(acc_bdw[...], axis=0, keepdims=True)
            dw_ref[...] = jnp.sum(acc_w[...], axis=1)

    return _tc_call(
        body, name="bwd_conv_glu", nt=nt,
        in_specs=[_rows(tm, d), pl.BlockSpec((CONV_HALO, d), lambda i: (jnp.minimum((i + 1) * hb, last_halo), 0)),
                  _rows(tm, d), _rows(tm, d), _rows(tm, d), _rows(tm, d), _const((CONV_HALO, 8, d)),
                  _const((4, d, dh_w))],
        out_specs=[_rows(tm, d), pl.BlockSpec((4, tm, dh_w), lambda i: (0, i, 0)), _acc_out((1, 2 * d)),
                   _acc_out((1, d)), _acc_out((CONV_HALO, d))],
        out_shape=[jax.ShapeDtypeStruct((t, d), F32), jax.ShapeDtypeStruct((4, t, dh_w), BF16),
                   jax.ShapeDtypeStruct((1, 2 * d), F32), jax.ShapeDtypeStruct((1, d), F32),
                   jax.ShapeDtypeStruct((CONV_HALO, d), F32)],
        scratch_shapes=[pltpu.VMEM((tm + CONV_HALO + 8, d), F32), pltpu.VMEM((7, tm + CONV_HALO, d), F32),
                        pltpu.VMEM((tm, d), F32), pltpu.VMEM((8, 2 * d), F32), pltpu.VMEM((8, d), F32),
                        pltpu.VMEM((CONV_HALO, 8, d), F32)],
        operands=(dc, dc, u, a, g, dz, wdw, w1s), riders=riders)


def _adamw_update(w_ref, g_ref, m_ref, v_ref, d_ref, nm_ref, nv_ref):
    gv = g_ref[...]
    nm = ADAM_B1 * m_ref[...] + (1.0 - ADAM_B1) * gv
    nv = ADAM_B2 * v_ref[...] + (1.0 - ADAM_B2) * (gv * gv)
    m_hat = nm / (1.0 - ADAM_B1 ** ADAM_STEP)
    v_hat = nv / (1.0 - ADAM_B2 ** ADAM_STEP)
    d_ref[...] = -ADAM_LR * (m_hat / (jnp.sqrt(v_hat) + ADAM_EPS) + ADAM_WD * w_ref[...])
    nm_ref[...] = nm
    nv_ref[...] = nv


ADAMW_STEPS = 8


def _adamw(params, name, riders=()):
    n = len(params)
    steps = ADAMW_STEPS if all(p[0].shape[0] % (8 * ADAMW_STEPS) == 0 for p in params) else 1

    def body(*refs):
        for k in range(n):
            _adamw_update(*refs[4 * k:4 * k + 4], *refs[4 * n + 3 * k:4 * n + 3 * k + 3])

    specs = [pl.BlockSpec((p[0].shape[0] // steps, p[0].shape[1]), lambda i: (i, 0)) for p in params]
    outs, rider_outs = _tc_call(
        body, name=name, nt=steps, in_specs=[s for s in specs for _ in range(4)],
        out_specs=[s for s in specs for _ in range(3)],
        out_shape=[jax.ShapeDtypeStruct(p[0].shape, F32) for p in params for _ in range(3)],
        operands=[a for p in params for a in p], riders=riders)
    return [tuple(outs[3 * k:3 * k + 3]) for k in range(n)], rider_outs


def _pad_to(v, n):
    return jnp.pad(v, (0, n - v.shape[0]))


def _round_up(n, m):
    return (n + m - 1) // m * m


def kernel(x, conv_w_pw1, conv_b_pw1, conv_w_dw, conv_b_dw, conv_ln_g, conv_ln_b, conv_w_pw2, conv_b_pw2, kv_w_k, kv_b_k, kv_w_v, kv_b_v, attn_w_q, attn_b_q, attn_sinks, attn_w_o, attn_b_o, ffn_w_gate, ffn_w_up, ffn_w_down, ln_mix_g, ln_mix_b, ln_ffn_g, ln_ffn_b, loss_target, m_conv_w_pw1, m_conv_b_pw1, m_conv_w_dw, m_conv_b_dw, m_conv_ln_g, m_conv_ln_b, m_conv_w_pw2, m_conv_b_pw2, m_kv_w_k, m_kv_b_k, m_kv_w_v, m_kv_b_v, m_attn_w_q, m_attn_b_q, m_attn_sinks, m_attn_w_o, m_attn_b_o, m_ffn_w_gate, m_ffn_w_up, m_ffn_w_down, m_ln_mix_g, m_ln_mix_b, m_ln_ffn_g, m_ln_ffn_b, v_conv_w_pw1, v_conv_b_pw1, v_conv_w_dw, v_conv_b_dw, v_conv_ln_g, v_conv_ln_b, v_conv_w_pw2, v_conv_b_pw2, v_kv_w_k, v_kv_b_k, v_kv_w_v, v_kv_b_v, v_attn_w_q, v_attn_b_q, v_attn_sinks, v_attn_w_o, v_attn_b_o, v_ffn_w_gate, v_ffn_w_up, v_ffn_w_down, v_ln_mix_g, v_ln_mix_b, v_ln_ffn_g, v_ln_ffn_b):
    args = dict(locals())
    w = {n: args[n] for n in WEIGHTS}
    mom = {n: args["m_" + n] for n in WEIGHTS}
    var = {n: args["v_" + n] for n in WEIGHTS}
    assert x.shape[0] == 1, "one sequence per device"
    t, d = x.shape[1], x.shape[2]
    dq = d // 4
    fs = ffn_w_gate.shape[-1]
    nq = d // HEAD_DIM
    x0 = x.reshape(t, d)
    target = loss_target.reshape(t, d)
    tm_big = min(512, t)
    tm_tn = min(1024, t)
    c_idx = lax.axis_index("c")

    me_idx = 2 * lax.axis_index("x") + lax.axis_index("y")

    def gather_buffer(v):
        buf = lax.empty((N_CHIPS,) + v.shape, v.dtype)
        return lax.dynamic_update_slice(buf, v[None], (me_idx,) + (0,) * v.ndim)

    def halves(v):
        return v.reshape(2, -1, v.shape[-1])

    small_sizes = [int(w[n].size) for n in SMALL_SHARDED]
    rs = _round_up(sum(small_sizes), 8 * 128) // 128
    spack = _pad_to(jnp.concatenate([w[n].reshape(-1) for n in SMALL_SHARDED]), rs * 128).reshape(rs, 128)
    conv_first = ['conv_w_pw1', 'conv_w_pw2']
    later = [n for n in BIG if n not in conv_first]
    (first_out,) = _run_riders(
        [_all_gather_rider([gather_buffer(halves(w[n].astype(BF16))) for n in conv_first], gather_buffer(spack))],
        "all_gather_conv")
    later_rider = _all_gather_rider([gather_buffer(halves(w[n].astype(BF16))) for n in later])
    gs = first_out[-1].reshape(N_CHIPS, rs * 128)
    full = {n: g.reshape((N_CHIPS,) + w[n].shape) for n, g in zip(conv_first, first_out)}
    off = 0
    for n, size in zip(SMALL_SHARDED, small_sizes):
        full[n] = gs[:, off:off + size].reshape((N_CHIPS,) + w[n].shape)
        off += size
    w1s = full['conv_w_pw1'].reshape(N_CHIPS, d, d // 2)
    w2 = full['conv_w_pw2'].reshape(d, d)
    b1 = full['conv_b_pw1'].reshape(1, 2 * d)
    wdw = jnp.pad(full['conv_w_dw'].reshape(N_CHIPS, CONV_WIDTH, dq).transpose(1, 0, 2).reshape(CONV_WIDTH, d),
                  ((0, CONV_HALO - CONV_WIDTH), (0, 0)))
    wdw = jnp.broadcast_to(wdw[:, None, :], (CONV_HALO, 8, d))
    bdw = full['conv_b_dw'].reshape(1, d)
    clng = full['conv_ln_g'].reshape(1, d)
    clnb = full['conv_ln_b'].reshape(1, d)
    b2 = full['conv_b_pw2'].reshape(1, d)
    bkv = jnp.concatenate([kv_b_k, kv_b_v]).reshape(1, 2 * KVD)
    sinks = attn_sinks.reshape(nq)
    mixg = [ln_mix_g[l].reshape(1, d) for l in range(DEPTH)]
    mixb = [ln_mix_b[l].reshape(1, d) for l in range(DEPTH)]
    ffng = [ln_ffn_g[l].reshape(1, d) for l in range(DEPTH)]
    ffnb = [ln_ffn_b[l].reshape(1, d) for l in range(DEPTH)]

    a_act, g_act, u_act = _fwd_pw1_glu(x0, w1s, b1, tm_big)
    (c_act, z1, x1), (later_out,) = _fwd_conv_tail(u_act, x0, wdw, bdw, clng, clnb, w2, b2, mixg[0], mixb[0], tm_big,
                                                   riders=[later_rider])
    full.update({n: g.reshape((N_CHIPS,) + w[n].shape) for n, g in zip(later, later_out)})
    wkv = jnp.concatenate([full['kv_w_k'].reshape(d, KVD), full['kv_w_v'].reshape(d, KVD)], axis=1)
    wqt = full['attn_w_q'].reshape(d, d).T
    wo = full['attn_w_o'].reshape(d, d)
    wg, wu, wd = full['ffn_w_gate'], full['ffn_w_up'], full['ffn_w_down']
    act0, bm0, hm0, z2, x2 = _fwd_ffn(x1, wg, wu, wd, 0, ffng[0], ffnb[0], tm_big)
    qt_act, kv_act, ot_act, z3, x3 = _fwd_attn(x2, wqt, attn_b_q.reshape(d, 1), wkv, bkv, sinks, wo, attn_b_o,
                                               mixg[1], mixb[1], tm_big)
    act1, bm1, hm1, dz4, d_fg1, d_fb1, loss_part = _fwd_ffn(x3, wg, wu, wd, 1, ffng[1], ffnb[1], tm_big, target=target)
    loss = lax.psum(loss_part[0, 0], ("x", "y", "c"))

    c_arr = c_idx.reshape(1).astype(jnp.int32)

    def halves4(v):
        return v.reshape(N_CHIPS, 2, -1, v.shape[-1])

    def arrays(group):
        return [p for _, p in group]

    def pair_sums(group, got):
        return _pair_sums(arrays(group), got, c_arr, "grad_pair_sum_" + group[0][0])

    pos_arr = jnp.stack([me_idx, c_idx]).astype(jnp.int32)

    def chip_sums(group, sums, got):
        return _chip_sums(sums, got, pos_arr, "grad_chip_sum_" + group[0][0])

    (dgg1, duu1, dz3, d_mg1, d_mb1, d_bo), _ = _bwd_ffn_dx(dz4, act1, bm1, wg, wu, wd, 1, z3, mixg[1], tm_big)
    g1 = [("ffn_w_gate1", halves4(_matmul_tn(x3[None], dgg1, tm_tn, "dw_gate1"))),
          ("ffn_w_up1", halves4(_matmul_tn(x3[None], duu1, tm_tn, "dw_up1"))),
          ("ffn_w_down1", halves4(_matmul_tn(hm1, dz4[None], tm_tn, "dw_down1")))]
    (dqt, dkv, dz2, d_bq, d_bkv, d_sinks, d_fg0, d_fb0), (got1,) = _bwd_attn(
        dz3, qt_act, kv_act, sinks, wo, wqt, wkv, z2, ffng[0], tm_big, riders=[_pair_exchange_rider(arrays(g1))])
    s1 = pair_sums(g1, got1)
    dwo = _matmul_nn(ot_act, dz3, tm_tn, "dw_o")
    dwq = _matmul_nn(dqt, x2, tm_tn, "dw_q").T
    dwkv = _matmul_tn(x2[None], dkv[None], tm_tn, "dw_kv")[0]
    g2 = [("attn_w_o", halves4(dwo)), ("attn_w_q", halves4(dwq)),
          ("kv_w_k", halves4(dwkv[:, 0:KVD])), ("kv_w_v", halves4(dwkv[:, KVD:2 * KVD]))]
    (dgg0, duu0, dz1, d_mg0, d_mb0, d_b2), (from_chips1, got2) = _bwd_ffn_dx(
        dz2, act0, bm0, wg, wu, wd, 0, z1, mixg[0], tm_big,
        riders=[_chip_scatter_rider(s1), _pair_exchange_rider(arrays(g2))])
    f1 = chip_sums(g1, s1, from_chips1)
    s2 = pair_sums(g2, got2)
    g3 = [("ffn_w_gate0", halves4(_matmul_tn(x1[None], dgg0, tm_tn, "dw_gate0"))),
          ("ffn_w_up0", halves4(_matmul_tn(x1[None], duu0, tm_tn, "dw_up0"))),
          ("ffn_w_down0", halves4(_matmul_tn(hm0, dz2[None], tm_tn, "dw_down0")))]
    (s_act, dc, d_clng, d_clnb), (got3, shared1) = _bwd_conv_head(
        dz1, c_act, w2, clng, clnb, tm_big, riders=[_pair_exchange_rider(arrays(g3)), _pair_share_rider(f1)])
    s3 = pair_sums(g3, got3)
    dw2 = _matmul_tn(s_act[None], dz1[None], tm_tn, "dw_pw2")
    (dx0, dh1, d_b1, d_bdw, d_wdw), (from_chips2, from_chips3) = _bwd_conv_glu(
        dc, u_act, a_act, g_act, dz1, wdw, w1s, tm_big, riders=[_chip_scatter_rider(s2), _chip_scatter_rider(s3)])
    f2 = chip_sums(g2, s2, from_chips2)
    f3 = chip_sums(g3, s3, from_chips3)
    dw1, grad_x = _matmul_tn(x0[None], dh1, tm_tn, "dw_pw1", carry=dx0)

    def rows4(v):
        return v.reshape(N_CHIPS, -1)

    def rep4(v):
        return jnp.broadcast_to(v.reshape(1, -1), (N_CHIPS, v.size))

    local = {
        'conv_b_pw1': rows4(d_b1),
        'conv_w_dw': rows4(d_wdw[0:CONV_WIDTH].reshape(CONV_WIDTH, N_CHIPS, dq).transpose(1, 0, 2)),
        'conv_b_dw': rows4(d_bdw), 'conv_ln_g': rows4(d_clng), 'conv_ln_b': rows4(d_clnb), 'conv_b_pw2': rows4(d_b2),
        'kv_b_k': rep4(d_bkv[:, 0:KVD]), 'kv_b_v': rep4(d_bkv[:, KVD:2 * KVD]), 'attn_b_q': rep4(d_bq),
        'attn_sinks': rep4(d_sinks), 'attn_b_o': rep4(d_bo),
        'ln_mix_g': rep4(jnp.concatenate([d_mg0, d_mg1])), 'ln_mix_b': rep4(jnp.concatenate([d_mb0, d_mb1])),
        'ln_ffn_g': rep4(jnp.concatenate([d_fg0, d_fg1])), 'ln_ffn_b': rep4(jnp.concatenate([d_fb0, d_fb1])),
    }
    n_small = sum(int(w[n].size) for n in SMALL)
    small_rows = _round_up(n_small, 2 * SUM_STEPS * 8 * 128) // 128
    small_local = jnp.concatenate([local[n] for n in SMALL], axis=1)
    small_local = jnp.pad(small_local, ((0, 0), (0, small_rows * 128 - n_small)))
    g4 = [("conv_w_pw1", halves4(dw1)), ("conv_w_pw2", halves4(dw2)),
          ("small", small_local.reshape(N_CHIPS, 2, small_rows // 2, 128))]
    got4, shared23 = _run_riders([_pair_exchange_rider(arrays(g4)), _pair_share_rider(f2 + f3)],
                                 "grad_pair_exchange_last")
    s4 = pair_sums(g4, got4)
    reduced = dict(zip([n for n, _ in g1], shared1))
    reduced.update(zip([n for n, _ in g2 + g3], shared23))
    for n in ('ffn_w_gate', 'ffn_w_up', 'ffn_w_down'):
        reduced[n] = jnp.stack([reduced[n + str(layer)].reshape(w[n].shape[1:]) for layer in range(DEPTH)])

    g_out, delta, new_m, new_v = {}, {}, {}, {}

    def adamw_matrices(names, name, riders=()):
        for n in names:
            g_out[n] = reduced[n].reshape(w[n].shape)
        two_d = [tuple(tree[n].reshape(-1, w[n].shape[-1]) for tree in (w, g_out, mom, var)) for n in names]
        results, rider_outs = _adamw(two_d, name, riders)
        for n, (dl, nm, nv) in zip(names, results):
            delta[n], new_m[n], new_v[n] = (r.reshape(w[n].shape) for r in (dl, nm, nv))
        return rider_outs

    (from_chips4,) = adamw_matrices([n for n in BIG if n not in conv_first], "adamw_attn_ffn",
                                    riders=[_chip_scatter_rider(s4)])
    f4 = chip_sums(g4, s4, from_chips4)
    (shared4,) = _run_riders([_pair_share_rider(f4)], "grad_pair_share_last")
    reduced.update(zip([n for n, _ in g4], shared4))
    adamw_matrices(conv_first, "adamw_conv")

    def pack_small(tree):
        return _pad_to(jnp.concatenate([tree[n].reshape(-1) for n in SMALL]), small_rows * 128).reshape(small_rows, 128)

    g_small = reduced['small'].reshape(small_rows, 128)
    ((dl, nm, nv),), _ = _adamw([(pack_small(w), g_small, pack_small(mom), pack_small(var))], "adamw_small")
    off = 0
    for n in SMALL:
        size, shape = int(w[n].size), w[n].shape
        for tree, flat in ((g_out, g_small), (delta, dl), (new_m, nm), (new_v, nv)):
            tree[n] = flat.reshape(-1)[off:off + size].reshape(shape)
        off += size

    return (loss, grad_x.reshape(x.shape), *[g_out[n] for n in WEIGHTS], *[delta[n] for n in WEIGHTS],
            *[new_m[n] for n in WEIGHTS], *[new_v[n] for n in WEIGHTS])
```

```python
import math

import jax
import jax.numpy as jnp
from jax import lax
from jax.experimental import pallas as pl
from jax.experimental.pallas import tpu as pltpu

F32 = jnp.float32
BF16 = jnp.bfloat16

DEPTH = 2
ALPHA = (2.0 * DEPTH) ** 0.25
LN_EPS = 1e-5
NEG_INF = -1e30
HEAD_DIM = 64
N_KV_HEADS = 2
KVD = N_KV_HEADS * HEAD_DIM
BLOCK = 128
CONV_WIDTH = 31
CONV_HALO = 32
ALIBI_MAX = 8.0
ADAM_LR, ADAM_B1, ADAM_B2, ADAM_EPS, ADAM_WD, ADAM_STEP = 0.001, 0.9, 0.999, 1e-08, 0.01, 10

LN_CHUNK = 64
N_CHIPS = 4
SUM_STEPS = 4
VMEM_LIMIT = 60 * 1024 * 1024
MESH = pl.DeviceIdType.MESH

NT_DIMS = (((1,), (1,)), ((), ()))
TN_DIMS = (((0,), (0,)), ((), ()))

WEIGHTS = ['conv_w_pw1', 'conv_b_pw1', 'conv_w_dw', 'conv_b_dw', 'conv_ln_g', 'conv_ln_b', 'conv_w_pw2', 'conv_b_pw2',
           'kv_w_k', 'kv_b_k', 'kv_w_v', 'kv_b_v', 'attn_w_q', 'attn_b_q', 'attn_sinks', 'attn_w_o', 'attn_b_o',
           'ffn_w_gate', 'ffn_w_up', 'ffn_w_down', 'ln_mix_g', 'ln_mix_b', 'ln_ffn_g', 'ln_ffn_b']
BIG = ['conv_w_pw1', 'conv_w_pw2', 'kv_w_k', 'kv_w_v', 'attn_w_q', 'attn_w_o', 'ffn_w_gate', 'ffn_w_up', 'ffn_w_down']
SMALL_SHARDED = ['conv_b_pw1', 'conv_w_dw', 'conv_b_dw', 'conv_ln_g', 'conv_ln_b', 'conv_b_pw2']
REPLICATED = ['kv_b_k', 'kv_b_v', 'attn_b_q', 'attn_sinks', 'attn_b_o', 'ln_mix_g', 'ln_mix_b', 'ln_ffn_g', 'ln_ffn_b']
SMALL = SMALL_SHARDED + REPLICATED


def _cparams(n_grid=1):
    return pltpu.CompilerParams(dimension_semantics=("arbitrary",) * n_grid, vmem_limit_bytes=VMEM_LIMIT)


def _rows(tm, width):
    return pl.BlockSpec((tm, width), lambda i: (i, 0))


def _const(shape):
    return pl.BlockSpec(shape, lambda *_: (0,) * len(shape), pipeline_mode=pl.Buffered(1))


def _acc_out(shape):
    return pl.BlockSpec(shape, lambda *_: (0,) * len(shape))


def _dot(a, b):
    return jnp.dot(a, b, preferred_element_type=F32)


def _dot_nt(a, b):
    return lax.dot_general(a, b, NT_DIMS, preferred_element_type=F32)


def _dot_tn(a, b):
    return lax.dot_general(a, b, TN_DIMS, preferred_element_type=F32)


def _colsum8(v):
    m, n = v.shape
    return jnp.sum(v.reshape(m // 8, 8, n), axis=0)


def _ln_stats(z):
    mu = jnp.mean(z, axis=-1, keepdims=True)
    zc = z - mu
    var = jnp.mean(zc * zc, axis=-1, keepdims=True)
    rstd = lax.rsqrt(var + LN_EPS)
    return zc * rstd, rstd


def _ln_fwd(z, g, b):
    zhat, _ = _ln_stats(z)
    return zhat * g + b


def _ln_bwd(dy, z, g):
    zhat, rstd = _ln_stats(z)
    dzh = dy * g
    m1 = jnp.mean(dzh, axis=-1, keepdims=True)
    m2 = jnp.mean(dzh * zhat, axis=-1, keepdims=True)
    return rstd * (dzh - m1 - zhat * m2), zhat


def _silu_and_grad(n):
    sg = jax.nn.sigmoid(n)
    return n * sg, sg * (1.0 + n * (1.0 - sg))


def _acc_init(i, *refs):
    @pl.when(i == 0)
    def _():
        for r in refs:
            r[...] = jnp.zeros_like(r)


def _mesh_pos():
    x, y, c = lax.axis_index("x"), lax.axis_index("y"), lax.axis_index("c")
    chips = [(1 - x, y), (x, 1 - y), (1 - x, 1 - y)]
    return x, y, c, chips


HBM_SPEC = pl.BlockSpec(memory_space=pltpu.HBM)


def _remote(src, dst, send_sems, recv_sems, k, to):
    return pltpu.make_async_remote_copy(src_ref=src, dst_ref=dst, send_sem=send_sems.at[k], recv_sem=recv_sems.at[k],
                                        device_id=to, device_id_type=MESH)


class _Rider:
    def __init__(self, operands, out_shapes, sem_shapes, start, finish, mid=None, in_place=False):
        self.operands, self.out_shapes, self.sem_shapes = list(operands), list(out_shapes), list(sem_shapes)
        self.start, self.finish, self.mid = start, finish, mid
        self.in_place = in_place


def _rider_aliases(riders, first_in, first_out):
    aliases, k_in, k_out = {}, first_in, first_out
    for r in riders:
        if r.in_place:
            aliases.update({k_in + k: k_out + k for k in range(len(r.operands))})
        k_in += len(r.operands)
        k_out += len(r.out_shapes)
    return aliases


def _split(refs, counts):
    parts, k = [], 0
    for n in counts:
        parts.append(refs[k:k + n])
        k += n
    return parts


def _rider_refs(riders, ins, outs, sems):
    return list(zip(riders, _split(ins, [len(r.operands) for r in riders]),
                    _split(outs, [len(r.out_shapes) for r in riders]),
                    _split(sems, [len(r.sem_shapes) for r in riders])))


def _tc_call(body, *, name, nt, in_specs, out_specs, out_shape, operands, scratch_shapes=(), riders=(), mid_frac=0.75):
    n_in, n_out, n_scr = len(in_specs), len(out_specs), len(scratch_shapes)
    r_ops = [o for r in riders for o in r.operands]
    r_outs = [o for r in riders for o in r.out_shapes]
    r_sems = [s for r in riders for s in r.sem_shapes]
    mid_step = min(max(int(nt * mid_frac), 0), nt - 1)

    def full(*refs):
        ins, r_in, outs, r_out, scr, r_sem = _split(refs, [n_in, len(r_ops), n_out, len(r_outs), n_scr, len(r_sems)])
        parts = _rider_refs(riders, r_in, r_out, r_sem)
        step = pl.program_id(0)

        @pl.when(step == 0)
        def _():
            for r, a, b, s in parts:
                r.start(a, b, s)

        body(*ins, *outs, *scr)

        @pl.when(step == mid_step)
        def _():
            for r, a, b, s in parts:
                if r.mid is not None:
                    r.mid(a, b, s)

        @pl.when(step == nt - 1)
        def _():
            for r, a, b, s in parts:
                r.finish(a, b, s)

    res = pl.pallas_call(
        full if riders else body, name=name, grid=(nt,), in_specs=list(in_specs) + [HBM_SPEC] * len(r_ops),
        out_specs=list(out_specs) + [HBM_SPEC] * len(r_outs), out_shape=list(out_shape) + r_outs,
        scratch_shapes=list(scratch_shapes) + r_sems, input_output_aliases=_rider_aliases(riders, n_in, n_out),
        compiler_params=_cparams(),
    )(*operands, *r_ops)
    return res[:n_out], _split(res[n_out:], [len(r.out_shapes) for r in riders])


def _run_riders(riders, name):
    r_ops = [o for r in riders for o in r.operands]
    r_outs = [o for r in riders for o in r.out_shapes]
    r_sems = [s for r in riders for s in r.sem_shapes]

    def body(*refs):
        r_in, r_out, r_sem = _split(refs, [len(r_ops), len(r_outs), len(r_sems)])
        parts = _rider_refs(riders, r_in, r_out, r_sem)
        for r, a, b, s in parts:
            r.start(a, b, s)
        for r, a, b, s in parts:
            if r.mid is not None:
                r.mid(a, b, s)
        for r, a, b, s in parts:
            r.finish(a, b, s)

    res = pl.pallas_call(body, name=name, out_shape=tuple(r_outs), in_specs=[HBM_SPEC] * len(r_ops),
                         out_specs=(HBM_SPEC,) * len(r_outs), scratch_shapes=r_sems,
                         input_output_aliases=_rider_aliases(riders, 0, 0))(*r_ops)
    return _split(list(res), [len(r.out_shapes) for r in riders])


def _all_gather_rider(bufs, small=None):
    n = len(bufs)
    n_small = 0 if small is None else 1

    def copies(outs, sems):
        send_sems, recv_sems = sems
        x, y, c, chips = _mesh_pos()
        me = 2 * x + y
        here, sibling = (x, y, c), (x, y, 1 - c)
        rows = [2 * cx + cy for cx, cy in chips]

        def big(p, k, chip_row, half, to):
            piece = outs[p].at[chip_row, half]
            return _remote(piece, piece, send_sems, recv_sems, 6 * p + k, to)

        first = [big(p, j, me, c, (cx, cy, c)) for p in range(n) for j, (cx, cy) in enumerate(chips)]
        landed = [big(p, j, rows[j], c, here) for p in range(n) for j in range(3)]
        passed = [big(p, 3 + j, rows[j], c, sibling) for p in range(n) for j in range(3)]
        arrivals = [big(p, 3 + j, rows[j], 1 - c, here) for p in range(n) for j in range(3)]
        if n_small:
            first = [_remote(outs[n].at[me], outs[n].at[me], send_sems, recv_sems, 6 * n + j, (cx, cy, c))
                     for j, (cx, cy) in enumerate(chips)] + first
            arrivals += [_remote(outs[n].at[rows[j]], outs[n].at[rows[j]], send_sems, recv_sems, 6 * n + j, here)
                         for j in range(3)]
        return first, landed, passed, arrivals

    def start(ins, outs, sems):
        for cp in copies(outs, sems)[0]:
            cp.start()

    def mid(ins, outs, sems):
        _, landed, passed, _ = copies(outs, sems)
        for got, fwd in zip(landed, passed):
            got.wait_recv()
            fwd.start()

    def finish(ins, outs, sems):
        first, _, passed, arrivals = copies(outs, sems)
        for cp in arrivals:
            cp.wait_recv()
        for cp in first + passed:
            cp.wait_send()

    operands = list(bufs) + ([small] if n_small else [])
    n_sem = 6 * n + 3 * n_small
    return _Rider(operands, [jax.ShapeDtypeStruct(o.shape, o.dtype) for o in operands],
                  [pltpu.SemaphoreType.DMA((n_sem,)), pltpu.SemaphoreType.DMA((n_sem,))], start, finish, mid,
                  in_place=True)


def _pair_exchange_rider(plist):
    n = len(plist)

    def copies(ins, outs, sems):
        x, y, c, _ = _mesh_pos()
        return [_remote(ins[k].at[:, 1 - c], outs[k], sems[0], sems[1], k, (x, y, 1 - c)) for k in range(n)]

    def start(ins, outs, sems):
        for cp in copies(ins, outs, sems):
            cp.start()

    def finish(ins, outs, sems):
        for cp in copies(ins, outs, sems):
            cp.wait()

    return _Rider(plist, [jax.ShapeDtypeStruct((p.shape[0],) + p.shape[2:], p.dtype) for p in plist],
                  [pltpu.SemaphoreType.DMA((n,)), pltpu.SemaphoreType.DMA((n,))], start, finish)


def _pair_sums(plist, gots, c, name):
    n = len(plist)

    def body(c_ref, *refs):
        for k in range(n):
            refs[2 * n + k][...] = refs[k][...] + refs[n + k][...]

    rows = [p.shape[2] // SUM_STEPS for p in plist]
    return pl.pallas_call(
        body, name=name, out_shape=[jax.ShapeDtypeStruct(g.shape, F32) for g in gots],
        grid_spec=pltpu.PrefetchScalarGridSpec(
            num_scalar_prefetch=1, grid=(N_CHIPS, SUM_STEPS),
            in_specs=[pl.BlockSpec((None, None, br, p.shape[3]), lambda j, i, c_ref: (j, c_ref[0], i, 0))
                      for p, br in zip(plist, rows)]
            + [pl.BlockSpec((None, br, p.shape[3]), lambda j, i, c_ref: (j, i, 0)) for p, br in zip(plist, rows)],
            out_specs=[pl.BlockSpec((None, br, p.shape[3]), lambda j, i, c_ref: (j, i, 0))
                       for p, br in zip(plist, rows)]),
        compiler_params=_cparams(2),
    )(c, *plist, *gots)


def _chip_scatter_rider(slist):
    n = len(slist)

    def copies(ins, outs, sems):
        send_sems, recv_sems = sems
        x, y, c, chips = _mesh_pos()
        sends = [_remote(ins[k].at[2 * cx + cy], outs[k].at[j], send_sems, recv_sems, 3 * k + j, (cx, cy, c))
                 for k in range(n) for j, (cx, cy) in enumerate(chips)]
        arrivals = [_remote(ins[k].at[0], outs[k].at[j], send_sems, recv_sems, 3 * k + j, (x, y, c))
                    for k in range(n) for j in range(3)]
        return sends, arrivals

    def start(ins, outs, sems):
        for cp in copies(ins, outs, sems)[0]:
            cp.start()

    def finish(ins, outs, sems):
        sends, arrivals = copies(ins, outs, sems)
        for cp in arrivals:
            cp.wait_recv()
        for cp in sends:
            cp.wait_send()

    return _Rider(slist, [jax.ShapeDtypeStruct((3,) + s.shape[1:], s.dtype) for s in slist],
                  [pltpu.SemaphoreType.DMA((3 * n,)), pltpu.SemaphoreType.DMA((3 * n,))], start, finish)


def _chip_sums(slist, gots, pos, name):
    n = len(slist)

    def body(pos_ref, *refs):
        me = pos_ref[0]
        for k in range(n):
            s_ref, got_ref, out_ref = refs[k], refs[n + k], refs[2 * n + k]
            total = None
            for chip in range(N_CHIPS):
                flip = jnp.bitwise_xor(me, chip)
                term = jnp.where(flip == 0, s_ref[...],
                                 jnp.where(flip == 2, got_ref[0], jnp.where(flip == 1, got_ref[1], got_ref[2])))
                total = term if total is None else total + term
            out_ref[...] = total

    rows = [s.shape[1] // SUM_STEPS for s in slist]
    return pl.pallas_call(
        body, name=name, out_shape=[jax.ShapeDtypeStruct((2,) + s.shape[1:], F32) for s in slist],
        grid_spec=pltpu.PrefetchScalarGridSpec(
            num_scalar_prefetch=1, grid=(SUM_STEPS,),
            in_specs=[pl.BlockSpec((None, br, s.shape[2]), lambda i, pos_ref: (pos_ref[0], i, 0))
                      for s, br in zip(slist, rows)]
            + [pl.BlockSpec((3, br, s.shape[2]), lambda i, pos_ref: (0, i, 0)) for s, br in zip(slist, rows)],
            out_specs=[pl.BlockSpec((None, br, s.shape[2]), lambda i, pos_ref: (pos_ref[1], i, 0))
                       for s, br in zip(slist, rows)]),
        compiler_params=_cparams(1),
    )(pos, *slist, *gots)


def _pair_share_rider(flist):
    n = len(flist)

    def copies(outs, sems):
        x, y, c, _ = _mesh_pos()
        sends = [_remote(outs[k].at[c], outs[k].at[c], sems[0], sems[1], k, (x, y, 1 - c)) for k in range(n)]
        arrivals = [_remote(outs[k].at[1 - c], outs[k].at[1 - c], sems[0], sems[1], k, (x, y, c)) for k in range(n)]
        return sends, arrivals

    def start(ins, outs, sems):
        for cp in copies(outs, sems)[0]:
            cp.start()

    def finish(ins, outs, sems):
        sends, arrivals = copies(outs, sems)
        for cp in arrivals:
            cp.wait_recv()
        for cp in sends:
            cp.wait_send()

    return _Rider(flist, [jax.ShapeDtypeStruct(f.shape, f.dtype) for f in flist],
                  [pltpu.SemaphoreType.DMA((n,)), pltpu.SemaphoreType.DMA((n,))], start, finish, in_place=True)


def _fwd_pw1_glu(x, w1s, b1, tm):
    t, d = x.shape
    dh = d // 2

    def body(x_ref, w_ref, b_ref, a_ref, g_ref, u_ref):
        xb = x_ref[...].astype(BF16)
        for hh in range(2):
            cs = slice(hh * dh, (hh + 1) * dh)
            a = _dot(xb, w_ref[hh]) + b_ref[:, hh * dh:(hh + 1) * dh]
            g = _dot(xb, w_ref[2 + hh]) + b_ref[:, d + hh * dh:d + (hh + 1) * dh]
            a_ref[:, cs] = a.astype(BF16)
            g_ref[:, cs] = g.astype(BF16)
            u_ref[:, cs] = a * jax.nn.sigmoid(g)

    return pl.pallas_call(
        body, name="fwd_pw1_glu", grid=(t // tm,),
        in_specs=[_rows(tm, d), _const((4, d, dh)), _const((1, 2 * d))],
        out_specs=[_rows(tm, d)] * 3,
        out_shape=[jax.ShapeDtypeStruct((t, d), BF16), jax.ShapeDtypeStruct((t, d), BF16),
                   jax.ShapeDtypeStruct((t, d), F32)],
        compiler_params=_cparams(),
    )(x, w1s, b1)


def _fill_shifted(sh_ref, ext_ref):
    n = sh_ref.shape[1]
    for s in range(1, 8):
        sh_ref[s - 1] = ext_ref[pl.ds(s, n), :]


def _ext_rows(ext, sh, e, base, rows, ls):
    if e % 8 == 0:
        return ext[pl.ds(base + e, rows), ls]
    return sh[e % 8 - 1, pl.ds(base + (e // 8) * 8, rows), ls]


CONV_CHUNK = 64
LANES = 256


def _tap_sum(w_ref, ext, sh, base, d, tap_row, out_ref, bias_ref=None):
    groups = CONV_CHUNK // 8
    for lg in range(d // LANES):
        ls = slice(lg * LANES, (lg + 1) * LANES)
        acc = jnp.zeros((groups, 8, LANES), F32)
        for k in range(CONV_WIDTH):
            x = _ext_rows(ext, sh, tap_row(k), base, CONV_CHUNK, ls)
            acc = acc + w_ref[k, :, ls] * x.reshape(groups, 8, LANES)
        acc = acc.reshape(CONV_CHUNK, LANES)
        out_ref[pl.ds(base, CONV_CHUNK), ls] = acc if bias_ref is None else acc + bias_ref[:, ls]


def _fwd_conv_tail(u, x0, wdw, bdw, lng, lnb, w2, b2, mixg, mixb, tm, riders=()):
    t, d = u.shape
    hb = tm // CONV_HALO

    def body(u_ref, uh_ref, x_ref, w_ref, bdw_ref, lng_ref, lnb_ref, w2_ref, b2_ref, mg_ref, mb_ref,
             c_ref, z_ref, y_ref, ext, sh):
        i = pl.program_id(0)
        ext[0:CONV_HALO] = jnp.where(i == 0, 0.0, uh_ref[...])
        ext[CONV_HALO:CONV_HALO + tm] = u_ref[...]
        ext[CONV_HALO + tm:CONV_HALO + tm + 8] = jnp.zeros((8, d), F32)
        _fill_shifted(sh, ext)

        def chunk(r, carry):
            base = pl.multiple_of(r * CONV_CHUNK, CONV_CHUNK)
            _tap_sum(w_ref, ext, sh, base, d, lambda k: k + CONV_HALO - (CONV_WIDTH - 1), c_ref, bdw_ref)
            return carry

        lax.fori_loop(0, tm // CONV_CHUNK, chunk, 0)
        n = _ln_fwd(c_ref[...], lng_ref[...], lnb_ref[...])
        s = n * jax.nn.sigmoid(n)
        m = _dot(s.astype(BF16), w2_ref[...]) + b2_ref[...]
        z = ALPHA * x_ref[...] + m
        z_ref[...] = z
        y_ref[...] = _ln_fwd(z, mg_ref[...], mb_ref[...])

    vec = _const((1, d))
    return _tc_call(
        body, name="fwd_conv_tail", nt=t // tm,
        in_specs=[_rows(tm, d), pl.BlockSpec((CONV_HALO, d), lambda i: (jnp.maximum(i * hb - 1, 0), 0)), _rows(tm, d),
                  _const((CONV_HALO, 8, d)), vec, vec, vec, _const((d, d)), vec, vec, vec],
        out_specs=[_rows(tm, d)] * 3,
        out_shape=[jax.ShapeDtypeStruct((t, d), F32)] * 3,
        scratch_shapes=[pltpu.VMEM((tm + CONV_HALO + 8, d), F32), pltpu.VMEM((7, tm + CONV_HALO, d), F32)],
        operands=(u, u, x0, wdw, bdw, lng, lnb, w2, b2, mixg, mixb), riders=riders)


def _fwd_ffn(x, wg, wu, wd, layer, lng, lnb, tm, target=None):
    t, d = x.shape
    fs = wg.shape[-1]
    nt = t // tm
    with_loss = target is not None

    def hidden(x_ref, wg_ref, wu_ref, wd_ref, act_ref, bm_ref, hm_ref):
        xv = x_ref[...]
        xb = xv.astype(BF16)
        f = jnp.zeros((tm, d), F32)
        for j in range(N_CHIPS):
            gj = _dot(xb, wg_ref[j])
            uj = _dot(xb, wu_ref[j])
            act, dact = _silu_and_grad(gj)
            act_ref[j] = act.astype(BF16)
            bm_ref[j] = (uj * dact).astype(BF16)
            hmb = (act * uj).astype(BF16)
            hm_ref[j] = hmb
            f = f + _dot(hmb, wd_ref[j])
        return ALPHA * xv + f

    def body(x_ref, wg_ref, wu_ref, wd_ref, g_ref, b_ref, act_ref, bm_ref, hm_ref, z_ref, y_ref):
        z = hidden(x_ref, wg_ref, wu_ref, wd_ref, act_ref, bm_ref, hm_ref)
        z_ref[...] = z
        y_ref[...] = _ln_fwd(z, g_ref[...], b_ref[...])

    def body_loss(x_ref, wg_ref, wu_ref, wd_ref, g_ref, b_ref, t_ref, act_ref, bm_ref, hm_ref, dz_ref,
                  dlg_ref, dlb_ref, loss_ref, acc_g, acc_b, acc_l):
        i = pl.program_id(0)
        _acc_init(i, acc_g, acc_b, acc_l)
        z = hidden(x_ref, wg_ref, wu_ref, wd_ref, act_ref, bm_ref, hm_ref)
        zhat, rstd = _ln_stats(z)
        gain = g_ref[...]
        err = zhat * gain + b_ref[...] - t_ref[...]
        acc_l[...] += _colsum8(err * err)
        dy = err * (1.0 / d)
        acc_g[...] += _colsum8(dy * zhat)
        acc_b[...] += _colsum8(dy)
        dzh = dy * gain
        m1 = jnp.mean(dzh, axis=-1, keepdims=True)
        m2 = jnp.mean(dzh * zhat, axis=-1, keepdims=True)
        dz_ref[...] = rstd * (dzh - m1 - zhat * m2)

        @pl.when(i == nt - 1)
        def _():
            dlg_ref[...] = jnp.sum(acc_g[...], axis=0, keepdims=True)
            dlb_ref[...] = jnp.sum(acc_b[...], axis=0, keepdims=True)
            loss_ref[...] = jnp.sum(acc_l[...], keepdims=True) * (0.5 / d)

    wcol = pl.BlockSpec((N_CHIPS, None, d, fs), lambda i: (0, layer, 0, 0), pipeline_mode=pl.Buffered(1))
    wrow = pl.BlockSpec((N_CHIPS, None, fs, d), lambda i: (0, layer, 0, 0), pipeline_mode=pl.Buffered(1))
    hid = pl.BlockSpec((N_CHIPS, tm, fs), lambda i: (0, i, 0))
    in_specs = [_rows(tm, d), wcol, wcol, wrow, _const((1, d)), _const((1, d))]
    hid_shapes = [jax.ShapeDtypeStruct((N_CHIPS, t, fs), BF16)] * 3
    if not with_loss:
        return pl.pallas_call(
            body, name=f"fwd_ffn{layer}", grid=(nt,), in_specs=in_specs,
            out_specs=[hid, hid, hid, _rows(tm, d), _rows(tm, d)],
            out_shape=hid_shapes + [jax.ShapeDtypeStruct((t, d), F32)] * 2, compiler_params=_cparams(),
        )(x, wg, wu, wd, lng, lnb)
    return pl.pallas_call(
        body_loss, name=f"fwd_ffn{layer}_loss", grid=(nt,), in_specs=in_specs + [_rows(tm, d)],
        out_specs=[hid, hid, hid, _rows(tm, d), _acc_out((1, d)), _acc_out((1, d)), _acc_out((1, 1))],
        out_shape=hid_shapes + [jax.ShapeDtypeStruct((t, d), F32)] + [jax.ShapeDtypeStruct((1, d), F32)] * 2
        + [jax.ShapeDtypeStruct((1, 1), F32)],
        scratch_shapes=[pltpu.VMEM((8, d), F32)] * 3, compiler_params=_cparams(),
    )(x, wg, wu, wd, lng, lnb, target)


def _attn_band():
    kt = lax.broadcasted_iota(jnp.int32, (BLOCK, BLOCK), 0)
    qi = lax.broadcasted_iota(jnp.int32, (BLOCK, BLOCK), 1)
    current = kt <= qi
    delta = qi - kt + jnp.where(current, 0, BLOCK)
    return current, delta.astype(F32)


def _fold(full, current):
    return jnp.where(current, full[BLOCK:2 * BLOCK], full[0:BLOCK])


def _unfold(folded, current):
    zero = jnp.zeros_like(folded)
    return jnp.concatenate([jnp.where(current, zero, folded), jnp.where(current, folded, zero)], axis=0)


def _slope(h, nq):
    return 2.0 ** (-ALIBI_MAX * (h + 1) / nq)


def _softmax_with_sink(s_full, slope, band, has_previous, sink):
    current, delta = band
    s = _fold(s_full, current) * (1.0 / math.sqrt(HEAD_DIM)) - slope * delta
    s = jnp.where(jnp.logical_or(current, has_previous), s, NEG_INF)
    m = jnp.maximum(jnp.max(s, axis=0, keepdims=True), sink)
    p = jnp.exp(s - m)
    e_sink = jnp.exp(sink - m)
    inv = 1.0 / (jnp.sum(p, axis=0, keepdims=True) + e_sink)
    return p * inv, e_sink * inv


def _heads_on_lanes(ref, b, g, group):
    first = g * group
    return jnp.concatenate([ref[b, (first + hh) * HEAD_DIM:(first + hh + 1) * HEAD_DIM, :] for hh in range(group)],
                           axis=1)


def _fill_kv(kv_scr, halo, tile, tm):
    for j in range(2 * N_KV_HEADS):
        kv_scr[j, 0:BLOCK] = halo[:, j * HEAD_DIM:(j + 1) * HEAD_DIM]
        kv_scr[j, BLOCK:BLOCK + tm] = tile[:, j * HEAD_DIM:(j + 1) * HEAD_DIM]


def _cols(d, tm):
    return pl.BlockSpec((d, tm), lambda i: (0, i))


def _fwd_attn(x, wqt, bqt, wkv, bkv, sinks, wo, bo, mixg, mixb, tm):
    t, d = x.shape
    nq = d // HEAD_DIM
    group = nq // N_KV_HEADS
    nb = tm // BLOCK

    def body(sink_ref, x_ref, xh_ref, wqt_ref, bqt_ref, wkv_ref, bkv_ref, wo_ref, bo_ref, mg_ref, mb_ref,
             qt_ref, kv_ref, ot_ref, z_ref, y_ref, kv_scr, qt_scr, ot_scr):
        i = pl.program_id(0)
        xv = x_ref[...]
        xb = xv.astype(BF16)
        qt = (_dot_nt(wqt_ref[...], xb) + bqt_ref[...]).astype(BF16)
        qt_ref[...] = qt
        for b in range(nb):
            qt_scr[b] = qt[:, b * BLOCK:(b + 1) * BLOCK]
        kvb = (_dot(xb, wkv_ref[...]) + bkv_ref[...]).astype(BF16)
        kv_ref[...] = kvb
        _fill_kv(kv_scr, (_dot(xh_ref[...].astype(BF16), wkv_ref[...]) + bkv_ref[...]).astype(BF16), kvb, tm)
        band = _attn_band()

        def block(b, carry):
            r0 = pl.multiple_of(b * BLOCK, BLOCK)
            has_previous = jnp.logical_or(i > 0, b > 0)
            for g in range(N_KV_HEADS):
                kk = kv_scr[g, pl.ds(r0, 2 * BLOCK), :]
                vv = kv_scr[N_KV_HEADS + g, pl.ds(r0, 2 * BLOCK), :]
                s_all = _dot(kk, _heads_on_lanes(qt_scr, b, g, group))
                probs = []
                for hh in range(group):
                    h = g * group + hh
                    p, _ = _softmax_with_sink(s_all[:, hh * BLOCK:(hh + 1) * BLOCK], _slope(h, nq), band,
                                              has_previous, sink_ref[h])
                    probs.append(_unfold(p.astype(BF16), band[0]))
                o_all = _dot_tn(vv, jnp.concatenate(probs, axis=1))
                for hh in range(group):
                    h = g * group + hh
                    ot_scr[b, h * HEAD_DIM:(h + 1) * HEAD_DIM, :] = o_all[:, hh * BLOCK:(hh + 1) * BLOCK].astype(BF16)
            return carry

        lax.fori_loop(0, nb, block, 0, unroll=True)
        ot = jnp.concatenate([ot_scr[b] for b in range(nb)], axis=1)
        ot_ref[...] = ot
        z = ALPHA * xv + _dot_tn(ot, wo_ref[...]) + bo_ref[...]
        z_ref[...] = z
        y_ref[...] = _ln_fwd(z, mg_ref[...], mb_ref[...])

    hb = tm // BLOCK
    vec = _const((1, d))
    return pl.pallas_call(
        body, name="fwd_attn", grid=(t // tm,),
        in_specs=[pl.BlockSpec(memory_space=pltpu.SMEM),
                  _rows(tm, d), pl.BlockSpec((BLOCK, d), lambda i: (jnp.maximum(i * hb - 1, 0), 0)),
                  _const((d, d)), _const((d, 1)), _const((d, 2 * KVD)), _const((1, 2 * KVD)), _const((d, d)), vec, vec,
                  vec],
        out_specs=[_cols(d, tm), _rows(tm, 2 * KVD), _cols(d, tm), _rows(tm, d), _rows(tm, d)],
        out_shape=[jax.ShapeDtypeStruct((d, t), BF16), jax.ShapeDtypeStruct((t, 2 * KVD), BF16),
                   jax.ShapeDtypeStruct((d, t), BF16), jax.ShapeDtypeStruct((t, d), F32),
                   jax.ShapeDtypeStruct((t, d), F32)],
        scratch_shapes=[pltpu.VMEM((2 * N_KV_HEADS, tm + BLOCK, HEAD_DIM), BF16), pltpu.VMEM((nb, d, BLOCK), BF16),
                        pltpu.VMEM((nb, d, BLOCK), BF16)],
        compiler_params=_cparams(),
    )(sinks, x, x, wqt, bqt, wkv, bkv, wo, bo, mixg, mixb)


def _write_sums(i, nt, pairs):
    @pl.when(i == nt - 1)
    def _():
        for out_ref, acc in pairs:
            out_ref[...] = jnp.sum(acc[...], axis=0, keepdims=True)


def _bwd_ffn_dx(dz, act, bm, wg, wu, wd, layer, z_in, g_in, tm, riders=()):
    t, d = dz.shape
    fs = wg.shape[-1]
    nt = t // tm

    def body(dz_ref, act_ref, bm_ref, wg_ref, wu_ref, wd_ref, zin_ref, gin_ref,
             dgg_ref, duu_ref, dzin_ref, dg_ref, db_ref, dsum_ref, acc_g, acc_b, acc_s):
        i = pl.program_id(0)
        _acc_init(i, acc_g, acc_b, acc_s)
        dzv = dz_ref[...]
        dzb = dzv.astype(BF16)
        dx = ALPHA * dzv
        for j in range(N_CHIPS):
            dh = _dot_nt(dzb, wd_ref[j])
            dgb = (dh * bm_ref[j].astype(F32)).astype(BF16)
            dub = (dh * act_ref[j].astype(F32)).astype(BF16)
            dgg_ref[j] = dgb
            duu_ref[j] = dub
            dx = dx + _dot_nt(dgb, wg_ref[j]) + _dot_nt(dub, wu_ref[j])
        dzin_ref[...] = dx
        for r0 in range(0, tm, LN_CHUNK):
            rows = slice(r0, r0 + LN_CHUNK)
            dxc = dzin_ref[rows, :]
            dz_in, zhat = _ln_bwd(dxc, zin_ref[rows, :], gin_ref[...])
            acc_g[...] += _colsum8(dxc * zhat)
            acc_b[...] += _colsum8(dxc)
            acc_s[...] += _colsum8(dz_in)
            dzin_ref[rows, :] = dz_in
        _write_sums(i, nt, [(dg_ref, acc_g), (db_ref, acc_b), (dsum_ref, acc_s)])

    wcol = pl.BlockSpec((N_CHIPS, None, d, fs), lambda i: (0, layer, 0, 0), pipeline_mode=pl.Buffered(1))
    wrow = pl.BlockSpec((N_CHIPS, None, fs, d), lambda i: (0, layer, 0, 0), pipeline_mode=pl.Buffered(1))
    hid = pl.BlockSpec((N_CHIPS, tm, fs), lambda i: (0, i, 0))
    return _tc_call(
        body, name=f"bwd_ffn_dx{layer}", nt=nt,
        in_specs=[_rows(tm, d), hid, hid, wcol, wcol, wrow, _rows(tm, d), _const((1, d))],
        out_specs=[hid, hid, _rows(tm, d)] + [_acc_out((1, d))] * 3,
        out_shape=[jax.ShapeDtypeStruct((N_CHIPS, t, fs), BF16)] * 2 + [jax.ShapeDtypeStruct((t, d), F32)]
        + [jax.ShapeDtypeStruct((1, d), F32)] * 3,
        scratch_shapes=[pltpu.VMEM((8, d), F32)] * 3,
        operands=(dz, act, bm, wg, wu, wd, z_in, g_in), riders=riders)


def _matmul_tn(a, b, tt, name, carry=None):
    ja, t, ka = a.shape
    jb, _, nb = b.shape
    nj = max(ja, jb)

    def body(a_ref, b_ref, *rest):
        o_ref = rest[-1] if carry is None else rest[1]

        @pl.when(pl.program_id(0) == 0)
        def _():
            o_ref[...] = jnp.zeros_like(o_ref)

        a0 = a_ref[0].astype(BF16) if ja == 1 else None
        b0 = b_ref[0].astype(BF16) if jb == 1 else None
        for j in range(nj):
            aj = a0 if ja == 1 else a_ref[j].astype(BF16)
            bj = b0 if jb == 1 else b_ref[j].astype(BF16)
            o_ref[j] += _dot_tn(aj, bj)
        if carry is not None:
            rest[2][...] = rest[0][...]

    in_specs = [pl.BlockSpec((ja, tt, ka), lambda i: (0, i, 0)), pl.BlockSpec((jb, tt, nb), lambda i: (0, i, 0))]
    out_specs = [pl.BlockSpec((nj, ka, nb), lambda i: (0, 0, 0))]
    out_shape = [jax.ShapeDtypeStruct((nj, ka, nb), F32)]
    operands = [a, b]
    if carry is not None:
        in_specs.append(_rows(tt, carry.shape[1]))
        out_specs.append(_rows(tt, carry.shape[1]))
        out_shape.append(jax.ShapeDtypeStruct(carry.shape, carry.dtype))
        operands.append(carry)
    res = pl.pallas_call(body, name=name, grid=(t // tt,), in_specs=in_specs, out_specs=out_specs,
                         out_shape=out_shape, compiler_params=_cparams())(*operands)
    return res[0] if carry is None else (res[0], res[1])


def _matmul_nn(at, b, tt, name):
    ka, t = at.shape
    nb = b.shape[1]

    def body(a_ref, b_ref, o_ref):
        @pl.when(pl.program_id(0) == 0)
        def _():
            o_ref[...] = jnp.zeros_like(o_ref)

        o_ref[...] += _dot(a_ref[...].astype(BF16), b_ref[...].astype(BF16))

    return pl.pallas_call(
        body, name=name, grid=(t // tt,),
        in_specs=[pl.BlockSpec((ka, tt), lambda i: (0, i)), pl.BlockSpec((tt, nb), lambda i: (i, 0))],
        out_specs=pl.BlockSpec((ka, nb), lambda i: (0, 0)), out_shape=jax.ShapeDtypeStruct((ka, nb), F32),
        compiler_params=_cparams(1),
    )(at, b)


def _bwd_attn(dz_all, qt, kv, sinks, wo, wqt, wkv, z_in, g_in, tm, riders=()):
    t, d = dz_all.shape
    nq = d // HEAD_DIM
    group = nq // N_KV_HEADS
    nb = tm // BLOCK
    nt = t // tm
    hb = tm // BLOCK
    n_kv = 2 * N_KV_HEADS

    def body(sink_ref, dz_ref, qt_ref, kv_ref, kvh_ref, wo_ref, wqt_ref, wkv_ref, zin_ref, gin_ref,
             dqt_ref, dkv_ref, dx_ref, dbq_ref, dbkv_ref, dsink_ref, ding_ref, dinb_ref,
             kv_scr, dkv_scr, qt_scr, dot_scr, dqt_scr, carry, acc_q, acc_kv, acc_s, acc_ig, acc_ib):
        i = pl.program_id(0)
        ti = nt - 1 - i
        _acc_init(i, carry, acc_q, acc_kv, acc_s, acc_ig, acc_ib)
        dz = dz_ref[...]
        do_t = _dot_nt(wo_ref[...], dz.astype(BF16)).astype(BF16)
        for b in range(nb):
            dot_scr[b] = do_t[:, b * BLOCK:(b + 1) * BLOCK]
            qt_scr[b] = qt_ref[:, b * BLOCK:(b + 1) * BLOCK]
        _fill_kv(kv_scr, kvh_ref[...], kv_ref[...], tm)
        dkv_scr[:, 0:tm] = jnp.zeros((n_kv, tm, HEAD_DIM), F32)
        dkv_scr[:, tm:tm + BLOCK] = carry[...]
        band = _attn_band()

        def block(b, c):
            r0 = pl.multiple_of(b * BLOCK, BLOCK)
            has_previous = jnp.logical_or(ti > 0, b > 0)
            for g in range(N_KV_HEADS):
                kk = kv_scr[g, pl.ds(r0, 2 * BLOCK), :]
                vv = kv_scr[N_KV_HEADS + g, pl.ds(r0, 2 * BLOCK), :]
                q_all = _heads_on_lanes(qt_scr, b, g, group)
                do_all = _heads_on_lanes(dot_scr, b, g, group)
                s_all = _dot(kk, q_all)
                dp_all = _dot(vv, do_all)
                probs, dscores = [], []
                for hh in range(group):
                    h = g * group + hh
                    cols = slice(hh * BLOCK, (hh + 1) * BLOCK)
                    p, p_sink = _softmax_with_sink(s_all[:, cols], _slope(h, nq), band, has_previous, sink_ref[h])
                    dp = _fold(dp_all[:, cols], band[0])
                    rs = jnp.sum(p * dp, axis=0, keepdims=True)
                    acc_s[h:h + 1, :] += -(p_sink * rs)
                    ds = p * (dp - rs) * (1.0 / math.sqrt(HEAD_DIM))
                    probs.append(_unfold(p.astype(BF16), band[0]))
                    dscores.append(_unfold(ds.astype(BF16), band[0]))
                p_all = jnp.concatenate(probs, axis=1)
                ds_all = jnp.concatenate(dscores, axis=1)
                dq_all = _dot_tn(kk, ds_all)
                for hh in range(group):
                    h = g * group + hh
                    dqt_scr[b, h * HEAD_DIM:(h + 1) * HEAD_DIM, :] = dq_all[:, hh * BLOCK:(hh + 1) * BLOCK]
                dkv_scr[g, pl.ds(r0, 2 * BLOCK), :] += _dot_nt(ds_all, q_all)
                dkv_scr[N_KV_HEADS + g, pl.ds(r0, 2 * BLOCK), :] += _dot_nt(p_all, do_all)
            return c

        lax.fori_loop(0, nb, block, 0, unroll=True)
        carry[...] = dkv_scr[:, 0:BLOCK]
        dkv = jnp.concatenate([dkv_scr[j, BLOCK:BLOCK + tm] for j in range(n_kv)], axis=1)
        acc_kv[...] += _colsum8(dkv)
        dkvb = dkv.astype(BF16)
        dkv_ref[...] = dkvb
        dqt = jnp.concatenate([dqt_scr[b] for b in range(nb)], axis=1)
        for b in range(nb):
            acc_q[...] += dqt_scr[b]
        dqtb = dqt.astype(BF16)
        dqt_ref[...] = dqtb
        dx = ALPHA * dz + _dot_tn(dqtb, wqt_ref[...]) + _dot_nt(dkvb, wkv_ref[...])
        dz_in, zhat_in = _ln_bwd(dx, zin_ref[...], gin_ref[...])
        acc_ig[...] += _colsum8(dx * zhat_in)
        acc_ib[...] += _colsum8(dx)
        dx_ref[...] = dz_in
        _write_sums(i, nt, [(dbkv_ref, acc_kv), (ding_ref, acc_ig), (dinb_ref, acc_ib)])

        @pl.when(i == nt - 1)
        def _():
            dbq_ref[...] = jnp.sum(acc_q[...], axis=1, keepdims=True)
            dsink_ref[...] = jnp.sum(acc_s[...], axis=1, keepdims=True)

    rev = lambda w: pl.BlockSpec((tm, w), lambda i: (nt - 1 - i, 0))
    rev_cols = pl.BlockSpec((d, tm), lambda i: (0, nt - 1 - i))
    vec = _const((1, d))
    return _tc_call(
        body, name="bwd_attn", nt=nt,
        in_specs=[pl.BlockSpec(memory_space=pltpu.SMEM), rev(d), rev_cols, rev(2 * KVD),
                  pl.BlockSpec((BLOCK, 2 * KVD), lambda i: (jnp.maximum((nt - 1 - i) * hb - 1, 0), 0)),
                  _const((d, d)), _const((d, d)), _const((d, 2 * KVD)), rev(d), vec],
        out_specs=[rev_cols, rev(2 * KVD), rev(d), _acc_out((d, 1)), _acc_out((1, 2 * KVD)), _acc_out((nq, 1)),
                   _acc_out((1, d)), _acc_out((1, d))],
        out_shape=[jax.ShapeDtypeStruct((d, t), BF16), jax.ShapeDtypeStruct((t, 2 * KVD), BF16),
                   jax.ShapeDtypeStruct((t, d), F32), jax.ShapeDtypeStruct((d, 1), F32),
                   jax.ShapeDtypeStruct((1, 2 * KVD), F32), jax.ShapeDtypeStruct((nq, 1), F32)]
        + [jax.ShapeDtypeStruct((1, d), F32)] * 2,
        scratch_shapes=[pltpu.VMEM((n_kv, tm + BLOCK, HEAD_DIM), BF16), pltpu.VMEM((n_kv, tm + BLOCK, HEAD_DIM), F32),
                        pltpu.VMEM((nb, d, BLOCK), BF16), pltpu.VMEM((nb, d, BLOCK), BF16),
                        pltpu.VMEM((nb, d, BLOCK), F32), pltpu.VMEM((n_kv, BLOCK, HEAD_DIM), F32),
                        pltpu.VMEM((d, BLOCK), F32), pltpu.VMEM((8, 2 * KVD), F32), pltpu.VMEM((nq, BLOCK), F32),
                        pltpu.VMEM((8, d), F32), pltpu.VMEM((8, d), F32)],
        operands=(sinks, dz_all, qt, kv, kv, wo, wqt, wkv, z_in, g_in), riders=riders)


def _bwd_conv_head(dz, c, w2, lng, lnb, tm, riders=()):
    t, d = dz.shape
    nt = t // tm

    def body(dz_ref, c_ref, w2_ref, lg_ref, lb_ref, s_ref, dc_ref, dlg_ref, dlb_ref, a3, a4):
        i = pl.program_id(0)
        _acc_init(i, a3, a4)
        dz = dz_ref[...]
        chat, rstd = _ln_stats(c_ref[...])
        n = chat * lg_ref[...] + lb_ref[...]
        act, dact = _silu_and_grad(n)
        s_ref[...] = act.astype(BF16)
        dn = _dot_nt(dz.astype(BF16), w2_ref[...]) * dact
        a3[...] += _colsum8(dn * chat)
        a4[...] += _colsum8(dn)
        dch = dn * lg_ref[...]
        m1 = jnp.mean(dch, axis=-1, keepdims=True)
        m2 = jnp.mean(dch * chat, axis=-1, keepdims=True)
        dc_ref[...] = rstd * (dch - m1 - chat * m2)
        _write_sums(i, nt, [(dlg_ref, a3), (dlb_ref, a4)])

    vec = _const((1, d))
    return _tc_call(
        body, name="bwd_conv_head", nt=nt,
        in_specs=[_rows(tm, d), _rows(tm, d), _const((d, d)), vec, vec],
        out_specs=[_rows(tm, d), _rows(tm, d)] + [_acc_out((1, d))] * 2,
        out_shape=[jax.ShapeDtypeStruct((t, d), BF16), jax.ShapeDtypeStruct((t, d), F32)]
        + [jax.ShapeDtypeStruct((1, d), F32)] * 2,
        scratch_shapes=[pltpu.VMEM((8, d), F32)] * 2, operands=(dz, c, w2, lng, lnb), riders=riders)


def _bwd_conv_glu(dc, u, a, g, dz, wdw, w1s, tm, riders=()):
    t, d = dc.shape
    dh_w = d // 2
    nt = t // tm
    hb = tm // CONV_HALO
    last_halo = t // CONV_HALO - 1

    def body(dc_ref, dcn_ref, u_ref, a_ref, g_ref, dz_ref, w_ref, w1_ref,
             dx_ref, dh_ref, db1_ref, dbdw_ref, dw_ref, ext, sh, du_scr, acc_b1, acc_bdw, acc_w):
        i = pl.program_id(0)
        _acc_init(i, acc_b1, acc_bdw, acc_w)
        dcv = dc_ref[...]
        acc_bdw[...] += _colsum8(dcv)

        ext[0:tm] = dcv
        ext[tm:tm + CONV_HALO] = jnp.where(i == nt - 1, 0.0, dcn_ref[...])
        ext[tm + CONV_HALO:tm + CONV_HALO + 8] = jnp.zeros((8, d), F32)
        _fill_shifted(sh, ext)

        def du_chunk(r, carry):
            base = pl.multiple_of(r * CONV_CHUNK, CONV_CHUNK)
            _tap_sum(w_ref, ext, sh, base, d, lambda k: CONV_WIDTH - 1 - k, du_scr)
            return carry

        lax.fori_loop(0, tm // CONV_CHUNK, du_chunk, 0)

        def dw_chunk(r, carry):
            base = pl.multiple_of(r * CONV_CHUNK, CONV_CHUNK)
            groups = CONV_CHUNK // 8
            for lg in range(d // LANES):
                ls = slice(lg * LANES, (lg + 1) * LANES)
                uv = u_ref[pl.ds(base, CONV_CHUNK), ls].reshape(groups, 8, LANES)
                for k in range(CONV_WIDTH):
                    x = _ext_rows(ext, sh, CONV_WIDTH - 1 - k, base, CONV_CHUNK, ls).reshape(groups, 8, LANES)
                    acc_w[k, :, ls] += jnp.sum(uv * x, axis=0)
            return carry

        lax.fori_loop(0, tm // CONV_CHUNK, dw_chunk, 0)

        du = du_scr[...]
        av = a_ref[...].astype(F32)
        sg = jax.nn.sigmoid(g_ref[...].astype(F32))
        da = du * sg
        dg = du * av * sg * (1.0 - sg)
        acc_b1[:, 0:d] += _colsum8(da)
        acc_b1[:, d:2 * d] += _colsum8(dg)
        dx = ALPHA * dz_ref[...]
        for j, part in enumerate([da[:, 0:dh_w], da[:, dh_w:d], dg[:, 0:dh_w], dg[:, dh_w:d]]):
            pb = part.astype(BF16)
            dh_ref[j] = pb
            dx = dx + _dot_nt(pb, w1_ref[j])
        dx_ref[...] = dx

        @pl.when(i == nt - 1)
        def _():
            db1_ref[...] = jnp.sum(acc_b1[...], axis=0, keepdims=True)
            dbdw_ref[...] = jnp.sum(acc_bdw[...], axis=0, keepdims=True)
            dw_ref[...] = jnp.sum(acc_w[...], axis=1)

    return _tc_call(
        body, name="bwd_conv_glu", nt=nt,
        in_specs=[_rows(tm, d), pl.BlockSpec((CONV_HALO, d), lambda i: (jnp.minimum((i + 1) * hb, last_halo), 0)),
                  _rows(tm, d), _rows(tm, d), _rows(tm, d), _rows(tm, d), _const((CONV_HALO, 8, d)),
                  _const((4, d, dh_w))],
        out_specs=[_rows(tm, d), pl.BlockSpec((4, tm, dh_w), lambda i: (0, i, 0)), _acc_out((1, 2 * d)),
                   _acc_out((1, d)), _acc_out((CONV_HALO, d))],
        out_shape=[jax.ShapeDtypeStruct((t, d), F32), jax.ShapeDtypeStruct((4, t, dh_w), BF16),
                   jax.ShapeDtypeStruct((1, 2 * d), F32), jax.ShapeDtypeStruct((1, d), F32),
                   jax.ShapeDtypeStruct((CONV_HALO, d), F32)],
        scratch_shapes=[pltpu.VMEM((tm + CONV_HALO + 8, d), F32), pltpu.VMEM((7, tm + CONV_HALO, d), F32),
                        pltpu.VMEM((tm, d), F32), pltpu.VMEM((8, 2 * d), F32), pltpu.VMEM((8, d), F32),
                        pltpu.VMEM((CONV_HALO, 8, d), F32)],
        operands=(dc, dc, u, a, g, dz, wdw, w1s), riders=riders)


def _adamw_update(w_ref, g_ref, m_ref, v_ref, d_ref, nm_ref, nv_ref):
    gv = g_ref[...]
    nm = ADAM_B1 * m_ref[...] + (1.0 - ADAM_B1) * gv
    nv = ADAM_B2 * v_ref[...] + (1.0 - ADAM_B2) * (gv * gv)
    m_hat = nm / (1.0 - ADAM_B1 ** ADAM_STEP)
    v_hat = nv / (1.0 - ADAM_B2 ** ADAM_STEP)
    d_ref[...] = -ADAM_LR * (m_hat / (jnp.sqrt(v_hat) + ADAM_EPS) + ADAM_WD * w_ref[...])
    nm_ref[...] = nm
    nv_ref[...] = nv


ADAMW_STEPS = 8


def _adamw(params, name, riders=()):
    n = len(params)
    steps = ADAMW_STEPS if all(p[0].shape[0] % (8 * ADAMW_STEPS) == 0 for p in params) else 1

    def body(*refs):
        for k in range(n):
            _adamw_update(*refs[4 * k:4 * k + 4], *refs[4 * n + 3 * k:4 * n + 3 * k + 3])

    specs = [pl.BlockSpec((p[0].shape[0] // steps, p[0].shape[1]), lambda i: (i, 0)) for p in params]
    outs, rider_outs = _tc_call(
        body, name=name, nt=steps, in_specs=[s for s in specs for _ in range(4)],
        out_specs=[s for s in specs for _ in range(3)],
        out_shape=[jax.ShapeDtypeStruct(p[0].shape, F32) for p in params for _ in range(3)],
        operands=[a for p in params for a in p], riders=riders)
    return [tuple(outs[3 * k:3 * k + 3]) for k in range(n)], rider_outs


def _pad_to(v, n):
    return jnp.pad(v, (0, n - v.shape[0]))


def _round_up(n, m):
    return (n + m - 1) // m * m


def kernel(x, conv_w_pw1, conv_b_pw1, conv_w_dw, conv_b_dw, conv_ln_g, conv_ln_b, conv_w_pw2, conv_b_pw2, kv_w_k, kv_b_k, kv_w_v, kv_b_v, attn_w_q, attn_b_q, attn_sinks, attn_w_o, attn_b_o, ffn_w_gate, ffn_w_up, ffn_w_down, ln_mix_g, ln_mix_b, ln_ffn_g, ln_ffn_b, loss_target, m_conv_w_pw1, m_conv_b_pw1, m_conv_w_dw, m_conv_b_dw, m_conv_ln_g, m_conv_ln_b, m_conv_w_pw2, m_conv_b_pw2, m_kv_w_k, m_kv_b_k, m_kv_w_v, m_kv_b_v, m_attn_w_q, m_attn_b_q, m_attn_sinks, m_attn_w_o, m_attn_b_o, m_ffn_w_gate, m_ffn_w_up, m_ffn_w_down, m_ln_mix_g, m_ln_mix_b, m_ln_ffn_g, m_ln_ffn_b, v_conv_w_pw1, v_conv_b_pw1, v_conv_w_dw, v_conv_b_dw, v_conv_ln_g, v_conv_ln_b, v_conv_w_pw2, v_conv_b_pw2, v_kv_w_k, v_kv_b_k, v_kv_w_v, v_kv_b_v, v_attn_w_q, v_attn_b_q, v_attn_sinks, v_attn_w_o, v_attn_b_o, v_ffn_w_gate, v_ffn_w_up, v_ffn_w_down, v_ln_mix_g, v_ln_mix_b, v_ln_ffn_g, v_ln_ffn_b):
    args = dict(locals())
    w = {n: args[n] for n in WEIGHTS}
    mom = {n: args["m_" + n] for n in WEIGHTS}
    var = {n: args["v_" + n] for n in WEIGHTS}
    assert x.shape[0] == 1, "one sequence per device"
    t, d = x.shape[1], x.shape[2]
    dq = d // 4
    fs = ffn_w_gate.shape[-1]
    nq = d // HEAD_DIM
    x0 = x.reshape(t, d)
    target = loss_target.reshape(t, d)
    tm_big = min(512, t)
    tm_tn = min(1024, t)
    c_idx = lax.axis_index("c")

    me_idx = 2 * lax.axis_index("x") + lax.axis_index("y")

    def gather_buffer(v):
        buf = lax.empty((N_CHIPS,) + v.shape, v.dtype)
        return lax.dynamic_update_slice(buf, v[None], (me_idx,) + (0,) * v.ndim)

    def halves(v):
        return v.reshape(2, -1, v.shape[-1])

    small_sizes = [int(w[n].size) for n in SMALL_SHARDED]
    rs = _round_up(sum(small_sizes), 8 * 128) // 128
    spack = _pad_to(jnp.concatenate([w[n].reshape(-1) for n in SMALL_SHARDED]), rs * 128).reshape(rs, 128)
    conv_first = ['conv_w_pw1', 'conv_w_pw2']
    later = [n for n in BIG if n not in conv_first]
    (first_out,) = _run_riders(
        [_all_gather_rider([gather_buffer(halves(w[n].astype(BF16))) for n in conv_first], gather_buffer(spack))],
        "all_gather_conv")
    later_rider = _all_gather_rider([gather_buffer(halves(w[n].astype(BF16))) for n in later])
    gs = first_out[-1].reshape(N_CHIPS, rs * 128)
    full = {n: g.reshape((N_CHIPS,) + w[n].shape) for n, g in zip(conv_first, first_out)}
    off = 0
    for n, size in zip(SMALL_SHARDED, small_sizes):
        full[n] = gs[:, off:off + size].reshape((N_CHIPS,) + w[n].shape)
        off += size
    w1s = full['conv_w_pw1'].reshape(N_CHIPS, d, d // 2)
    w2 = full['conv_w_pw2'].reshape(d, d)
    b1 = full['conv_b_pw1'].reshape(1, 2 * d)
    wdw = jnp.pad(full['conv_w_dw'].reshape(N_CHIPS, CONV_WIDTH, dq).transpose(1, 0, 2).reshape(CONV_WIDTH, d),
                  ((0, CONV_HALO - CONV_WIDTH), (0, 0)))
    wdw = jnp.broadcast_to(wdw[:, None, :], (CONV_HALO, 8, d))
    bdw = full['conv_b_dw'].reshape(1, d)
    clng = full['conv_ln_g'].reshape(1, d)
    clnb = full['conv_ln_b'].reshape(1, d)
    b2 = full['conv_b_pw2'].reshape(1, d)
    bkv = jnp.concatenate([kv_b_k, kv_b_v]).reshape(1, 2 * KVD)
    sinks = attn_sinks.reshape(nq)
    mixg = [ln_mix_g[l].reshape(1, d) for l in range(DEPTH)]
    mixb = [ln_mix_b[l].reshape(1, d) for l in range(DEPTH)]
    ffng = [ln_ffn_g[l].reshape(1, d) for l in range(DEPTH)]
    ffnb = [ln_ffn_b[l].reshape(1, d) for l in range(DEPTH)]

    a_act, g_act, u_act = _fwd_pw1_glu(x0, w1s, b1, tm_big)
    (c_act, z1, x1), (later_out,) = _fwd_conv_tail(u_act, x0, wdw, bdw, clng, clnb, w2, b2, mixg[0], mixb[0], tm_big,
                                                   riders=[later_rider])
    full.update({n: g.reshape((N_CHIPS,) + w[n].shape) for n, g in zip(later, later_out)})
    wkv = jnp.concatenate([full['kv_w_k'].reshape(d, KVD), full['kv_w_v'].reshape(d, KVD)], axis=1)
    wqt = full['attn_w_q'].reshape(d, d).T
    wo = full['attn_w_o'].reshape(d, d)
    wg, wu, wd = full['ffn_w_gate'], full['ffn_w_up'], full['ffn_w_down']
    act0, bm0, hm0, z2, x2 = _fwd_ffn(x1, wg, wu, wd, 0, ffng[0], ffnb[0], tm_big)
    qt_act, kv_act, ot_act, z3, x3 = _fwd_attn(x2, wqt, attn_b_q.reshape(d, 1), wkv, bkv, sinks, wo, attn_b_o,
                                               mixg[1], mixb[1], tm_big)
    act1, bm1, hm1, dz4, d_fg1, d_fb1, loss_part = _fwd_ffn(x3, wg, wu, wd, 1, ffng[1], ffnb[1], tm_big, target=target)
    loss = lax.psum(loss_part[0, 0], ("x", "y", "c"))

    c_arr = c_idx.reshape(1).astype(jnp.int32)

    def halves4(v):
        return v.reshape(N_CHIPS, 2, -1, v.shape[-1])

    def arrays(group):
        return [p for _, p in group]

    def pair_sums(group, got):
        return _pair_sums(arrays(group), got, c_arr, "grad_pair_sum_" + group[0][0])

    pos_arr = jnp.stack([me_idx, c_idx]).astype(jnp.int32)

    def chip_sums(group, sums, got):
        return _chip_sums(sums, got, pos_arr, "grad_chip_sum_" + group[0][0])

    (dgg1, duu1, dz3, d_mg1, d_mb1, d_bo), _ = _bwd_ffn_dx(dz4, act1, bm1, wg, wu, wd, 1, z3, mixg[1], tm_big)
    g1 = [("ffn_w_gate1", halves4(_matmul_tn(x3[None], dgg1, tm_tn, "dw_gate1"))),
          ("ffn_w_up1", halves4(_matmul_tn(x3[None], duu1, tm_tn, "dw_up1"))),
          ("ffn_w_down1", halves4(_matmul_tn(hm1, dz4[None], tm_tn, "dw_down1")))]
    (dqt, dkv, dz2, d_bq, d_bkv, d_sinks, d_fg0, d_fb0), (got1,) = _bwd_attn(
        dz3, qt_act, kv_act, sinks, wo, wqt, wkv, z2, ffng[0], tm_big, riders=[_pair_exchange_rider(arrays(g1))])
    s1 = pair_sums(g1, got1)
    dwo = _matmul_nn(ot_act, dz3, tm_tn, "dw_o")
    dwq = _matmul_nn(dqt, x2, tm_tn, "dw_q").T
    dwkv = _matmul_tn(x2[None], dkv[None], tm_tn, "dw_kv")[0]
    g2 = [("attn_w_o", halves4(dwo)), ("attn_w_q", halves4(dwq)),
          ("kv_w_k", halves4(dwkv[:, 0:KVD])), ("kv_w_v", halves4(dwkv[:, KVD:2 * KVD]))]
    (dgg0, duu0, dz1, d_mg0, d_mb0, d_b2), (from_chips1, got2) = _bwd_ffn_dx(
        dz2, act0, bm0, wg, wu, wd, 0, z1, mixg[0], tm_big,
        riders=[_chip_scatter_rider(s1), _pair_exchange_rider(arrays(g2))])
    f1 = chip_sums(g1, s1, from_chips1)
    s2 = pair_sums(g2, got2)
    g3 = [("ffn_w_gate0", halves4(_matmul_tn(x1[None], dgg0, tm_tn, "dw_gate0"))),
          ("ffn_w_up0", halves4(_matmul_tn(x1[None], duu0, tm_tn, "dw_up0"))),
          ("ffn_w_down0", halves4(_matmul_tn(hm0, dz2[None], tm_tn, "dw_down0")))]
    (s_act, dc, d_clng, d_clnb), (got3, shared1) = _bwd_conv_head(
        dz1, c_act, w2, clng, clnb, tm_big, riders=[_pair_exchange_rider(arrays(g3)), _pair_share_rider(f1)])
    s3 = pair_sums(g3, got3)
    dw2 = _matmul_tn(s_act[None], dz1[None], tm_tn, "dw_pw2")
    (dx0, dh1, d_b1, d_bdw, d_wdw), (from_chips2, from_chips3) = _bwd_conv_glu(
        dc, u_act, a_act, g_act, dz1, wdw, w1s, tm_big, riders=[_chip_scatter_rider(s2), _chip_scatter_rider(s3)])
    f2 = chip_sums(g2, s2, from_chips2)
    f3 = chip_sums(g3, s3, from_chips3)
    dw1, grad_x = _matmul_tn(x0[None], dh1, tm_tn, "dw_pw1", carry=dx0)

    def rows4(v):
        return v.reshape(N_CHIPS, -1)

    def rep4(v):
        return jnp.broadcast_to(v.reshape(1, -1), (N_CHIPS, v.size))

    local = {
        'conv_b_pw1': rows4(d_b1),
        'conv_w_dw': rows4(d_wdw[0:CONV_WIDTH].reshape(CONV_WIDTH, N_CHIPS, dq).transpose(1, 0, 2)),
        'conv_b_dw': rows4(d_bdw), 'conv_ln_g': rows4(d_clng), 'conv_ln_b': rows4(d_clnb), 'conv_b_pw2': rows4(d_b2),
        'kv_b_k': rep4(d_bkv[:, 0:KVD]), 'kv_b_v': rep4(d_bkv[:, KVD:2 * KVD]), 'attn_b_q': rep4(d_bq),
        'attn_sinks': rep4(d_sinks), 'attn_b_o': rep4(d_bo),
        'ln_mix_g': rep4(jnp.concatenate([d_mg0, d_mg1])), 'ln_mix_b': rep4(jnp.concatenate([d_mb0, d_mb1])),
        'ln_ffn_g': rep4(jnp.concatenate([d_fg0, d_fg1])), 'ln_ffn_b': rep4(jnp.concatenate([d_fb0, d_fb1])),
    }
    n_small = sum(int(w[n].size) for n in SMALL)
    small_rows = _round_up(n_small, 2 * SUM_STEPS * 8 * 128) // 128
    small_local = jnp.concatenate([local[n] for n in SMALL], axis=1)
    small_local = jnp.pad(small_local, ((0, 0), (0, small_rows * 128 - n_small)))
    g4 = [("conv_w_pw1", halves4(dw1)), ("conv_w_pw2", halves4(dw2)),
          ("small", small_local.reshape(N_CHIPS, 2, small_rows // 2, 128))]
    got4, shared23 = _run_riders([_pair_exchange_rider(arrays(g4)), _pair_share_rider(f2 + f3)],
                                 "grad_pair_exchange_last")
    s4 = pair_sums(g4, got4)
    reduced = dict(zip([n for n, _ in g1], shared1))
    reduced.update(zip([n for n, _ in g2 + g3], shared23))
    for n in ('ffn_w_gate', 'ffn_w_up', 'ffn_w_down'):
        reduced[n] = jnp.stack([reduced[n + str(layer)].reshape(w[n].shape[1:]) for layer in range(DEPTH)])

    g_out, delta, new_m, new_v = {}, {}, {}, {}

    def adamw_matrices(names, name, riders=()):
        for n in names:
            g_out[n] = reduced[n].reshape(w[n].shape)
        two_d = [tuple(tree[n].reshape(-1, w[n].shape[-1]) for tree in (w, g_out, mom, var)) for n in names]
        results, rider_outs = _adamw(two_d, name, riders)
        for n, (dl, nm, nv) in zip(names, results):
            delta[n], new_m[n], new_v[n] = (r.reshape(w[n].shape) for r in (dl, nm, nv))
        return rider_outs

    (from_chips4,) = adamw_matrices([n for n in BIG if n not in conv_first], "adamw_attn_ffn",
                                    riders=[_chip_scatter_rider(s4)])
    f4 = chip_sums(g4, s4, from_chips4)
    (shared4,) = _run_riders([_pair_share_rider(f4)], "grad_pair_share_last")
    reduced.update(zip([n for n, _ in g4], shared4))
    adamw_matrices(conv_first, "adamw_conv")

    def pack_small(tree):
        return _pad_to(jnp.concatenate([tree[n].reshape(-1) for n in SMALL]), small_rows * 128).reshape(small_rows, 128)

    g_small = reduced['small'].reshape(small_rows, 128)
    ((dl, nm, nv),), _ = _adamw([(pack_small(w), g_small, pack_small(mom), pack_small(var))], "adamw_small")
    off = 0
    for n in SMALL:
        size, shape = int(w[n].size), w[n].shape
        for tree, flat in ((g_out, g_small), (delta, dl), (new_m, nm), (new_v, nv)):
            tree[n] = flat.reshape(-1)[off:off + size].reshape(shape)
        off += size

    return (loss, grad_x.reshape(x.shape), *[g_out[n] for n in WEIGHTS], *[delta[n] for n in WEIGHTS],
            *[new_m[n] for n in WEIGHTS], *[new_v[n] for n in WEIGHTS])
```
